```python
import jax, jax.numpy as jnp
from jax import lax
import numpy as np

D_MODEL = 1024
BATCH = 4
SEQ = 4096
DEPTH = 1

CHUNK = 64
EPS = 1e-6

GDN_HEADS = 4
GDN_DK = 128
GDN_DV = 128
GDN_CONV = 4
MLA_HEADS = 8
MLA_NOPE = 64
MLA_ROPE = 32
MLA_V = 64
MLA_Q_LORA = 384
MLA_KV_LORA = 256
ROPE_THETA = 10000.0
Q_BLOCK = 128
D_MIX = GDN_HEADS * GDN_DV + MLA_HEADS * MLA_V

N_EXPERTS = 32
TOP_K = 4
D_FF = 1024
SWIGLU_LIMIT = 7.0
SWIGLU_ALPHA = 1.702
MOE_BLOCK = 128

IN_SIZES = (GDN_HEADS * GDN_DK,
            GDN_HEADS * GDN_DK,
            GDN_HEADS * GDN_DV,
            GDN_HEADS * GDN_DV,
            GDN_HEADS,
            GDN_HEADS,
            MLA_Q_LORA,
            MLA_KV_LORA,
            MLA_ROPE)
D_IN = 4 * GDN_HEADS * GDN_DK + 2 * GDN_HEADS + MLA_Q_LORA + MLA_KV_LORA + MLA_ROPE

kernel_name = "hybrid_gdn_mla_moe_block"


def _split_points(sizes):
    pts, acc = [], 0
    for s in sizes[:-1]:
        acc += s
        pts.append(acc)
    return pts


def rms_norm(x, w):
    xf = x.astype(jnp.float32)
    y = xf * lax.rsqrt(jnp.mean(xf * xf, axis=-1, keepdims=True) + EPS)
    return (y * w.astype(jnp.float32)).astype(x.dtype)


def l2_norm(x):
    return x * lax.rsqrt(jnp.sum(x * x, axis=-1, keepdims=True) + EPS)


def causal_conv(x, w):
    k = w.shape[0]
    xp = jnp.pad(x, ((0, 0), (k - 1, 0), (0, 0)))
    return lax.conv_general_dilated(xp, w[:, None, :].astype(x.dtype), window_strides=(1,), padding='VALID',
                                    dimension_numbers=('NWC', 'WIO', 'NWC'),
                                    feature_group_count=x.shape[-1])


def chunk_gated_delta_rule(q, k, v, g, beta):
    bsz, s, h, dk = q.shape
    dv = v.shape[-1]
    n = s // CHUNK

    def to_chunks(t):
        t = t.reshape((bsz, n, CHUNK, h) + t.shape[3:])
        return jnp.moveaxis(t, (1, 3), (0, 2))

    q, k, v, g, beta = (to_chunks(t) for t in (q, k, v, g, beta))
    G = jnp.cumsum(g, axis=-1)
    idx = jnp.arange(CHUNK)
    causal = idx[:, None] >= idx[None, :]
    strict = idx[:, None] > idx[None, :]
    decay = jnp.exp(jnp.where(causal, G[..., :, None] - G[..., None, :], -jnp.inf))
    kb = k * beta[..., None]
    a_mat = jnp.where(strict, jnp.einsum('nbhcd,nbhsd->nbhcs', kb, k) * decay, 0.0)
    eye = jnp.eye(CHUNK, dtype=jnp.float32)
    t_mat = lax.linalg.triangular_solve(eye + a_mat, jnp.broadcast_to(eye, a_mat.shape),
                                        left_side=True, lower=True, unit_diagonal=True)
    u = jnp.einsum('nbhcs,nbhsd->nbhcd', t_mat, v * beta[..., None])
    w = jnp.einsum('nbhcs,nbhsd->nbhcd', t_mat, kb * jnp.exp(G)[..., None])
    qk = jnp.einsum('nbhcd,nbhsd->nbhcs', q, k) * decay
    q_dec = q * jnp.exp(G)[..., None]
    k_dec = k * jnp.exp(G[..., -1:] - G)[..., None]
    g_tot = jnp.exp(G[..., -1])

    def step(state, inp):
        u_n, w_n, qk_n, q_n, k_n, gt_n = inp
        v_new = u_n - jnp.einsum('bhcd,bhde->bhce', w_n, state)
        o_n = jnp.einsum('bhcd,bhde->bhce', q_n, state) + jnp.einsum('bhcs,bhse->bhce', qk_n, v_new)
        state = state * gt_n[..., None, None] + jnp.einsum('bhcd,bhce->bhde', k_n, v_new)
        return state, o_n

    s0 = jnp.zeros((bsz, h, dk, dv), jnp.float32)
    _, o = lax.scan(step, s0, (u, w, qk, q_dec, k_dec, g_tot))
    return jnp.moveaxis(o, (0, 2), (1, 3)).reshape(bsz, s, h, dv)


def gated_deltanet(q, k, v, z, a, b, conv_w, a_log, dt_bias, gdn_norm_w):
    bsz, s, _ = q.shape
    qkv = jax.nn.silu(causal_conv(jnp.concatenate([q, k, v], axis=-1), conv_w))
    q, k, v = jnp.split(qkv, [GDN_HEADS * GDN_DK, 2 * GDN_HEADS * GDN_DK], axis=-1)
    q = l2_norm(q.reshape(bsz, s, GDN_HEADS, GDN_DK).astype(jnp.float32)) * (GDN_DK ** -0.5)
    k = l2_norm(k.reshape(bsz, s, GDN_HEADS, GDN_DK).astype(jnp.float32))
    v = v.reshape(bsz, s, GDN_HEADS, GDN_DV).astype(jnp.float32)
    beta = jax.nn.sigmoid(b.astype(jnp.float32))
    g = -jnp.exp(a_log.astype(jnp.float32)) * jax.nn.softplus(a.astype(jnp.float32) + dt_bias.astype(jnp.float32))
    o = chunk_gated_delta_rule(q, k, v, g, beta)
    zf = z.reshape(bsz, s, GDN_HEADS, GDN_DV).astype(jnp.float32)
    o = rms_norm(o, gdn_norm_w) * jax.nn.silu(zf)
    return o.reshape(bsz, s, GDN_HEADS * GDN_DV).astype(z.dtype)


def rope_angles(positions):
    half = MLA_ROPE // 2
    inv_freq = ROPE_THETA ** (-jnp.arange(half, dtype=jnp.float32) / half)
    ang = positions.astype(jnp.float32)[..., None] * inv_freq
    return jnp.cos(ang), jnp.sin(ang)


def apply_rope(x, cos, sin):
    half = x.shape[-1] // 2
    xf = x.astype(jnp.float32)
    x1, x2 = xf[..., :half], xf[..., half:]
    return jnp.concatenate([x1 * cos - x2 * sin, x2 * cos + x1 * sin], axis=-1).astype(x.dtype)


def mla(c_q, c_kv, k_r, positions, q_norm_w, w_uq, kv_norm_w, w_ukv, mla_out_norm_w):
    bsz, s, _ = c_q.shape
    cos, sin = rope_angles(positions)
    q = (rms_norm(c_q, q_norm_w) @ w_uq).reshape(bsz, s, MLA_HEADS, MLA_NOPE + MLA_ROPE)
    q_nope, q_pe = q[..., :MLA_NOPE], q[..., MLA_NOPE:]
    q_pe = apply_rope(q_pe, cos[:, :, None, :], sin[:, :, None, :])
    kv = (rms_norm(c_kv, kv_norm_w) @ w_ukv).reshape(bsz, s, MLA_HEADS, MLA_NOPE + MLA_V)
    k_nope, v = kv[..., :MLA_NOPE], kv[..., MLA_NOPE:]
    k_pe = apply_rope(k_r, cos, sin)[:, :, None, :]
    q = jnp.concatenate([q_nope, q_pe], axis=-1).transpose(0, 2, 1, 3)
    k = jnp.concatenate([k_nope, jnp.broadcast_to(k_pe, (bsz, s, MLA_HEADS, MLA_ROPE))],
                        axis=-1).transpose(0, 2, 1, 3)
    v = v.transpose(0, 2, 1, 3)
    n_qb = s // Q_BLOCK
    q_blocks = jnp.moveaxis(q.reshape(bsz, MLA_HEADS, n_qb, Q_BLOCK, MLA_NOPE + MLA_ROPE), 2, 0)
    k_chunk = jnp.arange(s) // CHUNK
    scale = (MLA_NOPE + MLA_ROPE) ** -0.5

    def attend(args):
        qb, start = args
        sc = jnp.einsum('bhqd,bhkd->bhqk', qb, k).astype(jnp.float32) * scale
        q_chunk = (start + jnp.arange(Q_BLOCK)) // CHUNK
        sc = jnp.where(k_chunk[None, :] <= q_chunk[:, None], sc, -jnp.inf)
        p = jax.nn.softmax(sc, axis=-1)
        return jnp.einsum('bhqk,bhkd->bhqd', p.astype(v.dtype), v)

    o = lax.map(attend, (q_blocks, jnp.arange(n_qb, dtype=jnp.int32) * Q_BLOCK))
    o = jnp.moveaxis(o, 0, 2).reshape(bsz, MLA_HEADS, s, MLA_V).transpose(0, 2, 1, 3)
    return rms_norm(o.reshape(bsz, s, MLA_HEADS * MLA_V), mla_out_norm_w)


def moe_ffn(h, router_w, router_b, w1, b1, w2, b2):
    bsz, s, d = h.shape
    t = bsz * s
    xf = h.reshape(t, d)
    logits = (xf @ router_w + router_b).astype(jnp.float32)
    top_logits, top_idx = lax.top_k(logits, TOP_K)
    gates = jax.nn.softmax(top_logits, axis=-1)
    flat_e = top_idx.reshape(-1)
    order = jnp.argsort(flat_e)
    sorted_e = flat_e[order]
    tok = order // TOP_K
    counts = jnp.bincount(flat_e, length=N_EXPERTS)
    padded = (counts + MOE_BLOCK - 1) // MOE_BLOCK * MOE_BLOCK
    start = jnp.cumsum(counts) - counts
    padded_end = jnp.cumsum(padded)
    padded_start = padded_end - padded
    dest = padded_start[sorted_e] + jnp.arange(t * TOP_K) - start[sorted_e]
    n_pad = t * TOP_K + N_EXPERTS * MOE_BLOCK
    n_blocks = n_pad // MOE_BLOCK
    buf = jnp.zeros((n_pad, d), h.dtype).at[dest].set(xf[tok])
    block_expert = jnp.minimum(jnp.searchsorted(padded_end, jnp.arange(n_blocks) * MOE_BLOCK, side='right'),
                               N_EXPERTS - 1)

    def expert_block(args):
        xb, e = args
        hp = xb @ w1[e] + b1[e]
        x_glu = jnp.minimum(hp[:, ::2], SWIGLU_LIMIT)
        x_lin = jnp.clip(hp[:, 1::2], -SWIGLU_LIMIT, SWIGLU_LIMIT)
        act = x_glu * jax.nn.sigmoid(SWIGLU_ALPHA * x_glu) * (x_lin + 1)
        return act @ w2[e] + b2[e]

    y_blocks = lax.map(expert_block, (buf.reshape(n_blocks, MOE_BLOCK, d), block_expert))
    y_assign = y_blocks.reshape(n_pad, d)[dest] * gates.reshape(-1)[order][:, None].astype(h.dtype)
    y = jnp.zeros((t, d), h.dtype).at[tok].add(y_assign)
    return y.reshape(bsz, s, d)


def setup_inputs(seed: int = 0) -> dict:
    key = jax.random.key(seed)
    ks = jax.random.split(key, 24)
    f32 = jnp.float32

    def nrm(k, shape, scale):
        return jax.random.normal(k, shape, f32) * scale

    def gain(k, shape):
        return 1.0 + 0.02 * jax.random.normal(k, shape, f32)

    offset = jax.random.randint(ks[1], (BATCH,), 0, 4096, dtype=jnp.int32)
    positions = offset[:, None] + jnp.arange(SEQ, dtype=jnp.int32)[None, :]
    return {
        "x": nrm(ks[0], (BATCH, SEQ, D_MODEL), 1.0),
        "positions": positions,
        "norm_mix_w": gain(ks[2], (DEPTH, D_MODEL)),
        "w_in": nrm(ks[3], (DEPTH, D_MODEL, D_IN), D_MODEL ** -0.5),
        "conv_w": nrm(ks[4], (DEPTH, GDN_CONV, 2 * GDN_HEADS * GDN_DK + GDN_HEADS * GDN_DV), 0.5),
        "a_log": jnp.log(jax.random.uniform(ks[5], (DEPTH, GDN_HEADS), f32, 1.0, 16.0)),
        "dt_bias": nrm(ks[6], (DEPTH, GDN_HEADS), 0.1),
        "gdn_norm_w": gain(ks[7], (DEPTH, GDN_DV)),
        "q_norm_w": gain(ks[8], (DEPTH, MLA_Q_LORA)),
        "w_uq": nrm(ks[9], (DEPTH, MLA_Q_LORA, MLA_HEADS * (MLA_NOPE + MLA_ROPE)), MLA_Q_LORA ** -0.5),
        "kv_norm_w": gain(ks[10], (DEPTH, MLA_KV_LORA)),
        "w_ukv": nrm(ks[11], (DEPTH, MLA_KV_LORA, MLA_HEADS * (MLA_NOPE + MLA_V)), MLA_KV_LORA ** -0.5),
        "mla_out_norm_w": gain(ks[12], (DEPTH, MLA_HEADS * MLA_V)),
        "w_out": nrm(ks[13], (DEPTH, D_MIX, D_MODEL), D_MIX ** -0.5),
        "norm_ffn_w": gain(ks[14], (DEPTH, D_MODEL)),
        "router_w": nrm(ks[15], (DEPTH, D_MODEL, N_EXPERTS), D_MODEL ** -0.5),
        "router_b": nrm(ks[16], (DEPTH, N_EXPERTS), 0.01),
        "w1": nrm(ks[17], (DEPTH, N_EXPERTS, D_MODEL, 2 * D_FF), D_MODEL ** -0.5),
        "b1": nrm(ks[18], (DEPTH, N_EXPERTS, 2 * D_FF), 0.01),
        "w2": nrm(ks[19], (DEPTH, N_EXPERTS, D_FF, D_MODEL), D_FF ** -0.5),
        "b2": nrm(ks[20], (DEPTH, N_EXPERTS, D_MODEL), 0.01),
        "norm_final_w": gain(ks[21], (D_MODEL,)),
    }


def reference(x, positions, norm_mix_w, w_in, conv_w, a_log, dt_bias, gdn_norm_w, q_norm_w, w_uq,
              kv_norm_w, w_ukv, mla_out_norm_w, w_out, norm_ffn_w, router_w, router_b, w1, b1, w2, b2,
              norm_final_w):
    split_pts = _split_points(IN_SIZES)
    for l in range(DEPTH):
        h = rms_norm(x, norm_mix_w[l])
        proj = h @ w_in[l]
        q_a, k_a, v_a, z_a, a_a, b_a, c_q, c_kv, k_r = jnp.split(proj, split_pts, axis=-1)
        o_a = gated_deltanet(q_a, k_a, v_a, z_a, a_a, b_a, conv_w[l], a_log[l], dt_bias[l], gdn_norm_w[l])
        o_b = mla(c_q, c_kv, k_r, positions, q_norm_w[l], w_uq[l], kv_norm_w[l], w_ukv[l], mla_out_norm_w[l])
        x = x + jnp.concatenate([o_a, o_b], axis=-1) @ w_out[l]
        x = x + moe_ffn(rms_norm(x, norm_ffn_w[l]), router_w[l], router_b[l], w1[l], b1[l], w2[l], b2[l])
    return rms_norm(x, norm_final_w)
```

```python
import functools

import jax
import jax.numpy as jnp
from jax import lax
from jax.experimental import pallas as pl
from jax.experimental.pallas import tpu as pltpu

F32 = jnp.float32
BF16 = jnp.bfloat16
HIGHEST = lax.Precision.HIGHEST

LANES = 128
EPS = 1e-6
CHUNK = 64

GDN_HEADS = 4
GDN_D = 128
GDN_CONV = 4
MLA_HEADS = 8
MLA_NOPE = 64
MLA_ROPE = 32
MLA_V = 64
MLA_Q_LORA = 384
MLA_KV_LORA = 256
ROPE_THETA = 10000.0
N_EXPERTS = 32
TOP_K = 4
SWIGLU_LIMIT = 7.0
SWIGLU_ALPHA = 1.702

NEG_BIG = -1e30
VMEM_LIMIT = 52 * 1024 * 1024


def _sigmoid(x):
    return 1.0 / (1.0 + jnp.exp(-x))


def _softplus(x):
    return jnp.maximum(x, 0.0) + jnp.log(1.0 + jnp.exp(-jnp.abs(x)))


def _dot(a, b, precision=None):
    return jnp.dot(a, b, preferred_element_type=F32, precision=precision)


def _dot_nt(a, b, precision=None):
    return lax.dot_general(a, b, (((1,), (1,)), ((), ())), preferred_element_type=F32,
                           precision=precision)


def _dot_tn(a, b):
    return lax.dot_general(a, b, (((0,), (0,)), ((), ())), preferred_element_type=F32)


C_QKV = 3 * GDN_HEADS * GDN_D
C_Z = GDN_HEADS * GDN_D
C_IN = C_QKV + C_Z + MLA_Q_LORA + MLA_KV_LORA + 3 * LANES


def _inproj_kernel(x_ref, nw_ref, w_ref, qkv_ref, z_ref, cq_ref, ckv_ref, kr_ref, ab_ref):
    x = x_ref[...]
    var = jnp.mean(x * x, axis=-1, keepdims=True)
    h = (x * lax.rsqrt(var + EPS) * nw_ref[...]).astype(BF16)
    p = _dot(h, w_ref[...])
    o = 0
    for ref in (qkv_ref, z_ref, cq_ref, ckv_ref, kr_ref, ab_ref):
        n = ref.shape[-1]
        ref[...] = p[:, o:o + n]
        o += n


def _inproj(x2, norm_w, w_all, tm):
    t, d = x2.shape
    widths = (C_QKV, C_Z, MLA_Q_LORA, MLA_KV_LORA, 2 * LANES, LANES)
    return pl.pallas_call(
        _inproj_kernel,
        grid=(t // tm,),
        in_specs=[pl.BlockSpec((tm, d), lambda i: (i, 0)),
                  pl.BlockSpec((1, d), lambda i: (0, 0)),
                  pl.BlockSpec((d, C_IN), lambda i: (0, 0))],
        out_specs=[pl.BlockSpec((tm, n), lambda i: (i, 0)) for n in widths],
        out_shape=[jax.ShapeDtypeStruct((t, n), F32) for n in widths],
        compiler_params=pltpu.CompilerParams(dimension_semantics=("parallel",),
                                             vmem_limit_bytes=VMEM_LIMIT),
        name="inproj",
    )(x2, norm_w, w_all)


def _gdn_kernel(qkv_ref, z_ref, ab_ref, cw_ref, alog_ref, dtb_ref, nw_ref, o_ref, ext_ref, s_ref):
    c = CHUNK

    @pl.when(pl.program_id(1) == 0)
    def _():
        ext_ref[0:8, :] = jnp.zeros((8, C_QKV), F32)
        s_ref[...] = jnp.zeros_like(s_ref)

    x = qkv_ref[...]
    ext_ref[8:8 + c, :] = x
    cw = cw_ref[...]
    xc = (cw[3:4] * x + cw[2:3] * ext_ref[7:7 + c, :] + cw[1:2] * ext_ref[6:6 + c, :]
          + cw[0:1] * ext_ref[5:5 + c, :])
    ext_ref[0:8, :] = x[c - 8:c, :]
    xc = xc * _sigmoid(xc)

    ab = ab_ref[...]
    g_all = -jnp.exp(alog_ref[...]) * _softplus(ab + dtb_ref[...])
    beta_all = _sigmoid(ab)
    ri = lax.broadcasted_iota(jnp.int32, (c, c), 0)
    ci = lax.broadcasted_iota(jnp.int32, (c, c), 1)
    causal = ri >= ci
    strict = ri > ci
    eye = (ri == ci).astype(F32)
    g_cum = _dot(causal.astype(F32), g_all, HIGHEST)
    e8 = (lax.broadcasted_iota(jnp.int32, (8, LANES), 0)
          == lax.broadcasted_iota(jnp.int32, (8, LANES), 1)).astype(F32)
    g_row_all = _dot_nt(e8, g_cum, HIGHEST)
    nw = nw_ref[...]

    for h in range(GDN_HEADS):
        q = xc[:, h * GDN_D:(h + 1) * GDN_D]
        k = xc[:, (GDN_HEADS + h) * GDN_D:(GDN_HEADS + h + 1) * GDN_D]
        v = xc[:, (2 * GDN_HEADS + h) * GDN_D:(2 * GDN_HEADS + h + 1) * GDN_D]
        q = q * lax.rsqrt(jnp.sum(q * q, axis=-1, keepdims=True) + EPS) * (GDN_D ** -0.5)
        k = k * lax.rsqrt(jnp.sum(k * k, axis=-1, keepdims=True) + EPS)
        gc = g_cum[:, h:h + 1]
        gr = g_row_all[h:h + 1, :]
        beta = beta_all[:, GDN_HEADS + h:GDN_HEADS + h + 1]
        decay = jnp.exp(jnp.where(causal, gc - gr, -jnp.inf))
        kb = k * beta
        k16 = k.astype(BF16)
        a_mat = jnp.where(strict, _dot_nt(kb.astype(BF16), k16) * decay, 0.0)
        t_mat = eye - a_mat
        pw = a_mat
        for _ in range(5):
            pw = _dot(pw, pw, HIGHEST)
            t_mat = t_mat + _dot(t_mat, pw, HIGHEST)
        e_g = jnp.exp(gc)
        rhs = jnp.concatenate([v * beta, kb * e_g], axis=1).astype(BF16)
        uw = _dot(t_mat.astype(BF16), rhs)
        u = uw[:, :GDN_D]
        w = uw[:, GDN_D:]
        qk = _dot_nt(q.astype(BF16), k16) * decay
        s = s_ref[h]
        s16 = s.astype(BF16)
        v_new = u - _dot(w.astype(BF16), s16)
        v16 = v_new.astype(BF16)
        o = _dot((q * e_g).astype(BF16), s16) + _dot(qk.astype(BF16), v16)
        g_last = gc[c - 1:c, :]
        k_dec = (k * jnp.exp(g_last - gc)).astype(BF16)
        s_ref[h] = s * jnp.exp(g_last) + _dot_tn(k_dec, v16)
        on = o * lax.rsqrt(jnp.mean(o * o, axis=-1, keepdims=True) + EPS) * nw
        zh = z_ref[:, h * GDN_D:(h + 1) * GDN_D]
        o_ref[:, h * GDN_D:(h + 1) * GDN_D] = on * (zh * _sigmoid(zh))


def _gdn(qkv, z, ab, conv_w8, alog_row, dtb_row, norm_w, bsz, seq):
    n = seq // CHUNK
    row = lambda b, i: (b * n + i, 0)
    const = lambda b, i: (0, 0)
    return pl.pallas_call(
        _gdn_kernel,
        grid=(bsz, n),
        in_specs=[pl.BlockSpec((CHUNK, C_QKV), row),
                  pl.BlockSpec((CHUNK, C_Z), row),
                  pl.BlockSpec((CHUNK, LANES), row),
                  pl.BlockSpec((8, C_QKV), const),
                  pl.BlockSpec((1, LANES), const),
                  pl.BlockSpec((1, LANES), const),
                  pl.BlockSpec((1, GDN_D), const)],
        out_specs=pl.BlockSpec((CHUNK, C_Z), row),
        out_shape=jax.ShapeDtypeStruct((bsz * seq, C_Z), F32),
        scratch_shapes=[pltpu.VMEM((8 + CHUNK, C_QKV), F32),
                        pltpu.VMEM((GDN_HEADS, GDN_D, GDN_D), F32)],
        compiler_params=pltpu.CompilerParams(dimension_semantics=("arbitrary", "arbitrary"),
                                             vmem_limit_bytes=VMEM_LIMIT),
        name="gdn",
    )(qkv, z, ab, conv_w8, alog_row, dtb_row, norm_w)


D_QK_PAD = LANES
C_HEADS = MLA_HEADS * D_QK_PAD


def _mla_proj_kernel(cq_ref, ckv_ref, kr_ref, pos_ref, qnw_ref, kvnw_ref, wq_ref, wqr_ref, wk_ref,
                     wv_ref, freq_ref, q_ref, k_ref, v_ref):
    cq = cq_ref[...]
    cqn = (cq * lax.rsqrt(jnp.mean(cq * cq, axis=-1, keepdims=True) + EPS) * qnw_ref[...]).astype(BF16)
    ckv = ckv_ref[...]
    ckvn = (ckv * lax.rsqrt(jnp.mean(ckv * ckv, axis=-1, keepdims=True) + EPS)
            * kvnw_ref[...]).astype(BF16)
    ang = pos_ref[...] * freq_ref[...]
    cs = jnp.cos(ang)
    sn = jnp.sin(ang)
    scale = (MLA_NOPE + MLA_ROPE) ** -0.5
    qa = _dot(cqn, wq_ref[...])
    qb = _dot(cqn, wqr_ref[...])
    kn = _dot(ckvn, wk_ref[...])
    kr = kr_ref[...]
    kpe = kr[:, :LANES] * cs + kr[:, LANES:] * sn
    for h in range(MLA_HEADS):
        sl = slice(h * LANES, (h + 1) * LANES)
        q_ref[:, sl] = ((qa[:, sl] * cs + qb[:, sl] * sn) * scale).astype(BF16)
        k_ref[:, sl] = (kn[:, sl] + kpe).astype(BF16)
    v_ref[...] = _dot(ckvn, wv_ref[...]).astype(BF16)


def _mla_proj(cq, ckv, kr2, pos, qnw, kvnw, wq, wqr, wk, wv, freq, tm):
    t = cq.shape[0]
    row = lambda i: (i, 0)
    const = lambda i: (0, 0)
    return pl.pallas_call(
        _mla_proj_kernel,
        grid=(t // tm,),
        in_specs=[pl.BlockSpec((tm, MLA_Q_LORA), row),
                  pl.BlockSpec((tm, MLA_KV_LORA), row),
                  pl.BlockSpec((tm, 2 * LANES), row),
                  pl.BlockSpec((tm, 1), row),
                  pl.BlockSpec((1, MLA_Q_LORA), const),
                  pl.BlockSpec((1, MLA_KV_LORA), const),
                  pl.BlockSpec((MLA_Q_LORA, C_HEADS), const),
                  pl.BlockSpec((MLA_Q_LORA, C_HEADS), const),
                  pl.BlockSpec((MLA_KV_LORA, C_HEADS), const),
                  pl.BlockSpec((MLA_KV_LORA, C_HEADS), const),
                  pl.BlockSpec((1, LANES), const)],
        out_specs=[pl.BlockSpec((tm, C_HEADS), row)] * 3,
        out_shape=[jax.ShapeDtypeStruct((t, C_HEADS), BF16)] * 3,
        compiler_params=pltpu.CompilerParams(dimension_semantics=("parallel",),
                                             vmem_limit_bytes=VMEM_LIMIT),
        name="mla_proj",
    )(cq, ckv, kr2, pos, qnw, kvnw, wq, wqr, wk, wv, freq)


def _attn_kernel(q_ref, k_ref, v_ref, nw_ref, o_ref, *, tq):
    i = pl.program_id(1)
    rq = lax.broadcasted_iota(jnp.int32, (tq, tq), 0) // CHUNK
    ck = lax.broadcasted_iota(jnp.int32, (tq, tq), 1) // CHUNK
    diag_mask = ck <= rq
    outs = []
    for h in range(MLA_HEADS):
        sl = slice(h * LANES, (h + 1) * LANES)
        qh = q_ref[:, sl]

        def tile(j, carry, masked):
            m, l, acc = carry
            rows = pl.ds(pl.multiple_of(j * tq, tq), tq)
            s = _dot_nt(qh, k_ref[rows, sl])
            if masked:
                s = jnp.where(diag_mask, s, NEG_BIG)
            m_new = jnp.maximum(m, jnp.max(s, axis=-1, keepdims=True))
            p = jnp.exp(s - m_new)
            alpha = jnp.exp(m - m_new)
            l = alpha * l + jnp.sum(p, axis=-1, keepdims=True)
            acc = alpha * acc + _dot(p.astype(BF16), v_ref[rows, sl])
            return m_new, l, acc

        init = (jnp.full((tq, 1), NEG_BIG, F32), jnp.zeros((tq, 1), F32), jnp.zeros((tq, LANES), F32))
        carry = lax.fori_loop(0, i, lambda j, c: tile(j, c, False), init)
        m, l, acc = tile(i, carry, True)
        outs.append(acc[:, :MLA_V] / l)
    o = jnp.concatenate(outs, axis=1)
    o_ref[...] = o * lax.rsqrt(jnp.mean(o * o, axis=-1, keepdims=True) + EPS) * nw_ref[...]


def _attention(q, k, v, norm_w, bsz, seq, tq):
    nq = seq // tq
    return pl.pallas_call(
        functools.partial(_attn_kernel, tq=tq),
        grid=(bsz, nq),
        in_specs=[pl.BlockSpec((tq, C_HEADS), lambda b, i: (b * nq + i, 0)),
                  pl.BlockSpec((seq, C_HEADS), lambda b, i: (b, 0)),
                  pl.BlockSpec((seq, C_HEADS), lambda b, i: (b, 0)),
                  pl.BlockSpec((1, MLA_HEADS * MLA_V), lambda b, i: (0, 0))],
        out_specs=pl.BlockSpec((tq, MLA_HEADS * MLA_V), lambda b, i: (b * nq + i, 0)),
        out_shape=jax.ShapeDtypeStruct((bsz * seq, MLA_HEADS * MLA_V), F32),
        compiler_params=pltpu.CompilerParams(dimension_semantics=("parallel", "parallel"),
                                             vmem_limit_bytes=VMEM_LIMIT),
        name="attention",
    )(q, k, v, norm_w)


def _router_kernel(x_ref, oa_ref, ob_ref, wo_ref, nw_ref, rw_ref, rb_ref,
                   x1_ref, h_ref, eidx_ref, gate_ref, rank_ref, cnt_ref, run_ref, *, tm):
    @pl.when(pl.program_id(0) == 0)
    def _():
        run_ref[...] = jnp.zeros_like(run_ref)

    half = oa_ref.shape[-1]
    y = _dot(oa_ref[...].astype(BF16), wo_ref[0:half, :]) + _dot(ob_ref[...].astype(BF16),
                                                                 wo_ref[half:, :])
    x1 = x_ref[...] + y
    x1_ref[...] = x1
    hn = x1 * lax.rsqrt(jnp.mean(x1 * x1, axis=-1, keepdims=True) + EPS) * nw_ref[...]
    h_ref[...] = hn.astype(BF16)
    lane = lax.broadcasted_iota(jnp.int32, (tm, LANES), 1)
    logits = _dot(hn, rw_ref[...], HIGHEST) + rb_ref[...]
    lg = jnp.where(lane < N_EXPERTS, logits, -jnp.inf)
    sels, tops = [], []
    eidx = jnp.zeros((tm, LANES), jnp.int32)
    for kk in range(TOP_K):
        m = jnp.max(lg, axis=-1, keepdims=True)
        idx = jnp.min(jnp.where(lg == m, lane, LANES), axis=-1, keepdims=True)
        sel = lane == idx
        lg = jnp.where(sel, -jnp.inf, lg)
        sels.append(sel)
        tops.append(m)
        eidx = jnp.where(lane == kk, idx, eidx)
    exps = [jnp.exp(tv - tops[0]) for tv in tops]
    den = exps[0] + exps[1] + exps[2] + exps[3]
    gates = jnp.zeros((tm, LANES), F32)
    multi = jnp.zeros((tm, LANES), F32)
    for kk in range(TOP_K):
        gates = jnp.where(lane == kk, exps[kk] / den, gates)
        multi = multi + sels[kk].astype(F32)
    ri = lax.broadcasted_iota(jnp.int32, (tm, tm), 0)
    ci = lax.broadcasted_iota(jnp.int32, (tm, tm), 1)
    before = _dot((ri > ci).astype(BF16), multi.astype(BF16)) + run_ref[...]
    rank = jnp.zeros((tm, LANES), jnp.int32)
    for kk in range(TOP_K):
        r = jnp.sum(jnp.where(sels[kk], before, 0.0), axis=-1, keepdims=True)
        rank = jnp.where(lane == kk, r.astype(jnp.int32), rank)
    run = run_ref[...] + jnp.sum(multi, axis=0, keepdims=True)
    run_ref[...] = run
    cnt_ref[...] = run
    eidx_ref[...] = eidx
    gate_ref[...] = gates
    rank_ref[...] = rank


def _router(x2, oa, ob, w_out, norm_w, rw, rb, tm):
    t, d = x2.shape
    half = oa.shape[-1]
    row = lambda i: (i, 0)
    const = lambda i: (0, 0)
    return pl.pallas_call(
        functools.partial(_router_kernel, tm=tm),
        grid=(t // tm,),
        in_specs=[pl.BlockSpec((tm, d), row),
                  pl.BlockSpec((tm, half), row),
                  pl.BlockSpec((tm, half), row),
                  pl.BlockSpec((2 * half, d), const),
                  pl.BlockSpec((1, d), const),
                  pl.BlockSpec((d, LANES), const),
                  pl.BlockSpec((1, LANES), const)],
        out_specs=[pl.BlockSpec((tm, d), row), pl.BlockSpec((tm, d), row),
                   pl.BlockSpec((tm, LANES), row), pl.BlockSpec((tm, LANES), row),
                   pl.BlockSpec((tm, LANES), row), pl.BlockSpec((1, LANES), const)],
        out_shape=[jax.ShapeDtypeStruct((t, d), F32), jax.ShapeDtypeStruct((t, d), BF16),
                   jax.ShapeDtypeStruct((t, LANES), jnp.int32), jax.ShapeDtypeStruct((t, LANES), F32),
                   jax.ShapeDtypeStruct((t, LANES), jnp.int32), jax.ShapeDtypeStruct((1, LANES), F32)],
        scratch_shapes=[pltpu.VMEM((1, LANES), F32)],
        compiler_params=pltpu.CompilerParams(dimension_semantics=("arbitrary",),
                                             vmem_limit_bytes=VMEM_LIMIT),
        name="router",
    )(x2, oa, ob, w_out, norm_w, rw, rb)


def _dispatch_kernel(dest_ref, h_ref, buf_in_ref, buf_ref, sem, *, tm):
    del buf_in_ref

    def issue(t, carry):
        for kk in range(TOP_K):
            d = dest_ref[0, 0, t * TOP_K + kk]
            pltpu.make_async_copy(h_ref.at[t], buf_ref.at[d], sem).start()
        return carry

    lax.fori_loop(0, tm, issue, 0)
    for kk in range(TOP_K):
        pltpu.make_async_copy(h_ref, buf_ref.at[pl.ds(0, tm)], sem).wait()


def _dispatch(dest, h3, n_pad, tm):
    t, s, _ = h3.shape
    buf0 = jnp.zeros((n_pad, s, LANES), h3.dtype)
    return pl.pallas_call(
        functools.partial(_dispatch_kernel, tm=tm),
        grid=(t // tm,),
        in_specs=[pl.BlockSpec((1, 1, tm * TOP_K), lambda i: (i, 0, 0), memory_space=pltpu.SMEM),
                  pl.BlockSpec((tm, s, LANES), lambda i: (i, 0, 0)),
                  pl.BlockSpec(memory_space=pl.ANY)],
        out_specs=pl.BlockSpec(memory_space=pl.ANY),
        out_shape=jax.ShapeDtypeStruct((n_pad, s, LANES), h3.dtype),
        scratch_shapes=[pltpu.SemaphoreType.DMA(())],
        input_output_aliases={2: 0},
        compiler_params=pltpu.CompilerParams(dimension_semantics=("arbitrary",)),
        name="dispatch",
    )(dest.reshape(t // tm, 1, tm * TOP_K), h3, buf0)


def _expert_kernel(be_ref, bv_ref, x_ref, w1g_ref, w1l_ref, b1g_ref, b1l_ref, w2_ref, b2_ref, y_ref):
    del be_ref
    valid = bv_ref[pl.program_id(0)] == 1

    @pl.when(jnp.logical_not(valid))
    def _():
        y_ref[...] = jnp.zeros_like(y_ref)

    @pl.when(valid)
    def _():
        x = x_ref[...]
        g = _dot(x, w1g_ref[0]) + b1g_ref[0]
        lin = _dot(x, w1l_ref[0]) + b1l_ref[0]
        g = jnp.minimum(g, SWIGLU_LIMIT)
        lin = jnp.clip(lin, -SWIGLU_LIMIT, SWIGLU_LIMIT)
        act = g * _sigmoid(SWIGLU_ALPHA * g) * (lin + 1.0)
        y_ref[...] = (_dot(act.astype(BF16), w2_ref[0]) + b2_ref[0]).astype(y_ref.dtype)


def _experts(block_expert, block_valid, xs, w1g, w1l, b1g, b1l, w2, b2, bm):
    n_pad, d = xs.shape
    dff = w1g.shape[-1]
    wmap = lambda i, be, bv: (be[i], 0, 0)
    grid_spec = pltpu.PrefetchScalarGridSpec(
        num_scalar_prefetch=2,
        grid=(n_pad // bm,),
        in_specs=[pl.BlockSpec((bm, d), lambda i, be, bv: (i, 0)),
                  pl.BlockSpec((1, d, dff), wmap),
                  pl.BlockSpec((1, d, dff), wmap),
                  pl.BlockSpec((1, 1, dff), wmap),
                  pl.BlockSpec((1, 1, dff), wmap),
                  pl.BlockSpec((1, dff, d), wmap),
                  pl.BlockSpec((1, 1, d), wmap)],
        out_specs=pl.BlockSpec((bm, d), lambda i, be, bv: (i, 0)),
    )
    return pl.pallas_call(
        _expert_kernel,
        grid_spec=grid_spec,
        out_shape=jax.ShapeDtypeStruct((n_pad, d), BF16),
        compiler_params=pltpu.CompilerParams(dimension_semantics=("arbitrary",),
                                             vmem_limit_bytes=VMEM_LIMIT),
        name="experts",
    )(block_expert, block_valid, xs, w1g, w1l, b1g, b1l, w2, b2)


def _combine_kernel(dest_ref, gate_ref, y_ref, o_ref, gbuf, sem, *, tm):
    def issue(t, carry):
        for kk in range(TOP_K):
            d = dest_ref[0, 0, t * TOP_K + kk]
            pltpu.make_async_copy(y_ref.at[d], gbuf.at[kk * tm + t], sem).start()
        return carry

    lax.fori_loop(0, tm, issue, 0)
    for kk in range(TOP_K):
        pltpu.make_async_copy(y_ref.at[pl.ds(0, tm)], gbuf.at[pl.ds(kk * tm, tm)], sem).wait()

    def mix(t, carry):
        acc = gate_ref[0, 0, t * TOP_K] * gbuf[t].astype(F32)
        for kk in range(1, TOP_K):
            acc = acc + gate_ref[0, 0, t * TOP_K + kk] * gbuf[kk * tm + t].astype(F32)
        o_ref[t] = acc
        return carry

    lax.fori_loop(0, tm, mix, 0)


def _combine(dest, gates, y3, t, tm):
    n_pad, s, _ = y3.shape
    smem = lambda: pl.BlockSpec((1, 1, tm * TOP_K), lambda i: (i, 0, 0), memory_space=pltpu.SMEM)
    return pl.pallas_call(
        functools.partial(_combine_kernel, tm=tm),
        grid=(t // tm,),
        in_specs=[smem(), smem(), pl.BlockSpec(memory_space=pl.ANY)],
        out_specs=pl.BlockSpec((tm, s, LANES), lambda i: (i, 0, 0)),
        out_shape=jax.ShapeDtypeStruct((t, s, LANES), F32),
        scratch_shapes=[pltpu.VMEM((TOP_K * tm, s, LANES), y3.dtype), pltpu.SemaphoreType.DMA(())],
        compiler_params=pltpu.CompilerParams(dimension_semantics=("arbitrary",)),
        name="combine",
    )(dest.reshape(t // tm, 1, tm * TOP_K), gates.reshape(t // tm, 1, tm * TOP_K), y3)


def _final_kernel(x1_ref, y_ref, nw_ref, o_ref):
    x = x1_ref[...] + y_ref[...]
    o_ref[...] = x * lax.rsqrt(jnp.mean(x * x, axis=-1, keepdims=True) + EPS) * nw_ref[...]


def _final(x1, y, norm_w, tm):
    t, d = x1.shape
    row = lambda i: (i, 0)
    return pl.pallas_call(
        _final_kernel,
        grid=(t // tm,),
        in_specs=[pl.BlockSpec((tm, d), row), pl.BlockSpec((tm, d), row),
                  pl.BlockSpec((1, d), lambda i: (0, 0))],
        out_specs=pl.BlockSpec((tm, d), row),
        out_shape=jax.ShapeDtypeStruct((t, d), F32),
        compiler_params=pltpu.CompilerParams(dimension_semantics=("parallel",)),
        name="final_norm",
    )(x1, y, norm_w)


def _pad_cols(w, n):
    return jnp.pad(w, ((0, 0), (0, n - w.shape[1])))


def _rot_half(w):
    half = w.shape[-1] // 2
    return jnp.concatenate([-w[..., half:], w[..., :half]], axis=-1)


def _prep_inproj_weight(w_in):
    d = w_in.shape[0]
    hd = GDN_HEADS * GDN_D
    o = 0
    parts = {}
    for name, n in (("qkv", 3 * hd), ("z", hd), ("a", GDN_HEADS), ("b", GDN_HEADS),
                    ("cq", MLA_Q_LORA), ("ckv", MLA_KV_LORA), ("kr", MLA_ROPE)):
        parts[name] = w_in[:, o:o + n]
        o += n
    zeros = lambda n: jnp.zeros((d, n), w_in.dtype)
    kr_tile = jnp.concatenate([zeros(MLA_NOPE), parts["kr"], zeros(LANES - MLA_NOPE - MLA_ROPE)], axis=1)
    krr_tile = jnp.concatenate([zeros(MLA_NOPE), _rot_half(parts["kr"]),
                                zeros(LANES - MLA_NOPE - MLA_ROPE)], axis=1)
    ab_tile = jnp.concatenate([parts["a"], parts["b"], zeros(LANES - 2 * GDN_HEADS)], axis=1)
    return jnp.concatenate([parts["qkv"], parts["z"], parts["cq"], parts["ckv"], kr_tile, krr_tile,
                            ab_tile], axis=1).astype(BF16)


def _prep_mla_weights(w_uq, w_ukv):
    dq = MLA_NOPE + MLA_ROPE
    wq = w_uq.reshape(MLA_Q_LORA, MLA_HEADS, dq)
    wq_rot = jnp.concatenate([jnp.zeros_like(wq[..., :MLA_NOPE]), _rot_half(wq[..., MLA_NOPE:])], axis=-1)
    pad = lambda w: jnp.pad(w, ((0, 0), (0, 0), (0, LANES - w.shape[-1]))).reshape(w.shape[0], C_HEADS)
    wkv = w_ukv.reshape(MLA_KV_LORA, MLA_HEADS, MLA_NOPE + MLA_V)
    return (pad(wq).astype(BF16), pad(wq_rot).astype(BF16), pad(wkv[..., :MLA_NOPE]).astype(BF16),
            pad(wkv[..., MLA_NOPE:]).astype(BF16))


def _rope_freq_row():
    half = MLA_ROPE // 2
    inv_freq = ROPE_THETA ** (-jnp.arange(half, dtype=F32) / half)
    row = jnp.zeros((LANES,), F32).at[MLA_NOPE:MLA_NOPE + MLA_ROPE].set(jnp.concatenate([inv_freq, inv_freq]))
    return row.reshape(1, LANES)


def _lane_row(v):
    return jnp.pad(v.astype(F32), (0, LANES - v.shape[0])).reshape(1, LANES)


def kernel(x, positions, norm_mix_w, w_in, conv_w, a_log, dt_bias, gdn_norm_w, q_norm_w, w_uq, kv_norm_w,
           w_ukv, mla_out_norm_w, w_out, norm_ffn_w, router_w, router_b, w1, b1, w2, b2, norm_final_w):
    bsz, seq, d = x.shape
    t = bsz * seq
    depth = w_in.shape[0]
    tm = min(512, t)
    tr = min(256, t)
    bm = 256
    tq = min(256, seq)
    x2 = x.reshape(t, d)
    pos = positions.reshape(t, 1).astype(F32)
    freq = _rope_freq_row()
    for l in range(depth):
        qkv, z, cq, ckv, kr2, ab = _inproj(x2, norm_mix_w[l].reshape(1, d), _prep_inproj_weight(w_in[l]), tm)
        conv_w8 = jnp.pad(conv_w[l], ((0, 8 - GDN_CONV), (0, 0)))
        o_a = _gdn(qkv, z, ab, conv_w8, _lane_row(a_log[l]), _lane_row(dt_bias[l]),
                   gdn_norm_w[l].reshape(1, GDN_D), bsz, seq)
        wq, wqr, wk, wv = _prep_mla_weights(w_uq[l], w_ukv[l])
        q, k, v = _mla_proj(cq, ckv, kr2, pos, q_norm_w[l].reshape(1, -1), kv_norm_w[l].reshape(1, -1),
                            wq, wqr, wk, wv, freq, tm)
        o_b = _attention(q, k, v, mla_out_norm_w[l].reshape(1, -1), bsz, seq, tq)
        rw = _pad_cols(router_w[l], LANES)
        rb = _lane_row(router_b[l])
        x1, hn, eidx, gates, rank, cnt = _router(x2, o_a, o_b, w_out[l].astype(BF16),
                                                 norm_ffn_w[l].reshape(1, d), rw, rb, tr)
        counts = cnt[0, :N_EXPERTS].astype(jnp.int32)
        padded = (counts + bm - 1) // bm * bm
        pend = jnp.cumsum(padded)
        pstart = pend - padded
        dest = pstart[eidx[:, :TOP_K]] + rank[:, :TOP_K]
        n_pad = t * TOP_K + N_EXPERTS * bm
        blk = jnp.arange(n_pad // bm, dtype=jnp.int32) * bm
        block_expert = jnp.minimum(jnp.searchsorted(pend, blk, side="right"), N_EXPERTS - 1).astype(jnp.int32)
        block_valid = (blk < pend[-1]).astype(jnp.int32)
        buf = _dispatch(dest, hn.reshape(t, d // LANES, LANES), n_pad, tr)
        w1l_ = w1[l]
        yb = _experts(block_expert, block_valid, buf.reshape(n_pad, d),
                      w1l_[:, :, 0::2].astype(BF16), w1l_[:, :, 1::2].astype(BF16),
                      b1[l][:, None, 0::2], b1[l][:, None, 1::2], w2[l].astype(BF16), b2[l][:, None, :], bm)
        y3 = _combine(dest, gates[:, :TOP_K], yb.reshape(n_pad, d // LANES, LANES), t, tr)
        last = norm_final_w if l == depth - 1 else None
        if last is None:
            x2 = x1 + y3.reshape(t, d)
        else:
            x2 = _final(x1, y3.reshape(t, d), last.reshape(1, d), tm)
    return x2.reshape(bsz, seq, d)
```

```python
import functools
import math

import jax
import jax.numpy as jnp
from jax import lax
from jax.experimental import pallas as pl
from jax.experimental.pallas import tpu as pltpu

F32 = jnp.float32
BF16 = jnp.bfloat16
HIGHEST = lax.Precision.HIGHEST

LANES = 128
MXU_COLS = 256
EPS = 1e-6
CHUNK = 64

GDN_HEADS = 4
GDN_D = 128
GDN_CONV = 4
MLA_HEADS = 8
MLA_NOPE = 64
MLA_ROPE = 32
MLA_V = 64
MLA_Q_LORA = 384
MLA_KV_LORA = 256
ROPE_THETA = 10000.0
N_EXPERTS = 32
TOP_K = 4
SWIGLU_LIMIT = 7.0
SWIGLU_ALPHA = 1.702

NEG_BIG = -1e30
VMEM_LIMIT = 52 * 1024 * 1024


def _sigmoid(x):
    return 1.0 / (1.0 + jnp.exp(-x))


def _softplus(x):
    return jnp.maximum(x, 0.0) + jnp.log(1.0 + jnp.exp(-jnp.abs(x)))


def _dot(a, b, precision=None):
    return jnp.dot(a, b, preferred_element_type=F32, precision=precision)


def _dot_nt(a, b, precision=None):
    return lax.dot_general(a, b, (((1,), (1,)), ((), ())), preferred_element_type=F32,
                           precision=precision)


def _dot_tn(a, b):
    return lax.dot_general(a, b, (((0,), (0,)), ((), ())), preferred_element_type=F32)


def _split_bf16(a):
    hi = a.astype(BF16)
    return hi, (a - hi.astype(F32)).astype(BF16)


def _dot_split(a, b):
    return _dot(a[0], b[0]) + _dot(a[0], b[1]) + _dot(a[1], b[0])


C_QKV = 3 * GDN_HEADS * GDN_D
C_Z = GDN_HEADS * GDN_D
C_IN = C_QKV + C_Z + MLA_Q_LORA + MLA_KV_LORA + 3 * LANES


def _inproj_kernel(x_ref, nw_ref, w_ref, qkv_ref, z_ref, cq_ref, ckv_ref, kr_ref, ab_ref):
    x = x_ref[...]
    var = jnp.mean(x * x, axis=-1, keepdims=True)
    h = (x * lax.rsqrt(var + EPS) * nw_ref[...]).astype(BF16)
    p = _dot(h, w_ref[...])
    o = 0
    for ref in (qkv_ref, z_ref, cq_ref, ckv_ref, kr_ref, ab_ref):
        n = ref.shape[-1]
        ref[...] = p[:, o:o + n]
        o += n


def _inproj(x2, norm_w, w_all, tm):
    t, d = x2.shape
    widths = (C_QKV, C_Z, MLA_Q_LORA, MLA_KV_LORA, 2 * LANES, LANES)
    return pl.pallas_call(
        _inproj_kernel,
        grid=(t // tm,),
        in_specs=[pl.BlockSpec((tm, d), lambda i: (i, 0)),
                  pl.BlockSpec((1, d), lambda i: (0, 0)),
                  pl.BlockSpec((d, C_IN), lambda i: (0, 0))],
        out_specs=[pl.BlockSpec((tm, n), lambda i: (i, 0)) for n in widths],
        out_shape=[jax.ShapeDtypeStruct((t, n), F32) for n in widths],
        compiler_params=pltpu.CompilerParams(dimension_semantics=("parallel",),
                                             vmem_limit_bytes=VMEM_LIMIT),
        name="inproj",
    )(x2, norm_w, w_all)


def _gdn_kernel(qkv_ref, z_ref, ab_ref, cw_ref, alog_ref, dtb_ref, nw_ref, o_ref, ext_ref, s_ref, *, rows):
    c = CHUNK
    n_chunks = rows // c

    @pl.when(pl.program_id(1) == 0)
    def _():
        ext_ref[0:8, :] = jnp.zeros((8, C_QKV), F32)
        s_ref[...] = jnp.zeros_like(s_ref)

    x = qkv_ref[...]
    ext_ref[8:8 + rows, :] = x
    cw = cw_ref[...]
    xc = (cw[3:4] * x + cw[2:3] * ext_ref[7:7 + rows, :] + cw[1:2] * ext_ref[6:6 + rows, :]
          + cw[0:1] * ext_ref[5:5 + rows, :])
    ext_ref[0:8, :] = x[rows - 8:rows, :]
    xc = xc * _sigmoid(xc)

    ab = ab_ref[...]
    g_all = -jnp.exp(alog_ref[...]) * _softplus(ab + dtb_ref[...])
    beta_all = _sigmoid(ab)
    rr = lax.broadcasted_iota(jnp.int32, (rows, rows), 0)
    rc = lax.broadcasted_iota(jnp.int32, (rows, rows), 1)
    in_chunk_prefix = jnp.where(rr // c == rc // c, jnp.where(rr >= rc, 1.0, 0.0), 0.0)
    g_cum = _dot(in_chunk_prefix, g_all, HIGHEST)
    e8 = (lax.broadcasted_iota(jnp.int32, (8, LANES), 0)
          == lax.broadcasted_iota(jnp.int32, (8, LANES), 1)).astype(F32)
    g_row_all = _dot_nt(e8, g_cum, HIGHEST)
    ri = lax.broadcasted_iota(jnp.int32, (c, c), 0)
    ci = lax.broadcasted_iota(jnp.int32, (c, c), 1)
    causal = ri >= ci
    strict = ri > ci
    eye = (ri == ci).astype(F32)
    nw = nw_ref[...]

    chains = [(n, h) for n in range(n_chunks) for h in range(GDN_HEADS)]
    qs, ks, vs, gcs, betas, decays = [], [], [], [], [], []
    for n, h in chains:
        rs = slice(n * c, (n + 1) * c)
        q = xc[rs, h * GDN_D:(h + 1) * GDN_D]
        k = xc[rs, (GDN_HEADS + h) * GDN_D:(GDN_HEADS + h + 1) * GDN_D]
        qs.append(q * lax.rsqrt(jnp.sum(q * q, axis=-1, keepdims=True) + EPS) * (GDN_D ** -0.5))
        ks.append(k * lax.rsqrt(jnp.sum(k * k, axis=-1, keepdims=True) + EPS))
        vs.append(xc[rs, (2 * GDN_HEADS + h) * GDN_D:(2 * GDN_HEADS + h + 1) * GDN_D])
        gc = g_cum[rs, h:h + 1]
        gr = g_row_all[h:h + 1, rs]
        gcs.append(gc)
        betas.append(beta_all[rs, GDN_HEADS + h:GDN_HEADS + h + 1])
        decays.append(jnp.exp(jnp.where(causal, gc - gr, -jnp.inf)))
    kbs = [k * b for k, b in zip(ks, betas)]
    k16s = [k.astype(BF16) for k in ks]
    a_mats = [jnp.where(strict, _dot_nt(kb.astype(BF16), k16) * dec, 0.0)
              for kb, k16, dec in zip(kbs, k16s, decays)]
    t_mats = [eye - a for a in a_mats]
    pw_s = [_split_bf16(a) for a in a_mats]
    for _ in range(5):
        pw_s = [_split_bf16(_dot_split(p, p)) for p in pw_s]
        t_s = [_split_bf16(t) for t in t_mats]
        t_mats = [t + _dot_split(ts, p) for t, ts, p in zip(t_mats, t_s, pw_s)]
    e_gs = [jnp.exp(gc) for gc in gcs]
    uws = [_dot(t.astype(BF16), jnp.concatenate([v * b, kb * eg], axis=1).astype(BF16))
           for t, v, b, kb, eg in zip(t_mats, vs, betas, kbs, e_gs)]
    qk16s = [(_dot_nt(q.astype(BF16), k16) * dec).astype(BF16) for q, k16, dec in zip(qs, k16s, decays)]
    qd16s = [(q * eg).astype(BF16) for q, eg in zip(qs, e_gs)]
    kd16s = [(k * jnp.exp(gc[c - 1:c, :] - gc)).astype(BF16) for k, gc in zip(ks, gcs)]
    g_tots = [jnp.exp(gc[c - 1:c, :]) for gc in gcs]

    heads = range(GDN_HEADS)
    states = [s_ref[h] for h in heads]
    for n in range(n_chunks):
        rs = slice(n * c, (n + 1) * c)
        ix = [n * GDN_HEADS + h for h in heads]
        s16s = [states[h].astype(BF16) for h in heads]
        v16s = [(uws[i][:, :GDN_D] - _dot(uws[i][:, GDN_D:].astype(BF16), s16s[h])).astype(BF16)
                for h, i in zip(heads, ix)]
        states = [states[h] * g_tots[i] + _dot_tn(kd16s[i], v16s[h]) for h, i in zip(heads, ix)]
        os_ = [_dot(qd16s[i], s16s[h]) + _dot(qk16s[i], v16s[h]) for h, i in zip(heads, ix)]
        for h in heads:
            o = os_[h]
            on = o * lax.rsqrt(jnp.mean(o * o, axis=-1, keepdims=True) + EPS) * nw
            zh = z_ref[rs, h * GDN_D:(h + 1) * GDN_D]
            o_ref[rs, h * GDN_D:(h + 1) * GDN_D] = on * (zh * _sigmoid(zh))
    for h in heads:
        s_ref[h] = states[h]


def _gdn(qkv, z, ab, conv_w8, alog_row, dtb_row, norm_w, bsz, seq, rows):
    n = seq // rows
    row = lambda b, i: (b * n + i, 0)
    const = lambda b, i: (0, 0)
    return pl.pallas_call(
        functools.partial(_gdn_kernel, rows=rows),
        grid=(bsz, n),
        in_specs=[pl.BlockSpec((rows, C_QKV), row),
                  pl.BlockSpec((rows, C_Z), row),
                  pl.BlockSpec((rows, LANES), row),
                  pl.BlockSpec((8, C_QKV), const),
                  pl.BlockSpec((1, LANES), const),
                  pl.BlockSpec((1, LANES), const),
                  pl.BlockSpec((1, GDN_D), const)],
        out_specs=pl.BlockSpec((rows, C_Z), row),
        out_shape=jax.ShapeDtypeStruct((bsz * seq, C_Z), F32),
        scratch_shapes=[pltpu.VMEM((8 + rows, C_QKV), F32),
                        pltpu.VMEM((GDN_HEADS, GDN_D, GDN_D), F32)],
        compiler_params=pltpu.CompilerParams(dimension_semantics=("arbitrary", "arbitrary"),
                                             vmem_limit_bytes=VMEM_LIMIT),
        name="gdn",
    )(qkv, z, ab, conv_w8, alog_row, dtb_row, norm_w)


D_QK_PAD = LANES
C_HEADS = MLA_HEADS * D_QK_PAD
C_V = MLA_HEADS * MLA_V
V_AUG = MLA_V + 16
C_VAUG = MLA_HEADS * V_AUG
Q_SCALE = (MLA_NOPE + MLA_ROPE) ** -0.5 * math.log2(math.e)


def _mla_proj_kernel(cq_ref, ckv_ref, kr_ref, posc_ref, posr_ref, qnw_ref, kvnw_ref, wqt_ref, wqrt_ref,
                     wk_ref, wvt_ref, freqr_ref, freqc_ref, qt_ref, k_ref, vt_ref):
    cq = cq_ref[...]
    cqn = (cq * lax.rsqrt(jnp.mean(cq * cq, axis=-1, keepdims=True) + EPS) * qnw_ref[...]).astype(BF16)
    ckv = ckv_ref[...]
    ckvn = (ckv * lax.rsqrt(jnp.mean(ckv * ckv, axis=-1, keepdims=True) + EPS)
            * kvnw_ref[...]).astype(BF16)
    ang = posc_ref[...] * freqr_ref[...]
    cs = jnp.cos(ang)
    sn = jnp.sin(ang)
    ang_t = freqc_ref[...] * posr_ref[...]
    cs_t = jnp.cos(ang_t)
    sn_t = jnp.sin(ang_t)
    qa_t = _dot_nt(wqt_ref[...], cqn)
    qb_t = _dot_nt(wqrt_ref[...], cqn)
    kn = _dot(ckvn, wk_ref[...])
    kr = kr_ref[...]
    kpe = kr[:, :LANES] * cs + kr[:, LANES:] * sn
    for h in range(MLA_HEADS):
        sl = slice(h * LANES, (h + 1) * LANES)
        qt_ref[sl, :] = ((qa_t[sl, :] * cs_t + qb_t[sl, :] * sn_t) * Q_SCALE).astype(BF16)
        k_ref[:, sl] = (kn[:, sl] + kpe).astype(BF16)
    v_t = _dot_nt(wvt_ref[...], ckvn).astype(BF16)
    ones = jnp.ones((V_AUG - MLA_V, v_t.shape[1]), BF16)
    for h in range(MLA_HEADS):
        vt_ref[0, h * V_AUG:h * V_AUG + MLA_V, :] = v_t[h * MLA_V:(h + 1) * MLA_V, :]
        vt_ref[0, h * V_AUG + MLA_V:(h + 1) * V_AUG, :] = ones


def _mla_proj(cq, ckv, kr2, posc, posr, qnw, kvnw, wqt, wqrt, wk, wvt, freqr, freqc, tm):
    t = cq.shape[0]
    row = lambda i: (i, 0)
    col = lambda i: (0, i)
    const = lambda i: (0, 0)
    return pl.pallas_call(
        _mla_proj_kernel,
        grid=(t // tm,),
        in_specs=[pl.BlockSpec((tm, MLA_Q_LORA), row),
                  pl.BlockSpec((tm, MLA_KV_LORA), row),
                  pl.BlockSpec((tm, 2 * LANES), row),
                  pl.BlockSpec((tm, 1), row),
                  pl.BlockSpec((1, tm), col),
                  pl.BlockSpec((1, MLA_Q_LORA), const),
                  pl.BlockSpec((1, MLA_KV_LORA), const),
                  pl.BlockSpec((C_HEADS, MLA_Q_LORA), const),
                  pl.BlockSpec((C_HEADS, MLA_Q_LORA), const),
                  pl.BlockSpec((MLA_KV_LORA, C_HEADS), const),
                  pl.BlockSpec((C_V, MLA_KV_LORA), const),
                  pl.BlockSpec((1, LANES), const),
                  pl.BlockSpec((LANES, 1), const)],
        out_specs=[pl.BlockSpec((C_HEADS, tm), col),
                   pl.BlockSpec((tm, C_HEADS), row),
                   pl.BlockSpec((1, C_VAUG, tm), lambda i: (i, 0, 0))],
        out_shape=[jax.ShapeDtypeStruct((C_HEADS, t), BF16),
                   jax.ShapeDtypeStruct((t, C_HEADS), BF16),
                   jax.ShapeDtypeStruct((t // tm, C_VAUG, tm), BF16)],
        compiler_params=pltpu.CompilerParams(dimension_semantics=("parallel",),
                                             vmem_limit_bytes=VMEM_LIMIT),
        name="mla_proj",
    )(cq, ckv, kr2, posc, posr, qnw, kvnw, wqt, wqrt, wk, wvt, freqr, freqc)


def _attn_kernel(qt_ref, k_ref, vt_ref, nw_ref, ot_ref, m_ref, acc_ref, *, tq):
    i = pl.program_id(1)
    m_ref[...] = jnp.full(m_ref.shape, NEG_BIG, F32)
    acc_ref[...] = jnp.zeros(acc_ref.shape, F32)
    key_chunk = lax.broadcasted_iota(jnp.int32, (tq, tq), 0) // CHUNK
    qry_chunk = lax.broadcasted_iota(jnp.int32, (tq, tq), 1) // CHUNK
    diag_mask = key_chunk <= qry_chunk

    def tile(j, masked):
        rows = pl.ds(pl.multiple_of(j * tq, tq), tq)
        scores = []
        for h in range(MLA_HEADS):
            hs = slice(h * LANES, (h + 1) * LANES)
            s_t = _dot(k_ref[rows, hs], qt_ref[hs, :])
            scores.append(jnp.where(diag_mask, s_t, NEG_BIG) if masked else s_t)
        probs, alphas = [], []
        for h in range(MLA_HEADS):
            m_old = m_ref[h]
            m_new = jnp.maximum(m_old, jnp.max(scores[h], axis=0, keepdims=True))
            m_ref[h] = m_new
            probs.append(jnp.exp2(scores[h] - m_new[0:1, :]).astype(BF16))
            alphas.append(jnp.exp2(m_old[0:1, :] - m_new[0:1, :]))
        for h in range(MLA_HEADS):
            v_aug = vt_ref[j, h * V_AUG:(h + 1) * V_AUG, :]
            acc_ref[h] = alphas[h] * acc_ref[h] + _dot(v_aug, probs[h])

    def full_tile(j, carry):
        tile(j, False)
        return carry

    lax.fori_loop(0, i, full_tile, 0)
    tile(i, True)

    outs = []
    for h in range(MLA_HEADS):
        acc = acc_ref[h]
        outs.append(acc[:MLA_V, :] / acc[MLA_V:MLA_V + 1, :])
    ssq = outs[0] * outs[0]
    for o in outs[1:]:
        ssq = ssq + o * o
    inv = lax.rsqrt(jnp.sum(ssq, axis=0, keepdims=True) / C_V + EPS)
    for h in range(MLA_HEADS):
        vs = slice(h * MLA_V, (h + 1) * MLA_V)
        ot_ref[vs, :] = outs[h] * inv * nw_ref[vs, :]


def _attention(qt, k, vt, norm_w_col, bsz, seq, tq):
    nq = seq // tq
    return pl.pallas_call(
        functools.partial(_attn_kernel, tq=tq),
        grid=(bsz, nq),
        in_specs=[pl.BlockSpec((C_HEADS, tq), lambda b, i: (0, b * nq + i)),
                  pl.BlockSpec((seq, C_HEADS), lambda b, i: (b, 0)),
                  pl.BlockSpec((nq, C_VAUG, tq), lambda b, i: (b, 0, 0)),
                  pl.BlockSpec((C_V, 1), lambda b, i: (0, 0))],
        out_specs=pl.BlockSpec((C_V, tq), lambda b, i: (0, b * nq + i)),
        out_shape=jax.ShapeDtypeStruct((C_V, bsz * seq), F32),
        scratch_shapes=[pltpu.VMEM((MLA_HEADS, 8, tq), F32),
                        pltpu.VMEM((MLA_HEADS, V_AUG, tq), F32)],
        compiler_params=pltpu.CompilerParams(dimension_semantics=("parallel", "parallel"),
                                             vmem_limit_bytes=VMEM_LIMIT),
        name="attention",
    )(qt, k, vt, norm_w_col)


def _router_kernel(x_ref, oa_ref, obt_ref, wo_ref, nw_ref, rw_ref, rb_ref,
                   x1_ref, h_ref, eidx_ref, gate_ref, rank_ref, cnt_ref, run_ref, *, tm):
    @pl.when(pl.program_id(0) == 0)
    def _():
        run_ref[...] = jnp.zeros_like(run_ref)

    half = oa_ref.shape[-1]
    y = (_dot(oa_ref[...].astype(BF16), wo_ref[0:half, :])
         + _dot_tn(obt_ref[...].astype(BF16), wo_ref[half:, :]))
    x1 = x_ref[...] + y
    x1_ref[...] = x1
    hn = x1 * lax.rsqrt(jnp.mean(x1 * x1, axis=-1, keepdims=True) + EPS) * nw_ref[...]
    h_ref[...] = hn.astype(BF16)
    lane = lax.broadcasted_iota(jnp.int32, (tm, LANES), 1)
    logits = _dot(hn, rw_ref[...], HIGHEST) + rb_ref[...]
    lg = jnp.where(lane < N_EXPERTS, logits, -jnp.inf)
    sels, tops = [], []
    eidx = jnp.zeros((tm, LANES), jnp.int32)
    for kk in range(TOP_K):
        m = jnp.max(lg, axis=-1, keepdims=True)
        idx = jnp.min(jnp.where(lg == m, lane, LANES), axis=-1, keepdims=True)
        sel = lane == idx
        lg = jnp.where(sel, -jnp.inf, lg)
        sels.append(sel)
        tops.append(m)
        eidx = jnp.where(lane == kk, idx, eidx)
    exps = [jnp.exp(tv - tops[0]) for tv in tops]
    den = exps[0] + exps[1] + exps[2] + exps[3]
    gates = jnp.zeros((tm, LANES), F32)
    multi = jnp.zeros((tm, LANES), F32)
    for kk in range(TOP_K):
        gates = jnp.where(lane == kk, exps[kk] / den, gates)
        multi = multi + sels[kk].astype(F32)
    ri = lax.broadcasted_iota(jnp.int32, (tm, tm), 0)
    ci = lax.broadcasted_iota(jnp.int32, (tm, tm), 1)
    before = _dot((ri > ci).astype(BF16), multi.astype(BF16)) + run_ref[...]
    rank = jnp.zeros((tm, LANES), jnp.int32)
    for kk in range(TOP_K):
        r = jnp.sum(jnp.where(sels[kk], before, 0.0), axis=-1, keepdims=True)
        rank = jnp.where(lane == kk, r.astype(jnp.int32), rank)
    run = run_ref[...] + jnp.sum(multi, axis=0, keepdims=True)
    run_ref[...] = run
    cnt_ref[...] = run
    eidx_ref[...] = eidx
    gate_ref[...] = gates
    rank_ref[...] = rank


def _router(x2, oa, obt, w_out, norm_w, rw, rb, tm):
    t, d = x2.shape
    half = oa.shape[-1]
    row = lambda i: (i, 0)
    const = lambda i: (0, 0)
    return pl.pallas_call(
        functools.partial(_router_kernel, tm=tm),
        grid=(t // tm,),
        in_specs=[pl.BlockSpec((tm, d), row),
                  pl.BlockSpec((tm, half), row),
                  pl.BlockSpec((half, tm), lambda i: (0, i)),
                  pl.BlockSpec((2 * half, d), const),
                  pl.BlockSpec((1, d), const),
                  pl.BlockSpec((d, LANES), const),
                  pl.BlockSpec((1, LANES), const)],
        out_specs=[pl.BlockSpec((tm, d), row), pl.BlockSpec((tm, d), row),
                   pl.BlockSpec((tm, LANES), row), pl.BlockSpec((tm, LANES), row),
                   pl.BlockSpec((tm, LANES), row), pl.BlockSpec((1, LANES), const)],
        out_shape=[jax.ShapeDtypeStruct((t, d), F32), jax.ShapeDtypeStruct((t, d), BF16),
                   jax.ShapeDtypeStruct((t, LANES), jnp.int32), jax.ShapeDtypeStruct((t, LANES), F32),
                   jax.ShapeDtypeStruct((t, LANES), jnp.int32), jax.ShapeDtypeStruct((1, LANES), F32)],
        scratch_shapes=[pltpu.VMEM((1, LANES), F32)],
        compiler_params=pltpu.CompilerParams(dimension_semantics=("arbitrary",),
                                             vmem_limit_bytes=VMEM_LIMIT),
        name="router",
    )(x2, oa, obt, w_out, norm_w, rw, rb)


def _dispatch_kernel(dest_ref, h_ref, buf_in_ref, buf_ref, sem, *, tm):
    del buf_in_ref

    def issue(t, carry):
        for kk in range(TOP_K):
            d = dest_ref[0, 0, t * TOP_K + kk]
            pltpu.make_async_copy(h_ref.at[t], buf_ref.at[d], sem).start()
        return carry

    lax.fori_loop(0, tm, issue, 0)
    for kk in range(TOP_K):
        pltpu.make_async_copy(h_ref, buf_ref.at[pl.ds(0, tm)], sem).wait()


def _dispatch(dest, h3, n_pad, tm):
    t, s, _ = h3.shape
    buf0 = jnp.zeros((n_pad, s, LANES), h3.dtype)
    return pl.pallas_call(
        functools.partial(_dispatch_kernel, tm=tm),
        grid=(t // tm,),
        in_specs=[pl.BlockSpec((1, 1, tm * TOP_K), lambda i: (i, 0, 0), memory_space=pltpu.SMEM),
                  pl.BlockSpec((tm, s, LANES), lambda i: (i, 0, 0)),
                  pl.BlockSpec(memory_space=pl.ANY)],
        out_specs=pl.BlockSpec(memory_space=pl.ANY),
        out_shape=jax.ShapeDtypeStruct((n_pad, s, LANES), h3.dtype),
        scratch_shapes=[pltpu.SemaphoreType.DMA(())],
        input_output_aliases={2: 0},
        compiler_params=pltpu.CompilerParams(dimension_semantics=("arbitrary",)),
        name="dispatch",
    )(dest.reshape(t // tm, 1, tm * TOP_K), h3, buf0)


def _expert_kernel(be_ref, bv_ref, bf_ref, x_ref, w1_ref, b1g_ref, b1l_ref, w2_ref, b2_ref, y_ref,
                   w1s_ref, w2s_ref):
    del be_ref
    i = pl.program_id(0)
    valid = bv_ref[i] == 1
    n_tiles = w1_ref.shape[-1] // MXU_COLS
    half = MXU_COLS // 2

    @pl.when(bf_ref[i] == 1)
    def _():
        r = lax.broadcasted_iota(jnp.int32, (MXU_COLS, MXU_COLS), 0)
        c = lax.broadcasted_iota(jnp.int32, (MXU_COLS, MXU_COLS), 1)
        perm = (r == jnp.where(c < half, 2 * c, 2 * (c - half) + 1)).astype(BF16)
        for tix in range(n_tiles):
            cols = slice(tix * MXU_COLS, (tix + 1) * MXU_COLS)
            w1s_ref[:, cols] = _dot(w1_ref[0, :, cols].astype(BF16), perm).astype(BF16)
        w2s_ref[...] = w2_ref[0].astype(BF16)

    @pl.when(jnp.logical_not(valid))
    def _():
        y_ref[...] = jnp.zeros_like(y_ref)

    @pl.when(valid)
    def _():
        x = x_ref[...]
        acts = []
        for tix in range(n_tiles):
            hp = _dot(x, w1s_ref[:, tix * MXU_COLS:(tix + 1) * MXU_COLS])
            feat = slice(tix * half, (tix + 1) * half)
            g = jnp.minimum(hp[:, :half] + b1g_ref[0][:, feat], SWIGLU_LIMIT)
            lin = jnp.clip(hp[:, half:] + b1l_ref[0][:, feat], -SWIGLU_LIMIT, SWIGLU_LIMIT)
            acts.append((g * _sigmoid(SWIGLU_ALPHA * g) * (lin + 1.0)).astype(BF16))
        act = jnp.concatenate(acts, axis=1)
        y_ref[...] = (_dot(act, w2s_ref[...]) + b2_ref[0]).astype(y_ref.dtype)


def _experts(block_expert, block_valid, block_first, xs, w1, b1g, b1l, w2, b2, bm):
    n_pad, d = xs.shape
    dff2 = w1.shape[-1]
    dff = dff2 // 2
    wmap = lambda i, be, bv, bf: (be[i], 0, 0)
    grid_spec = pltpu.PrefetchScalarGridSpec(
        num_scalar_prefetch=3,
        grid=(n_pad // bm,),
        in_specs=[pl.BlockSpec((bm, d), lambda i, be, bv, bf: (i, 0)),
                  pl.BlockSpec((1, d, dff2), wmap),
                  pl.BlockSpec((1, 1, dff), wmap),
                  pl.BlockSpec((1, 1, dff), wmap),
                  pl.BlockSpec((1, dff, d), wmap),
                  pl.BlockSpec((1, 1, d), wmap)],
        out_specs=pl.BlockSpec((bm, d), lambda i, be, bv, bf: (i, 0)),
        scratch_shapes=[pltpu.VMEM((d, dff2), BF16), pltpu.VMEM((dff, d), BF16)],
    )
    return pl.pallas_call(
        _expert_kernel,
        grid_spec=grid_spec,
        out_shape=jax.ShapeDtypeStruct((n_pad, d), BF16),
        compiler_params=pltpu.CompilerParams(dimension_semantics=("arbitrary",),
                                             vmem_limit_bytes=VMEM_LIMIT),
        name="experts",
    )(block_expert, block_valid, block_first, xs, w1, b1g, b1l, w2, b2)


def _combine_kernel(dest_ref, gate_ref, y_ref, o_ref, gbuf, sem, *, tm):
    def issue(t, carry):
        for kk in range(TOP_K):
            d = dest_ref[0, 0, t * TOP_K + kk]
            pltpu.make_async_copy(y_ref.at[d], gbuf.at[kk * tm + t], sem).start()
        return carry

    lax.fori_loop(0, tm, issue, 0)
    for kk in range(TOP_K):
        pltpu.make_async_copy(y_ref.at[pl.ds(0, tm)], gbuf.at[pl.ds(kk * tm, tm)], sem).wait()

    def mix(t, carry):
        acc = gate_ref[0, 0, t * TOP_K] * gbuf[t].astype(F32)
        for kk in range(1, TOP_K):
            acc = acc + gate_ref[0, 0, t * TOP_K + kk] * gbuf[kk * tm + t].astype(F32)
        o_ref[t] = acc
        return carry

    lax.fori_loop(0, tm, mix, 0)


def _combine(dest, gates, y3, t, tm):
    n_pad, s, _ = y3.shape
    smem = lambda: pl.BlockSpec((1, 1, tm * TOP_K), lambda i: (i, 0, 0), memory_space=pltpu.SMEM)
    return pl.pallas_call(
        functools.partial(_combine_kernel, tm=tm),
        grid=(t // tm,),
        in_specs=[smem(), smem(), pl.BlockSpec(memory_space=pl.ANY)],
        out_specs=pl.BlockSpec((tm, s, LANES), lambda i: (i, 0, 0)),
        out_shape=jax.ShapeDtypeStruct((t, s, LANES), F32),
        scratch_shapes=[pltpu.VMEM((TOP_K * tm, s, LANES), y3.dtype), pltpu.SemaphoreType.DMA(())],
        compiler_params=pltpu.CompilerParams(dimension_semantics=("arbitrary",)),
        name="combine",
    )(dest.reshape(t // tm, 1, tm * TOP_K), gates.reshape(t // tm, 1, tm * TOP_K), y3)


def _final_kernel(x1_ref, y_ref, nw_ref, o_ref):
    x = x1_ref[...] + y_ref[...]
    o_ref[...] = x * lax.rsqrt(jnp.mean(x * x, axis=-1, keepdims=True) + EPS) * nw_ref[...]


def _final(x1, y, norm_w, tm):
    t, d = x1.shape
    row = lambda i: (i, 0)
    return pl.pallas_call(
        _final_kernel,
        grid=(t // tm,),
        in_specs=[pl.BlockSpec((tm, d), row), pl.BlockSpec((tm, d), row),
                  pl.BlockSpec((1, d), lambda i: (0, 0))],
        out_specs=pl.BlockSpec((tm, d), row),
        out_shape=jax.ShapeDtypeStruct((t, d), F32),
        compiler_params=pltpu.CompilerParams(dimension_semantics=("parallel",)),
        name="final_norm",
    )(x1, y, norm_w)


def _pad_cols(w, n):
    return jnp.pad(w, ((0, 0), (0, n - w.shape[1])))


def _rot_half(w):
    half = w.shape[-1] // 2
    return jnp.concatenate([-w[..., half:], w[..., :half]], axis=-1)


def _prep_inproj_weight(w_in):
    d = w_in.shape[0]
    hd = GDN_HEADS * GDN_D
    o = 0
    parts = {}
    for name, n in (("qkv", 3 * hd), ("z", hd), ("a", GDN_HEADS), ("b", GDN_HEADS),
                    ("cq", MLA_Q_LORA), ("ckv", MLA_KV_LORA), ("kr", MLA_ROPE)):
        parts[name] = w_in[:, o:o + n]
        o += n
    zeros = lambda n: jnp.zeros((d, n), w_in.dtype)
    kr_tile = jnp.concatenate([zeros(MLA_NOPE), parts["kr"], zeros(LANES - MLA_NOPE - MLA_ROPE)], axis=1)
    krr_tile = jnp.concatenate([zeros(MLA_NOPE), _rot_half(parts["kr"]),
                                zeros(LANES - MLA_NOPE - MLA_ROPE)], axis=1)
    ab_tile = jnp.concatenate([parts["a"], parts["b"], zeros(LANES - 2 * GDN_HEADS)], axis=1)
    return jnp.concatenate([parts["qkv"], parts["z"], parts["cq"], parts["ckv"], kr_tile, krr_tile,
                            ab_tile], axis=1).astype(BF16)


def _prep_mla_weights(w_uq, w_ukv):
    dq = MLA_NOPE + MLA_ROPE
    wq = w_uq.reshape(MLA_Q_LORA, MLA_HEADS, dq)
    wq_rot = jnp.concatenate([jnp.zeros_like(wq[..., :MLA_NOPE]), _rot_half(wq[..., MLA_NOPE:])], axis=-1)
    pad = lambda w: jnp.pad(w, ((0, 0), (0, 0), (0, LANES - w.shape[-1]))).reshape(w.shape[0], C_HEADS)
    wkv = w_ukv.reshape(MLA_KV_LORA, MLA_HEADS, MLA_NOPE + MLA_V)
    wv = wkv[..., MLA_NOPE:].reshape(MLA_KV_LORA, C_V)
    return (pad(wq).T.astype(BF16), pad(wq_rot).T.astype(BF16), pad(wkv[..., :MLA_NOPE]).astype(BF16),
            wv.T.astype(BF16))


def _rope_freq():
    half = MLA_ROPE // 2
    inv_freq = ROPE_THETA ** (-jnp.arange(half, dtype=F32) / half)
    return jnp.zeros((LANES,), F32).at[MLA_NOPE:MLA_NOPE + MLA_ROPE].set(jnp.concatenate([inv_freq, inv_freq]))


def _lane_row(v):
    return jnp.pad(v.astype(F32), (0, LANES - v.shape[0])).reshape(1, LANES)


def _routing_tables(cnt, eidx, rank, n_blocks, bm):
    e_ids = jnp.arange(N_EXPERTS, dtype=jnp.int32)
    counts = cnt[0, :N_EXPERTS].astype(jnp.int32)
    padded = (counts + bm - 1) // bm * bm
    pend = jnp.sum(jnp.where(e_ids[None, :] <= e_ids[:, None], padded[None, :], 0), axis=1)
    pstart = pend - padded
    dest = jnp.sum(jnp.where(eidx[:, :TOP_K, None] == e_ids, pstart, 0), axis=-1) + rank[:, :TOP_K]
    blk = jnp.arange(n_blocks, dtype=jnp.int32) * bm
    block_expert = jnp.minimum(jnp.sum((pend[None, :] <= blk[:, None]).astype(jnp.int32), axis=1),
                               N_EXPERTS - 1)
    block_valid = blk < pend[-1]
    prev = jnp.concatenate([jnp.full((1,), -1, jnp.int32), block_expert[:-1]])
    block_first = jnp.logical_and(block_valid, block_expert != prev)
    return dest, block_expert, block_valid.astype(jnp.int32), block_first.astype(jnp.int32)


def kernel(x, positions, norm_mix_w, w_in, conv_w, a_log, dt_bias, gdn_norm_w, q_norm_w, w_uq, kv_norm_w,
           w_ukv, mla_out_norm_w, w_out, norm_ffn_w, router_w, router_b, w1, b1, w2, b2, norm_final_w):
    bsz, seq, d = x.shape
    t = bsz * seq
    depth = w_in.shape[0]
    tm = min(512, t)
    tr = min(256, t)
    bm = 256
    tq = min(256, seq)
    x2 = x.reshape(t, d)
    posf = positions.astype(F32)
    freq = _rope_freq()
    for l in range(depth):
        qkv, z, cq, ckv, kr2, ab = _inproj(x2, norm_mix_w[l].reshape(1, d), _prep_inproj_weight(w_in[l]), tm)
        conv_w8 = jnp.pad(conv_w[l], ((0, 8 - GDN_CONV), (0, 0)))
        o_a = _gdn(qkv, z, ab, conv_w8, _lane_row(a_log[l]), _lane_row(dt_bias[l]),
                   gdn_norm_w[l].reshape(1, GDN_D), bsz, seq, min(256, seq))
        wqt, wqrt, wk, wvt = _prep_mla_weights(w_uq[l], w_ukv[l])
        qt, k, vt = _mla_proj(cq, ckv, kr2, posf.reshape(t, 1), posf.reshape(1, t),
                              q_norm_w[l].reshape(1, -1), kv_norm_w[l].reshape(1, -1),
                              wqt, wqrt, wk, wvt, freq.reshape(1, LANES), freq.reshape(LANES, 1), tq)
        o_bt = _attention(qt, k, vt, mla_out_norm_w[l].reshape(-1, 1), bsz, seq, tq)
        rw = _pad_cols(router_w[l], LANES)
        rb = _lane_row(router_b[l])
        x1, hn, eidx, gates, rank, cnt = _router(x2, o_a, o_bt, w_out[l].astype(BF16),
                                                 norm_ffn_w[l].reshape(1, d), rw, rb, tr)
        n_pad = t * TOP_K + N_EXPERTS * bm
        dest, block_expert, block_valid, block_first = _routing_tables(cnt, eidx, rank, n_pad // bm, bm)
        buf = _dispatch(dest, hn.reshape(t, d // LANES, LANES), n_pad, tr)
        yb = _experts(block_expert, block_valid, block_first, buf.reshape(n_pad, d), w1[l],
                      b1[l][:, None, 0::2], b1[l][:, None, 1::2], w2[l], b2[l][:, None, :], bm)
        y3 = _combine(dest, gates[:, :TOP_K], yb.reshape(n_pad, d // LANES, LANES), t, tr)
        if l == depth - 1:
            x2 = _final(x1, y3.reshape(t, d), norm_final_w.reshape(1, d), tm)
        else:
            x2 = x1 + y3.reshape(t, d)
    return x2.reshape(bsz, seq, d)
```

```python
import functools
import math

import jax
import jax.numpy as jnp
from jax import lax
from jax.experimental import pallas as pl
from jax.experimental.pallas import tpu as pltpu

F32 = jnp.float32
BF16 = jnp.bfloat16
HIGHEST = lax.Precision.HIGHEST

LANES = 128
MXU_COLS = 256
EPS = 1e-6
CHUNK = 64

GDN_HEADS = 4
GDN_D = 128
GDN_CONV = 4
MLA_HEADS = 8
MLA_NOPE = 64
MLA_ROPE = 32
MLA_V = 64
MLA_Q_LORA = 384
MLA_KV_LORA = 256
ROPE_THETA = 10000.0
N_EXPERTS = 32
TOP_K = 4
SWIGLU_LIMIT = 7.0
SWIGLU_ALPHA = 1.702

NEG_BIG = -1e30
VMEM_LIMIT = 52 * 1024 * 1024


def _sigmoid(x):
    return 1.0 / (1.0 + jnp.exp(-x))


def _softplus(x):
    return jnp.maximum(x, 0.0) + jnp.log(1.0 + jnp.exp(-jnp.abs(x)))


def _dot(a, b, precision=None):
    return jnp.dot(a, b, preferred_element_type=F32, precision=precision)


def _dot_nt(a, b, precision=None):
    return lax.dot_general(a, b, (((1,), (1,)), ((), ())), preferred_element_type=F32,
                           precision=precision)


def _dot_tn(a, b):
    return lax.dot_general(a, b, (((0,), (0,)), ((), ())), preferred_element_type=F32)


def _split_bf16(a):
    hi = a.astype(BF16)
    return hi, (a - hi.astype(F32)).astype(BF16)


def _dot_split(a, b):
    return _dot(a[0], b[0]) + _dot(a[0], b[1]) + _dot(a[1], b[0])


C_QKV = 3 * GDN_HEADS * GDN_D
C_Z = GDN_HEADS * GDN_D
C_IN = C_QKV + C_Z + MLA_Q_LORA + MLA_KV_LORA + 3 * LANES


def _inproj_kernel(x_ref, nw_ref, w_ref, qkv_ref, z_ref, cq_ref, ckv_ref, kr_ref, ab_ref):
    x = x_ref[...]
    var = jnp.mean(x * x, axis=-1, keepdims=True)
    h = (x * lax.rsqrt(var + EPS) * nw_ref[...]).astype(BF16)
    p = _dot(h, w_ref[...])
    o = 0
    for ref in (qkv_ref, z_ref, cq_ref, ckv_ref, kr_ref, ab_ref):
        n = ref.shape[-1]
        ref[...] = p[:, o:o + n]
        o += n


def _inproj(x2, norm_w, w_all, tm):
    t, d = x2.shape
    widths = (C_QKV, C_Z, MLA_Q_LORA, MLA_KV_LORA, 2 * LANES, LANES)
    return pl.pallas_call(
        _inproj_kernel,
        grid=(t // tm,),
        in_specs=[pl.BlockSpec((tm, d), lambda i: (i, 0)),
                  pl.BlockSpec((1, d), lambda i: (0, 0)),
                  pl.BlockSpec((d, C_IN), lambda i: (0, 0))],
        out_specs=[pl.BlockSpec((tm, n), lambda i: (i, 0)) for n in widths],
        out_shape=[jax.ShapeDtypeStruct((t, n), F32) for n in widths],
        compiler_params=pltpu.CompilerParams(dimension_semantics=("parallel",),
                                             vmem_limit_bytes=VMEM_LIMIT),
        name="inproj",
    )(x2, norm_w, w_all)


def _gdn_kernel(qkv_ref, z_ref, ab_ref, cw_ref, alog_ref, dtb_ref, nw_ref, o_ref, ext_ref, s_ref, *, rows):
    c = CHUNK
    n_chunks = rows // c

    @pl.when(pl.program_id(1) == 0)
    def _():
        ext_ref[0:8, :] = jnp.zeros((8, C_QKV), F32)
        s_ref[...] = jnp.zeros_like(s_ref)

    x = qkv_ref[...]
    ext_ref[8:8 + rows, :] = x
    cw = cw_ref[...]
    xc = (cw[3:4] * x + cw[2:3] * ext_ref[7:7 + rows, :] + cw[1:2] * ext_ref[6:6 + rows, :]
          + cw[0:1] * ext_ref[5:5 + rows, :])
    ext_ref[0:8, :] = x[rows - 8:rows, :]
    xc = xc * _sigmoid(xc)

    ab = ab_ref[...]
    g_all = -jnp.exp(alog_ref[...]) * _softplus(ab + dtb_ref[...])
    beta_all = _sigmoid(ab)
    rr = lax.broadcasted_iota(jnp.int32, (rows, rows), 0)
    rc = lax.broadcasted_iota(jnp.int32, (rows, rows), 1)
    in_chunk_prefix = jnp.where(rr // c == rc // c, jnp.where(rr >= rc, 1.0, 0.0), 0.0)
    g_cum = _dot(in_chunk_prefix, g_all, HIGHEST)
    e8 = (lax.broadcasted_iota(jnp.int32, (8, LANES), 0)
          == lax.broadcasted_iota(jnp.int32, (8, LANES), 1)).astype(F32)
    g_row_all = _dot_nt(e8, g_cum, HIGHEST)
    ri = lax.broadcasted_iota(jnp.int32, (c, c), 0)
    ci = lax.broadcasted_iota(jnp.int32, (c, c), 1)
    causal = ri >= ci
    strict = ri > ci
    eye = (ri == ci).astype(F32)
    nw = nw_ref[...]

    chains = [(n, h) for n in range(n_chunks) for h in range(GDN_HEADS)]
    qs, ks, vs, gcs, betas, decays = [], [], [], [], [], []
    for n, h in chains:
        rs = slice(n * c, (n + 1) * c)
        q = xc[rs, h * GDN_D:(h + 1) * GDN_D]
        k = xc[rs, (GDN_HEADS + h) * GDN_D:(GDN_HEADS + h + 1) * GDN_D]
        qs.append(q * lax.rsqrt(jnp.sum(q * q, axis=-1, keepdims=True) + EPS) * (GDN_D ** -0.5))
        ks.append(k * lax.rsqrt(jnp.sum(k * k, axis=-1, keepdims=True) + EPS))
        vs.append(xc[rs, (2 * GDN_HEADS + h) * GDN_D:(2 * GDN_HEADS + h + 1) * GDN_D])
        gc = g_cum[rs, h:h + 1]
        gr = g_row_all[h:h + 1, rs]
        gcs.append(gc)
        betas.append(beta_all[rs, GDN_HEADS + h:GDN_HEADS + h + 1])
        decays.append(jnp.exp(jnp.where(causal, gc - gr, -jnp.inf)))
    kbs = [k * b for k, b in zip(ks, betas)]
    k16s = [k.astype(BF16) for k in ks]
    a_mats = [jnp.where(strict, _dot_nt(kb.astype(BF16), k16) * dec, 0.0)
              for kb, k16, dec in zip(kbs, k16s, decays)]
    t_mats = [eye - a for a in a_mats]
    pw_s = [_split_bf16(a) for a in a_mats]
    for _ in range(5):
        pw_s = [_split_bf16(_dot_split(p, p)) for p in pw_s]
        t_s = [_split_bf16(t) for t in t_mats]
        t_mats = [t + _dot_split(ts, p) for t, ts, p in zip(t_mats, t_s, pw_s)]
    e_gs = [jnp.exp(gc) for gc in gcs]
    uws = [_dot(t.astype(BF16), jnp.concatenate([v * b, kb * eg], axis=1).astype(BF16))
           for t, v, b, kb, eg in zip(t_mats, vs, betas, kbs, e_gs)]
    qk16s = [(_dot_nt(q.astype(BF16), k16) * dec).astype(BF16) for q, k16, dec in zip(qs, k16s, decays)]
    qd16s = [(q * eg).astype(BF16) for q, eg in zip(qs, e_gs)]
    kd16s = [(k * jnp.exp(gc[c - 1:c, :] - gc)).astype(BF16) for k, gc in zip(ks, gcs)]
    g_tots = [jnp.exp(gc[c - 1:c, :]) for gc in gcs]

    heads = range(GDN_HEADS)
    states = [s_ref[h] for h in heads]
    for n in range(n_chunks):
        rs = slice(n * c, (n + 1) * c)
        ix = [n * GDN_HEADS + h for h in heads]
        s16s = [states[h].astype(BF16) for h in heads]
        v16s = [(uws[i][:, :GDN_D] - _dot(uws[i][:, GDN_D:].astype(BF16), s16s[h])).astype(BF16)
                for h, i in zip(heads, ix)]
        states = [states[h] * g_tots[i] + _dot_tn(kd16s[i], v16s[h]) for h, i in zip(heads, ix)]
        os_ = [_dot(qd16s[i], s16s[h]) + _dot(qk16s[i], v16s[h]) for h, i in zip(heads, ix)]
        for h in heads:
            o = os_[h]
            on = o * lax.rsqrt(jnp.mean(o * o, axis=-1, keepdims=True) + EPS) * nw
            zh = z_ref[rs, h * GDN_D:(h + 1) * GDN_D]
            o_ref[rs, h * GDN_D:(h + 1) * GDN_D] = on * (zh * _sigmoid(zh))
    for h in heads:
        s_ref[h] = states[h]


def _gdn(qkv, z, ab, conv_w8, alog_row, dtb_row, norm_w, bsz, seq, rows):
    n = seq // rows
    row = lambda b, i: (b * n + i, 0)
    const = lambda b, i: (0, 0)
    return pl.pallas_call(
        functools.partial(_gdn_kernel, rows=rows),
        grid=(bsz, n),
        in_specs=[pl.BlockSpec((rows, C_QKV), row),
                  pl.BlockSpec((rows, C_Z), row),
                  pl.BlockSpec((rows, LANES), row),
                  pl.BlockSpec((8, C_QKV), const),
                  pl.BlockSpec((1, LANES), const),
                  pl.BlockSpec((1, LANES), const),
                  pl.BlockSpec((1, GDN_D), const)],
        out_specs=pl.BlockSpec((rows, C_Z), row),
        out_shape=jax.ShapeDtypeStruct((bsz * seq, C_Z), F32),
        scratch_shapes=[pltpu.VMEM((8 + rows, C_QKV), F32),
                        pltpu.VMEM((GDN_HEADS, GDN_D, GDN_D), F32)],
        compiler_params=pltpu.CompilerParams(dimension_semantics=("arbitrary", "arbitrary"),
                                             vmem_limit_bytes=VMEM_LIMIT),
        name="gdn",
    )(qkv, z, ab, conv_w8, alog_row, dtb_row, norm_w)


D_QK_PAD = LANES
C_HEADS = MLA_HEADS * D_QK_PAD
C_V = MLA_HEADS * MLA_V
V_AUG = MLA_V + 16
C_VAUG = MLA_HEADS * V_AUG
Q_SCALE = (MLA_NOPE + MLA_ROPE) ** -0.5 * math.log2(math.e)


def _mla_proj_kernel(cq_ref, ckv_ref, kr_ref, posc_ref, posr_ref, qnw_ref, kvnw_ref, wqt_ref, wqrt_ref,
                     wk_ref, wvt_ref, freqr_ref, freqc_ref, qt_ref, k_ref, vt_ref):
    cq = cq_ref[...]
    cqn = (cq * lax.rsqrt(jnp.mean(cq * cq, axis=-1, keepdims=True) + EPS) * qnw_ref[...]).astype(BF16)
    ckv = ckv_ref[...]
    ckvn = (ckv * lax.rsqrt(jnp.mean(ckv * ckv, axis=-1, keepdims=True) + EPS)
            * kvnw_ref[...]).astype(BF16)
    ang = posc_ref[...] * freqr_ref[...]
    cs = jnp.cos(ang)
    sn = jnp.sin(ang)
    ang_t = freqc_ref[...] * posr_ref[...]
    cs_t = jnp.cos(ang_t)
    sn_t = jnp.sin(ang_t)
    qa_t = _dot_nt(wqt_ref[...], cqn)
    qb_t = _dot_nt(wqrt_ref[...], cqn)
    kn = _dot(ckvn, wk_ref[...])
    kr = kr_ref[...]
    kpe = kr[:, :LANES] * cs + kr[:, LANES:] * sn
    for h in range(MLA_HEADS):
        sl = slice(h * LANES, (h + 1) * LANES)
        qt_ref[sl, :] = ((qa_t[sl, :] * cs_t + qb_t[sl, :] * sn_t) * Q_SCALE).astype(BF16)
        k_ref[:, sl] = (kn[:, sl] + kpe).astype(BF16)
    v_t = _dot_nt(wvt_ref[...], ckvn).astype(BF16)
    ones = jnp.ones((V_AUG - MLA_V, v_t.shape[1]), BF16)
    for h in range(MLA_HEADS):
        vt_ref[0, h * V_AUG:h * V_AUG + MLA_V, :] = v_t[h * MLA_V:(h + 1) * MLA_V, :]
        vt_ref[0, h * V_AUG + MLA_V:(h + 1) * V_AUG, :] = ones


def _mla_proj(cq, ckv, kr2, posc, posr, qnw, kvnw, wqt, wqrt, wk, wvt, freqr, freqc, tm):
    t = cq.shape[0]
    row = lambda i: (i, 0)
    col = lambda i: (0, i)
    const = lambda i: (0, 0)
    return pl.pallas_call(
        _mla_proj_kernel,
        grid=(t // tm,),
        in_specs=[pl.BlockSpec((tm, MLA_Q_LORA), row),
                  pl.BlockSpec((tm, MLA_KV_LORA), row),
                  pl.BlockSpec((tm, 2 * LANES), row),
                  pl.BlockSpec((tm, 1), row),
                  pl.BlockSpec((1, tm), col),
                  pl.BlockSpec((1, MLA_Q_LORA), const),
                  pl.BlockSpec((1, MLA_KV_LORA), const),
                  pl.BlockSpec((C_HEADS, MLA_Q_LORA), const),
                  pl.BlockSpec((C_HEADS, MLA_Q_LORA), const),
                  pl.BlockSpec((MLA_KV_LORA, C_HEADS), const),
                  pl.BlockSpec((C_V, MLA_KV_LORA), const),
                  pl.BlockSpec((1, LANES), const),
                  pl.BlockSpec((LANES, 1), const)],
        out_specs=[pl.BlockSpec((C_HEADS, tm), col),
                   pl.BlockSpec((tm, C_HEADS), row),
                   pl.BlockSpec((1, C_VAUG, tm), lambda i: (i, 0, 0))],
        out_shape=[jax.ShapeDtypeStruct((C_HEADS, t), BF16),
                   jax.ShapeDtypeStruct((t, C_HEADS), BF16),
                   jax.ShapeDtypeStruct((t // tm, C_VAUG, tm), BF16)],
        compiler_params=pltpu.CompilerParams(dimension_semantics=("parallel",),
                                             vmem_limit_bytes=VMEM_LIMIT),
        name="mla_proj",
    )(cq, ckv, kr2, posc, posr, qnw, kvnw, wqt, wqrt, wk, wvt, freqr, freqc)


def _attn_kernel(qt_ref, k_ref, vt_ref, nw_ref, ot_ref, m_ref, acc_ref, *, tq):
    i = pl.program_id(1)
    m_ref[...] = jnp.full(m_ref.shape, NEG_BIG, F32)
    acc_ref[...] = jnp.zeros(acc_ref.shape, F32)
    key_chunk = lax.broadcasted_iota(jnp.int32, (tq, tq), 0) // CHUNK
    qry_chunk = lax.broadcasted_iota(jnp.int32, (tq, tq), 1) // CHUNK
    diag_mask = key_chunk <= qry_chunk

    def tile(j, masked):
        rows = pl.ds(pl.multiple_of(j * tq, tq), tq)
        scores = []
        for h in range(MLA_HEADS):
            hs = slice(h * LANES, (h + 1) * LANES)
            s_t = _dot(k_ref[rows, hs], qt_ref[hs, :])
            scores.append(jnp.where(diag_mask, s_t, NEG_BIG) if masked else s_t)
        probs, alphas = [], []
        for h in range(MLA_HEADS):
            m_old = m_ref[h]
            m_new = jnp.maximum(m_old, jnp.max(scores[h], axis=0, keepdims=True))
            m_ref[h] = m_new
            probs.append(jnp.exp2(scores[h] - m_new[0:1, :]).astype(BF16))
            alphas.append(jnp.exp2(m_old[0:1, :] - m_new[0:1, :]))
        for h in range(MLA_HEADS):
            v_aug = vt_ref[j, h * V_AUG:(h + 1) * V_AUG, :]
            acc_ref[h] = alphas[h] * acc_ref[h] + _dot(v_aug, probs[h])

    def full_tile(j, carry):
        tile(j, False)
        return carry

    lax.fori_loop(0, i, full_tile, 0)
    tile(i, True)

    outs = []
    for h in range(MLA_HEADS):
        acc = acc_ref[h]
        outs.append(acc[:MLA_V, :] / acc[MLA_V:MLA_V + 1, :])
    ssq = outs[0] * outs[0]
    for o in outs[1:]:
        ssq = ssq + o * o
    inv = lax.rsqrt(jnp.sum(ssq, axis=0, keepdims=True) / C_V + EPS)
    for h in range(MLA_HEADS):
        vs = slice(h * MLA_V, (h + 1) * MLA_V)
        ot_ref[vs, :] = outs[h] * inv * nw_ref[vs, :]


def _attention(qt, k, vt, norm_w_col, bsz, seq, tq):
    nq = seq // tq
    return pl.pallas_call(
        functools.partial(_attn_kernel, tq=tq),
        grid=(bsz, nq),
        in_specs=[pl.BlockSpec((C_HEADS, tq), lambda b, i: (0, b * nq + i)),
                  pl.BlockSpec((seq, C_HEADS), lambda b, i: (b, 0)),
                  pl.BlockSpec((nq, C_VAUG, tq), lambda b, i: (b, 0, 0)),
                  pl.BlockSpec((C_V, 1), lambda b, i: (0, 0))],
        out_specs=pl.BlockSpec((C_V, tq), lambda b, i: (0, b * nq + i)),
        out_shape=jax.ShapeDtypeStruct((C_V, bsz * seq), F32),
        scratch_shapes=[pltpu.VMEM((MLA_HEADS, 8, tq), F32),
                        pltpu.VMEM((MLA_HEADS, V_AUG, tq), F32)],
        compiler_params=pltpu.CompilerParams(dimension_semantics=("parallel", "parallel"),
                                             vmem_limit_bytes=VMEM_LIMIT),
        name="attention",
    )(qt, k, vt, norm_w_col)


def _router_kernel(x_ref, oa_ref, obt_ref, wo_ref, nw_ref, rw_ref, rb_ref,
                   x1_ref, h_ref, eidx_ref, gate_ref, rank_ref, cnt_ref, run_ref, *, tm):
    @pl.when(pl.program_id(0) == 0)
    def _():
        run_ref[...] = jnp.zeros_like(run_ref)

    half = oa_ref.shape[-1]
    y = (_dot(oa_ref[...].astype(BF16), wo_ref[0:half, :])
         + _dot_tn(obt_ref[...].astype(BF16), wo_ref[half:, :]))
    x1 = x_ref[...] + y
    x1_ref[...] = x1
    hn = x1 * lax.rsqrt(jnp.mean(x1 * x1, axis=-1, keepdims=True) + EPS) * nw_ref[...]
    h_ref[...] = hn.astype(BF16)
    lane = lax.broadcasted_iota(jnp.int32, (tm, LANES), 1)
    logits = _dot_split(_split_bf16(hn), _split_bf16(rw_ref[...])) + rb_ref[...]
    lg = jnp.where(lane < N_EXPERTS, logits, -jnp.inf)
    sels, tops = [], []
    eidx = jnp.zeros((tm, LANES), jnp.int32)
    for kk in range(TOP_K):
        m = jnp.max(lg, axis=-1, keepdims=True)
        idx = jnp.min(jnp.where(lg == m, lane, LANES), axis=-1, keepdims=True)
        sel = lane == idx
        lg = jnp.where(sel, -jnp.inf, lg)
        sels.append(sel)
        tops.append(m)
        eidx = jnp.where(lane == kk, idx, eidx)
    exps = [jnp.exp(tv - tops[0]) for tv in tops]
    den = exps[0] + exps[1] + exps[2] + exps[3]
    gates = jnp.zeros((tm, LANES), F32)
    multi = jnp.zeros((tm, LANES), F32)
    for kk in range(TOP_K):
        gates = jnp.where(lane == kk, exps[kk] / den, gates)
        multi = multi + sels[kk].astype(F32)
    ri = lax.broadcasted_iota(jnp.int32, (tm, tm), 0)
    ci = lax.broadcasted_iota(jnp.int32, (tm, tm), 1)
    before = _dot((ri > ci).astype(BF16), multi.astype(BF16)) + run_ref[...]
    rank = jnp.zeros((tm, LANES), jnp.int32)
    for kk in range(TOP_K):
        r = jnp.sum(jnp.where(sels[kk], before, 0.0), axis=-1, keepdims=True)
        rank = jnp.where(lane == kk, r.astype(jnp.int32), rank)
    run = run_ref[...] + jnp.sum(multi, axis=0, keepdims=True)
    run_ref[...] = run
    cnt_ref[...] = run
    eidx_ref[...] = eidx
    gate_ref[...] = gates
    rank_ref[...] = rank


def _router(x2, oa, obt, w_out, norm_w, rw, rb, tm):
    t, d = x2.shape
    half = oa.shape[-1]
    row = lambda i: (i, 0)
    const = lambda i: (0, 0)
    return pl.pallas_call(
        functools.partial(_router_kernel, tm=tm),
        grid=(t // tm,),
        in_specs=[pl.BlockSpec((tm, d), row),
                  pl.BlockSpec((tm, half), row),
                  pl.BlockSpec((half, tm), lambda i: (0, i)),
                  pl.BlockSpec((2 * half, d), const),
                  pl.BlockSpec((1, d), const),
                  pl.BlockSpec((d, LANES), const),
                  pl.BlockSpec((1, LANES), const)],
        out_specs=[pl.BlockSpec((tm, d), row), pl.BlockSpec((tm, d), row),
                   pl.BlockSpec((tm, LANES), row), pl.BlockSpec((tm, LANES), row),
                   pl.BlockSpec((tm, LANES), row), pl.BlockSpec((1, LANES), const)],
        out_shape=[jax.ShapeDtypeStruct((t, d), F32), jax.ShapeDtypeStruct((t, d), BF16),
                   jax.ShapeDtypeStruct((t, LANES), jnp.int32), jax.ShapeDtypeStruct((t, LANES), F32),
                   jax.ShapeDtypeStruct((t, LANES), jnp.int32), jax.ShapeDtypeStruct((1, LANES), F32)],
        scratch_shapes=[pltpu.VMEM((1, LANES), F32)],
        compiler_params=pltpu.CompilerParams(dimension_semantics=("arbitrary",),
                                             vmem_limit_bytes=VMEM_LIMIT),
        name="router",
    )(x2, oa, obt, w_out, norm_w, rw, rb)


def _dispatch_kernel(dest_ref, h_ref, buf_in_ref, buf_ref, sem, *, tm):
    del buf_in_ref

    def issue(t, carry):
        for kk in range(TOP_K):
            d = dest_ref[0, 0, t * TOP_K + kk]
            pltpu.make_async_copy(h_ref.at[t], buf_ref.at[d], sem).start()
        return carry

    lax.fori_loop(0, tm, issue, 0)
    for kk in range(TOP_K):
        pltpu.make_async_copy(h_ref, buf_ref.at[pl.ds(0, tm)], sem).wait()


def _dispatch(dest, h3, n_pad, tm):
    t, s, _ = h3.shape
    buf0 = jnp.zeros((n_pad, s, LANES), h3.dtype)
    return pl.pallas_call(
        functools.partial(_dispatch_kernel, tm=tm),
        grid=(t // tm,),
        in_specs=[pl.BlockSpec((1, 1, tm * TOP_K), lambda i: (i, 0, 0), memory_space=pltpu.SMEM),
                  pl.BlockSpec((tm, s, LANES), lambda i: (i, 0, 0)),
                  pl.BlockSpec(memory_space=pl.ANY)],
        out_specs=pl.BlockSpec(memory_space=pl.ANY),
        out_shape=jax.ShapeDtypeStruct((n_pad, s, LANES), h3.dtype),
        scratch_shapes=[pltpu.SemaphoreType.DMA(())],
        input_output_aliases={2: 0},
        compiler_params=pltpu.CompilerParams(dimension_semantics=("arbitrary",)),
        name="dispatch",
    )(dest.reshape(t // tm, 1, tm * TOP_K), h3, buf0)


def _expert_kernel(be_ref, bv_ref, bf_ref, bslot_ref, bnext_ref, x_ref, w1_hbm, b1g_ref, b1l_ref, w2_hbm,
                   b2_ref, y_ref, w1f_ref, w2f_ref, w1s_ref, w2s_ref, sem):
    i = pl.program_id(0)
    valid = bv_ref[i] == 1
    n_tiles = w1s_ref.shape[-1] // MXU_COLS
    half = MXU_COLS // 2

    def fetch(expert, slot):
        return (pltpu.make_async_copy(w1_hbm.at[expert], w1f_ref.at[slot], sem.at[0, slot]),
                pltpu.make_async_copy(w2_hbm.at[expert], w2f_ref.at[slot], sem.at[1, slot]))

    @pl.when(i == 0)
    def _():
        for cp in fetch(be_ref[0], bslot_ref[0]):
            cp.start()

    @pl.when(bf_ref[i] == 1)
    def _():
        slot = bslot_ref[i]
        nxt = bnext_ref[i]

        @pl.when(nxt >= 0)
        def _():
            for cp in fetch(nxt, 1 - slot):
                cp.start()

        for cp in fetch(be_ref[i], slot):
            cp.wait()
        r = lax.broadcasted_iota(jnp.int32, (MXU_COLS, MXU_COLS), 0)
        c = lax.broadcasted_iota(jnp.int32, (MXU_COLS, MXU_COLS), 1)
        perm = (r == jnp.where(c < half, 2 * c, 2 * (c - half) + 1)).astype(BF16)
        for tix in range(n_tiles):
            cols = slice(tix * MXU_COLS, (tix + 1) * MXU_COLS)
            w1s_ref[:, cols] = _dot(w1f_ref[slot, :, cols].astype(BF16), perm).astype(BF16)
        w2s_ref[...] = w2f_ref[slot].astype(BF16)

    @pl.when(jnp.logical_not(valid))
    def _():
        y_ref[...] = jnp.zeros_like(y_ref)

    @pl.when(valid)
    def _():
        x = x_ref[...]
        acts = []
        for tix in range(n_tiles):
            hp = _dot(x, w1s_ref[:, tix * MXU_COLS:(tix + 1) * MXU_COLS])
            feat = slice(tix * half, (tix + 1) * half)
            g = jnp.minimum(hp[:, :half] + b1g_ref[0][:, feat], SWIGLU_LIMIT)
            lin = jnp.clip(hp[:, half:] + b1l_ref[0][:, feat], -SWIGLU_LIMIT, SWIGLU_LIMIT)
            acts.append((g * _sigmoid(SWIGLU_ALPHA * g) * (lin + 1.0)).astype(BF16))
        act = jnp.concatenate(acts, axis=1)
        y_ref[...] = (_dot(act, w2s_ref[...]) + b2_ref[0]).astype(y_ref.dtype)


def _experts(tables, xs, w1, b1g, b1l, w2, b2, bm):
    n_pad, d = xs.shape
    dff2 = w1.shape[-1]
    dff = dff2 // 2
    wmap = lambda i, be, *_: (be[i], 0, 0)
    grid_spec = pltpu.PrefetchScalarGridSpec(
        num_scalar_prefetch=len(tables),
        grid=(n_pad // bm,),
        in_specs=[pl.BlockSpec((bm, d), lambda i, *_: (i, 0)),
                  pl.BlockSpec(memory_space=pl.ANY),
                  pl.BlockSpec((1, 1, dff), wmap),
                  pl.BlockSpec((1, 1, dff), wmap),
                  pl.BlockSpec(memory_space=pl.ANY),
                  pl.BlockSpec((1, 1, d), wmap)],
        out_specs=pl.BlockSpec((bm, d), lambda i, *_: (i, 0)),
        scratch_shapes=[pltpu.VMEM((2, d, dff2), F32), pltpu.VMEM((2, dff, d), F32),
                        pltpu.VMEM((d, dff2), BF16), pltpu.VMEM((dff, d), BF16),
                        pltpu.SemaphoreType.DMA((2, 2))],
    )
    return pl.pallas_call(
        _expert_kernel,
        grid_spec=grid_spec,
        out_shape=jax.ShapeDtypeStruct((n_pad, d), BF16),
        compiler_params=pltpu.CompilerParams(dimension_semantics=("arbitrary",),
                                             vmem_limit_bytes=VMEM_LIMIT),
        name="experts",
    )(*tables, xs, w1, b1g, b1l, w2, b2)


def _combine_kernel(dest_ref, gate_ref, y_ref, o_ref, gbuf, sem, *, tm):
    def issue(t, carry):
        for kk in range(TOP_K):
            d = dest_ref[0, 0, t * TOP_K + kk]
            pltpu.make_async_copy(y_ref.at[d], gbuf.at[kk * tm + t], sem).start()
        return carry

    lax.fori_loop(0, tm, issue, 0)
    for kk in range(TOP_K):
        pltpu.make_async_copy(y_ref.at[pl.ds(0, tm)], gbuf.at[pl.ds(kk * tm, tm)], sem).wait()

    def mix(t, carry):
        acc = gate_ref[0, 0, t * TOP_K] * gbuf[t].astype(F32)
        for kk in range(1, TOP_K):
            acc = acc + gate_ref[0, 0, t * TOP_K + kk] * gbuf[kk * tm + t].astype(F32)
        o_ref[t] = acc
        return carry

    lax.fori_loop(0, tm, mix, 0)


def _combine(dest, gates, y3, t, tm):
    n_pad, s, _ = y3.shape
    smem = lambda: pl.BlockSpec((1, 1, tm * TOP_K), lambda i: (i, 0, 0), memory_space=pltpu.SMEM)
    return pl.pallas_call(
        functools.partial(_combine_kernel, tm=tm),
        grid=(t // tm,),
        in_specs=[smem(), smem(), pl.BlockSpec(memory_space=pl.ANY)],
        out_specs=pl.BlockSpec((tm, s, LANES), lambda i: (i, 0, 0)),
        out_shape=jax.ShapeDtypeStruct((t, s, LANES), F32),
        scratch_shapes=[pltpu.VMEM((TOP_K * tm, s, LANES), y3.dtype), pltpu.SemaphoreType.DMA(())],
        compiler_params=pltpu.CompilerParams(dimension_semantics=("arbitrary",)),
        name="combine",
    )(dest.reshape(t // tm, 1, tm * TOP_K), gates.reshape(t // tm, 1, tm * TOP_K), y3)


def _final_kernel(x1_ref, y_ref, nw_ref, o_ref):
    x = x1_ref[...] + y_ref[...]
    o_ref[...] = x * lax.rsqrt(jnp.mean(x * x, axis=-1, keepdims=True) + EPS) * nw_ref[...]


def _final(x1, y, norm_w, tm):
    t, d = x1.shape
    row = lambda i: (i, 0)
    return pl.pallas_call(
        _final_kernel,
        grid=(t // tm,),
        in_specs=[pl.BlockSpec((tm, d), row), pl.BlockSpec((tm, d), row),
                  pl.BlockSpec((1, d), lambda i: (0, 0))],
        out_specs=pl.BlockSpec((tm, d), row),
        out_shape=jax.ShapeDtypeStruct((t, d), F32),
        compiler_params=pltpu.CompilerParams(dimension_semantics=("parallel",)),
        name="final_norm",
    )(x1, y, norm_w)


def _pad_cols(w, n):
    return jnp.pad(w, ((0, 0), (0, n - w.shape[1])))


def _rot_half(w):
    half = w.shape[-1] // 2
    return jnp.concatenate([-w[..., half:], w[..., :half]], axis=-1)


def _prep_inproj_weight(w_in):
    d = w_in.shape[0]
    hd = GDN_HEADS * GDN_D
    o = 0
    parts = {}
    for name, n in (("qkv", 3 * hd), ("z", hd), ("a", GDN_HEADS), ("b", GDN_HEADS),
                    ("cq", MLA_Q_LORA), ("ckv", MLA_KV_LORA), ("kr", MLA_ROPE)):
        parts[name] = w_in[:, o:o + n]
        o += n
    zeros = lambda n: jnp.zeros((d, n), w_in.dtype)
    kr_tile = jnp.concatenate([zeros(MLA_NOPE), parts["kr"], zeros(LANES - MLA_NOPE - MLA_ROPE)], axis=1)
    krr_tile = jnp.concatenate([zeros(MLA_NOPE), _rot_half(parts["kr"]),
                                zeros(LANES - MLA_NOPE - MLA_ROPE)], axis=1)
    ab_tile = jnp.concatenate([parts["a"], parts["b"], zeros(LANES - 2 * GDN_HEADS)], axis=1)
    return jnp.concatenate([parts["qkv"], parts["z"], parts["cq"], parts["ckv"], kr_tile, krr_tile,
                            ab_tile], axis=1).astype(BF16)


def _prep_mla_weights(w_uq, w_ukv):
    dq = MLA_NOPE + MLA_ROPE
    wq = w_uq.reshape(MLA_Q_LORA, MLA_HEADS, dq)
    wq_rot = jnp.concatenate([jnp.zeros_like(wq[..., :MLA_NOPE]), _rot_half(wq[..., MLA_NOPE:])], axis=-1)
    pad = lambda w: jnp.pad(w, ((0, 0), (0, 0), (0, LANES - w.shape[-1]))).reshape(w.shape[0], C_HEADS)
    wkv = w_ukv.reshape(MLA_KV_LORA, MLA_HEADS, MLA_NOPE + MLA_V)
    wv = wkv[..., MLA_NOPE:].reshape(MLA_KV_LORA, C_V)
    return (pad(wq).T.astype(BF16), pad(wq_rot).T.astype(BF16), pad(wkv[..., :MLA_NOPE]).astype(BF16),
            wv.T.astype(BF16))


def _rope_freq():
    half = MLA_ROPE // 2
    inv_freq = ROPE_THETA ** (-jnp.arange(half, dtype=F32) / half)
    return jnp.zeros((LANES,), F32).at[MLA_NOPE:MLA_NOPE + MLA_ROPE].set(jnp.concatenate([inv_freq, inv_freq]))


def _lane_row(v):
    return jnp.pad(v.astype(F32), (0, LANES - v.shape[0])).reshape(1, LANES)


def _routing_tables(cnt, eidx, rank, n_blocks, bm):
    e_ids = jnp.arange(N_EXPERTS, dtype=jnp.int32)
    counts = cnt[0, :N_EXPERTS].astype(jnp.int32)
    padded = (counts + bm - 1) // bm * bm
    pend = jnp.sum(jnp.where(e_ids[None, :] <= e_ids[:, None], padded[None, :], 0), axis=1)
    pstart = pend - padded
    dest = jnp.sum(jnp.where(eidx[:, :TOP_K, None] == e_ids, pstart, 0), axis=-1) + rank[:, :TOP_K]
    blk = jnp.arange(n_blocks, dtype=jnp.int32) * bm
    block_expert = jnp.minimum(jnp.sum((pend[None, :] <= blk[:, None]).astype(jnp.int32), axis=1),
                               N_EXPERTS - 1)
    block_valid = blk < pend[-1]
    prev = jnp.concatenate([jnp.full((1,), -1, jnp.int32), block_expert[:-1]])
    block_first = jnp.logical_and(block_valid, block_expert != prev)
    present = counts > 0
    later = jnp.logical_and(present[None, :], e_ids[None, :] > e_ids[:, None])
    next_present = jnp.min(jnp.where(later, e_ids[None, :], N_EXPERTS), axis=1)
    next_present = jnp.where(next_present == N_EXPERTS, -1, next_present)
    ordinal = jnp.sum(jnp.where(e_ids[None, :] < e_ids[:, None], present[None, :].astype(jnp.int32), 0), axis=1)
    pick = lambda table: jnp.sum(jnp.where(block_expert[:, None] == e_ids, table, 0), axis=1)
    tables = (block_expert, block_valid.astype(jnp.int32), block_first.astype(jnp.int32),
              pick(ordinal % 2), pick(next_present))
    return dest, tuple(t.astype(jnp.int32) for t in tables)


def kernel(x, positions, norm_mix_w, w_in, conv_w, a_log, dt_bias, gdn_norm_w, q_norm_w, w_uq, kv_norm_w,
           w_ukv, mla_out_norm_w, w_out, norm_ffn_w, router_w, router_b, w1, b1, w2, b2, norm_final_w):
    bsz, seq, d = x.shape
    t = bsz * seq
    depth = w_in.shape[0]
    tm = min(512, t)
    tr = min(256, t)
    bm = 256
    tq = min(512, seq)
    x2 = x.reshape(t, d)
    posf = positions.astype(F32)
    freq = _rope_freq()
    for l in range(depth):
        qkv, z, cq, ckv, kr2, ab = _inproj(x2, norm_mix_w[l].reshape(1, d), _prep_inproj_weight(w_in[l]), tm)
        conv_w8 = jnp.pad(conv_w[l], ((0, 8 - GDN_CONV), (0, 0)))
        o_a = _gdn(qkv, z, ab, conv_w8, _lane_row(a_log[l]), _lane_row(dt_bias[l]),
                   gdn_norm_w[l].reshape(1, GDN_D), bsz, seq, min(256, seq))
        wqt, wqrt, wk, wvt = _prep_mla_weights(w_uq[l], w_ukv[l])
        qt, k, vt = _mla_proj(cq, ckv, kr2, posf.reshape(t, 1), posf.reshape(1, t),
                              q_norm_w[l].reshape(1, -1), kv_norm_w[l].reshape(1, -1),
                              wqt, wqrt, wk, wvt, freq.reshape(1, LANES), freq.reshape(LANES, 1), tq)
        o_bt = _attention(qt, k, vt, mla_out_norm_w[l].reshape(-1, 1), bsz, seq, tq)
        rw = _pad_cols(router_w[l], LANES)
        rb = _lane_row(router_b[l])
        x1, hn, eidx, gates, rank, cnt = _router(x2, o_a, o_bt, w_out[l].astype(BF16),
                                                 norm_ffn_w[l].reshape(1, d), rw, rb, tr)
        n_pad = t * TOP_K + N_EXPERTS * bm
        dest, tables = _routing_tables(cnt, eidx, rank, n_pad // bm, bm)
        buf = _dispatch(dest, hn.reshape(t, d // LANES, LANES), n_pad, tr)
        yb = _experts(tables, buf.reshape(n_pad, d), w1[l],
                      b1[l][:, None, 0::2], b1[l][:, None, 1::2], w2[l], b2[l][:, None, :], bm)
        y3 = _combine(dest, gates[:, :TOP_K], yb.reshape(n_pad, d // LANES, LANES), t, tr)
        if l == depth - 1:
            x2 = _final(x1, y3.reshape(t, d), norm_final_w.reshape(1, d), tm)
        else:
            x2 = x1 + y3.reshape(t, d)
    return x2.reshape(bsz, seq, d)
```

```python
import functools
import math

import jax
import jax.numpy as jnp
from jax import lax
from jax.experimental import pallas as pl
from jax.experimental.pallas import tpu as pltpu
from jax.experimental.pallas import tpu_sc as plsc

F32 = jnp.float32
BF16 = jnp.bfloat16
HIGHEST = lax.Precision.HIGHEST

LANES = 128
MXU_COLS = 256
EPS = 1e-6
CHUNK = 64

GDN_HEADS = 4
GDN_D = 128
GDN_CONV = 4
MLA_HEADS = 8
MLA_NOPE = 64
MLA_ROPE = 32
MLA_V = 64
MLA_Q_LORA = 384
MLA_KV_LORA = 256
ROPE_THETA = 10000.0
N_EXPERTS = 32
TOP_K = 4
SWIGLU_LIMIT = 7.0
SWIGLU_ALPHA = 1.702

NEG_BIG = -1e30
VMEM_LIMIT = 52 * 1024 * 1024


def _sigmoid(x):
    return 1.0 / (1.0 + jnp.exp(-x))


def _softplus(x):
    return jnp.maximum(x, 0.0) + jnp.log(1.0 + jnp.exp(-jnp.abs(x)))


def _dot(a, b, precision=None):
    return jnp.dot(a, b, preferred_element_type=F32, precision=precision)


def _dot_nt(a, b, precision=None):
    return lax.dot_general(a, b, (((1,), (1,)), ((), ())), preferred_element_type=F32,
                           precision=precision)


def _dot_tn(a, b):
    return lax.dot_general(a, b, (((0,), (0,)), ((), ())), preferred_element_type=F32)


def _split_bf16(a):
    hi = a.astype(BF16)
    return hi, (a - hi.astype(F32)).astype(BF16)


def _pack_bf16_pair(lo, hi):
    lo_bits = pltpu.bitcast(lo.astype(BF16).astype(F32), jnp.int32)
    hi_bits = pltpu.bitcast(hi.astype(BF16).astype(F32), jnp.int32)
    return jnp.bitwise_or(hi_bits, lax.shift_right_logical(lo_bits, 16))


def _unpack_bf16_pair(w):
    lo = pltpu.bitcast(lax.shift_left(w, 16), F32)
    hi = pltpu.bitcast(jnp.bitwise_and(w, -65536), F32)
    return lo, hi


def _dot_split(a, b):
    return _dot(a[0], b[0]) + _dot(a[0], b[1]) + _dot(a[1], b[0])


C_QKV = 3 * GDN_HEADS * GDN_D
C_Z = GDN_HEADS * GDN_D
C_IN = C_QKV + C_Z + MLA_Q_LORA + MLA_KV_LORA + 3 * LANES


def _inproj_kernel(x_ref, nw_ref, w_ref, qkv_ref, z_ref, cq_ref, ckv_ref, kr_ref, ab_ref):
    x = x_ref[...]
    var = jnp.mean(x * x, axis=-1, keepdims=True)
    h = (x * lax.rsqrt(var + EPS) * nw_ref[...]).astype(BF16)
    p = _dot(h, w_ref[...])
    o = 0
    for ref in (qkv_ref, z_ref, cq_ref, ckv_ref, kr_ref, ab_ref):
        n = ref.shape[-1]
        ref[...] = p[:, o:o + n]
        o += n


def _inproj(x2, norm_w, w_all, tm):
    t, d = x2.shape
    widths = (C_QKV, C_Z, MLA_Q_LORA, MLA_KV_LORA, 2 * LANES, LANES)
    return pl.pallas_call(
        _inproj_kernel,
        grid=(t // tm,),
        in_specs=[pl.BlockSpec((tm, d), lambda i: (i, 0)),
                  pl.BlockSpec((1, d), lambda i: (0, 0)),
                  pl.BlockSpec((d, C_IN), lambda i: (0, 0))],
        out_specs=[pl.BlockSpec((tm, n), lambda i: (i, 0)) for n in widths],
        out_shape=[jax.ShapeDtypeStruct((t, n), F32) for n in widths],
        compiler_params=pltpu.CompilerParams(dimension_semantics=("parallel",),
                                             vmem_limit_bytes=VMEM_LIMIT),
        name="inproj",
    )(x2, norm_w, w_all)


def _gdn_kernel(qkv_ref, z_ref, ab_ref, cw_ref, alog_ref, dtb_ref, nw_ref, o_ref, ext_ref, s_ref, *, rows):
    c = CHUNK
    n_chunks = rows // c

    @pl.when(pl.program_id(1) == 0)
    def _():
        ext_ref[0:8, :] = jnp.zeros((8, C_QKV), F32)
        s_ref[...] = jnp.zeros_like(s_ref)

    x = qkv_ref[...]
    ext_ref[8:8 + rows, :] = x
    cw = cw_ref[...]
    xc = (cw[3:4] * x + cw[2:3] * ext_ref[7:7 + rows, :] + cw[1:2] * ext_ref[6:6 + rows, :]
          + cw[0:1] * ext_ref[5:5 + rows, :])
    ext_ref[0:8, :] = x[rows - 8:rows, :]
    xc = xc * _sigmoid(xc)

    ab = ab_ref[...]
    g_all = -jnp.exp(alog_ref[...]) * _softplus(ab + dtb_ref[...])
    beta_all = _sigmoid(ab)
    rr = lax.broadcasted_iota(jnp.int32, (rows, rows), 0)
    rc = lax.broadcasted_iota(jnp.int32, (rows, rows), 1)
    in_chunk_prefix = jnp.where(rr // c == rc // c, jnp.where(rr >= rc, 1.0, 0.0), 0.0)
    g_cum = _dot(in_chunk_prefix, g_all, HIGHEST)
    e8 = (lax.broadcasted_iota(jnp.int32, (8, LANES), 0)
          == lax.broadcasted_iota(jnp.int32, (8, LANES), 1)).astype(F32)
    g_row_all = _dot_nt(e8, g_cum, HIGHEST)
    ri = lax.broadcasted_iota(jnp.int32, (c, c), 0)
    ci = lax.broadcasted_iota(jnp.int32, (c, c), 1)
    causal = ri >= ci
    strict = ri > ci
    eye = (ri == ci).astype(F32)
    nw = nw_ref[...]

    chains = [(n, h) for n in range(n_chunks) for h in range(GDN_HEADS)]
    qs, ks, vs, gcs, betas, decays = [], [], [], [], [], []
    for n, h in chains:
        rs = slice(n * c, (n + 1) * c)
        q = xc[rs, h * GDN_D:(h + 1) * GDN_D]
        k = xc[rs, (GDN_HEADS + h) * GDN_D:(GDN_HEADS + h + 1) * GDN_D]
        qs.append(q * lax.rsqrt(jnp.sum(q * q, axis=-1, keepdims=True) + EPS) * (GDN_D ** -0.5))
        ks.append(k * lax.rsqrt(jnp.sum(k * k, axis=-1, keepdims=True) + EPS))
        vs.append(xc[rs, (2 * GDN_HEADS + h) * GDN_D:(2 * GDN_HEADS + h + 1) * GDN_D])
        gc = g_cum[rs, h:h + 1]
        gr = g_row_all[h:h + 1, rs]
        gcs.append(gc)
        betas.append(beta_all[rs, GDN_HEADS + h:GDN_HEADS + h + 1])
        decays.append(jnp.exp(jnp.where(causal, gc - gr, -jnp.inf)))
    kbs = [k * b for k, b in zip(ks, betas)]
    k16s = [k.astype(BF16) for k in ks]
    a_mats = [jnp.where(strict, _dot_nt(kb.astype(BF16), k16) * dec, 0.0)
              for kb, k16, dec in zip(kbs, k16s, decays)]
    t_mats = [eye - a for a in a_mats]
    pw_s = [_split_bf16(a) for a in a_mats]
    for _ in range(5):
        pw_s = [_split_bf16(_dot_split(p, p)) for p in pw_s]
        t_s = [_split_bf16(t) for t in t_mats]
        t_mats = [t + _dot_split(ts, p) for t, ts, p in zip(t_mats, t_s, pw_s)]
    e_gs = [jnp.exp(gc) for gc in gcs]
    uws = [_dot(t.astype(BF16), jnp.concatenate([v * b, kb * eg], axis=1).astype(BF16))
           for t, v, b, kb, eg in zip(t_mats, vs, betas, kbs, e_gs)]
    qk16s = [(_dot_nt(q.astype(BF16), k16) * dec).astype(BF16) for q, k16, dec in zip(qs, k16s, decays)]
    qd16s = [(q * eg).astype(BF16) for q, eg in zip(qs, e_gs)]
    kd16s = [(k * jnp.exp(gc[c - 1:c, :] - gc)).astype(BF16) for k, gc in zip(ks, gcs)]
    g_tots = [jnp.exp(gc[c - 1:c, :]) for gc in gcs]

    heads = range(GDN_HEADS)
    states = [s_ref[h] for h in heads]
    for n in range(n_chunks):
        rs = slice(n * c, (n + 1) * c)
        ix = [n * GDN_HEADS + h for h in heads]
        s16s = [states[h].astype(BF16) for h in heads]
        v16s = [(uws[i][:, :GDN_D] - _dot(uws[i][:, GDN_D:].astype(BF16), s16s[h])).astype(BF16)
                for h, i in zip(heads, ix)]
        states = [states[h] * g_tots[i] + _dot_tn(kd16s[i], v16s[h]) for h, i in zip(heads, ix)]
        os_ = [_dot(qd16s[i], s16s[h]) + _dot(qk16s[i], v16s[h]) for h, i in zip(heads, ix)]
        for h in heads:
            o = os_[h]
            on = o * lax.rsqrt(jnp.mean(o * o, axis=-1, keepdims=True) + EPS) * nw
            zh = z_ref[rs, h * GDN_D:(h + 1) * GDN_D]
            o_ref[rs, h * GDN_D:(h + 1) * GDN_D] = on * (zh * _sigmoid(zh))
    for h in heads:
        s_ref[h] = states[h]


def _gdn(qkv, z, ab, conv_w8, alog_row, dtb_row, norm_w, bsz, seq, rows):
    n = seq // rows
    row = lambda b, i: (b * n + i, 0)
    const = lambda b, i: (0, 0)
    return pl.pallas_call(
        functools.partial(_gdn_kernel, rows=rows),
        grid=(bsz, n),
        in_specs=[pl.BlockSpec((rows, C_QKV), row),
                  pl.BlockSpec((rows, C_Z), row),
                  pl.BlockSpec((rows, LANES), row),
                  pl.BlockSpec((8, C_QKV), const),
                  pl.BlockSpec((1, LANES), const),
                  pl.BlockSpec((1, LANES), const),
                  pl.BlockSpec((1, GDN_D), const)],
        out_specs=pl.BlockSpec((rows, C_Z), row),
        out_shape=jax.ShapeDtypeStruct((bsz * seq, C_Z), F32),
        scratch_shapes=[pltpu.VMEM((8 + rows, C_QKV), F32),
                        pltpu.VMEM((GDN_HEADS, GDN_D, GDN_D), F32)],
        compiler_params=pltpu.CompilerParams(dimension_semantics=("arbitrary", "arbitrary"),
                                             vmem_limit_bytes=VMEM_LIMIT),
        name="gdn",
    )(qkv, z, ab, conv_w8, alog_row, dtb_row, norm_w)


D_QK_PAD = LANES
C_HEADS = MLA_HEADS * D_QK_PAD
C_V = MLA_HEADS * MLA_V
V_AUG = MLA_V + 16
C_VAUG = MLA_HEADS * V_AUG
Q_SCALE = (MLA_NOPE + MLA_ROPE) ** -0.5 * math.log2(math.e)


def _mla_proj_kernel(cq_ref, ckv_ref, kr_ref, posc_ref, posr_ref, qnw_ref, kvnw_ref, wqt_ref, wqrt_ref,
                     wk_ref, wvt_ref, freqr_ref, freqc_ref, qt_ref, k_ref, vt_ref):
    cq = cq_ref[...]
    cqn = (cq * lax.rsqrt(jnp.mean(cq * cq, axis=-1, keepdims=True) + EPS) * qnw_ref[...]).astype(BF16)
    ckv = ckv_ref[...]
    ckvn = (ckv * lax.rsqrt(jnp.mean(ckv * ckv, axis=-1, keepdims=True) + EPS)
            * kvnw_ref[...]).astype(BF16)
    ang = posc_ref[...] * freqr_ref[...]
    cs = jnp.cos(ang)
    sn = jnp.sin(ang)
    ang_t = freqc_ref[...] * posr_ref[...]
    cs_t = jnp.cos(ang_t)
    sn_t = jnp.sin(ang_t)
    qa_t = _dot_nt(wqt_ref[...], cqn)
    qb_t = _dot_nt(wqrt_ref[...], cqn)
    kn = _dot(ckvn, wk_ref[...])
    kr = kr_ref[...]
    kpe = kr[:, :LANES] * cs + kr[:, LANES:] * sn
    for h in range(MLA_HEADS):
        sl = slice(h * LANES, (h + 1) * LANES)
        qt_ref[sl, :] = ((qa_t[sl, :] * cs_t + qb_t[sl, :] * sn_t) * Q_SCALE).astype(BF16)
        k_ref[:, sl] = (kn[:, sl] + kpe).astype(BF16)
    v_t = _dot_nt(wvt_ref[...], ckvn).astype(BF16)
    ones = jnp.ones((V_AUG - MLA_V, v_t.shape[1]), BF16)
    for h in range(MLA_HEADS):
        vt_ref[0, h * V_AUG:h * V_AUG + MLA_V, :] = v_t[h * MLA_V:(h + 1) * MLA_V, :]
        vt_ref[0, h * V_AUG + MLA_V:(h + 1) * V_AUG, :] = ones


def _mla_proj(cq, ckv, kr2, posc, posr, qnw, kvnw, wqt, wqrt, wk, wvt, freqr, freqc, tm):
    t = cq.shape[0]
    row = lambda i: (i, 0)
    col = lambda i: (0, i)
    const = lambda i: (0, 0)
    return pl.pallas_call(
        _mla_proj_kernel,
        grid=(t // tm,),
        in_specs=[pl.BlockSpec((tm, MLA_Q_LORA), row),
                  pl.BlockSpec((tm, MLA_KV_LORA), row),
                  pl.BlockSpec((tm, 2 * LANES), row),
                  pl.BlockSpec((tm, 1), row),
                  pl.BlockSpec((1, tm), col),
                  pl.BlockSpec((1, MLA_Q_LORA), const),
                  pl.BlockSpec((1, MLA_KV_LORA), const),
                  pl.BlockSpec((C_HEADS, MLA_Q_LORA), const),
                  pl.BlockSpec((C_HEADS, MLA_Q_LORA), const),
                  pl.BlockSpec((MLA_KV_LORA, C_HEADS), const),
                  pl.BlockSpec((C_V, MLA_KV_LORA), const),
                  pl.BlockSpec((1, LANES), const),
                  pl.BlockSpec((LANES, 1), const)],
        out_specs=[pl.BlockSpec((C_HEADS, tm), col),
                   pl.BlockSpec((tm, C_HEADS), row),
                   pl.BlockSpec((1, C_VAUG, tm), lambda i: (i, 0, 0))],
        out_shape=[jax.ShapeDtypeStruct((C_HEADS, t), BF16),
                   jax.ShapeDtypeStruct((t, C_HEADS), BF16),
                   jax.ShapeDtypeStruct((t // tm, C_VAUG, tm), BF16)],
        compiler_params=pltpu.CompilerParams(dimension_semantics=("parallel",),
                                             vmem_limit_bytes=VMEM_LIMIT),
        name="mla_proj",
    )(cq, ckv, kr2, posc, posr, qnw, kvnw, wqt, wqrt, wk, wvt, freqr, freqc)


def _attn_kernel(qt_ref, k_ref, vt_ref, nw_ref, ot_ref, m_ref, acc_ref, *, tq):
    i = pl.program_id(1)
    m_ref[...] = jnp.full(m_ref.shape, NEG_BIG, F32)
    acc_ref[...] = jnp.zeros(acc_ref.shape, F32)
    key_chunk = lax.broadcasted_iota(jnp.int32, (tq, tq), 0) // CHUNK
    qry_chunk = lax.broadcasted_iota(jnp.int32, (tq, tq), 1) // CHUNK
    diag_mask = key_chunk <= qry_chunk

    def tile(j, masked):
        rows = pl.ds(pl.multiple_of(j * tq, tq), tq)
        scores = []
        for h in range(MLA_HEADS):
            hs = slice(h * LANES, (h + 1) * LANES)
            s_t = _dot(k_ref[rows, hs], qt_ref[hs, :])
            scores.append(jnp.where(diag_mask, s_t, NEG_BIG) if masked else s_t)
        probs, alphas = [], []
        for h in range(MLA_HEADS):
            m_old = m_ref[h]
            m_new = jnp.maximum(m_old, jnp.max(scores[h], axis=0, keepdims=True))
            m_ref[h] = m_new
            probs.append(jnp.exp2(scores[h] - m_new[0:1, :]).astype(BF16))
            alphas.append(jnp.exp2(m_old[0:1, :] - m_new[0:1, :]))
        for h in range(MLA_HEADS):
            v_aug = vt_ref[j, h * V_AUG:(h + 1) * V_AUG, :]
            acc_ref[h] = alphas[h] * acc_ref[h] + _dot(v_aug, probs[h])

    def full_tile(j, carry):
        tile(j, False)
        return carry

    lax.fori_loop(0, i, full_tile, 0)
    tile(i, True)

    outs = []
    for h in range(MLA_HEADS):
        acc = acc_ref[h]
        outs.append(acc[:MLA_V, :] / acc[MLA_V:MLA_V + 1, :])
    ssq = outs[0] * outs[0]
    for o in outs[1:]:
        ssq = ssq + o * o
    inv = lax.rsqrt(jnp.sum(ssq, axis=0, keepdims=True) / C_V + EPS)
    for h in range(MLA_HEADS):
        vs = slice(h * MLA_V, (h + 1) * MLA_V)
        ot_ref[vs, :] = outs[h] * inv * nw_ref[vs, :]


def _attention(qt, k, vt, norm_w_col, bsz, seq, tq):
    nq = seq // tq
    return pl.pallas_call(
        functools.partial(_attn_kernel, tq=tq),
        grid=(bsz, nq),
        in_specs=[pl.BlockSpec((C_HEADS, tq), lambda b, i: (0, b * nq + i)),
                  pl.BlockSpec((seq, C_HEADS), lambda b, i: (b, 0)),
                  pl.BlockSpec((nq, C_VAUG, tq), lambda b, i: (b, 0, 0)),
                  pl.BlockSpec((C_V, 1), lambda b, i: (0, 0))],
        out_specs=pl.BlockSpec((C_V, tq), lambda b, i: (0, b * nq + i)),
        out_shape=jax.ShapeDtypeStruct((C_V, bsz * seq), F32),
        scratch_shapes=[pltpu.VMEM((MLA_HEADS, 8, tq), F32),
                        pltpu.VMEM((MLA_HEADS, V_AUG, tq), F32)],
        compiler_params=pltpu.CompilerParams(dimension_semantics=("parallel", "parallel"),
                                             vmem_limit_bytes=VMEM_LIMIT),
        name="attention",
    )(qt, k, vt, norm_w_col)


def _router_kernel(x_ref, oa_ref, obt_ref, wo_ref, nw_ref, rw_ref, rb_ref,
                   x1_ref, h_ref, eidx_ref, gate_ref, rank_ref, cnt_ref, run_ref, *, tm):
    @pl.when(pl.program_id(0) == 0)
    def _():
        run_ref[...] = jnp.zeros_like(run_ref)

    half = oa_ref.shape[-1]
    y = (_dot(oa_ref[...].astype(BF16), wo_ref[0:half, :])
         + _dot_tn(obt_ref[...].astype(BF16), wo_ref[half:, :]))
    x1 = x_ref[...] + y
    x1_ref[...] = x1
    hn = x1 * lax.rsqrt(jnp.mean(x1 * x1, axis=-1, keepdims=True) + EPS) * nw_ref[...]
    d_half = hn.shape[-1] // 2
    h_ref[...] = _pack_bf16_pair(hn[:, :d_half], hn[:, d_half:])
    lane = lax.broadcasted_iota(jnp.int32, (tm, LANES), 1)
    logits = _dot_split(_split_bf16(hn), _split_bf16(rw_ref[...])) + rb_ref[...]
    lg = jnp.where(lane < N_EXPERTS, logits, -jnp.inf)
    sels, tops = [], []
    eidx = jnp.zeros((tm, LANES), jnp.int32)
    for kk in range(TOP_K):
        m = jnp.max(lg, axis=-1, keepdims=True)
        idx = jnp.min(jnp.where(lg == m, lane, LANES), axis=-1, keepdims=True)
        sel = lane == idx
        lg = jnp.where(sel, -jnp.inf, lg)
        sels.append(sel)
        tops.append(m)
        eidx = jnp.where(lane == kk, idx, eidx)
    exps = [jnp.exp(tv - tops[0]) for tv in tops]
    den = exps[0] + exps[1] + exps[2] + exps[3]
    gates = jnp.zeros((tm, LANES), F32)
    multi = jnp.zeros((tm, LANES), F32)
    for kk in range(TOP_K):
        gates = jnp.where(lane == kk, exps[kk] / den, gates)
        multi = multi + sels[kk].astype(F32)
    ri = lax.broadcasted_iota(jnp.int32, (tm, tm), 0)
    ci = lax.broadcasted_iota(jnp.int32, (tm, tm), 1)
    before = _dot((ri > ci).astype(BF16), multi.astype(BF16)) + run_ref[...]
    rank = jnp.zeros((tm, LANES), jnp.int32)
    for kk in range(TOP_K):
        r = jnp.sum(jnp.where(sels[kk], before, 0.0), axis=-1, keepdims=True)
        rank = jnp.where(lane == kk, r.astype(jnp.int32), rank)
    run = run_ref[...] + jnp.sum(multi, axis=0, keepdims=True)
    run_ref[...] = run
    cnt_ref[...] = run
    eidx_ref[...] = eidx
    gate_ref[...] = gates
    rank_ref[...] = rank


def _router(x2, oa, obt, w_out, norm_w, rw, rb, tm):
    t, d = x2.shape
    half = oa.shape[-1]
    row = lambda i: (i, 0)
    const = lambda i: (0, 0)
    return pl.pallas_call(
        functools.partial(_router_kernel, tm=tm),
        grid=(t // tm,),
        in_specs=[pl.BlockSpec((tm, d), row),
                  pl.BlockSpec((tm, half), row),
                  pl.BlockSpec((half, tm), lambda i: (0, i)),
                  pl.BlockSpec((2 * half, d), const),
                  pl.BlockSpec((1, d), const),
                  pl.BlockSpec((d, LANES), const),
                  pl.BlockSpec((1, LANES), const)],
        out_specs=[pl.BlockSpec((tm, d), row), pl.BlockSpec((tm, d // 2), row),
                   pl.BlockSpec((tm, LANES), row), pl.BlockSpec((tm, LANES), row),
                   pl.BlockSpec((tm, LANES), row), pl.BlockSpec((1, LANES), const)],
        out_shape=[jax.ShapeDtypeStruct((t, d), F32), jax.ShapeDtypeStruct((t, d // 2), jnp.int32),
                   jax.ShapeDtypeStruct((t, LANES), jnp.int32), jax.ShapeDtypeStruct((t, LANES), F32),
                   jax.ShapeDtypeStruct((t, LANES), jnp.int32), jax.ShapeDtypeStruct((1, LANES), F32)],
        scratch_shapes=[pltpu.VMEM((1, LANES), F32)],
        compiler_params=pltpu.CompilerParams(dimension_semantics=("arbitrary",),
                                             vmem_limit_bytes=VMEM_LIMIT),
        name="router",
    )(x2, oa, obt, w_out, norm_w, rw, rb)


SC_ROWS = 64


def _sc_mesh():
    return plsc.VectorSubcoreMesh(core_axis_name="c", subcore_axis_name="s")


def _dispatch(dest_flat, h3, n_pad):
    t, s, _ = h3.shape
    mesh = _sc_mesh()
    n_workers = mesh.num_cores * mesh.num_subcores
    per_worker = t // n_workers
    assert per_worker % SC_ROWS == 0

    @functools.partial(
        pl.kernel, mesh=mesh, out_type=jax.ShapeDtypeStruct((n_pad, s, LANES), h3.dtype),
        scratch_types=[pltpu.VMEM((TOP_K, SC_ROWS), jnp.int32), pltpu.VMEM((SC_ROWS, s, LANES), h3.dtype),
                       pltpu.SemaphoreType.DMA],
        name="sc_dispatch")
    def scatter(h_hbm, dest_hbm, out_hbm, idx_v, rows_v, sem):
        base = (lax.axis_index("s") * mesh.num_cores + lax.axis_index("c")) * per_worker

        @pl.loop(0, per_worker // SC_ROWS)
        def _(i):
            off = pl.multiple_of(base + i * SC_ROWS, SC_ROWS)
            pltpu.sync_copy(h_hbm.at[pl.ds(off, SC_ROWS)], rows_v)
            for kk in range(TOP_K):
                pltpu.sync_copy(dest_hbm.at[pl.ds(kk * t + off, SC_ROWS)], idx_v.at[kk])
            copies = [pltpu.async_copy(rows_v, out_hbm.at[idx_v.at[kk]], sem) for kk in range(TOP_K)]
            for cp in copies:
                cp.wait()

    return scatter(h3, dest_flat)


def _expert_kernel(be_ref, bv_ref, bf_ref, bslot_ref, bnext_ref, x_ref, w1_hbm, b1g_ref, b1l_ref, w2_hbm,
                   b2_ref, y_ref, w1f_ref, w2f_ref, w1s_ref, w2s_ref, sem):
    i = pl.program_id(0)
    n_rows = bv_ref[i]
    valid = n_rows > 0
    n_tiles = w1s_ref.shape[-1] // MXU_COLS
    half = MXU_COLS // 2

    def fetch(expert, slot):
        return (pltpu.make_async_copy(w1_hbm.at[expert], w1f_ref.at[slot], sem.at[0, slot]),
                pltpu.make_async_copy(w2_hbm.at[expert], w2f_ref.at[slot], sem.at[1, slot]))

    @pl.when(i == 0)
    def _():
        for cp in fetch(be_ref[0], bslot_ref[0]):
            cp.start()

    @pl.when(bf_ref[i] == 1)
    def _():
        slot = bslot_ref[i]
        nxt = bnext_ref[i]

        @pl.when(nxt >= 0)
        def _():
            for cp in fetch(nxt, 1 - slot):
                cp.start()

        for cp in fetch(be_ref[i], slot):
            cp.wait()
        r = lax.broadcasted_iota(jnp.int32, (MXU_COLS, MXU_COLS), 0)
        c = lax.broadcasted_iota(jnp.int32, (MXU_COLS, MXU_COLS), 1)
        perm = (r == jnp.where(c < half, 2 * c, 2 * (c - half) + 1)).astype(BF16)
        for tix in range(n_tiles):
            cols = slice(tix * MXU_COLS, (tix + 1) * MXU_COLS)
            w1s_ref[:, cols] = _dot(w1f_ref[slot, :, cols].astype(BF16), perm).astype(BF16)
        w2s_ref[...] = w2f_ref[slot].astype(BF16)

    @pl.when(jnp.logical_not(valid))
    def _():
        y_ref[...] = jnp.zeros_like(y_ref)

    @pl.when(valid)
    def _():
        row = lax.broadcasted_iota(jnp.int32, x_ref.shape, 0)
        x_lo, x_hi = _unpack_bf16_pair(jnp.where(row < n_rows, x_ref[...], 0))
        x = jnp.concatenate([x_lo.astype(BF16), x_hi.astype(BF16)], axis=1)
        acts = []
        for tix in range(n_tiles):
            hp = _dot(x, w1s_ref[:, tix * MXU_COLS:(tix + 1) * MXU_COLS])
            feat = slice(tix * half, (tix + 1) * half)
            g = jnp.minimum(hp[:, :half] + b1g_ref[0][:, feat], SWIGLU_LIMIT)
            lin = jnp.clip(hp[:, half:] + b1l_ref[0][:, feat], -SWIGLU_LIMIT, SWIGLU_LIMIT)
            acts.append((g * _sigmoid(SWIGLU_ALPHA * g) * (lin + 1.0)).astype(BF16))
        act = jnp.concatenate(acts, axis=1)
        y = _dot(act, w2s_ref[...]) + b2_ref[0]
        d_half = y.shape[-1] // 2
        y_ref[...] = _pack_bf16_pair(y[:, :d_half], y[:, d_half:])


def _experts(tables, xs, w1, b1g, b1l, w2, b2, bm):
    n_pad, d_half = xs.shape
    d = 2 * d_half
    dff2 = w1.shape[-1]
    dff = dff2 // 2
    wmap = lambda i, be, *_: (be[i], 0, 0)
    grid_spec = pltpu.PrefetchScalarGridSpec(
        num_scalar_prefetch=len(tables),
        grid=(n_pad // bm,),
        in_specs=[pl.BlockSpec((bm, d_half), lambda i, *_: (i, 0)),
                  pl.BlockSpec(memory_space=pl.ANY),
                  pl.BlockSpec((1, 1, dff), wmap),
                  pl.BlockSpec((1, 1, dff), wmap),
                  pl.BlockSpec(memory_space=pl.ANY),
                  pl.BlockSpec((1, 1, d), wmap)],
        out_specs=pl.BlockSpec((bm, d_half), lambda i, *_: (i, 0)),
        scratch_shapes=[pltpu.VMEM((2, d, dff2), F32), pltpu.VMEM((2, dff, d), F32),
                        pltpu.VMEM((d, dff2), BF16), pltpu.VMEM((dff, d), BF16),
                        pltpu.SemaphoreType.DMA((2, 2))],
    )
    return pl.pallas_call(
        _expert_kernel,
        grid_spec=grid_spec,
        out_shape=jax.ShapeDtypeStruct((n_pad, d_half), jnp.int32),
        compiler_params=pltpu.CompilerParams(dimension_semantics=("arbitrary",),
                                             vmem_limit_bytes=VMEM_LIMIT),
        name="experts",
    )(*tables, xs, w1, b1g, b1l, w2, b2)


def _gather_rows(dest_flat, y3):
    n_out = dest_flat.shape[0]
    _, s, _ = y3.shape
    mesh = _sc_mesh()
    n_workers = mesh.num_cores * mesh.num_subcores
    per_worker = n_out // n_workers
    assert per_worker % SC_ROWS == 0

    @functools.partial(
        pl.kernel, mesh=mesh, out_type=jax.ShapeDtypeStruct((n_out, s, LANES), y3.dtype),
        scratch_types=[pltpu.VMEM((SC_ROWS,), jnp.int32), pltpu.VMEM((SC_ROWS, s, LANES), y3.dtype),
                       pltpu.SemaphoreType.DMA],
        name="sc_gather")
    def gather(y_hbm, dest_hbm, out_hbm, idx_v, rows_v, sem):
        base = (lax.axis_index("s") * mesh.num_cores + lax.axis_index("c")) * per_worker

        @pl.loop(0, per_worker // SC_ROWS)
        def _(i):
            off = pl.multiple_of(base + i * SC_ROWS, SC_ROWS)
            pltpu.sync_copy(dest_hbm.at[pl.ds(off, SC_ROWS)], idx_v)
            pltpu.async_copy(y_hbm.at[idx_v], rows_v, sem).wait()
            pltpu.sync_copy(rows_v, out_hbm.at[pl.ds(off, SC_ROWS)])

    return gather(y3, dest_flat)


def _mix_kernel(gate_ref, y4_ref, o_ref, *, tm):
    s = y4_ref.shape[2]

    def mix(tok, carry):
        acc_lo, acc_hi = _unpack_bf16_pair(y4_ref[0, tok])
        g = gate_ref[0, 0, tok * TOP_K]
        acc_lo, acc_hi = g * acc_lo, g * acc_hi
        for kk in range(1, TOP_K):
            lo, hi = _unpack_bf16_pair(y4_ref[kk, tok])
            g = gate_ref[0, 0, tok * TOP_K + kk]
            acc_lo, acc_hi = acc_lo + g * lo, acc_hi + g * hi
        o_ref[tok, 0:s, :] = acc_lo
        o_ref[tok, s:2 * s, :] = acc_hi
        return carry

    lax.fori_loop(0, tm, mix, 0, unroll=8)


def _mix(gates, y4, tm):
    _, t, s, _ = y4.shape
    return pl.pallas_call(
        functools.partial(_mix_kernel, tm=tm),
        grid=(t // tm,),
        in_specs=[pl.BlockSpec((1, 1, tm * TOP_K), lambda i: (i, 0, 0), memory_space=pltpu.SMEM),
                  pl.BlockSpec((TOP_K, tm, s, LANES), lambda i: (0, i, 0, 0))],
        out_specs=pl.BlockSpec((tm, 2 * s, LANES), lambda i: (i, 0, 0)),
        out_shape=jax.ShapeDtypeStruct((t, 2 * s, LANES), F32),
        compiler_params=pltpu.CompilerParams(dimension_semantics=("parallel",)),
        name="mix",
    )(gates.reshape(t // tm, 1, tm * TOP_K), y4)


def _final_kernel(x1_ref, y_ref, nw_ref, o_ref):
    x = x1_ref[...] + y_ref[...]
    o_ref[...] = x * lax.rsqrt(jnp.mean(x * x, axis=-1, keepdims=True) + EPS) * nw_ref[...]


def _final(x1, y, norm_w, tm):
    t, d = x1.shape
    row = lambda i: (i, 0)
    return pl.pallas_call(
        _final_kernel,
        grid=(t // tm,),
        in_specs=[pl.BlockSpec((tm, d), row), pl.BlockSpec((tm, d), row),
                  pl.BlockSpec((1, d), lambda i: (0, 0))],
        out_specs=pl.BlockSpec((tm, d), row),
        out_shape=jax.ShapeDtypeStruct((t, d), F32),
        compiler_params=pltpu.CompilerParams(dimension_semantics=("parallel",)),
        name="final_norm",
    )(x1, y, norm_w)


def _pad_cols(w, n):
    return jnp.pad(w, ((0, 0), (0, n - w.shape[1])))


def _rot_half(w):
    half = w.shape[-1] // 2
    return jnp.concatenate([-w[..., half:], w[..., :half]], axis=-1)


def _prep_inproj_weight(w_in):
    d = w_in.shape[0]
    hd = GDN_HEADS * GDN_D
    o = 0
    parts = {}
    for name, n in (("qkv", 3 * hd), ("z", hd), ("a", GDN_HEADS), ("b", GDN_HEADS),
                    ("cq", MLA_Q_LORA), ("ckv", MLA_KV_LORA), ("kr", MLA_ROPE)):
        parts[name] = w_in[:, o:o + n]
        o += n
    zeros = lambda n: jnp.zeros((d, n), w_in.dtype)
    kr_tile = jnp.concatenate([zeros(MLA_NOPE), parts["kr"], zeros(LANES - MLA_NOPE - MLA_ROPE)], axis=1)
    krr_tile = jnp.concatenate([zeros(MLA_NOPE), _rot_half(parts["kr"]),
                                zeros(LANES - MLA_NOPE - MLA_ROPE)], axis=1)
    ab_tile = jnp.concatenate([parts["a"], parts["b"], zeros(LANES - 2 * GDN_HEADS)], axis=1)
    return jnp.concatenate([parts["qkv"], parts["z"], parts["cq"], parts["ckv"], kr_tile, krr_tile,
                            ab_tile], axis=1).astype(BF16)


def _prep_mla_weights(w_uq, w_ukv):
    dq = MLA_NOPE + MLA_ROPE
    wq = w_uq.reshape(MLA_Q_LORA, MLA_HEADS, dq)
    wq_rot = jnp.concatenate([jnp.zeros_like(wq[..., :MLA_NOPE]), _rot_half(wq[..., MLA_NOPE:])], axis=-1)
    pad = lambda w: jnp.pad(w, ((0, 0), (0, 0), (0, LANES - w.shape[-1]))).reshape(w.shape[0], C_HEADS)
    wkv = w_ukv.reshape(MLA_KV_LORA, MLA_HEADS, MLA_NOPE + MLA_V)
    wv = wkv[..., MLA_NOPE:].reshape(MLA_KV_LORA, C_V)
    return (pad(wq).T.astype(BF16), pad(wq_rot).T.astype(BF16), pad(wkv[..., :MLA_NOPE]).astype(BF16),
            wv.T.astype(BF16))


def _rope_freq():
    half = MLA_ROPE // 2
    inv_freq = ROPE_THETA ** (-jnp.arange(half, dtype=F32) / half)
    return jnp.zeros((LANES,), F32).at[MLA_NOPE:MLA_NOPE + MLA_ROPE].set(jnp.concatenate([inv_freq, inv_freq]))


def _lane_row(v):
    return jnp.pad(v.astype(F32), (0, LANES - v.shape[0])).reshape(1, LANES)


def _routing_tables(cnt, eidx, rank, n_blocks, bm):
    e_ids = jnp.arange(N_EXPERTS, dtype=jnp.int32)
    counts = cnt[0, :N_EXPERTS].astype(jnp.int32)
    padded = (counts + bm - 1) // bm * bm
    pend = jnp.sum(jnp.where(e_ids[None, :] <= e_ids[:, None], padded[None, :], 0), axis=1)
    pstart = pend - padded
    dest = jnp.sum(jnp.where(eidx[:, :TOP_K, None] == e_ids, pstart, 0), axis=-1) + rank[:, :TOP_K]
    blk = jnp.arange(n_blocks, dtype=jnp.int32) * bm
    block_expert = jnp.minimum(jnp.sum((pend[None, :] <= blk[:, None]).astype(jnp.int32), axis=1),
                               N_EXPERTS - 1)
    block_valid = blk < pend[-1]
    prev = jnp.concatenate([jnp.full((1,), -1, jnp.int32), block_expert[:-1]])
    block_first = jnp.logical_and(block_valid, block_expert != prev)
    present = counts > 0
    later = jnp.logical_and(present[None, :], e_ids[None, :] > e_ids[:, None])
    next_present = jnp.min(jnp.where(later, e_ids[None, :], N_EXPERTS), axis=1)
    next_present = jnp.where(next_present == N_EXPERTS, -1, next_present)
    ordinal = jnp.sum(jnp.where(e_ids[None, :] < e_ids[:, None], present[None, :].astype(jnp.int32), 0), axis=1)
    pick = lambda table: jnp.sum(jnp.where(block_expert[:, None] == e_ids, table, 0), axis=1)
    block_rows = jnp.where(block_valid, jnp.clip(pick(pstart + counts) - blk, 0, bm), 0)
    tables = (block_expert, block_rows, block_first.astype(jnp.int32), pick(ordinal % 2), pick(next_present))
    return dest, tuple(t.astype(jnp.int32) for t in tables)


def kernel(x, positions, norm_mix_w, w_in, conv_w, a_log, dt_bias, gdn_norm_w, q_norm_w, w_uq, kv_norm_w,
           w_ukv, mla_out_norm_w, w_out, norm_ffn_w, router_w, router_b, w1, b1, w2, b2, norm_final_w):
    bsz, seq, d = x.shape
    t = bsz * seq
    depth = w_in.shape[0]
    tm = min(512, t)
    tr = min(256, t)
    bm = 256
    tq = min(512, seq)
    x2 = x.reshape(t, d)
    posf = positions.astype(F32)
    freq = _rope_freq()
    for l in range(depth):
        qkv, z, cq, ckv, kr2, ab = _inproj(x2, norm_mix_w[l].reshape(1, d), _prep_inproj_weight(w_in[l]), tm)
        conv_w8 = jnp.pad(conv_w[l], ((0, 8 - GDN_CONV), (0, 0)))
        o_a = _gdn(qkv, z, ab, conv_w8, _lane_row(a_log[l]), _lane_row(dt_bias[l]),
                   gdn_norm_w[l].reshape(1, GDN_D), bsz, seq, min(256, seq))
        wqt, wqrt, wk, wvt = _prep_mla_weights(w_uq[l], w_ukv[l])
        qt, k, vt = _mla_proj(cq, ckv, kr2, posf.reshape(t, 1), posf.reshape(1, t),
                              q_norm_w[l].reshape(1, -1), kv_norm_w[l].reshape(1, -1),
                              wqt, wqrt, wk, wvt, freq.reshape(1, LANES), freq.reshape(LANES, 1), tq)
        o_bt = _attention(qt, k, vt, mla_out_norm_w[l].reshape(-1, 1), bsz, seq, tq)
        rw = _pad_cols(router_w[l], LANES)
        rb = _lane_row(router_b[l])
        x1, hn, eidx, gates, rank, cnt = _router(x2, o_a, o_bt, w_out[l].astype(BF16),
                                                 norm_ffn_w[l].reshape(1, d), rw, rb, tr)
        n_pad = t * TOP_K + N_EXPERTS * bm
        dest, tables = _routing_tables(cnt, eidx, rank, n_pad // bm, bm)
        dest_flat = dest.T.reshape(TOP_K * t)
        s_words = d // 2 // LANES
        buf = _dispatch(dest_flat, hn.reshape(t, s_words, LANES), n_pad)
        yb = _experts(tables, buf.reshape(n_pad, d // 2), w1[l],
                      b1[l][:, None, 0::2], b1[l][:, None, 1::2], w2[l], b2[l][:, None, :], bm)
        y4 = _gather_rows(dest_flat, yb.reshape(n_pad, s_words, LANES))
        y3 = _mix(gates[:, :TOP_K], y4.reshape(TOP_K, t, s_words, LANES), tr)
        if l == depth - 1:
            x2 = _final(x1, y3.reshape(t, d), norm_final_w.reshape(1, d), tm)
        else:
            x2 = x1 + y3.reshape(t, d)
    return x2.reshape(bsz, seq, d)
```

```python
import functools
import math

import jax
import jax.numpy as jnp
from jax import lax
from jax.experimental import pallas as pl
from jax.experimental.pallas import tpu as pltpu
from jax.experimental.pallas import tpu_sc as plsc

F32 = jnp.float32
BF16 = jnp.bfloat16
HIGHEST = lax.Precision.HIGHEST

LANES = 128
MXU_COLS = 256
EPS = 1e-6
CHUNK = 64

GDN_HEADS = 4
GDN_D = 128
GDN_CONV = 4
MLA_HEADS = 8
MLA_NOPE = 64
MLA_ROPE = 32
MLA_V = 64
MLA_Q_LORA = 384
MLA_KV_LORA = 256
ROPE_THETA = 10000.0
N_EXPERTS = 32
TOP_K = 4
SWIGLU_LIMIT = 7.0
SWIGLU_ALPHA = 1.702

NEG_BIG = -1e30
VMEM_LIMIT = 52 * 1024 * 1024


def _sigmoid(x):
    return 1.0 / (1.0 + jnp.exp(-x))


def _softplus(x):
    return jnp.maximum(x, 0.0) + jnp.log(1.0 + jnp.exp(-jnp.abs(x)))


def _dot(a, b, precision=None):
    return jnp.dot(a, b, preferred_element_type=F32, precision=precision)


def _dot_nt(a, b, precision=None):
    return lax.dot_general(a, b, (((1,), (1,)), ((), ())), preferred_element_type=F32,
                           precision=precision)


def _dot_tn(a, b):
    return lax.dot_general(a, b, (((0,), (0,)), ((), ())), preferred_element_type=F32)


def _split_bf16(a):
    hi = a.astype(BF16)
    return hi, (a - hi.astype(F32)).astype(BF16)


def _pack_bf16_pair(lo, hi):
    lo_bits = pltpu.bitcast(lo.astype(BF16).astype(F32), jnp.int32)
    hi_bits = pltpu.bitcast(hi.astype(BF16).astype(F32), jnp.int32)
    return jnp.bitwise_or(hi_bits, lax.shift_right_logical(lo_bits, 16))


def _unpack_bf16_pair(w):
    lo = pltpu.bitcast(lax.shift_left(w, 16), F32)
    hi = pltpu.bitcast(jnp.bitwise_and(w, -65536), F32)
    return lo, hi


def _load_packed_rows(ref, lead, m):
    s = ref.shape[-2] // m
    return jnp.concatenate([ref[(*lead, pl.ds(j, m, stride=s), slice(None))] for j in range(s)], axis=1)


def _store_packed_rows(ref, words):
    m = words.shape[0]
    s = ref.shape[0] // m
    for j in range(s):
        ref[pl.ds(j, m, stride=s), :] = words[:, j * LANES:(j + 1) * LANES]


def _dot_split(a, b):
    return _dot(a[0], b[0]) + _dot(a[0], b[1]) + _dot(a[1], b[0])


C_QKV = 3 * GDN_HEADS * GDN_D
C_Z = GDN_HEADS * GDN_D
C_IN = C_QKV + C_Z + MLA_Q_LORA + MLA_KV_LORA + 3 * LANES


def _inproj_kernel(x_ref, nw_ref, w_ref, qkv_ref, z_ref, cq_ref, ckv_ref, kr_ref, ab_ref):
    x = x_ref[...]
    var = jnp.mean(x * x, axis=-1, keepdims=True)
    h = (x * lax.rsqrt(var + EPS) * nw_ref[...]).astype(BF16)
    p = _dot(h, w_ref[...])
    o = 0
    for ref in (qkv_ref, z_ref, cq_ref, ckv_ref, kr_ref, ab_ref):
        n = ref.shape[-1]
        ref[...] = p[:, o:o + n]
        o += n


def _inproj(x2, norm_w, w_all, tm):
    t, d = x2.shape
    widths = (C_QKV, C_Z, MLA_Q_LORA, MLA_KV_LORA, 2 * LANES, LANES)
    return pl.pallas_call(
        _inproj_kernel,
        grid=(t // tm,),
        in_specs=[pl.BlockSpec((tm, d), lambda i: (i, 0)),
                  pl.BlockSpec((1, d), lambda i: (0, 0)),
                  pl.BlockSpec((d, C_IN), lambda i: (0, 0))],
        out_specs=[pl.BlockSpec((tm, n), lambda i: (i, 0)) for n in widths],
        out_shape=[jax.ShapeDtypeStruct((t, n), F32) for n in widths],
        compiler_params=pltpu.CompilerParams(dimension_semantics=("parallel",),
                                             vmem_limit_bytes=VMEM_LIMIT),
        name="inproj",
    )(x2, norm_w, w_all)


def _gdn_kernel(qkv_ref, z_ref, ab_ref, cw_ref, alog_ref, dtb_ref, nw_ref, o_ref, ext_ref, s_ref, *, rows):
    c = CHUNK
    n_chunks = rows // c

    @pl.when(pl.program_id(1) == 0)
    def _():
        ext_ref[0:8, :] = jnp.zeros((8, C_QKV), F32)
        s_ref[...] = jnp.zeros_like(s_ref)

    x = qkv_ref[...]
    ext_ref[8:8 + rows, :] = x
    cw = cw_ref[...]
    xc = (cw[3:4] * x + cw[2:3] * ext_ref[7:7 + rows, :] + cw[1:2] * ext_ref[6:6 + rows, :]
          + cw[0:1] * ext_ref[5:5 + rows, :])
    ext_ref[0:8, :] = x[rows - 8:rows, :]
    xc = xc * _sigmoid(xc)

    ab = ab_ref[...]
    g_all = -jnp.exp(alog_ref[...]) * _softplus(ab + dtb_ref[...])
    beta_all = _sigmoid(ab)
    rr = lax.broadcasted_iota(jnp.int32, (rows, rows), 0)
    rc = lax.broadcasted_iota(jnp.int32, (rows, rows), 1)
    in_chunk_prefix = jnp.where(rr // c == rc // c, jnp.where(rr >= rc, 1.0, 0.0), 0.0)
    g_cum = _dot(in_chunk_prefix, g_all, HIGHEST)
    e8 = (lax.broadcasted_iota(jnp.int32, (8, LANES), 0)
          == lax.broadcasted_iota(jnp.int32, (8, LANES), 1)).astype(F32)
    g_row_all = _dot_nt(e8, g_cum, HIGHEST)
    ri = lax.broadcasted_iota(jnp.int32, (c, c), 0)
    ci = lax.broadcasted_iota(jnp.int32, (c, c), 1)
    causal = ri >= ci
    strict = ri > ci
    eye = (ri == ci).astype(F32)
    nw = nw_ref[...]

    chains = [(n, h) for n in range(n_chunks) for h in range(GDN_HEADS)]
    qs, ks, vs, gcs, betas, decays = [], [], [], [], [], []
    for n, h in chains:
        rs = slice(n * c, (n + 1) * c)
        q = xc[rs, h * GDN_D:(h + 1) * GDN_D]
        k = xc[rs, (GDN_HEADS + h) * GDN_D:(GDN_HEADS + h + 1) * GDN_D]
        qs.append(q * lax.rsqrt(jnp.sum(q * q, axis=-1, keepdims=True) + EPS) * (GDN_D ** -0.5))
        ks.append(k * lax.rsqrt(jnp.sum(k * k, axis=-1, keepdims=True) + EPS))
        vs.append(xc[rs, (2 * GDN_HEADS + h) * GDN_D:(2 * GDN_HEADS + h + 1) * GDN_D])
        gc = g_cum[rs, h:h + 1]
        gr = g_row_all[h:h + 1, rs]
        gcs.append(gc)
        betas.append(beta_all[rs, GDN_HEADS + h:GDN_HEADS + h + 1])
        decays.append(jnp.exp(jnp.where(causal, gc - gr, -jnp.inf)))
    kbs = [k * b for k, b in zip(ks, betas)]
    k16s = [k.astype(BF16) for k in ks]
    a_mats = [jnp.where(strict, _dot_nt(kb.astype(BF16), k16) * dec, 0.0)
              for kb, k16, dec in zip(kbs, k16s, decays)]
    t_mats = [eye - a for a in a_mats]
    pw_s = [_split_bf16(a) for a in a_mats]
    for _ in range(5):
        pw_s = [_split_bf16(_dot_split(p, p)) for p in pw_s]
        t_s = [_split_bf16(t) for t in t_mats]
        t_mats = [t + _dot_split(ts, p) for t, ts, p in zip(t_mats, t_s, pw_s)]
    e_gs = [jnp.exp(gc) for gc in gcs]
    uws = [_dot(t.astype(BF16), jnp.concatenate([v * b, kb * eg], axis=1).astype(BF16))
           for t, v, b, kb, eg in zip(t_mats, vs, betas, kbs, e_gs)]
    qk16s = [(_dot_nt(q.astype(BF16), k16) * dec).astype(BF16) for q, k16, dec in zip(qs, k16s, decays)]
    qd16s = [(q * eg).astype(BF16) for q, eg in zip(qs, e_gs)]
    kd16s = [(k * jnp.exp(gc[c - 1:c, :] - gc)).astype(BF16) for k, gc in zip(ks, gcs)]
    g_tots = [jnp.exp(gc[c - 1:c, :]) for gc in gcs]

    heads = range(GDN_HEADS)
    states = [s_ref[h] for h in heads]
    for n in range(n_chunks):
        rs = slice(n * c, (n + 1) * c)
        ix = [n * GDN_HEADS + h for h in heads]
        s16s = [states[h].astype(BF16) for h in heads]
        v16s = [(uws[i][:, :GDN_D] - _dot(uws[i][:, GDN_D:].astype(BF16), s16s[h])).astype(BF16)
                for h, i in zip(heads, ix)]
        states = [states[h] * g_tots[i] + _dot_tn(kd16s[i], v16s[h]) for h, i in zip(heads, ix)]
        os_ = [_dot(qd16s[i], s16s[h]) + _dot(qk16s[i], v16s[h]) for h, i in zip(heads, ix)]
        for h in heads:
            o = os_[h]
            on = o * lax.rsqrt(jnp.mean(o * o, axis=-1, keepdims=True) + EPS) * nw
            zh = z_ref[rs, h * GDN_D:(h + 1) * GDN_D]
            o_ref[rs, h * GDN_D:(h + 1) * GDN_D] = on * (zh * _sigmoid(zh))
    for h in heads:
        s_ref[h] = states[h]


def _gdn(qkv, z, ab, conv_w8, alog_row, dtb_row, norm_w, bsz, seq, rows):
    n = seq // rows
    row = lambda b, i: (b * n + i, 0)
    const = lambda b, i: (0, 0)
    return pl.pallas_call(
        functools.partial(_gdn_kernel, rows=rows),
        grid=(bsz, n),
        in_specs=[pl.BlockSpec((rows, C_QKV), row),
                  pl.BlockSpec((rows, C_Z), row),
                  pl.BlockSpec((rows, LANES), row),
                  pl.BlockSpec((8, C_QKV), const),
                  pl.BlockSpec((1, LANES), const),
                  pl.BlockSpec((1, LANES), const),
                  pl.BlockSpec((1, GDN_D), const)],
        out_specs=pl.BlockSpec((rows, C_Z), row),
        out_shape=jax.ShapeDtypeStruct((bsz * seq, C_Z), F32),
        scratch_shapes=[pltpu.VMEM((8 + rows, C_QKV), F32),
                        pltpu.VMEM((GDN_HEADS, GDN_D, GDN_D), F32)],
        compiler_params=pltpu.CompilerParams(dimension_semantics=("arbitrary", "arbitrary"),
                                             vmem_limit_bytes=VMEM_LIMIT),
        name="gdn",
    )(qkv, z, ab, conv_w8, alog_row, dtb_row, norm_w)


D_QK_PAD = LANES
C_HEADS = MLA_HEADS * D_QK_PAD
C_V = MLA_HEADS * MLA_V
V_AUG = MLA_V + 16
C_VAUG = MLA_HEADS * V_AUG
Q_SCALE = (MLA_NOPE + MLA_ROPE) ** -0.5 * math.log2(math.e)


def _mla_proj_kernel(cq_ref, ckv_ref, kr_ref, posc_ref, posr_ref, qnw_ref, kvnw_ref, wqt_ref, wqrt_ref,
                     wk_ref, wvt_ref, freqr_ref, freqc_ref, qt_ref, k_ref, vt_ref):
    cq = cq_ref[...]
    cqn = (cq * lax.rsqrt(jnp.mean(cq * cq, axis=-1, keepdims=True) + EPS) * qnw_ref[...]).astype(BF16)
    ckv = ckv_ref[...]
    ckvn = (ckv * lax.rsqrt(jnp.mean(ckv * ckv, axis=-1, keepdims=True) + EPS)
            * kvnw_ref[...]).astype(BF16)
    ang = posc_ref[...] * freqr_ref[...]
    cs = jnp.cos(ang)
    sn = jnp.sin(ang)
    ang_t = freqc_ref[...] * posr_ref[...]
    cs_t = jnp.cos(ang_t)
    sn_t = jnp.sin(ang_t)
    qa_t = _dot_nt(wqt_ref[...], cqn)
    qb_t = _dot_nt(wqrt_ref[...], cqn)
    kn = _dot(ckvn, wk_ref[...])
    kr = kr_ref[...]
    kpe = kr[:, :LANES] * cs + kr[:, LANES:] * sn
    for h in range(MLA_HEADS):
        sl = slice(h * LANES, (h + 1) * LANES)
        qt_ref[sl, :] = ((qa_t[sl, :] * cs_t + qb_t[sl, :] * sn_t) * Q_SCALE).astype(BF16)
        k_ref[:, sl] = (kn[:, sl] + kpe).astype(BF16)
    v_t = _dot_nt(wvt_ref[...], ckvn).astype(BF16)
    ones = jnp.ones((V_AUG - MLA_V, v_t.shape[1]), BF16)
    for h in range(MLA_HEADS):
        vt_ref[0, h * V_AUG:h * V_AUG + MLA_V, :] = v_t[h * MLA_V:(h + 1) * MLA_V, :]
        vt_ref[0, h * V_AUG + MLA_V:(h + 1) * V_AUG, :] = ones


def _mla_proj(cq, ckv, kr2, posc, posr, qnw, kvnw, wqt, wqrt, wk, wvt, freqr, freqc, tm):
    t = cq.shape[0]
    row = lambda i: (i, 0)
    col = lambda i: (0, i)
    const = lambda i: (0, 0)
    return pl.pallas_call(
        _mla_proj_kernel,
        grid=(t // tm,),
        in_specs=[pl.BlockSpec((tm, MLA_Q_LORA), row),
                  pl.BlockSpec((tm, MLA_KV_LORA), row),
                  pl.BlockSpec((tm, 2 * LANES), row),
                  pl.BlockSpec((tm, 1), row),
                  pl.BlockSpec((1, tm), col),
                  pl.BlockSpec((1, MLA_Q_LORA), const),
                  pl.BlockSpec((1, MLA_KV_LORA), const),
                  pl.BlockSpec((C_HEADS, MLA_Q_LORA), const),
                  pl.BlockSpec((C_HEADS, MLA_Q_LORA), const),
                  pl.BlockSpec((MLA_KV_LORA, C_HEADS), const),
                  pl.BlockSpec((C_V, MLA_KV_LORA), const),
                  pl.BlockSpec((1, LANES), const),
                  pl.BlockSpec((LANES, 1), const)],
        out_specs=[pl.BlockSpec((C_HEADS, tm), col),
                   pl.BlockSpec((tm, C_HEADS), row),
                   pl.BlockSpec((1, C_VAUG, tm), lambda i: (i, 0, 0))],
        out_shape=[jax.ShapeDtypeStruct((C_HEADS, t), BF16),
                   jax.ShapeDtypeStruct((t, C_HEADS), BF16),
                   jax.ShapeDtypeStruct((t // tm, C_VAUG, tm), BF16)],
        compiler_params=pltpu.CompilerParams(dimension_semantics=("parallel",),
                                             vmem_limit_bytes=VMEM_LIMIT),
        name="mla_proj",
    )(cq, ckv, kr2, posc, posr, qnw, kvnw, wqt, wqrt, wk, wvt, freqr, freqc)


def _attn_kernel(qt_ref, k_ref, vt_ref, nw_ref, ot_ref, m_ref, acc_ref, *, tq):
    i = pl.program_id(1)
    m_ref[...] = jnp.full(m_ref.shape, NEG_BIG, F32)
    acc_ref[...] = jnp.zeros(acc_ref.shape, F32)
    key_chunk = lax.broadcasted_iota(jnp.int32, (tq, tq), 0) // CHUNK
    qry_chunk = lax.broadcasted_iota(jnp.int32, (tq, tq), 1) // CHUNK
    diag_mask = key_chunk <= qry_chunk

    def tile(j, masked):
        rows = pl.ds(pl.multiple_of(j * tq, tq), tq)
        scores = []
        for h in range(MLA_HEADS):
            hs = slice(h * LANES, (h + 1) * LANES)
            s_t = _dot(k_ref[rows, hs], qt_ref[hs, :])
            scores.append(jnp.where(diag_mask, s_t, NEG_BIG) if masked else s_t)
        probs, alphas = [], []
        for h in range(MLA_HEADS):
            m_old = m_ref[h]
            m_new = jnp.maximum(m_old, jnp.max(scores[h], axis=0, keepdims=True))
            m_ref[h] = m_new
            probs.append(jnp.exp2(scores[h] - m_new[0:1, :]).astype(BF16))
            alphas.append(jnp.exp2(m_old[0:1, :] - m_new[0:1, :]))
        for h in range(MLA_HEADS):
            v_aug = vt_ref[j, h * V_AUG:(h + 1) * V_AUG, :]
            acc_ref[h] = alphas[h] * acc_ref[h] + _dot(v_aug, probs[h])

    def full_tile(j, carry):
        tile(j, False)
        return carry

    lax.fori_loop(0, i, full_tile, 0)
    tile(i, True)

    outs = []
    for h in range(MLA_HEADS):
        acc = acc_ref[h]
        outs.append(acc[:MLA_V, :] / acc[MLA_V:MLA_V + 1, :])
    ssq = outs[0] * outs[0]
    for o in outs[1:]:
        ssq = ssq + o * o
    inv = lax.rsqrt(jnp.sum(ssq, axis=0, keepdims=True) / C_V + EPS)
    for h in range(MLA_HEADS):
        vs = slice(h * MLA_V, (h + 1) * MLA_V)
        ot_ref[vs, :] = outs[h] * inv * nw_ref[vs, :]


def _attention(qt, k, vt, norm_w_col, bsz, seq, tq):
    nq = seq // tq
    return pl.pallas_call(
        functools.partial(_attn_kernel, tq=tq),
        grid=(bsz, nq),
        in_specs=[pl.BlockSpec((C_HEADS, tq), lambda b, i: (0, b * nq + i)),
                  pl.BlockSpec((seq, C_HEADS), lambda b, i: (b, 0)),
                  pl.BlockSpec((nq, C_VAUG, tq), lambda b, i: (b, 0, 0)),
                  pl.BlockSpec((C_V, 1), lambda b, i: (0, 0))],
        out_specs=pl.BlockSpec((C_V, tq), lambda b, i: (0, b * nq + i)),
        out_shape=jax.ShapeDtypeStruct((C_V, bsz * seq), F32),
        scratch_shapes=[pltpu.VMEM((MLA_HEADS, 8, tq), F32),
                        pltpu.VMEM((MLA_HEADS, V_AUG, tq), F32)],
        compiler_params=pltpu.CompilerParams(dimension_semantics=("parallel", "parallel"),
                                             vmem_limit_bytes=VMEM_LIMIT),
        name="attention",
    )(qt, k, vt, norm_w_col)


def _router_kernel(x_ref, oa_ref, obt_ref, wo_ref, nw_ref, rw_ref, rb_ref,
                   x1_ref, h_ref, eidx_ref, gate_ref, rank_ref, cnt_ref, run_ref, *, tm):
    @pl.when(pl.program_id(0) == 0)
    def _():
        run_ref[...] = jnp.zeros_like(run_ref)

    half = oa_ref.shape[-1]
    y = (_dot(oa_ref[...].astype(BF16), wo_ref[0:half, :])
         + _dot_tn(obt_ref[...].astype(BF16), wo_ref[half:, :]))
    x1 = x_ref[...] + y
    x1_ref[...] = x1
    hn = x1 * lax.rsqrt(jnp.mean(x1 * x1, axis=-1, keepdims=True) + EPS) * nw_ref[...]
    d_half = hn.shape[-1] // 2
    _store_packed_rows(h_ref, _pack_bf16_pair(hn[:, :d_half], hn[:, d_half:]))
    lane = lax.broadcasted_iota(jnp.int32, (tm, LANES), 1)
    logits = _dot_split(_split_bf16(hn), _split_bf16(rw_ref[...])) + rb_ref[...]
    lg = jnp.where(lane < N_EXPERTS, logits, -jnp.inf)
    sels, tops = [], []
    eidx = jnp.zeros((tm, LANES), jnp.int32)
    for kk in range(TOP_K):
        m = jnp.max(lg, axis=-1, keepdims=True)
        idx = jnp.min(jnp.where(lg == m, lane, LANES), axis=-1, keepdims=True)
        sel = lane == idx
        lg = jnp.where(sel, -jnp.inf, lg)
        sels.append(sel)
        tops.append(m)
        eidx = jnp.where(lane == kk, idx, eidx)
    exps = [jnp.exp(tv - tops[0]) for tv in tops]
    den = exps[0] + exps[1] + exps[2] + exps[3]
    gates = jnp.zeros((tm, LANES), F32)
    multi = jnp.zeros((tm, LANES), F32)
    for kk in range(TOP_K):
        gates = jnp.where(lane == kk, exps[kk] / den, gates)
        multi = multi + sels[kk].astype(F32)
    ri = lax.broadcasted_iota(jnp.int32, (tm, tm), 0)
    ci = lax.broadcasted_iota(jnp.int32, (tm, tm), 1)
    before = _dot((ri > ci).astype(BF16), multi.astype(BF16)) + run_ref[...]
    rank = jnp.zeros((tm, LANES), jnp.int32)
    for kk in range(TOP_K):
        r = jnp.sum(jnp.where(sels[kk], before, 0.0), axis=-1, keepdims=True)
        rank = jnp.where(lane == kk, r.astype(jnp.int32), rank)
    run = run_ref[...] + jnp.sum(multi, axis=0, keepdims=True)
    run_ref[...] = run
    cnt_ref[...] = run
    eidx_ref[...] = eidx
    gate_ref[...] = gates
    rank_ref[...] = rank


def _router(x2, oa, obt, w_out, norm_w, rw, rb, tm):
    t, d = x2.shape
    half = oa.shape[-1]
    s_words = d // 2 // LANES
    row = lambda i: (i, 0)
    const = lambda i: (0, 0)
    return pl.pallas_call(
        functools.partial(_router_kernel, tm=tm),
        grid=(t // tm,),
        in_specs=[pl.BlockSpec((tm, d), row),
                  pl.BlockSpec((tm, half), row),
                  pl.BlockSpec((half, tm), lambda i: (0, i)),
                  pl.BlockSpec((2 * half, d), const),
                  pl.BlockSpec((1, d), const),
                  pl.BlockSpec((d, LANES), const),
                  pl.BlockSpec((1, LANES), const)],
        out_specs=[pl.BlockSpec((tm, d), row), pl.BlockSpec((tm * s_words, LANES), row),
                   pl.BlockSpec((tm, LANES), row), pl.BlockSpec((tm, LANES), row),
                   pl.BlockSpec((tm, LANES), row), pl.BlockSpec((1, LANES), const)],
        out_shape=[jax.ShapeDtypeStruct((t, d), F32), jax.ShapeDtypeStruct((t * s_words, LANES), jnp.int32),
                   jax.ShapeDtypeStruct((t, LANES), jnp.int32), jax.ShapeDtypeStruct((t, LANES), F32),
                   jax.ShapeDtypeStruct((t, LANES), jnp.int32), jax.ShapeDtypeStruct((1, LANES), F32)],
        scratch_shapes=[pltpu.VMEM((1, LANES), F32)],
        compiler_params=pltpu.CompilerParams(dimension_semantics=("arbitrary",),
                                             vmem_limit_bytes=VMEM_LIMIT),
        name="router",
    )(x2, oa, obt, w_out, norm_w, rw, rb)


SC_ROWS = 64


def _sc_mesh():
    return plsc.VectorSubcoreMesh(core_axis_name="c", subcore_axis_name="s")


def _dispatch(dest_flat, h3, n_pad):
    t, s, _ = h3.shape
    mesh = _sc_mesh()
    n_workers = mesh.num_cores * mesh.num_subcores
    per_worker = t // n_workers
    assert per_worker % SC_ROWS == 0

    @functools.partial(
        pl.kernel, mesh=mesh, out_type=jax.ShapeDtypeStruct((n_pad, s, LANES), h3.dtype),
        scratch_types=[pltpu.VMEM((TOP_K, SC_ROWS), jnp.int32), pltpu.VMEM((SC_ROWS, s, LANES), h3.dtype),
                       pltpu.SemaphoreType.DMA],
        name="sc_dispatch")
    def scatter(h_hbm, dest_hbm, out_hbm, idx_v, rows_v, sem):
        base = (lax.axis_index("s") * mesh.num_cores + lax.axis_index("c")) * per_worker

        @pl.loop(0, per_worker // SC_ROWS)
        def _(i):
            off = pl.multiple_of(base + i * SC_ROWS, SC_ROWS)
            pltpu.sync_copy(h_hbm.at[pl.ds(off, SC_ROWS)], rows_v)
            for kk in range(TOP_K):
                pltpu.sync_copy(dest_hbm.at[pl.ds(kk * t + off, SC_ROWS)], idx_v.at[kk])
            copies = [pltpu.async_copy(rows_v, out_hbm.at[idx_v.at[kk]], sem) for kk in range(TOP_K)]
            for cp in copies:
                cp.wait()

    return scatter(h3, dest_flat)


def _expert_kernel(be_ref, bv_ref, bf_ref, bslot_ref, bnext_ref, x_ref, w1_hbm, b1g_ref, b1l_ref, w2_hbm,
                   b2_ref, y_ref, w1f_ref, w2f_ref, w1s_ref, w2s_ref, sem, *, bm):
    i = pl.program_id(0)
    n_rows = bv_ref[i]
    valid = n_rows > 0
    n_tiles = w1s_ref.shape[-1] // MXU_COLS
    half = MXU_COLS // 2

    def fetch(expert, slot):
        return (pltpu.make_async_copy(w1_hbm.at[expert], w1f_ref.at[slot], sem.at[0, slot]),
                pltpu.make_async_copy(w2_hbm.at[expert], w2f_ref.at[slot], sem.at[1, slot]))

    @pl.when(i == 0)
    def _():
        for cp in fetch(be_ref[0], bslot_ref[0]):
            cp.start()

    @pl.when(bf_ref[i] == 1)
    def _():
        slot = bslot_ref[i]
        nxt = bnext_ref[i]

        @pl.when(nxt >= 0)
        def _():
            for cp in fetch(nxt, 1 - slot):
                cp.start()

        for cp in fetch(be_ref[i], slot):
            cp.wait()
        r = lax.broadcasted_iota(jnp.int32, (MXU_COLS, MXU_COLS), 0)
        c = lax.broadcasted_iota(jnp.int32, (MXU_COLS, MXU_COLS), 1)
        perm = (r == jnp.where(c < half, 2 * c, 2 * (c - half) + 1)).astype(BF16)
        for tix in range(n_tiles):
            cols = slice(tix * MXU_COLS, (tix + 1) * MXU_COLS)
            w1s_ref[:, cols] = _dot(w1f_ref[slot, :, cols].astype(BF16), perm).astype(BF16)
        w2s_ref[...] = w2f_ref[slot].astype(BF16)

    @pl.when(jnp.logical_not(valid))
    def _():
        y_ref[...] = jnp.zeros_like(y_ref)

    @pl.when(valid)
    def _():
        words = _load_packed_rows(x_ref, (), bm)
        row = lax.broadcasted_iota(jnp.int32, words.shape, 0)
        x_lo, x_hi = _unpack_bf16_pair(jnp.where(row < n_rows, words, 0))
        x = jnp.concatenate([x_lo.astype(BF16), x_hi.astype(BF16)], axis=1)
        acts = []
        for tix in range(n_tiles):
            hp = _dot(x, w1s_ref[:, tix * MXU_COLS:(tix + 1) * MXU_COLS])
            feat = slice(tix * half, (tix + 1) * half)
            g = jnp.minimum(hp[:, :half] + b1g_ref[0][:, feat], SWIGLU_LIMIT)
            lin = jnp.clip(hp[:, half:] + b1l_ref[0][:, feat], -SWIGLU_LIMIT, SWIGLU_LIMIT)
            acts.append((g * _sigmoid(SWIGLU_ALPHA * g) * (lin + 1.0)).astype(BF16))
        act = jnp.concatenate(acts, axis=1)
        y = _dot(act, w2s_ref[...]) + b2_ref[0]
        d_half = y.shape[-1] // 2
        _store_packed_rows(y_ref, _pack_bf16_pair(y[:, :d_half], y[:, d_half:]))


def _experts(tables, xs, w1, b1g, b1l, w2, b2, bm, d):
    s_words = d // 2 // LANES
    n_pad = xs.shape[0] // s_words
    dff2 = w1.shape[-1]
    dff = dff2 // 2
    wmap = lambda i, be, *_: (be[i], 0, 0)
    grid_spec = pltpu.PrefetchScalarGridSpec(
        num_scalar_prefetch=len(tables),
        grid=(n_pad // bm,),
        in_specs=[pl.BlockSpec((bm * s_words, LANES), lambda i, *_: (i, 0)),
                  pl.BlockSpec(memory_space=pl.ANY),
                  pl.BlockSpec((1, 1, dff), wmap),
                  pl.BlockSpec((1, 1, dff), wmap),
                  pl.BlockSpec(memory_space=pl.ANY),
                  pl.BlockSpec((1, 1, d), wmap)],
        out_specs=pl.BlockSpec((bm * s_words, LANES), lambda i, *_: (i, 0)),
        scratch_shapes=[pltpu.VMEM((2, d, dff2), F32), pltpu.VMEM((2, dff, d), F32),
                        pltpu.VMEM((d, dff2), BF16), pltpu.VMEM((dff, d), BF16),
                        pltpu.SemaphoreType.DMA((2, 2))],
    )
    return pl.pallas_call(
        functools.partial(_expert_kernel, bm=bm),
        grid_spec=grid_spec,
        out_shape=jax.ShapeDtypeStruct((n_pad * s_words, LANES), jnp.int32),
        compiler_params=pltpu.CompilerParams(dimension_semantics=("arbitrary",),
                                             vmem_limit_bytes=VMEM_LIMIT),
        name="experts",
    )(*tables, xs, w1, b1g, b1l, w2, b2)


def _gather_rows(dest_flat, y3):
    n_out = dest_flat.shape[0]
    _, s, _ = y3.shape
    mesh = _sc_mesh()
    n_workers = mesh.num_cores * mesh.num_subcores
    per_worker = n_out // n_workers
    assert per_worker % SC_ROWS == 0

    @functools.partial(
        pl.kernel, mesh=mesh, out_type=jax.ShapeDtypeStruct((n_out, s, LANES), y3.dtype),
        scratch_types=[pltpu.VMEM((SC_ROWS,), jnp.int32), pltpu.VMEM((SC_ROWS, s, LANES), y3.dtype),
                       pltpu.SemaphoreType.DMA],
        name="sc_gather")
    def gather(y_hbm, dest_hbm, out_hbm, idx_v, rows_v, sem):
        base = (lax.axis_index("s") * mesh.num_cores + lax.axis_index("c")) * per_worker

        @pl.loop(0, per_worker // SC_ROWS)
        def _(i):
            off = pl.multiple_of(base + i * SC_ROWS, SC_ROWS)
            pltpu.sync_copy(dest_hbm.at[pl.ds(off, SC_ROWS)], idx_v)
            pltpu.async_copy(y_hbm.at[idx_v], rows_v, sem).wait()
            pltpu.sync_copy(rows_v, out_hbm.at[pl.ds(off, SC_ROWS)])

    return gather(y3, dest_flat)


def _combine_kernel(x1_ref, gate_ref, y4_ref, nw_ref, o_ref, *, tm, normalize):
    x = x1_ref[...]
    gates = gate_ref[...]
    for kk in range(TOP_K):
        lo, hi = _unpack_bf16_pair(_load_packed_rows(y4_ref, (kk,), tm))
        x = x + gates[:, kk:kk + 1] * jnp.concatenate([lo, hi], axis=1)
    if normalize:
        x = x * lax.rsqrt(jnp.mean(x * x, axis=-1, keepdims=True) + EPS) * nw_ref[...]
    o_ref[...] = x


def _combine(x1, gates, y4, norm_w, tm, normalize):
    t, d = x1.shape
    s_words = d // 2 // LANES
    row = lambda i: (i, 0)
    return pl.pallas_call(
        functools.partial(_combine_kernel, tm=tm, normalize=normalize),
        grid=(t // tm,),
        in_specs=[pl.BlockSpec((tm, d), row), pl.BlockSpec((tm, LANES), row),
                  pl.BlockSpec((TOP_K, tm * s_words, LANES), lambda i: (0, i, 0)),
                  pl.BlockSpec((1, d), lambda i: (0, 0))],
        out_specs=pl.BlockSpec((tm, d), row),
        out_shape=jax.ShapeDtypeStruct((t, d), F32),
        compiler_params=pltpu.CompilerParams(dimension_semantics=("parallel",)),
        name="combine",
    )(x1, gates, y4, norm_w)


def _pad_cols(w, n):
    return jnp.pad(w, ((0, 0), (0, n - w.shape[1])))


def _rot_half(w):
    half = w.shape[-1] // 2
    return jnp.concatenate([-w[..., half:], w[..., :half]], axis=-1)


def _prep_inproj_weight(w_in):
    d = w_in.shape[0]
    hd = GDN_HEADS * GDN_D
    o = 0
    parts = {}
    for name, n in (("qkv", 3 * hd), ("z", hd), ("a", GDN_HEADS), ("b", GDN_HEADS),
                    ("cq", MLA_Q_LORA), ("ckv", MLA_KV_LORA), ("kr", MLA_ROPE)):
        parts[name] = w_in[:, o:o + n]
        o += n
    zeros = lambda n: jnp.zeros((d, n), w_in.dtype)
    kr_tile = jnp.concatenate([zeros(MLA_NOPE), parts["kr"], zeros(LANES - MLA_NOPE - MLA_ROPE)], axis=1)
    krr_tile = jnp.concatenate([zeros(MLA_NOPE), _rot_half(parts["kr"]),
                                zeros(LANES - MLA_NOPE - MLA_ROPE)], axis=1)
    ab_tile = jnp.concatenate([parts["a"], parts["b"], zeros(LANES - 2 * GDN_HEADS)], axis=1)
    return jnp.concatenate([parts["qkv"], parts["z"], parts["cq"], parts["ckv"], kr_tile, krr_tile,
                            ab_tile], axis=1).astype(BF16)


def _prep_mla_weights(w_uq, w_ukv):
    dq = MLA_NOPE + MLA_ROPE
    wq = w_uq.reshape(MLA_Q_LORA, MLA_HEADS, dq)
    wq_rot = jnp.concatenate([jnp.zeros_like(wq[..., :MLA_NOPE]), _rot_half(wq[..., MLA_NOPE:])], axis=-1)
    pad = lambda w: jnp.pad(w, ((0, 0), (0, 0), (0, LANES - w.shape[-1]))).reshape(w.shape[0], C_HEADS)
    wkv = w_ukv.reshape(MLA_KV_LORA, MLA_HEADS, MLA_NOPE + MLA_V)
    wv = wkv[..., MLA_NOPE:].reshape(MLA_KV_LORA, C_V)
    return (pad(wq).T.astype(BF16), pad(wq_rot).T.astype(BF16), pad(wkv[..., :MLA_NOPE]).astype(BF16),
            wv.T.astype(BF16))


def _rope_freq():
    half = MLA_ROPE // 2
    inv_freq = ROPE_THETA ** (-jnp.arange(half, dtype=F32) / half)
    return jnp.zeros((LANES,), F32).at[MLA_NOPE:MLA_NOPE + MLA_ROPE].set(jnp.concatenate([inv_freq, inv_freq]))


def _lane_row(v):
    return jnp.pad(v.astype(F32), (0, LANES - v.shape[0])).reshape(1, LANES)


def _routing_tables(cnt, eidx, rank, n_blocks, bm):
    e_ids = jnp.arange(N_EXPERTS, dtype=jnp.int32)
    counts = cnt[0, :N_EXPERTS].astype(jnp.int32)
    padded = (counts + bm - 1) // bm * bm
    pend = jnp.sum(jnp.where(e_ids[None, :] <= e_ids[:, None], padded[None, :], 0), axis=1)
    pstart = pend - padded
    dest = jnp.sum(jnp.where(eidx[:, :TOP_K, None] == e_ids, pstart, 0), axis=-1) + rank[:, :TOP_K]
    blk = jnp.arange(n_blocks, dtype=jnp.int32) * bm
    block_expert = jnp.minimum(jnp.sum((pend[None, :] <= blk[:, None]).astype(jnp.int32), axis=1),
                               N_EXPERTS - 1)
    block_valid = blk < pend[-1]
    prev = jnp.concatenate([jnp.full((1,), -1, jnp.int32), block_expert[:-1]])
    block_first = jnp.logical_and(block_valid, block_expert != prev)
    present = counts > 0
    later = jnp.logical_and(present[None, :], e_ids[None, :] > e_ids[:, None])
    next_present = jnp.min(jnp.where(later, e_ids[None, :], N_EXPERTS), axis=1)
    next_present = jnp.where(next_present == N_EXPERTS, -1, next_present)
    ordinal = jnp.sum(jnp.where(e_ids[None, :] < e_ids[:, None], present[None, :].astype(jnp.int32), 0), axis=1)
    pick = lambda table: jnp.sum(jnp.where(block_expert[:, None] == e_ids, table, 0), axis=1)
    block_rows = jnp.where(block_valid, jnp.clip(pick(pstart + counts) - blk, 0, bm), 0)
    tables = (block_expert, block_rows, block_first.astype(jnp.int32), pick(ordinal % 2), pick(next_present))
    return dest, tuple(t.astype(jnp.int32) for t in tables)


def kernel(x, positions, norm_mix_w, w_in, conv_w, a_log, dt_bias, gdn_norm_w, q_norm_w, w_uq, kv_norm_w,
           w_ukv, mla_out_norm_w, w_out, norm_ffn_w, router_w, router_b, w1, b1, w2, b2, norm_final_w):
    bsz, seq, d = x.shape
    t = bsz * seq
    depth = w_in.shape[0]
    tm = min(512, t)
    tr = min(256, t)
    bm = 256
    tq = min(512, seq)
    x2 = x.reshape(t, d)
    posf = positions.astype(F32)
    freq = _rope_freq()
    for l in range(depth):
        qkv, z, cq, ckv, kr2, ab = _inproj(x2, norm_mix_w[l].reshape(1, d), _prep_inproj_weight(w_in[l]), tm)
        conv_w8 = jnp.pad(conv_w[l], ((0, 8 - GDN_CONV), (0, 0)))
        o_a = _gdn(qkv, z, ab, conv_w8, _lane_row(a_log[l]), _lane_row(dt_bias[l]),
                   gdn_norm_w[l].reshape(1, GDN_D), bsz, seq, min(256, seq))
        wqt, wqrt, wk, wvt = _prep_mla_weights(w_uq[l], w_ukv[l])
        qt, k, vt = _mla_proj(cq, ckv, kr2, posf.reshape(t, 1), posf.reshape(1, t),
                              q_norm_w[l].reshape(1, -1), kv_norm_w[l].reshape(1, -1),
                              wqt, wqrt, wk, wvt, freq.reshape(1, LANES), freq.reshape(LANES, 1), tq)
        o_bt = _attention(qt, k, vt, mla_out_norm_w[l].reshape(-1, 1), bsz, seq, tq)
        rw = _pad_cols(router_w[l], LANES)
        rb = _lane_row(router_b[l])
        x1, hn, eidx, gates, rank, cnt = _router(x2, o_a, o_bt, w_out[l].astype(BF16),
                                                 norm_ffn_w[l].reshape(1, d), rw, rb, tr)
        n_pad = t * TOP_K + N_EXPERTS * bm
        dest, tables = _routing_tables(cnt, eidx, rank, n_pad // bm, bm)
        dest_flat = dest.T.reshape(TOP_K * t)
        s_words = d // 2 // LANES
        buf = _dispatch(dest_flat, hn.reshape(t, s_words, LANES), n_pad)
        yb = _experts(tables, buf.reshape(n_pad * s_words, LANES), w1[l],
                      b1[l][:, None, 0::2], b1[l][:, None, 1::2], w2[l], b2[l][:, None, :], bm, d)
        y4 = _gather_rows(dest_flat, yb.reshape(n_pad, s_words, LANES))
        x2 = _combine(x1, gates, y4.reshape(TOP_K, t * s_words, LANES), norm_final_w.reshape(1, d), tr,
                      normalize=(l == depth - 1))
    return x2.reshape(bsz, seq, d)
```

```python
import functools
import math

import jax
import jax.numpy as jnp
from jax import lax
from jax.experimental import pallas as pl
from jax.experimental.pallas import tpu as pltpu
from jax.experimental.pallas import tpu_sc as plsc

F32 = jnp.float32
BF16 = jnp.bfloat16
HIGHEST = lax.Precision.HIGHEST

LANES = 128
MXU_COLS = 256
EPS = 1e-6
CHUNK = 64

GDN_HEADS = 4
GDN_D = 128
GDN_CONV = 4
MLA_HEADS = 8
MLA_NOPE = 64
MLA_ROPE = 32
MLA_V = 64
MLA_Q_LORA = 384
MLA_KV_LORA = 256
ROPE_THETA = 10000.0
N_EXPERTS = 32
TOP_K = 4
SWIGLU_LIMIT = 7.0
SWIGLU_ALPHA = 1.702

NEG_BIG = -1e30
VMEM_LIMIT = 52 * 1024 * 1024


def _sigmoid(x):
    return 1.0 / (1.0 + jnp.exp(-x))


def _softplus(x):
    return jnp.maximum(x, 0.0) + jnp.log(1.0 + jnp.exp(-jnp.abs(x)))


def _dot(a, b, precision=None):
    return jnp.dot(a, b, preferred_element_type=F32, precision=precision)


def _dot_nt(a, b, precision=None):
    return lax.dot_general(a, b, (((1,), (1,)), ((), ())), preferred_element_type=F32,
                           precision=precision)


def _dot_tn(a, b):
    return lax.dot_general(a, b, (((0,), (0,)), ((), ())), preferred_element_type=F32)


def _split_bf16(a):
    hi = a.astype(BF16)
    return hi, (a - hi.astype(F32)).astype(BF16)


def _pack_bf16_pair(lo, hi):
    lo_bits = pltpu.bitcast(lo.astype(BF16).astype(F32), jnp.int32)
    hi_bits = pltpu.bitcast(hi.astype(BF16).astype(F32), jnp.int32)
    return jnp.bitwise_or(hi_bits, lax.shift_right_logical(lo_bits, 16))


def _unpack_bf16_pair(w):
    lo = pltpu.bitcast(lax.shift_left(w, 16), F32)
    hi = pltpu.bitcast(jnp.bitwise_and(w, -65536), F32)
    return lo, hi


def _load_packed_rows(ref, lead, m):
    s = ref.shape[-2] // m
    return jnp.concatenate([ref[(*lead, pl.ds(j, m, stride=s), slice(None))] for j in range(s)], axis=1)


def _store_packed_rows(ref, words):
    m = words.shape[0]
    s = ref.shape[0] // m
    for j in range(s):
        ref[pl.ds(j, m, stride=s), :] = words[:, j * LANES:(j + 1) * LANES]


def _dot_split(a, b):
    return _dot(a[0], b[0]) + _dot(a[0], b[1]) + _dot(a[1], b[0])


C_QKV = 3 * GDN_HEADS * GDN_D
C_Z = GDN_HEADS * GDN_D
C_IN = C_QKV + C_Z + MLA_Q_LORA + MLA_KV_LORA + 3 * LANES


def _inproj_kernel(x_ref, nw_ref, w_ref, qkv_ref, z_ref, cq_ref, ckv_ref, kr_ref, ab_ref):
    x = x_ref[...]
    var = jnp.mean(x * x, axis=-1, keepdims=True)
    h = (x * lax.rsqrt(var + EPS) * nw_ref[...]).astype(BF16)
    p = _dot(h, w_ref[...])
    o = 0
    for ref in (qkv_ref, z_ref, cq_ref, ckv_ref, kr_ref, ab_ref):
        n = ref.shape[-1]
        ref[...] = p[:, o:o + n]
        o += n


def _inproj(x2, norm_w, w_all, tm):
    t, d = x2.shape
    widths = (C_QKV, C_Z, MLA_Q_LORA, MLA_KV_LORA, 2 * LANES, LANES)
    return pl.pallas_call(
        _inproj_kernel,
        grid=(t // tm,),
        in_specs=[pl.BlockSpec((tm, d), lambda i: (i, 0)),
                  pl.BlockSpec((1, d), lambda i: (0, 0)),
                  pl.BlockSpec((d, C_IN), lambda i: (0, 0))],
        out_specs=[pl.BlockSpec((tm, n), lambda i: (i, 0)) for n in widths],
        out_shape=[jax.ShapeDtypeStruct((t, n), F32) for n in widths],
        compiler_params=pltpu.CompilerParams(dimension_semantics=("parallel",),
                                             vmem_limit_bytes=VMEM_LIMIT),
        name="inproj",
    )(x2, norm_w, w_all)


def _gdn_kernel(qkv_ref, z_ref, ab_ref, cw_ref, alog_ref, dtb_ref, nw_ref, o_ref, ext_ref, s_ref, *, rows):
    c = CHUNK
    n_chunks = rows // c

    @pl.when(pl.program_id(1) == 0)
    def _():
        ext_ref[0:8, :] = jnp.zeros((8, C_QKV), F32)
        s_ref[...] = jnp.zeros_like(s_ref)

    x = qkv_ref[...]
    ext_ref[8:8 + rows, :] = x
    cw = cw_ref[...]
    xc = (cw[3:4] * x + cw[2:3] * ext_ref[7:7 + rows, :] + cw[1:2] * ext_ref[6:6 + rows, :]
          + cw[0:1] * ext_ref[5:5 + rows, :])
    ext_ref[0:8, :] = x[rows - 8:rows, :]
    xc = xc * _sigmoid(xc)

    ab = ab_ref[...]
    g_all = -jnp.exp(alog_ref[...]) * _softplus(ab + dtb_ref[...])
    beta_all = _sigmoid(ab)
    rr = lax.broadcasted_iota(jnp.int32, (rows, rows), 0)
    rc = lax.broadcasted_iota(jnp.int32, (rows, rows), 1)
    in_chunk_prefix = jnp.where(rr // c == rc // c, jnp.where(rr >= rc, 1.0, 0.0), 0.0)
    g_cum = _dot(in_chunk_prefix, g_all, HIGHEST)
    e8 = (lax.broadcasted_iota(jnp.int32, (8, LANES), 0)
          == lax.broadcasted_iota(jnp.int32, (8, LANES), 1)).astype(F32)
    g_row_all = _dot_nt(e8, g_cum, HIGHEST)
    ri = lax.broadcasted_iota(jnp.int32, (c, c), 0)
    ci = lax.broadcasted_iota(jnp.int32, (c, c), 1)
    causal = ri >= ci
    strict = ri > ci
    eye = (ri == ci).astype(F32)
    nw = nw_ref[...]

    chains = [(n, h) for n in range(n_chunks) for h in range(GDN_HEADS)]
    qs, ks, vs, gcs, betas, decays = [], [], [], [], [], []
    for n, h in chains:
        rs = slice(n * c, (n + 1) * c)
        q = xc[rs, h * GDN_D:(h + 1) * GDN_D]
        k = xc[rs, (GDN_HEADS + h) * GDN_D:(GDN_HEADS + h + 1) * GDN_D]
        qs.append(q * lax.rsqrt(jnp.sum(q * q, axis=-1, keepdims=True) + EPS) * (GDN_D ** -0.5))
        ks.append(k * lax.rsqrt(jnp.sum(k * k, axis=-1, keepdims=True) + EPS))
        vs.append(xc[rs, (2 * GDN_HEADS + h) * GDN_D:(2 * GDN_HEADS + h + 1) * GDN_D])
        gc = g_cum[rs, h:h + 1]
        gr = g_row_all[h:h + 1, rs]
        gcs.append(gc)
        betas.append(beta_all[rs, GDN_HEADS + h:GDN_HEADS + h + 1])
        decays.append(jnp.exp(jnp.where(causal, gc - gr, -jnp.inf)))
    kbs = [k * b for k, b in zip(ks, betas)]
    k16s = [k.astype(BF16) for k in ks]
    a_mats = [jnp.where(strict, _dot_nt(kb.astype(BF16), k16) * dec, 0.0)
              for kb, k16, dec in zip(kbs, k16s, decays)]
    t_mats = [eye - a for a in a_mats]
    pw_s = [_split_bf16(a) for a in a_mats]
    for _ in range(5):
        pw_s = [_split_bf16(_dot_split(p, p)) for p in pw_s]
        t_s = [_split_bf16(t) for t in t_mats]
        t_mats = [t + _dot_split(ts, p) for t, ts, p in zip(t_mats, t_s, pw_s)]
    e_gs = [jnp.exp(gc) for gc in gcs]
    uws = [_dot(t.astype(BF16), jnp.concatenate([v * b, kb * eg], axis=1).astype(BF16))
           for t, v, b, kb, eg in zip(t_mats, vs, betas, kbs, e_gs)]
    qk16s = [(_dot_nt(q.astype(BF16), k16) * dec).astype(BF16) for q, k16, dec in zip(qs, k16s, decays)]
    qd16s = [(q * eg).astype(BF16) for q, eg in zip(qs, e_gs)]
    kd16s = [(k * jnp.exp(gc[c - 1:c, :] - gc)).astype(BF16) for k, gc in zip(ks, gcs)]
    g_tots = [jnp.exp(gc[c - 1:c, :]) for gc in gcs]

    heads = range(GDN_HEADS)
    states = [s_ref[h] for h in heads]
    for n in range(n_chunks):
        rs = slice(n * c, (n + 1) * c)
        ix = [n * GDN_HEADS + h for h in heads]
        s16s = [states[h].astype(BF16) for h in heads]
        v16s = [(uws[i][:, :GDN_D] - _dot(uws[i][:, GDN_D:].astype(BF16), s16s[h])).astype(BF16)
                for h, i in zip(heads, ix)]
        states = [states[h] * g_tots[i] + _dot_tn(kd16s[i], v16s[h]) for h, i in zip(heads, ix)]
        os_ = [_dot(qd16s[i], s16s[h]) + _dot(qk16s[i], v16s[h]) for h, i in zip(heads, ix)]
        for h in heads:
            o = os_[h]
            on = o * lax.rsqrt(jnp.mean(o * o, axis=-1, keepdims=True) + EPS) * nw
            zh = z_ref[rs, h * GDN_D:(h + 1) * GDN_D]
            o_ref[rs, h * GDN_D:(h + 1) * GDN_D] = on * (zh * _sigmoid(zh))
    for h in heads:
        s_ref[h] = states[h]


def _gdn(qkv, z, ab, conv_w8, alog_row, dtb_row, norm_w, bsz, seq, rows):
    n = seq // rows
    row = lambda b, i: (b * n + i, 0)
    const = lambda b, i: (0, 0)
    return pl.pallas_call(
        functools.partial(_gdn_kernel, rows=rows),
        grid=(bsz, n),
        in_specs=[pl.BlockSpec((rows, C_QKV), row),
                  pl.BlockSpec((rows, C_Z), row),
                  pl.BlockSpec((rows, LANES), row),
                  pl.BlockSpec((8, C_QKV), const),
                  pl.BlockSpec((1, LANES), const),
                  pl.BlockSpec((1, LANES), const),
                  pl.BlockSpec((1, GDN_D), const)],
        out_specs=pl.BlockSpec((rows, C_Z), row),
        out_shape=jax.ShapeDtypeStruct((bsz * seq, C_Z), F32),
        scratch_shapes=[pltpu.VMEM((8 + rows, C_QKV), F32),
                        pltpu.VMEM((GDN_HEADS, GDN_D, GDN_D), F32)],
        compiler_params=pltpu.CompilerParams(dimension_semantics=("arbitrary", "arbitrary"),
                                             vmem_limit_bytes=VMEM_LIMIT),
        name="gdn",
    )(qkv, z, ab, conv_w8, alog_row, dtb_row, norm_w)


D_QK_PAD = LANES
C_HEADS = MLA_HEADS * D_QK_PAD
C_V = MLA_HEADS * MLA_V
V_AUG = MLA_V + 16
C_VAUG = MLA_HEADS * V_AUG
Q_SCALE = (MLA_NOPE + MLA_ROPE) ** -0.5 * math.log2(math.e)


def _mla_proj_kernel(cq_ref, ckv_ref, kr_ref, posr_ref, qnw_ref, kvnw_ref, wqt_ref, wqrt_ref,
                     wk_ref, wvt_ref, freqc_ref, qt_ref, k_ref, vt_ref):
    cq = cq_ref[...]
    cqn = (cq * lax.rsqrt(jnp.mean(cq * cq, axis=-1, keepdims=True) + EPS) * qnw_ref[...]).astype(BF16)
    ckv = ckv_ref[...]
    ckvn = (ckv * lax.rsqrt(jnp.mean(ckv * ckv, axis=-1, keepdims=True) + EPS)
            * kvnw_ref[...]).astype(BF16)
    tm = posr_ref.shape[-1]
    ang = freqc_ref[...] * posr_ref[...]
    c16 = jnp.cos(ang)
    s16 = jnp.sin(ang)
    pad = LANES - MLA_NOPE - MLA_ROPE
    cs_t = jnp.concatenate([jnp.ones((MLA_NOPE, tm), F32), c16, c16, jnp.ones((pad, tm), F32)], axis=0)
    sn_t = jnp.concatenate([jnp.zeros((MLA_NOPE, tm), F32), s16, s16, jnp.zeros((pad, tm), F32)], axis=0)
    cs = cs_t.T
    sn = sn_t.T
    qa_t = _dot_nt(wqt_ref[...], cqn)
    qb_t = _dot_nt(wqrt_ref[...], cqn)
    kn = _dot(ckvn, wk_ref[...])
    kr = kr_ref[...]
    kpe = kr[:, :LANES] * cs + kr[:, LANES:] * sn
    for h in range(MLA_HEADS):
        sl = slice(h * LANES, (h + 1) * LANES)
        qt_ref[sl, :] = ((qa_t[sl, :] * cs_t + qb_t[sl, :] * sn_t) * Q_SCALE).astype(BF16)
        k_ref[:, sl] = (kn[:, sl] + kpe).astype(BF16)
    v_t = _dot_nt(wvt_ref[...], ckvn).astype(BF16)
    ones = jnp.ones((V_AUG - MLA_V, v_t.shape[1]), BF16)
    for h in range(MLA_HEADS):
        vt_ref[0, h * V_AUG:h * V_AUG + MLA_V, :] = v_t[h * MLA_V:(h + 1) * MLA_V, :]
        vt_ref[0, h * V_AUG + MLA_V:(h + 1) * V_AUG, :] = ones


def _mla_proj(cq, ckv, kr2, posr, qnw, kvnw, wqt, wqrt, wk, wvt, freqc, tm):
    t = cq.shape[0]
    row = lambda i: (i, 0)
    col = lambda i: (0, i)
    const = lambda i: (0, 0)
    return pl.pallas_call(
        _mla_proj_kernel,
        grid=(t // tm,),
        in_specs=[pl.BlockSpec((tm, MLA_Q_LORA), row),
                  pl.BlockSpec((tm, MLA_KV_LORA), row),
                  pl.BlockSpec((tm, 2 * LANES), row),
                  pl.BlockSpec((1, tm), col),
                  pl.BlockSpec((1, MLA_Q_LORA), const),
                  pl.BlockSpec((1, MLA_KV_LORA), const),
                  pl.BlockSpec((C_HEADS, MLA_Q_LORA), const),
                  pl.BlockSpec((C_HEADS, MLA_Q_LORA), const),
                  pl.BlockSpec((MLA_KV_LORA, C_HEADS), const),
                  pl.BlockSpec((C_V, MLA_KV_LORA), const),
                  pl.BlockSpec((MLA_ROPE // 2, 1), const)],
        out_specs=[pl.BlockSpec((C_HEADS, tm), col),
                   pl.BlockSpec((tm, C_HEADS), row),
                   pl.BlockSpec((1, C_VAUG, tm), lambda i: (i, 0, 0))],
        out_shape=[jax.ShapeDtypeStruct((C_HEADS, t), BF16),
                   jax.ShapeDtypeStruct((t, C_HEADS), BF16),
                   jax.ShapeDtypeStruct((t // tm, C_VAUG, tm), BF16)],
        compiler_params=pltpu.CompilerParams(dimension_semantics=("parallel",),
                                             vmem_limit_bytes=VMEM_LIMIT),
        name="mla_proj",
    )(cq, ckv, kr2, posr, qnw, kvnw, wqt, wqrt, wk, wvt, freqc)


def _attn_kernel(qt_ref, k_ref, vt_ref, nw_ref, ot_ref, m_ref, acc_ref, *, tq):
    i = pl.program_id(1)
    m_ref[...] = jnp.full(m_ref.shape, NEG_BIG, F32)
    acc_ref[...] = jnp.zeros(acc_ref.shape, F32)
    key_chunk = lax.broadcasted_iota(jnp.int32, (tq, tq), 0) // CHUNK
    qry_chunk = lax.broadcasted_iota(jnp.int32, (tq, tq), 1) // CHUNK
    diag_mask = key_chunk <= qry_chunk

    def tile(j, masked):
        rows = pl.ds(pl.multiple_of(j * tq, tq), tq)
        scores = []
        for h in range(MLA_HEADS):
            hs = slice(h * LANES, (h + 1) * LANES)
            s_t = _dot(k_ref[rows, hs], qt_ref[hs, :])
            scores.append(jnp.where(diag_mask, s_t, NEG_BIG) if masked else s_t)
        probs, alphas = [], []
        for h in range(MLA_HEADS):
            m_old = m_ref[h]
            m_new = jnp.maximum(m_old, jnp.max(scores[h], axis=0, keepdims=True))
            m_ref[h] = m_new
            probs.append(jnp.exp2(scores[h] - m_new[0:1, :]).astype(BF16))
            alphas.append(jnp.exp2(m_old[0:1, :] - m_new[0:1, :]))
        for h in range(MLA_HEADS):
            v_aug = vt_ref[j, h * V_AUG:(h + 1) * V_AUG, :]
            acc_ref[h] = alphas[h] * acc_ref[h] + _dot(v_aug, probs[h])

    def full_tile(j, carry):
        tile(j, False)
        return carry

    lax.fori_loop(0, i, full_tile, 0)
    tile(i, True)

    outs = []
    for h in range(MLA_HEADS):
        acc = acc_ref[h]
        outs.append(acc[:MLA_V, :] / acc[MLA_V:MLA_V + 1, :])
    ssq = outs[0] * outs[0]
    for o in outs[1:]:
        ssq = ssq + o * o
    inv = lax.rsqrt(jnp.sum(ssq, axis=0, keepdims=True) / C_V + EPS)
    for h in range(MLA_HEADS):
        vs = slice(h * MLA_V, (h + 1) * MLA_V)
        ot_ref[vs, :] = outs[h] * inv * nw_ref[vs, :]


def _attention(qt, k, vt, norm_w_col, bsz, seq, tq):
    nq = seq // tq
    return pl.pallas_call(
        functools.partial(_attn_kernel, tq=tq),
        grid=(bsz, nq),
        in_specs=[pl.BlockSpec((C_HEADS, tq), lambda b, i: (0, b * nq + i)),
                  pl.BlockSpec((seq, C_HEADS), lambda b, i: (b, 0)),
                  pl.BlockSpec((nq, C_VAUG, tq), lambda b, i: (b, 0, 0)),
                  pl.BlockSpec((C_V, 1), lambda b, i: (0, 0))],
        out_specs=pl.BlockSpec((C_V, tq), lambda b, i: (0, b * nq + i)),
        out_shape=jax.ShapeDtypeStruct((C_V, bsz * seq), F32),
        scratch_shapes=[pltpu.VMEM((MLA_HEADS, 8, tq), F32),
                        pltpu.VMEM((MLA_HEADS, V_AUG, tq), F32)],
        compiler_params=pltpu.CompilerParams(dimension_semantics=("parallel", "parallel"),
                                             vmem_limit_bytes=VMEM_LIMIT),
        name="attention",
    )(qt, k, vt, norm_w_col)


def _router_kernel(x_ref, oa_ref, obt_ref, wo_ref, nw_ref, rw_ref, rb_ref,
                   x1_ref, h_ref, eidx_ref, gate_ref, rank_ref, cnt_ref, run_ref, *, tm):
    @pl.when(pl.program_id(0) == 0)
    def _():
        run_ref[...] = jnp.zeros_like(run_ref)

    half = oa_ref.shape[-1]
    y = (_dot(oa_ref[...].astype(BF16), wo_ref[0:half, :])
         + _dot_tn(obt_ref[...].astype(BF16), wo_ref[half:, :]))
    x1 = x_ref[...] + y
    x1_ref[...] = x1
    hn = x1 * lax.rsqrt(jnp.mean(x1 * x1, axis=-1, keepdims=True) + EPS) * nw_ref[...]
    d_half = hn.shape[-1] // 2
    _store_packed_rows(h_ref, _pack_bf16_pair(hn[:, :d_half], hn[:, d_half:]))
    lane = lax.broadcasted_iota(jnp.int32, (tm, LANES), 1)
    logits = _dot_split(_split_bf16(hn), _split_bf16(rw_ref[...])) + rb_ref[...]
    lg = jnp.where(lane < N_EXPERTS, logits, -jnp.inf)
    sels, tops = [], []
    eidx = jnp.zeros((tm, LANES), jnp.int32)
    for kk in range(TOP_K):
        m = jnp.max(lg, axis=-1, keepdims=True)
        idx = jnp.min(jnp.where(lg == m, lane, LANES), axis=-1, keepdims=True)
        sel = lane == idx
        lg = jnp.where(sel, -jnp.inf, lg)
        sels.append(sel)
        tops.append(m)
        eidx = jnp.where(lane == kk, idx, eidx)
    exps = [jnp.exp(tv - tops[0]) for tv in tops]
    den = exps[0] + exps[1] + exps[2] + exps[3]
    gates = jnp.zeros((tm, LANES), F32)
    multi = jnp.zeros((tm, LANES), F32)
    for kk in range(TOP_K):
        gates = jnp.where(lane == kk, exps[kk] / den, gates)
        multi = multi + sels[kk].astype(F32)
    ri = lax.broadcasted_iota(jnp.int32, (tm, tm), 0)
    ci = lax.broadcasted_iota(jnp.int32, (tm, tm), 1)
    before = _dot((ri > ci).astype(BF16), multi.astype(BF16)) + run_ref[...]
    rank = jnp.zeros((tm, LANES), jnp.int32)
    for kk in range(TOP_K):
        r = jnp.sum(jnp.where(sels[kk], before, 0.0), axis=-1, keepdims=True)
        rank = jnp.where(lane == kk, r.astype(jnp.int32), rank)
    run = run_ref[...] + jnp.sum(multi, axis=0, keepdims=True)
    run_ref[...] = run
    cnt_ref[...] = run
    eidx_ref[...] = eidx
    gate_ref[...] = gates
    rank_ref[...] = rank


def _router(x2, oa, obt, w_out, norm_w, rw, rb, tm):
    t, d = x2.shape
    half = oa.shape[-1]
    s_words = d // 2 // LANES
    row = lambda i: (i, 0)
    const = lambda i: (0, 0)
    return pl.pallas_call(
        functools.partial(_router_kernel, tm=tm),
        grid=(t // tm,),
        in_specs=[pl.BlockSpec((tm, d), row),
                  pl.BlockSpec((tm, half), row),
                  pl.BlockSpec((half, tm), lambda i: (0, i)),
                  pl.BlockSpec((2 * half, d), const),
                  pl.BlockSpec((1, d), const),
                  pl.BlockSpec((d, LANES), const),
                  pl.BlockSpec((1, LANES), const)],
        out_specs=[pl.BlockSpec((tm, d), row), pl.BlockSpec((tm * s_words, LANES), row),
                   pl.BlockSpec((tm, LANES), row), pl.BlockSpec((tm, LANES), row),
                   pl.BlockSpec((tm, LANES), row), pl.BlockSpec((1, LANES), const)],
        out_shape=[jax.ShapeDtypeStruct((t, d), F32), jax.ShapeDtypeStruct((t * s_words, LANES), jnp.int32),
                   jax.ShapeDtypeStruct((t, LANES), jnp.int32), jax.ShapeDtypeStruct((t, LANES), F32),
                   jax.ShapeDtypeStruct((t, LANES), jnp.int32), jax.ShapeDtypeStruct((1, LANES), F32)],
        scratch_shapes=[pltpu.VMEM((1, LANES), F32)],
        compiler_params=pltpu.CompilerParams(dimension_semantics=("arbitrary",),
                                             vmem_limit_bytes=VMEM_LIMIT),
        name="router",
    )(x2, oa, obt, w_out, norm_w, rw, rb)


SC_ROWS = 64


def _sc_mesh():
    return plsc.VectorSubcoreMesh(core_axis_name="c", subcore_axis_name="s")


def _dispatch(dest_flat, h3, n_pad):
    t, s, _ = h3.shape
    mesh = _sc_mesh()
    n_workers = mesh.num_cores * mesh.num_subcores
    per_worker = t // n_workers
    assert per_worker % SC_ROWS == 0

    @functools.partial(
        pl.kernel, mesh=mesh, out_type=jax.ShapeDtypeStruct((n_pad, s, LANES), h3.dtype),
        scratch_types=[pltpu.VMEM((TOP_K, SC_ROWS), jnp.int32), pltpu.VMEM((SC_ROWS, s, LANES), h3.dtype),
                       pltpu.SemaphoreType.DMA],
        name="sc_dispatch")
    def scatter(h_hbm, dest_hbm, out_hbm, idx_v, rows_v, sem):
        base = (lax.axis_index("s") * mesh.num_cores + lax.axis_index("c")) * per_worker

        @pl.loop(0, per_worker // SC_ROWS)
        def _(i):
            off = pl.multiple_of(base + i * SC_ROWS, SC_ROWS)
            pltpu.sync_copy(h_hbm.at[pl.ds(off, SC_ROWS)], rows_v)
            for kk in range(TOP_K):
                pltpu.sync_copy(dest_hbm.at[pl.ds(kk * t + off, SC_ROWS)], idx_v.at[kk])
            copies = [pltpu.async_copy(rows_v, out_hbm.at[idx_v.at[kk]], sem) for kk in range(TOP_K)]
            for cp in copies:
                cp.wait()

    return scatter(h3, dest_flat)


def _expert_kernel(cnt_ref, blk0_ref, nblk_ref, total_ref, x_hbm, w1_ref, b1g_ref, b1l_ref, w2_ref, b2_ref,
                   y_hbm, w1s_ref, w2s_ref, xbuf, ybuf, xsem, ysem, *, bm):
    e = pl.program_id(0)
    n_blocks = nblk_ref[e]
    first = blk0_ref[e]
    total = total_ref[0]
    rows_per_block = xbuf.shape[1]
    n_tiles = w1s_ref.shape[-1] // MXU_COLS
    half = MXU_COLS // 2

    def x_copy(g, slot):
        return pltpu.make_async_copy(x_hbm.at[pl.ds(pl.multiple_of(g * rows_per_block, rows_per_block),
                                                    rows_per_block)], xbuf.at[slot], xsem.at[slot])

    def y_copy(g, slot):
        return pltpu.make_async_copy(ybuf.at[slot], y_hbm.at[pl.ds(pl.multiple_of(g * rows_per_block,
                                                                                   rows_per_block),
                                                                   rows_per_block)], ysem.at[slot])

    @pl.when(e == 0)
    def _():
        x_copy(0, 0).start()

    @pl.when(n_blocks > 0)
    def _():
        r = lax.broadcasted_iota(jnp.int32, (MXU_COLS, MXU_COLS), 0)
        c = lax.broadcasted_iota(jnp.int32, (MXU_COLS, MXU_COLS), 1)
        perm = (r == jnp.where(c < half, 2 * c, 2 * (c - half) + 1)).astype(BF16)
        for tix in range(n_tiles):
            cols = slice(tix * MXU_COLS, (tix + 1) * MXU_COLS)
            w1s_ref[:, cols] = _dot(w1_ref[0, :, cols].astype(BF16), perm).astype(BF16)
        w2s_ref[...] = w2_ref[0].astype(BF16)

        def block(j, carry):
            g = first + j
            slot = lax.rem(g, 2)
            x_copy(g, slot).wait()

            @pl.when(g + 1 < total)
            def _():
                x_copy(g + 1, 1 - slot).start()

            @pl.when(g >= 2)
            def _():
                y_copy(g - 2, slot).wait()

            n_rows = cnt_ref[e] - j * bm
            words = _load_packed_rows(xbuf, (slot,), bm)
            row = lax.broadcasted_iota(jnp.int32, words.shape, 0)
            x_lo, x_hi = _unpack_bf16_pair(jnp.where(row < n_rows, words, 0))
            x = jnp.concatenate([x_lo.astype(BF16), x_hi.astype(BF16)], axis=1)
            acts = []
            for tix in range(n_tiles):
                hp = _dot(x, w1s_ref[:, tix * MXU_COLS:(tix + 1) * MXU_COLS])
                feat = slice(tix * half, (tix + 1) * half)
                gl = jnp.minimum(hp[:, :half] + b1g_ref[0][:, feat], SWIGLU_LIMIT)
                lin = jnp.clip(hp[:, half:] + b1l_ref[0][:, feat], -SWIGLU_LIMIT, SWIGLU_LIMIT)
                acts.append((gl * _sigmoid(SWIGLU_ALPHA * gl) * (lin + 1.0)).astype(BF16))
            y = _dot(jnp.concatenate(acts, axis=1), w2s_ref[...]) + b2_ref[0]
            d_half = y.shape[-1] // 2
            _store_packed_rows(ybuf.at[slot], _pack_bf16_pair(y[:, :d_half], y[:, d_half:]))
            y_copy(g, slot).start()
            return carry

        lax.fori_loop(0, n_blocks, block, 0)

    @pl.when(e == pl.num_programs(0) - 1)
    def _():
        @pl.when(total >= 2)
        def _():
            y_copy(total - 2, lax.rem(total, 2)).wait()

        y_copy(total - 1, lax.rem(total - 1, 2)).wait()


def _experts(tables, xs, w1, b1g, b1l, w2, b2, bm, d):
    s_words = d // 2 // LANES
    dff2 = w1.shape[-1]
    dff = dff2 // 2
    emap = lambda e, *_: (e, 0, 0)
    grid_spec = pltpu.PrefetchScalarGridSpec(
        num_scalar_prefetch=len(tables),
        grid=(N_EXPERTS,),
        in_specs=[pl.BlockSpec(memory_space=pl.ANY),
                  pl.BlockSpec((1, d, dff2), emap),
                  pl.BlockSpec((1, 1, dff), emap),
                  pl.BlockSpec((1, 1, dff), emap),
                  pl.BlockSpec((1, dff, d), emap),
                  pl.BlockSpec((1, 1, d), emap)],
        out_specs=pl.BlockSpec(memory_space=pl.ANY),
        scratch_shapes=[pltpu.VMEM((d, dff2), BF16), pltpu.VMEM((dff, d), BF16),
                        pltpu.VMEM((2, bm * s_words, LANES), jnp.int32),
                        pltpu.VMEM((2, bm * s_words, LANES), jnp.int32),
                        pltpu.SemaphoreType.DMA((2,)), pltpu.SemaphoreType.DMA((2,))],
    )
    return pl.pallas_call(
        functools.partial(_expert_kernel, bm=bm),
        grid_spec=grid_spec,
        out_shape=jax.ShapeDtypeStruct(xs.shape, jnp.int32),
        compiler_params=pltpu.CompilerParams(dimension_semantics=("arbitrary",),
                                             vmem_limit_bytes=VMEM_LIMIT),
        name="experts",
    )(*tables, xs, w1, b1g, b1l, w2, b2)


def _gather_rows(dest_flat, y3):
    n_out = dest_flat.shape[0]
    _, s, _ = y3.shape
    mesh = _sc_mesh()
    n_workers = mesh.num_cores * mesh.num_subcores
    per_worker = n_out // n_workers
    n_chunks = per_worker // SC_ROWS
    assert per_worker % SC_ROWS == 0 and n_chunks % 2 == 0

    @functools.partial(
        pl.kernel, mesh=mesh, out_type=jax.ShapeDtypeStruct((n_out, s, LANES), y3.dtype),
        scratch_types=[pltpu.VMEM((2, SC_ROWS), jnp.int32), pltpu.VMEM((2, SC_ROWS, s, LANES), y3.dtype),
                       pltpu.SemaphoreType.DMA((2,)), pltpu.SemaphoreType.DMA((2,))],
        name="sc_gather")
    def gather(y_hbm, dest_hbm, out_hbm, idx_v, rows_v, gsem, wsem):
        base = (lax.axis_index("s") * mesh.num_cores + lax.axis_index("c")) * per_worker

        def rows_of(chunk):
            return pl.ds(pl.multiple_of(base + chunk * SC_ROWS, SC_ROWS), SC_ROWS)

        def gather_copy(slot):
            return pltpu.make_async_copy(y_hbm.at[idx_v.at[slot]], rows_v.at[slot], gsem.at[slot])

        def write_copy(chunk, slot):
            return pltpu.make_async_copy(rows_v.at[slot], out_hbm.at[rows_of(chunk)], wsem.at[slot])

        def start_gather(chunk, slot):
            pltpu.sync_copy(dest_hbm.at[rows_of(chunk)], idx_v.at[slot])
            gather_copy(slot).start()

        start_gather(0, 0)

        @pl.loop(0, n_chunks // 2)
        def _(pair):
            for slot in (0, 1):
                chunk = 2 * pair + slot
                gather_copy(slot).wait()
                write_copy(chunk, slot).start()

                @pl.when(chunk >= 1)
                def _():
                    write_copy(chunk - 1, 1 - slot).wait()

                @pl.when(chunk + 1 < n_chunks)
                def _():
                    start_gather(chunk + 1, 1 - slot)

        write_copy(n_chunks - 1, 1).wait()

    return gather(y3, dest_flat)


def _combine_kernel(x1_ref, gate_ref, y4_ref, nw_ref, o_ref, *, tm, normalize):
    x = x1_ref[...]
    gates = gate_ref[...]
    for kk in range(TOP_K):
        lo, hi = _unpack_bf16_pair(_load_packed_rows(y4_ref, (kk,), tm))
        x = x + gates[:, kk:kk + 1] * jnp.concatenate([lo, hi], axis=1)
    if normalize:
        x = x * lax.rsqrt(jnp.mean(x * x, axis=-1, keepdims=True) + EPS) * nw_ref[...]
    o_ref[...] = x


def _combine(x1, gates, y4, norm_w, tm, normalize):
    t, d = x1.shape
    s_words = d // 2 // LANES
    row = lambda i: (i, 0)
    return pl.pallas_call(
        functools.partial(_combine_kernel, tm=tm, normalize=normalize),
        grid=(t // tm,),
        in_specs=[pl.BlockSpec((tm, d), row), pl.BlockSpec((tm, LANES), row),
                  pl.BlockSpec((TOP_K, tm * s_words, LANES), lambda i: (0, i, 0)),
                  pl.BlockSpec((1, d), lambda i: (0, 0))],
        out_specs=pl.BlockSpec((tm, d), row),
        out_shape=jax.ShapeDtypeStruct((t, d), F32),
        compiler_params=pltpu.CompilerParams(dimension_semantics=("parallel",)),
        name="combine",
    )(x1, gates, y4, norm_w)


def _pad_cols(w, n):
    return jnp.pad(w, ((0, 0), (0, n - w.shape[1])))


def _rot_half(w):
    half = w.shape[-1] // 2
    return jnp.concatenate([-w[..., half:], w[..., :half]], axis=-1)


def _prep_inproj_weight(w_in):
    d = w_in.shape[0]
    hd = GDN_HEADS * GDN_D
    o = 0
    parts = {}
    for name, n in (("qkv", 3 * hd), ("z", hd), ("a", GDN_HEADS), ("b", GDN_HEADS),
                    ("cq", MLA_Q_LORA), ("ckv", MLA_KV_LORA), ("kr", MLA_ROPE)):
        parts[name] = w_in[:, o:o + n]
        o += n
    zeros = lambda n: jnp.zeros((d, n), w_in.dtype)
    kr_tile = jnp.concatenate([zeros(MLA_NOPE), parts["kr"], zeros(LANES - MLA_NOPE - MLA_ROPE)], axis=1)
    krr_tile = jnp.concatenate([zeros(MLA_NOPE), _rot_half(parts["kr"]),
                                zeros(LANES - MLA_NOPE - MLA_ROPE)], axis=1)
    ab_tile = jnp.concatenate([parts["a"], parts["b"], zeros(LANES - 2 * GDN_HEADS)], axis=1)
    return jnp.concatenate([parts["qkv"], parts["z"], parts["cq"], parts["ckv"], kr_tile, krr_tile,
                            ab_tile], axis=1).astype(BF16)


def _prep_mla_weights(w_uq, w_ukv):
    dq = MLA_NOPE + MLA_ROPE
    wq = w_uq.reshape(MLA_Q_LORA, MLA_HEADS, dq)
    wq_rot = jnp.concatenate([jnp.zeros_like(wq[..., :MLA_NOPE]), _rot_half(wq[..., MLA_NOPE:])], axis=-1)
    pad = lambda w: jnp.pad(w, ((0, 0), (0, 0), (0, LANES - w.shape[-1]))).reshape(w.shape[0], C_HEADS)
    wkv = w_ukv.reshape(MLA_KV_LORA, MLA_HEADS, MLA_NOPE + MLA_V)
    wv = wkv[..., MLA_NOPE:].reshape(MLA_KV_LORA, C_V)
    return (pad(wq).T.astype(BF16), pad(wq_rot).T.astype(BF16), pad(wkv[..., :MLA_NOPE]).astype(BF16),
            wv.T.astype(BF16))


def _rope_freq():
    half = MLA_ROPE // 2
    return (ROPE_THETA ** (-jnp.arange(half, dtype=F32) / half)).reshape(half, 1)


def _lane_row(v):
    return jnp.pad(v.astype(F32), (0, LANES - v.shape[0])).reshape(1, LANES)


def _routing_tables(cnt, eidx, rank, bm):
    e_ids = jnp.arange(N_EXPERTS, dtype=jnp.int32)
    counts = cnt[0, :N_EXPERTS].astype(jnp.int32)
    padded = (counts + bm - 1) // bm * bm
    pend = jnp.sum(jnp.where(e_ids[None, :] <= e_ids[:, None], padded[None, :], 0), axis=1)
    pstart = pend - padded
    dest = jnp.sum(jnp.where(eidx[:, :TOP_K, None] == e_ids, pstart, 0), axis=-1) + rank[:, :TOP_K]
    tables = (counts, pstart // bm, padded // bm, pend[-1:] // bm)
    return dest, tuple(t.astype(jnp.int32) for t in tables)


def kernel(x, positions, norm_mix_w, w_in, conv_w, a_log, dt_bias, gdn_norm_w, q_norm_w, w_uq, kv_norm_w,
           w_ukv, mla_out_norm_w, w_out, norm_ffn_w, router_w, router_b, w1, b1, w2, b2, norm_final_w):
    bsz, seq, d = x.shape
    t = bsz * seq
    depth = w_in.shape[0]
    tm = min(512, t)
    tr = min(256, t)
    bm = 256
    tq = min(512, seq)
    x2 = x.reshape(t, d)
    posf = positions.astype(F32)
    freq = _rope_freq()
    for l in range(depth):
        qkv, z, cq, ckv, kr2, ab = _inproj(x2, norm_mix_w[l].reshape(1, d), _prep_inproj_weight(w_in[l]), tm)
        conv_w8 = jnp.pad(conv_w[l], ((0, 8 - GDN_CONV), (0, 0)))
        o_a = _gdn(qkv, z, ab, conv_w8, _lane_row(a_log[l]), _lane_row(dt_bias[l]),
                   gdn_norm_w[l].reshape(1, GDN_D), bsz, seq, min(256, seq))
        wqt, wqrt, wk, wvt = _prep_mla_weights(w_uq[l], w_ukv[l])
        qt, k, vt = _mla_proj(cq, ckv, kr2, posf.reshape(1, t),
                              q_norm_w[l].reshape(1, -1), kv_norm_w[l].reshape(1, -1),
                              wqt, wqrt, wk, wvt, freq, tq)
        o_bt = _attention(qt, k, vt, mla_out_norm_w[l].reshape(-1, 1), bsz, seq, tq)
        rw = _pad_cols(router_w[l], LANES)
        rb = _lane_row(router_b[l])
        x1, hn, eidx, gates, rank, cnt = _router(x2, o_a, o_bt, w_out[l].astype(BF16),
                                                 norm_ffn_w[l].reshape(1, d), rw, rb, tr)
        n_pad = t * TOP_K + N_EXPERTS * bm
        dest, tables = _routing_tables(cnt, eidx, rank, bm)
        dest_flat = dest.T.reshape(TOP_K * t)
        s_words = d // 2 // LANES
        buf = _dispatch(dest_flat, hn.reshape(t, s_words, LANES), n_pad)
        yb = _experts(tables, buf.reshape(n_pad * s_words, LANES), w1[l],
                      b1[l][:, None, 0::2], b1[l][:, None, 1::2], w2[l], b2[l][:, None, :], bm, d)
        y4 = _gather_rows(dest_flat, yb.reshape(n_pad, s_words, LANES))
        x2 = _combine(x1, gates, y4.reshape(TOP_K, t * s_words, LANES), norm_final_w.reshape(1, d), tr,
                      normalize=(l == depth - 1))
    return x2.reshape(bsz, seq, d)
```

```python
import functools
import math

import jax
import jax.numpy as jnp
from jax import lax
from jax.experimental import pallas as pl
from jax.experimental.pallas import tpu as pltpu
from jax.experimental.pallas import tpu_sc as plsc

F32 = jnp.float32
BF16 = jnp.bfloat16
HIGHEST = lax.Precision.HIGHEST

LANES = 128
MXU_COLS = 256
EPS = 1e-6
CHUNK = 64

GDN_HEADS = 4
GDN_D = 128
GDN_CONV = 4
MLA_HEADS = 8
MLA_NOPE = 64
MLA_ROPE = 32
MLA_V = 64
MLA_Q_LORA = 384
MLA_KV_LORA = 256
ROPE_THETA = 10000.0
N_EXPERTS = 32
TOP_K = 4
SWIGLU_LIMIT = 7.0
SWIGLU_ALPHA = 1.702

NEG_BIG = -1e30
VMEM_LIMIT = 52 * 1024 * 1024


def _sigmoid(x):
    return 1.0 / (1.0 + jnp.exp(-x))


def _softplus(x):
    return jnp.maximum(x, 0.0) + jnp.log(1.0 + jnp.exp(-jnp.abs(x)))


def _dot(a, b, precision=None):
    return jnp.dot(a, b, preferred_element_type=F32, precision=precision)


def _dot_nt(a, b, precision=None):
    return lax.dot_general(a, b, (((1,), (1,)), ((), ())), preferred_element_type=F32,
                           precision=precision)


def _dot_tn(a, b):
    return lax.dot_general(a, b, (((0,), (0,)), ((), ())), preferred_element_type=F32)


def _split_bf16(a):
    hi = a.astype(BF16)
    return hi, (a - hi.astype(F32)).astype(BF16)


def _pack_bf16_pair(lo, hi):
    lo_bits = pltpu.bitcast(lo.astype(BF16).astype(F32), jnp.int32)
    hi_bits = pltpu.bitcast(hi.astype(BF16).astype(F32), jnp.int32)
    return jnp.bitwise_or(hi_bits, lax.shift_right_logical(lo_bits, 16))


def _unpack_bf16_pair(w):
    lo = pltpu.bitcast(lax.shift_left(w, 16), F32)
    hi = pltpu.bitcast(jnp.bitwise_and(w, -65536), F32)
    return lo, hi


def _load_packed_rows(ref, lead, m):
    s = ref.shape[-2] // m
    return jnp.concatenate([ref[(*lead, pl.ds(j, m, stride=s), slice(None))] for j in range(s)], axis=1)


def _store_packed_rows(ref, words):
    m = words.shape[0]
    s = ref.shape[0] // m
    for j in range(s):
        ref[pl.ds(j, m, stride=s), :] = words[:, j * LANES:(j + 1) * LANES]


def _dot_split(a, b):
    return _dot(a[0], b[0]) + _dot(a[0], b[1]) + _dot(a[1], b[0])


C_QKV = 3 * GDN_HEADS * GDN_D
C_Z = GDN_HEADS * GDN_D
C_IN = C_QKV + C_Z + MLA_Q_LORA + MLA_KV_LORA + 3 * LANES


def _inproj_kernel(x_ref, nw_ref, w_ref, qkv_ref, z_ref, cq_ref, ckv_ref, kr_ref, ab_ref):
    x = x_ref[...]
    var = jnp.mean(x * x, axis=-1, keepdims=True)
    h = (x * lax.rsqrt(var + EPS) * nw_ref[...]).astype(BF16)
    p = _dot(h, w_ref[...])
    o = 0
    for ref in (qkv_ref, z_ref, cq_ref, ckv_ref, kr_ref, ab_ref):
        n = ref.shape[-1]
        ref[...] = p[:, o:o + n]
        o += n


def _inproj(x2, norm_w, w_all, tm):
    t, d = x2.shape
    widths = (C_QKV, C_Z, MLA_Q_LORA, MLA_KV_LORA, 2 * LANES, LANES)
    return pl.pallas_call(
        _inproj_kernel,
        grid=(t // tm,),
        in_specs=[pl.BlockSpec((tm, d), lambda i: (i, 0)),
                  pl.BlockSpec((1, d), lambda i: (0, 0)),
                  pl.BlockSpec((d, C_IN), lambda i: (0, 0))],
        out_specs=[pl.BlockSpec((tm, n), lambda i: (i, 0)) for n in widths],
        out_shape=[jax.ShapeDtypeStruct((t, n), F32) for n in widths],
        compiler_params=pltpu.CompilerParams(dimension_semantics=("parallel",),
                                             vmem_limit_bytes=VMEM_LIMIT),
        name="inproj",
    )(x2, norm_w, w_all)


def _gdn_kernel(qkv_ref, z_ref, ab_ref, cw_ref, alog_ref, dtb_ref, nw_ref, o_ref, ext_ref, s_ref, *, rows):
    c = CHUNK
    n_chunks = rows // c

    @pl.when(pl.program_id(1) == 0)
    def _():
        ext_ref[0:8, :] = jnp.zeros((8, C_QKV), F32)
        s_ref[...] = jnp.zeros_like(s_ref)

    x = qkv_ref[...]
    ext_ref[8:8 + rows, :] = x
    cw = cw_ref[...]
    xc = (cw[3:4] * x + cw[2:3] * ext_ref[7:7 + rows, :] + cw[1:2] * ext_ref[6:6 + rows, :]
          + cw[0:1] * ext_ref[5:5 + rows, :])
    ext_ref[0:8, :] = x[rows - 8:rows, :]
    xc = xc * _sigmoid(xc)

    ab = ab_ref[...]
    g_all = -jnp.exp(alog_ref[...]) * _softplus(ab + dtb_ref[...])
    beta_all = _sigmoid(ab)
    rr = lax.broadcasted_iota(jnp.int32, (rows, rows), 0)
    rc = lax.broadcasted_iota(jnp.int32, (rows, rows), 1)
    in_chunk_prefix = jnp.where(rr // c == rc // c, jnp.where(rr >= rc, 1.0, 0.0), 0.0)
    g_cum = _dot(in_chunk_prefix, g_all, HIGHEST)
    e8 = (lax.broadcasted_iota(jnp.int32, (8, LANES), 0)
          == lax.broadcasted_iota(jnp.int32, (8, LANES), 1)).astype(F32)
    g_row_all = _dot_nt(e8, g_cum, HIGHEST)
    ri = lax.broadcasted_iota(jnp.int32, (c, c), 0)
    ci = lax.broadcasted_iota(jnp.int32, (c, c), 1)
    causal = ri >= ci
    strict = ri > ci
    eye = (ri == ci).astype(F32)
    nw = nw_ref[...]

    chains = [(n, h) for n in range(n_chunks) for h in range(GDN_HEADS)]
    qs, ks, vs, gcs, betas, decays = [], [], [], [], [], []
    for n, h in chains:
        rs = slice(n * c, (n + 1) * c)
        q = xc[rs, h * GDN_D:(h + 1) * GDN_D]
        k = xc[rs, (GDN_HEADS + h) * GDN_D:(GDN_HEADS + h + 1) * GDN_D]
        qs.append(q * lax.rsqrt(jnp.sum(q * q, axis=-1, keepdims=True) + EPS) * (GDN_D ** -0.5))
        ks.append(k * lax.rsqrt(jnp.sum(k * k, axis=-1, keepdims=True) + EPS))
        vs.append(xc[rs, (2 * GDN_HEADS + h) * GDN_D:(2 * GDN_HEADS + h + 1) * GDN_D])
        gc = g_cum[rs, h:h + 1]
        gr = g_row_all[h:h + 1, rs]
        gcs.append(gc)
        betas.append(beta_all[rs, GDN_HEADS + h:GDN_HEADS + h + 1])
        decays.append(jnp.exp(jnp.where(causal, gc - gr, -jnp.inf)))
    kbs = [k * b for k, b in zip(ks, betas)]
    k16s = [k.astype(BF16) for k in ks]
    a_mats = [jnp.where(strict, _dot_nt(kb.astype(BF16), k16) * dec, 0.0)
              for kb, k16, dec in zip(kbs, k16s, decays)]
    t_mats = [eye - a for a in a_mats]
    pw_s = [_split_bf16(a) for a in a_mats]
    for _ in range(5):
        pw_s = [_split_bf16(_dot_split(p, p)) for p in pw_s]
        t_s = [_split_bf16(t) for t in t_mats]
        t_mats = [t + _dot_split(ts, p) for t, ts, p in zip(t_mats, t_s, pw_s)]
    e_gs = [jnp.exp(gc) for gc in gcs]
    uws = [_dot(t.astype(BF16), jnp.concatenate([v * b, kb * eg], axis=1).astype(BF16))
           for t, v, b, kb, eg in zip(t_mats, vs, betas, kbs, e_gs)]
    qk16s = [(_dot_nt(q.astype(BF16), k16) * dec).astype(BF16) for q, k16, dec in zip(qs, k16s, decays)]
    qd16s = [(q * eg).astype(BF16) for q, eg in zip(qs, e_gs)]
    kd16s = [(k * jnp.exp(gc[c - 1:c, :] - gc)).astype(BF16) for k, gc in zip(ks, gcs)]
    g_tots = [jnp.exp(gc[c - 1:c, :]) for gc in gcs]

    heads = range(GDN_HEADS)
    states = [s_ref[h] for h in heads]
    for n in range(n_chunks):
        rs = slice(n * c, (n + 1) * c)
        ix = [n * GDN_HEADS + h for h in heads]
        s16s = [states[h].astype(BF16) for h in heads]
        v16s = [(uws[i][:, :GDN_D] - _dot(uws[i][:, GDN_D:].astype(BF16), s16s[h])).astype(BF16)
                for h, i in zip(heads, ix)]
        states = [states[h] * g_tots[i] + _dot_tn(kd16s[i], v16s[h]) for h, i in zip(heads, ix)]
        os_ = [_dot(qd16s[i], s16s[h]) + _dot(qk16s[i], v16s[h]) for h, i in zip(heads, ix)]
        for h in heads:
            o = os_[h]
            on = o * lax.rsqrt(jnp.mean(o * o, axis=-1, keepdims=True) + EPS) * nw
            zh = z_ref[rs, h * GDN_D:(h + 1) * GDN_D]
            o_ref[rs, h * GDN_D:(h + 1) * GDN_D] = on * (zh * _sigmoid(zh))
    for h in heads:
        s_ref[h] = states[h]


def _gdn(qkv, z, ab, conv_w8, alog_row, dtb_row, norm_w, bsz, seq, rows):
    n = seq // rows
    row = lambda b, i: (b * n + i, 0)
    const = lambda b, i: (0, 0)
    return pl.pallas_call(
        functools.partial(_gdn_kernel, rows=rows),
        grid=(bsz, n),
        in_specs=[pl.BlockSpec((rows, C_QKV), row),
                  pl.BlockSpec((rows, C_Z), row),
                  pl.BlockSpec((rows, LANES), row),
                  pl.BlockSpec((8, C_QKV), const),
                  pl.BlockSpec((1, LANES), const),
                  pl.BlockSpec((1, LANES), const),
                  pl.BlockSpec((1, GDN_D), const)],
        out_specs=pl.BlockSpec((rows, C_Z), row),
        out_shape=jax.ShapeDtypeStruct((bsz * seq, C_Z), F32),
        scratch_shapes=[pltpu.VMEM((8 + rows, C_QKV), F32),
                        pltpu.VMEM((GDN_HEADS, GDN_D, GDN_D), F32)],
        compiler_params=pltpu.CompilerParams(dimension_semantics=("arbitrary", "arbitrary"),
                                             vmem_limit_bytes=VMEM_LIMIT),
        name="gdn",
    )(qkv, z, ab, conv_w8, alog_row, dtb_row, norm_w)


D_QK_PAD = LANES
C_HEADS = MLA_HEADS * D_QK_PAD
C_V = MLA_HEADS * MLA_V
V_AUG = MLA_V + 16
C_VAUG = MLA_HEADS * V_AUG
Q_SCALE = (MLA_NOPE + MLA_ROPE) ** -0.5 * math.log2(math.e)


def _mla_proj_kernel(cq_ref, ckv_ref, kr_ref, posr_ref, qnw_ref, kvnw_ref, wqt_ref, wqrt_ref,
                     wk_ref, wvt_ref, freqc_ref, qt_ref, k_ref, vt_ref):
    cq = cq_ref[...]
    cqn = (cq * lax.rsqrt(jnp.mean(cq * cq, axis=-1, keepdims=True) + EPS) * qnw_ref[...]).astype(BF16)
    ckv = ckv_ref[...]
    ckvn = (ckv * lax.rsqrt(jnp.mean(ckv * ckv, axis=-1, keepdims=True) + EPS)
            * kvnw_ref[...]).astype(BF16)
    tm = posr_ref.shape[-1]
    ang = freqc_ref[...] * posr_ref[...]
    c16 = jnp.cos(ang)
    s16 = jnp.sin(ang)
    pad = LANES - MLA_NOPE - MLA_ROPE
    cs_t = jnp.concatenate([jnp.ones((MLA_NOPE, tm), F32), c16, c16, jnp.ones((pad, tm), F32)], axis=0)
    sn_t = jnp.concatenate([jnp.zeros((MLA_NOPE, tm), F32), s16, s16, jnp.zeros((pad, tm), F32)], axis=0)
    cs = cs_t.T
    sn = sn_t.T
    qa_t = _dot_nt(wqt_ref[...], cqn)
    qb_t = _dot_nt(wqrt_ref[...], cqn)
    kn = _dot(ckvn, wk_ref[...])
    kr = kr_ref[...]
    kpe = kr[:, :LANES] * cs + kr[:, LANES:] * sn
    for h in range(MLA_HEADS):
        sl = slice(h * LANES, (h + 1) * LANES)
        qt_ref[sl, :] = ((qa_t[sl, :] * cs_t + qb_t[sl, :] * sn_t) * Q_SCALE).astype(BF16)
        k_ref[:, sl] = (kn[:, sl] + kpe).astype(BF16)
    v_t = _dot_nt(wvt_ref[...], ckvn).astype(BF16)
    ones = jnp.ones((V_AUG - MLA_V, v_t.shape[1]), BF16)
    for h in range(MLA_HEADS):
        vt_ref[0, h * V_AUG:h * V_AUG + MLA_V, :] = v_t[h * MLA_V:(h + 1) * MLA_V, :]
        vt_ref[0, h * V_AUG + MLA_V:(h + 1) * V_AUG, :] = ones


def _mla_proj(cq, ckv, kr2, posr, qnw, kvnw, wqt, wqrt, wk, wvt, freqc, tm):
    t = cq.shape[0]
    row = lambda i: (i, 0)
    col = lambda i: (0, i)
    const = lambda i: (0, 0)
    return pl.pallas_call(
        _mla_proj_kernel,
        grid=(t // tm,),
        in_specs=[pl.BlockSpec((tm, MLA_Q_LORA), row),
                  pl.BlockSpec((tm, MLA_KV_LORA), row),
                  pl.BlockSpec((tm, 2 * LANES), row),
                  pl.BlockSpec((1, tm), col),
                  pl.BlockSpec((1, MLA_Q_LORA), const),
                  pl.BlockSpec((1, MLA_KV_LORA), const),
                  pl.BlockSpec((C_HEADS, MLA_Q_LORA), const),
                  pl.BlockSpec((C_HEADS, MLA_Q_LORA), const),
                  pl.BlockSpec((MLA_KV_LORA, C_HEADS), const),
                  pl.BlockSpec((C_V, MLA_KV_LORA), const),
                  pl.BlockSpec((MLA_ROPE // 2, 1), const)],
        out_specs=[pl.BlockSpec((C_HEADS, tm), col),
                   pl.BlockSpec((tm, C_HEADS), row),
                   pl.BlockSpec((1, C_VAUG, tm), lambda i: (i, 0, 0))],
        out_shape=[jax.ShapeDtypeStruct((C_HEADS, t), BF16),
                   jax.ShapeDtypeStruct((t, C_HEADS), BF16),
                   jax.ShapeDtypeStruct((t // tm, C_VAUG, tm), BF16)],
        compiler_params=pltpu.CompilerParams(dimension_semantics=("parallel",),
                                             vmem_limit_bytes=VMEM_LIMIT),
        name="mla_proj",
    )(cq, ckv, kr2, posr, qnw, kvnw, wqt, wqrt, wk, wvt, freqc)


def _attn_kernel(qt_ref, k_ref, vt_ref, nw_ref, ot_ref, m_ref, acc_ref, *, tq):
    i = pl.program_id(1)
    m_ref[...] = jnp.full(m_ref.shape, NEG_BIG, F32)
    acc_ref[...] = jnp.zeros(acc_ref.shape, F32)
    key_chunk = lax.broadcasted_iota(jnp.int32, (tq, tq), 0) // CHUNK
    qry_chunk = lax.broadcasted_iota(jnp.int32, (tq, tq), 1) // CHUNK
    diag_mask = key_chunk <= qry_chunk

    def tile(j, masked):
        rows = pl.ds(pl.multiple_of(j * tq, tq), tq)
        scores = []
        for h in range(MLA_HEADS):
            hs = slice(h * LANES, (h + 1) * LANES)
            s_t = _dot(k_ref[rows, hs], qt_ref[hs, :])
            scores.append(jnp.where(diag_mask, s_t, NEG_BIG) if masked else s_t)
        probs, alphas = [], []
        for h in range(MLA_HEADS):
            m_old = m_ref[h]
            m_new = jnp.maximum(m_old, jnp.max(scores[h], axis=0, keepdims=True))
            m_ref[h] = m_new
            probs.append(jnp.exp2(scores[h] - m_new[0:1, :]).astype(BF16))
            alphas.append(jnp.exp2(m_old[0:1, :] - m_new[0:1, :]))
        for h in range(MLA_HEADS):
            v_aug = vt_ref[j, h * V_AUG:(h + 1) * V_AUG, :]
            acc_ref[h] = alphas[h] * acc_ref[h] + _dot(v_aug, probs[h])

    def full_tile(j, carry):
        tile(j, False)
        return carry

    lax.fori_loop(0, i, full_tile, 0)
    tile(i, True)

    outs = []
    for h in range(MLA_HEADS):
        acc = acc_ref[h]
        outs.append(acc[:MLA_V, :] / acc[MLA_V:MLA_V + 1, :])
    ssq = outs[0] * outs[0]
    for o in outs[1:]:
        ssq = ssq + o * o
    inv = lax.rsqrt(jnp.sum(ssq, axis=0, keepdims=True) / C_V + EPS)
    for h in range(MLA_HEADS):
        vs = slice(h * MLA_V, (h + 1) * MLA_V)
        ot_ref[vs, :] = outs[h] * inv * nw_ref[vs, :]


def _attention(qt, k, vt, norm_w_col, bsz, seq, tq):
    nq = seq // tq
    return pl.pallas_call(
        functools.partial(_attn_kernel, tq=tq),
        grid=(bsz, nq),
        in_specs=[pl.BlockSpec((C_HEADS, tq), lambda b, i: (0, b * nq + i)),
                  pl.BlockSpec((seq, C_HEADS), lambda b, i: (b, 0)),
                  pl.BlockSpec((nq, C_VAUG, tq), lambda b, i: (b, 0, 0)),
                  pl.BlockSpec((C_V, 1), lambda b, i: (0, 0))],
        out_specs=pl.BlockSpec((C_V, tq), lambda b, i: (0, b * nq + i)),
        out_shape=jax.ShapeDtypeStruct((C_V, bsz * seq), F32),
        scratch_shapes=[pltpu.VMEM((MLA_HEADS, 8, tq), F32),
                        pltpu.VMEM((MLA_HEADS, V_AUG, tq), F32)],
        compiler_params=pltpu.CompilerParams(dimension_semantics=("parallel", "parallel"),
                                             vmem_limit_bytes=VMEM_LIMIT),
        name="attention",
    )(qt, k, vt, norm_w_col)


def _router_kernel(x_ref, oa_ref, obt_ref, wo_ref, nw_ref, rw_ref, rb_ref,
                   x1_ref, h_ref, eidx_ref, gate_ref, rank_ref, cnt_ref, run_ref, *, tm):
    @pl.when(pl.program_id(0) == 0)
    def _():
        run_ref[...] = jnp.zeros_like(run_ref)

    half = oa_ref.shape[-1]
    y = (_dot(oa_ref[...].astype(BF16), wo_ref[0:half, :])
         + _dot_tn(obt_ref[...].astype(BF16), wo_ref[half:, :]))
    x1 = x_ref[...] + y
    x1_ref[...] = x1
    hn = x1 * lax.rsqrt(jnp.mean(x1 * x1, axis=-1, keepdims=True) + EPS) * nw_ref[...]
    d_half = hn.shape[-1] // 2
    _store_packed_rows(h_ref, _pack_bf16_pair(hn[:, :d_half], hn[:, d_half:]))
    lane = lax.broadcasted_iota(jnp.int32, (tm, LANES), 1)
    logits = _dot_split(_split_bf16(hn), _split_bf16(rw_ref[...])) + rb_ref[...]
    lg = jnp.where(lane < N_EXPERTS, logits, -jnp.inf)
    sels, tops = [], []
    eidx = jnp.zeros((tm, LANES), jnp.int32)
    for kk in range(TOP_K):
        m = jnp.max(lg, axis=-1, keepdims=True)
        idx = jnp.min(jnp.where(lg == m, lane, LANES), axis=-1, keepdims=True)
        sel = lane == idx
        lg = jnp.where(sel, -jnp.inf, lg)
        sels.append(sel)
        tops.append(m)
        eidx = jnp.where(lane == kk, idx, eidx)
    exps = [jnp.exp(tv - tops[0]) for tv in tops]
    den = exps[0] + exps[1] + exps[2] + exps[3]
    gates = jnp.zeros((tm, LANES), F32)
    multi = jnp.zeros((tm, LANES), F32)
    for kk in range(TOP_K):
        gates = jnp.where(lane == kk, exps[kk] / den, gates)
        multi = multi + sels[kk].astype(F32)
    ri = lax.broadcasted_iota(jnp.int32, (tm, tm), 0)
    ci = lax.broadcasted_iota(jnp.int32, (tm, tm), 1)
    before = _dot((ri > ci).astype(BF16), multi.astype(BF16)) + run_ref[...]
    rank = jnp.zeros((tm, LANES), jnp.int32)
    for kk in range(TOP_K):
        r = jnp.sum(jnp.where(sels[kk], before, 0.0), axis=-1, keepdims=True)
        rank = jnp.where(lane == kk, r.astype(jnp.int32), rank)
    run = run_ref[...] + jnp.sum(multi, axis=0, keepdims=True)
    run_ref[...] = run
    cnt_ref[...] = run
    eidx_ref[...] = eidx
    gate_ref[...] = gates
    rank_ref[...] = rank


def _router(x2, oa, obt, w_out, norm_w, rw, rb, tm):
    t, d = x2.shape
    half = oa.shape[-1]
    s_words = d // 2 // LANES
    row = lambda i: (i, 0)
    const = lambda i: (0, 0)
    return pl.pallas_call(
        functools.partial(_router_kernel, tm=tm),
        grid=(t // tm,),
        in_specs=[pl.BlockSpec((tm, d), row),
                  pl.BlockSpec((tm, half), row),
                  pl.BlockSpec((half, tm), lambda i: (0, i)),
                  pl.BlockSpec((2 * half, d), const),
                  pl.BlockSpec((1, d), const),
                  pl.BlockSpec((d, LANES), const),
                  pl.BlockSpec((1, LANES), const)],
        out_specs=[pl.BlockSpec((tm, d), row), pl.BlockSpec((tm * s_words, LANES), row),
                   pl.BlockSpec((tm, LANES), row), pl.BlockSpec((tm, LANES), row),
                   pl.BlockSpec((tm, LANES), row), pl.BlockSpec((1, LANES), const)],
        out_shape=[jax.ShapeDtypeStruct((t, d), F32), jax.ShapeDtypeStruct((t * s_words, LANES), jnp.int32),
                   jax.ShapeDtypeStruct((t, LANES), jnp.int32), jax.ShapeDtypeStruct((t, LANES), F32),
                   jax.ShapeDtypeStruct((t, LANES), jnp.int32), jax.ShapeDtypeStruct((1, LANES), F32)],
        scratch_shapes=[pltpu.VMEM((1, LANES), F32)],
        compiler_params=pltpu.CompilerParams(dimension_semantics=("arbitrary",),
                                             vmem_limit_bytes=VMEM_LIMIT),
        name="router",
    )(x2, oa, obt, w_out, norm_w, rw, rb)


SC_ROWS = 64


def _sc_mesh():
    return plsc.VectorSubcoreMesh(core_axis_name="c", subcore_axis_name="s")


def _dispatch(dest_flat, h3, n_pad):
    t, s, _ = h3.shape
    mesh = _sc_mesh()
    n_workers = mesh.num_cores * mesh.num_subcores
    per_worker = t // n_workers
    assert per_worker % SC_ROWS == 0

    @functools.partial(
        pl.kernel, mesh=mesh, out_type=jax.ShapeDtypeStruct((n_pad, s, LANES), h3.dtype),
        scratch_types=[pltpu.VMEM((TOP_K, SC_ROWS), jnp.int32), pltpu.VMEM((SC_ROWS, s, LANES), h3.dtype),
                       pltpu.SemaphoreType.DMA],
        name="sc_dispatch")
    def scatter(h_hbm, dest_hbm, out_hbm, idx_v, rows_v, sem):
        base = (lax.axis_index("s") * mesh.num_cores + lax.axis_index("c")) * per_worker

        @pl.loop(0, per_worker // SC_ROWS)
        def _(i):
            off = pl.multiple_of(base + i * SC_ROWS, SC_ROWS)
            pltpu.sync_copy(h_hbm.at[pl.ds(off, SC_ROWS)], rows_v)
            for kk in range(TOP_K):
                pltpu.sync_copy(dest_hbm.at[pl.ds(kk * t + off, SC_ROWS)], idx_v.at[kk])
            copies = [pltpu.async_copy(rows_v, out_hbm.at[idx_v.at[kk]], sem) for kk in range(TOP_K)]
            for cp in copies:
                cp.wait()

    return scatter(h3, dest_flat)


def _expert_kernel(cnt_ref, blk0_ref, nblk_ref, total_ref, x_hbm, w1_ref, b1g_ref, b1l_ref, w2_ref, b2_ref,
                   y_hbm, w1s_ref, w2s_ref, xbuf, ybuf, xsem, ysem, *, bm):
    e = pl.program_id(0)
    n_blocks = nblk_ref[e]
    first = blk0_ref[e]
    total = total_ref[0]
    rows_per_block = xbuf.shape[1]
    n_tiles = w1s_ref.shape[-1] // MXU_COLS
    half = MXU_COLS // 2

    def x_copy(g, slot):
        return pltpu.make_async_copy(x_hbm.at[pl.ds(pl.multiple_of(g * rows_per_block, rows_per_block),
                                                    rows_per_block)], xbuf.at[slot], xsem.at[slot])

    def y_copy(g, slot):
        return pltpu.make_async_copy(ybuf.at[slot], y_hbm.at[pl.ds(pl.multiple_of(g * rows_per_block,
                                                                                   rows_per_block),
                                                                   rows_per_block)], ysem.at[slot])

    @pl.when(e == 0)
    def _():
        x_copy(0, 0).start()

    @pl.when(n_blocks > 0)
    def _():
        r = lax.broadcasted_iota(jnp.int32, (MXU_COLS, MXU_COLS), 0)
        c = lax.broadcasted_iota(jnp.int32, (MXU_COLS, MXU_COLS), 1)
        perm = (r == jnp.where(c < half, 2 * c, 2 * (c - half) + 1)).astype(BF16)
        for tix in range(n_tiles):
            cols = slice(tix * MXU_COLS, (tix + 1) * MXU_COLS)
            w1s_ref[:, cols] = _dot(w1_ref[0, :, cols].astype(BF16), perm).astype(BF16)
        w2s_ref[...] = w2_ref[0].astype(BF16)

        def block(j, carry):
            g = first + j
            slot = lax.rem(g, 2)
            x_copy(g, slot).wait()

            @pl.when(g + 1 < total)
            def _():
                x_copy(g + 1, 1 - slot).start()

            @pl.when(g >= 2)
            def _():
                y_copy(g - 2, slot).wait()

            n_rows = cnt_ref[e] - j * bm
            words = _load_packed_rows(xbuf, (slot,), bm)
            row = lax.broadcasted_iota(jnp.int32, words.shape, 0)
            x_lo, x_hi = _unpack_bf16_pair(jnp.where(row < n_rows, words, 0))
            x = jnp.concatenate([x_lo.astype(BF16), x_hi.astype(BF16)], axis=1)
            acts = []
            for tix in range(n_tiles):
                hp = _dot(x, w1s_ref[:, tix * MXU_COLS:(tix + 1) * MXU_COLS])
                feat = slice(tix * half, (tix + 1) * half)
                gl = jnp.minimum(hp[:, :half] + b1g_ref[0][:, feat], SWIGLU_LIMIT)
                lin = jnp.clip(hp[:, half:] + b1l_ref[0][:, feat], -SWIGLU_LIMIT, SWIGLU_LIMIT)
                acts.append((gl * _sigmoid(SWIGLU_ALPHA * gl) * (lin + 1.0)).astype(BF16))
            y = _dot(jnp.concatenate(acts, axis=1), w2s_ref[...]) + b2_ref[0]
            d_half = y.shape[-1] // 2
            _store_packed_rows(ybuf.at[slot], _pack_bf16_pair(y[:, :d_half], y[:, d_half:]))
            y_copy(g, slot).start()
            return carry

        lax.fori_loop(0, n_blocks, block, 0)

    @pl.when(e == pl.num_programs(0) - 1)
    def _():
        @pl.when(total >= 2)
        def _():
            y_copy(total - 2, lax.rem(total, 2)).wait()

        y_copy(total - 1, lax.rem(total - 1, 2)).wait()


def _experts(tables, xs, w1, b1g, b1l, w2, b2, bm, d):
    s_words = d // 2 // LANES
    dff2 = w1.shape[-1]
    dff = dff2 // 2
    emap = lambda e, *_: (e, 0, 0)
    grid_spec = pltpu.PrefetchScalarGridSpec(
        num_scalar_prefetch=len(tables),
        grid=(N_EXPERTS,),
        in_specs=[pl.BlockSpec(memory_space=pl.ANY),
                  pl.BlockSpec((1, d, dff2), emap),
                  pl.BlockSpec((1, 1, dff), emap),
                  pl.BlockSpec((1, 1, dff), emap),
                  pl.BlockSpec((1, dff, d), emap),
                  pl.BlockSpec((1, 1, d), emap)],
        out_specs=pl.BlockSpec(memory_space=pl.ANY),
        scratch_shapes=[pltpu.VMEM((d, dff2), BF16), pltpu.VMEM((dff, d), BF16),
                        pltpu.VMEM((2, bm * s_words, LANES), jnp.int32),
                        pltpu.VMEM((2, bm * s_words, LANES), jnp.int32),
                        pltpu.SemaphoreType.DMA((2,)), pltpu.SemaphoreType.DMA((2,))],
    )
    return pl.pallas_call(
        functools.partial(_expert_kernel, bm=bm),
        grid_spec=grid_spec,
        out_shape=jax.ShapeDtypeStruct(xs.shape, jnp.int32),
        compiler_params=pltpu.CompilerParams(dimension_semantics=("arbitrary",),
                                             vmem_limit_bytes=VMEM_LIMIT),
        name="experts",
    )(*tables, xs, w1, b1g, b1l, w2, b2)


def _gather_rows(dest_flat, y3):
    n_out = dest_flat.shape[0]
    _, s, _ = y3.shape
    mesh = _sc_mesh()
    n_workers = mesh.num_cores * mesh.num_subcores
    per_worker = n_out // n_workers
    n_chunks = per_worker // SC_ROWS
    assert per_worker % SC_ROWS == 0 and n_chunks % 2 == 0

    @functools.partial(
        pl.kernel, mesh=mesh, out_type=jax.ShapeDtypeStruct((n_out, s, LANES), y3.dtype),
        scratch_types=[pltpu.VMEM((2, SC_ROWS), jnp.int32), pltpu.VMEM((2, SC_ROWS, s, LANES), y3.dtype),
                       pltpu.SemaphoreType.DMA((2,)), pltpu.SemaphoreType.DMA((2,))],
        name="sc_gather")
    def gather(y_hbm, dest_hbm, out_hbm, idx_v, rows_v, gsem, wsem):
        base = (lax.axis_index("s") * mesh.num_cores + lax.axis_index("c")) * per_worker

        def rows_of(chunk):
            return pl.ds(pl.multiple_of(base + chunk * SC_ROWS, SC_ROWS), SC_ROWS)

        def gather_copy(slot):
            return pltpu.make_async_copy(y_hbm.at[idx_v.at[slot]], rows_v.at[slot], gsem.at[slot])

        def write_copy(chunk, slot):
            return pltpu.make_async_copy(rows_v.at[slot], out_hbm.at[rows_of(chunk)], wsem.at[slot])

        def start_gather(chunk, slot):
            pltpu.sync_copy(dest_hbm.at[rows_of(chunk)], idx_v.at[slot])
            gather_copy(slot).start()

        start_gather(0, 0)

        @pl.loop(0, n_chunks // 2)
        def _(pair):
            for slot in (0, 1):
                chunk = 2 * pair + slot
                gather_copy(slot).wait()
                write_copy(chunk, slot).start()

                @pl.when(chunk >= 1)
                def _():
                    write_copy(chunk - 1, 1 - slot).wait()

                @pl.when(chunk + 1 < n_chunks)
                def _():
                    start_gather(chunk + 1, 1 - slot)

        write_copy(n_chunks - 1, 1).wait()

    return gather(y3, dest_flat)


def _combine_kernel(x1_ref, gate_ref, y4_ref, nw_ref, o_ref, *, tm, normalize):
    x = x1_ref[...]
    gates = gate_ref[...]
    for kk in range(TOP_K):
        lo, hi = _unpack_bf16_pair(_load_packed_rows(y4_ref, (kk,), tm))
        x = x + gates[:, kk:kk + 1] * jnp.concatenate([lo, hi], axis=1)
    if normalize:
        x = x * lax.rsqrt(jnp.mean(x * x, axis=-1, keepdims=True) + EPS) * nw_ref[...]
    o_ref[...] = x


def _combine(x1, gates, y4, norm_w, tm, normalize):
    t, d = x1.shape
    s_words = d // 2 // LANES
    row = lambda i: (i, 0)
    return pl.pallas_call(
        functools.partial(_combine_kernel, tm=tm, normalize=normalize),
        grid=(t // tm,),
        in_specs=[pl.BlockSpec((tm, d), row), pl.BlockSpec((tm, LANES), row),
                  pl.BlockSpec((TOP_K, tm * s_words, LANES), lambda i: (0, i, 0)),
                  pl.BlockSpec((1, d), lambda i: (0, 0))],
        out_specs=pl.BlockSpec((tm, d), row),
        out_shape=jax.ShapeDtypeStruct((t, d), F32),
        compiler_params=pltpu.CompilerParams(dimension_semantics=("parallel",)),
        name="combine",
    )(x1, gates, y4, norm_w)


def _pad_cols(w, n):
    return jnp.pad(w, ((0, 0), (0, n - w.shape[1])))


def _rot_half(w):
    half = w.shape[-1] // 2
    return jnp.concatenate([-w[..., half:], w[..., :half]], axis=-1)


def _prep_inproj_weight(w_in):
    d = w_in.shape[0]
    hd = GDN_HEADS * GDN_D
    o = 0
    parts = {}
    for name, n in (("qkv", 3 * hd), ("z", hd), ("a", GDN_HEADS), ("b", GDN_HEADS),
                    ("cq", MLA_Q_LORA), ("ckv", MLA_KV_LORA), ("kr", MLA_ROPE)):
        parts[name] = w_in[:, o:o + n]
        o += n
    zeros = lambda n: jnp.zeros((d, n), w_in.dtype)
    kr_tile = jnp.concatenate([zeros(MLA_NOPE), parts["kr"], zeros(LANES - MLA_NOPE - MLA_ROPE)], axis=1)
    krr_tile = jnp.concatenate([zeros(MLA_NOPE), _rot_half(parts["kr"]),
                                zeros(LANES - MLA_NOPE - MLA_ROPE)], axis=1)
    ab_tile = jnp.concatenate([parts["a"], parts["b"], zeros(LANES - 2 * GDN_HEADS)], axis=1)
    return jnp.concatenate([parts["qkv"], parts["z"], parts["cq"], parts["ckv"], kr_tile, krr_tile,
                            ab_tile], axis=1).astype(BF16)


def _prep_mla_weights(w_uq, w_ukv):
    dq = MLA_NOPE + MLA_ROPE
    wq = w_uq.reshape(MLA_Q_LORA, MLA_HEADS, dq)
    wq_rot = jnp.concatenate([jnp.zeros_like(wq[..., :MLA_NOPE]), _rot_half(wq[..., MLA_NOPE:])], axis=-1)
    pad = lambda w: jnp.pad(w, ((0, 0), (0, 0), (0, LANES - w.shape[-1]))).reshape(w.shape[0], C_HEADS)
    wkv = w_ukv.reshape(MLA_KV_LORA, MLA_HEADS, MLA_NOPE + MLA_V)
    wv = wkv[..., MLA_NOPE:].reshape(MLA_KV_LORA, C_V)
    return (pad(wq).T.astype(BF16), pad(wq_rot).T.astype(BF16), pad(wkv[..., :MLA_NOPE]).astype(BF16),
            wv.T.astype(BF16))


def _rope_freq():
    half = MLA_ROPE // 2
    return (ROPE_THETA ** (-jnp.arange(half, dtype=F32) / half)).reshape(half, 1)


def _lane_row(v):
    return jnp.pad(v.astype(F32), (0, LANES - v.shape[0])).reshape(1, LANES)


def _routing_tables(cnt, eidx, rank, bm):
    e_ids = jnp.arange(N_EXPERTS, dtype=jnp.int32)
    counts = cnt[0, :N_EXPERTS].astype(jnp.int32)
    padded = (counts + bm - 1) // bm * bm
    pend = jnp.sum(jnp.where(e_ids[None, :] <= e_ids[:, None], padded[None, :], 0), axis=1)
    pstart = pend - padded
    dest = jnp.sum(jnp.where(eidx[:, :TOP_K, None] == e_ids, pstart, 0), axis=-1) + rank[:, :TOP_K]
    tables = (counts, pstart // bm, padded // bm, pend[-1:] // bm)
    return dest, tuple(t.astype(jnp.int32) for t in tables)


def kernel(x, positions, norm_mix_w, w_in, conv_w, a_log, dt_bias, gdn_norm_w, q_norm_w, w_uq, kv_norm_w,
           w_ukv, mla_out_norm_w, w_out, norm_ffn_w, router_w, router_b, w1, b1, w2, b2, norm_final_w):
    bsz, seq, d = x.shape
    t = bsz * seq
    depth = w_in.shape[0]
    tm = min(512, t)
    tr = min(256, t)
    bm = 512
    tq = min(512, seq)
    x2 = x.reshape(t, d)
    posf = positions.astype(F32)
    freq = _rope_freq()
    for l in range(depth):
        qkv, z, cq, ckv, kr2, ab = _inproj(x2, norm_mix_w[l].reshape(1, d), _prep_inproj_weight(w_in[l]), tm)
        conv_w8 = jnp.pad(conv_w[l], ((0, 8 - GDN_CONV), (0, 0)))
        o_a = _gdn(qkv, z, ab, conv_w8, _lane_row(a_log[l]), _lane_row(dt_bias[l]),
                   gdn_norm_w[l].reshape(1, GDN_D), bsz, seq, min(256, seq))
        wqt, wqrt, wk, wvt = _prep_mla_weights(w_uq[l], w_ukv[l])
        qt, k, vt = _mla_proj(cq, ckv, kr2, posf.reshape(1, t),
                              q_norm_w[l].reshape(1, -1), kv_norm_w[l].reshape(1, -1),
                              wqt, wqrt, wk, wvt, freq, tq)
        o_bt = _attention(qt, k, vt, mla_out_norm_w[l].reshape(-1, 1), bsz, seq, tq)
        rw = _pad_cols(router_w[l], LANES)
        rb = _lane_row(router_b[l])
        x1, hn, eidx, gates, rank, cnt = _router(x2, o_a, o_bt, w_out[l].astype(BF16),
                                                 norm_ffn_w[l].reshape(1, d), rw, rb, tr)
        n_pad = t * TOP_K + N_EXPERTS * bm
        dest, tables = _routing_tables(cnt, eidx, rank, bm)
        dest_flat = dest.T.reshape(TOP_K * t)
        s_words = d // 2 // LANES
        buf = _dispatch(dest_flat, hn.reshape(t, s_words, LANES), n_pad)
        yb = _experts(tables, buf.reshape(n_pad * s_words, LANES), w1[l],
                      b1[l][:, None, 0::2], b1[l][:, None, 1::2], w2[l], b2[l][:, None, :], bm, d)
        y4 = _gather_rows(dest_flat, yb.reshape(n_pad, s_words, LANES))
        x2 = _combine(x1, gates, y4.reshape(TOP_K, t * s_words, LANES), norm_final_w.reshape(1, d), tr,
                      normalize=(l == depth - 1))
    return x2.reshape(bsz, seq, d)
```

```python
import functools
import math

import jax
import jax.numpy as jnp
from jax import lax
from jax.experimental import pallas as pl
from jax.experimental.pallas import tpu as pltpu
from jax.experimental.pallas import tpu_sc as plsc

F32 = jnp.float32
BF16 = jnp.bfloat16
HIGHEST = lax.Precision.HIGHEST

LANES = 128
MXU_COLS = 256
EPS = 1e-6
CHUNK = 64

GDN_HEADS = 4
GDN_D = 128
GDN_CONV = 4
MLA_HEADS = 8
MLA_NOPE = 64
MLA_ROPE = 32
MLA_V = 64
MLA_Q_LORA = 384
MLA_KV_LORA = 256
ROPE_THETA = 10000.0
N_EXPERTS = 32
TOP_K = 4
SWIGLU_LIMIT = 7.0
SWIGLU_ALPHA = 1.702

NEG_BIG = -1e30
VMEM_LIMIT = 52 * 1024 * 1024


def _sigmoid(x):
    return 1.0 / (1.0 + jnp.exp(-x))


def _softplus(x):
    return jnp.maximum(x, 0.0) + jnp.log(1.0 + jnp.exp(-jnp.abs(x)))


def _dot(a, b, precision=None):
    return jnp.dot(a, b, preferred_element_type=F32, precision=precision)


def _dot_nt(a, b, precision=None):
    return lax.dot_general(a, b, (((1,), (1,)), ((), ())), preferred_element_type=F32,
                           precision=precision)


def _dot_tn(a, b):
    return lax.dot_general(a, b, (((0,), (0,)), ((), ())), preferred_element_type=F32)


def _split_bf16(a):
    hi = a.astype(BF16)
    return hi, (a - hi.astype(F32)).astype(BF16)


def _pack_bf16_pair(lo, hi):
    lo_bits = pltpu.bitcast(lo.astype(BF16).astype(F32), jnp.int32)
    hi_bits = pltpu.bitcast(hi.astype(BF16).astype(F32), jnp.int32)
    return jnp.bitwise_or(hi_bits, lax.shift_right_logical(lo_bits, 16))


def _unpack_bf16_pair(w):
    lo = pltpu.bitcast(lax.shift_left(w, 16), F32)
    hi = pltpu.bitcast(jnp.bitwise_and(w, -65536), F32)
    return lo, hi


def _load_packed_rows(ref, lead, m):
    s = ref.shape[-2] // m
    return jnp.concatenate([ref[(*lead, pl.ds(j, m, stride=s), slice(None))] for j in range(s)], axis=1)


def _store_packed_rows(ref, words):
    m = words.shape[0]
    s = ref.shape[0] // m
    for j in range(s):
        ref[pl.ds(j, m, stride=s), :] = words[:, j * LANES:(j + 1) * LANES]


def _dot_split_nt(a, b):
    return _dot_nt(a[0], b[0]) + _dot_nt(a[0], b[1]) + _dot_nt(a[1], b[0])


def _dot_split(a, b):
    return _dot(a[0], b[0]) + _dot(a[0], b[1]) + _dot(a[1], b[0])


C_QKV = 3 * GDN_HEADS * GDN_D
C_Z = GDN_HEADS * GDN_D
C_IN = C_QKV + C_Z + MLA_Q_LORA + MLA_KV_LORA + 3 * LANES


def _inproj_kernel(x_ref, nw_ref, w_ref, qkv_ref, z_ref, cq_ref, ckv_ref, kr_ref, ab_ref):
    x = x_ref[...]
    var = jnp.mean(x * x, axis=-1, keepdims=True)
    h = (x * lax.rsqrt(var + EPS) * nw_ref[...]).astype(BF16)
    p = _dot(h, w_ref[...])
    o = 0
    for ref in (qkv_ref, z_ref, cq_ref, ckv_ref, kr_ref, ab_ref):
        n = ref.shape[-1]
        ref[...] = p[:, o:o + n]
        o += n


def _inproj(x2, norm_w, w_all, tm):
    t, d = x2.shape
    widths = (C_QKV, C_Z, MLA_Q_LORA, MLA_KV_LORA, 2 * LANES, LANES)
    return pl.pallas_call(
        _inproj_kernel,
        grid=(t // tm,),
        in_specs=[pl.BlockSpec((tm, d), lambda i: (i, 0)),
                  pl.BlockSpec((1, d), lambda i: (0, 0)),
                  pl.BlockSpec((d, C_IN), lambda i: (0, 0))],
        out_specs=[pl.BlockSpec((tm, n), lambda i: (i, 0)) for n in widths],
        out_shape=[jax.ShapeDtypeStruct((t, n), F32) for n in widths],
        compiler_params=pltpu.CompilerParams(dimension_semantics=("parallel",),
                                             vmem_limit_bytes=VMEM_LIMIT),
        name="inproj",
    )(x2, norm_w, w_all)


def _gdn_kernel(qkv_ref, z_ref, ab_ref, cw_ref, alog_ref, dtb_ref, nw_ref, o_ref, ext_ref, s_ref, *, rows):
    c = CHUNK
    n_chunks = rows // c

    @pl.when(pl.program_id(1) == 0)
    def _():
        ext_ref[0:8, :] = jnp.zeros((8, C_QKV), F32)
        s_ref[...] = jnp.zeros_like(s_ref)

    x = qkv_ref[...]
    ext_ref[8:8 + rows, :] = x
    cw = cw_ref[...]
    ext = ext_ref[...]
    xc = cw[3:4] * x
    for j in range(1, GDN_CONV):
        xc = xc + cw[3 - j:4 - j] * pltpu.roll(ext, j, axis=0)[8:8 + rows, :]
    ext_ref[0:8, :] = x[rows - 8:rows, :]
    xc = xc * _sigmoid(xc)

    ab = ab_ref[...]
    g_all = -jnp.exp(alog_ref[...]) * _softplus(ab + dtb_ref[...])
    beta_all = _sigmoid(ab)
    rr = lax.broadcasted_iota(jnp.int32, (rows, rows), 0)
    rc = lax.broadcasted_iota(jnp.int32, (rows, rows), 1)
    in_chunk_prefix = jnp.where(rr // c == rc // c, jnp.where(rr >= rc, 1.0, 0.0), 0.0)
    g_cum = _dot(in_chunk_prefix, g_all, HIGHEST)
    e8 = (lax.broadcasted_iota(jnp.int32, (8, LANES), 0)
          == lax.broadcasted_iota(jnp.int32, (8, LANES), 1)).astype(F32)
    g_row_all = _dot_nt(e8, g_cum, HIGHEST)
    ri = lax.broadcasted_iota(jnp.int32, (c, c), 0)
    ci = lax.broadcasted_iota(jnp.int32, (c, c), 1)
    causal = ri >= ci
    strict = ri > ci
    eye = (ri == ci).astype(F32)
    nw = nw_ref[...]

    chains = [(n, h) for n in range(n_chunks) for h in range(GDN_HEADS)]
    qs, ks, vs, gcs, betas, decays = [], [], [], [], [], []
    for n, h in chains:
        rs = slice(n * c, (n + 1) * c)
        q = xc[rs, h * GDN_D:(h + 1) * GDN_D]
        k = xc[rs, (GDN_HEADS + h) * GDN_D:(GDN_HEADS + h + 1) * GDN_D]
        qs.append(q * lax.rsqrt(jnp.sum(q * q, axis=-1, keepdims=True) + EPS) * (GDN_D ** -0.5))
        ks.append(k * lax.rsqrt(jnp.sum(k * k, axis=-1, keepdims=True) + EPS))
        vs.append(xc[rs, (2 * GDN_HEADS + h) * GDN_D:(2 * GDN_HEADS + h + 1) * GDN_D])
        gc = g_cum[rs, h:h + 1]
        gr = g_row_all[h:h + 1, rs]
        gcs.append(gc)
        betas.append(beta_all[rs, GDN_HEADS + h:GDN_HEADS + h + 1])
        decays.append(jnp.exp(jnp.where(causal, gc - gr, -jnp.inf)))
    kbs = [k * b for k, b in zip(ks, betas)]
    k16s = [k.astype(BF16) for k in ks]
    a_mats = [jnp.where(strict, _dot_nt(kb.astype(BF16), k16) * dec, 0.0)
              for kb, k16, dec in zip(kbs, k16s, decays)]
    t_mats = [eye - a for a in a_mats]
    pw_s = [_split_bf16(a) for a in a_mats]
    for _ in range(5):
        pw_s = [_split_bf16(_dot_split(p, p)) for p in pw_s]
        t_s = [_split_bf16(t) for t in t_mats]
        t_mats = [t + _dot_split(ts, p) for t, ts, p in zip(t_mats, t_s, pw_s)]
    e_gs = [jnp.exp(gc) for gc in gcs]
    uws = [_dot(t.astype(BF16), jnp.concatenate([v * b, kb * eg], axis=1).astype(BF16))
           for t, v, b, kb, eg in zip(t_mats, vs, betas, kbs, e_gs)]
    qk16s = [(_dot_nt(q.astype(BF16), k16) * dec).astype(BF16) for q, k16, dec in zip(qs, k16s, decays)]
    qd16s = [(q * eg).astype(BF16) for q, eg in zip(qs, e_gs)]
    kd16s = [(k * jnp.exp(gc[c - 1:c, :] - gc)).astype(BF16) for k, gc in zip(ks, gcs)]
    g_tots = [jnp.exp(gc[c - 1:c, :]) for gc in gcs]

    heads = range(GDN_HEADS)
    states = [s_ref[h] for h in heads]
    for n in range(n_chunks):
        rs = slice(n * c, (n + 1) * c)
        ix = [n * GDN_HEADS + h for h in heads]
        s16s = [states[h].astype(BF16) for h in heads]
        v16s = [(uws[i][:, :GDN_D] - _dot(uws[i][:, GDN_D:].astype(BF16), s16s[h])).astype(BF16)
                for h, i in zip(heads, ix)]
        states = [states[h] * g_tots[i] + _dot_tn(kd16s[i], v16s[h]) for h, i in zip(heads, ix)]
        os_ = [_dot(qd16s[i], s16s[h]) + _dot(qk16s[i], v16s[h]) for h, i in zip(heads, ix)]
        for h in heads:
            o = os_[h]
            on = o * lax.rsqrt(jnp.mean(o * o, axis=-1, keepdims=True) + EPS) * nw
            zh = z_ref[rs, h * GDN_D:(h + 1) * GDN_D]
            o_ref[rs, h * GDN_D:(h + 1) * GDN_D] = on * (zh * _sigmoid(zh))
    for h in heads:
        s_ref[h] = states[h]


def _gdn(qkv, z, ab, conv_w8, alog_row, dtb_row, norm_w, bsz, seq, rows):
    n = seq // rows
    row = lambda b, i: (b * n + i, 0)
    const = lambda b, i: (0, 0)
    return pl.pallas_call(
        functools.partial(_gdn_kernel, rows=rows),
        grid=(bsz, n),
        in_specs=[pl.BlockSpec((rows, C_QKV), row),
                  pl.BlockSpec((rows, C_Z), row),
                  pl.BlockSpec((rows, LANES), row),
                  pl.BlockSpec((8, C_QKV), const),
                  pl.BlockSpec((1, LANES), const),
                  pl.BlockSpec((1, LANES), const),
                  pl.BlockSpec((1, GDN_D), const)],
        out_specs=pl.BlockSpec((rows, C_Z), row),
        out_shape=jax.ShapeDtypeStruct((bsz * seq, C_Z), F32),
        scratch_shapes=[pltpu.VMEM((8 + rows, C_QKV), F32),
                        pltpu.VMEM((GDN_HEADS, GDN_D, GDN_D), F32)],
        compiler_params=pltpu.CompilerParams(dimension_semantics=("arbitrary", "arbitrary"),
                                             vmem_limit_bytes=VMEM_LIMIT),
        name="gdn",
    )(qkv, z, ab, conv_w8, alog_row, dtb_row, norm_w)


D_QK_PAD = LANES
C_HEADS = MLA_HEADS * D_QK_PAD
C_V = MLA_HEADS * MLA_V
V_AUG = MLA_V + 16
C_VAUG = MLA_HEADS * V_AUG
Q_SCALE = (MLA_NOPE + MLA_ROPE) ** -0.5 * math.log2(math.e)


def _mla_proj_kernel(cq_ref, ckv_ref, kr_ref, posr_ref, qnw_ref, kvnw_ref, wqt_ref, wqrt_ref,
                     wk_ref, wvt_ref, freqc_ref, qt_ref, k_ref, vt_ref):
    cq = cq_ref[...]
    cqn = (cq * lax.rsqrt(jnp.mean(cq * cq, axis=-1, keepdims=True) + EPS) * qnw_ref[...]).astype(BF16)
    ckv = ckv_ref[...]
    ckvn = (ckv * lax.rsqrt(jnp.mean(ckv * ckv, axis=-1, keepdims=True) + EPS)
            * kvnw_ref[...]).astype(BF16)
    tm = posr_ref.shape[-1]
    ang = freqc_ref[...] * posr_ref[...]
    c16 = jnp.cos(ang)
    s16 = jnp.sin(ang)
    pad = LANES - MLA_NOPE - MLA_ROPE
    cs_t = jnp.concatenate([jnp.ones((MLA_NOPE, tm), F32), c16, c16, jnp.ones((pad, tm), F32)], axis=0)
    sn_t = jnp.concatenate([jnp.zeros((MLA_NOPE, tm), F32), s16, s16, jnp.zeros((pad, tm), F32)], axis=0)
    cs = cs_t.T
    sn = sn_t.T
    qa_t = _dot_nt(wqt_ref[...], cqn)
    qb_t = _dot_nt(wqrt_ref[...], cqn)
    kn = _dot(ckvn, wk_ref[...])
    kr = kr_ref[...]
    kpe = kr[:, :LANES] * cs + kr[:, LANES:] * sn
    for h in range(MLA_HEADS):
        sl = slice(h * LANES, (h + 1) * LANES)
        qt_ref[sl, :] = ((qa_t[sl, :] * cs_t + qb_t[sl, :] * sn_t) * Q_SCALE).astype(BF16)
        k_ref[:, sl] = (kn[:, sl] + kpe).astype(BF16)
    v_t = _dot_nt(wvt_ref[...], ckvn).astype(BF16)
    ones = jnp.ones((V_AUG - MLA_V, v_t.shape[1]), BF16)
    for h in range(MLA_HEADS):
        vt_ref[0, h * V_AUG:h * V_AUG + MLA_V, :] = v_t[h * MLA_V:(h + 1) * MLA_V, :]
        vt_ref[0, h * V_AUG + MLA_V:(h + 1) * V_AUG, :] = ones


def _mla_proj(cq, ckv, kr2, posr, qnw, kvnw, wqt, wqrt, wk, wvt, freqc, tm):
    t = cq.shape[0]
    row = lambda i: (i, 0)
    col = lambda i: (0, i)
    const = lambda i: (0, 0)
    return pl.pallas_call(
        _mla_proj_kernel,
        grid=(t // tm,),
        in_specs=[pl.BlockSpec((tm, MLA_Q_LORA), row),
                  pl.BlockSpec((tm, MLA_KV_LORA), row),
                  pl.BlockSpec((tm, 2 * LANES), row),
                  pl.BlockSpec((1, tm), col),
                  pl.BlockSpec((1, MLA_Q_LORA), const),
                  pl.BlockSpec((1, MLA_KV_LORA), const),
                  pl.BlockSpec((C_HEADS, MLA_Q_LORA), const),
                  pl.BlockSpec((C_HEADS, MLA_Q_LORA), const),
                  pl.BlockSpec((MLA_KV_LORA, C_HEADS), const),
                  pl.BlockSpec((C_V, MLA_KV_LORA), const),
                  pl.BlockSpec((MLA_ROPE // 2, 1), const)],
        out_specs=[pl.BlockSpec((C_HEADS, tm), col),
                   pl.BlockSpec((tm, C_HEADS), row),
                   pl.BlockSpec((1, C_VAUG, tm), lambda i: (i, 0, 0))],
        out_shape=[jax.ShapeDtypeStruct((C_HEADS, t), BF16),
                   jax.ShapeDtypeStruct((t, C_HEADS), BF16),
                   jax.ShapeDtypeStruct((t // tm, C_VAUG, tm), BF16)],
        compiler_params=pltpu.CompilerParams(dimension_semantics=("parallel",),
                                             vmem_limit_bytes=VMEM_LIMIT),
        name="mla_proj",
    )(cq, ckv, kr2, posr, qnw, kvnw, wqt, wqrt, wk, wvt, freqc)


def _attn_kernel(qt_ref, k_ref, vt_ref, nw_ref, ot_ref, m_ref, acc_ref, *, tq):
    i = pl.program_id(1)
    m_ref[...] = jnp.full(m_ref.shape, NEG_BIG, F32)
    acc_ref[...] = jnp.zeros(acc_ref.shape, F32)
    key_chunk = lax.broadcasted_iota(jnp.int32, (tq, tq), 0) // CHUNK
    qry_chunk = lax.broadcasted_iota(jnp.int32, (tq, tq), 1) // CHUNK
    diag_mask = key_chunk <= qry_chunk

    def tile(j, masked):
        rows = pl.ds(pl.multiple_of(j * tq, tq), tq)
        scores = []
        for h in range(MLA_HEADS):
            hs = slice(h * LANES, (h + 1) * LANES)
            s_t = _dot(k_ref[rows, hs], qt_ref[hs, :])
            scores.append(jnp.where(diag_mask, s_t, NEG_BIG) if masked else s_t)
        probs, alphas = [], []
        for h in range(MLA_HEADS):
            m_old = m_ref[h]
            m_new = jnp.maximum(m_old, jnp.max(scores[h], axis=0, keepdims=True))
            m_ref[h] = m_new
            probs.append(jnp.exp2(scores[h] - m_new[0:1, :]).astype(BF16))
            alphas.append(jnp.exp2(m_old[0:1, :] - m_new[0:1, :]))
        for h in range(MLA_HEADS):
            v_aug = vt_ref[j, h * V_AUG:(h + 1) * V_AUG, :]
            acc_ref[h] = alphas[h] * acc_ref[h] + _dot(v_aug, probs[h])

    def full_tile(j, carry):
        tile(j, False)
        return carry

    lax.fori_loop(0, i, full_tile, 0)
    tile(i, True)

    outs = []
    for h in range(MLA_HEADS):
        acc = acc_ref[h]
        outs.append(acc[:MLA_V, :] / acc[MLA_V:MLA_V + 1, :])
    ssq = outs[0] * outs[0]
    for o in outs[1:]:
        ssq = ssq + o * o
    inv = lax.rsqrt(jnp.sum(ssq, axis=0, keepdims=True) / C_V + EPS)
    for h in range(MLA_HEADS):
        vs = slice(h * MLA_V, (h + 1) * MLA_V)
        ot_ref[vs, :] = outs[h] * inv * nw_ref[vs, :]


def _attention(qt, k, vt, norm_w_col, bsz, seq, tq):
    nq = seq // tq
    return pl.pallas_call(
        functools.partial(_attn_kernel, tq=tq),
        grid=(bsz, nq),
        in_specs=[pl.BlockSpec((C_HEADS, tq), lambda b, i: (0, b * nq + i)),
                  pl.BlockSpec((seq, C_HEADS), lambda b, i: (b, 0)),
                  pl.BlockSpec((nq, C_VAUG, tq), lambda b, i: (b, 0, 0)),
                  pl.BlockSpec((C_V, 1), lambda b, i: (0, 0))],
        out_specs=pl.BlockSpec((C_V, tq), lambda b, i: (0, b * nq + i)),
        out_shape=jax.ShapeDtypeStruct((C_V, bsz * seq), F32),
        scratch_shapes=[pltpu.VMEM((MLA_HEADS, 8, tq), F32),
                        pltpu.VMEM((MLA_HEADS, V_AUG, tq), F32)],
        compiler_params=pltpu.CompilerParams(dimension_semantics=("parallel", "parallel"),
                                             vmem_limit_bytes=VMEM_LIMIT),
        name="attention",
    )(qt, k, vt, norm_w_col)


def _router_kernel(x_ref, oa_ref, obt_ref, wo_ref, nw_ref, rw_ref, rb_ref,
                   x1_ref, h_ref, eidx_ref, gate_ref, rank_ref, cnt_ref, run_ref, *, tm):
    @pl.when(pl.program_id(0) == 0)
    def _():
        run_ref[...] = jnp.zeros_like(run_ref)

    half = oa_ref.shape[-1]
    y = (_dot(oa_ref[...].astype(BF16), wo_ref[0:half, :])
         + _dot_tn(obt_ref[...].astype(BF16), wo_ref[half:, :]))
    x1 = x_ref[...] + y
    x1_ref[...] = x1
    hn = x1 * lax.rsqrt(jnp.mean(x1 * x1, axis=-1, keepdims=True) + EPS) * nw_ref[...]
    d_half = hn.shape[-1] // 2
    _store_packed_rows(h_ref, _pack_bf16_pair(hn[:, :d_half], hn[:, d_half:]))
    lg = _dot_split_nt(_split_bf16(rw_ref[...]), _split_bf16(hn)) + rb_ref[...]
    expert = lax.broadcasted_iota(jnp.int32, lg.shape, 0)
    sels, tops, idxs = [], [], []
    for _ in range(TOP_K):
        m = jnp.max(lg, axis=0, keepdims=True)
        idx = jnp.min(jnp.where(lg == m, expert, N_EXPERTS), axis=0, keepdims=True)
        sel = expert == idx
        lg = jnp.where(sel, -jnp.inf, lg)
        sels.append(sel)
        tops.append(m)
        idxs.append(idx)
    exps = [jnp.exp(tv - tops[0]) for tv in tops]
    den = exps[0] + exps[1] + exps[2] + exps[3]
    multi = sels[0].astype(F32)
    for sel in sels[1:]:
        multi = multi + sel.astype(F32)
    ri = lax.broadcasted_iota(jnp.int32, (tm, tm), 0)
    ci = lax.broadcasted_iota(jnp.int32, (tm, tm), 1)
    before = _dot(multi.astype(BF16), (ri < ci).astype(BF16)) + run_ref[...]
    ranks = [jnp.sum(jnp.where(sel, before, 0.0), axis=0, keepdims=True).astype(jnp.int32) for sel in sels]
    run = run_ref[...] + jnp.sum(multi, axis=1, keepdims=True)
    run_ref[...] = run
    cnt_ref[...] = jnp.broadcast_to(run, cnt_ref.shape)
    pad_i = jnp.zeros((8 - TOP_K, tm), jnp.int32)
    eidx_ref[...] = jnp.concatenate(idxs + [pad_i], axis=0)
    rank_ref[...] = jnp.concatenate(ranks + [pad_i], axis=0)
    gates_t = jnp.concatenate([e / den for e in exps] + [jnp.zeros((LANES - TOP_K, tm), F32)], axis=0)
    eye = (ri == ci).astype(BF16)
    g_hi, g_lo = _split_bf16(gates_t)
    gate_ref[...] = _dot_nt(eye, g_hi) + _dot_nt(eye, g_lo)


def _router(x2, oa, obt, w_out, norm_w, rw, rb, tm):
    t, d = x2.shape
    half = oa.shape[-1]
    s_words = d // 2 // LANES
    row = lambda i: (i, 0)
    col = lambda i: (0, i)
    const = lambda i: (0, 0)
    return pl.pallas_call(
        functools.partial(_router_kernel, tm=tm),
        grid=(t // tm,),
        in_specs=[pl.BlockSpec((tm, d), row),
                  pl.BlockSpec((tm, half), row),
                  pl.BlockSpec((half, tm), lambda i: (0, i)),
                  pl.BlockSpec((2 * half, d), const),
                  pl.BlockSpec((1, d), const),
                  pl.BlockSpec((N_EXPERTS, d), const),
                  pl.BlockSpec((N_EXPERTS, 1), const)],
        out_specs=[pl.BlockSpec((tm, d), row), pl.BlockSpec((tm * s_words, LANES), row),
                   pl.BlockSpec((8, tm), col), pl.BlockSpec((tm, LANES), row),
                   pl.BlockSpec((8, tm), col), pl.BlockSpec((N_EXPERTS, LANES), const)],
        out_shape=[jax.ShapeDtypeStruct((t, d), F32), jax.ShapeDtypeStruct((t * s_words, LANES), jnp.int32),
                   jax.ShapeDtypeStruct((8, t), jnp.int32), jax.ShapeDtypeStruct((t, LANES), F32),
                   jax.ShapeDtypeStruct((8, t), jnp.int32), jax.ShapeDtypeStruct((N_EXPERTS, LANES), F32)],
        scratch_shapes=[pltpu.VMEM((N_EXPERTS, 1), F32)],
        compiler_params=pltpu.CompilerParams(dimension_semantics=("arbitrary",),
                                             vmem_limit_bytes=VMEM_LIMIT),
        name="router",
    )(x2, oa, obt, w_out, norm_w, rw, rb)


SC_ROWS = 64


def _sc_mesh():
    return plsc.VectorSubcoreMesh(core_axis_name="c", subcore_axis_name="s")


def _dispatch(dest_flat, h3, n_pad):
    t, s, _ = h3.shape
    mesh = _sc_mesh()
    n_workers = mesh.num_cores * mesh.num_subcores
    per_worker = t // n_workers
    assert per_worker % SC_ROWS == 0

    @functools.partial(
        pl.kernel, mesh=mesh, out_type=jax.ShapeDtypeStruct((n_pad, s, LANES), h3.dtype),
        scratch_types=[pltpu.VMEM((TOP_K, SC_ROWS), jnp.int32), pltpu.VMEM((SC_ROWS, s, LANES), h3.dtype),
                       pltpu.SemaphoreType.DMA],
        name="sc_dispatch")
    def scatter(h_hbm, dest_hbm, out_hbm, idx_v, rows_v, sem):
        base = (lax.axis_index("s") * mesh.num_cores + lax.axis_index("c")) * per_worker

        @pl.loop(0, per_worker // SC_ROWS)
        def _(i):
            off = pl.multiple_of(base + i * SC_ROWS, SC_ROWS)
            pltpu.sync_copy(h_hbm.at[pl.ds(off, SC_ROWS)], rows_v)
            for kk in range(TOP_K):
                pltpu.sync_copy(dest_hbm.at[pl.ds(kk * t + off, SC_ROWS)], idx_v.at[kk])
            copies = [pltpu.async_copy(rows_v, out_hbm.at[idx_v.at[kk]], sem) for kk in range(TOP_K)]
            for cp in copies:
                cp.wait()

    return scatter(h3, dest_flat)


def _expert_kernel(cnt_ref, blk0_ref, nblk_ref, total_ref, x_hbm, w1_ref, b1g_ref, b1l_ref, w2_ref, b2_ref,
                   y_hbm, w1s_ref, w2s_ref, xbuf, ybuf, xsem, ysem, *, bm):
    e = pl.program_id(0)
    n_blocks = nblk_ref[e]
    first = blk0_ref[e]
    total = total_ref[0]
    rows_per_block = xbuf.shape[1]
    n_tiles = w1s_ref.shape[-1] // MXU_COLS
    half = MXU_COLS // 2

    def x_copy(g, slot):
        return pltpu.make_async_copy(x_hbm.at[pl.ds(pl.multiple_of(g * rows_per_block, rows_per_block),
                                                    rows_per_block)], xbuf.at[slot], xsem.at[slot])

    def y_copy(g, slot):
        return pltpu.make_async_copy(ybuf.at[slot], y_hbm.at[pl.ds(pl.multiple_of(g * rows_per_block,
                                                                                   rows_per_block),
                                                                   rows_per_block)], ysem.at[slot])

    @pl.when(e == 0)
    def _():
        x_copy(0, 0).start()

    @pl.when(n_blocks > 0)
    def _():
        r = lax.broadcasted_iota(jnp.int32, (MXU_COLS, MXU_COLS), 0)
        c = lax.broadcasted_iota(jnp.int32, (MXU_COLS, MXU_COLS), 1)
        perm = (r == jnp.where(c < half, 2 * c, 2 * (c - half) + 1)).astype(BF16)
        for tix in range(n_tiles):
            cols = slice(tix * MXU_COLS, (tix + 1) * MXU_COLS)
            w1s_ref[:, cols] = _dot(w1_ref[0, :, cols].astype(BF16), perm).astype(BF16)
        w2s_ref[...] = w2_ref[0].astype(BF16)

        def block(j, carry):
            g = first + j
            slot = lax.rem(g, 2)
            x_copy(g, slot).wait()

            @pl.when(g + 1 < total)
            def _():
                x_copy(g + 1, 1 - slot).start()

            @pl.when(g >= 2)
            def _():
                y_copy(g - 2, slot).wait()

            n_rows = cnt_ref[e] - j * bm
            words = _load_packed_rows(xbuf, (slot,), bm)
            row = lax.broadcasted_iota(jnp.int32, words.shape, 0)
            x_lo, x_hi = _unpack_bf16_pair(jnp.where(row < n_rows, words, 0))
            x = jnp.concatenate([x_lo.astype(BF16), x_hi.astype(BF16)], axis=1)
            acts = []
            for tix in range(n_tiles):
                hp = _dot(x, w1s_ref[:, tix * MXU_COLS:(tix + 1) * MXU_COLS])
                feat = slice(tix * half, (tix + 1) * half)
                gl = jnp.minimum(hp[:, :half] + b1g_ref[0][:, feat], SWIGLU_LIMIT)
                lin = jnp.clip(hp[:, half:] + b1l_ref[0][:, feat], -SWIGLU_LIMIT, SWIGLU_LIMIT)
                acts.append((gl * _sigmoid(SWIGLU_ALPHA * gl) * (lin + 1.0)).astype(BF16))
            y = _dot(jnp.concatenate(acts, axis=1), w2s_ref[...]) + b2_ref[0]
            d_half = y.shape[-1] // 2
            _store_packed_rows(ybuf.at[slot], _pack_bf16_pair(y[:, :d_half], y[:, d_half:]))
            y_copy(g, slot).start()
            return carry

        lax.fori_loop(0, n_blocks, block, 0)

    @pl.when(e == pl.num_programs(0) - 1)
    def _():
        @pl.when(total >= 2)
        def _():
            y_copy(total - 2, lax.rem(total, 2)).wait()

        y_copy(total - 1, lax.rem(total - 1, 2)).wait()


def _experts(tables, xs, w1, b1g, b1l, w2, b2, bm, d):
    s_words = d // 2 // LANES
    dff2 = w1.shape[-1]
    dff = dff2 // 2
    emap = lambda e, *_: (e, 0, 0)
    grid_spec = pltpu.PrefetchScalarGridSpec(
        num_scalar_prefetch=len(tables),
        grid=(N_EXPERTS,),
        in_specs=[pl.BlockSpec(memory_space=pl.ANY),
                  pl.BlockSpec((1, d, dff2), emap),
                  pl.BlockSpec((1, 1, dff), emap),
                  pl.BlockSpec((1, 1, dff), emap),
                  pl.BlockSpec((1, dff, d), emap),
                  pl.BlockSpec((1, 1, d), emap)],
        out_specs=pl.BlockSpec(memory_space=pl.ANY),
        scratch_shapes=[pltpu.VMEM((d, dff2), BF16), pltpu.VMEM((dff, d), BF16),
                        pltpu.VMEM((2, bm * s_words, LANES), jnp.int32),
                        pltpu.VMEM((2, bm * s_words, LANES), jnp.int32),
                        pltpu.SemaphoreType.DMA((2,)), pltpu.SemaphoreType.DMA((2,))],
    )
    return pl.pallas_call(
        functools.partial(_expert_kernel, bm=bm),
        grid_spec=grid_spec,
        out_shape=jax.ShapeDtypeStruct(xs.shape, jnp.int32),
        compiler_params=pltpu.CompilerParams(dimension_semantics=("arbitrary",),
                                             vmem_limit_bytes=VMEM_LIMIT),
        name="experts",
    )(*tables, xs, w1, b1g, b1l, w2, b2)


def _gather_rows(dest_flat, y3):
    n_out = dest_flat.shape[0]
    _, s, _ = y3.shape
    mesh = _sc_mesh()
    n_workers = mesh.num_cores * mesh.num_subcores
    per_worker = n_out // n_workers
    n_chunks = per_worker // SC_ROWS
    assert per_worker % SC_ROWS == 0 and n_chunks % 2 == 0

    @functools.partial(
        pl.kernel, mesh=mesh, out_type=jax.ShapeDtypeStruct((n_out, s, LANES), y3.dtype),
        scratch_types=[pltpu.VMEM((2, SC_ROWS), jnp.int32), pltpu.VMEM((2, SC_ROWS, s, LANES), y3.dtype),
                       pltpu.SemaphoreType.DMA((2,)), pltpu.SemaphoreType.DMA((2,))],
        name="sc_gather")
    def gather(y_hbm, dest_hbm, out_hbm, idx_v, rows_v, gsem, wsem):
        base = (lax.axis_index("s") * mesh.num_cores + lax.axis_index("c")) * per_worker

        def rows_of(chunk):
            return pl.ds(pl.multiple_of(base + chunk * SC_ROWS, SC_ROWS), SC_ROWS)

        def gather_copy(slot):
            return pltpu.make_async_copy(y_hbm.at[idx_v.at[slot]], rows_v.at[slot], gsem.at[slot])

        def write_copy(chunk, slot):
            return pltpu.make_async_copy(rows_v.at[slot], out_hbm.at[rows_of(chunk)], wsem.at[slot])

        def start_gather(chunk, slot):
            pltpu.sync_copy(dest_hbm.at[rows_of(chunk)], idx_v.at[slot])
            gather_copy(slot).start()

        start_gather(0, 0)

        @pl.loop(0, n_chunks // 2)
        def _(pair):
            for slot in (0, 1):
                chunk = 2 * pair + slot
                gather_copy(slot).wait()
                write_copy(chunk, slot).start()

                @pl.when(chunk >= 1)
                def _():
                    write_copy(chunk - 1, 1 - slot).wait()

                @pl.when(chunk + 1 < n_chunks)
                def _():
                    start_gather(chunk + 1, 1 - slot)

        write_copy(n_chunks - 1, 1).wait()

    return gather(y3, dest_flat)


def _combine_kernel(x1_ref, gate_ref, y4_ref, nw_ref, o_ref, *, tm, normalize):
    x = x1_ref[...]
    gates = gate_ref[...]
    for kk in range(TOP_K):
        lo, hi = _unpack_bf16_pair(_load_packed_rows(y4_ref, (kk,), tm))
        x = x + gates[:, kk:kk + 1] * jnp.concatenate([lo, hi], axis=1)
    if normalize:
        x = x * lax.rsqrt(jnp.mean(x * x, axis=-1, keepdims=True) + EPS) * nw_ref[...]
    o_ref[...] = x


def _combine(x1, gates, y4, norm_w, tm, normalize):
    t, d = x1.shape
    s_words = d // 2 // LANES
    row = lambda i: (i, 0)
    return pl.pallas_call(
        functools.partial(_combine_kernel, tm=tm, normalize=normalize),
        grid=(t // tm,),
        in_specs=[pl.BlockSpec((tm, d), row), pl.BlockSpec((tm, LANES), row),
                  pl.BlockSpec((TOP_K, tm * s_words, LANES), lambda i: (0, i, 0)),
                  pl.BlockSpec((1, d), lambda i: (0, 0))],
        out_specs=pl.BlockSpec((tm, d), row),
        out_shape=jax.ShapeDtypeStruct((t, d), F32),
        compiler_params=pltpu.CompilerParams(dimension_semantics=("parallel",)),
        name="combine",
    )(x1, gates, y4, norm_w)


def _pad_cols(w, n):
    return jnp.pad(w, ((0, 0), (0, n - w.shape[1])))


def _rot_half(w):
    half = w.shape[-1] // 2
    return jnp.concatenate([-w[..., half:], w[..., :half]], axis=-1)


def _prep_inproj_weight(w_in):
    d = w_in.shape[0]
    hd = GDN_HEADS * GDN_D
    o = 0
    parts = {}
    for name, n in (("qkv", 3 * hd), ("z", hd), ("a", GDN_HEADS), ("b", GDN_HEADS),
                    ("cq", MLA_Q_LORA), ("ckv", MLA_KV_LORA), ("kr", MLA_ROPE)):
        parts[name] = w_in[:, o:o + n]
        o += n
    zeros = lambda n: jnp.zeros((d, n), w_in.dtype)
    kr_tile = jnp.concatenate([zeros(MLA_NOPE), parts["kr"], zeros(LANES - MLA_NOPE - MLA_ROPE)], axis=1)
    krr_tile = jnp.concatenate([zeros(MLA_NOPE), _rot_half(parts["kr"]),
                                zeros(LANES - MLA_NOPE - MLA_ROPE)], axis=1)
    ab_tile = jnp.concatenate([parts["a"], parts["b"], zeros(LANES - 2 * GDN_HEADS)], axis=1)
    return jnp.concatenate([parts["qkv"], parts["z"], parts["cq"], parts["ckv"], kr_tile, krr_tile,
                            ab_tile], axis=1).astype(BF16)


def _prep_mla_weights(w_uq, w_ukv):
    dq = MLA_NOPE + MLA_ROPE
    wq = w_uq.reshape(MLA_Q_LORA, MLA_HEADS, dq)
    wq_rot = jnp.concatenate([jnp.zeros_like(wq[..., :MLA_NOPE]), _rot_half(wq[..., MLA_NOPE:])], axis=-1)
    pad = lambda w: jnp.pad(w, ((0, 0), (0, 0), (0, LANES - w.shape[-1]))).reshape(w.shape[0], C_HEADS)
    wkv = w_ukv.reshape(MLA_KV_LORA, MLA_HEADS, MLA_NOPE + MLA_V)
    wv = wkv[..., MLA_NOPE:].reshape(MLA_KV_LORA, C_V)
    return (pad(wq).T.astype(BF16), pad(wq_rot).T.astype(BF16), pad(wkv[..., :MLA_NOPE]).astype(BF16),
            wv.T.astype(BF16))


def _rope_freq():
    half = MLA_ROPE // 2
    return (ROPE_THETA ** (-jnp.arange(half, dtype=F32) / half)).reshape(half, 1)


def _lane_row(v):
    return jnp.pad(v.astype(F32), (0, LANES - v.shape[0])).reshape(1, LANES)


def _routing_tables(cnt, eidx, rank, bm):
    e_ids = jnp.arange(N_EXPERTS, dtype=jnp.int32)
    counts = cnt[:, 0].astype(jnp.int32)
    padded = (counts + bm - 1) // bm * bm
    pend = jnp.sum(jnp.where(e_ids[None, :] <= e_ids[:, None], padded[None, :], 0), axis=1)
    pstart = pend - padded
    dest = jnp.sum(jnp.where(eidx[:TOP_K, :, None] == e_ids, pstart, 0), axis=-1) + rank[:TOP_K]
    tables = (counts, pstart // bm, padded // bm, pend[-1:] // bm)
    return dest.reshape(-1), tuple(t.astype(jnp.int32) for t in tables)


def kernel(x, positions, norm_mix_w, w_in, conv_w, a_log, dt_bias, gdn_norm_w, q_norm_w, w_uq, kv_norm_w,
           w_ukv, mla_out_norm_w, w_out, norm_ffn_w, router_w, router_b, w1, b1, w2, b2, norm_final_w):
    bsz, seq, d = x.shape
    t = bsz * seq
    depth = w_in.shape[0]
    tm = min(512, t)
    tr = min(512, t)
    bm = 512
    tq = min(512, seq)
    x2 = x.reshape(t, d)
    posf = positions.astype(F32)
    freq = _rope_freq()
    for l in range(depth):
        qkv, z, cq, ckv, kr2, ab = _inproj(x2, norm_mix_w[l].reshape(1, d), _prep_inproj_weight(w_in[l]), tm)
        conv_w8 = jnp.pad(conv_w[l], ((0, 8 - GDN_CONV), (0, 0)))
        o_a = _gdn(qkv, z, ab, conv_w8, _lane_row(a_log[l]), _lane_row(dt_bias[l]),
                   gdn_norm_w[l].reshape(1, GDN_D), bsz, seq, min(256, seq))
        wqt, wqrt, wk, wvt = _prep_mla_weights(w_uq[l], w_ukv[l])
        qt, k, vt = _mla_proj(cq, ckv, kr2, posf.reshape(1, t),
                              q_norm_w[l].reshape(1, -1), kv_norm_w[l].reshape(1, -1),
                              wqt, wqrt, wk, wvt, freq, tq)
        o_bt = _attention(qt, k, vt, mla_out_norm_w[l].reshape(-1, 1), bsz, seq, tq)
        x1, hn, eidx, gates, rank, cnt = _router(x2, o_a, o_bt, w_out[l].astype(BF16),
                                                 norm_ffn_w[l].reshape(1, d), router_w[l].T,
                                                 router_b[l].reshape(N_EXPERTS, 1), tr)
        n_pad = t * TOP_K + N_EXPERTS * bm
        dest_flat, tables = _routing_tables(cnt, eidx, rank, bm)
        s_words = d // 2 // LANES
        buf = _dispatch(dest_flat, hn.reshape(t, s_words, LANES), n_pad)
        yb = _experts(tables, buf.reshape(n_pad * s_words, LANES), w1[l],
                      b1[l][:, None, 0::2], b1[l][:, None, 1::2], w2[l], b2[l][:, None, :], bm, d)
        y4 = _gather_rows(dest_flat, yb.reshape(n_pad, s_words, LANES))
        x2 = _combine(x1, gates, y4.reshape(TOP_K, t * s_words, LANES), norm_final_w.reshape(1, d), tr,
                      normalize=(l == depth - 1))
    return x2.reshape(bsz, seq, d)
```

```python
import functools
import math

import jax
import jax.numpy as jnp
from jax import lax
from jax.experimental import pallas as pl
from jax.experimental.pallas import tpu as pltpu
from jax.experimental.pallas import tpu_sc as plsc

F32 = jnp.float32
BF16 = jnp.bfloat16
HIGHEST = lax.Precision.HIGHEST

LANES = 128
MXU_COLS = 256
EPS = 1e-6
CHUNK = 64

GDN_HEADS = 4
GDN_D = 128
GDN_CONV = 4
MLA_HEADS = 8
MLA_NOPE = 64
MLA_ROPE = 32
MLA_V = 64
MLA_Q_LORA = 384
MLA_KV_LORA = 256
ROPE_THETA = 10000.0
N_EXPERTS = 32
TOP_K = 4
SWIGLU_LIMIT = 7.0
SWIGLU_ALPHA = 1.702

NEG_BIG = -1e30
VMEM_LIMIT = 52 * 1024 * 1024


def _sigmoid(x):
    return 1.0 / (1.0 + jnp.exp(-x))


def _softplus(x):
    return jnp.maximum(x, 0.0) + jnp.log(1.0 + jnp.exp(-jnp.abs(x)))


def _dot(a, b, precision=None):
    return jnp.dot(a, b, preferred_element_type=F32, precision=precision)


def _dot_nt(a, b, precision=None):
    return lax.dot_general(a, b, (((1,), (1,)), ((), ())), preferred_element_type=F32,
                           precision=precision)


def _dot_tn(a, b):
    return lax.dot_general(a, b, (((0,), (0,)), ((), ())), preferred_element_type=F32)


def _split_bf16(a):
    hi = a.astype(BF16)
    return hi, (a - hi.astype(F32)).astype(BF16)


def _pack_bf16_pair(lo, hi):
    lo_bits = pltpu.bitcast(lo.astype(BF16).astype(F32), jnp.int32)
    hi_bits = pltpu.bitcast(hi.astype(BF16).astype(F32), jnp.int32)
    return jnp.bitwise_or(hi_bits, lax.shift_right_logical(lo_bits, 16))


def _unpack_bf16_pair(w):
    lo = pltpu.bitcast(lax.shift_left(w, 16), F32)
    hi = pltpu.bitcast(jnp.bitwise_and(w, -65536), F32)
    return lo, hi


def _load_packed_rows(ref, lead, m):
    s = ref.shape[-2] // m
    return jnp.concatenate([ref[(*lead, pl.ds(j, m, stride=s), slice(None))] for j in range(s)], axis=1)


def _store_packed_rows(ref, words):
    m = words.shape[0]
    s = ref.shape[0] // m
    for j in range(s):
        ref[pl.ds(j, m, stride=s), :] = words[:, j * LANES:(j + 1) * LANES]


def _dot_split_nt(a, b):
    return _dot_nt(a[0], b[0]) + _dot_nt(a[0], b[1]) + _dot_nt(a[1], b[0])


def _dot_split(a, b):
    return _dot(a[0], b[0]) + _dot(a[0], b[1]) + _dot(a[1], b[0])


C_QKV = 3 * GDN_HEADS * GDN_D
C_Z = GDN_HEADS * GDN_D
C_IN = C_QKV + C_Z + MLA_Q_LORA + MLA_KV_LORA + 3 * LANES


def _inproj_kernel(x_ref, nw_ref, w_ref, qkv_ref, z_ref, cq_ref, ckv_ref, kr_ref, ab_ref):
    x = x_ref[...]
    var = jnp.mean(x * x, axis=-1, keepdims=True)
    h = (x * lax.rsqrt(var + EPS) * nw_ref[...]).astype(BF16)
    p = _dot(h, w_ref[...])
    o = 0
    for ref in (qkv_ref, z_ref, cq_ref, ckv_ref, kr_ref, ab_ref):
        n = ref.shape[-1]
        ref[...] = p[:, o:o + n]
        o += n


def _inproj(x2, norm_w, w_all, tm):
    t, d = x2.shape
    widths = (C_QKV, C_Z, MLA_Q_LORA, MLA_KV_LORA, 2 * LANES, LANES)
    return pl.pallas_call(
        _inproj_kernel,
        grid=(t // tm,),
        in_specs=[pl.BlockSpec((tm, d), lambda i: (i, 0)),
                  pl.BlockSpec((1, d), lambda i: (0, 0)),
                  pl.BlockSpec((d, C_IN), lambda i: (0, 0))],
        out_specs=[pl.BlockSpec((tm, n), lambda i: (i, 0)) for n in widths],
        out_shape=[jax.ShapeDtypeStruct((t, n), F32) for n in widths],
        compiler_params=pltpu.CompilerParams(dimension_semantics=("parallel",),
                                             vmem_limit_bytes=VMEM_LIMIT),
        name="inproj",
    )(x2, norm_w, w_all)


def _gdn_kernel(qkv_ref, z_ref, ab_ref, cw_ref, alog_ref, dtb_ref, nw_ref, o_ref, ext_ref, s_ref):
    c = CHUNK
    bsz, rows, _ = qkv_ref.shape
    chunks_per_batch = rows // c

    @pl.when(pl.program_id(0) == 0)
    def _():
        ext_ref[:, 0:8, :] = jnp.zeros((bsz, 8, C_QKV), F32)
        s_ref[...] = jnp.zeros_like(s_ref)

    cw = cw_ref[...]
    nw = nw_ref[...]
    rr = lax.broadcasted_iota(jnp.int32, (rows, rows), 0)
    rc = lax.broadcasted_iota(jnp.int32, (rows, rows), 1)
    in_chunk_prefix = jnp.where(rr // c == rc // c, jnp.where(rr >= rc, 1.0, 0.0), 0.0)
    e8 = (lax.broadcasted_iota(jnp.int32, (8, LANES), 0)
          == lax.broadcasted_iota(jnp.int32, (8, LANES), 1)).astype(F32)
    ri = lax.broadcasted_iota(jnp.int32, (c, c), 0)
    ci = lax.broadcasted_iota(jnp.int32, (c, c), 1)
    causal = ri >= ci
    strict = ri > ci
    eye = (ri == ci).astype(F32)
    chains = [(j, h) for j in range(chunks_per_batch) for h in range(GDN_HEADS)]

    def prep_pieces(b):
        g = {"xc": [None] * (C_QKV // LANES)}

        def conv(tiles):
            def run():
                cols = slice(tiles[0] * LANES, (tiles[-1] + 1) * LANES)
                x = qkv_ref[b, :, cols]
                ext_ref[b, 8:8 + rows, cols] = x
                ext = ext_ref[b, :, cols]
                xb = cw[3:4, cols] * x
                for j in range(1, GDN_CONV):
                    xb = xb + cw[3 - j:4 - j, cols] * pltpu.roll(ext, j, axis=0)[8:8 + rows, :]
                ext_ref[b, 0:8, cols] = x[rows - 8:rows, :]
                xb = xb * _sigmoid(xb)
                for i, tile in enumerate(tiles):
                    g["xc"][tile] = xb[:, i * LANES:(i + 1) * LANES]
            return run

        def gates():
            ab = ab_ref[b]
            g_all = -jnp.exp(alog_ref[...]) * _softplus(ab + dtb_ref[...])
            g["beta"] = _sigmoid(ab)
            g["g_cum"] = _dot(in_chunk_prefix, g_all, HIGHEST)
            g["g_row"] = _dot_nt(e8, g["g_cum"], HIGHEST)
            for name in ("q", "k", "v", "gc", "bt", "dec"):
                g[name] = []

        def chunk(j):
            def run():
                rs = slice(j * c, (j + 1) * c)
                xc = g["xc"]
                for h in range(GDN_HEADS):
                    q = xc[h][rs]
                    k = xc[GDN_HEADS + h][rs]
                    g["q"].append(q * lax.rsqrt(jnp.sum(q * q, axis=-1, keepdims=True) + EPS) * (GDN_D ** -0.5))
                    g["k"].append(k * lax.rsqrt(jnp.sum(k * k, axis=-1, keepdims=True) + EPS))
                    g["v"].append(xc[2 * GDN_HEADS + h][rs])
                    gc = g["g_cum"][rs, h:h + 1]
                    gr = g["g_row"][h:h + 1, rs]
                    g["gc"].append(gc)
                    g["bt"].append(g["beta"][rs, GDN_HEADS + h:GDN_HEADS + h + 1])
                    g["dec"].append(jnp.exp(jnp.where(causal, gc - gr, -jnp.inf)))
            return run

        def a_mats():
            g["kb"] = [k * bt for k, bt in zip(g["k"], g["bt"])]
            g["k16"] = [k.astype(BF16) for k in g["k"]]
            g["a"] = [jnp.where(strict, _dot_nt(kb.astype(BF16), k16) * dec, 0.0)
                      for kb, k16, dec in zip(g["kb"], g["k16"], g["dec"])]

        n_tiles = C_QKV // LANES
        conv_pieces = [conv(list(range(i, i + 2))) for i in range(0, n_tiles, 2)]
        return g, conv_pieces + [gates] + [chunk(j) for j in range(chunks_per_batch)] + [a_mats]

    def inverse_rounds(g):
        def start():
            g["t"] = [eye - a for a in g["a"]]
            g["pw"] = [_split_bf16(a) for a in g["a"]]

        def square():
            g["pw"] = [_split_bf16(_dot_split(p, p)) for p in g["pw"]]

        def update():
            g["t"] = [t + _dot_split(_split_bf16(t), p) for t, p in zip(g["t"], g["pw"])]

        return [start] + [square, update] * 5

    groups = []
    pending = []
    for b in range(bsz):
        g, pieces = prep_pieces(b)
        groups.append(g)
        n_a, n_b = len(pending), len(pieces)
        ia = ib = 0
        while ia < n_a or ib < n_b:
            if ib >= n_b or (ia < n_a and ia * n_b <= ib * n_a):
                pending[ia]()
                ia += 1
            else:
                pieces[ib]()
                ib += 1
        pending = inverse_rounds(g)
    for fn in pending:
        fn()

    uws, qk16s, qd16s, kd16s, g_tots = [], [], [], [], []
    for g in groups:
        e_gs = [jnp.exp(gc) for gc in g["gc"]]
        uws.append([_dot(t.astype(BF16), jnp.concatenate([v * bt, kb * eg], axis=1).astype(BF16))
                    for t, v, bt, kb, eg in zip(g["t"], g["v"], g["bt"], g["kb"], e_gs)])
        qk16s.append([(_dot_nt(q.astype(BF16), k16) * dec).astype(BF16)
                      for q, k16, dec in zip(g["q"], g["k16"], g["dec"])])
        qd16s.append([(q * eg).astype(BF16) for q, eg in zip(g["q"], e_gs)])
        kd16s.append([(k * jnp.exp(gc[c - 1:c, :] - gc)).astype(BF16) for k, gc in zip(g["k"], g["gc"])])
        g_tots.append([jnp.exp(gc[c - 1:c, :]) for gc in g["gc"]])

    pairs = [(b, h) for b in range(bsz) for h in range(GDN_HEADS)]
    states = [s_ref[b * GDN_HEADS + h] for b, h in pairs]
    for j in range(chunks_per_batch):
        rs = slice(j * c, (j + 1) * c)
        pick = lambda per_group: [per_group[b][chains.index((j, h))] for b, h in pairs]
        uw_j, qk_j, qd_j, kd_j, gt_j = (pick(x) for x in (uws, qk16s, qd16s, kd16s, g_tots))
        s16s = [s.astype(BF16) for s in states]
        v16s = [(uw[:, :GDN_D] - _dot(uw[:, GDN_D:].astype(BF16), s16)).astype(BF16)
                for uw, s16 in zip(uw_j, s16s)]
        states = [s * gt + _dot_tn(kd, v16) for s, gt, kd, v16 in zip(states, gt_j, kd_j, v16s)]
        os_ = [_dot(qd, s16) + _dot(qk, v16) for qd, qk, s16, v16 in zip(qd_j, qk_j, s16s, v16s)]
        for (b, h), o in zip(pairs, os_):
            on = o * lax.rsqrt(jnp.mean(o * o, axis=-1, keepdims=True) + EPS) * nw
            zh = z_ref[b, rs, h * GDN_D:(h + 1) * GDN_D]
            o_ref[b, rs, h * GDN_D:(h + 1) * GDN_D] = on * (zh * _sigmoid(zh))
    for p, s in enumerate(states):
        s_ref[p] = s


def _gdn(qkv, z, ab, conv_w8, alog_row, dtb_row, norm_w, rows):
    bsz, seq, _ = qkv.shape
    blk = lambda i: (0, i, 0)
    const = lambda i: (0, 0)
    return pl.pallas_call(
        _gdn_kernel,
        grid=(seq // rows,),
        in_specs=[pl.BlockSpec((bsz, rows, C_QKV), blk),
                  pl.BlockSpec((bsz, rows, C_Z), blk),
                  pl.BlockSpec((bsz, rows, LANES), blk),
                  pl.BlockSpec((8, C_QKV), const),
                  pl.BlockSpec((1, LANES), const),
                  pl.BlockSpec((1, LANES), const),
                  pl.BlockSpec((1, GDN_D), const)],
        out_specs=pl.BlockSpec((bsz, rows, C_Z), blk),
        out_shape=jax.ShapeDtypeStruct((bsz, seq, C_Z), F32),
        scratch_shapes=[pltpu.VMEM((bsz, 8 + rows, C_QKV), F32),
                        pltpu.VMEM((bsz * GDN_HEADS, GDN_D, GDN_D), F32)],
        compiler_params=pltpu.CompilerParams(dimension_semantics=("arbitrary",),
                                             vmem_limit_bytes=VMEM_LIMIT),
        name="gdn",
    )(qkv, z, ab, conv_w8, alog_row, dtb_row, norm_w)


D_QK_PAD = LANES
C_HEADS = MLA_HEADS * D_QK_PAD
C_V = MLA_HEADS * MLA_V
V_AUG = MLA_V + 16
C_VAUG = MLA_HEADS * V_AUG
Q_SCALE = (MLA_NOPE + MLA_ROPE) ** -0.5 * math.log2(math.e)


def _mla_proj_kernel(cq_ref, ckv_ref, kr_ref, posr_ref, qnw_ref, kvnw_ref, wqt_ref, wqrt_ref,
                     wk_ref, wvt_ref, freqc_ref, qt_ref, k_ref, vt_ref):
    cq = cq_ref[...]
    cqn = (cq * lax.rsqrt(jnp.mean(cq * cq, axis=-1, keepdims=True) + EPS) * qnw_ref[...]).astype(BF16)
    ckv = ckv_ref[...]
    ckvn = (ckv * lax.rsqrt(jnp.mean(ckv * ckv, axis=-1, keepdims=True) + EPS)
            * kvnw_ref[...]).astype(BF16)
    tm = posr_ref.shape[-1]
    ang = freqc_ref[...] * posr_ref[...]
    c16 = jnp.cos(ang)
    s16 = jnp.sin(ang)
    pad = LANES - MLA_NOPE - MLA_ROPE
    cs_t = jnp.concatenate([jnp.ones((MLA_NOPE, tm), F32), c16, c16, jnp.ones((pad, tm), F32)], axis=0)
    sn_t = jnp.concatenate([jnp.zeros((MLA_NOPE, tm), F32), s16, s16, jnp.zeros((pad, tm), F32)], axis=0)
    cs = cs_t.T
    sn = sn_t.T
    qa_t = _dot_nt(wqt_ref[...], cqn)
    qb_t = _dot_nt(wqrt_ref[...], cqn)
    kn = _dot(ckvn, wk_ref[...])
    kr = kr_ref[...]
    kpe = kr[:, :LANES] * cs + kr[:, LANES:] * sn
    for h in range(MLA_HEADS):
        sl = slice(h * LANES, (h + 1) * LANES)
        qt_ref[sl, :] = ((qa_t[sl, :] * cs_t + qb_t[sl, :] * sn_t) * Q_SCALE).astype(BF16)
        k_ref[:, sl] = (kn[:, sl] + kpe).astype(BF16)
    v_t = _dot_nt(wvt_ref[...], ckvn).astype(BF16)
    ones = jnp.ones((V_AUG - MLA_V, v_t.shape[1]), BF16)
    for h in range(MLA_HEADS):
        vt_ref[0, h * V_AUG:h * V_AUG + MLA_V, :] = v_t[h * MLA_V:(h + 1) * MLA_V, :]
        vt_ref[0, h * V_AUG + MLA_V:(h + 1) * V_AUG, :] = ones


def _mla_proj(cq, ckv, kr2, posr, qnw, kvnw, wqt, wqrt, wk, wvt, freqc, tm):
    t = cq.shape[0]
    row = lambda i: (i, 0)
    col = lambda i: (0, i)
    const = lambda i: (0, 0)
    return pl.pallas_call(
        _mla_proj_kernel,
        grid=(t // tm,),
        in_specs=[pl.BlockSpec((tm, MLA_Q_LORA), row),
                  pl.BlockSpec((tm, MLA_KV_LORA), row),
                  pl.BlockSpec((tm, 2 * LANES), row),
                  pl.BlockSpec((1, tm), col),
                  pl.BlockSpec((1, MLA_Q_LORA), const),
                  pl.BlockSpec((1, MLA_KV_LORA), const),
                  pl.BlockSpec((C_HEADS, MLA_Q_LORA), const),
                  pl.BlockSpec((C_HEADS, MLA_Q_LORA), const),
                  pl.BlockSpec((MLA_KV_LORA, C_HEADS), const),
                  pl.BlockSpec((C_V, MLA_KV_LORA), const),
                  pl.BlockSpec((MLA_ROPE // 2, 1), const)],
        out_specs=[pl.BlockSpec((C_HEADS, tm), col),
                   pl.BlockSpec((tm, C_HEADS), row),
                   pl.BlockSpec((1, C_VAUG, tm), lambda i: (i, 0, 0))],
        out_shape=[jax.ShapeDtypeStruct((C_HEADS, t), BF16),
                   jax.ShapeDtypeStruct((t, C_HEADS), BF16),
                   jax.ShapeDtypeStruct((t // tm, C_VAUG, tm), BF16)],
        compiler_params=pltpu.CompilerParams(dimension_semantics=("parallel",),
                                             vmem_limit_bytes=VMEM_LIMIT),
        name="mla_proj",
    )(cq, ckv, kr2, posr, qnw, kvnw, wqt, wqrt, wk, wvt, freqc)


def _attn_kernel(qt_ref, k_ref, vt_ref, nw_ref, ot_ref, m_ref, acc_ref, *, tq):
    i = pl.program_id(1)
    m_ref[...] = jnp.full(m_ref.shape, NEG_BIG, F32)
    acc_ref[...] = jnp.zeros(acc_ref.shape, F32)
    key_chunk = lax.broadcasted_iota(jnp.int32, (tq, tq), 0) // CHUNK
    qry_chunk = lax.broadcasted_iota(jnp.int32, (tq, tq), 1) // CHUNK
    diag_mask = key_chunk <= qry_chunk

    def tile(j, masked):
        rows = pl.ds(pl.multiple_of(j * tq, tq), tq)
        scores = []
        for h in range(MLA_HEADS):
            hs = slice(h * LANES, (h + 1) * LANES)
            s_t = _dot(k_ref[rows, hs], qt_ref[hs, :])
            scores.append(jnp.where(diag_mask, s_t, NEG_BIG) if masked else s_t)
        probs, alphas = [], []
        for h in range(MLA_HEADS):
            m_old = m_ref[h]
            m_new = jnp.maximum(m_old, jnp.max(scores[h], axis=0, keepdims=True))
            m_ref[h] = m_new
            probs.append(jnp.exp2(scores[h] - m_new[0:1, :]).astype(BF16))
            alphas.append(jnp.exp2(m_old[0:1, :] - m_new[0:1, :]))
        for h in range(MLA_HEADS):
            v_aug = vt_ref[j, h * V_AUG:(h + 1) * V_AUG, :]
            acc_ref[h] = alphas[h] * acc_ref[h] + _dot(v_aug, probs[h])

    def full_tile(j, carry):
        tile(j, False)
        return carry

    lax.fori_loop(0, i, full_tile, 0)
    tile(i, True)

    outs = []
    for h in range(MLA_HEADS):
        acc = acc_ref[h]
        outs.append(acc[:MLA_V, :] / acc[MLA_V:MLA_V + 1, :])
    ssq = outs[0] * outs[0]
    for o in outs[1:]:
        ssq = ssq + o * o
    inv = lax.rsqrt(jnp.sum(ssq, axis=0, keepdims=True) / C_V + EPS)
    for h in range(MLA_HEADS):
        vs = slice(h * MLA_V, (h + 1) * MLA_V)
        ot_ref[vs, :] = outs[h] * inv * nw_ref[vs, :]


def _attention(qt, k, vt, norm_w_col, bsz, seq, tq):
    nq = seq // tq
    return pl.pallas_call(
        functools.partial(_attn_kernel, tq=tq),
        grid=(bsz, nq),
        in_specs=[pl.BlockSpec((C_HEADS, tq), lambda b, i: (0, b * nq + i)),
                  pl.BlockSpec((seq, C_HEADS), lambda b, i: (b, 0)),
                  pl.BlockSpec((nq, C_VAUG, tq), lambda b, i: (b, 0, 0)),
                  pl.BlockSpec((C_V, 1), lambda b, i: (0, 0))],
        out_specs=pl.BlockSpec((C_V, tq), lambda b, i: (0, b * nq + i)),
        out_shape=jax.ShapeDtypeStruct((C_V, bsz * seq), F32),
        scratch_shapes=[pltpu.VMEM((MLA_HEADS, 8, tq), F32),
                        pltpu.VMEM((MLA_HEADS, V_AUG, tq), F32)],
        compiler_params=pltpu.CompilerParams(dimension_semantics=("parallel", "parallel"),
                                             vmem_limit_bytes=VMEM_LIMIT),
        name="attention",
    )(qt, k, vt, norm_w_col)


def _router_kernel(x_ref, oa_ref, obt_ref, wo_ref, nw_ref, rw_ref, rb_ref,
                   x1_ref, h_ref, eidx_ref, gate_ref, rank_ref, cnt_ref, run_ref, *, tm):
    @pl.when(pl.program_id(0) == 0)
    def _():
        run_ref[...] = jnp.zeros_like(run_ref)

    half = oa_ref.shape[-1]
    y = (_dot(oa_ref[...].astype(BF16), wo_ref[0:half, :])
         + _dot_tn(obt_ref[...].astype(BF16), wo_ref[half:, :]))
    x1 = x_ref[...] + y
    x1_ref[...] = x1
    hn = x1 * lax.rsqrt(jnp.mean(x1 * x1, axis=-1, keepdims=True) + EPS) * nw_ref[...]
    d_half = hn.shape[-1] // 2
    _store_packed_rows(h_ref, _pack_bf16_pair(hn[:, :d_half], hn[:, d_half:]))
    lg = _dot_split_nt(_split_bf16(rw_ref[...]), _split_bf16(hn)) + rb_ref[...]
    expert = lax.broadcasted_iota(jnp.int32, lg.shape, 0)
    sels, tops, idxs = [], [], []
    for _ in range(TOP_K):
        m = jnp.max(lg, axis=0, keepdims=True)
        idx = jnp.min(jnp.where(lg == m, expert, N_EXPERTS), axis=0, keepdims=True)
        sel = expert == idx
        lg = jnp.where(sel, -jnp.inf, lg)
        sels.append(sel)
        tops.append(m)
        idxs.append(idx)
    exps = [jnp.exp(tv - tops[0]) for tv in tops]
    den = exps[0] + exps[1] + exps[2] + exps[3]
    multi = sels[0].astype(F32)
    for sel in sels[1:]:
        multi = multi + sel.astype(F32)
    ri = lax.broadcasted_iota(jnp.int32, (tm, tm), 0)
    ci = lax.broadcasted_iota(jnp.int32, (tm, tm), 1)
    before = _dot(multi.astype(BF16), (ri < ci).astype(BF16)) + run_ref[...]
    ranks = [jnp.sum(jnp.where(sel, before, 0.0), axis=0, keepdims=True).astype(jnp.int32) for sel in sels]
    run = run_ref[...] + jnp.sum(multi, axis=1, keepdims=True)
    run_ref[...] = run
    cnt_ref[...] = jnp.broadcast_to(run, cnt_ref.shape)
    pad_i = jnp.zeros((8 - TOP_K, tm), jnp.int32)
    eidx_ref[...] = jnp.concatenate(idxs + [pad_i], axis=0)
    rank_ref[...] = jnp.concatenate(ranks + [pad_i], axis=0)
    gates_t = jnp.concatenate([e / den for e in exps] + [jnp.zeros((LANES - TOP_K, tm), F32)], axis=0)
    eye = (ri == ci).astype(BF16)
    g_hi, g_lo = _split_bf16(gates_t)
    gate_ref[...] = _dot_nt(eye, g_hi) + _dot_nt(eye, g_lo)


def _router(x2, oa, obt, w_out, norm_w, rw, rb, tm):
    t, d = x2.shape
    half = oa.shape[-1]
    s_words = d // 2 // LANES
    row = lambda i: (i, 0)
    col = lambda i: (0, i)
    const = lambda i: (0, 0)
    return pl.pallas_call(
        functools.partial(_router_kernel, tm=tm),
        grid=(t // tm,),
        in_specs=[pl.BlockSpec((tm, d), row),
                  pl.BlockSpec((tm, half), row),
                  pl.BlockSpec((half, tm), lambda i: (0, i)),
                  pl.BlockSpec((2 * half, d), const),
                  pl.BlockSpec((1, d), const),
                  pl.BlockSpec((N_EXPERTS, d), const),
                  pl.BlockSpec((N_EXPERTS, 1), const)],
        out_specs=[pl.BlockSpec((tm, d), row), pl.BlockSpec((tm * s_words, LANES), row),
                   pl.BlockSpec((8, tm), col), pl.BlockSpec((tm, LANES), row),
                   pl.BlockSpec((8, tm), col), pl.BlockSpec((N_EXPERTS, LANES), const)],
        out_shape=[jax.ShapeDtypeStruct((t, d), F32), jax.ShapeDtypeStruct((t * s_words, LANES), jnp.int32),
                   jax.ShapeDtypeStruct((8, t), jnp.int32), jax.ShapeDtypeStruct((t, LANES), F32),
                   jax.ShapeDtypeStruct((8, t), jnp.int32), jax.ShapeDtypeStruct((N_EXPERTS, LANES), F32)],
        scratch_shapes=[pltpu.VMEM((N_EXPERTS, 1), F32)],
        compiler_params=pltpu.CompilerParams(dimension_semantics=("arbitrary",),
                                             vmem_limit_bytes=VMEM_LIMIT),
        name="router",
    )(x2, oa, obt, w_out, norm_w, rw, rb)


SC_ROWS = 64


def _sc_mesh():
    return plsc.VectorSubcoreMesh(core_axis_name="c", subcore_axis_name="s")


def _dispatch(dest_flat, h3, n_pad):
    t, s, _ = h3.shape
    mesh = _sc_mesh()
    n_workers = mesh.num_cores * mesh.num_subcores
    per_worker = t // n_workers
    assert per_worker % SC_ROWS == 0

    @functools.partial(
        pl.kernel, mesh=mesh, out_type=jax.ShapeDtypeStruct((n_pad, s, LANES), h3.dtype),
        scratch_types=[pltpu.VMEM((TOP_K, SC_ROWS), jnp.int32), pltpu.VMEM((SC_ROWS, s, LANES), h3.dtype),
                       pltpu.SemaphoreType.DMA],
        name="sc_dispatch")
    def scatter(h_hbm, dest_hbm, out_hbm, idx_v, rows_v, sem):
        base = (lax.axis_index("s") * mesh.num_cores + lax.axis_index("c")) * per_worker

        @pl.loop(0, per_worker // SC_ROWS)
        def _(i):
            off = pl.multiple_of(base + i * SC_ROWS, SC_ROWS)
            pltpu.sync_copy(h_hbm.at[pl.ds(off, SC_ROWS)], rows_v)
            for kk in range(TOP_K):
                pltpu.sync_copy(dest_hbm.at[pl.ds(kk * t + off, SC_ROWS)], idx_v.at[kk])
            copies = [pltpu.async_copy(rows_v, out_hbm.at[idx_v.at[kk]], sem) for kk in range(TOP_K)]
            for cp in copies:
                cp.wait()

    return scatter(h3, dest_flat)


def _expert_kernel(cnt_ref, blk0_ref, nblk_ref, total_ref, x_hbm, w1_ref, b1g_ref, b1l_ref, w2_ref, b2_ref,
                   y_hbm, w1s_ref, w2s_ref, xbuf, ybuf, xsem, ysem, *, bm):
    e = pl.program_id(0)
    n_blocks = nblk_ref[e]
    first = blk0_ref[e]
    total = total_ref[0]
    rows_per_block = xbuf.shape[1]
    n_tiles = w1s_ref.shape[-1] // MXU_COLS
    half = MXU_COLS // 2

    def x_copy(g, slot):
        return pltpu.make_async_copy(x_hbm.at[pl.ds(pl.multiple_of(g * rows_per_block, rows_per_block),
                                                    rows_per_block)], xbuf.at[slot], xsem.at[slot])

    def y_copy(g, slot):
        return pltpu.make_async_copy(ybuf.at[slot], y_hbm.at[pl.ds(pl.multiple_of(g * rows_per_block,
                                                                                   rows_per_block),
                                                                   rows_per_block)], ysem.at[slot])

    @pl.when(e == 0)
    def _():
        x_copy(0, 0).start()

    @pl.when(n_blocks > 0)
    def _():
        r = lax.broadcasted_iota(jnp.int32, (MXU_COLS, MXU_COLS), 0)
        c = lax.broadcasted_iota(jnp.int32, (MXU_COLS, MXU_COLS), 1)
        perm = (r == jnp.where(c < half, 2 * c, 2 * (c - half) + 1)).astype(BF16)
        for tix in range(n_tiles):
            cols = slice(tix * MXU_COLS, (tix + 1) * MXU_COLS)
            w1s_ref[:, cols] = _dot(w1_ref[0, :, cols].astype(BF16), perm).astype(BF16)
        w2s_ref[...] = w2_ref[0].astype(BF16)

        def block(j, carry):
            g = first + j
            slot = lax.rem(g, 2)
            x_copy(g, slot).wait()

            @pl.when(g + 1 < total)
            def _():
                x_copy(g + 1, 1 - slot).start()

            @pl.when(g >= 2)
            def _():
                y_copy(g - 2, slot).wait()

            n_rows = cnt_ref[e] - j * bm
            words = _load_packed_rows(xbuf, (slot,), bm)
            row = lax.broadcasted_iota(jnp.int32, words.shape, 0)
            x_lo, x_hi = _unpack_bf16_pair(jnp.where(row < n_rows, words, 0))
            x = jnp.concatenate([x_lo.astype(BF16), x_hi.astype(BF16)], axis=1)
            acts = []
            for tix in range(n_tiles):
                hp = _dot(x, w1s_ref[:, tix * MXU_COLS:(tix + 1) * MXU_COLS])
                feat = slice(tix * half, (tix + 1) * half)
                gl = jnp.minimum(hp[:, :half] + b1g_ref[0][:, feat], SWIGLU_LIMIT)
                lin = jnp.clip(hp[:, half:] + b1l_ref[0][:, feat], -SWIGLU_LIMIT, SWIGLU_LIMIT)
                acts.append((gl * _sigmoid(SWIGLU_ALPHA * gl) * (lin + 1.0)).astype(BF16))
            y = _dot(jnp.concatenate(acts, axis=1), w2s_ref[...]) + b2_ref[0]
            d_half = y.shape[-1] // 2
            _store_packed_rows(ybuf.at[slot], _pack_bf16_pair(y[:, :d_half], y[:, d_half:]))
            y_copy(g, slot).start()
            return carry

        lax.fori_loop(0, n_blocks, block, 0)

    @pl.when(e == pl.num_programs(0) - 1)
    def _():
        @pl.when(total >= 2)
        def _():
            y_copy(total - 2, lax.rem(total, 2)).wait()

        y_copy(total - 1, lax.rem(total - 1, 2)).wait()


def _experts(tables, xs, w1, b1g, b1l, w2, b2, bm, d):
    s_words = d // 2 // LANES
    dff2 = w1.shape[-1]
    dff = dff2 // 2
    emap = lambda e, *_: (e, 0, 0)
    grid_spec = pltpu.PrefetchScalarGridSpec(
        num_scalar_prefetch=len(tables),
        grid=(N_EXPERTS,),
        in_specs=[pl.BlockSpec(memory_space=pl.ANY),
                  pl.BlockSpec((1, d, dff2), emap),
                  pl.BlockSpec((1, 1, dff), emap),
                  pl.BlockSpec((1, 1, dff), emap),
                  pl.BlockSpec((1, dff, d), emap),
                  pl.BlockSpec((1, 1, d), emap)],
        out_specs=pl.BlockSpec(memory_space=pl.ANY),
        scratch_shapes=[pltpu.VMEM((d, dff2), BF16), pltpu.VMEM((dff, d), BF16),
                        pltpu.VMEM((2, bm * s_words, LANES), jnp.int32),
                        pltpu.VMEM((2, bm * s_words, LANES), jnp.int32),
                        pltpu.SemaphoreType.DMA((2,)), pltpu.SemaphoreType.DMA((2,))],
    )
    return pl.pallas_call(
        functools.partial(_expert_kernel, bm=bm),
        grid_spec=grid_spec,
        out_shape=jax.ShapeDtypeStruct(xs.shape, jnp.int32),
        compiler_params=pltpu.CompilerParams(dimension_semantics=("arbitrary",),
                                             vmem_limit_bytes=VMEM_LIMIT),
        name="experts",
    )(*tables, xs, w1, b1g, b1l, w2, b2)


def _gather_rows(dest_flat, y3):
    n_out = dest_flat.shape[0]
    _, s, _ = y3.shape
    mesh = _sc_mesh()
    n_workers = mesh.num_cores * mesh.num_subcores
    per_worker = n_out // n_workers
    n_chunks = per_worker // SC_ROWS
    assert per_worker % SC_ROWS == 0 and n_chunks % 2 == 0

    @functools.partial(
        pl.kernel, mesh=mesh, out_type=jax.ShapeDtypeStruct((n_out, s, LANES), y3.dtype),
        scratch_types=[pltpu.VMEM((2, SC_ROWS), jnp.int32), pltpu.VMEM((2, SC_ROWS, s, LANES), y3.dtype),
                       pltpu.SemaphoreType.DMA((2,)), pltpu.SemaphoreType.DMA((2,))],
        name="sc_gather")
    def gather(y_hbm, dest_hbm, out_hbm, idx_v, rows_v, gsem, wsem):
        base = (lax.axis_index("s") * mesh.num_cores + lax.axis_index("c")) * per_worker

        def rows_of(chunk):
            return pl.ds(pl.multiple_of(base + chunk * SC_ROWS, SC_ROWS), SC_ROWS)

        def gather_copy(slot):
            return pltpu.make_async_copy(y_hbm.at[idx_v.at[slot]], rows_v.at[slot], gsem.at[slot])

        def write_copy(chunk, slot):
            return pltpu.make_async_copy(rows_v.at[slot], out_hbm.at[rows_of(chunk)], wsem.at[slot])

        def start_gather(chunk, slot):
            pltpu.sync_copy(dest_hbm.at[rows_of(chunk)], idx_v.at[slot])
            gather_copy(slot).start()

        start_gather(0, 0)

        @pl.loop(0, n_chunks // 2)
        def _(pair):
            for slot in (0, 1):
                chunk = 2 * pair + slot
                gather_copy(slot).wait()
                write_copy(chunk, slot).start()

                @pl.when(chunk >= 1)
                def _():
                    write_copy(chunk - 1, 1 - slot).wait()

                @pl.when(chunk + 1 < n_chunks)
                def _():
                    start_gather(chunk + 1, 1 - slot)

        write_copy(n_chunks - 1, 1).wait()

    return gather(y3, dest_flat)


def _combine_kernel(x1_ref, gate_ref, y4_ref, nw_ref, o_ref, *, tm, normalize):
    x = x1_ref[...]
    gates = gate_ref[...]
    for kk in range(TOP_K):
        lo, hi = _unpack_bf16_pair(_load_packed_rows(y4_ref, (kk,), tm))
        x = x + gates[:, kk:kk + 1] * jnp.concatenate([lo, hi], axis=1)
    if normalize:
        x = x * lax.rsqrt(jnp.mean(x * x, axis=-1, keepdims=True) + EPS) * nw_ref[...]
    o_ref[...] = x


def _combine(x1, gates, y4, norm_w, tm, normalize):
    t, d = x1.shape
    s_words = d // 2 // LANES
    row = lambda i: (i, 0)
    return pl.pallas_call(
        functools.partial(_combine_kernel, tm=tm, normalize=normalize),
        grid=(t // tm,),
        in_specs=[pl.BlockSpec((tm, d), row), pl.BlockSpec((tm, LANES), row),
                  pl.BlockSpec((TOP_K, tm * s_words, LANES), lambda i: (0, i, 0)),
                  pl.BlockSpec((1, d), lambda i: (0, 0))],
        out_specs=pl.BlockSpec((tm, d), row),
        out_shape=jax.ShapeDtypeStruct((t, d), F32),
        compiler_params=pltpu.CompilerParams(dimension_semantics=("parallel",)),
        name="combine",
    )(x1, gates, y4, norm_w)


def _pad_cols(w, n):
    return jnp.pad(w, ((0, 0), (0, n - w.shape[1])))


def _rot_half(w):
    half = w.shape[-1] // 2
    return jnp.concatenate([-w[..., half:], w[..., :half]], axis=-1)


def _prep_inproj_weight(w_in):
    d = w_in.shape[0]
    hd = GDN_HEADS * GDN_D
    o = 0
    parts = {}
    for name, n in (("qkv", 3 * hd), ("z", hd), ("a", GDN_HEADS), ("b", GDN_HEADS),
                    ("cq", MLA_Q_LORA), ("ckv", MLA_KV_LORA), ("kr", MLA_ROPE)):
        parts[name] = w_in[:, o:o + n]
        o += n
    zeros = lambda n: jnp.zeros((d, n), w_in.dtype)
    kr_tile = jnp.concatenate([zeros(MLA_NOPE), parts["kr"], zeros(LANES - MLA_NOPE - MLA_ROPE)], axis=1)
    krr_tile = jnp.concatenate([zeros(MLA_NOPE), _rot_half(parts["kr"]),
                                zeros(LANES - MLA_NOPE - MLA_ROPE)], axis=1)
    ab_tile = jnp.concatenate([parts["a"], parts["b"], zeros(LANES - 2 * GDN_HEADS)], axis=1)
    return jnp.concatenate([parts["qkv"], parts["z"], parts["cq"], parts["ckv"], kr_tile, krr_tile,
                            ab_tile], axis=1).astype(BF16)


def _prep_mla_weights(w_uq, w_ukv):
    dq = MLA_NOPE + MLA_ROPE
    wq = w_uq.reshape(MLA_Q_LORA, MLA_HEADS, dq)
    wq_rot = jnp.concatenate([jnp.zeros_like(wq[..., :MLA_NOPE]), _rot_half(wq[..., MLA_NOPE:])], axis=-1)
    pad = lambda w: jnp.pad(w, ((0, 0), (0, 0), (0, LANES - w.shape[-1]))).reshape(w.shape[0], C_HEADS)
    wkv = w_ukv.reshape(MLA_KV_LORA, MLA_HEADS, MLA_NOPE + MLA_V)
    wv = wkv[..., MLA_NOPE:].reshape(MLA_KV_LORA, C_V)
    return (pad(wq).T.astype(BF16), pad(wq_rot).T.astype(BF16), pad(wkv[..., :MLA_NOPE]).astype(BF16),
            wv.T.astype(BF16))


def _rope_freq():
    half = MLA_ROPE // 2
    return (ROPE_THETA ** (-jnp.arange(half, dtype=F32) / half)).reshape(half, 1)


def _lane_row(v):
    return jnp.pad(v.astype(F32), (0, LANES - v.shape[0])).reshape(1, LANES)


def _routing_tables(cnt, eidx, rank, bm):
    e_ids = jnp.arange(N_EXPERTS, dtype=jnp.int32)
    counts = cnt[:, 0].astype(jnp.int32)
    padded = (counts + bm - 1) // bm * bm
    pend = jnp.sum(jnp.where(e_ids[None, :] <= e_ids[:, None], padded[None, :], 0), axis=1)
    pstart = pend - padded
    dest = jnp.sum(jnp.where(eidx[:TOP_K, :, None] == e_ids, pstart, 0), axis=-1) + rank[:TOP_K]
    tables = (counts, pstart // bm, padded // bm, pend[-1:] // bm)
    return dest.reshape(-1), tuple(t.astype(jnp.int32) for t in tables)


def kernel(x, positions, norm_mix_w, w_in, conv_w, a_log, dt_bias, gdn_norm_w, q_norm_w, w_uq, kv_norm_w,
           w_ukv, mla_out_norm_w, w_out, norm_ffn_w, router_w, router_b, w1, b1, w2, b2, norm_final_w):
    bsz, seq, d = x.shape
    t = bsz * seq
    depth = w_in.shape[0]
    tm = min(512, t)
    tr = min(512, t)
    bm = 512
    tq = min(512, seq)
    x2 = x.reshape(t, d)
    posf = positions.astype(F32)
    freq = _rope_freq()
    for l in range(depth):
        qkv, z, cq, ckv, kr2, ab = _inproj(x2, norm_mix_w[l].reshape(1, d), _prep_inproj_weight(w_in[l]), tm)
        conv_w8 = jnp.pad(conv_w[l], ((0, 8 - GDN_CONV), (0, 0)))
        o_a = _gdn(qkv.reshape(bsz, seq, C_QKV), z.reshape(bsz, seq, C_Z), ab.reshape(bsz, seq, LANES),
                   conv_w8, _lane_row(a_log[l]), _lane_row(dt_bias[l]), gdn_norm_w[l].reshape(1, GDN_D),
                   min(2 * CHUNK, seq)).reshape(t, C_Z)
        wqt, wqrt, wk, wvt = _prep_mla_weights(w_uq[l], w_ukv[l])
        qt, k, vt = _mla_proj(cq, ckv, kr2, posf.reshape(1, t),
                              q_norm_w[l].reshape(1, -1), kv_norm_w[l].reshape(1, -1),
                              wqt, wqrt, wk, wvt, freq, tq)
        o_bt = _attention(qt, k, vt, mla_out_norm_w[l].reshape(-1, 1), bsz, seq, tq)
        x1, hn, eidx, gates, rank, cnt = _router(x2, o_a, o_bt, w_out[l].astype(BF16),
                                                 norm_ffn_w[l].reshape(1, d), router_w[l].T,
                                                 router_b[l].reshape(N_EXPERTS, 1), tr)
        n_pad = t * TOP_K + N_EXPERTS * bm
        dest_flat, tables = _routing_tables(cnt, eidx, rank, bm)
        s_words = d // 2 // LANES
        buf = _dispatch(dest_flat, hn.reshape(t, s_words, LANES), n_pad)
        yb = _experts(tables, buf.reshape(n_pad * s_words, LANES), w1[l],
                      b1[l][:, None, 0::2], b1[l][:, None, 1::2], w2[l], b2[l][:, None, :], bm, d)
        y4 = _gather_rows(dest_flat, yb.reshape(n_pad, s_words, LANES))
        x2 = _combine(x1, gates, y4.reshape(TOP_K, t * s_words, LANES), norm_final_w.reshape(1, d), tr,
                      normalize=(l == depth - 1))
    return x2.reshape(bsz, seq, d)
```

```python
import functools
import math

import jax
import jax.numpy as jnp
from jax import lax
from jax.experimental import pallas as pl
from jax.experimental.pallas import tpu as pltpu
from jax.experimental.pallas import tpu_sc as plsc

F32 = jnp.float32
BF16 = jnp.bfloat16
HIGHEST = lax.Precision.HIGHEST

LANES = 128
MXU_COLS = 256
EPS = 1e-6
CHUNK = 64

GDN_HEADS = 4
GDN_D = 128
GDN_CONV = 4
MLA_HEADS = 8
MLA_NOPE = 64
MLA_ROPE = 32
MLA_V = 64
MLA_Q_LORA = 384
MLA_KV_LORA = 256
ROPE_THETA = 10000.0
N_EXPERTS = 32
TOP_K = 4
SWIGLU_LIMIT = 7.0
SWIGLU_ALPHA = 1.702

NEG_BIG = -1e30
VMEM_LIMIT = 52 * 1024 * 1024


def _sigmoid(x):
    return 1.0 / (1.0 + jnp.exp(-x))


def _softplus(x):
    return jnp.maximum(x, 0.0) + jnp.log(1.0 + jnp.exp(-jnp.abs(x)))


def _dot(a, b, precision=None):
    return jnp.dot(a, b, preferred_element_type=F32, precision=precision)


def _dot_nt(a, b, precision=None):
    return lax.dot_general(a, b, (((1,), (1,)), ((), ())), preferred_element_type=F32,
                           precision=precision)


def _dot_tn(a, b):
    return lax.dot_general(a, b, (((0,), (0,)), ((), ())), preferred_element_type=F32)


def _split_bf16(a):
    hi = a.astype(BF16)
    return hi, (a - hi.astype(F32)).astype(BF16)


def _pack_bf16_pair(lo, hi):
    lo_bits = pltpu.bitcast(lo.astype(BF16).astype(F32), jnp.int32)
    hi_bits = pltpu.bitcast(hi.astype(BF16).astype(F32), jnp.int32)
    return jnp.bitwise_or(hi_bits, lax.shift_right_logical(lo_bits, 16))


def _unpack_bf16_pair(w):
    lo = pltpu.bitcast(lax.shift_left(w, 16), F32)
    hi = pltpu.bitcast(jnp.bitwise_and(w, -65536), F32)
    return lo, hi


def _load_packed_rows(ref, lead, m, s):
    return jnp.concatenate([ref[(*lead, pl.ds(j, m, stride=s), slice(None))] for j in range(s)], axis=1)


def _store_packed_rows(ref, words):
    m = words.shape[0]
    s = words.shape[1] // LANES
    for j in range(s):
        ref[pl.ds(j, m, stride=s), :] = words[:, j * LANES:(j + 1) * LANES]


def _dot_split_nt(a, b):
    return _dot_nt(a[0], b[0]) + _dot_nt(a[0], b[1]) + _dot_nt(a[1], b[0])


def _dot_split(a, b):
    return _dot(a[0], b[0]) + _dot(a[0], b[1]) + _dot(a[1], b[0])


C_QKV = 3 * GDN_HEADS * GDN_D
C_Z = GDN_HEADS * GDN_D
C_IN = C_QKV + C_Z + MLA_Q_LORA + MLA_KV_LORA + 3 * LANES


def _inproj_kernel(x_ref, nw_ref, w_ref, qkv_ref, z_ref, cq_ref, ckv_ref, kr_ref, ab_ref):
    x = x_ref[...]
    var = jnp.mean(x * x, axis=-1, keepdims=True)
    h = (x * lax.rsqrt(var + EPS) * nw_ref[...]).astype(BF16)
    p = _dot(h, w_ref[...])
    o = 0
    for ref in (qkv_ref, z_ref, cq_ref, ckv_ref, kr_ref, ab_ref):
        n = ref.shape[-1]
        ref[...] = p[:, o:o + n]
        o += n


def _inproj(x2, norm_w, w_all, tm):
    t, d = x2.shape
    widths = (C_QKV, C_Z, MLA_Q_LORA, MLA_KV_LORA, 2 * LANES, LANES)
    return pl.pallas_call(
        _inproj_kernel,
        grid=(t // tm,),
        in_specs=[pl.BlockSpec((tm, d), lambda i: (i, 0)),
                  pl.BlockSpec((1, d), lambda i: (0, 0)),
                  pl.BlockSpec((d, C_IN), lambda i: (0, 0))],
        out_specs=[pl.BlockSpec((tm, n), lambda i: (i, 0)) for n in widths],
        out_shape=[jax.ShapeDtypeStruct((t, n), F32) for n in widths],
        compiler_params=pltpu.CompilerParams(dimension_semantics=("parallel",),
                                             vmem_limit_bytes=VMEM_LIMIT),
        name="inproj",
    )(x2, norm_w, w_all)


def _gdn_kernel(qkv_ref, z_ref, ab_ref, cw_ref, alog_ref, dtb_ref, nw_ref, o_ref, ext_ref, s_ref):
    c = CHUNK
    bsz, rows, _ = qkv_ref.shape
    chunks_per_batch = rows // c

    @pl.when(pl.program_id(0) == 0)
    def _():
        ext_ref[:, 0:8, :] = jnp.zeros((bsz, 8, C_QKV), F32)
        s_ref[...] = jnp.zeros_like(s_ref)

    cw = cw_ref[...]
    nw = nw_ref[...]
    rr = lax.broadcasted_iota(jnp.int32, (rows, rows), 0)
    rc = lax.broadcasted_iota(jnp.int32, (rows, rows), 1)
    in_chunk_prefix = jnp.where(rr // c == rc // c, jnp.where(rr >= rc, 1.0, 0.0), 0.0)
    e8 = (lax.broadcasted_iota(jnp.int32, (8, LANES), 0)
          == lax.broadcasted_iota(jnp.int32, (8, LANES), 1)).astype(F32)
    ri = lax.broadcasted_iota(jnp.int32, (c, c), 0)
    ci = lax.broadcasted_iota(jnp.int32, (c, c), 1)
    causal = ri >= ci
    strict = ri > ci
    eye = (ri == ci).astype(F32)
    chains = [(j, h) for j in range(chunks_per_batch) for h in range(GDN_HEADS)]

    def prep_pieces(b):
        g = {"xc": [None] * (C_QKV // LANES)}

        def conv(tiles):
            def run():
                cols = slice(tiles[0] * LANES, (tiles[-1] + 1) * LANES)
                x = qkv_ref[b, :, cols]
                ext_ref[b, 8:8 + rows, cols] = x
                ext = ext_ref[b, :, cols]
                xb = cw[3:4, cols] * x
                for j in range(1, GDN_CONV):
                    xb = xb + cw[3 - j:4 - j, cols] * pltpu.roll(ext, j, axis=0)[8:8 + rows, :]
                ext_ref[b, 0:8, cols] = x[rows - 8:rows, :]
                xb = xb * _sigmoid(xb)
                for i, tile in enumerate(tiles):
                    g["xc"][tile] = xb[:, i * LANES:(i + 1) * LANES]
            return run

        def gates():
            ab = ab_ref[b]
            g_all = -jnp.exp(alog_ref[...]) * _softplus(ab + dtb_ref[...])
            g["beta"] = _sigmoid(ab)
            g["g_cum"] = _dot(in_chunk_prefix, g_all, HIGHEST)
            g["g_row"] = _dot_nt(e8, g["g_cum"], HIGHEST)
            for name in ("q", "k", "v", "gc", "bt", "dec"):
                g[name] = []

        def chunk(j):
            def run():
                rs = slice(j * c, (j + 1) * c)
                xc = g["xc"]
                for h in range(GDN_HEADS):
                    q = xc[h][rs]
                    k = xc[GDN_HEADS + h][rs]
                    g["q"].append(q * lax.rsqrt(jnp.sum(q * q, axis=-1, keepdims=True) + EPS) * (GDN_D ** -0.5))
                    g["k"].append(k * lax.rsqrt(jnp.sum(k * k, axis=-1, keepdims=True) + EPS))
                    g["v"].append(xc[2 * GDN_HEADS + h][rs])
                    gc = g["g_cum"][rs, h:h + 1]
                    gr = g["g_row"][h:h + 1, rs]
                    g["gc"].append(gc)
                    g["bt"].append(g["beta"][rs, GDN_HEADS + h:GDN_HEADS + h + 1])
                    g["dec"].append(jnp.exp(jnp.where(causal, gc - gr, -jnp.inf)))
            return run

        def a_mats():
            g["kb"] = [k * bt for k, bt in zip(g["k"], g["bt"])]
            g["k16"] = [k.astype(BF16) for k in g["k"]]
            g["a"] = [jnp.where(strict, _dot_nt(kb.astype(BF16), k16) * dec, 0.0)
                      for kb, k16, dec in zip(g["kb"], g["k16"], g["dec"])]

        n_tiles = C_QKV // LANES
        conv_pieces = [conv(list(range(i, i + 2))) for i in range(0, n_tiles, 2)]
        return g, conv_pieces + [gates] + [chunk(j) for j in range(chunks_per_batch)] + [a_mats]

    def inverse_rounds(g):
        def start():
            g["t"] = [eye - a for a in g["a"]]
            g["pw"] = [_split_bf16(a) for a in g["a"]]

        def square():
            g["pw"] = [_split_bf16(_dot_split(p, p)) for p in g["pw"]]

        def update():
            g["t"] = [t + _dot_split(_split_bf16(t), p) for t, p in zip(g["t"], g["pw"])]

        return [start] + [square, update] * 5

    groups = []
    pending = []
    for b in range(bsz):
        g, pieces = prep_pieces(b)
        groups.append(g)
        n_a, n_b = len(pending), len(pieces)
        ia = ib = 0
        while ia < n_a or ib < n_b:
            if ib >= n_b or (ia < n_a and ia * n_b <= ib * n_a):
                pending[ia]()
                ia += 1
            else:
                pieces[ib]()
                ib += 1
        pending = inverse_rounds(g)
    for fn in pending:
        fn()

    uws, qk16s, qd16s, kd16s, g_tots = [], [], [], [], []
    for g in groups:
        e_gs = [jnp.exp(gc) for gc in g["gc"]]
        uws.append([_dot(t.astype(BF16), jnp.concatenate([v * bt, kb * eg], axis=1).astype(BF16))
                    for t, v, bt, kb, eg in zip(g["t"], g["v"], g["bt"], g["kb"], e_gs)])
        qk16s.append([(_dot_nt(q.astype(BF16), k16) * dec).astype(BF16)
                      for q, k16, dec in zip(g["q"], g["k16"], g["dec"])])
        qd16s.append([(q * eg).astype(BF16) for q, eg in zip(g["q"], e_gs)])
        kd16s.append([(k * jnp.exp(gc[c - 1:c, :] - gc)).astype(BF16) for k, gc in zip(g["k"], g["gc"])])
        g_tots.append([jnp.exp(gc[c - 1:c, :]) for gc in g["gc"]])

    pairs = [(b, h) for b in range(bsz) for h in range(GDN_HEADS)]
    states = [s_ref[b * GDN_HEADS + h] for b, h in pairs]
    for j in range(chunks_per_batch):
        rs = slice(j * c, (j + 1) * c)
        pick = lambda per_group: [per_group[b][chains.index((j, h))] for b, h in pairs]
        uw_j, qk_j, qd_j, kd_j, gt_j = (pick(x) for x in (uws, qk16s, qd16s, kd16s, g_tots))
        s16s = [s.astype(BF16) for s in states]
        v16s = [(uw[:, :GDN_D] - _dot(uw[:, GDN_D:].astype(BF16), s16)).astype(BF16)
                for uw, s16 in zip(uw_j, s16s)]
        states = [s * gt + _dot_tn(kd, v16) for s, gt, kd, v16 in zip(states, gt_j, kd_j, v16s)]
        os_ = [_dot(qd, s16) + _dot(qk, v16) for qd, qk, s16, v16 in zip(qd_j, qk_j, s16s, v16s)]
        for (b, h), o in zip(pairs, os_):
            on = o * lax.rsqrt(jnp.mean(o * o, axis=-1, keepdims=True) + EPS) * nw
            zh = z_ref[b, rs, h * GDN_D:(h + 1) * GDN_D]
            o_ref[b, rs, h * GDN_D:(h + 1) * GDN_D] = on * (zh * _sigmoid(zh))
    for p, s in enumerate(states):
        s_ref[p] = s


def _gdn(qkv, z, ab, conv_w8, alog_row, dtb_row, norm_w, rows):
    bsz, seq, _ = qkv.shape
    blk = lambda i: (0, i, 0)
    const = lambda i: (0, 0)
    return pl.pallas_call(
        _gdn_kernel,
        grid=(seq // rows,),
        in_specs=[pl.BlockSpec((bsz, rows, C_QKV), blk),
                  pl.BlockSpec((bsz, rows, C_Z), blk),
                  pl.BlockSpec((bsz, rows, LANES), blk),
                  pl.BlockSpec((8, C_QKV), const),
                  pl.BlockSpec((1, LANES), const),
                  pl.BlockSpec((1, LANES), const),
                  pl.BlockSpec((1, GDN_D), const)],
        out_specs=pl.BlockSpec((bsz, rows, C_Z), blk),
        out_shape=jax.ShapeDtypeStruct((bsz, seq, C_Z), F32),
        scratch_shapes=[pltpu.VMEM((bsz, 8 + rows, C_QKV), F32),
                        pltpu.VMEM((bsz * GDN_HEADS, GDN_D, GDN_D), F32)],
        compiler_params=pltpu.CompilerParams(dimension_semantics=("arbitrary",),
                                             vmem_limit_bytes=VMEM_LIMIT),
        name="gdn",
    )(qkv, z, ab, conv_w8, alog_row, dtb_row, norm_w)


D_QK_PAD = LANES
C_HEADS = MLA_HEADS * D_QK_PAD
C_V = MLA_HEADS * MLA_V
V_AUG = MLA_V + 16
C_VAUG = MLA_HEADS * V_AUG
Q_SCALE = (MLA_NOPE + MLA_ROPE) ** -0.5 * math.log2(math.e)


def _mla_proj_kernel(cq_ref, ckv_ref, kr_ref, posr_ref, qnw_ref, kvnw_ref, wqt_ref, wqrt_ref,
                     wk_ref, wvt_ref, freqc_ref, qt_ref, k_ref, vt_ref):
    cq = cq_ref[...]
    cqn = (cq * lax.rsqrt(jnp.mean(cq * cq, axis=-1, keepdims=True) + EPS) * qnw_ref[...]).astype(BF16)
    ckv = ckv_ref[...]
    ckvn = (ckv * lax.rsqrt(jnp.mean(ckv * ckv, axis=-1, keepdims=True) + EPS)
            * kvnw_ref[...]).astype(BF16)
    tm = posr_ref.shape[-1]
    ang = freqc_ref[...] * posr_ref[...]
    c16 = jnp.cos(ang)
    s16 = jnp.sin(ang)
    pad = LANES - MLA_NOPE - MLA_ROPE
    cs_t = jnp.concatenate([jnp.ones((MLA_NOPE, tm), F32), c16, c16, jnp.ones((pad, tm), F32)], axis=0)
    sn_t = jnp.concatenate([jnp.zeros((MLA_NOPE, tm), F32), s16, s16, jnp.zeros((pad, tm), F32)], axis=0)
    cs = cs_t.T
    sn = sn_t.T
    qa_t = _dot_nt(wqt_ref[...], cqn)
    qb_t = _dot_nt(wqrt_ref[...], cqn)
    kn = _dot(ckvn, wk_ref[...])
    kr = kr_ref[...]
    kpe = kr[:, :LANES] * cs + kr[:, LANES:] * sn
    for h in range(MLA_HEADS):
        sl = slice(h * LANES, (h + 1) * LANES)
        qt_ref[sl, :] = ((qa_t[sl, :] * cs_t + qb_t[sl, :] * sn_t) * Q_SCALE).astype(BF16)
        k_ref[:, sl] = (kn[:, sl] + kpe).astype(BF16)
    v_t = _dot_nt(wvt_ref[...], ckvn).astype(BF16)
    ones = jnp.ones((V_AUG - MLA_V, v_t.shape[1]), BF16)
    for h in range(MLA_HEADS):
        vt_ref[0, h * V_AUG:h * V_AUG + MLA_V, :] = v_t[h * MLA_V:(h + 1) * MLA_V, :]
        vt_ref[0, h * V_AUG + MLA_V:(h + 1) * V_AUG, :] = ones


def _mla_proj(cq, ckv, kr2, posr, qnw, kvnw, wqt, wqrt, wk, wvt, freqc, tm):
    t = cq.shape[0]
    row = lambda i: (i, 0)
    col = lambda i: (0, i)
    const = lambda i: (0, 0)
    return pl.pallas_call(
        _mla_proj_kernel,
        grid=(t // tm,),
        in_specs=[pl.BlockSpec((tm, MLA_Q_LORA), row),
                  pl.BlockSpec((tm, MLA_KV_LORA), row),
                  pl.BlockSpec((tm, 2 * LANES), row),
                  pl.BlockSpec((1, tm), col),
                  pl.BlockSpec((1, MLA_Q_LORA), const),
                  pl.BlockSpec((1, MLA_KV_LORA), const),
                  pl.BlockSpec((C_HEADS, MLA_Q_LORA), const),
                  pl.BlockSpec((C_HEADS, MLA_Q_LORA), const),
                  pl.BlockSpec((MLA_KV_LORA, C_HEADS), const),
                  pl.BlockSpec((C_V, MLA_KV_LORA), const),
                  pl.BlockSpec((MLA_ROPE // 2, 1), const)],
        out_specs=[pl.BlockSpec((C_HEADS, tm), col),
                   pl.BlockSpec((tm, C_HEADS), row),
                   pl.BlockSpec((1, C_VAUG, tm), lambda i: (i, 0, 0))],
        out_shape=[jax.ShapeDtypeStruct((C_HEADS, t), BF16),
                   jax.ShapeDtypeStruct((t, C_HEADS), BF16),
                   jax.ShapeDtypeStruct((t // tm, C_VAUG, tm), BF16)],
        compiler_params=pltpu.CompilerParams(dimension_semantics=("parallel",),
                                             vmem_limit_bytes=VMEM_LIMIT),
        name="mla_proj",
    )(cq, ckv, kr2, posr, qnw, kvnw, wqt, wqrt, wk, wvt, freqc)


def _attn_kernel(qt_ref, k_ref, vt_ref, nw_ref, ot_ref, m_ref, acc_ref, *, tq):
    i = pl.program_id(1)
    m_ref[...] = jnp.full(m_ref.shape, NEG_BIG, F32)
    acc_ref[...] = jnp.zeros(acc_ref.shape, F32)
    key_chunk = lax.broadcasted_iota(jnp.int32, (tq, tq), 0) // CHUNK
    qry_chunk = lax.broadcasted_iota(jnp.int32, (tq, tq), 1) // CHUNK
    diag_mask = key_chunk <= qry_chunk

    def tile(j, masked):
        rows = pl.ds(pl.multiple_of(j * tq, tq), tq)

        scores = []
        for h in range(MLA_HEADS):
            hs = slice(h * LANES, (h + 1) * LANES)
            s_t = _dot(k_ref[rows, hs], qt_ref[hs, :])
            scores.append(jnp.where(diag_mask, s_t, NEG_BIG) if masked else s_t)
        probs, alphas = [], []
        for h in range(MLA_HEADS):
            m_old = m_ref[h]
            m_new = jnp.maximum(m_old, jnp.max(scores[h], axis=0, keepdims=True))
            m_ref[h] = m_new
            probs.append(jnp.exp2(scores[h] - m_new[0:1, :]).astype(BF16))
            alphas.append(jnp.exp2(m_old[0:1, :] - m_new[0:1, :]))
        for h in range(MLA_HEADS):
            v_aug = vt_ref[j, h * V_AUG:(h + 1) * V_AUG, :]
            acc_ref[h] = alphas[h] * acc_ref[h] + _dot(v_aug, probs[h])

    def full_tile(j, carry):
        tile(j, False)
        return carry

    lax.fori_loop(0, i, full_tile, 0)
    tile(i, True)

    outs = []
    for h in range(MLA_HEADS):
        acc = acc_ref[h]
        outs.append(acc[:MLA_V, :] / acc[MLA_V:MLA_V + 1, :])
    ssq = outs[0] * outs[0]
    for o in outs[1:]:
        ssq = ssq + o * o
    inv = lax.rsqrt(jnp.sum(ssq, axis=0, keepdims=True) / C_V + EPS)
    for h in range(MLA_HEADS):
        vs = slice(h * MLA_V, (h + 1) * MLA_V)
        ot_ref[vs, :] = outs[h] * inv * nw_ref[vs, :]


def _attention(qt, k, vt, norm_w_col, bsz, seq, tq):
    nq = seq // tq
    return pl.pallas_call(
        functools.partial(_attn_kernel, tq=tq),
        grid=(bsz, nq),
        in_specs=[pl.BlockSpec((C_HEADS, tq), lambda b, i: (0, b * nq + i)),
                  pl.BlockSpec((seq, C_HEADS), lambda b, i: (b, 0)),
                  pl.BlockSpec((nq, C_VAUG, tq), lambda b, i: (b, 0, 0)),
                  pl.BlockSpec((C_V, 1), lambda b, i: (0, 0))],
        out_specs=pl.BlockSpec((C_V, tq), lambda b, i: (0, b * nq + i)),
        out_shape=jax.ShapeDtypeStruct((C_V, bsz * seq), F32),
        scratch_shapes=[pltpu.VMEM((MLA_HEADS, 8, tq), F32),
                        pltpu.VMEM((MLA_HEADS, V_AUG, tq), F32)],
        compiler_params=pltpu.CompilerParams(dimension_semantics=("parallel", "parallel"),
                                             vmem_limit_bytes=VMEM_LIMIT),
        name="attention",
    )(qt, k, vt, norm_w_col)


def _router_kernel(x_ref, oa_ref, obt_ref, wo_ref, nw_ref, rw_ref, rb_ref,
                   x1_ref, h_ref, eidx_ref, gate_ref, rank_ref, cnt_ref, run_ref, *, tm):
    @pl.when(pl.program_id(0) == 0)
    def _():
        run_ref[...] = jnp.zeros_like(run_ref)

    half = oa_ref.shape[-1]
    y = (_dot(oa_ref[...].astype(BF16), wo_ref[0:half, :])
         + _dot_tn(obt_ref[...].astype(BF16), wo_ref[half:, :]))
    x1 = x_ref[...] + y
    x1_ref[...] = x1
    hn = x1 * lax.rsqrt(jnp.mean(x1 * x1, axis=-1, keepdims=True) + EPS) * nw_ref[...]
    d_half = hn.shape[-1] // 2
    _store_packed_rows(h_ref, _pack_bf16_pair(hn[:, :d_half], hn[:, d_half:]))
    lg = _dot_split_nt(_split_bf16(rw_ref[...]), _split_bf16(hn)) + rb_ref[...]
    expert = lax.broadcasted_iota(jnp.int32, lg.shape, 0)
    sels, tops, idxs = [], [], []
    for _ in range(TOP_K):
        m = jnp.max(lg, axis=0, keepdims=True)
        idx = jnp.min(jnp.where(lg == m, expert, N_EXPERTS), axis=0, keepdims=True)
        sel = expert == idx
        lg = jnp.where(sel, -jnp.inf, lg)
        sels.append(sel)
        tops.append(m)
        idxs.append(idx)
    exps = [jnp.exp(tv - tops[0]) for tv in tops]
    den = exps[0] + exps[1] + exps[2] + exps[3]
    multi = sels[0].astype(F32)
    for sel in sels[1:]:
        multi = multi + sel.astype(F32)
    ri = lax.broadcasted_iota(jnp.int32, (tm, tm), 0)
    ci = lax.broadcasted_iota(jnp.int32, (tm, tm), 1)
    before = _dot(multi.astype(BF16), (ri < ci).astype(BF16)) + run_ref[...]
    ranks = [jnp.sum(jnp.where(sel, before, 0.0), axis=0, keepdims=True).astype(jnp.int32) for sel in sels]
    run = run_ref[...] + jnp.sum(multi, axis=1, keepdims=True)
    run_ref[...] = run
    cnt_ref[...] = jnp.broadcast_to(run, cnt_ref.shape)
    pad_i = jnp.zeros((8 - TOP_K, tm), jnp.int32)
    eidx_ref[...] = jnp.concatenate(idxs + [pad_i], axis=0)
    rank_ref[...] = jnp.concatenate(ranks + [pad_i], axis=0)
    gates_t = jnp.concatenate([e / den for e in exps] + [jnp.zeros((LANES - TOP_K, tm), F32)], axis=0)
    eye = (ri == ci).astype(BF16)
    g_hi, g_lo = _split_bf16(gates_t)
    gate_ref[...] = _dot_nt(eye, g_hi) + _dot_nt(eye, g_lo)


def _router(x2, oa, obt, w_out, norm_w, rw, rb, tm):
    t, d = x2.shape
    half = oa.shape[-1]
    s_words = d // 2 // LANES
    row = lambda i: (i, 0)
    col = lambda i: (0, i)
    const = lambda i: (0, 0)
    return pl.pallas_call(
        functools.partial(_router_kernel, tm=tm),
        grid=(t // tm,),
        in_specs=[pl.BlockSpec((tm, d), row),
                  pl.BlockSpec((tm, half), row),
                  pl.BlockSpec((half, tm), lambda i: (0, i)),
                  pl.BlockSpec((2 * half, d), const),
                  pl.BlockSpec((1, d), const),
                  pl.BlockSpec((N_EXPERTS, d), const),
                  pl.BlockSpec((N_EXPERTS, 1), const)],
        out_specs=[pl.BlockSpec((tm, d), row), pl.BlockSpec((tm * s_words, LANES), row),
                   pl.BlockSpec((8, tm), col), pl.BlockSpec((tm, LANES), row),
                   pl.BlockSpec((8, tm), col), pl.BlockSpec((N_EXPERTS, LANES), const)],
        out_shape=[jax.ShapeDtypeStruct((t, d), F32), jax.ShapeDtypeStruct((t * s_words, LANES), jnp.int32),
                   jax.ShapeDtypeStruct((8, t), jnp.int32), jax.ShapeDtypeStruct((t, LANES), F32),
                   jax.ShapeDtypeStruct((8, t), jnp.int32), jax.ShapeDtypeStruct((N_EXPERTS, LANES), F32)],
        scratch_shapes=[pltpu.VMEM((N_EXPERTS, 1), F32)],
        compiler_params=pltpu.CompilerParams(dimension_semantics=("arbitrary",),
                                             vmem_limit_bytes=VMEM_LIMIT),
        name="router",
    )(x2, oa, obt, w_out, norm_w, rw, rb)


SC_ROWS = 64


def _sc_mesh():
    return plsc.VectorSubcoreMesh(core_axis_name="c", subcore_axis_name="s")


def _dispatch(dest_flat, h3, n_pad):
    t, s, _ = h3.shape
    mesh = _sc_mesh()
    n_workers = mesh.num_cores * mesh.num_subcores
    per_worker = t // n_workers
    assert per_worker % SC_ROWS == 0

    @functools.partial(
        pl.kernel, mesh=mesh, out_type=jax.ShapeDtypeStruct((n_pad, s, LANES), h3.dtype),
        scratch_types=[pltpu.VMEM((TOP_K, SC_ROWS), jnp.int32), pltpu.VMEM((SC_ROWS, s, LANES), h3.dtype),
                       pltpu.SemaphoreType.DMA],
        name="sc_dispatch")
    def scatter(h_hbm, dest_hbm, out_hbm, idx_v, rows_v, sem):
        base = (lax.axis_index("s") * mesh.num_cores + lax.axis_index("c")) * per_worker

        @pl.loop(0, per_worker // SC_ROWS)
        def _(i):
            off = pl.multiple_of(base + i * SC_ROWS, SC_ROWS)
            pltpu.sync_copy(h_hbm.at[pl.ds(off, SC_ROWS)], rows_v)
            for kk in range(TOP_K):
                pltpu.sync_copy(dest_hbm.at[pl.ds(kk * t + off, SC_ROWS)], idx_v.at[kk])
            copies = [pltpu.async_copy(rows_v, out_hbm.at[idx_v.at[kk]], sem) for kk in range(TOP_K)]
            for cp in copies:
                cp.wait()

    return scatter(h3, dest_flat)


def _expert_kernel(cnt_ref, blk0_ref, nblk_ref, total_ref, x_hbm, w1_ref, b1g_ref, b1l_ref, w2_ref, b2_ref,
                   y_hbm, w1s_ref, w2s_ref, xbuf, ybuf, xsem, ysem, *, bm):
    e = pl.program_id(0)
    n_blocks = nblk_ref[e]
    first = blk0_ref[e]
    total = total_ref[0]
    rows_per_block = xbuf.shape[1]
    s_words = rows_per_block // bm
    n_tiles = w1s_ref.shape[-1] // MXU_COLS
    half = MXU_COLS // 2

    def x_copy(g, slot):
        return pltpu.make_async_copy(x_hbm.at[pl.ds(pl.multiple_of(g * rows_per_block, rows_per_block),
                                                    rows_per_block)], xbuf.at[slot], xsem.at[slot])

    def y_copy(g, slot):
        return pltpu.make_async_copy(ybuf.at[slot], y_hbm.at[pl.ds(pl.multiple_of(g * rows_per_block,
                                                                                   rows_per_block),
                                                                   rows_per_block)], ysem.at[slot])

    @pl.when(e == 0)
    def _():
        x_copy(0, 0).start()

    @pl.when(n_blocks > 0)
    def _():
        r = lax.broadcasted_iota(jnp.int32, (MXU_COLS, MXU_COLS), 0)
        c = lax.broadcasted_iota(jnp.int32, (MXU_COLS, MXU_COLS), 1)
        perm = (r == jnp.where(c < half, 2 * c, 2 * (c - half) + 1)).astype(BF16)
        for tix in range(n_tiles):
            cols = slice(tix * MXU_COLS, (tix + 1) * MXU_COLS)
            w1s_ref[:, cols] = _dot(w1_ref[0, :, cols].astype(BF16), perm).astype(BF16)
        w2s_ref[...] = w2_ref[0].astype(BF16)

        def block(j, carry):
            g = first + j
            slot = lax.rem(g, 2)
            x_copy(g, slot).wait()

            @pl.when(g + 1 < total)
            def _():
                x_copy(g + 1, 1 - slot).start()

            @pl.when(g >= 2)
            def _():
                y_copy(g - 2, slot).wait()

            n_rows = cnt_ref[e] - j * bm

            def ffn(m):
                words = _load_packed_rows(xbuf, (slot,), m, s_words)
                row = lax.broadcasted_iota(jnp.int32, words.shape, 0)
                x_lo, x_hi = _unpack_bf16_pair(jnp.where(row < n_rows, words, 0))
                x = jnp.concatenate([x_lo.astype(BF16), x_hi.astype(BF16)], axis=1)
                acts = []
                for tix in range(n_tiles):
                    hp = _dot(x, w1s_ref[:, tix * MXU_COLS:(tix + 1) * MXU_COLS])
                    feat = slice(tix * half, (tix + 1) * half)
                    gl = jnp.minimum(hp[:, :half] + b1g_ref[0][:, feat], SWIGLU_LIMIT)
                    lin = jnp.clip(hp[:, half:] + b1l_ref[0][:, feat], -SWIGLU_LIMIT, SWIGLU_LIMIT)
                    acts.append((gl * _sigmoid(SWIGLU_ALPHA * gl) * (lin + 1.0)).astype(BF16))
                y = _dot(jnp.concatenate(acts, axis=1), w2s_ref[...]) + b2_ref[0]
                d_half = y.shape[-1] // 2
                _store_packed_rows(ybuf.at[slot], _pack_bf16_pair(y[:, :d_half], y[:, d_half:]))

            @pl.when(n_rows > bm // 2)
            def _():
                ffn(bm)

            @pl.when(n_rows <= bm // 2)
            def _():
                ffn(bm // 2)

            y_copy(g, slot).start()
            return carry

        lax.fori_loop(0, n_blocks, block, 0)

    @pl.when(e == pl.num_programs(0) - 1)
    def _():
        @pl.when(total >= 2)
        def _():
            y_copy(total - 2, lax.rem(total, 2)).wait()

        y_copy(total - 1, lax.rem(total - 1, 2)).wait()


def _experts(tables, xs, w1, b1g, b1l, w2, b2, bm, d):
    s_words = d // 2 // LANES
    dff2 = w1.shape[-1]
    dff = dff2 // 2
    emap = lambda e, *_: (e, 0, 0)
    grid_spec = pltpu.PrefetchScalarGridSpec(
        num_scalar_prefetch=len(tables),
        grid=(N_EXPERTS,),
        in_specs=[pl.BlockSpec(memory_space=pl.ANY),
                  pl.BlockSpec((1, d, dff2), emap),
                  pl.BlockSpec((1, 1, dff), emap),
                  pl.BlockSpec((1, 1, dff), emap),
                  pl.BlockSpec((1, dff, d), emap),
                  pl.BlockSpec((1, 1, d), emap)],
        out_specs=pl.BlockSpec(memory_space=pl.ANY),
        scratch_shapes=[pltpu.VMEM((d, dff2), BF16), pltpu.VMEM((dff, d), BF16),
                        pltpu.VMEM((2, bm * s_words, LANES), jnp.int32),
                        pltpu.VMEM((2, bm * s_words, LANES), jnp.int32),
                        pltpu.SemaphoreType.DMA((2,)), pltpu.SemaphoreType.DMA((2,))],
    )
    return pl.pallas_call(
        functools.partial(_expert_kernel, bm=bm),
        grid_spec=grid_spec,
        out_shape=jax.ShapeDtypeStruct(xs.shape, jnp.int32),
        compiler_params=pltpu.CompilerParams(dimension_semantics=("arbitrary",),
                                             vmem_limit_bytes=VMEM_LIMIT),
        name="experts",
    )(*tables, xs, w1, b1g, b1l, w2, b2)


def _gather_rows(dest_flat, y3):
    n_out = dest_flat.shape[0]
    _, s, _ = y3.shape
    mesh = _sc_mesh()
    n_workers = mesh.num_cores * mesh.num_subcores
    per_worker = n_out // n_workers
    n_chunks = per_worker // SC_ROWS
    assert per_worker % SC_ROWS == 0 and n_chunks % 2 == 0

    @functools.partial(
        pl.kernel, mesh=mesh, out_type=jax.ShapeDtypeStruct((n_out, s, LANES), y3.dtype),
        scratch_types=[pltpu.VMEM((2, SC_ROWS), jnp.int32), pltpu.VMEM((2, SC_ROWS, s, LANES), y3.dtype),
                       pltpu.SemaphoreType.DMA((2,)), pltpu.SemaphoreType.DMA((2,))],
        name="sc_gather")
    def gather(y_hbm, dest_hbm, out_hbm, idx_v, rows_v, gsem, wsem):
        base = (lax.axis_index("s") * mesh.num_cores + lax.axis_index("c")) * per_worker

        def rows_of(chunk):
            return pl.ds(pl.multiple_of(base + chunk * SC_ROWS, SC_ROWS), SC_ROWS)

        def gather_copy(slot):
            return pltpu.make_async_copy(y_hbm.at[idx_v.at[slot]], rows_v.at[slot], gsem.at[slot])

        def write_copy(chunk, slot):
            return pltpu.make_async_copy(rows_v.at[slot], out_hbm.at[rows_of(chunk)], wsem.at[slot])

        def start_gather(chunk, slot):
            pltpu.sync_copy(dest_hbm.at[rows_of(chunk)], idx_v.at[slot])
            gather_copy(slot).start()

        start_gather(0, 0)

        @pl.loop(0, n_chunks // 2)
        def _(pair):
            for slot in (0, 1):
                chunk = 2 * pair + slot
                gather_copy(slot).wait()
                write_copy(chunk, slot).start()

                @pl.when(chunk >= 1)
                def _():
                    write_copy(chunk - 1, 1 - slot).wait()

                @pl.when(chunk + 1 < n_chunks)
                def _():
                    start_gather(chunk + 1, 1 - slot)

        write_copy(n_chunks - 1, 1).wait()

    return gather(y3, dest_flat)


def _combine_kernel(x1_ref, gate_ref, y4_ref, nw_ref, o_ref, *, tm, normalize):
    x = x1_ref[...]
    gates = gate_ref[...]
    for kk in range(TOP_K):
        lo, hi = _unpack_bf16_pair(_load_packed_rows(y4_ref, (kk,), tm, y4_ref.shape[1] // tm))
        x = x + gates[:, kk:kk + 1] * jnp.concatenate([lo, hi], axis=1)
    if normalize:
        x = x * lax.rsqrt(jnp.mean(x * x, axis=-1, keepdims=True) + EPS) * nw_ref[...]
    o_ref[...] = x


def _combine(x1, gates, y4, norm_w, tm, normalize):
    t, d = x1.shape
    s_words = d // 2 // LANES
    row = lambda i: (i, 0)
    return pl.pallas_call(
        functools.partial(_combine_kernel, tm=tm, normalize=normalize),
        grid=(t // tm,),
        in_specs=[pl.BlockSpec((tm, d), row), pl.BlockSpec((tm, LANES), row),
                  pl.BlockSpec((TOP_K, tm * s_words, LANES), lambda i: (0, i, 0)),
                  pl.BlockSpec((1, d), lambda i: (0, 0))],
        out_specs=pl.BlockSpec((tm, d), row),
        out_shape=jax.ShapeDtypeStruct((t, d), F32),
        compiler_params=pltpu.CompilerParams(dimension_semantics=("parallel",)),
        name="combine",
    )(x1, gates, y4, norm_w)


def _tiles(t, seq):
    tm = min(2 * MXU_COLS, t)
    tr = min(2 * MXU_COLS, t)
    tq = min(2 * MXU_COLS, seq)
    gdn_rows = min(2 * CHUNK, seq)
    bm = 2 * MXU_COLS
    assert t % tm == 0 and t % tr == 0 and seq % tq == 0 and seq % gdn_rows == 0 and tq % CHUNK == 0
    return tm, tr, tq, gdn_rows, bm


def _rot_half(w):
    half = w.shape[-1] // 2
    return jnp.concatenate([-w[..., half:], w[..., :half]], axis=-1)


def _prep_inproj_weight(w_in):
    d = w_in.shape[0]
    hd = GDN_HEADS * GDN_D
    o = 0
    parts = {}
    for name, n in (("qkv", 3 * hd), ("z", hd), ("a", GDN_HEADS), ("b", GDN_HEADS),
                    ("cq", MLA_Q_LORA), ("ckv", MLA_KV_LORA), ("kr", MLA_ROPE)):
        parts[name] = w_in[:, o:o + n]
        o += n
    zeros = lambda n: jnp.zeros((d, n), w_in.dtype)
    kr_tile = jnp.concatenate([zeros(MLA_NOPE), parts["kr"], zeros(LANES - MLA_NOPE - MLA_ROPE)], axis=1)
    krr_tile = jnp.concatenate([zeros(MLA_NOPE), _rot_half(parts["kr"]),
                                zeros(LANES - MLA_NOPE - MLA_ROPE)], axis=1)
    ab_tile = jnp.concatenate([parts["a"], parts["b"], zeros(LANES - 2 * GDN_HEADS)], axis=1)
    return jnp.concatenate([parts["qkv"], parts["z"], parts["cq"], parts["ckv"], kr_tile, krr_tile,
                            ab_tile], axis=1).astype(BF16)


def _prep_mla_weights(w_uq, w_ukv):
    dq = MLA_NOPE + MLA_ROPE
    wq = w_uq.reshape(MLA_Q_LORA, MLA_HEADS, dq)
    wq_rot = jnp.concatenate([jnp.zeros_like(wq[..., :MLA_NOPE]), _rot_half(wq[..., MLA_NOPE:])], axis=-1)
    pad = lambda w: jnp.pad(w, ((0, 0), (0, 0), (0, LANES - w.shape[-1]))).reshape(w.shape[0], C_HEADS)
    wkv = w_ukv.reshape(MLA_KV_LORA, MLA_HEADS, MLA_NOPE + MLA_V)
    wv = wkv[..., MLA_NOPE:].reshape(MLA_KV_LORA, C_V)
    return (pad(wq).T.astype(BF16), pad(wq_rot).T.astype(BF16), pad(wkv[..., :MLA_NOPE]).astype(BF16),
            wv.T.astype(BF16))


def _rope_freq():
    half = MLA_ROPE // 2
    return (ROPE_THETA ** (-jnp.arange(half, dtype=F32) / half)).reshape(half, 1)


def _lane_row(v):
    return jnp.pad(v.astype(F32), (0, LANES - v.shape[0])).reshape(1, LANES)


def _routing_tables(cnt, eidx, rank, bm):
    e_ids = jnp.arange(N_EXPERTS, dtype=jnp.int32)
    counts = cnt[:, 0].astype(jnp.int32)
    padded = (counts + bm - 1) // bm * bm
    pend = jnp.sum(jnp.where(e_ids[None, :] <= e_ids[:, None], padded[None, :], 0), axis=1)
    pstart = pend - padded
    dest = jnp.sum(jnp.where(eidx[:TOP_K, :, None] == e_ids, pstart, 0), axis=-1) + rank[:TOP_K]
    tables = (counts, pstart // bm, padded // bm, pend[-1:] // bm)
    return dest.reshape(-1), tuple(t.astype(jnp.int32) for t in tables)


def kernel(x, positions, norm_mix_w, w_in, conv_w, a_log, dt_bias, gdn_norm_w, q_norm_w, w_uq, kv_norm_w,
           w_ukv, mla_out_norm_w, w_out, norm_ffn_w, router_w, router_b, w1, b1, w2, b2, norm_final_w):
    bsz, seq, d = x.shape
    t = bsz * seq
    depth = w_in.shape[0]
    tm, tr, tq, gdn_rows, bm = _tiles(t, seq)
    x2 = x.reshape(t, d)
    posf = positions.astype(F32)
    freq = _rope_freq()
    for l in range(depth):
        qkv, z, cq, ckv, kr2, ab = _inproj(x2, norm_mix_w[l].reshape(1, d), _prep_inproj_weight(w_in[l]), tm)
        conv_w8 = jnp.pad(conv_w[l], ((0, 8 - GDN_CONV), (0, 0)))
        o_a = _gdn(qkv.reshape(bsz, seq, C_QKV), z.reshape(bsz, seq, C_Z), ab.reshape(bsz, seq, LANES),
                   conv_w8, _lane_row(a_log[l]), _lane_row(dt_bias[l]), gdn_norm_w[l].reshape(1, GDN_D),
                   gdn_rows).reshape(t, C_Z)
        wqt, wqrt, wk, wvt = _prep_mla_weights(w_uq[l], w_ukv[l])
        qt, k, vt = _mla_proj(cq, ckv, kr2, posf.reshape(1, t),
                              q_norm_w[l].reshape(1, -1), kv_norm_w[l].reshape(1, -1),
                              wqt, wqrt, wk, wvt, freq, tq)
        o_bt = _attention(qt, k, vt, mla_out_norm_w[l].reshape(-1, 1), bsz, seq, tq)
        x1, hn, eidx, gates, rank, cnt = _router(x2, o_a, o_bt, w_out[l].astype(BF16),
                                                 norm_ffn_w[l].reshape(1, d), router_w[l].T,
                                                 router_b[l].reshape(N_EXPERTS, 1), tr)
        n_pad = t * TOP_K + N_EXPERTS * bm
        dest_flat, tables = _routing_tables(cnt, eidx, rank, bm)
        s_words = d // 2 // LANES
        buf = _dispatch(dest_flat, hn.reshape(t, s_words, LANES), n_pad)
        yb = _experts(tables, buf.reshape(n_pad * s_words, LANES), w1[l],
                      b1[l][:, None, 0::2], b1[l][:, None, 1::2], w2[l], b2[l][:, None, :], bm, d)
        y4 = _gather_rows(dest_flat, yb.reshape(n_pad, s_words, LANES))
        x2 = _combine(x1, gates, y4.reshape(TOP_K, t * s_words, LANES), norm_final_w.reshape(1, d), tr,
                      normalize=(l == depth - 1))
    return x2.reshape(bsz, seq, d)
```

```python
import functools
import math

import jax
import jax.numpy as jnp
from jax import lax
from jax.experimental import pallas as pl
from jax.experimental.pallas import tpu as pltpu
from jax.experimental.pallas import tpu_sc as plsc

F32 = jnp.float32
BF16 = jnp.bfloat16
HIGHEST = lax.Precision.HIGHEST

LANES = 128
MXU_COLS = 256
EPS = 1e-6
CHUNK = 64

GDN_HEADS = 4
GDN_D = 128
GDN_CONV = 4
MLA_HEADS = 8
MLA_NOPE = 64
MLA_ROPE = 32
MLA_V = 64
MLA_Q_LORA = 384
MLA_KV_LORA = 256
ROPE_THETA = 10000.0
N_EXPERTS = 32
TOP_K = 4
SWIGLU_LIMIT = 7.0
SWIGLU_ALPHA = 1.702

NEG_BIG = -1e30
VMEM_LIMIT = 52 * 1024 * 1024


def _sigmoid(x):
    return 1.0 / (1.0 + jnp.exp(-x))


def _softplus(x):
    return jnp.maximum(x, 0.0) + jnp.log(1.0 + jnp.exp(-jnp.abs(x)))


def _dot(a, b, precision=None):
    return jnp.dot(a, b, preferred_element_type=F32, precision=precision)


def _dot_nt(a, b, precision=None):
    return lax.dot_general(a, b, (((1,), (1,)), ((), ())), preferred_element_type=F32,
                           precision=precision)


def _dot_tn(a, b):
    return lax.dot_general(a, b, (((0,), (0,)), ((), ())), preferred_element_type=F32)


def _split_bf16(a):
    hi = a.astype(BF16)
    return hi, (a - hi.astype(F32)).astype(BF16)


def _pack_bf16_pair(lo, hi):
    lo_bits = pltpu.bitcast(lo.astype(BF16).astype(F32), jnp.int32)
    hi_bits = pltpu.bitcast(hi.astype(BF16).astype(F32), jnp.int32)
    return jnp.bitwise_or(hi_bits, lax.shift_right_logical(lo_bits, 16))


def _unpack_bf16_pair(w):
    lo = pltpu.bitcast(lax.shift_left(w, 16), F32)
    hi = pltpu.bitcast(jnp.bitwise_and(w, -65536), F32)
    return lo, hi


def _load_packed_rows(ref, lead, m, s):
    return jnp.concatenate([ref[(*lead, pl.ds(j, m, stride=s), slice(None))] for j in range(s)], axis=1)


def _store_packed_rows(ref, words):
    m = words.shape[0]
    s = words.shape[1] // LANES
    for j in range(s):
        ref[pl.ds(j, m, stride=s), :] = words[:, j * LANES:(j + 1) * LANES]


def _dot_split_nt(a, b):
    return _dot_nt(a[0], b[0]) + _dot_nt(a[0], b[1]) + _dot_nt(a[1], b[0])


def _dot_split(a, b):
    return _dot(a[0], b[0]) + _dot(a[0], b[1]) + _dot(a[1], b[0])


C_QKV = 3 * GDN_HEADS * GDN_D
C_Z = GDN_HEADS * GDN_D
C_IN = C_QKV + C_Z + MLA_Q_LORA + MLA_KV_LORA + 3 * LANES


def _inproj_kernel(x_ref, nw_ref, w_ref, qkv_ref, z_ref, cq_ref, ckv_ref, kr_ref, ab_ref):
    x = x_ref[...]
    var = jnp.mean(x * x, axis=-1, keepdims=True)
    h = (x * lax.rsqrt(var + EPS) * nw_ref[...]).astype(BF16)
    p = _dot(h, w_ref[...])
    o = 0
    for ref in (qkv_ref, z_ref, cq_ref, ckv_ref, kr_ref, ab_ref):
        n = ref.shape[-1]
        ref[...] = p[:, o:o + n]
        o += n


def _inproj(x2, norm_w, w_all, tm):
    t, d = x2.shape
    widths = (C_QKV, C_Z, MLA_Q_LORA, MLA_KV_LORA, 2 * LANES, LANES)
    return pl.pallas_call(
        _inproj_kernel,
        grid=(t // tm,),
        in_specs=[pl.BlockSpec((tm, d), lambda i: (i, 0)),
                  pl.BlockSpec((1, d), lambda i: (0, 0)),
                  pl.BlockSpec((d, C_IN), lambda i: (0, 0))],
        out_specs=[pl.BlockSpec((tm, n), lambda i: (i, 0)) for n in widths],
        out_shape=[jax.ShapeDtypeStruct((t, n), F32) for n in widths],
        compiler_params=pltpu.CompilerParams(dimension_semantics=("parallel",),
                                             vmem_limit_bytes=VMEM_LIMIT),
        name="inproj",
    )(x2, norm_w, w_all)


def _gdn_kernel(qkv_ref, z_ref, ab_ref, cw_ref, alog_ref, dtb_ref, nw_ref, o_ref, ext_ref, s_ref):
    c = CHUNK
    bsz, rows, _ = qkv_ref.shape
    chunks_per_batch = rows // c

    @pl.when(pl.program_id(0) == 0)
    def _():
        ext_ref[:, 0:8, :] = jnp.zeros((bsz, 8, C_QKV), F32)
        s_ref[...] = jnp.zeros_like(s_ref)

    cw = cw_ref[...]
    nw = nw_ref[...]
    rr = lax.broadcasted_iota(jnp.int32, (rows, rows), 0)
    rc = lax.broadcasted_iota(jnp.int32, (rows, rows), 1)
    in_chunk_prefix = jnp.where(rr // c == rc // c, jnp.where(rr >= rc, 1.0, 0.0), 0.0)
    e8 = (lax.broadcasted_iota(jnp.int32, (8, LANES), 0)
          == lax.broadcasted_iota(jnp.int32, (8, LANES), 1)).astype(F32)
    ri = lax.broadcasted_iota(jnp.int32, (c, c), 0)
    ci = lax.broadcasted_iota(jnp.int32, (c, c), 1)
    causal = ri >= ci
    strict = ri > ci
    eye = (ri == ci).astype(F32)
    chains = [(j, h) for j in range(chunks_per_batch) for h in range(GDN_HEADS)]

    def prep_pieces(b):
        g = {"xc": [None] * (C_QKV // LANES)}

        def conv(tiles):
            def run():
                cols = slice(tiles[0] * LANES, (tiles[-1] + 1) * LANES)
                x = qkv_ref[b, :, cols]
                ext_ref[b, 8:8 + rows, cols] = x
                ext = ext_ref[b, :, cols]
                xb = cw[3:4, cols] * x
                for j in range(1, GDN_CONV):
                    xb = xb + cw[3 - j:4 - j, cols] * pltpu.roll(ext, j, axis=0)[8:8 + rows, :]
                ext_ref[b, 0:8, cols] = x[rows - 8:rows, :]
                xb = xb * _sigmoid(xb)
                for i, tile in enumerate(tiles):
                    g["xc"][tile] = xb[:, i * LANES:(i + 1) * LANES]
            return run

        def gates():
            ab = ab_ref[b]
            g_all = -jnp.exp(alog_ref[...]) * _softplus(ab + dtb_ref[...])
            g["beta"] = _sigmoid(ab)
            g["g_cum"] = _dot(in_chunk_prefix, g_all, HIGHEST)
            g["g_row"] = _dot_nt(e8, g["g_cum"], HIGHEST)
            for name in ("q", "k", "v", "gc", "bt", "dec"):
                g[name] = []

        def chunk(j):
            def run():
                rs = slice(j * c, (j + 1) * c)
                xc = g["xc"]
                for h in range(GDN_HEADS):
                    q = xc[h][rs]
                    k = xc[GDN_HEADS + h][rs]
                    g["q"].append(q * lax.rsqrt(jnp.sum(q * q, axis=-1, keepdims=True) + EPS) * (GDN_D ** -0.5))
                    g["k"].append(k * lax.rsqrt(jnp.sum(k * k, axis=-1, keepdims=True) + EPS))
                    g["v"].append(xc[2 * GDN_HEADS + h][rs])
                    gc = g["g_cum"][rs, h:h + 1]
                    gr = g["g_row"][h:h + 1, rs]
                    g["gc"].append(gc)
                    g["bt"].append(g["beta"][rs, GDN_HEADS + h:GDN_HEADS + h + 1])
                    g["dec"].append(jnp.exp(jnp.where(causal, gc - gr, -jnp.inf)))
            return run

        def a_mats():
            g["kb"] = [k * bt for k, bt in zip(g["k"], g["bt"])]
            g["k16"] = [k.astype(BF16) for k in g["k"]]
            g["a"] = [jnp.where(strict, _dot_nt(kb.astype(BF16), k16) * dec, 0.0)
                      for kb, k16, dec in zip(g["kb"], g["k16"], g["dec"])]

        n_tiles = C_QKV // LANES
        conv_pieces = [conv(list(range(i, i + 2))) for i in range(0, n_tiles, 2)]
        return g, conv_pieces + [gates] + [chunk(j) for j in range(chunks_per_batch)] + [a_mats]

    def inverse_rounds(g):
        def start():
            g["t"] = [eye - a for a in g["a"]]
            g["pw"] = [_split_bf16(a) for a in g["a"]]

        def square():
            g["pw"] = [_split_bf16(_dot_split(p, p)) for p in g["pw"]]

        def update():
            g["t"] = [t + _dot_split(_split_bf16(t), p) for t, p in zip(g["t"], g["pw"])]

        return [start] + [square, update] * 5

    groups = []
    pending = []
    for b in range(bsz):
        g, pieces = prep_pieces(b)
        groups.append(g)
        n_a, n_b = len(pending), len(pieces)
        ia = ib = 0
        while ia < n_a or ib < n_b:
            if ib >= n_b or (ia < n_a and ia * n_b <= ib * n_a):
                pending[ia]()
                ia += 1
            else:
                pieces[ib]()
                ib += 1
        pending = inverse_rounds(g)
    for fn in pending:
        fn()

    uws, qk16s, qd16s, kd16s, g_tots = [], [], [], [], []
    for g in groups:
        e_gs = [jnp.exp(gc) for gc in g["gc"]]
        uws.append([_dot(t.astype(BF16), jnp.concatenate([v * bt, kb * eg], axis=1).astype(BF16))
                    for t, v, bt, kb, eg in zip(g["t"], g["v"], g["bt"], g["kb"], e_gs)])
        qk16s.append([(_dot_nt(q.astype(BF16), k16) * dec).astype(BF16)
                      for q, k16, dec in zip(g["q"], g["k16"], g["dec"])])
        qd16s.append([(q * eg).astype(BF16) for q, eg in zip(g["q"], e_gs)])
        kd16s.append([(k * jnp.exp(gc[c - 1:c, :] - gc)).astype(BF16) for k, gc in zip(g["k"], g["gc"])])
        g_tots.append([jnp.exp(gc[c - 1:c, :]) for gc in g["gc"]])

    pairs = [(b, h) for b in range(bsz) for h in range(GDN_HEADS)]
    states = [s_ref[b * GDN_HEADS + h] for b, h in pairs]
    for j in range(chunks_per_batch):
        rs = slice(j * c, (j + 1) * c)
        pick = lambda per_group: [per_group[b][chains.index((j, h))] for b, h in pairs]
        uw_j, qk_j, qd_j, kd_j, gt_j = (pick(x) for x in (uws, qk16s, qd16s, kd16s, g_tots))
        s16s = [s.astype(BF16) for s in states]
        v16s = [(uw[:, :GDN_D] - _dot(uw[:, GDN_D:].astype(BF16), s16)).astype(BF16)
                for uw, s16 in zip(uw_j, s16s)]
        states = [s * gt + _dot_tn(kd, v16) for s, gt, kd, v16 in zip(states, gt_j, kd_j, v16s)]
        os_ = [_dot(qd, s16) + _dot(qk, v16) for qd, qk, s16, v16 in zip(qd_j, qk_j, s16s, v16s)]
        for (b, h), o in zip(pairs, os_):
            on = o * lax.rsqrt(jnp.mean(o * o, axis=-1, keepdims=True) + EPS) * nw
            zh = z_ref[b, rs, h * GDN_D:(h + 1) * GDN_D]
            o_ref[b, rs, h * GDN_D:(h + 1) * GDN_D] = on * (zh * _sigmoid(zh))
    for p, s in enumerate(states):
        s_ref[p] = s


def _gdn(qkv, z, ab, conv_w8, alog_row, dtb_row, norm_w, rows):
    bsz, seq, _ = qkv.shape
    blk = lambda i: (0, i, 0)
    const = lambda i: (0, 0)
    return pl.pallas_call(
        _gdn_kernel,
        grid=(seq // rows,),
        in_specs=[pl.BlockSpec((bsz, rows, C_QKV), blk),
                  pl.BlockSpec((bsz, rows, C_Z), blk),
                  pl.BlockSpec((bsz, rows, LANES), blk),
                  pl.BlockSpec((8, C_QKV), const),
                  pl.BlockSpec((1, LANES), const),
                  pl.BlockSpec((1, LANES), const),
                  pl.BlockSpec((1, GDN_D), const)],
        out_specs=pl.BlockSpec((bsz, rows, C_Z), blk),
        out_shape=jax.ShapeDtypeStruct((bsz, seq, C_Z), F32),
        scratch_shapes=[pltpu.VMEM((bsz, 8 + rows, C_QKV), F32),
                        pltpu.VMEM((bsz * GDN_HEADS, GDN_D, GDN_D), F32)],
        compiler_params=pltpu.CompilerParams(dimension_semantics=("arbitrary",),
                                             vmem_limit_bytes=VMEM_LIMIT),
        name="gdn",
    )(qkv, z, ab, conv_w8, alog_row, dtb_row, norm_w)


D_QK_PAD = LANES
C_HEADS = MLA_HEADS * D_QK_PAD
C_V = MLA_HEADS * MLA_V
V_AUG = MLA_V + 16
C_VAUG = MLA_HEADS * V_AUG
Q_SCALE = (MLA_NOPE + MLA_ROPE) ** -0.5 * math.log2(math.e)


def _mla_proj_kernel(cq_ref, ckv_ref, kr_ref, posr_ref, qnw_ref, kvnw_ref, wqt_ref, wqrt_ref,
                     wk_ref, wvt_ref, freqc_ref, qt_ref, k_ref, vt_ref):
    cq = cq_ref[...]
    cqn = (cq * lax.rsqrt(jnp.mean(cq * cq, axis=-1, keepdims=True) + EPS) * qnw_ref[...]).astype(BF16)
    ckv = ckv_ref[...]
    ckvn = (ckv * lax.rsqrt(jnp.mean(ckv * ckv, axis=-1, keepdims=True) + EPS)
            * kvnw_ref[...]).astype(BF16)
    tm = posr_ref.shape[-1]
    ang = freqc_ref[...] * posr_ref[...]
    c16 = jnp.cos(ang)
    s16 = jnp.sin(ang)
    pad = LANES - MLA_NOPE - MLA_ROPE
    cs_t = jnp.concatenate([jnp.ones((MLA_NOPE, tm), F32), c16, c16, jnp.ones((pad, tm), F32)], axis=0)
    sn_t = jnp.concatenate([jnp.zeros((MLA_NOPE, tm), F32), s16, s16, jnp.zeros((pad, tm), F32)], axis=0)
    cs = cs_t.T
    sn = sn_t.T
    qa_t = _dot_nt(wqt_ref[...], cqn)
    qb_t = _dot_nt(wqrt_ref[...], cqn)
    kn = _dot(ckvn, wk_ref[...])
    kr = kr_ref[...]
    kpe = kr[:, :LANES] * cs + kr[:, LANES:] * sn
    for h in range(MLA_HEADS):
        sl = slice(h * LANES, (h + 1) * LANES)
        qt_ref[sl, :] = ((qa_t[sl, :] * cs_t + qb_t[sl, :] * sn_t) * Q_SCALE).astype(BF16)
        k_ref[:, sl] = (kn[:, sl] + kpe).astype(BF16)
    v_t = _dot_nt(wvt_ref[...], ckvn).astype(BF16)
    ones = jnp.ones((V_AUG - MLA_V, v_t.shape[1]), BF16)
    for h in range(MLA_HEADS):
        vt_ref[0, h * V_AUG:h * V_AUG + MLA_V, :] = v_t[h * MLA_V:(h + 1) * MLA_V, :]
        vt_ref[0, h * V_AUG + MLA_V:(h + 1) * V_AUG, :] = ones


def _mla_proj(cq, ckv, kr2, posr, qnw, kvnw, wqt, wqrt, wk, wvt, freqc, tm):
    t = cq.shape[0]
    row = lambda i: (i, 0)
    col = lambda i: (0, i)
    const = lambda i: (0, 0)
    return pl.pallas_call(
        _mla_proj_kernel,
        grid=(t // tm,),
        in_specs=[pl.BlockSpec((tm, MLA_Q_LORA), row),
                  pl.BlockSpec((tm, MLA_KV_LORA), row),
                  pl.BlockSpec((tm, 2 * LANES), row),
                  pl.BlockSpec((1, tm), col),
                  pl.BlockSpec((1, MLA_Q_LORA), const),
                  pl.BlockSpec((1, MLA_KV_LORA), const),
                  pl.BlockSpec((C_HEADS, MLA_Q_LORA), const),
                  pl.BlockSpec((C_HEADS, MLA_Q_LORA), const),
                  pl.BlockSpec((MLA_KV_LORA, C_HEADS), const),
                  pl.BlockSpec((C_V, MLA_KV_LORA), const),
                  pl.BlockSpec((MLA_ROPE // 2, 1), const)],
        out_specs=[pl.BlockSpec((C_HEADS, tm), col),
                   pl.BlockSpec((tm, C_HEADS), row),
                   pl.BlockSpec((1, C_VAUG, tm), lambda i: (i, 0, 0))],
        out_shape=[jax.ShapeDtypeStruct((C_HEADS, t), BF16),
                   jax.ShapeDtypeStruct((t, C_HEADS), BF16),
                   jax.ShapeDtypeStruct((t // tm, C_VAUG, tm), BF16)],
        compiler_params=pltpu.CompilerParams(dimension_semantics=("parallel",),
                                             vmem_limit_bytes=VMEM_LIMIT),
        name="mla_proj",
    )(cq, ckv, kr2, posr, qnw, kvnw, wqt, wqrt, wk, wvt, freqc)


def _attn_kernel(qt_ref, k_ref, vt_ref, nw_ref, ot_ref, m_ref, acc_ref, *, tq):
    i = pl.program_id(1)
    m_ref[...] = jnp.full(m_ref.shape, NEG_BIG, F32)
    acc_ref[...] = jnp.zeros(acc_ref.shape, F32)
    key_chunk = lax.broadcasted_iota(jnp.int32, (tq, tq), 0) // CHUNK
    qry_chunk = lax.broadcasted_iota(jnp.int32, (tq, tq), 1) // CHUNK
    diag_mask = key_chunk <= qry_chunk

    def tile(j, masked):
        rows = pl.ds(pl.multiple_of(j * tq, tq), tq)

        scores = []
        for h in range(MLA_HEADS):
            hs = slice(h * LANES, (h + 1) * LANES)
            s_t = _dot(k_ref[rows, hs], qt_ref[hs, :])
            scores.append(jnp.where(diag_mask, s_t, NEG_BIG) if masked else s_t)
        probs, alphas = [], []
        for h in range(MLA_HEADS):
            m_old = m_ref[h]
            m_new = jnp.maximum(m_old, jnp.max(scores[h], axis=0, keepdims=True))
            m_ref[h] = m_new
            probs.append(jnp.exp2(scores[h] - m_new[0:1, :]).astype(BF16))
            alphas.append(jnp.exp2(m_old[0:1, :] - m_new[0:1, :]))
        for h in range(MLA_HEADS):
            v_aug = vt_ref[j, h * V_AUG:(h + 1) * V_AUG, :]
            acc_ref[h] = alphas[h] * acc_ref[h] + _dot(v_aug, probs[h])

    def full_tile(j, carry):
        tile(j, False)
        return carry

    lax.fori_loop(0, i, full_tile, 0)
    tile(i, True)

    outs = []
    for h in range(MLA_HEADS):
        acc = acc_ref[h]
        outs.append(acc[:MLA_V, :] / acc[MLA_V:MLA_V + 1, :])
    ssq = outs[0] * outs[0]
    for o in outs[1:]:
        ssq = ssq + o * o
    inv = lax.rsqrt(jnp.sum(ssq, axis=0, keepdims=True) / C_V + EPS)
    for h in range(MLA_HEADS):
        vs = slice(h * MLA_V, (h + 1) * MLA_V)
        ot_ref[vs, :] = outs[h] * inv * nw_ref[vs, :]


def _attention(qt, k, vt, norm_w_col, bsz, seq, tq):
    nq = seq // tq
    return pl.pallas_call(
        functools.partial(_attn_kernel, tq=tq),
        grid=(bsz, nq),
        in_specs=[pl.BlockSpec((C_HEADS, tq), lambda b, i: (0, b * nq + i)),
                  pl.BlockSpec((seq, C_HEADS), lambda b, i: (b, 0)),
                  pl.BlockSpec((nq, C_VAUG, tq), lambda b, i: (b, 0, 0)),
                  pl.BlockSpec((C_V, 1), lambda b, i: (0, 0))],
        out_specs=pl.BlockSpec((C_V, tq), lambda b, i: (0, b * nq + i)),
        out_shape=jax.ShapeDtypeStruct((C_V, bsz * seq), F32),
        scratch_shapes=[pltpu.VMEM((MLA_HEADS, 8, tq), F32),
                        pltpu.VMEM((MLA_HEADS, V_AUG, tq), F32)],
        compiler_params=pltpu.CompilerParams(dimension_semantics=("parallel", "parallel"),
                                             vmem_limit_bytes=VMEM_LIMIT),
        name="attention",
    )(qt, k, vt, norm_w_col)


def _router_kernel(x_ref, oa_ref, obt_ref, wo_ref, nw_ref, rw_ref, rb_ref,
                   x1_ref, h_ref, eidx_ref, gate_ref, rank_ref, cnt_ref, run_ref, *, tm):
    @pl.when(pl.program_id(0) == 0)
    def _():
        run_ref[...] = jnp.zeros_like(run_ref)

    half = oa_ref.shape[-1]
    y = (_dot(oa_ref[...].astype(BF16), wo_ref[0:half, :])
         + _dot_tn(obt_ref[...].astype(BF16), wo_ref[half:, :]))
    x1 = x_ref[...] + y
    x1_ref[...] = x1
    hn = x1 * lax.rsqrt(jnp.mean(x1 * x1, axis=-1, keepdims=True) + EPS) * nw_ref[...]
    d_half = hn.shape[-1] // 2
    _store_packed_rows(h_ref, _pack_bf16_pair(hn[:, :d_half], hn[:, d_half:]))
    lg = _dot_split_nt(_split_bf16(rw_ref[...]), _split_bf16(hn)) + rb_ref[...]
    expert = lax.broadcasted_iota(jnp.int32, lg.shape, 0)
    sels, tops, idxs = [], [], []
    for _ in range(TOP_K):
        m = jnp.max(lg, axis=0, keepdims=True)
        idx = jnp.min(jnp.where(lg == m, expert, N_EXPERTS), axis=0, keepdims=True)
        sel = expert == idx
        lg = jnp.where(sel, -jnp.inf, lg)
        sels.append(sel)
        tops.append(m)
        idxs.append(idx)
    exps = [jnp.exp(tv - tops[0]) for tv in tops]
    den = exps[0] + exps[1] + exps[2] + exps[3]
    multi = sels[0].astype(F32)
    for sel in sels[1:]:
        multi = multi + sel.astype(F32)
    ri = lax.broadcasted_iota(jnp.int32, (tm, tm), 0)
    ci = lax.broadcasted_iota(jnp.int32, (tm, tm), 1)
    before = _dot(multi.astype(BF16), (ri < ci).astype(BF16)) + run_ref[...]
    ranks = [jnp.sum(jnp.where(sel, before, 0.0), axis=0, keepdims=True).astype(jnp.int32) for sel in sels]
    run = run_ref[...] + jnp.sum(multi, axis=1, keepdims=True)
    run_ref[...] = run
    cnt_ref[...] = jnp.broadcast_to(run, cnt_ref.shape)
    pad_i = jnp.zeros((8 - TOP_K, tm), jnp.int32)
    eidx_ref[...] = jnp.concatenate(idxs + [pad_i], axis=0)
    rank_ref[...] = jnp.concatenate(ranks + [pad_i], axis=0)
    gates_t = jnp.concatenate([e / den for e in exps] + [jnp.zeros((LANES - TOP_K, tm), F32)], axis=0)
    eye = (ri == ci).astype(BF16)
    g_hi, g_lo = _split_bf16(gates_t)
    gate_ref[...] = _dot_nt(eye, g_hi) + _dot_nt(eye, g_lo)


def _router(x2, oa, obt, w_out, norm_w, rw, rb, tm):
    t, d = x2.shape
    half = oa.shape[-1]
    s_words = d // 2 // LANES
    row = lambda i: (i, 0)
    col = lambda i: (0, i)
    const = lambda i: (0, 0)
    return pl.pallas_call(
        functools.partial(_router_kernel, tm=tm),
        grid=(t // tm,),
        in_specs=[pl.BlockSpec((tm, d), row),
                  pl.BlockSpec((tm, half), row),
                  pl.BlockSpec((half, tm), lambda i: (0, i)),
                  pl.BlockSpec((2 * half, d), const),
                  pl.BlockSpec((1, d), const),
                  pl.BlockSpec((N_EXPERTS, d), const),
                  pl.BlockSpec((N_EXPERTS, 1), const)],
        out_specs=[pl.BlockSpec((tm, d), row), pl.BlockSpec((tm * s_words, LANES), row),
                   pl.BlockSpec((8, tm), col), pl.BlockSpec((tm, LANES), row),
                   pl.BlockSpec((8, tm), col), pl.BlockSpec((N_EXPERTS, LANES), const)],
        out_shape=[jax.ShapeDtypeStruct((t, d), F32), jax.ShapeDtypeStruct((t * s_words, LANES), jnp.int32),
                   jax.ShapeDtypeStruct((8, t), jnp.int32), jax.ShapeDtypeStruct((t, LANES), F32),
                   jax.ShapeDtypeStruct((8, t), jnp.int32), jax.ShapeDtypeStruct((N_EXPERTS, LANES), F32)],
        scratch_shapes=[pltpu.VMEM((N_EXPERTS, 1), F32)],
        compiler_params=pltpu.CompilerParams(dimension_semantics=("arbitrary",),
                                             vmem_limit_bytes=VMEM_LIMIT),
        name="router",
    )(x2, oa, obt, w_out, norm_w, rw, rb)


SC_ROWS = 64
COMBINE_RANGES = 4


def _sc_mesh():
    return plsc.VectorSubcoreMesh(core_axis_name="c", subcore_axis_name="s")


def _dispatch(dest_flat, h3, n_pad):
    t, s, _ = h3.shape
    mesh = _sc_mesh()
    n_workers = mesh.num_cores * mesh.num_subcores
    per_worker = t // n_workers
    assert per_worker % SC_ROWS == 0

    @functools.partial(
        pl.kernel, mesh=mesh, out_type=jax.ShapeDtypeStruct((n_pad, s, LANES), h3.dtype),
        scratch_types=[pltpu.VMEM((TOP_K, SC_ROWS), jnp.int32), pltpu.VMEM((SC_ROWS, s, LANES), h3.dtype),
                       pltpu.SemaphoreType.DMA],
        name="sc_dispatch")
    def scatter(h_hbm, dest_hbm, out_hbm, idx_v, rows_v, sem):
        base = (lax.axis_index("s") * mesh.num_cores + lax.axis_index("c")) * per_worker

        @pl.loop(0, per_worker // SC_ROWS)
        def _(i):
            off = pl.multiple_of(base + i * SC_ROWS, SC_ROWS)
            pltpu.sync_copy(h_hbm.at[pl.ds(off, SC_ROWS)], rows_v)
            for kk in range(TOP_K):
                pltpu.sync_copy(dest_hbm.at[pl.ds(kk * t + off, SC_ROWS)], idx_v.at[kk])
            copies = [pltpu.async_copy(rows_v, out_hbm.at[idx_v.at[kk]], sem) for kk in range(TOP_K)]
            for cp in copies:
                cp.wait()

    return scatter(h3, dest_flat)


def _expert_kernel(cnt_ref, blk0_ref, nblk_ref, total_ref, x_hbm, w1_ref, b1g_ref, b1l_ref, w2_ref, b2_ref,
                   y_hbm, w1s_ref, w2s_ref, xbuf, ybuf, xsem, ysem, *, bm):
    e = pl.program_id(0)
    n_blocks = nblk_ref[e]
    first = blk0_ref[e]
    total = total_ref[0]
    rows_per_block = xbuf.shape[1]
    s_words = rows_per_block // bm
    n_tiles = w1s_ref.shape[-1] // MXU_COLS
    half = MXU_COLS // 2

    def x_copy(g, slot):
        return pltpu.make_async_copy(x_hbm.at[pl.ds(pl.multiple_of(g * rows_per_block, rows_per_block),
                                                    rows_per_block)], xbuf.at[slot], xsem.at[slot])

    def y_copy(g, slot):
        return pltpu.make_async_copy(ybuf.at[slot], y_hbm.at[pl.ds(pl.multiple_of(g * rows_per_block,
                                                                                   rows_per_block),
                                                                   rows_per_block)], ysem.at[slot])

    @pl.when(e == 0)
    def _():
        x_copy(0, 0).start()

    @pl.when(n_blocks > 0)
    def _():
        r = lax.broadcasted_iota(jnp.int32, (MXU_COLS, MXU_COLS), 0)
        c = lax.broadcasted_iota(jnp.int32, (MXU_COLS, MXU_COLS), 1)
        perm = (r == jnp.where(c < half, 2 * c, 2 * (c - half) + 1)).astype(BF16)
        for tix in range(n_tiles):
            cols = slice(tix * MXU_COLS, (tix + 1) * MXU_COLS)
            w1s_ref[:, cols] = _dot(w1_ref[0, :, cols].astype(BF16), perm).astype(BF16)
        w2s_ref[...] = w2_ref[0].astype(BF16)

        def block(j, carry):
            g = first + j
            slot = lax.rem(g, 2)
            x_copy(g, slot).wait()

            @pl.when(g + 1 < total)
            def _():
                x_copy(g + 1, 1 - slot).start()

            @pl.when(g >= 2)
            def _():
                y_copy(g - 2, slot).wait()

            n_rows = cnt_ref[e] - j * bm

            def ffn(m):
                words = _load_packed_rows(xbuf, (slot,), m, s_words)
                row = lax.broadcasted_iota(jnp.int32, words.shape, 0)
                x_lo, x_hi = _unpack_bf16_pair(jnp.where(row < n_rows, words, 0))
                x = jnp.concatenate([x_lo.astype(BF16), x_hi.astype(BF16)], axis=1)
                acts = []
                for tix in range(n_tiles):
                    hp = _dot(x, w1s_ref[:, tix * MXU_COLS:(tix + 1) * MXU_COLS])
                    feat = slice(tix * half, (tix + 1) * half)
                    gl = jnp.minimum(hp[:, :half] + b1g_ref[0][:, feat], SWIGLU_LIMIT)
                    lin = jnp.clip(hp[:, half:] + b1l_ref[0][:, feat], -SWIGLU_LIMIT, SWIGLU_LIMIT)
                    acts.append((gl * _sigmoid(SWIGLU_ALPHA * gl) * (lin + 1.0)).astype(BF16))
                y = _dot(jnp.concatenate(acts, axis=1), w2s_ref[...]) + b2_ref[0]
                d_half = y.shape[-1] // 2
                _store_packed_rows(ybuf.at[slot], _pack_bf16_pair(y[:, :d_half], y[:, d_half:]))

            @pl.when(n_rows > bm // 2)
            def _():
                ffn(bm)

            @pl.when(n_rows <= bm // 2)
            def _():
                ffn(bm // 2)

            y_copy(g, slot).start()
            return carry

        lax.fori_loop(0, n_blocks, block, 0)

    @pl.when(e == pl.num_programs(0) - 1)
    def _():
        @pl.when(total >= 2)
        def _():
            y_copy(total - 2, lax.rem(total, 2)).wait()

        y_copy(total - 1, lax.rem(total - 1, 2)).wait()


def _experts(tables, xs, w1, b1g, b1l, w2, b2, bm, d):
    s_words = d // 2 // LANES
    dff2 = w1.shape[-1]
    dff = dff2 // 2
    emap = lambda e, *_: (e, 0, 0)
    grid_spec = pltpu.PrefetchScalarGridSpec(
        num_scalar_prefetch=len(tables),
        grid=(N_EXPERTS,),
        in_specs=[pl.BlockSpec(memory_space=pl.ANY),
                  pl.BlockSpec((1, d, dff2), emap),
                  pl.BlockSpec((1, 1, dff), emap),
                  pl.BlockSpec((1, 1, dff), emap),
                  pl.BlockSpec((1, dff, d), emap),
                  pl.BlockSpec((1, 1, d), emap)],
        out_specs=pl.BlockSpec(memory_space=pl.ANY),
        scratch_shapes=[pltpu.VMEM((d, dff2), BF16), pltpu.VMEM((dff, d), BF16),
                        pltpu.VMEM((2, bm * s_words, LANES), jnp.int32),
                        pltpu.VMEM((2, bm * s_words, LANES), jnp.int32),
                        pltpu.SemaphoreType.DMA((2,)), pltpu.SemaphoreType.DMA((2,))],
    )
    return pl.pallas_call(
        functools.partial(_expert_kernel, bm=bm),
        grid_spec=grid_spec,
        out_shape=jax.ShapeDtypeStruct(xs.shape, jnp.int32),
        compiler_params=pltpu.CompilerParams(dimension_semantics=("arbitrary",),
                                             vmem_limit_bytes=VMEM_LIMIT),
        name="experts",
    )(*tables, xs, w1, b1g, b1l, w2, b2)


def _gather_rows(dest_flat, y3):
    n_out = dest_flat.shape[0]
    _, s, _ = y3.shape
    mesh = _sc_mesh()
    n_workers = mesh.num_cores * mesh.num_subcores
    per_worker = n_out // n_workers
    n_chunks = per_worker // SC_ROWS
    assert per_worker % SC_ROWS == 0 and n_chunks % 2 == 0

    @functools.partial(
        pl.kernel, mesh=mesh, out_type=jax.ShapeDtypeStruct((n_out, s, LANES), y3.dtype),
        scratch_types=[pltpu.VMEM((2, SC_ROWS), jnp.int32), pltpu.VMEM((2, SC_ROWS, s, LANES), y3.dtype),
                       pltpu.SemaphoreType.DMA((2,)), pltpu.SemaphoreType.DMA((2,))],
        name="sc_gather")
    def gather(y_hbm, dest_hbm, out_hbm, idx_v, rows_v, gsem, wsem):
        base = (lax.axis_index("s") * mesh.num_cores + lax.axis_index("c")) * per_worker

        def rows_of(chunk):
            return pl.ds(pl.multiple_of(base + chunk * SC_ROWS, SC_ROWS), SC_ROWS)

        def gather_copy(slot):
            return pltpu.make_async_copy(y_hbm.at[idx_v.at[slot]], rows_v.at[slot], gsem.at[slot])

        def write_copy(chunk, slot):
            return pltpu.make_async_copy(rows_v.at[slot], out_hbm.at[rows_of(chunk)], wsem.at[slot])

        def start_gather(chunk, slot):
            pltpu.sync_copy(dest_hbm.at[rows_of(chunk)], idx_v.at[slot])
            gather_copy(slot).start()

        start_gather(0, 0)

        @pl.loop(0, n_chunks // 2)
        def _(pair):
            for slot in (0, 1):
                chunk = 2 * pair + slot
                gather_copy(slot).wait()
                write_copy(chunk, slot).start()

                @pl.when(chunk >= 1)
                def _():
                    write_copy(chunk - 1, 1 - slot).wait()

                @pl.when(chunk + 1 < n_chunks)
                def _():
                    start_gather(chunk + 1, 1 - slot)

        write_copy(n_chunks - 1, 1).wait()

    return gather(y3, dest_flat)


def _combine_kernel(x1_ref, gate_ref, y4_ref, nw_ref, *out_refs, tm, normalize):
    o_ref = out_refs[-1]
    x = x1_ref[...]
    gates = gate_ref[...]
    for kk in range(TOP_K):
        lo, hi = _unpack_bf16_pair(_load_packed_rows(y4_ref, (kk,), tm, y4_ref.shape[1] // tm))
        x = x + gates[:, kk:kk + 1] * jnp.concatenate([lo, hi], axis=1)
    if normalize:
        x = x * lax.rsqrt(jnp.mean(x * x, axis=-1, keepdims=True) + EPS) * nw_ref[...]
    o_ref[...] = x


def _combine(x1, gates, y4, norm_w, out_prev, first_tile, tm, normalize):
    t, d = x1.shape
    s_words = d // 2 // LANES
    n_tiles = y4.shape[1] // (tm * s_words)
    row = lambda i: (first_tile + i, 0)
    in_specs = [pl.BlockSpec((tm, d), row), pl.BlockSpec((tm, LANES), row),
                pl.BlockSpec((TOP_K, tm * s_words, LANES), lambda i: (0, i, 0)),
                pl.BlockSpec((1, d), lambda i: (0, 0))]
    args = [x1, gates, y4, norm_w]
    aliases = {}
    if out_prev is not None:
        in_specs.append(pl.BlockSpec(memory_space=pl.ANY))
        args.append(out_prev)
        aliases = {len(args) - 1: 0}
    return pl.pallas_call(
        functools.partial(_combine_kernel, tm=tm, normalize=normalize),
        grid=(n_tiles,),
        in_specs=in_specs,
        out_specs=pl.BlockSpec((tm, d), row),
        out_shape=jax.ShapeDtypeStruct((t, d), F32),
        input_output_aliases=aliases,
        compiler_params=pltpu.CompilerParams(dimension_semantics=("parallel",)),
        name="combine",
    )(*args)


def _tiles(t, seq):
    tm = min(2 * MXU_COLS, t)
    tr = min(2 * MXU_COLS, t)
    tq = min(2 * MXU_COLS, seq)
    gdn_rows = min(2 * CHUNK, seq)
    bm = 2 * MXU_COLS
    assert t % tm == 0 and t % tr == 0 and seq % tq == 0 and seq % gdn_rows == 0 and tq % CHUNK == 0
    return tm, tr, tq, gdn_rows, bm


def _rot_half(w):
    half = w.shape[-1] // 2
    return jnp.concatenate([-w[..., half:], w[..., :half]], axis=-1)


def _prep_inproj_weight(w_in):
    d = w_in.shape[0]
    hd = GDN_HEADS * GDN_D
    o = 0
    parts = {}
    for name, n in (("qkv", 3 * hd), ("z", hd), ("a", GDN_HEADS), ("b", GDN_HEADS),
                    ("cq", MLA_Q_LORA), ("ckv", MLA_KV_LORA), ("kr", MLA_ROPE)):
        parts[name] = w_in[:, o:o + n]
        o += n
    zeros = lambda n: jnp.zeros((d, n), w_in.dtype)
    kr_tile = jnp.concatenate([zeros(MLA_NOPE), parts["kr"], zeros(LANES - MLA_NOPE - MLA_ROPE)], axis=1)
    krr_tile = jnp.concatenate([zeros(MLA_NOPE), _rot_half(parts["kr"]),
                                zeros(LANES - MLA_NOPE - MLA_ROPE)], axis=1)
    ab_tile = jnp.concatenate([parts["a"], parts["b"], zeros(LANES - 2 * GDN_HEADS)], axis=1)
    return jnp.concatenate([parts["qkv"], parts["z"], parts["cq"], parts["ckv"], kr_tile, krr_tile,
                            ab_tile], axis=1).astype(BF16)


def _prep_mla_weights(w_uq, w_ukv):
    dq = MLA_NOPE + MLA_ROPE
    wq = w_uq.reshape(MLA_Q_LORA, MLA_HEADS, dq)
    wq_rot = jnp.concatenate([jnp.zeros_like(wq[..., :MLA_NOPE]), _rot_half(wq[..., MLA_NOPE:])], axis=-1)
    pad = lambda w: jnp.pad(w, ((0, 0), (0, 0), (0, LANES - w.shape[-1]))).reshape(w.shape[0], C_HEADS)
    wkv = w_ukv.reshape(MLA_KV_LORA, MLA_HEADS, MLA_NOPE + MLA_V)
    wv = wkv[..., MLA_NOPE:].reshape(MLA_KV_LORA, C_V)
    return (pad(wq).T.astype(BF16), pad(wq_rot).T.astype(BF16), pad(wkv[..., :MLA_NOPE]).astype(BF16),
            wv.T.astype(BF16))


def _rope_freq():
    half = MLA_ROPE // 2
    return (ROPE_THETA ** (-jnp.arange(half, dtype=F32) / half)).reshape(half, 1)


def _lane_row(v):
    return jnp.pad(v.astype(F32), (0, LANES - v.shape[0])).reshape(1, LANES)


def _routing_tables(cnt, eidx, rank, bm):
    e_ids = jnp.arange(N_EXPERTS, dtype=jnp.int32)
    counts = cnt[:, 0].astype(jnp.int32)
    padded = (counts + bm - 1) // bm * bm
    pend = jnp.sum(jnp.where(e_ids[None, :] <= e_ids[:, None], padded[None, :], 0), axis=1)
    pstart = pend - padded
    dest = jnp.sum(jnp.where(eidx[:TOP_K, :, None] == e_ids, pstart, 0), axis=-1) + rank[:TOP_K]
    tables = (counts, pstart // bm, padded // bm, pend[-1:] // bm)
    return dest.reshape(-1), tuple(t.astype(jnp.int32) for t in tables)


def kernel(x, positions, norm_mix_w, w_in, conv_w, a_log, dt_bias, gdn_norm_w, q_norm_w, w_uq, kv_norm_w,
           w_ukv, mla_out_norm_w, w_out, norm_ffn_w, router_w, router_b, w1, b1, w2, b2, norm_final_w):
    bsz, seq, d = x.shape
    t = bsz * seq
    depth = w_in.shape[0]
    tm, tr, tq, gdn_rows, bm = _tiles(t, seq)
    x2 = x.reshape(t, d)
    posf = positions.astype(F32)
    freq = _rope_freq()
    for l in range(depth):
        qkv, z, cq, ckv, kr2, ab = _inproj(x2, norm_mix_w[l].reshape(1, d), _prep_inproj_weight(w_in[l]), tm)
        conv_w8 = jnp.pad(conv_w[l], ((0, 8 - GDN_CONV), (0, 0)))
        o_a = _gdn(qkv.reshape(bsz, seq, C_QKV), z.reshape(bsz, seq, C_Z), ab.reshape(bsz, seq, LANES),
                   conv_w8, _lane_row(a_log[l]), _lane_row(dt_bias[l]), gdn_norm_w[l].reshape(1, GDN_D),
                   gdn_rows).reshape(t, C_Z)
        wqt, wqrt, wk, wvt = _prep_mla_weights(w_uq[l], w_ukv[l])
        qt, k, vt = _mla_proj(cq, ckv, kr2, posf.reshape(1, t),
                              q_norm_w[l].reshape(1, -1), kv_norm_w[l].reshape(1, -1),
                              wqt, wqrt, wk, wvt, freq, tq)
        o_bt = _attention(qt, k, vt, mla_out_norm_w[l].reshape(-1, 1), bsz, seq, tq)
        x1, hn, eidx, gates, rank, cnt = _router(x2, o_a, o_bt, w_out[l].astype(BF16),
                                                 norm_ffn_w[l].reshape(1, d), router_w[l].T,
                                                 router_b[l].reshape(N_EXPERTS, 1), tr)
        n_pad = t * TOP_K + N_EXPERTS * bm
        dest_flat, tables = _routing_tables(cnt, eidx, rank, bm)
        s_words = d // 2 // LANES
        buf = _dispatch(dest_flat, hn.reshape(t, s_words, LANES), n_pad)
        yb = _experts(tables, buf.reshape(n_pad * s_words, LANES), w1[l],
                      b1[l][:, None, 0::2], b1[l][:, None, 1::2], w2[l], b2[l][:, None, :], bm, d)
        yb3 = yb.reshape(n_pad, s_words, LANES)
        dest_kt = dest_flat.reshape(TOP_K, t)
        tc = t // COMBINE_RANGES
        out = None
        for r in range(COMBINE_RANGES):
            y4 = _gather_rows(dest_kt[:, r * tc:(r + 1) * tc].reshape(TOP_K * tc), yb3)
            out = _combine(x1, gates, y4.reshape(TOP_K, tc * s_words, LANES), norm_final_w.reshape(1, d),
                           out, r * tc // tr, tr, normalize=(l == depth - 1))
        x2 = out
    return x2.reshape(bsz, seq, d)
```

```python
import functools
import math

import jax
import jax.numpy as jnp
from jax import lax
from jax.experimental import pallas as pl
from jax.experimental.pallas import tpu as pltpu
from jax.experimental.pallas import tpu_sc as plsc

F32 = jnp.float32
BF16 = jnp.bfloat16
HIGHEST = lax.Precision.HIGHEST

LANES = 128
MXU_COLS = 256
EPS = 1e-6
CHUNK = 64

GDN_HEADS = 4
GDN_D = 128
GDN_CONV = 4
MLA_HEADS = 8
MLA_NOPE = 64
MLA_ROPE = 32
MLA_V = 64
MLA_Q_LORA = 384
MLA_KV_LORA = 256
ROPE_THETA = 10000.0
N_EXPERTS = 32
TOP_K = 4
SWIGLU_LIMIT = 7.0
SWIGLU_ALPHA = 1.702

NEG_BIG = -1e30
VMEM_LIMIT = 52 * 1024 * 1024


def _sigmoid(x):
    return 1.0 / (1.0 + jnp.exp(-x))


def _softplus(x):
    return jnp.maximum(x, 0.0) + jnp.log(1.0 + jnp.exp(-jnp.abs(x)))


def _dot(a, b, precision=None):
    return jnp.dot(a, b, preferred_element_type=F32, precision=precision)


def _dot_nt(a, b, precision=None):
    return lax.dot_general(a, b, (((1,), (1,)), ((), ())), preferred_element_type=F32,
                           precision=precision)


def _dot_tn(a, b):
    return lax.dot_general(a, b, (((0,), (0,)), ((), ())), preferred_element_type=F32)


def _split_bf16(a):
    hi = a.astype(BF16)
    return hi, (a - hi.astype(F32)).astype(BF16)


def _pack_bf16_pair(lo, hi):
    lo_bits = pltpu.bitcast(lo.astype(BF16).astype(F32), jnp.int32)
    hi_bits = pltpu.bitcast(hi.astype(BF16).astype(F32), jnp.int32)
    return jnp.bitwise_or(hi_bits, lax.shift_right_logical(lo_bits, 16))


def _unpack_bf16_pair(w):
    lo = pltpu.bitcast(lax.shift_left(w, 16), F32)
    hi = pltpu.bitcast(jnp.bitwise_and(w, -65536), F32)
    return lo, hi


def _load_packed_rows(ref, lead, m, s):
    return jnp.concatenate([ref[(*lead, pl.ds(j, m, stride=s), slice(None))] for j in range(s)], axis=1)


def _store_packed_rows(ref, words):
    m = words.shape[0]
    s = words.shape[1] // LANES
    for j in range(s):
        ref[pl.ds(j, m, stride=s), :] = words[:, j * LANES:(j + 1) * LANES]


def _dot_split_nt(a, b):
    return _dot_nt(a[0], b[0]) + _dot_nt(a[0], b[1]) + _dot_nt(a[1], b[0])


def _dot_split(a, b):
    return _dot(a[0], b[0]) + _dot(a[0], b[1]) + _dot(a[1], b[0])


C_QKV = 3 * GDN_HEADS * GDN_D
C_Z = GDN_HEADS * GDN_D
C_IN = C_QKV + C_Z + MLA_Q_LORA + MLA_KV_LORA + 3 * LANES


def _inproj_kernel(x_ref, nw_ref, w_ref, qkv_ref, z_ref, cq_ref, ckv_ref, kr_ref, ab_ref, wb_ref):
    @pl.when(pl.program_id(0) == 0)
    def _():
        d = w_ref.shape[0]
        hd4 = C_QKV + C_Z
        wb_ref[:, 0:hd4] = w_ref[:, 0:hd4].astype(BF16)
        tail = w_ref[:, hd4:]
        o_cq = 2 * GDN_HEADS
        o_ckv = o_cq + MLA_Q_LORA
        o_kr = o_ckv + MLA_KV_LORA
        kr = tail[:, o_kr:o_kr + MLA_ROPE]
        half = MLA_ROPE // 2
        zeros = lambda n: jnp.zeros((d, n), F32)
        pad = LANES - MLA_NOPE - MLA_ROPE
        groups = [tail[:, o_cq:o_ckv], tail[:, o_ckv:o_kr],
                  jnp.concatenate([zeros(MLA_NOPE), kr, zeros(pad)], axis=1),
                  jnp.concatenate([zeros(MLA_NOPE), -kr[:, half:], kr[:, :half], zeros(pad)], axis=1),
                  jnp.concatenate([tail[:, 0:o_cq], zeros(LANES - o_cq)], axis=1)]
        o = hd4
        for grp in groups:
            wb_ref[:, o:o + grp.shape[1]] = grp.astype(BF16)
            o += grp.shape[1]

    x = x_ref[...]
    var = jnp.mean(x * x, axis=-1, keepdims=True)
    h = (x * lax.rsqrt(var + EPS) * nw_ref[...]).astype(BF16)
    p = _dot(h, wb_ref[...])
    o = 0
    for ref in (qkv_ref, z_ref, cq_ref, ckv_ref, kr_ref, ab_ref):
        n = ref.shape[-1]
        ref[...] = p[:, o:o + n]
        o += n


def _inproj(x2, norm_w, w_in, tm):
    t, d = x2.shape
    widths = (C_QKV, C_Z, MLA_Q_LORA, MLA_KV_LORA, 2 * LANES, LANES)
    return pl.pallas_call(
        _inproj_kernel,
        grid=(t // tm,),
        in_specs=[pl.BlockSpec((tm, d), lambda i: (i, 0)),
                  pl.BlockSpec((1, d), lambda i: (0, 0)),
                  pl.BlockSpec(w_in.shape, lambda i: (0, 0), pipeline_mode=pl.Buffered(1))],
        out_specs=[pl.BlockSpec((tm, n), lambda i: (i, 0)) for n in widths],
        out_shape=[jax.ShapeDtypeStruct((t, n), F32) for n in widths],
        scratch_shapes=[pltpu.VMEM((d, C_IN), BF16)],
        compiler_params=pltpu.CompilerParams(dimension_semantics=("arbitrary",),
                                             vmem_limit_bytes=VMEM_LIMIT),
        name="inproj",
    )(x2, norm_w, w_in)


def _gdn_kernel(qkv_ref, z_ref, ab_ref, cw_ref, alog_ref, dtb_ref, nw_ref, o_ref, ext_ref, s_ref):
    c = CHUNK
    bsz, rows, _ = qkv_ref.shape
    chunks_per_batch = rows // c

    @pl.when(pl.program_id(0) == 0)
    def _():
        ext_ref[:, 0:8, :] = jnp.zeros((bsz, 8, C_QKV), F32)
        s_ref[...] = jnp.zeros_like(s_ref)

    cw = cw_ref[...]
    nw = nw_ref[...]
    rr = lax.broadcasted_iota(jnp.int32, (rows, rows), 0)
    rc = lax.broadcasted_iota(jnp.int32, (rows, rows), 1)
    in_chunk_prefix = jnp.where(rr // c == rc // c, jnp.where(rr >= rc, 1.0, 0.0), 0.0)
    e8 = (lax.broadcasted_iota(jnp.int32, (8, LANES), 0)
          == lax.broadcasted_iota(jnp.int32, (8, LANES), 1)).astype(F32)
    ri = lax.broadcasted_iota(jnp.int32, (c, c), 0)
    ci = lax.broadcasted_iota(jnp.int32, (c, c), 1)
    causal = ri >= ci
    strict = ri > ci
    eye = (ri == ci).astype(F32)
    chains = [(j, h) for j in range(chunks_per_batch) for h in range(GDN_HEADS)]

    def prep_pieces(b):
        g = {"xc": [None] * (C_QKV // LANES)}

        def conv(tiles):
            def run():
                cols = slice(tiles[0] * LANES, (tiles[-1] + 1) * LANES)
                x = qkv_ref[b, :, cols]
                ext_ref[b, 8:8 + rows, cols] = x
                ext = ext_ref[b, :, cols]
                xb = cw[3:4, cols] * x
                for j in range(1, GDN_CONV):
                    xb = xb + cw[3 - j:4 - j, cols] * pltpu.roll(ext, j, axis=0)[8:8 + rows, :]
                ext_ref[b, 0:8, cols] = x[rows - 8:rows, :]
                xb = xb * _sigmoid(xb)
                for i, tile in enumerate(tiles):
                    g["xc"][tile] = xb[:, i * LANES:(i + 1) * LANES]
            return run

        def gates():
            ab = ab_ref[b]
            g_all = -jnp.exp(alog_ref[...]) * _softplus(ab + dtb_ref[...])
            g["beta"] = _sigmoid(ab)
            g["g_cum"] = _dot(in_chunk_prefix, g_all, HIGHEST)
            g["g_row"] = _dot_nt(e8, g["g_cum"], HIGHEST)
            for name in ("q", "k", "v", "gc", "bt", "dec"):
                g[name] = []

        def chunk(j):
            def run():
                rs = slice(j * c, (j + 1) * c)
                xc = g["xc"]
                for h in range(GDN_HEADS):
                    q = xc[h][rs]
                    k = xc[GDN_HEADS + h][rs]
                    g["q"].append(q * lax.rsqrt(jnp.sum(q * q, axis=-1, keepdims=True) + EPS) * (GDN_D ** -0.5))
                    g["k"].append(k * lax.rsqrt(jnp.sum(k * k, axis=-1, keepdims=True) + EPS))
                    g["v"].append(xc[2 * GDN_HEADS + h][rs])
                    gc = g["g_cum"][rs, h:h + 1]
                    gr = g["g_row"][h:h + 1, rs]
                    g["gc"].append(gc)
                    g["bt"].append(g["beta"][rs, GDN_HEADS + h:GDN_HEADS + h + 1])
                    g["dec"].append(jnp.exp(jnp.where(causal, gc - gr, -jnp.inf)))
            return run

        def a_mats():
            g["kb"] = [k * bt for k, bt in zip(g["k"], g["bt"])]
            g["k16"] = [k.astype(BF16) for k in g["k"]]
            g["a"] = [jnp.where(strict, _dot_nt(kb.astype(BF16), k16) * dec, 0.0)
                      for kb, k16, dec in zip(g["kb"], g["k16"], g["dec"])]

        n_tiles = C_QKV // LANES
        conv_pieces = [conv(list(range(i, i + 2))) for i in range(0, n_tiles, 2)]
        return g, conv_pieces + [gates] + [chunk(j) for j in range(chunks_per_batch)] + [a_mats]

    def inverse_rounds(g):
        def start():
            g["t"] = [eye - a for a in g["a"]]
            g["pw"] = [_split_bf16(a) for a in g["a"]]

        def square():
            g["pw"] = [_split_bf16(_dot_split(p, p)) for p in g["pw"]]

        def update():
            g["t"] = [t + _dot_split(_split_bf16(t), p) for t, p in zip(g["t"], g["pw"])]

        return [start] + [square, update] * 5

    groups = []
    pending = []
    for b in range(bsz):
        g, pieces = prep_pieces(b)
        groups.append(g)
        n_a, n_b = len(pending), len(pieces)
        ia = ib = 0
        while ia < n_a or ib < n_b:
            if ib >= n_b or (ia < n_a and ia * n_b <= ib * n_a):
                pending[ia]()
                ia += 1
            else:
                pieces[ib]()
                ib += 1
        pending = inverse_rounds(g)
    for fn in pending:
        fn()

    uws, qk16s, qd16s, kd16s, g_tots = [], [], [], [], []
    for g in groups:
        e_gs = [jnp.exp(gc) for gc in g["gc"]]
        uws.append([_dot(t.astype(BF16), jnp.concatenate([v * bt, kb * eg], axis=1).astype(BF16))
                    for t, v, bt, kb, eg in zip(g["t"], g["v"], g["bt"], g["kb"], e_gs)])
        qk16s.append([(_dot_nt(q.astype(BF16), k16) * dec).astype(BF16)
                      for q, k16, dec in zip(g["q"], g["k16"], g["dec"])])
        qd16s.append([(q * eg).astype(BF16) for q, eg in zip(g["q"], e_gs)])
        kd16s.append([(k * jnp.exp(gc[c - 1:c, :] - gc)).astype(BF16) for k, gc in zip(g["k"], g["gc"])])
        g_tots.append([jnp.exp(gc[c - 1:c, :]) for gc in g["gc"]])

    pairs = [(b, h) for b in range(bsz) for h in range(GDN_HEADS)]
    states = [s_ref[b * GDN_HEADS + h] for b, h in pairs]
    for j in range(chunks_per_batch):
        rs = slice(j * c, (j + 1) * c)
        pick = lambda per_group: [per_group[b][chains.index((j, h))] for b, h in pairs]
        uw_j, qk_j, qd_j, kd_j, gt_j = (pick(x) for x in (uws, qk16s, qd16s, kd16s, g_tots))
        s16s = [s.astype(BF16) for s in states]
        v16s = [(uw[:, :GDN_D] - _dot(uw[:, GDN_D:].astype(BF16), s16)).astype(BF16)
                for uw, s16 in zip(uw_j, s16s)]
        states = [s * gt + _dot_tn(kd, v16) for s, gt, kd, v16 in zip(states, gt_j, kd_j, v16s)]
        os_ = [_dot(qd, s16) + _dot(qk, v16) for qd, qk, s16, v16 in zip(qd_j, qk_j, s16s, v16s)]
        for (b, h), o in zip(pairs, os_):
            on = o * lax.rsqrt(jnp.mean(o * o, axis=-1, keepdims=True) + EPS) * nw
            zh = z_ref[b, rs, h * GDN_D:(h + 1) * GDN_D]
            o_ref[b, rs, h * GDN_D:(h + 1) * GDN_D] = on * (zh * _sigmoid(zh))
    for p, s in enumerate(states):
        s_ref[p] = s


def _gdn(qkv, z, ab, conv_w8, alog_row, dtb_row, norm_w, rows):
    bsz, seq, _ = qkv.shape
    blk = lambda i: (0, i, 0)
    const = lambda i: (0, 0)
    return pl.pallas_call(
        _gdn_kernel,
        grid=(seq // rows,),
        in_specs=[pl.BlockSpec((bsz, rows, C_QKV), blk),
                  pl.BlockSpec((bsz, rows, C_Z), blk),
                  pl.BlockSpec((bsz, rows, LANES), blk),
                  pl.BlockSpec((8, C_QKV), const),
                  pl.BlockSpec((1, LANES), const),
                  pl.BlockSpec((1, LANES), const),
                  pl.BlockSpec((1, GDN_D), const)],
        out_specs=pl.BlockSpec((bsz, rows, C_Z), blk),
        out_shape=jax.ShapeDtypeStruct((bsz, seq, C_Z), F32),
        scratch_shapes=[pltpu.VMEM((bsz, 8 + rows, C_QKV), F32),
                        pltpu.VMEM((bsz * GDN_HEADS, GDN_D, GDN_D), F32)],
        compiler_params=pltpu.CompilerParams(dimension_semantics=("arbitrary",),
                                             vmem_limit_bytes=VMEM_LIMIT),
        name="gdn",
    )(qkv, z, ab, conv_w8, alog_row, dtb_row, norm_w)


D_QK_PAD = LANES
C_HEADS = MLA_HEADS * D_QK_PAD
C_V = MLA_HEADS * MLA_V
V_AUG = MLA_V + 16
C_VAUG = MLA_HEADS * V_AUG
Q_SCALE = (MLA_NOPE + MLA_ROPE) ** -0.5 * math.log2(math.e)


def _mla_proj_kernel(cq_ref, ckv_ref, kr_ref, posr_ref, qnw_ref, kvnw_ref, wqt_ref, wqrt_ref,
                     wk_ref, wvt_ref, freqc_ref, qt_ref, k_ref, vt_ref):
    cq = cq_ref[...]
    cqn = (cq * lax.rsqrt(jnp.mean(cq * cq, axis=-1, keepdims=True) + EPS) * qnw_ref[...]).astype(BF16)
    ckv = ckv_ref[...]
    ckvn = (ckv * lax.rsqrt(jnp.mean(ckv * ckv, axis=-1, keepdims=True) + EPS)
            * kvnw_ref[...]).astype(BF16)
    tm = posr_ref.shape[-1]
    ang = freqc_ref[...] * posr_ref[...]
    c16 = jnp.cos(ang)
    s16 = jnp.sin(ang)
    pad = LANES - MLA_NOPE - MLA_ROPE
    cs_t = jnp.concatenate([jnp.ones((MLA_NOPE, tm), F32), c16, c16, jnp.ones((pad, tm), F32)], axis=0)
    sn_t = jnp.concatenate([jnp.zeros((MLA_NOPE, tm), F32), s16, s16, jnp.zeros((pad, tm), F32)], axis=0)
    cs = cs_t.T
    sn = sn_t.T
    qa_t = _dot_nt(wqt_ref[...], cqn)
    qb_t = _dot_nt(wqrt_ref[...], cqn)
    kn = _dot(ckvn, wk_ref[...])
    kr = kr_ref[...]
    kpe = kr[:, :LANES] * cs + kr[:, LANES:] * sn
    for h in range(MLA_HEADS):
        sl = slice(h * LANES, (h + 1) * LANES)
        qt_ref[sl, :] = ((qa_t[sl, :] * cs_t + qb_t[sl, :] * sn_t) * Q_SCALE).astype(BF16)
        k_ref[:, sl] = (kn[:, sl] + kpe).astype(BF16)
    v_t = _dot_nt(wvt_ref[...], ckvn).astype(BF16)
    ones = jnp.ones((V_AUG - MLA_V, v_t.shape[1]), BF16)
    for h in range(MLA_HEADS):
        vt_ref[0, h * V_AUG:h * V_AUG + MLA_V, :] = v_t[h * MLA_V:(h + 1) * MLA_V, :]
        vt_ref[0, h * V_AUG + MLA_V:(h + 1) * V_AUG, :] = ones


def _mla_proj(cq, ckv, kr2, posr, qnw, kvnw, wqt, wqrt, wk, wvt, freqc, tm):
    t = cq.shape[0]
    row = lambda i: (i, 0)
    col = lambda i: (0, i)
    const = lambda i: (0, 0)
    return pl.pallas_call(
        _mla_proj_kernel,
        grid=(t // tm,),
        in_specs=[pl.BlockSpec((tm, MLA_Q_LORA), row),
                  pl.BlockSpec((tm, MLA_KV_LORA), row),
                  pl.BlockSpec((tm, 2 * LANES), row),
                  pl.BlockSpec((1, tm), col),
                  pl.BlockSpec((1, MLA_Q_LORA), const),
                  pl.BlockSpec((1, MLA_KV_LORA), const),
                  pl.BlockSpec((C_HEADS, MLA_Q_LORA), const),
                  pl.BlockSpec((C_HEADS, MLA_Q_LORA), const),
                  pl.BlockSpec((MLA_KV_LORA, C_HEADS), const),
                  pl.BlockSpec((C_V, MLA_KV_LORA), const),
                  pl.BlockSpec((MLA_ROPE // 2, 1), const)],
        out_specs=[pl.BlockSpec((C_HEADS, tm), col),
                   pl.BlockSpec((tm, C_HEADS), row),
                   pl.BlockSpec((1, C_VAUG, tm), lambda i: (i, 0, 0))],
        out_shape=[jax.ShapeDtypeStruct((C_HEADS, t), BF16),
                   jax.ShapeDtypeStruct((t, C_HEADS), BF16),
                   jax.ShapeDtypeStruct((t // tm, C_VAUG, tm), BF16)],
        compiler_params=pltpu.CompilerParams(dimension_semantics=("parallel",),
                                             vmem_limit_bytes=VMEM_LIMIT),
        name="mla_proj",
    )(cq, ckv, kr2, posr, qnw, kvnw, wqt, wqrt, wk, wvt, freqc)


def _attn_kernel(qt_ref, k_ref, vt_ref, nw_ref, ot_ref, m_ref, acc_ref, *, tq):
    i = pl.program_id(1)
    m_ref[...] = jnp.full(m_ref.shape, NEG_BIG, F32)
    acc_ref[...] = jnp.zeros(acc_ref.shape, F32)
    key_chunk = lax.broadcasted_iota(jnp.int32, (tq, tq), 0) // CHUNK
    qry_chunk = lax.broadcasted_iota(jnp.int32, (tq, tq), 1) // CHUNK
    diag_mask = key_chunk <= qry_chunk

    def tile(j, masked):
        rows = pl.ds(pl.multiple_of(j * tq, tq), tq)

        scores = []
        for h in range(MLA_HEADS):
            hs = slice(h * LANES, (h + 1) * LANES)
            s_t = _dot(k_ref[rows, hs], qt_ref[hs, :])
            scores.append(jnp.where(diag_mask, s_t, NEG_BIG) if masked else s_t)
        probs, alphas = [], []
        for h in range(MLA_HEADS):
            m_old = m_ref[h]
            m_new = jnp.maximum(m_old, jnp.max(scores[h], axis=0, keepdims=True))
            m_ref[h] = m_new
            probs.append(jnp.exp2(scores[h] - m_new[0:1, :]).astype(BF16))
            alphas.append(jnp.exp2(m_old[0:1, :] - m_new[0:1, :]))
        for h in range(MLA_HEADS):
            v_aug = vt_ref[j, h * V_AUG:(h + 1) * V_AUG, :]
            acc_ref[h] = alphas[h] * acc_ref[h] + _dot(v_aug, probs[h])

    def full_tile(j, carry):
        tile(j, False)
        return carry

    lax.fori_loop(0, i, full_tile, 0)
    tile(i, True)

    outs = []
    for h in range(MLA_HEADS):
        acc = acc_ref[h]
        outs.append(acc[:MLA_V, :] / acc[MLA_V:MLA_V + 1, :])
    ssq = outs[0] * outs[0]
    for o in outs[1:]:
        ssq = ssq + o * o
    inv = lax.rsqrt(jnp.sum(ssq, axis=0, keepdims=True) / C_V + EPS)
    for h in range(MLA_HEADS):
        vs = slice(h * MLA_V, (h + 1) * MLA_V)
        ot_ref[vs, :] = outs[h] * inv * nw_ref[vs, :]


def _attention(qt, k, vt, norm_w_col, bsz, seq, tq):
    nq = seq // tq
    return pl.pallas_call(
        functools.partial(_attn_kernel, tq=tq),
        grid=(bsz, nq),
        in_specs=[pl.BlockSpec((C_HEADS, tq), lambda b, i: (0, b * nq + i)),
                  pl.BlockSpec((seq, C_HEADS), lambda b, i: (b, 0)),
                  pl.BlockSpec((nq, C_VAUG, tq), lambda b, i: (b, 0, 0)),
                  pl.BlockSpec((C_V, 1), lambda b, i: (0, 0))],
        out_specs=pl.BlockSpec((C_V, tq), lambda b, i: (0, b * nq + i)),
        out_shape=jax.ShapeDtypeStruct((C_V, bsz * seq), F32),
        scratch_shapes=[pltpu.VMEM((MLA_HEADS, 8, tq), F32),
                        pltpu.VMEM((MLA_HEADS, V_AUG, tq), F32)],
        compiler_params=pltpu.CompilerParams(dimension_semantics=("parallel", "parallel"),
                                             vmem_limit_bytes=VMEM_LIMIT),
        name="attention",
    )(qt, k, vt, norm_w_col)


def _router_kernel(x_ref, oa_ref, obt_ref, wo_ref, nw_ref, rw_ref, rb_ref,
                   x1_ref, h_ref, eidx_ref, gate_ref, rank_ref, cnt_ref, run_ref, *, tm):
    @pl.when(pl.program_id(0) == 0)
    def _():
        run_ref[...] = jnp.zeros_like(run_ref)

    half = oa_ref.shape[-1]
    y = (_dot(oa_ref[...].astype(BF16), wo_ref[0:half, :])
         + _dot_tn(obt_ref[...].astype(BF16), wo_ref[half:, :]))
    x1 = x_ref[...] + y
    x1_ref[...] = x1
    hn = x1 * lax.rsqrt(jnp.mean(x1 * x1, axis=-1, keepdims=True) + EPS) * nw_ref[...]
    d_half = hn.shape[-1] // 2
    _store_packed_rows(h_ref, _pack_bf16_pair(hn[:, :d_half], hn[:, d_half:]))
    lg = _dot_split_nt(_split_bf16(rw_ref[...]), _split_bf16(hn)) + rb_ref[...]
    expert = lax.broadcasted_iota(jnp.int32, lg.shape, 0)
    sels, tops, idxs = [], [], []
    for _ in range(TOP_K):
        m = jnp.max(lg, axis=0, keepdims=True)
        idx = jnp.min(jnp.where(lg == m, expert, N_EXPERTS), axis=0, keepdims=True)
        sel = expert == idx
        lg = jnp.where(sel, -jnp.inf, lg)
        sels.append(sel)
        tops.append(m)
        idxs.append(idx)
    exps = [jnp.exp(tv - tops[0]) for tv in tops]
    den = exps[0] + exps[1] + exps[2] + exps[3]
    multi = sels[0].astype(F32)
    for sel in sels[1:]:
        multi = multi + sel.astype(F32)
    ri = lax.broadcasted_iota(jnp.int32, (tm, tm), 0)
    ci = lax.broadcasted_iota(jnp.int32, (tm, tm), 1)
    before = _dot(multi.astype(BF16), (ri < ci).astype(BF16)) + run_ref[...]
    ranks = [jnp.sum(jnp.where(sel, before, 0.0), axis=0, keepdims=True).astype(jnp.int32) for sel in sels]
    run = run_ref[...] + jnp.sum(multi, axis=1, keepdims=True)
    run_ref[...] = run
    cnt_ref[...] = jnp.broadcast_to(run, cnt_ref.shape)
    pad_i = jnp.zeros((8 - TOP_K, tm), jnp.int32)
    eidx_ref[...] = jnp.concatenate(idxs + [pad_i], axis=0)
    rank_ref[...] = jnp.concatenate(ranks + [pad_i], axis=0)
    gates_t = jnp.concatenate([e / den for e in exps] + [jnp.zeros((LANES - TOP_K, tm), F32)], axis=0)
    eye = (ri == ci).astype(BF16)
    g_hi, g_lo = _split_bf16(gates_t)
    gate_ref[...] = _dot_nt(eye, g_hi) + _dot_nt(eye, g_lo)


def _router(x2, oa, obt, w_out, norm_w, rw, rb, tm):
    t, d = x2.shape
    half = oa.shape[-1]
    s_words = d // 2 // LANES
    row = lambda i: (i, 0)
    col = lambda i: (0, i)
    const = lambda i: (0, 0)
    return pl.pallas_call(
        functools.partial(_router_kernel, tm=tm),
        grid=(t // tm,),
        in_specs=[pl.BlockSpec((tm, d), row),
                  pl.BlockSpec((tm, half), row),
                  pl.BlockSpec((half, tm), lambda i: (0, i)),
                  pl.BlockSpec((2 * half, d), const),
                  pl.BlockSpec((1, d), const),
                  pl.BlockSpec((N_EXPERTS, d), const),
                  pl.BlockSpec((N_EXPERTS, 1), const)],
        out_specs=[pl.BlockSpec((tm, d), row), pl.BlockSpec((tm * s_words, LANES), row),
                   pl.BlockSpec((8, tm), col), pl.BlockSpec((tm, LANES), row),
                   pl.BlockSpec((8, tm), col), pl.BlockSpec((N_EXPERTS, LANES), const)],
        out_shape=[jax.ShapeDtypeStruct((t, d), F32), jax.ShapeDtypeStruct((t * s_words, LANES), jnp.int32),
                   jax.ShapeDtypeStruct((8, t), jnp.int32), jax.ShapeDtypeStruct((t, LANES), F32),
                   jax.ShapeDtypeStruct((8, t), jnp.int32), jax.ShapeDtypeStruct((N_EXPERTS, LANES), F32)],
        scratch_shapes=[pltpu.VMEM((N_EXPERTS, 1), F32)],
        compiler_params=pltpu.CompilerParams(dimension_semantics=("arbitrary",),
                                             vmem_limit_bytes=VMEM_LIMIT),
        name="router",
    )(x2, oa, obt, w_out, norm_w, rw, rb)


SC_ROWS = 64


def _sc_mesh():
    return plsc.VectorSubcoreMesh(core_axis_name="c", subcore_axis_name="s")


def _dispatch(dest_flat, h3, n_pad):
    t, s, _ = h3.shape
    mesh = _sc_mesh()
    n_workers = mesh.num_cores * mesh.num_subcores
    per_worker = t // n_workers
    assert per_worker % SC_ROWS == 0

    @functools.partial(
        pl.kernel, mesh=mesh, out_type=jax.ShapeDtypeStruct((n_pad, s, LANES), h3.dtype),
        scratch_types=[pltpu.VMEM((TOP_K, SC_ROWS), jnp.int32), pltpu.VMEM((SC_ROWS, s, LANES), h3.dtype),
                       pltpu.SemaphoreType.DMA],
        name="sc_dispatch")
    def scatter(h_hbm, dest_hbm, out_hbm, idx_v, rows_v, sem):
        base = (lax.axis_index("s") * mesh.num_cores + lax.axis_index("c")) * per_worker

        @pl.loop(0, per_worker // SC_ROWS)
        def _(i):
            off = pl.multiple_of(base + i * SC_ROWS, SC_ROWS)
            pltpu.sync_copy(h_hbm.at[pl.ds(off, SC_ROWS)], rows_v)
            for kk in range(TOP_K):
                pltpu.sync_copy(dest_hbm.at[pl.ds(kk * t + off, SC_ROWS)], idx_v.at[kk])
            copies = [pltpu.async_copy(rows_v, out_hbm.at[idx_v.at[kk]], sem) for kk in range(TOP_K)]
            for cp in copies:
                cp.wait()

    return scatter(h3, dest_flat)


def _expert_kernel(cnt_ref, blk0_ref, nblk_ref, total_ref, x_hbm, w1_ref, b1g_ref, b1l_ref, w2_ref, b2_ref,
                   y_hbm, w1s_ref, w2s_ref, xbuf, ybuf, xsem, ysem, *, bm):
    e = pl.program_id(0)
    n_blocks = nblk_ref[e]
    first = blk0_ref[e]
    total = total_ref[0]
    rows_per_block = xbuf.shape[1]
    s_words = rows_per_block // bm
    n_tiles = w1s_ref.shape[-1] // MXU_COLS
    half = MXU_COLS // 2

    def x_copy(g, slot):
        return pltpu.make_async_copy(x_hbm.at[pl.ds(pl.multiple_of(g * rows_per_block, rows_per_block),
                                                    rows_per_block)], xbuf.at[slot], xsem.at[slot])

    def y_copy(g, slot):
        return pltpu.make_async_copy(ybuf.at[slot], y_hbm.at[pl.ds(pl.multiple_of(g * rows_per_block,
                                                                                   rows_per_block),
                                                                   rows_per_block)], ysem.at[slot])

    @pl.when(e == 0)
    def _():
        x_copy(0, 0).start()

    @pl.when(n_blocks > 0)
    def _():
        r = lax.broadcasted_iota(jnp.int32, (MXU_COLS, MXU_COLS), 0)
        c = lax.broadcasted_iota(jnp.int32, (MXU_COLS, MXU_COLS), 1)
        perm = (r == jnp.where(c < half, 2 * c, 2 * (c - half) + 1)).astype(BF16)
        for tix in range(n_tiles):
            cols = slice(tix * MXU_COLS, (tix + 1) * MXU_COLS)
            w1s_ref[:, cols] = _dot(w1_ref[0, :, cols].astype(BF16), perm).astype(BF16)
        w2s_ref[...] = w2_ref[0].astype(BF16)

        def block(j, carry):
            g = first + j
            slot = lax.rem(g, 2)
            x_copy(g, slot).wait()

            @pl.when(g + 1 < total)
            def _():
                x_copy(g + 1, 1 - slot).start()

            @pl.when(g >= 2)
            def _():
                y_copy(g - 2, slot).wait()

            n_rows = cnt_ref[e] - j * bm

            def ffn(m):
                words = _load_packed_rows(xbuf, (slot,), m, s_words)
                row = lax.broadcasted_iota(jnp.int32, words.shape, 0)
                x_lo, x_hi = _unpack_bf16_pair(jnp.where(row < n_rows, words, 0))
                x = jnp.concatenate([x_lo.astype(BF16), x_hi.astype(BF16)], axis=1)
                acts = []
                for tix in range(n_tiles):
                    hp = _dot(x, w1s_ref[:, tix * MXU_COLS:(tix + 1) * MXU_COLS])
                    feat = slice(tix * half, (tix + 1) * half)
                    gl = jnp.minimum(hp[:, :half] + b1g_ref[0][:, feat], SWIGLU_LIMIT)
                    lin = jnp.clip(hp[:, half:] + b1l_ref[0][:, feat], -SWIGLU_LIMIT, SWIGLU_LIMIT)
                    acts.append((gl * _sigmoid(SWIGLU_ALPHA * gl) * (lin + 1.0)).astype(BF16))
                y = _dot(jnp.concatenate(acts, axis=1), w2s_ref[...]) + b2_ref[0]
                d_half = y.shape[-1] // 2
                _store_packed_rows(ybuf.at[slot], _pack_bf16_pair(y[:, :d_half], y[:, d_half:]))

            @pl.when(n_rows > bm // 2)
            def _():
                ffn(bm)

            @pl.when(n_rows <= bm // 2)
            def _():
                ffn(bm // 2)

            y_copy(g, slot).start()
            return carry

        lax.fori_loop(0, n_blocks, block, 0)

    @pl.when(e == pl.num_programs(0) - 1)
    def _():
        @pl.when(total >= 2)
        def _():
            y_copy(total - 2, lax.rem(total, 2)).wait()

        y_copy(total - 1, lax.rem(total - 1, 2)).wait()


def _experts(tables, xs, w1, b1g, b1l, w2, b2, bm, d):
    s_words = d // 2 // LANES
    dff2 = w1.shape[-1]
    dff = dff2 // 2
    emap = lambda e, *_: (e, 0, 0)
    grid_spec = pltpu.PrefetchScalarGridSpec(
        num_scalar_prefetch=len(tables),
        grid=(N_EXPERTS,),
        in_specs=[pl.BlockSpec(memory_space=pl.ANY),
                  pl.BlockSpec((1, d, dff2), emap),
                  pl.BlockSpec((1, 1, dff), emap),
                  pl.BlockSpec((1, 1, dff), emap),
                  pl.BlockSpec((1, dff, d), emap),
                  pl.BlockSpec((1, 1, d), emap)],
        out_specs=pl.BlockSpec(memory_space=pl.ANY),
        scratch_shapes=[pltpu.VMEM((d, dff2), BF16), pltpu.VMEM((dff, d), BF16),
                        pltpu.VMEM((2, bm * s_words, LANES), jnp.int32),
                        pltpu.VMEM((2, bm * s_words, LANES), jnp.int32),
                        pltpu.SemaphoreType.DMA((2,)), pltpu.SemaphoreType.DMA((2,))],
    )
    return pl.pallas_call(
        functools.partial(_expert_kernel, bm=bm),
        grid_spec=grid_spec,
        out_shape=jax.ShapeDtypeStruct(xs.shape, jnp.int32),
        compiler_params=pltpu.CompilerParams(dimension_semantics=("arbitrary",),
                                             vmem_limit_bytes=VMEM_LIMIT),
        name="experts",
    )(*tables, xs, w1, b1g, b1l, w2, b2)


def _gather_rows(dest_flat, y3):
    n_out = dest_flat.shape[0]
    _, s, _ = y3.shape
    mesh = _sc_mesh()
    n_workers = mesh.num_cores * mesh.num_subcores
    per_worker = n_out // n_workers
    n_chunks = per_worker // SC_ROWS
    assert per_worker % SC_ROWS == 0 and n_chunks % 2 == 0

    @functools.partial(
        pl.kernel, mesh=mesh, out_type=jax.ShapeDtypeStruct((n_out, s, LANES), y3.dtype),
        scratch_types=[pltpu.VMEM((2, SC_ROWS), jnp.int32), pltpu.VMEM((2, SC_ROWS, s, LANES), y3.dtype),
                       pltpu.SemaphoreType.DMA((2,)), pltpu.SemaphoreType.DMA((2,))],
        name="sc_gather")
    def gather(y_hbm, dest_hbm, out_hbm, idx_v, rows_v, gsem, wsem):
        base = (lax.axis_index("s") * mesh.num_cores + lax.axis_index("c")) * per_worker

        def rows_of(chunk):
            return pl.ds(pl.multiple_of(base + chunk * SC_ROWS, SC_ROWS), SC_ROWS)

        def gather_copy(slot):
            return pltpu.make_async_copy(y_hbm.at[idx_v.at[slot]], rows_v.at[slot], gsem.at[slot])

        def write_copy(chunk, slot):
            return pltpu.make_async_copy(rows_v.at[slot], out_hbm.at[rows_of(chunk)], wsem.at[slot])

        def start_gather(chunk, slot):
            pltpu.sync_copy(dest_hbm.at[rows_of(chunk)], idx_v.at[slot])
            gather_copy(slot).start()

        start_gather(0, 0)

        @pl.loop(0, n_chunks // 2)
        def _(pair):
            for slot in (0, 1):
                chunk = 2 * pair + slot
                gather_copy(slot).wait()
                write_copy(chunk, slot).start()

                @pl.when(chunk >= 1)
                def _():
                    write_copy(chunk - 1, 1 - slot).wait()

                @pl.when(chunk + 1 < n_chunks)
                def _():
                    start_gather(chunk + 1, 1 - slot)

        write_copy(n_chunks - 1, 1).wait()

    return gather(y3, dest_flat)


def _combine_kernel(x1_ref, gate_ref, y4_ref, nw_ref, o_ref, *, tm, normalize):
    x = x1_ref[...]
    gates = gate_ref[...]
    for kk in range(TOP_K):
        lo, hi = _unpack_bf16_pair(_load_packed_rows(y4_ref, (kk,), tm, y4_ref.shape[1] // tm))
        x = x + gates[:, kk:kk + 1] * jnp.concatenate([lo, hi], axis=1)
    if normalize:
        x = x * lax.rsqrt(jnp.mean(x * x, axis=-1, keepdims=True) + EPS) * nw_ref[...]
    o_ref[...] = x


def _combine(x1, gates, y4, norm_w, tm, normalize):
    t, d = x1.shape
    s_words = d // 2 // LANES
    row = lambda i: (i, 0)
    return pl.pallas_call(
        functools.partial(_combine_kernel, tm=tm, normalize=normalize),
        grid=(t // tm,),
        in_specs=[pl.BlockSpec((tm, d), row), pl.BlockSpec((tm, LANES), row),
                  pl.BlockSpec((TOP_K, tm * s_words, LANES), lambda i: (0, i, 0)),
                  pl.BlockSpec((1, d), lambda i: (0, 0))],
        out_specs=pl.BlockSpec((tm, d), row),
        out_shape=jax.ShapeDtypeStruct((t, d), F32),
        compiler_params=pltpu.CompilerParams(dimension_semantics=("parallel",)),
        name="combine",
    )(x1, gates, y4, norm_w)


def _tiles(t, seq):
    tm = min(2 * MXU_COLS, t)
    tr = min(2 * MXU_COLS, t)
    tq = min(2 * MXU_COLS, seq)
    gdn_rows = min(2 * CHUNK, seq)
    bm = 2 * MXU_COLS
    assert t % tm == 0 and t % tr == 0 and seq % tq == 0 and seq % gdn_rows == 0 and tq % CHUNK == 0
    return tm, tr, tq, gdn_rows, bm


def _rot_half(w):
    half = w.shape[-1] // 2
    return jnp.concatenate([-w[..., half:], w[..., :half]], axis=-1)


def _prep_mla_weights(w_uq, w_ukv):
    dq = MLA_NOPE + MLA_ROPE
    wq = w_uq.reshape(MLA_Q_LORA, MLA_HEADS, dq)
    wq_rot = jnp.concatenate([jnp.zeros_like(wq[..., :MLA_NOPE]), _rot_half(wq[..., MLA_NOPE:])], axis=-1)
    pad = lambda w: jnp.pad(w, ((0, 0), (0, 0), (0, LANES - w.shape[-1]))).reshape(w.shape[0], C_HEADS)
    wkv = w_ukv.reshape(MLA_KV_LORA, MLA_HEADS, MLA_NOPE + MLA_V)
    wv = wkv[..., MLA_NOPE:].reshape(MLA_KV_LORA, C_V)
    return (pad(wq).T.astype(BF16), pad(wq_rot).T.astype(BF16), pad(wkv[..., :MLA_NOPE]).astype(BF16),
            wv.T.astype(BF16))


def _rope_freq():
    half = MLA_ROPE // 2
    return (ROPE_THETA ** (-jnp.arange(half, dtype=F32) / half)).reshape(half, 1)


def _lane_row(v):
    return jnp.pad(v.astype(F32), (0, LANES - v.shape[0])).reshape(1, LANES)


def _routing_tables(cnt, eidx, rank, bm):
    e_ids = jnp.arange(N_EXPERTS, dtype=jnp.int32)
    counts = cnt[:, 0].astype(jnp.int32)
    padded = (counts + bm - 1) // bm * bm
    pend = jnp.sum(jnp.where(e_ids[None, :] <= e_ids[:, None], padded[None, :], 0), axis=1)
    pstart = pend - padded
    dest = jnp.sum(jnp.where(eidx[:TOP_K, :, None] == e_ids, pstart, 0), axis=-1) + rank[:TOP_K]
    tables = (counts, pstart // bm, padded // bm, pend[-1:] // bm)
    return dest.reshape(-1), tuple(t.astype(jnp.int32) for t in tables)


def kernel(x, positions, norm_mix_w, w_in, conv_w, a_log, dt_bias, gdn_norm_w, q_norm_w, w_uq, kv_norm_w,
           w_ukv, mla_out_norm_w, w_out, norm_ffn_w, router_w, router_b, w1, b1, w2, b2, norm_final_w):
    bsz, seq, d = x.shape
    t = bsz * seq
    depth = w_in.shape[0]
    tm, tr, tq, gdn_rows, bm = _tiles(t, seq)
    x2 = x.reshape(t, d)
    posf = positions.astype(F32)
    freq = _rope_freq()
    for l in range(depth):
        qkv, z, cq, ckv, kr2, ab = _inproj(x2, norm_mix_w[l].reshape(1, d), w_in[l], tm)
        conv_w8 = jnp.pad(conv_w[l], ((0, 8 - GDN_CONV), (0, 0)))
        o_a = _gdn(qkv.reshape(bsz, seq, C_QKV), z.reshape(bsz, seq, C_Z), ab.reshape(bsz, seq, LANES),
                   conv_w8, _lane_row(a_log[l]), _lane_row(dt_bias[l]), gdn_norm_w[l].reshape(1, GDN_D),
                   gdn_rows).reshape(t, C_Z)
        wqt, wqrt, wk, wvt = _prep_mla_weights(w_uq[l], w_ukv[l])
        qt, k, vt = _mla_proj(cq, ckv, kr2, posf.reshape(1, t),
                              q_norm_w[l].reshape(1, -1), kv_norm_w[l].reshape(1, -1),
                              wqt, wqrt, wk, wvt, freq, tq)
        o_bt = _attention(qt, k, vt, mla_out_norm_w[l].reshape(-1, 1), bsz, seq, tq)
        x1, hn, eidx, gates, rank, cnt = _router(x2, o_a, o_bt, w_out[l].astype(BF16),
                                                 norm_ffn_w[l].reshape(1, d), router_w[l].T,
                                                 router_b[l].reshape(N_EXPERTS, 1), tr)
        n_pad = t * TOP_K + N_EXPERTS * bm
        dest_flat, tables = _routing_tables(cnt, eidx, rank, bm)
        s_words = d // 2 // LANES
        buf = _dispatch(dest_flat, hn.reshape(t, s_words, LANES), n_pad)
        yb = _experts(tables, buf.reshape(n_pad * s_words, LANES), w1[l],
                      b1[l][:, None, 0::2], b1[l][:, None, 1::2], w2[l], b2[l][:, None, :], bm, d)
        y4 = _gather_rows(dest_flat, yb.reshape(n_pad, s_words, LANES))
        x2 = _combine(x1, gates, y4.reshape(TOP_K, t * s_words, LANES), norm_final_w.reshape(1, d), tr,
                      normalize=(l == depth - 1))
    return x2.reshape(bsz, seq, d)
```

```python
import functools
import math

import jax
import jax.numpy as jnp
from jax import lax
from jax.experimental import pallas as pl
from jax.experimental.pallas import tpu as pltpu
from jax.experimental.pallas import tpu_sc as plsc

F32 = jnp.float32
BF16 = jnp.bfloat16
HIGHEST = lax.Precision.HIGHEST

LANES = 128
MXU_COLS = 256
EPS = 1e-6
CHUNK = 64

GDN_HEADS = 4
GDN_D = 128
GDN_CONV = 4
MLA_HEADS = 8
MLA_NOPE = 64
MLA_ROPE = 32
MLA_V = 64
MLA_Q_LORA = 384
MLA_KV_LORA = 256
ROPE_THETA = 10000.0
N_EXPERTS = 32
TOP_K = 4
SWIGLU_LIMIT = 7.0
SWIGLU_ALPHA = 1.702

NEG_BIG = -1e30
VMEM_LIMIT = 52 * 1024 * 1024


def _sigmoid(x):
    return 1.0 / (1.0 + jnp.exp(-x))


def _softplus(x):
    return jnp.maximum(x, 0.0) + jnp.log(1.0 + jnp.exp(-jnp.abs(x)))


def _dot(a, b, precision=None):
    return jnp.dot(a, b, preferred_element_type=F32, precision=precision)


def _dot_nt(a, b, precision=None):
    return lax.dot_general(a, b, (((1,), (1,)), ((), ())), preferred_element_type=F32,
                           precision=precision)


def _dot_tn(a, b):
    return lax.dot_general(a, b, (((0,), (0,)), ((), ())), preferred_element_type=F32)


def _split_bf16(a):
    hi = a.astype(BF16)
    return hi, (a - hi.astype(F32)).astype(BF16)


def _pack_bf16_pair(lo, hi):
    lo_bits = pltpu.bitcast(lo.astype(BF16).astype(F32), jnp.int32)
    hi_bits = pltpu.bitcast(hi.astype(BF16).astype(F32), jnp.int32)
    return jnp.bitwise_or(hi_bits, lax.shift_right_logical(lo_bits, 16))


def _unpack_bf16_pair(w):
    lo = pltpu.bitcast(lax.shift_left(w, 16), F32)
    hi = pltpu.bitcast(jnp.bitwise_and(w, -65536), F32)
    return lo, hi


def _load_packed_rows(ref, lead, m, s):
    return jnp.concatenate([ref[(*lead, pl.ds(j, m, stride=s), slice(None))] for j in range(s)], axis=1)


def _store_packed_rows(ref, words):
    m = words.shape[0]
    s = words.shape[1] // LANES
    for j in range(s):
        ref[pl.ds(j, m, stride=s), :] = words[:, j * LANES:(j + 1) * LANES]


def _dot_split_nt(a, b):
    return _dot_nt(a[0], b[0]) + _dot_nt(a[0], b[1]) + _dot_nt(a[1], b[0])


def _dot_split(a, b):
    return _dot(a[0], b[0]) + _dot(a[0], b[1]) + _dot(a[1], b[0])


C_QKV = 3 * GDN_HEADS * GDN_D
C_Z = GDN_HEADS * GDN_D
C_IN = C_QKV + C_Z + MLA_Q_LORA + MLA_KV_LORA + 3 * LANES


def _inproj_kernel(x_ref, nw_ref, w_ref, qkv_ref, z_ref, cq_ref, ckv_ref, kr_ref, ab_ref, wb_ref):
    @pl.when(pl.program_id(0) == 0)
    def _():
        d = w_ref.shape[0]
        hd4 = C_QKV + C_Z
        wb_ref[:, 0:hd4] = w_ref[:, 0:hd4].astype(BF16)
        tail = w_ref[:, hd4:]
        o_cq = 2 * GDN_HEADS
        o_ckv = o_cq + MLA_Q_LORA
        o_kr = o_ckv + MLA_KV_LORA
        kr = tail[:, o_kr:o_kr + MLA_ROPE]
        half = MLA_ROPE // 2
        zeros = lambda n: jnp.zeros((d, n), F32)
        pad = LANES - MLA_NOPE - MLA_ROPE
        groups = [tail[:, o_cq:o_ckv], tail[:, o_ckv:o_kr],
                  jnp.concatenate([zeros(MLA_NOPE), kr, zeros(pad)], axis=1),
                  jnp.concatenate([zeros(MLA_NOPE), -kr[:, half:], kr[:, :half], zeros(pad)], axis=1),
                  jnp.concatenate([tail[:, 0:o_cq], zeros(LANES - o_cq)], axis=1)]
        o = hd4
        for grp in groups:
            wb_ref[:, o:o + grp.shape[1]] = grp.astype(BF16)
            o += grp.shape[1]

    x = x_ref[...]
    var = jnp.mean(x * x, axis=-1, keepdims=True)
    h = (x * lax.rsqrt(var + EPS) * nw_ref[...]).astype(BF16)
    p = _dot(h, wb_ref[...])
    o = 0
    for ref in (qkv_ref, z_ref, cq_ref, ckv_ref, kr_ref, ab_ref):
        n = ref.shape[-1]
        ref[...] = p[:, o:o + n]
        o += n


def _inproj(x2, norm_w, w_in, tm):
    t, d = x2.shape
    widths = (C_QKV, C_Z, MLA_Q_LORA, MLA_KV_LORA, 2 * LANES, LANES)
    return pl.pallas_call(
        _inproj_kernel,
        grid=(t // tm,),
        in_specs=[pl.BlockSpec((tm, d), lambda i: (i, 0)),
                  pl.BlockSpec((1, d), lambda i: (0, 0)),
                  pl.BlockSpec(w_in.shape, lambda i: (0, 0), pipeline_mode=pl.Buffered(1))],
        out_specs=[pl.BlockSpec((tm, n), lambda i: (i, 0)) for n in widths],
        out_shape=[jax.ShapeDtypeStruct((t, n), F32) for n in widths],
        scratch_shapes=[pltpu.VMEM((d, C_IN), BF16)],
        compiler_params=pltpu.CompilerParams(dimension_semantics=("arbitrary",),
                                             vmem_limit_bytes=VMEM_LIMIT),
        name="inproj",
    )(x2, norm_w, w_in)


def _gdn_kernel(qkv_ref, z_ref, ab_ref, cw_ref, alog_ref, dtb_ref, nw_ref, o_ref, ext_ref, s_ref):
    c = CHUNK
    bsz, rows, _ = qkv_ref.shape
    chunks_per_batch = rows // c

    @pl.when(pl.program_id(0) == 0)
    def _():
        ext_ref[:, 0:8, :] = jnp.zeros((bsz, 8, C_QKV), F32)
        s_ref[...] = jnp.zeros_like(s_ref)

    cw = cw_ref[...]
    nw = nw_ref[...]
    rr = lax.broadcasted_iota(jnp.int32, (rows, rows), 0)
    rc = lax.broadcasted_iota(jnp.int32, (rows, rows), 1)
    in_chunk_prefix = jnp.where(rr // c == rc // c, jnp.where(rr >= rc, 1.0, 0.0), 0.0)
    e8 = (lax.broadcasted_iota(jnp.int32, (8, LANES), 0)
          == lax.broadcasted_iota(jnp.int32, (8, LANES), 1)).astype(F32)
    ri = lax.broadcasted_iota(jnp.int32, (c, c), 0)
    ci = lax.broadcasted_iota(jnp.int32, (c, c), 1)
    causal = ri >= ci
    strict = ri > ci
    eye = (ri == ci).astype(F32)
    chains = [(j, h) for j in range(chunks_per_batch) for h in range(GDN_HEADS)]

    def prep_pieces(b):
        g = {"xc": [None] * (C_QKV // LANES)}

        def conv(tiles):
            def run():
                cols = slice(tiles[0] * LANES, (tiles[-1] + 1) * LANES)
                x = qkv_ref[b, :, cols]
                ext_ref[b, 8:8 + rows, cols] = x
                ext = ext_ref[b, :, cols]
                xb = cw[3:4, cols] * x
                for j in range(1, GDN_CONV):
                    xb = xb + cw[3 - j:4 - j, cols] * pltpu.roll(ext, j, axis=0)[8:8 + rows, :]
                ext_ref[b, 0:8, cols] = x[rows - 8:rows, :]
                xb = xb * _sigmoid(xb)
                for i, tile in enumerate(tiles):
                    g["xc"][tile] = xb[:, i * LANES:(i + 1) * LANES]
            return run

        def gates():
            ab = ab_ref[b]
            g_all = -jnp.exp(alog_ref[...]) * _softplus(ab + dtb_ref[...])
            g["beta"] = _sigmoid(ab)
            g["g_cum"] = _dot(in_chunk_prefix, g_all, HIGHEST)
            g["g_row"] = _dot_nt(e8, g["g_cum"], HIGHEST)
            for name in ("q", "k", "v", "gc", "bt", "dec"):
                g[name] = []

        def chunk(j):
            def run():
                rs = slice(j * c, (j + 1) * c)
                xc = g["xc"]
                for h in range(GDN_HEADS):
                    q = xc[h][rs]
                    k = xc[GDN_HEADS + h][rs]
                    g["q"].append(q * lax.rsqrt(jnp.sum(q * q, axis=-1, keepdims=True) + EPS) * (GDN_D ** -0.5))
                    g["k"].append(k * lax.rsqrt(jnp.sum(k * k, axis=-1, keepdims=True) + EPS))
                    g["v"].append(xc[2 * GDN_HEADS + h][rs])
                    gc = g["g_cum"][rs, h:h + 1]
                    gr = g["g_row"][h:h + 1, rs]
                    g["gc"].append(gc)
                    g["bt"].append(g["beta"][rs, GDN_HEADS + h:GDN_HEADS + h + 1])
                    g["dec"].append(jnp.exp(jnp.where(causal, gc - gr, -jnp.inf)))
            return run

        def a_mats():
            g["kb"] = [k * bt for k, bt in zip(g["k"], g["bt"])]
            g["k16"] = [k.astype(BF16) for k in g["k"]]
            g["a"] = [jnp.where(strict, _dot_nt(kb.astype(BF16), k16) * dec, 0.0)
                      for kb, k16, dec in zip(g["kb"], g["k16"], g["dec"])]

        n_tiles = C_QKV // LANES
        conv_pieces = [conv(list(range(i, i + 2))) for i in range(0, n_tiles, 2)]
        return g, conv_pieces + [gates] + [chunk(j) for j in range(chunks_per_batch)] + [a_mats]

    def inverse_rounds(g):
        def start():
            g["t"] = [eye - a for a in g["a"]]
            g["pw"] = [_split_bf16(a) for a in g["a"]]

        def square():
            g["pw"] = [_split_bf16(_dot_split(p, p)) for p in g["pw"]]

        def update():
            g["t"] = [t + _dot_split(_split_bf16(t), p) for t, p in zip(g["t"], g["pw"])]

        return [start] + [square, update] * 5

    groups = []
    pending = []
    for b in range(bsz):
        g, pieces = prep_pieces(b)
        groups.append(g)
        n_a, n_b = len(pending), len(pieces)
        ia = ib = 0
        while ia < n_a or ib < n_b:
            if ib >= n_b or (ia < n_a and ia * n_b <= ib * n_a):
                pending[ia]()
                ia += 1
            else:
                pieces[ib]()
                ib += 1
        pending = inverse_rounds(g)
    for fn in pending:
        fn()

    uws, qk16s, qd16s, kd16s, g_tots = [], [], [], [], []
    for g in groups:
        e_gs = [jnp.exp(gc) for gc in g["gc"]]
        uws.append([_dot(t.astype(BF16), jnp.concatenate([v * bt, kb * eg], axis=1).astype(BF16))
                    for t, v, bt, kb, eg in zip(g["t"], g["v"], g["bt"], g["kb"], e_gs)])
        qk16s.append([(_dot_nt(q.astype(BF16), k16) * dec).astype(BF16)
                      for q, k16, dec in zip(g["q"], g["k16"], g["dec"])])
        qd16s.append([(q * eg).astype(BF16) for q, eg in zip(g["q"], e_gs)])
        kd16s.append([(k * jnp.exp(gc[c - 1:c, :] - gc)).astype(BF16) for k, gc in zip(g["k"], g["gc"])])
        g_tots.append([jnp.exp(gc[c - 1:c, :]) for gc in g["gc"]])

    pairs = [(b, h) for b in range(bsz) for h in range(GDN_HEADS)]
    states = [s_ref[b * GDN_HEADS + h] for b, h in pairs]
    for j in range(chunks_per_batch):
        rs = slice(j * c, (j + 1) * c)
        pick = lambda per_group: [per_group[b][chains.index((j, h))] for b, h in pairs]
        uw_j, qk_j, qd_j, kd_j, gt_j = (pick(x) for x in (uws, qk16s, qd16s, kd16s, g_tots))
        s16s = [s.astype(BF16) for s in states]
        v16s = [(uw[:, :GDN_D] - _dot(uw[:, GDN_D:].astype(BF16), s16)).astype(BF16)
                for uw, s16 in zip(uw_j, s16s)]
        states = [s * gt + _dot_tn(kd, v16) for s, gt, kd, v16 in zip(states, gt_j, kd_j, v16s)]
        os_ = [_dot(qd, s16) + _dot(qk, v16) for qd, qk, s16, v16 in zip(qd_j, qk_j, s16s, v16s)]
        for (b, h), o in zip(pairs, os_):
            on = o * lax.rsqrt(jnp.mean(o * o, axis=-1, keepdims=True) + EPS) * nw
            zh = z_ref[b, rs, h * GDN_D:(h + 1) * GDN_D]
            o_ref[b, rs, h * GDN_D:(h + 1) * GDN_D] = on * (zh * _sigmoid(zh))
    for p, s in enumerate(states):
        s_ref[p] = s


def _gdn(qkv, z, ab, conv_w8, alog_row, dtb_row, norm_w, rows):
    bsz, seq, _ = qkv.shape
    blk = lambda i: (0, i, 0)
    const = lambda i: (0, 0)
    return pl.pallas_call(
        _gdn_kernel,
        grid=(seq // rows,),
        in_specs=[pl.BlockSpec((bsz, rows, C_QKV), blk),
                  pl.BlockSpec((bsz, rows, C_Z), blk),
                  pl.BlockSpec((bsz, rows, LANES), blk),
                  pl.BlockSpec((8, C_QKV), const),
                  pl.BlockSpec((1, LANES), const),
                  pl.BlockSpec((1, LANES), const),
                  pl.BlockSpec((1, GDN_D), const)],
        out_specs=pl.BlockSpec((bsz, rows, C_Z), blk),
        out_shape=jax.ShapeDtypeStruct((bsz, seq, C_Z), F32),
        scratch_shapes=[pltpu.VMEM((bsz, 8 + rows, C_QKV), F32),
                        pltpu.VMEM((bsz * GDN_HEADS, GDN_D, GDN_D), F32)],
        compiler_params=pltpu.CompilerParams(dimension_semantics=("arbitrary",),
                                             vmem_limit_bytes=VMEM_LIMIT),
        name="gdn",
    )(qkv, z, ab, conv_w8, alog_row, dtb_row, norm_w)


D_QK_PAD = LANES
C_HEADS = MLA_HEADS * D_QK_PAD
C_V = MLA_HEADS * MLA_V
V_AUG = MLA_V + 16
C_VAUG = MLA_HEADS * V_AUG
Q_SCALE = (MLA_NOPE + MLA_ROPE) ** -0.5 * math.log2(math.e)


def _mla_proj_kernel(cq_ref, ckv_ref, kr_ref, posr_ref, qnw_ref, kvnw_ref, wqt_ref, wqrt_ref,
                     wk_ref, wvt_ref, freqc_ref, qt_ref, k_ref, vt_ref):
    cq = cq_ref[...]
    cqn = (cq * lax.rsqrt(jnp.mean(cq * cq, axis=-1, keepdims=True) + EPS) * qnw_ref[...]).astype(BF16)
    ckv = ckv_ref[...]
    ckvn = (ckv * lax.rsqrt(jnp.mean(ckv * ckv, axis=-1, keepdims=True) + EPS)
            * kvnw_ref[...]).astype(BF16)
    tm = posr_ref.shape[-1]
    ang = freqc_ref[...] * posr_ref[...]
    c16 = jnp.cos(ang)
    s16 = jnp.sin(ang)
    pad = LANES - MLA_NOPE - MLA_ROPE
    cs_t = jnp.concatenate([jnp.ones((MLA_NOPE, tm), F32), c16, c16, jnp.ones((pad, tm), F32)], axis=0)
    sn_t = jnp.concatenate([jnp.zeros((MLA_NOPE, tm), F32), s16, s16, jnp.zeros((pad, tm), F32)], axis=0)
    cs = cs_t.T
    sn = sn_t.T
    qa_t = _dot_nt(wqt_ref[...], cqn)
    qb_t = _dot_nt(wqrt_ref[...], cqn)
    kn = _dot(ckvn, wk_ref[...])
    kr = kr_ref[...]
    kpe = kr[:, :LANES] * cs + kr[:, LANES:] * sn
    for h in range(MLA_HEADS):
        sl = slice(h * LANES, (h + 1) * LANES)
        qt_ref[sl, :] = ((qa_t[sl, :] * cs_t + qb_t[sl, :] * sn_t) * Q_SCALE).astype(BF16)
        k_ref[:, sl] = (kn[:, sl] + kpe).astype(BF16)
    v_t = _dot_nt(wvt_ref[...], ckvn).astype(BF16)
    ones = jnp.ones((V_AUG - MLA_V, v_t.shape[1]), BF16)
    for h in range(MLA_HEADS):
        vt_ref[0, h * V_AUG:h * V_AUG + MLA_V, :] = v_t[h * MLA_V:(h + 1) * MLA_V, :]
        vt_ref[0, h * V_AUG + MLA_V:(h + 1) * V_AUG, :] = ones


def _mla_proj(cq, ckv, kr2, posr, qnw, kvnw, wqt, wqrt, wk, wvt, freqc, tm):
    t = cq.shape[0]
    row = lambda i: (i, 0)
    col = lambda i: (0, i)
    const = lambda i: (0, 0)
    return pl.pallas_call(
        _mla_proj_kernel,
        grid=(t // tm,),
        in_specs=[pl.BlockSpec((tm, MLA_Q_LORA), row),
                  pl.BlockSpec((tm, MLA_KV_LORA), row),
                  pl.BlockSpec((tm, 2 * LANES), row),
                  pl.BlockSpec((1, tm), col),
                  pl.BlockSpec((1, MLA_Q_LORA), const),
                  pl.BlockSpec((1, MLA_KV_LORA), const),
                  pl.BlockSpec((C_HEADS, MLA_Q_LORA), const),
                  pl.BlockSpec((C_HEADS, MLA_Q_LORA), const),
                  pl.BlockSpec((MLA_KV_LORA, C_HEADS), const),
                  pl.BlockSpec((C_V, MLA_KV_LORA), const),
                  pl.BlockSpec((MLA_ROPE // 2, 1), const)],
        out_specs=[pl.BlockSpec((C_HEADS, tm), col),
                   pl.BlockSpec((tm, C_HEADS), row),
                   pl.BlockSpec((1, C_VAUG, tm), lambda i: (i, 0, 0))],
        out_shape=[jax.ShapeDtypeStruct((C_HEADS, t), BF16),
                   jax.ShapeDtypeStruct((t, C_HEADS), BF16),
                   jax.ShapeDtypeStruct((t // tm, C_VAUG, tm), BF16)],
        compiler_params=pltpu.CompilerParams(dimension_semantics=("parallel",),
                                             vmem_limit_bytes=VMEM_LIMIT),
        name="mla_proj",
    )(cq, ckv, kr2, posr, qnw, kvnw, wqt, wqrt, wk, wvt, freqc)


def _attn_kernel(qt_ref, k_ref, vt_ref, nw_ref, ot_ref, m_ref, acc_ref, *, tq):
    i = pl.program_id(1)
    m_ref[...] = jnp.full(m_ref.shape, NEG_BIG, F32)
    acc_ref[...] = jnp.zeros(acc_ref.shape, F32)
    key_chunk = lax.broadcasted_iota(jnp.int32, (tq, tq), 0) // CHUNK
    qry_chunk = lax.broadcasted_iota(jnp.int32, (tq, tq), 1) // CHUNK
    diag_mask = key_chunk <= qry_chunk

    def tile(j, masked):
        rows = pl.ds(pl.multiple_of(j * tq, tq), tq)

        scores = []
        for h in range(MLA_HEADS):
            hs = slice(h * LANES, (h + 1) * LANES)
            s_t = _dot(k_ref[rows, hs], qt_ref[hs, :])
            scores.append(jnp.where(diag_mask, s_t, NEG_BIG) if masked else s_t)
        probs, alphas = [], []
        for h in range(MLA_HEADS):
            m_old = m_ref[h]
            m_new = jnp.maximum(m_old, jnp.max(scores[h], axis=0, keepdims=True))
            m_ref[h] = m_new
            probs.append(jnp.exp2(scores[h] - m_new[0:1, :]).astype(BF16))
            alphas.append(jnp.exp2(m_old[0:1, :] - m_new[0:1, :]))
        for h in range(MLA_HEADS):
            v_aug = vt_ref[j, h * V_AUG:(h + 1) * V_AUG, :]
            acc_ref[h] = alphas[h] * acc_ref[h] + _dot(v_aug, probs[h])

    def full_tile(j, carry):
        tile(j, False)
        return carry

    lax.fori_loop(0, i, full_tile, 0)
    tile(i, True)

    outs = []
    for h in range(MLA_HEADS):
        acc = acc_ref[h]
        outs.append(acc[:MLA_V, :] / acc[MLA_V:MLA_V + 1, :])
    ssq = outs[0] * outs[0]
    for o in outs[1:]:
        ssq = ssq + o * o
    inv = lax.rsqrt(jnp.sum(ssq, axis=0, keepdims=True) / C_V + EPS)
    for h in range(MLA_HEADS):
        vs = slice(h * MLA_V, (h + 1) * MLA_V)
        ot_ref[vs, :] = outs[h] * inv * nw_ref[vs, :]


def _attention(qt, k, vt, norm_w_col, bsz, seq, tq):
    nq = seq // tq
    return pl.pallas_call(
        functools.partial(_attn_kernel, tq=tq),
        grid=(bsz, nq),
        in_specs=[pl.BlockSpec((C_HEADS, tq), lambda b, i: (0, b * nq + i)),
                  pl.BlockSpec((seq, C_HEADS), lambda b, i: (b, 0)),
                  pl.BlockSpec((nq, C_VAUG, tq), lambda b, i: (b, 0, 0)),
                  pl.BlockSpec((C_V, 1), lambda b, i: (0, 0))],
        out_specs=pl.BlockSpec((C_V, tq), lambda b, i: (0, b * nq + i)),
        out_shape=jax.ShapeDtypeStruct((C_V, bsz * seq), F32),
        scratch_shapes=[pltpu.VMEM((MLA_HEADS, 8, tq), F32),
                        pltpu.VMEM((MLA_HEADS, V_AUG, tq), F32)],
        compiler_params=pltpu.CompilerParams(dimension_semantics=("parallel", "parallel"),
                                             vmem_limit_bytes=VMEM_LIMIT),
        name="attention",
    )(qt, k, vt, norm_w_col)


def _router_kernel(x_ref, oa_ref, obt_ref, wo_ref, nw_ref, rw_ref, rb_ref,
                   x1_ref, h_ref, eidx_ref, gate_ref, rank_ref, cnt_ref, run_ref, *, tm):
    @pl.when(pl.program_id(0) == 0)
    def _():
        run_ref[...] = jnp.zeros_like(run_ref)

    half = oa_ref.shape[-1]
    y = (_dot(oa_ref[...].astype(BF16), wo_ref[0:half, :])
         + _dot_tn(obt_ref[...].astype(BF16), wo_ref[half:, :]))
    x1 = x_ref[...] + y
    x1_ref[...] = x1
    hn = x1 * lax.rsqrt(jnp.mean(x1 * x1, axis=-1, keepdims=True) + EPS) * nw_ref[...]
    d_half = hn.shape[-1] // 2
    _store_packed_rows(h_ref, _pack_bf16_pair(hn[:, :d_half], hn[:, d_half:]))
    lg = _dot_split_nt(_split_bf16(rw_ref[...]), _split_bf16(hn)) + rb_ref[...]
    expert = lax.broadcasted_iota(jnp.int32, lg.shape, 0)
    sels, tops, idxs = [], [], []
    for _ in range(TOP_K):
        m = jnp.max(lg, axis=0, keepdims=True)
        idx = jnp.min(jnp.where(lg == m, expert, N_EXPERTS), axis=0, keepdims=True)
        sel = expert == idx
        lg = jnp.where(sel, -jnp.inf, lg)
        sels.append(sel)
        tops.append(m)
        idxs.append(idx)
    exps = [jnp.exp(tv - tops[0]) for tv in tops]
    den = exps[0] + exps[1] + exps[2] + exps[3]
    multi = sels[0].astype(F32)
    for sel in sels[1:]:
        multi = multi + sel.astype(F32)
    ri = lax.broadcasted_iota(jnp.int32, (tm, tm), 0)
    ci = lax.broadcasted_iota(jnp.int32, (tm, tm), 1)
    before = _dot(multi.astype(BF16), (ri < ci).astype(BF16)) + run_ref[...]
    ranks = [jnp.sum(jnp.where(sel, before, 0.0), axis=0, keepdims=True).astype(jnp.int32) for sel in sels]
    run = run_ref[...] + jnp.sum(multi, axis=1, keepdims=True)
    run_ref[...] = run
    cnt_ref[...] = jnp.broadcast_to(run, cnt_ref.shape)
    pad_i = jnp.zeros((8 - TOP_K, tm), jnp.int32)
    eidx_ref[...] = jnp.concatenate(idxs + [pad_i], axis=0)
    rank_ref[...] = jnp.concatenate(ranks + [pad_i], axis=0)
    gates_t = jnp.concatenate([e / den for e in exps] + [jnp.zeros((LANES - TOP_K, tm), F32)], axis=0)
    eye = (ri == ci).astype(BF16)
    g_hi, g_lo = _split_bf16(gates_t)
    gate_ref[...] = _dot_nt(eye, g_hi) + _dot_nt(eye, g_lo)


def _router(x2, oa, obt, w_out, norm_w, rw, rb, tm):
    t, d = x2.shape
    half = oa.shape[-1]
    s_words = d // 2 // LANES
    row = lambda i: (i, 0)
    col = lambda i: (0, i)
    const = lambda i: (0, 0)
    return pl.pallas_call(
        functools.partial(_router_kernel, tm=tm),
        grid=(t // tm,),
        in_specs=[pl.BlockSpec((tm, d), row),
                  pl.BlockSpec((tm, half), row),
                  pl.BlockSpec((half, tm), lambda i: (0, i)),
                  pl.BlockSpec((2 * half, d), const),
                  pl.BlockSpec((1, d), const),
                  pl.BlockSpec((N_EXPERTS, d), const),
                  pl.BlockSpec((N_EXPERTS, 1), const)],
        out_specs=[pl.BlockSpec((tm, d), row), pl.BlockSpec((tm * s_words, LANES), row),
                   pl.BlockSpec((8, tm), col), pl.BlockSpec((tm, LANES), row),
                   pl.BlockSpec((8, tm), col), pl.BlockSpec((N_EXPERTS, LANES), const)],
        out_shape=[jax.ShapeDtypeStruct((t, d), F32), jax.ShapeDtypeStruct((t * s_words, LANES), jnp.int32),
                   jax.ShapeDtypeStruct((8, t), jnp.int32), jax.ShapeDtypeStruct((t, LANES), F32),
                   jax.ShapeDtypeStruct((8, t), jnp.int32), jax.ShapeDtypeStruct((N_EXPERTS, LANES), F32)],
        scratch_shapes=[pltpu.VMEM((N_EXPERTS, 1), F32)],
        compiler_params=pltpu.CompilerParams(dimension_semantics=("arbitrary",),
                                             vmem_limit_bytes=VMEM_LIMIT),
        name="router",
    )(x2, oa, obt, w_out, norm_w, rw, rb)


SC_ROWS = 64


def _sc_mesh():
    return plsc.VectorSubcoreMesh(core_axis_name="c", subcore_axis_name="s")


def _dispatch(dest_flat, h3, n_pad):
    t, s, _ = h3.shape
    mesh = _sc_mesh()
    n_workers = mesh.num_cores * mesh.num_subcores
    per_worker = t // n_workers
    assert per_worker % SC_ROWS == 0

    @functools.partial(
        pl.kernel, mesh=mesh, out_type=jax.ShapeDtypeStruct((n_pad, s, LANES), h3.dtype),
        scratch_types=[pltpu.VMEM((TOP_K, SC_ROWS), jnp.int32), pltpu.VMEM((SC_ROWS, s, LANES), h3.dtype),
                       pltpu.SemaphoreType.DMA],
        name="sc_dispatch")
    def scatter(h_hbm, dest_hbm, out_hbm, idx_v, rows_v, sem):
        base = (lax.axis_index("s") * mesh.num_cores + lax.axis_index("c")) * per_worker

        @pl.loop(0, per_worker // SC_ROWS)
        def _(i):
            off = pl.multiple_of(base + i * SC_ROWS, SC_ROWS)
            pltpu.sync_copy(h_hbm.at[pl.ds(off, SC_ROWS)], rows_v)
            for kk in range(TOP_K):
                pltpu.sync_copy(dest_hbm.at[pl.ds(kk * t + off, SC_ROWS)], idx_v.at[kk])
            copies = [pltpu.async_copy(rows_v, out_hbm.at[idx_v.at[kk]], sem) for kk in range(TOP_K)]
            for cp in copies:
                cp.wait()

    return scatter(h3, dest_flat)


def _expert_kernel(cnt_ref, blk0_ref, nblk_ref, total_ref, x_hbm, w1_ref, b1g_ref, b1l_ref, w2_ref, b2_ref,
                   y_hbm, w1s_ref, w2s_ref, xbuf, ybuf, xsem, ysem, *, bm):
    e = pl.program_id(0)
    n_blocks = nblk_ref[e]
    first = blk0_ref[e]
    total = total_ref[0]
    rows_per_block = xbuf.shape[1]
    s_words = rows_per_block // bm
    n_tiles = w1s_ref.shape[-1] // MXU_COLS
    half = MXU_COLS // 2

    def x_copy(g, slot):
        return pltpu.make_async_copy(x_hbm.at[pl.ds(pl.multiple_of(g * rows_per_block, rows_per_block),
                                                    rows_per_block)], xbuf.at[slot], xsem.at[slot])

    def y_copy(g, slot):
        return pltpu.make_async_copy(ybuf.at[slot], y_hbm.at[pl.ds(pl.multiple_of(g * rows_per_block,
                                                                                   rows_per_block),
                                                                   rows_per_block)], ysem.at[slot])

    @pl.when(e == 0)
    def _():
        x_copy(0, 0).start()

    @pl.when(n_blocks > 0)
    def _():
        r = lax.broadcasted_iota(jnp.int32, (MXU_COLS, MXU_COLS), 0)
        c = lax.broadcasted_iota(jnp.int32, (MXU_COLS, MXU_COLS), 1)
        perm = (r == jnp.where(c < half, 2 * c, 2 * (c - half) + 1)).astype(BF16)
        for tix in range(n_tiles):
            cols = slice(tix * MXU_COLS, (tix + 1) * MXU_COLS)
            w1s_ref[:, cols] = _dot(w1_ref[0, :, cols].astype(BF16), perm).astype(BF16)
        w2s_ref[...] = w2_ref[0].astype(BF16)

        def block(j, carry):
            g = first + j
            slot = lax.rem(g, 2)
            x_copy(g, slot).wait()

            @pl.when(g + 1 < total)
            def _():
                x_copy(g + 1, 1 - slot).start()

            @pl.when(g >= 2)
            def _():
                y_copy(g - 2, slot).wait()

            n_rows = cnt_ref[e] - j * bm

            def ffn(m):
                words = _load_packed_rows(xbuf, (slot,), m, s_words)
                row = lax.broadcasted_iota(jnp.int32, words.shape, 0)
                x_lo, x_hi = _unpack_bf16_pair(jnp.where(row < n_rows, words, 0))
                x = jnp.concatenate([x_lo.astype(BF16), x_hi.astype(BF16)], axis=1)
                acts = []
                for tix in range(n_tiles):
                    hp = _dot(x, w1s_ref[:, tix * MXU_COLS:(tix + 1) * MXU_COLS])
                    feat = slice(tix * half, (tix + 1) * half)
                    gl = jnp.minimum(hp[:, :half] + b1g_ref[0][:, feat], SWIGLU_LIMIT)
                    lin = jnp.clip(hp[:, half:] + b1l_ref[0][:, feat], -SWIGLU_LIMIT, SWIGLU_LIMIT)
                    acts.append((gl * _sigmoid(SWIGLU_ALPHA * gl) * (lin + 1.0)).astype(BF16))
                y = _dot(jnp.concatenate(acts, axis=1), w2s_ref[...]) + b2_ref[0]
                d_half = y.shape[-1] // 2
                _store_packed_rows(ybuf.at[slot], _pack_bf16_pair(y[:, :d_half], y[:, d_half:]))

            @pl.when(n_rows > bm // 2)
            def _():
                ffn(bm)

            @pl.when(n_rows <= bm // 2)
            def _():
                ffn(bm // 2)

            y_copy(g, slot).start()
            return carry

        lax.fori_loop(0, n_blocks, block, 0)

    @pl.when(e == pl.num_programs(0) - 1)
    def _():
        @pl.when(total >= 2)
        def _():
            y_copy(total - 2, lax.rem(total, 2)).wait()

        y_copy(total - 1, lax.rem(total - 1, 2)).wait()


def _experts(tables, xs, w1, b1g, b1l, w2, b2, bm, d):
    s_words = d // 2 // LANES
    dff2 = w1.shape[-1]
    dff = dff2 // 2
    emap = lambda e, *_: (e, 0, 0)
    grid_spec = pltpu.PrefetchScalarGridSpec(
        num_scalar_prefetch=len(tables),
        grid=(N_EXPERTS,),
        in_specs=[pl.BlockSpec(memory_space=pl.ANY),
                  pl.BlockSpec((1, d, dff2), emap),
                  pl.BlockSpec((1, 1, dff), emap),
                  pl.BlockSpec((1, 1, dff), emap),
                  pl.BlockSpec((1, dff, d), emap),
                  pl.BlockSpec((1, 1, d), emap)],
        out_specs=pl.BlockSpec(memory_space=pl.ANY),
        scratch_shapes=[pltpu.VMEM((d, dff2), BF16), pltpu.VMEM((dff, d), BF16),
                        pltpu.VMEM((2, bm * s_words, LANES), jnp.int32),
                        pltpu.VMEM((2, bm * s_words, LANES), jnp.int32),
                        pltpu.SemaphoreType.DMA((2,)), pltpu.SemaphoreType.DMA((2,))],
    )
    return pl.pallas_call(
        functools.partial(_expert_kernel, bm=bm),
        grid_spec=grid_spec,
        out_shape=jax.ShapeDtypeStruct(xs.shape, jnp.int32),
        compiler_params=pltpu.CompilerParams(dimension_semantics=("arbitrary",),
                                             vmem_limit_bytes=VMEM_LIMIT),
        name="experts",
    )(*tables, xs, w1, b1g, b1l, w2, b2)


def _gather_rows(dest_flat, y3):
    n_out = dest_flat.shape[0]
    _, s, _ = y3.shape
    mesh = _sc_mesh()
    n_workers = mesh.num_cores * mesh.num_subcores
    per_worker = n_out // n_workers
    n_chunks = per_worker // SC_ROWS
    assert per_worker % SC_ROWS == 0 and n_chunks % 2 == 0

    @functools.partial(
        pl.kernel, mesh=mesh, out_type=jax.ShapeDtypeStruct((n_out, s, LANES), y3.dtype),
        scratch_types=[pltpu.VMEM((2, SC_ROWS), jnp.int32), pltpu.VMEM((2, SC_ROWS, s, LANES), y3.dtype),
                       pltpu.SemaphoreType.DMA((2,)), pltpu.SemaphoreType.DMA((2,))],
        name="sc_gather")
    def gather(y_hbm, dest_hbm, out_hbm, idx_v, rows_v, gsem, wsem):
        base = (lax.axis_index("s") * mesh.num_cores + lax.axis_index("c")) * per_worker

        def rows_of(chunk):
            return pl.ds(pl.multiple_of(base + chunk * SC_ROWS, SC_ROWS), SC_ROWS)

        def gather_copy(slot):
            return pltpu.make_async_copy(y_hbm.at[idx_v.at[slot]], rows_v.at[slot], gsem.at[slot])

        def write_copy(chunk, slot):
            return pltpu.make_async_copy(rows_v.at[slot], out_hbm.at[rows_of(chunk)], wsem.at[slot])

        def start_gather(chunk, slot):
            pltpu.sync_copy(dest_hbm.at[rows_of(chunk)], idx_v.at[slot])
            gather_copy(slot).start()

        start_gather(0, 0)

        @pl.loop(0, n_chunks // 2)
        def _(pair):
            for slot in (0, 1):
                chunk = 2 * pair + slot
                gather_copy(slot).wait()
                write_copy(chunk, slot).start()

                @pl.when(chunk >= 1)
                def _():
                    write_copy(chunk - 1, 1 - slot).wait()

                @pl.when(chunk + 1 < n_chunks)
                def _():
                    start_gather(chunk + 1, 1 - slot)

        write_copy(n_chunks - 1, 1).wait()

    return gather(y3, dest_flat)


def _combine_kernel(x1_ref, gate_ref, y4_ref, nw_ref, o_ref, *, tm, normalize):
    x = x1_ref[...]
    gates = gate_ref[...]
    for kk in range(TOP_K):
        lo, hi = _unpack_bf16_pair(_load_packed_rows(y4_ref, (kk,), tm, y4_ref.shape[1] // tm))
        x = x + gates[:, kk:kk + 1] * jnp.concatenate([lo, hi], axis=1)
    if normalize:
        x = x * lax.rsqrt(jnp.mean(x * x, axis=-1, keepdims=True) + EPS) * nw_ref[...]
    o_ref[...] = x


def _combine(x1, gates, y4, norm_w, tm, normalize):
    t, d = x1.shape
    s_words = d // 2 // LANES
    row = lambda i: (i, 0)
    return pl.pallas_call(
        functools.partial(_combine_kernel, tm=tm, normalize=normalize),
        grid=(t // tm,),
        in_specs=[pl.BlockSpec((tm, d), row), pl.BlockSpec((tm, LANES), row),
                  pl.BlockSpec((TOP_K, tm * s_words, LANES), lambda i: (0, i, 0)),
                  pl.BlockSpec((1, d), lambda i: (0, 0))],
        out_specs=pl.BlockSpec((tm, d), row),
        out_shape=jax.ShapeDtypeStruct((t, d), F32),
        compiler_params=pltpu.CompilerParams(dimension_semantics=("parallel",)),
        name="combine",
    )(x1, gates, y4, norm_w)


def _tiles(t, seq):
    tm = min(2 * MXU_COLS, t)
    tr = min(2 * MXU_COLS, t)
    tq = min(2 * MXU_COLS, seq)
    gdn_rows = min(4 * CHUNK, seq)
    bm = 2 * MXU_COLS
    assert t % tm == 0 and t % tr == 0 and seq % tq == 0 and seq % gdn_rows == 0 and tq % CHUNK == 0
    return tm, tr, tq, gdn_rows, bm


def _rot_half(w):
    half = w.shape[-1] // 2
    return jnp.concatenate([-w[..., half:], w[..., :half]], axis=-1)


def _prep_mla_weights(w_uq, w_ukv):
    dq = MLA_NOPE + MLA_ROPE
    wq = w_uq.reshape(MLA_Q_LORA, MLA_HEADS, dq)
    wq_rot = jnp.concatenate([jnp.zeros_like(wq[..., :MLA_NOPE]), _rot_half(wq[..., MLA_NOPE:])], axis=-1)
    pad = lambda w: jnp.pad(w, ((0, 0), (0, 0), (0, LANES - w.shape[-1]))).reshape(w.shape[0], C_HEADS)
    wkv = w_ukv.reshape(MLA_KV_LORA, MLA_HEADS, MLA_NOPE + MLA_V)
    wv = wkv[..., MLA_NOPE:].reshape(MLA_KV_LORA, C_V)
    return (pad(wq).T.astype(BF16), pad(wq_rot).T.astype(BF16), pad(wkv[..., :MLA_NOPE]).astype(BF16),
            wv.T.astype(BF16))


def _rope_freq():
    half = MLA_ROPE // 2
    return (ROPE_THETA ** (-jnp.arange(half, dtype=F32) / half)).reshape(half, 1)


def _lane_row(v):
    return jnp.pad(v.astype(F32), (0, LANES - v.shape[0])).reshape(1, LANES)


def _routing_tables(cnt, eidx, rank, bm):
    e_ids = jnp.arange(N_EXPERTS, dtype=jnp.int32)
    counts = cnt[:, 0].astype(jnp.int32)
    padded = (counts + bm - 1) // bm * bm
    pend = jnp.sum(jnp.where(e_ids[None, :] <= e_ids[:, None], padded[None, :], 0), axis=1)
    pstart = pend - padded
    dest = jnp.sum(jnp.where(eidx[:TOP_K, :, None] == e_ids, pstart, 0), axis=-1) + rank[:TOP_K]
    tables = (counts, pstart // bm, padded // bm, pend[-1:] // bm)
    return dest.reshape(-1), tuple(t.astype(jnp.int32) for t in tables)


def kernel(x, positions, norm_mix_w, w_in, conv_w, a_log, dt_bias, gdn_norm_w, q_norm_w, w_uq, kv_norm_w,
           w_ukv, mla_out_norm_w, w_out, norm_ffn_w, router_w, router_b, w1, b1, w2, b2, norm_final_w):
    bsz, seq, d = x.shape
    t = bsz * seq
    depth = w_in.shape[0]
    tm, tr, tq, gdn_rows, bm = _tiles(t, seq)
    x2 = x.reshape(t, d)
    posf = positions.astype(F32)
    freq = _rope_freq()
    for l in range(depth):
        qkv, z, cq, ckv, kr2, ab = _inproj(x2, norm_mix_w[l].reshape(1, d), w_in[l], tm)
        conv_w8 = jnp.pad(conv_w[l], ((0, 8 - GDN_CONV), (0, 0)))
        o_a = _gdn(qkv.reshape(bsz, seq, C_QKV), z.reshape(bsz, seq, C_Z), ab.reshape(bsz, seq, LANES),
                   conv_w8, _lane_row(a_log[l]), _lane_row(dt_bias[l]), gdn_norm_w[l].reshape(1, GDN_D),
                   gdn_rows).reshape(t, C_Z)
        wqt, wqrt, wk, wvt = _prep_mla_weights(w_uq[l], w_ukv[l])
        qt, k, vt = _mla_proj(cq, ckv, kr2, posf.reshape(1, t),
                              q_norm_w[l].reshape(1, -1), kv_norm_w[l].reshape(1, -1),
                              wqt, wqrt, wk, wvt, freq, tq)
        o_bt = _attention(qt, k, vt, mla_out_norm_w[l].reshape(-1, 1), bsz, seq, tq)
        x1, hn, eidx, gates, rank, cnt = _router(x2, o_a, o_bt, w_out[l].astype(BF16),
                                                 norm_ffn_w[l].reshape(1, d), router_w[l].T,
                                                 router_b[l].reshape(N_EXPERTS, 1), tr)
        n_pad = t * TOP_K + N_EXPERTS * bm
        dest_flat, tables = _routing_tables(cnt, eidx, rank, bm)
        s_words = d // 2 // LANES
        buf = _dispatch(dest_flat, hn.reshape(t, s_words, LANES), n_pad)
        yb = _experts(tables, buf.reshape(n_pad * s_words, LANES), w1[l],
                      b1[l][:, None, 0::2], b1[l][:, None, 1::2], w2[l], b2[l][:, None, :], bm, d)
        y4 = _gather_rows(dest_flat, yb.reshape(n_pad, s_words, LANES))
        x2 = _combine(x1, gates, y4.reshape(TOP_K, t * s_words, LANES), norm_final_w.reshape(1, d), tr,
                      normalize=(l == depth - 1))
    return x2.reshape(bsz, seq, d)
```

```python
import functools
import math

import jax
import jax.numpy as jnp
from jax import lax
from jax.experimental import pallas as pl
from jax.experimental.pallas import tpu as pltpu
from jax.experimental.pallas import tpu_sc as plsc

F32 = jnp.float32
BF16 = jnp.bfloat16
HIGHEST = lax.Precision.HIGHEST

LANES = 128
MXU_COLS = 256
EPS = 1e-6
CHUNK = 64

GDN_HEADS = 4
GDN_D = 128
GDN_CONV = 4
MLA_HEADS = 8
MLA_NOPE = 64
MLA_ROPE = 32
MLA_V = 64
MLA_Q_LORA = 384
MLA_KV_LORA = 256
ROPE_THETA = 10000.0
N_EXPERTS = 32
TOP_K = 4
SWIGLU_LIMIT = 7.0
SWIGLU_ALPHA = 1.702

NEG_BIG = -1e30
VMEM_LIMIT = 52 * 1024 * 1024


def _sigmoid(x):
    return 1.0 / (1.0 + jnp.exp(-x))


def _softplus(x):
    return jnp.maximum(x, 0.0) + jnp.log(1.0 + jnp.exp(-jnp.abs(x)))


def _dot(a, b, precision=None):
    return jnp.dot(a, b, preferred_element_type=F32, precision=precision)


def _dot_nt(a, b, precision=None):
    return lax.dot_general(a, b, (((1,), (1,)), ((), ())), preferred_element_type=F32,
                           precision=precision)


def _dot_tn(a, b):
    return lax.dot_general(a, b, (((0,), (0,)), ((), ())), preferred_element_type=F32)


def _split_bf16(a):
    hi = a.astype(BF16)
    return hi, (a - hi.astype(F32)).astype(BF16)


def _pack_bf16_pair(lo, hi):
    lo_bits = pltpu.bitcast(lo.astype(BF16).astype(F32), jnp.int32)
    hi_bits = pltpu.bitcast(hi.astype(BF16).astype(F32), jnp.int32)
    return jnp.bitwise_or(hi_bits, lax.shift_right_logical(lo_bits, 16))


def _unpack_bf16_pair(w):
    lo = pltpu.bitcast(lax.shift_left(w, 16), F32)
    hi = pltpu.bitcast(jnp.bitwise_and(w, -65536), F32)
    return lo, hi


def _load_packed_rows(ref, lead, m, s):
    return jnp.concatenate([ref[(*lead, pl.ds(j, m, stride=s), slice(None))] for j in range(s)], axis=1)


def _store_packed_rows(ref, words):
    m = words.shape[0]
    s = words.shape[1] // LANES
    for j in range(s):
        ref[pl.ds(j, m, stride=s), :] = words[:, j * LANES:(j + 1) * LANES]


def _dot_split_nt(a, b):
    return _dot_nt(a[0], b[0]) + _dot_nt(a[0], b[1]) + _dot_nt(a[1], b[0])


def _dot_split(a, b):
    return _dot(a[0], b[0]) + _dot(a[0], b[1]) + _dot(a[1], b[0])


C_QKV = 3 * GDN_HEADS * GDN_D
C_Z = GDN_HEADS * GDN_D
C_IN = C_QKV + C_Z + MLA_Q_LORA + MLA_KV_LORA + 3 * LANES


def _inproj_kernel(x_ref, nw_ref, w_ref, qkv_ref, z_ref, cq_ref, ckv_ref, kr_ref, ab_ref, wb_ref):
    @pl.when(pl.program_id(0) == 0)
    def _():
        d = w_ref.shape[0]
        hd4 = C_QKV + C_Z
        wb_ref[:, 0:hd4] = w_ref[:, 0:hd4].astype(BF16)
        tail = w_ref[:, hd4:]
        o_cq = 2 * GDN_HEADS
        o_ckv = o_cq + MLA_Q_LORA
        o_kr = o_ckv + MLA_KV_LORA
        kr = tail[:, o_kr:o_kr + MLA_ROPE]
        half = MLA_ROPE // 2
        zeros = lambda n: jnp.zeros((d, n), F32)
        pad = LANES - MLA_NOPE - MLA_ROPE
        groups = [tail[:, o_cq:o_ckv], tail[:, o_ckv:o_kr],
                  jnp.concatenate([zeros(MLA_NOPE), kr, zeros(pad)], axis=1),
                  jnp.concatenate([zeros(MLA_NOPE), -kr[:, half:], kr[:, :half], zeros(pad)], axis=1),
                  jnp.concatenate([tail[:, 0:o_cq], zeros(LANES - o_cq)], axis=1)]
        o = hd4
        for grp in groups:
            wb_ref[:, o:o + grp.shape[1]] = grp.astype(BF16)
            o += grp.shape[1]

    x = x_ref[...]
    var = jnp.mean(x * x, axis=-1, keepdims=True)
    h = (x * lax.rsqrt(var + EPS) * nw_ref[...]).astype(BF16)
    p = _dot(h, wb_ref[...])
    o = 0
    for ref in (qkv_ref, z_ref, cq_ref, ckv_ref, kr_ref, ab_ref):
        n = ref.shape[-1]
        ref[...] = p[:, o:o + n]
        o += n


def _inproj(x2, norm_w, w_in, tm):
    t, d = x2.shape
    widths = (C_QKV, C_Z, MLA_Q_LORA, MLA_KV_LORA, 2 * LANES, LANES)
    return pl.pallas_call(
        _inproj_kernel,
        grid=(t // tm,),
        in_specs=[pl.BlockSpec((tm, d), lambda i: (i, 0)),
                  pl.BlockSpec((1, d), lambda i: (0, 0)),
                  pl.BlockSpec(w_in.shape, lambda i: (0, 0), pipeline_mode=pl.Buffered(1))],
        out_specs=[pl.BlockSpec((tm, n), lambda i: (i, 0)) for n in widths],
        out_shape=[jax.ShapeDtypeStruct((t, n), F32) for n in widths],
        scratch_shapes=[pltpu.VMEM((d, C_IN), BF16)],
        compiler_params=pltpu.CompilerParams(dimension_semantics=("arbitrary",),
                                             vmem_limit_bytes=VMEM_LIMIT),
        name="inproj",
    )(x2, norm_w, w_in)


def _gdn_kernel(qkv_ref, z_ref, ab_ref, cw_ref, alog_ref, dtb_ref, nw_ref, o_ref, ext_ref, s_ref):
    c = CHUNK
    bsz, rows, _ = qkv_ref.shape
    chunks_per_batch = rows // c

    @pl.when(pl.program_id(0) == 0)
    def _():
        ext_ref[:, 0:8, :] = jnp.zeros((bsz, 8, C_QKV), F32)
        s_ref[...] = jnp.zeros_like(s_ref)

    cw = cw_ref[...]
    nw = nw_ref[...]
    rr = lax.broadcasted_iota(jnp.int32, (rows, rows), 0)
    rc = lax.broadcasted_iota(jnp.int32, (rows, rows), 1)
    in_chunk_prefix = jnp.where(rr // c == rc // c, jnp.where(rr >= rc, 1.0, 0.0), 0.0)
    e8 = (lax.broadcasted_iota(jnp.int32, (8, LANES), 0)
          == lax.broadcasted_iota(jnp.int32, (8, LANES), 1)).astype(F32)
    ri = lax.broadcasted_iota(jnp.int32, (c, c), 0)
    ci = lax.broadcasted_iota(jnp.int32, (c, c), 1)
    causal = ri >= ci
    strict = ri > ci
    eye = (ri == ci).astype(F32)
    chains = [(j, h) for j in range(chunks_per_batch) for h in range(GDN_HEADS)]

    def prep_pieces(b):
        g = {"xc": [None] * (C_QKV // LANES)}

        def conv(tiles):
            def run():
                cols = slice(tiles[0] * LANES, (tiles[-1] + 1) * LANES)
                x = qkv_ref[b, :, cols]
                ext_ref[b, 8:8 + rows, cols] = x
                ext = ext_ref[b, :, cols]
                xb = cw[3:4, cols] * x
                for j in range(1, GDN_CONV):
                    xb = xb + cw[3 - j:4 - j, cols] * pltpu.roll(ext, j, axis=0)[8:8 + rows, :]
                ext_ref[b, 0:8, cols] = x[rows - 8:rows, :]
                xb = xb * _sigmoid(xb)
                for i, tile in enumerate(tiles):
                    g["xc"][tile] = xb[:, i * LANES:(i + 1) * LANES]
            return run

        def gates():
            ab = ab_ref[b]
            g_all = -jnp.exp(alog_ref[...]) * _softplus(ab + dtb_ref[...])
            g["beta"] = _sigmoid(ab)
            g["g_cum"] = _dot(in_chunk_prefix, g_all, HIGHEST)
            g["g_row"] = _dot_nt(e8, g["g_cum"], HIGHEST)
            for name in ("q", "k", "v", "gc", "bt", "dec"):
                g[name] = []

        def chunk(j):
            def run():
                rs = slice(j * c, (j + 1) * c)
                xc = g["xc"]
                for h in range(GDN_HEADS):
                    q = xc[h][rs]
                    k = xc[GDN_HEADS + h][rs]
                    g["q"].append(q * lax.rsqrt(jnp.sum(q * q, axis=-1, keepdims=True) + EPS) * (GDN_D ** -0.5))
                    g["k"].append(k * lax.rsqrt(jnp.sum(k * k, axis=-1, keepdims=True) + EPS))
                    g["v"].append(xc[2 * GDN_HEADS + h][rs])
                    gc = g["g_cum"][rs, h:h + 1]
                    gr = g["g_row"][h:h + 1, rs]
                    g["gc"].append(gc)
                    g["bt"].append(g["beta"][rs, GDN_HEADS + h:GDN_HEADS + h + 1])
                    g["dec"].append(jnp.exp(jnp.where(causal, gc - gr, -jnp.inf)))
            return run

        def a_mats():
            g["kb"] = [k * bt for k, bt in zip(g["k"], g["bt"])]
            g["k16"] = [k.astype(BF16) for k in g["k"]]
            g["a"] = [jnp.where(strict, _dot_nt(kb.astype(BF16), k16) * dec, 0.0)
                      for kb, k16, dec in zip(g["kb"], g["k16"], g["dec"])]

        n_tiles = C_QKV // LANES
        conv_pieces = [conv(list(range(i, i + 2))) for i in range(0, n_tiles, 2)]
        return g, conv_pieces + [gates] + [chunk(j) for j in range(chunks_per_batch)] + [a_mats]

    def inverse_rounds(g):
        def start():
            g["t"] = [eye - a for a in g["a"]]
            g["pw"] = [_split_bf16(a) for a in g["a"]]

        def square():
            g["pw"] = [_split_bf16(_dot_split(p, p)) for p in g["pw"]]

        def update():
            g["t"] = [t + _dot_split(_split_bf16(t), p) for t, p in zip(g["t"], g["pw"])]

        return [start] + [square, update] * 5

    groups = []
    pending = []
    for b in range(bsz):
        g, pieces = prep_pieces(b)
        groups.append(g)
        n_a, n_b = len(pending), len(pieces)
        ia = ib = 0
        while ia < n_a or ib < n_b:
            if ib >= n_b or (ia < n_a and ia * n_b <= ib * n_a):
                pending[ia]()
                ia += 1
            else:
                pieces[ib]()
                ib += 1
        pending = inverse_rounds(g)
    for fn in pending:
        fn()

    uws, qk16s, qd16s, kd16s, g_tots = [], [], [], [], []
    for g in groups:
        e_gs = [jnp.exp(gc) for gc in g["gc"]]
        uws.append([_dot(t.astype(BF16), jnp.concatenate([v * bt, kb * eg], axis=1).astype(BF16))
                    for t, v, bt, kb, eg in zip(g["t"], g["v"], g["bt"], g["kb"], e_gs)])
        qk16s.append([(_dot_nt(q.astype(BF16), k16) * dec).astype(BF16)
                      for q, k16, dec in zip(g["q"], g["k16"], g["dec"])])
        qd16s.append([(q * eg).astype(BF16) for q, eg in zip(g["q"], e_gs)])
        kd16s.append([(k * jnp.exp(gc[c - 1:c, :] - gc)).astype(BF16) for k, gc in zip(g["k"], g["gc"])])
        g_tots.append([jnp.exp(gc[c - 1:c, :]) for gc in g["gc"]])

    pairs = [(b, h) for b in range(bsz) for h in range(GDN_HEADS)]
    states = [s_ref[b * GDN_HEADS + h] for b, h in pairs]
    for j in range(chunks_per_batch):
        rs = slice(j * c, (j + 1) * c)
        pick = lambda per_group: [per_group[b][chains.index((j, h))] for b, h in pairs]
        uw_j, qk_j, qd_j, kd_j, gt_j = (pick(x) for x in (uws, qk16s, qd16s, kd16s, g_tots))
        s16s = [s.astype(BF16) for s in states]
        v16s = [(uw[:, :GDN_D] - _dot(uw[:, GDN_D:].astype(BF16), s16)).astype(BF16)
                for uw, s16 in zip(uw_j, s16s)]
        states = [s * gt + _dot_tn(kd, v16) for s, gt, kd, v16 in zip(states, gt_j, kd_j, v16s)]
        os_ = [_dot(qd, s16) + _dot(qk, v16) for qd, qk, s16, v16 in zip(qd_j, qk_j, s16s, v16s)]
        for (b, h), o in zip(pairs, os_):
            on = o * lax.rsqrt(jnp.mean(o * o, axis=-1, keepdims=True) + EPS) * nw
            zh = z_ref[b, rs, h * GDN_D:(h + 1) * GDN_D]
            o_ref[b, rs, h * GDN_D:(h + 1) * GDN_D] = on * (zh * _sigmoid(zh))
    for p, s in enumerate(states):
        s_ref[p] = s


def _gdn(qkv, z, ab, conv_w8, alog_row, dtb_row, norm_w, rows):
    bsz, seq, _ = qkv.shape
    blk = lambda i: (0, i, 0)
    const = lambda i: (0, 0)
    return pl.pallas_call(
        _gdn_kernel,
        grid=(seq // rows,),
        in_specs=[pl.BlockSpec((bsz, rows, C_QKV), blk),
                  pl.BlockSpec((bsz, rows, C_Z), blk),
                  pl.BlockSpec((bsz, rows, LANES), blk),
                  pl.BlockSpec((8, C_QKV), const),
                  pl.BlockSpec((1, LANES), const),
                  pl.BlockSpec((1, LANES), const),
                  pl.BlockSpec((1, GDN_D), const)],
        out_specs=pl.BlockSpec((bsz, rows, C_Z), blk),
        out_shape=jax.ShapeDtypeStruct((bsz, seq, C_Z), F32),
        scratch_shapes=[pltpu.VMEM((bsz, 8 + rows, C_QKV), F32),
                        pltpu.VMEM((bsz * GDN_HEADS, GDN_D, GDN_D), F32)],
        compiler_params=pltpu.CompilerParams(dimension_semantics=("arbitrary",),
                                             vmem_limit_bytes=VMEM_LIMIT),
        name="gdn",
    )(qkv, z, ab, conv_w8, alog_row, dtb_row, norm_w)


D_QK_PAD = LANES
C_HEADS = MLA_HEADS * D_QK_PAD
C_V = MLA_HEADS * MLA_V
V_AUG = MLA_V + 16
C_VAUG = MLA_HEADS * V_AUG
Q_SCALE = (MLA_NOPE + MLA_ROPE) ** -0.5 * math.log2(math.e)


def _mla_proj_kernel(cq_ref, ckv_ref, kr_ref, posr_ref, qnw_ref, kvnw_ref, wqt_ref, wqrt_ref,
                     wk_ref, wvt_ref, freqc_ref, qt_ref, k_ref, vt_ref):
    cq = cq_ref[...]
    cqn = (cq * lax.rsqrt(jnp.mean(cq * cq, axis=-1, keepdims=True) + EPS) * qnw_ref[...]).astype(BF16)
    ckv = ckv_ref[...]
    ckvn = (ckv * lax.rsqrt(jnp.mean(ckv * ckv, axis=-1, keepdims=True) + EPS)
            * kvnw_ref[...]).astype(BF16)
    tm = posr_ref.shape[-1]
    ang = freqc_ref[...] * posr_ref[...]
    c16 = jnp.cos(ang)
    s16 = jnp.sin(ang)
    pad = LANES - MLA_NOPE - MLA_ROPE
    cs_t = jnp.concatenate([jnp.ones((MLA_NOPE, tm), F32), c16, c16, jnp.ones((pad, tm), F32)], axis=0)
    sn_t = jnp.concatenate([jnp.zeros((MLA_NOPE, tm), F32), s16, s16, jnp.zeros((pad, tm), F32)], axis=0)
    cs = cs_t.T
    sn = sn_t.T
    qa_t = _dot_nt(wqt_ref[...], cqn)
    qb_t = _dot_nt(wqrt_ref[...], cqn)
    kn = _dot(ckvn, wk_ref[...])
    kr = kr_ref[...]
    kpe = kr[:, :LANES] * cs + kr[:, LANES:] * sn
    for h in range(MLA_HEADS):
        sl = slice(h * LANES, (h + 1) * LANES)
        qt_ref[sl, :] = ((qa_t[sl, :] * cs_t + qb_t[sl, :] * sn_t) * Q_SCALE).astype(BF16)
        k_ref[:, sl] = (kn[:, sl] + kpe).astype(BF16)
    v_t = _dot_nt(wvt_ref[...], ckvn).astype(BF16)
    ones = jnp.ones((V_AUG - MLA_V, v_t.shape[1]), BF16)
    for h in range(MLA_HEADS):
        vt_ref[0, h * V_AUG:h * V_AUG + MLA_V, :] = v_t[h * MLA_V:(h + 1) * MLA_V, :]
        vt_ref[0, h * V_AUG + MLA_V:(h + 1) * V_AUG, :] = ones


def _mla_proj(cq, ckv, kr2, posr, qnw, kvnw, wqt, wqrt, wk, wvt, freqc, tm):
    t = cq.shape[0]
    row = lambda i: (i, 0)
    col = lambda i: (0, i)
    const = lambda i: (0, 0)
    return pl.pallas_call(
        _mla_proj_kernel,
        grid=(t // tm,),
        in_specs=[pl.BlockSpec((tm, MLA_Q_LORA), row),
                  pl.BlockSpec((tm, MLA_KV_LORA), row),
                  pl.BlockSpec((tm, 2 * LANES), row),
                  pl.BlockSpec((1, tm), col),
                  pl.BlockSpec((1, MLA_Q_LORA), const),
                  pl.BlockSpec((1, MLA_KV_LORA), const),
                  pl.BlockSpec((C_HEADS, MLA_Q_LORA), const),
                  pl.BlockSpec((C_HEADS, MLA_Q_LORA), const),
                  pl.BlockSpec((MLA_KV_LORA, C_HEADS), const),
                  pl.BlockSpec((C_V, MLA_KV_LORA), const),
                  pl.BlockSpec((MLA_ROPE // 2, 1), const)],
        out_specs=[pl.BlockSpec((C_HEADS, tm), col),
                   pl.BlockSpec((tm, C_HEADS), row),
                   pl.BlockSpec((1, C_VAUG, tm), lambda i: (i, 0, 0))],
        out_shape=[jax.ShapeDtypeStruct((C_HEADS, t), BF16),
                   jax.ShapeDtypeStruct((t, C_HEADS), BF16),
                   jax.ShapeDtypeStruct((t // tm, C_VAUG, tm), BF16)],
        compiler_params=pltpu.CompilerParams(dimension_semantics=("parallel",),
                                             vmem_limit_bytes=VMEM_LIMIT),
        name="mla_proj",
    )(cq, ckv, kr2, posr, qnw, kvnw, wqt, wqrt, wk, wvt, freqc)


def _attn_kernel(qt_ref, k_ref, vt_ref, nw_ref, ot_ref, m_ref, acc_ref, *, tq):
    i = pl.program_id(1)
    m_ref[...] = jnp.full(m_ref.shape, NEG_BIG, F32)
    acc_ref[...] = jnp.zeros(acc_ref.shape, F32)
    key_chunk = lax.broadcasted_iota(jnp.int32, (tq, tq), 0) // CHUNK
    qry_chunk = lax.broadcasted_iota(jnp.int32, (tq, tq), 1) // CHUNK
    diag_mask = key_chunk <= qry_chunk

    def tile(j, masked):
        rows = pl.ds(pl.multiple_of(j * tq, tq), tq)

        scores = []
        for h in range(MLA_HEADS):
            hs = slice(h * LANES, (h + 1) * LANES)
            s_t = _dot(k_ref[rows, hs], qt_ref[hs, :])
            scores.append(jnp.where(diag_mask, s_t, NEG_BIG) if masked else s_t)
        probs, alphas = [], []
        for h in range(MLA_HEADS):
            m_old = m_ref[h]
            m_new = jnp.maximum(m_old, jnp.max(scores[h], axis=0, keepdims=True))
            m_ref[h] = m_new
            probs.append(jnp.exp2(scores[h] - m_new[0:1, :]).astype(BF16))
            alphas.append(jnp.exp2(m_old[0:1, :] - m_new[0:1, :]))
        for h in range(MLA_HEADS):
            v_aug = vt_ref[j, h * V_AUG:(h + 1) * V_AUG, :]
            acc_ref[h] = alphas[h] * acc_ref[h] + _dot(v_aug, probs[h])

    def full_tile(j, carry):
        tile(j, False)
        return carry

    lax.fori_loop(0, i, full_tile, 0)
    tile(i, True)

    outs = []
    for h in range(MLA_HEADS):
        acc = acc_ref[h]
        outs.append(acc[:MLA_V, :] / acc[MLA_V:MLA_V + 1, :])
    ssq = outs[0] * outs[0]
    for o in outs[1:]:
        ssq = ssq + o * o
    inv = lax.rsqrt(jnp.sum(ssq, axis=0, keepdims=True) / C_V + EPS)
    for h in range(MLA_HEADS):
        vs = slice(h * MLA_V, (h + 1) * MLA_V)
        ot_ref[vs, :] = outs[h] * inv * nw_ref[vs, :]


def _attention(qt, k, vt, norm_w_col, bsz, seq, tq):
    nq = seq // tq
    return pl.pallas_call(
        functools.partial(_attn_kernel, tq=tq),
        grid=(bsz, nq),
        in_specs=[pl.BlockSpec((C_HEADS, tq), lambda b, i: (0, b * nq + i)),
                  pl.BlockSpec((seq, C_HEADS), lambda b, i: (b, 0)),
                  pl.BlockSpec((nq, C_VAUG, tq), lambda b, i: (b, 0, 0)),
                  pl.BlockSpec((C_V, 1), lambda b, i: (0, 0))],
        out_specs=pl.BlockSpec((C_V, tq), lambda b, i: (0, b * nq + i)),
        out_shape=jax.ShapeDtypeStruct((C_V, bsz * seq), F32),
        scratch_shapes=[pltpu.VMEM((MLA_HEADS, 8, tq), F32),
                        pltpu.VMEM((MLA_HEADS, V_AUG, tq), F32)],
        compiler_params=pltpu.CompilerParams(dimension_semantics=("parallel", "parallel"),
                                             vmem_limit_bytes=VMEM_LIMIT),
        name="attention",
    )(qt, k, vt, norm_w_col)


def _router_kernel(x_ref, oa_ref, obt_ref, wo_ref, nw_ref, rw_ref, rb_ref,
                   x1_ref, h_ref, eidx_ref, gate_ref, rank_ref, cnt_ref, run_ref, *, tm):
    @pl.when(pl.program_id(0) == 0)
    def _():
        run_ref[...] = jnp.zeros_like(run_ref)

    half = oa_ref.shape[-1]
    y = (_dot(oa_ref[...].astype(BF16), wo_ref[0:half, :])
         + _dot_tn(obt_ref[...].astype(BF16), wo_ref[half:, :]))
    x1 = x_ref[...] + y
    x1_ref[...] = x1
    hn = x1 * lax.rsqrt(jnp.mean(x1 * x1, axis=-1, keepdims=True) + EPS) * nw_ref[...]
    d_half = hn.shape[-1] // 2
    _store_packed_rows(h_ref, _pack_bf16_pair(hn[:, :d_half], hn[:, d_half:]))
    lg = _dot_split_nt(_split_bf16(rw_ref[...]), _split_bf16(hn)) + rb_ref[...]
    expert = lax.broadcasted_iota(jnp.int32, lg.shape, 0)
    sels, tops, idxs = [], [], []
    for _ in range(TOP_K):
        m = jnp.max(lg, axis=0, keepdims=True)
        idx = jnp.min(jnp.where(lg == m, expert, N_EXPERTS), axis=0, keepdims=True)
        sel = expert == idx
        lg = jnp.where(sel, -jnp.inf, lg)
        sels.append(sel)
        tops.append(m)
        idxs.append(idx)
    exps = [jnp.exp(tv - tops[0]) for tv in tops]
    den = exps[0] + exps[1] + exps[2] + exps[3]
    multi = sels[0].astype(F32)
    for sel in sels[1:]:
        multi = multi + sel.astype(F32)
    ri = lax.broadcasted_iota(jnp.int32, (tm, tm), 0)
    ci = lax.broadcasted_iota(jnp.int32, (tm, tm), 1)
    before = _dot(multi.astype(BF16), (ri < ci).astype(BF16)) + run_ref[...]
    ranks = [jnp.sum(jnp.where(sel, before, 0.0), axis=0, keepdims=True).astype(jnp.int32) for sel in sels]
    run = run_ref[...] + jnp.sum(multi, axis=1, keepdims=True)
    run_ref[...] = run
    cnt_ref[...] = jnp.broadcast_to(run, cnt_ref.shape)
    pad_i = jnp.zeros((8 - TOP_K, tm), jnp.int32)
    eidx_ref[...] = jnp.concatenate(idxs + [pad_i], axis=0)
    rank_ref[...] = jnp.concatenate(ranks + [pad_i], axis=0)
    gates_t = jnp.concatenate([e / den for e in exps] + [jnp.zeros((LANES - TOP_K, tm), F32)], axis=0)
    gate_ref[...] = gates_t.T


def _router(x2, oa, obt, w_out, norm_w, rw, rb, tm):
    t, d = x2.shape
    half = oa.shape[-1]
    s_words = d // 2 // LANES
    row = lambda i: (i, 0)
    col = lambda i: (0, i)
    const = lambda i: (0, 0)
    return pl.pallas_call(
        functools.partial(_router_kernel, tm=tm),
        grid=(t // tm,),
        in_specs=[pl.BlockSpec((tm, d), row),
                  pl.BlockSpec((tm, half), row),
                  pl.BlockSpec((half, tm), lambda i: (0, i)),
                  pl.BlockSpec((2 * half, d), const),
                  pl.BlockSpec((1, d), const),
                  pl.BlockSpec((N_EXPERTS, d), const),
                  pl.BlockSpec((N_EXPERTS, 1), const)],
        out_specs=[pl.BlockSpec((tm, d), row), pl.BlockSpec((tm * s_words, LANES), row),
                   pl.BlockSpec((8, tm), col), pl.BlockSpec((tm, LANES), row),
                   pl.BlockSpec((8, tm), col), pl.BlockSpec((N_EXPERTS, LANES), const)],
        out_shape=[jax.ShapeDtypeStruct((t, d), F32), jax.ShapeDtypeStruct((t * s_words, LANES), jnp.int32),
                   jax.ShapeDtypeStruct((8, t), jnp.int32), jax.ShapeDtypeStruct((t, LANES), F32),
                   jax.ShapeDtypeStruct((8, t), jnp.int32), jax.ShapeDtypeStruct((N_EXPERTS, LANES), F32)],
        scratch_shapes=[pltpu.VMEM((N_EXPERTS, 1), F32)],
        compiler_params=pltpu.CompilerParams(dimension_semantics=("arbitrary",),
                                             vmem_limit_bytes=VMEM_LIMIT),
        name="router",
    )(x2, oa, obt, w_out, norm_w, rw, rb)


SC_ROWS = 64


def _sc_mesh():
    return plsc.VectorSubcoreMesh(core_axis_name="c", subcore_axis_name="s")


def _dispatch(dest_chunks, h3, n_pad):
    t, s, _ = h3.shape
    mesh = _sc_mesh()
    n_workers = mesh.num_cores * mesh.num_subcores
    per_worker = t // n_workers
    n_chunks = per_worker // SC_ROWS
    assert per_worker % SC_ROWS == 0 and n_chunks % 2 == 0

    @functools.partial(
        pl.kernel, mesh=mesh, out_type=jax.ShapeDtypeStruct((n_pad, s, LANES), h3.dtype),
        scratch_types=[pltpu.VMEM((2, TOP_K * SC_ROWS), jnp.int32), pltpu.VMEM((2, SC_ROWS, s, LANES), h3.dtype),
                       pltpu.SemaphoreType.DMA((2,)), pltpu.SemaphoreType.DMA((2,)), pltpu.SemaphoreType.DMA],
        name="sc_dispatch")
    def scatter(h_hbm, dest_hbm, out_hbm, idx_v, rows_v, rsem, isem, ssem):
        base = (lax.axis_index("s") * mesh.num_cores + lax.axis_index("c")) * per_worker

        def loads(chunk, slot):
            off = pl.multiple_of(base + chunk * SC_ROWS, SC_ROWS)
            return (pltpu.make_async_copy(h_hbm.at[pl.ds(off, SC_ROWS)], rows_v.at[slot], rsem.at[slot]),
                    pltpu.make_async_copy(dest_hbm.at[pl.ds(off * TOP_K, SC_ROWS * TOP_K)], idx_v.at[slot],
                                          isem.at[slot]))

        for cp in loads(0, 0):
            cp.start()

        @pl.loop(0, n_chunks // 2)
        def _(pair):
            for slot in (0, 1):
                chunk = 2 * pair + slot
                for cp in loads(chunk, slot):
                    cp.wait()

                @pl.when(chunk + 1 < n_chunks)
                def _():
                    for cp in loads(chunk + 1, 1 - slot):
                        cp.start()

                copies = [pltpu.async_copy(rows_v.at[slot],
                                           out_hbm.at[idx_v.at[slot, pl.ds(kk * SC_ROWS, SC_ROWS)]], ssem)
                          for kk in range(TOP_K)]
                for cp in copies:
                    cp.wait()

    return scatter(h3, dest_chunks)


def _expert_kernel(cnt_ref, blk0_ref, nblk_ref, total_ref, x_hbm, w1_ref, b1g_ref, b1l_ref, w2_ref, b2_ref,
                   y_hbm, w1s_ref, w2s_ref, xbuf, ybuf, xsem, ysem, *, bm):
    e = pl.program_id(0)
    n_blocks = nblk_ref[e]
    first = blk0_ref[e]
    total = total_ref[0]
    rows_per_block = xbuf.shape[1]
    s_words = rows_per_block // bm
    n_tiles = w1s_ref.shape[-1] // MXU_COLS
    half = MXU_COLS // 2

    def x_copy(g, slot):
        return pltpu.make_async_copy(x_hbm.at[pl.ds(pl.multiple_of(g * rows_per_block, rows_per_block),
                                                    rows_per_block)], xbuf.at[slot], xsem.at[slot])

    def y_copy(g, slot):
        return pltpu.make_async_copy(ybuf.at[slot], y_hbm.at[pl.ds(pl.multiple_of(g * rows_per_block,
                                                                                   rows_per_block),
                                                                   rows_per_block)], ysem.at[slot])

    @pl.when(e == 0)
    def _():
        x_copy(0, 0).start()

    @pl.when(n_blocks > 0)
    def _():
        r = lax.broadcasted_iota(jnp.int32, (MXU_COLS, MXU_COLS), 0)
        c = lax.broadcasted_iota(jnp.int32, (MXU_COLS, MXU_COLS), 1)
        perm = (r == jnp.where(c < half, 2 * c, 2 * (c - half) + 1)).astype(BF16)
        for tix in range(n_tiles):
            cols = slice(tix * MXU_COLS, (tix + 1) * MXU_COLS)
            w1s_ref[:, cols] = _dot(w1_ref[0, :, cols].astype(BF16), perm).astype(BF16)
        w2s_ref[...] = w2_ref[0].astype(BF16)

        def block(j, carry):
            g = first + j
            slot = lax.rem(g, 2)
            x_copy(g, slot).wait()

            @pl.when(g + 1 < total)
            def _():
                x_copy(g + 1, 1 - slot).start()

            @pl.when(g >= 2)
            def _():
                y_copy(g - 2, slot).wait()

            n_rows = cnt_ref[e] - j * bm

            def ffn(m):
                words = _load_packed_rows(xbuf, (slot,), m, s_words)
                row = lax.broadcasted_iota(jnp.int32, words.shape, 0)
                x_lo, x_hi = _unpack_bf16_pair(jnp.where(row < n_rows, words, 0))
                x = jnp.concatenate([x_lo.astype(BF16), x_hi.astype(BF16)], axis=1)
                acts = []
                for tix in range(n_tiles):
                    hp = _dot(x, w1s_ref[:, tix * MXU_COLS:(tix + 1) * MXU_COLS])
                    feat = slice(tix * half, (tix + 1) * half)
                    gl = jnp.minimum(hp[:, :half] + b1g_ref[0][:, feat], SWIGLU_LIMIT)
                    lin = jnp.clip(hp[:, half:] + b1l_ref[0][:, feat], -SWIGLU_LIMIT, SWIGLU_LIMIT)
                    acts.append((gl * _sigmoid(SWIGLU_ALPHA * gl) * (lin + 1.0)).astype(BF16))
                y = _dot(jnp.concatenate(acts, axis=1), w2s_ref[...]) + b2_ref[0]
                d_half = y.shape[-1] // 2
                _store_packed_rows(ybuf.at[slot], _pack_bf16_pair(y[:, :d_half], y[:, d_half:]))

            @pl.when(n_rows > bm // 2)
            def _():
                ffn(bm)

            @pl.when(n_rows <= bm // 2)
            def _():
                ffn(bm // 2)

            y_copy(g, slot).start()
            return carry

        lax.fori_loop(0, n_blocks, block, 0)

    @pl.when(e == pl.num_programs(0) - 1)
    def _():
        @pl.when(total >= 2)
        def _():
            y_copy(total - 2, lax.rem(total, 2)).wait()

        y_copy(total - 1, lax.rem(total - 1, 2)).wait()


def _experts(tables, xs, w1, b1g, b1l, w2, b2, bm, d):
    s_words = d // 2 // LANES
    dff2 = w1.shape[-1]
    dff = dff2 // 2
    emap = lambda e, *_: (e, 0, 0)
    grid_spec = pltpu.PrefetchScalarGridSpec(
        num_scalar_prefetch=len(tables),
        grid=(N_EXPERTS,),
        in_specs=[pl.BlockSpec(memory_space=pl.ANY),
                  pl.BlockSpec((1, d, dff2), emap),
                  pl.BlockSpec((1, 1, dff), emap),
                  pl.BlockSpec((1, 1, dff), emap),
                  pl.BlockSpec((1, dff, d), emap),
                  pl.BlockSpec((1, 1, d), emap)],
        out_specs=pl.BlockSpec(memory_space=pl.ANY),
        scratch_shapes=[pltpu.VMEM((d, dff2), BF16), pltpu.VMEM((dff, d), BF16),
                        pltpu.VMEM((2, bm * s_words, LANES), jnp.int32),
                        pltpu.VMEM((2, bm * s_words, LANES), jnp.int32),
                        pltpu.SemaphoreType.DMA((2,)), pltpu.SemaphoreType.DMA((2,))],
    )
    return pl.pallas_call(
        functools.partial(_expert_kernel, bm=bm),
        grid_spec=grid_spec,
        out_shape=jax.ShapeDtypeStruct(xs.shape, jnp.int32),
        compiler_params=pltpu.CompilerParams(dimension_semantics=("arbitrary",),
                                             vmem_limit_bytes=VMEM_LIMIT),
        name="experts",
    )(*tables, xs, w1, b1g, b1l, w2, b2)


def _gather_rows(dest_flat, y3):
    n_out = dest_flat.shape[0]
    _, s, _ = y3.shape
    mesh = _sc_mesh()
    n_workers = mesh.num_cores * mesh.num_subcores
    per_worker = n_out // n_workers
    n_chunks = per_worker // SC_ROWS
    assert per_worker % SC_ROWS == 0 and n_chunks % 2 == 0

    @functools.partial(
        pl.kernel, mesh=mesh, out_type=jax.ShapeDtypeStruct((n_out, s, LANES), y3.dtype),
        scratch_types=[pltpu.VMEM((2, SC_ROWS), jnp.int32), pltpu.VMEM((2, SC_ROWS, s, LANES), y3.dtype),
                       pltpu.SemaphoreType.DMA((2,)), pltpu.SemaphoreType.DMA((2,))],
        name="sc_gather")
    def gather(y_hbm, dest_hbm, out_hbm, idx_v, rows_v, gsem, wsem):
        base = (lax.axis_index("s") * mesh.num_cores + lax.axis_index("c")) * per_worker

        def rows_of(chunk):
            return pl.ds(pl.multiple_of(base + chunk * SC_ROWS, SC_ROWS), SC_ROWS)

        def gather_copy(slot):
            return pltpu.make_async_copy(y_hbm.at[idx_v.at[slot]], rows_v.at[slot], gsem.at[slot])

        def write_copy(chunk, slot):
            return pltpu.make_async_copy(rows_v.at[slot], out_hbm.at[rows_of(chunk)], wsem.at[slot])

        def start_gather(chunk, slot):
            pltpu.sync_copy(dest_hbm.at[rows_of(chunk)], idx_v.at[slot])
            gather_copy(slot).start()

        start_gather(0, 0)

        @pl.loop(0, n_chunks // 2)
        def _(pair):
            for slot in (0, 1):
                chunk = 2 * pair + slot
                gather_copy(slot).wait()
                write_copy(chunk, slot).start()

                @pl.when(chunk >= 1)
                def _():
                    write_copy(chunk - 1, 1 - slot).wait()

                @pl.when(chunk + 1 < n_chunks)
                def _():
                    start_gather(chunk + 1, 1 - slot)

        write_copy(n_chunks - 1, 1).wait()

    return gather(y3, dest_flat)


def _combine_kernel(x1_ref, gate_ref, y4_ref, nw_ref, o_ref, *, tm, normalize):
    x = x1_ref[...]
    gates = gate_ref[...]
    for kk in range(TOP_K):
        lo, hi = _unpack_bf16_pair(_load_packed_rows(y4_ref, (kk,), tm, y4_ref.shape[1] // tm))
        x = x + gates[:, kk:kk + 1] * jnp.concatenate([lo, hi], axis=1)
    if normalize:
        x = x * lax.rsqrt(jnp.mean(x * x, axis=-1, keepdims=True) + EPS) * nw_ref[...]
    o_ref[...] = x


def _combine(x1, gates, y4, norm_w, tm, normalize):
    t, d = x1.shape
    s_words = d // 2 // LANES
    row = lambda i: (i, 0)
    return pl.pallas_call(
        functools.partial(_combine_kernel, tm=tm, normalize=normalize),
        grid=(t // tm,),
        in_specs=[pl.BlockSpec((tm, d), row), pl.BlockSpec((tm, LANES), row),
                  pl.BlockSpec((TOP_K, tm * s_words, LANES), lambda i: (0, i, 0)),
                  pl.BlockSpec((1, d), lambda i: (0, 0))],
        out_specs=pl.BlockSpec((tm, d), row),
        out_shape=jax.ShapeDtypeStruct((t, d), F32),
        compiler_params=pltpu.CompilerParams(dimension_semantics=("parallel",)),
        name="combine",
    )(x1, gates, y4, norm_w)


def _tiles(t, seq):
    tm = min(2 * MXU_COLS, t)
    tr = min(2 * MXU_COLS, t)
    tq = min(2 * MXU_COLS, seq)
    gdn_rows = min(4 * CHUNK, seq)
    bm = 2 * MXU_COLS
    assert t % tm == 0 and t % tr == 0 and seq % tq == 0 and seq % gdn_rows == 0 and tq % CHUNK == 0
    return tm, tr, tq, gdn_rows, bm


def _rot_half(w):
    half = w.shape[-1] // 2
    return jnp.concatenate([-w[..., half:], w[..., :half]], axis=-1)


def _prep_mla_weights(w_uq, w_ukv):
    dq = MLA_NOPE + MLA_ROPE
    wq = w_uq.reshape(MLA_Q_LORA, MLA_HEADS, dq)
    wq_rot = jnp.concatenate([jnp.zeros_like(wq[..., :MLA_NOPE]), _rot_half(wq[..., MLA_NOPE:])], axis=-1)
    pad = lambda w: jnp.pad(w, ((0, 0), (0, 0), (0, LANES - w.shape[-1]))).reshape(w.shape[0], C_HEADS)
    wkv = w_ukv.reshape(MLA_KV_LORA, MLA_HEADS, MLA_NOPE + MLA_V)
    wv = wkv[..., MLA_NOPE:].reshape(MLA_KV_LORA, C_V)
    return (pad(wq).T.astype(BF16), pad(wq_rot).T.astype(BF16), pad(wkv[..., :MLA_NOPE]).astype(BF16),
            wv.T.astype(BF16))


def _rope_freq():
    half = MLA_ROPE // 2
    return (ROPE_THETA ** (-jnp.arange(half, dtype=F32) / half)).reshape(half, 1)


def _lane_row(v):
    return jnp.pad(v.astype(F32), (0, LANES - v.shape[0])).reshape(1, LANES)


def _routing_tables(cnt, eidx, rank, bm):
    e_ids = jnp.arange(N_EXPERTS, dtype=jnp.int32)
    counts = cnt[:, 0].astype(jnp.int32)
    padded = (counts + bm - 1) // bm * bm
    pend = jnp.sum(jnp.where(e_ids[None, :] <= e_ids[:, None], padded[None, :], 0), axis=1)
    pstart = pend - padded
    dest = jnp.sum(jnp.where(eidx[:TOP_K, :, None] == e_ids, pstart, 0), axis=-1) + rank[:TOP_K]
    tables = (counts, pstart // bm, padded // bm, pend[-1:] // bm)
    return dest.reshape(-1), tuple(t.astype(jnp.int32) for t in tables)


def kernel(x, positions, norm_mix_w, w_in, conv_w, a_log, dt_bias, gdn_norm_w, q_norm_w, w_uq, kv_norm_w,
           w_ukv, mla_out_norm_w, w_out, norm_ffn_w, router_w, router_b, w1, b1, w2, b2, norm_final_w):
    bsz, seq, d = x.shape
    t = bsz * seq
    depth = w_in.shape[0]
    tm, tr, tq, gdn_rows, bm = _tiles(t, seq)
    x2 = x.reshape(t, d)
    posf = positions.astype(F32)
    freq = _rope_freq()
    for l in range(depth):
        qkv, z, cq, ckv, kr2, ab = _inproj(x2, norm_mix_w[l].reshape(1, d), w_in[l], tm)
        conv_w8 = jnp.pad(conv_w[l], ((0, 8 - GDN_CONV), (0, 0)))
        o_a = _gdn(qkv.reshape(bsz, seq, C_QKV), z.reshape(bsz, seq, C_Z), ab.reshape(bsz, seq, LANES),
                   conv_w8, _lane_row(a_log[l]), _lane_row(dt_bias[l]), gdn_norm_w[l].reshape(1, GDN_D),
                   gdn_rows).reshape(t, C_Z)
        wqt, wqrt, wk, wvt = _prep_mla_weights(w_uq[l], w_ukv[l])
        qt, k, vt = _mla_proj(cq, ckv, kr2, posf.reshape(1, t),
                              q_norm_w[l].reshape(1, -1), kv_norm_w[l].reshape(1, -1),
                              wqt, wqrt, wk, wvt, freq, tq)
        o_bt = _attention(qt, k, vt, mla_out_norm_w[l].reshape(-1, 1), bsz, seq, tq)
        x1, hn, eidx, gates, rank, cnt = _router(x2, o_a, o_bt, w_out[l].astype(BF16),
                                                 norm_ffn_w[l].reshape(1, d), router_w[l].T,
                                                 router_b[l].reshape(N_EXPERTS, 1), tr)
        n_pad = t * TOP_K + N_EXPERTS * bm
        dest_flat, tables = _routing_tables(cnt, eidx, rank, bm)
        s_words = d // 2 // LANES
        dest_chunks = dest_flat.reshape(TOP_K, t // SC_ROWS, SC_ROWS).transpose(1, 0, 2).reshape(-1)
        buf = _dispatch(dest_chunks, hn.reshape(t, s_words, LANES), n_pad)
        yb = _experts(tables, buf.reshape(n_pad * s_words, LANES), w1[l],
                      b1[l][:, None, 0::2], b1[l][:, None, 1::2], w2[l], b2[l][:, None, :], bm, d)
        y4 = _gather_rows(dest_flat, yb.reshape(n_pad, s_words, LANES))
        x2 = _combine(x1, gates, y4.reshape(TOP_K, t * s_words, LANES), norm_final_w.reshape(1, d), tr,
                      normalize=(l == depth - 1))
    return x2.reshape(bsz, seq, d)
```

```python
import functools
import math

import jax
import jax.numpy as jnp
from jax import lax
from jax.experimental import pallas as pl
from jax.experimental.pallas import tpu as pltpu
from jax.experimental.pallas import tpu_sc as plsc

F32 = jnp.float32
BF16 = jnp.bfloat16
HIGHEST = lax.Precision.HIGHEST

LANES = 128
MXU_COLS = 256
EPS = 1e-6
CHUNK = 64

GDN_HEADS = 4
GDN_D = 128
GDN_CONV = 4
MLA_HEADS = 8
MLA_NOPE = 64
MLA_ROPE = 32
MLA_V = 64
MLA_Q_LORA = 384
MLA_KV_LORA = 256
ROPE_THETA = 10000.0
N_EXPERTS = 32
TOP_K = 4
SWIGLU_LIMIT = 7.0
SWIGLU_ALPHA = 1.702

NEG_BIG = -1e30
VMEM_LIMIT = 52 * 1024 * 1024


def _sigmoid(x):
    return 1.0 / (1.0 + jnp.exp(-x))


def _softplus(x):
    return jnp.maximum(x, 0.0) + jnp.log(1.0 + jnp.exp(-jnp.abs(x)))


def _dot(a, b, precision=None):
    return jnp.dot(a, b, preferred_element_type=F32, precision=precision)


def _dot_nt(a, b, precision=None):
    return lax.dot_general(a, b, (((1,), (1,)), ((), ())), preferred_element_type=F32,
                           precision=precision)


def _dot_tn(a, b):
    return lax.dot_general(a, b, (((0,), (0,)), ((), ())), preferred_element_type=F32)


def _split_bf16(a):
    hi = a.astype(BF16)
    return hi, (a - hi.astype(F32)).astype(BF16)


def _pack_bf16_pair(lo, hi):
    lo_bits = pltpu.bitcast(lo.astype(BF16).astype(F32), jnp.int32)
    hi_bits = pltpu.bitcast(hi.astype(BF16).astype(F32), jnp.int32)
    return jnp.bitwise_or(hi_bits, lax.shift_right_logical(lo_bits, 16))


def _unpack_bf16_pair(w):
    lo = pltpu.bitcast(lax.shift_left(w, 16), F32)
    hi = pltpu.bitcast(jnp.bitwise_and(w, -65536), F32)
    return lo, hi


def _load_packed_rows(ref, lead, m, s):
    return jnp.concatenate([ref[(*lead, pl.ds(j, m, stride=s), slice(None))] for j in range(s)], axis=1)


def _store_packed_rows(ref, words):
    m = words.shape[0]
    s = words.shape[1] // LANES
    for j in range(s):
        ref[pl.ds(j, m, stride=s), :] = words[:, j * LANES:(j + 1) * LANES]


def _dot_split_nt(a, b):
    return _dot_nt(a[0], b[0]) + _dot_nt(a[0], b[1]) + _dot_nt(a[1], b[0])


def _dot_split(a, b):
    return _dot(a[0], b[0]) + _dot(a[0], b[1]) + _dot(a[1], b[0])


C_QKV = 3 * GDN_HEADS * GDN_D
C_Z = GDN_HEADS * GDN_D
C_IN = C_QKV + C_Z + MLA_Q_LORA + MLA_KV_LORA + 3 * LANES


def _inproj_kernel(x_ref, nw_ref, w_ref, qkv_ref, z_ref, cq_ref, ckv_ref, kr_ref, ab_ref, wb_ref):
    @pl.when(pl.program_id(0) == 0)
    def _():
        d = w_ref.shape[0]
        hd4 = C_QKV + C_Z
        wb_ref[:, 0:hd4] = w_ref[:, 0:hd4].astype(BF16)
        tail = w_ref[:, hd4:]
        o_cq = 2 * GDN_HEADS
        o_ckv = o_cq + MLA_Q_LORA
        o_kr = o_ckv + MLA_KV_LORA
        kr = tail[:, o_kr:o_kr + MLA_ROPE]
        half = MLA_ROPE // 2
        zeros = lambda n: jnp.zeros((d, n), F32)
        pad = LANES - MLA_NOPE - MLA_ROPE
        groups = [tail[:, o_cq:o_ckv], tail[:, o_ckv:o_kr],
                  jnp.concatenate([zeros(MLA_NOPE), kr, zeros(pad)], axis=1),
                  jnp.concatenate([zeros(MLA_NOPE), -kr[:, half:], kr[:, :half], zeros(pad)], axis=1),
                  jnp.concatenate([tail[:, 0:o_cq], zeros(LANES - o_cq)], axis=1)]
        o = hd4
        for grp in groups:
            wb_ref[:, o:o + grp.shape[1]] = grp.astype(BF16)
            o += grp.shape[1]

    x = x_ref[...]
    var = jnp.mean(x * x, axis=-1, keepdims=True)
    h = (x * lax.rsqrt(var + EPS) * nw_ref[...]).astype(BF16)
    p = _dot(h, wb_ref[...])
    o = 0
    for ref in (qkv_ref, z_ref, cq_ref, ckv_ref, kr_ref, ab_ref):
        n = ref.shape[-1]
        ref[...] = p[:, o:o + n]
        o += n


def _inproj(x2, norm_w, w_in, tm):
    t, d = x2.shape
    widths = (C_QKV, C_Z, MLA_Q_LORA, MLA_KV_LORA, 2 * LANES, LANES)
    return pl.pallas_call(
        _inproj_kernel,
        grid=(t // tm,),
        in_specs=[pl.BlockSpec((tm, d), lambda i: (i, 0)),
                  pl.BlockSpec((1, d), lambda i: (0, 0)),
                  pl.BlockSpec(w_in.shape, lambda i: (0, 0), pipeline_mode=pl.Buffered(1))],
        out_specs=[pl.BlockSpec((tm, n), lambda i: (i, 0)) for n in widths],
        out_shape=[jax.ShapeDtypeStruct((t, n), F32) for n in widths],
        scratch_shapes=[pltpu.VMEM((d, C_IN), BF16)],
        compiler_params=pltpu.CompilerParams(dimension_semantics=("arbitrary",),
                                             vmem_limit_bytes=VMEM_LIMIT),
        name="inproj",
    )(x2, norm_w, w_in)


def _gdn_kernel(qkv_ref, z_ref, ab_ref, cw_ref, alog_ref, dtb_ref, nw_ref, o_ref, ext_ref, s_ref):
    c = CHUNK
    bsz, rows, _ = qkv_ref.shape
    chunks_per_batch = rows // c

    @pl.when(pl.program_id(0) == 0)
    def _():
        ext_ref[:, 0:8, :] = jnp.zeros((bsz, 8, C_QKV), F32)
        s_ref[...] = jnp.zeros_like(s_ref)

    cw = cw_ref[...]
    nw = nw_ref[...]
    rr = lax.broadcasted_iota(jnp.int32, (rows, rows), 0)
    rc = lax.broadcasted_iota(jnp.int32, (rows, rows), 1)
    in_chunk_prefix = jnp.where(rr // c == rc // c, jnp.where(rr >= rc, 1.0, 0.0), 0.0)
    e8 = (lax.broadcasted_iota(jnp.int32, (8, LANES), 0)
          == lax.broadcasted_iota(jnp.int32, (8, LANES), 1)).astype(F32)
    ri = lax.broadcasted_iota(jnp.int32, (c, c), 0)
    ci = lax.broadcasted_iota(jnp.int32, (c, c), 1)
    causal = ri >= ci
    strict = ri > ci
    eye = (ri == ci).astype(F32)
    chains = [(j, h) for j in range(chunks_per_batch) for h in range(GDN_HEADS)]

    def prep_pieces(b):
        g = {"xc": [None] * (C_QKV // LANES)}

        def conv(tiles):
            def run():
                cols = slice(tiles[0] * LANES, (tiles[-1] + 1) * LANES)
                x = qkv_ref[b, :, cols]
                ext_ref[b, 8:8 + rows, cols] = x
                ext = ext_ref[b, :, cols]
                xb = cw[3:4, cols] * x
                for j in range(1, GDN_CONV):
                    xb = xb + cw[3 - j:4 - j, cols] * pltpu.roll(ext, j, axis=0)[8:8 + rows, :]
                ext_ref[b, 0:8, cols] = x[rows - 8:rows, :]
                xb = xb * _sigmoid(xb)
                for i, tile in enumerate(tiles):
                    g["xc"][tile] = xb[:, i * LANES:(i + 1) * LANES]
            return run

        def gates():
            ab = ab_ref[b]
            g_all = -jnp.exp(alog_ref[...]) * _softplus(ab + dtb_ref[...])
            g["beta"] = _sigmoid(ab)
            g["g_cum"] = _dot(in_chunk_prefix, g_all, HIGHEST)
            g["g_row"] = _dot_nt(e8, g["g_cum"], HIGHEST)
            for name in ("q", "k", "v", "gc", "bt", "dec"):
                g[name] = []

        def chunk(j):
            def run():
                rs = slice(j * c, (j + 1) * c)
                xc = g["xc"]
                for h in range(GDN_HEADS):
                    q = xc[h][rs]
                    k = xc[GDN_HEADS + h][rs]
                    g["q"].append(q * lax.rsqrt(jnp.sum(q * q, axis=-1, keepdims=True) + EPS) * (GDN_D ** -0.5))
                    g["k"].append(k * lax.rsqrt(jnp.sum(k * k, axis=-1, keepdims=True) + EPS))
                    g["v"].append(xc[2 * GDN_HEADS + h][rs])
                    gc = g["g_cum"][rs, h:h + 1]
                    gr = g["g_row"][h:h + 1, rs]
                    g["gc"].append(gc)
                    g["bt"].append(g["beta"][rs, GDN_HEADS + h:GDN_HEADS + h + 1])
                    g["dec"].append(jnp.exp(jnp.where(causal, gc - gr, -jnp.inf)))
            return run

        def a_mats():
            g["kb"] = [k * bt for k, bt in zip(g["k"], g["bt"])]
            g["k16"] = [k.astype(BF16) for k in g["k"]]
            g["a"] = [jnp.where(strict, _dot_nt(kb.astype(BF16), k16) * dec, 0.0)
                      for kb, k16, dec in zip(g["kb"], g["k16"], g["dec"])]

        n_tiles = C_QKV // LANES
        conv_pieces = [conv(list(range(i, i + 2))) for i in range(0, n_tiles, 2)]
        return g, conv_pieces + [gates] + [chunk(j) for j in range(chunks_per_batch)] + [a_mats]

    def inverse_rounds(g):
        def start():
            g["t"] = [eye - a for a in g["a"]]
            g["pw"] = [_split_bf16(a) for a in g["a"]]

        def square():
            g["pw"] = [_split_bf16(_dot_split(p, p)) for p in g["pw"]]

        def update():
            g["t"] = [t + _dot_split(_split_bf16(t), p) for t, p in zip(g["t"], g["pw"])]

        return [start] + [square, update] * (CHUNK.bit_length() - 2)

    groups = []
    pending = []
    for b in range(bsz):
        g, pieces = prep_pieces(b)
        groups.append(g)
        n_a, n_b = len(pending), len(pieces)
        ia = ib = 0
        while ia < n_a or ib < n_b:
            if ib >= n_b or (ia < n_a and ia * n_b <= ib * n_a):
                pending[ia]()
                ia += 1
            else:
                pieces[ib]()
                ib += 1
        pending = inverse_rounds(g)
    for fn in pending:
        fn()

    uws, qk16s, qd16s, kd16s, g_tots = [], [], [], [], []
    for g in groups:
        e_gs = [jnp.exp(gc) for gc in g["gc"]]
        uws.append([_dot(t.astype(BF16), jnp.concatenate([v * bt, kb * eg], axis=1).astype(BF16))
                    for t, v, bt, kb, eg in zip(g["t"], g["v"], g["bt"], g["kb"], e_gs)])
        qk16s.append([(_dot_nt(q.astype(BF16), k16) * dec).astype(BF16)
                      for q, k16, dec in zip(g["q"], g["k16"], g["dec"])])
        qd16s.append([(q * eg).astype(BF16) for q, eg in zip(g["q"], e_gs)])
        kd16s.append([(k * jnp.exp(gc[c - 1:c, :] - gc)).astype(BF16) for k, gc in zip(g["k"], g["gc"])])
        g_tots.append([jnp.exp(gc[c - 1:c, :]) for gc in g["gc"]])

    pairs = [(b, h) for b in range(bsz) for h in range(GDN_HEADS)]
    states = [s_ref[b * GDN_HEADS + h] for b, h in pairs]
    for j in range(chunks_per_batch):
        rs = slice(j * c, (j + 1) * c)
        pick = lambda per_group: [per_group[b][chains.index((j, h))] for b, h in pairs]
        uw_j, qk_j, qd_j, kd_j, gt_j = (pick(x) for x in (uws, qk16s, qd16s, kd16s, g_tots))
        s16s = [s.astype(BF16) for s in states]
        v16s = [(uw[:, :GDN_D] - _dot(uw[:, GDN_D:].astype(BF16), s16)).astype(BF16)
                for uw, s16 in zip(uw_j, s16s)]
        states = [s * gt + _dot_tn(kd, v16) for s, gt, kd, v16 in zip(states, gt_j, kd_j, v16s)]
        os_ = [_dot(qd, s16) + _dot(qk, v16) for qd, qk, s16, v16 in zip(qd_j, qk_j, s16s, v16s)]
        for (b, h), o in zip(pairs, os_):
            on = o * lax.rsqrt(jnp.mean(o * o, axis=-1, keepdims=True) + EPS) * nw
            zh = z_ref[b, rs, h * GDN_D:(h + 1) * GDN_D]
            o_ref[b, rs, h * GDN_D:(h + 1) * GDN_D] = on * (zh * _sigmoid(zh))
    for p, s in enumerate(states):
        s_ref[p] = s


def _gdn(qkv, z, ab, conv_w8, alog_row, dtb_row, norm_w, rows):
    bsz, seq, _ = qkv.shape
    blk = lambda i: (0, i, 0)
    const = lambda i: (0, 0)
    return pl.pallas_call(
        _gdn_kernel,
        grid=(seq // rows,),
        in_specs=[pl.BlockSpec((bsz, rows, C_QKV), blk),
                  pl.BlockSpec((bsz, rows, C_Z), blk),
                  pl.BlockSpec((bsz, rows, LANES), blk),
                  pl.BlockSpec((8, C_QKV), const),
                  pl.BlockSpec((1, LANES), const),
                  pl.BlockSpec((1, LANES), const),
                  pl.BlockSpec((1, GDN_D), const)],
        out_specs=pl.BlockSpec((bsz, rows, C_Z), blk),
        out_shape=jax.ShapeDtypeStruct((bsz, seq, C_Z), F32),
        scratch_shapes=[pltpu.VMEM((bsz, 8 + rows, C_QKV), F32),
                        pltpu.VMEM((bsz * GDN_HEADS, GDN_D, GDN_D), F32)],
        compiler_params=pltpu.CompilerParams(dimension_semantics=("arbitrary",),
                                             vmem_limit_bytes=VMEM_LIMIT),
        name="gdn",
    )(qkv, z, ab, conv_w8, alog_row, dtb_row, norm_w)


D_QK_PAD = LANES
C_HEADS = MLA_HEADS * D_QK_PAD
C_V = MLA_HEADS * MLA_V
V_AUG = MLA_V + 16
C_VAUG = MLA_HEADS * V_AUG
Q_SCALE = (MLA_NOPE + MLA_ROPE) ** -0.5 * math.log2(math.e)


def _mla_proj_kernel(cq_ref, ckv_ref, kr_ref, posr_ref, qnw_ref, kvnw_ref, wqt_ref, wqrt_ref,
                     wk_ref, wvt_ref, freqc_ref, qt_ref, k_ref, vt_ref):
    cq = cq_ref[...]
    cqn = (cq * lax.rsqrt(jnp.mean(cq * cq, axis=-1, keepdims=True) + EPS) * qnw_ref[...]).astype(BF16)
    ckv = ckv_ref[...]
    ckvn = (ckv * lax.rsqrt(jnp.mean(ckv * ckv, axis=-1, keepdims=True) + EPS)
            * kvnw_ref[...]).astype(BF16)
    tm = posr_ref.shape[-1]
    ang = freqc_ref[...] * posr_ref[...]
    c16 = jnp.cos(ang)
    s16 = jnp.sin(ang)
    pad = LANES - MLA_NOPE - MLA_ROPE
    cs_t = jnp.concatenate([jnp.ones((MLA_NOPE, tm), F32), c16, c16, jnp.ones((pad, tm), F32)], axis=0)
    sn_t = jnp.concatenate([jnp.zeros((MLA_NOPE, tm), F32), s16, s16, jnp.zeros((pad, tm), F32)], axis=0)
    cs = cs_t.T
    sn = sn_t.T
    qa_t = _dot_nt(wqt_ref[...], cqn)
    qb_t = _dot_nt(wqrt_ref[...], cqn)
    kn = _dot(ckvn, wk_ref[...])
    kr = kr_ref[...]
    kpe = kr[:, :LANES] * cs + kr[:, LANES:] * sn
    for h in range(MLA_HEADS):
        sl = slice(h * LANES, (h + 1) * LANES)
        qt_ref[sl, :] = ((qa_t[sl, :] * cs_t + qb_t[sl, :] * sn_t) * Q_SCALE).astype(BF16)
        k_ref[:, sl] = (kn[:, sl] + kpe).astype(BF16)
    v_t = _dot_nt(wvt_ref[...], ckvn).astype(BF16)
    ones = jnp.ones((V_AUG - MLA_V, v_t.shape[1]), BF16)
    for h in range(MLA_HEADS):
        vt_ref[0, h * V_AUG:h * V_AUG + MLA_V, :] = v_t[h * MLA_V:(h + 1) * MLA_V, :]
        vt_ref[0, h * V_AUG + MLA_V:(h + 1) * V_AUG, :] = ones


def _mla_proj(cq, ckv, kr2, posr, qnw, kvnw, wqt, wqrt, wk, wvt, freqc, tm):
    t = cq.shape[0]
    row = lambda i: (i, 0)
    col = lambda i: (0, i)
    const = lambda i: (0, 0)
    return pl.pallas_call(
        _mla_proj_kernel,
        grid=(t // tm,),
        in_specs=[pl.BlockSpec((tm, MLA_Q_LORA), row),
                  pl.BlockSpec((tm, MLA_KV_LORA), row),
                  pl.BlockSpec((tm, 2 * LANES), row),
                  pl.BlockSpec((1, tm), col),
                  pl.BlockSpec((1, MLA_Q_LORA), const),
                  pl.BlockSpec((1, MLA_KV_LORA), const),
                  pl.BlockSpec((C_HEADS, MLA_Q_LORA), const),
                  pl.BlockSpec((C_HEADS, MLA_Q_LORA), const),
                  pl.BlockSpec((MLA_KV_LORA, C_HEADS), const),
                  pl.BlockSpec((C_V, MLA_KV_LORA), const),
                  pl.BlockSpec((MLA_ROPE // 2, 1), const)],
        out_specs=[pl.BlockSpec((C_HEADS, tm), col),
                   pl.BlockSpec((tm, C_HEADS), row),
                   pl.BlockSpec((1, C_VAUG, tm), lambda i: (i, 0, 0))],
        out_shape=[jax.ShapeDtypeStruct((C_HEADS, t), BF16),
                   jax.ShapeDtypeStruct((t, C_HEADS), BF16),
                   jax.ShapeDtypeStruct((t // tm, C_VAUG, tm), BF16)],
        compiler_params=pltpu.CompilerParams(dimension_semantics=("parallel",),
                                             vmem_limit_bytes=VMEM_LIMIT),
        name="mla_proj",
    )(cq, ckv, kr2, posr, qnw, kvnw, wqt, wqrt, wk, wvt, freqc)


def _attn_kernel(qt_ref, k_ref, vt_ref, nw_ref, ot_ref, m_ref, acc_ref, *, tq):
    i = pl.program_id(1)
    m_ref[...] = jnp.full(m_ref.shape, NEG_BIG, F32)
    acc_ref[...] = jnp.zeros(acc_ref.shape, F32)
    key_chunk = lax.broadcasted_iota(jnp.int32, (tq, tq), 0) // CHUNK
    qry_chunk = lax.broadcasted_iota(jnp.int32, (tq, tq), 1) // CHUNK
    diag_mask = key_chunk <= qry_chunk

    def tile(j, masked):
        rows = pl.ds(pl.multiple_of(j * tq, tq), tq)

        scores = []
        for h in range(MLA_HEADS):
            hs = slice(h * LANES, (h + 1) * LANES)
            s_t = _dot(k_ref[rows, hs], qt_ref[hs, :])
            scores.append(jnp.where(diag_mask, s_t, NEG_BIG) if masked else s_t)
        probs, alphas = [], []
        for h in range(MLA_HEADS):
            m_old = m_ref[h]
            m_new = jnp.maximum(m_old, jnp.max(scores[h], axis=0, keepdims=True))
            m_ref[h] = m_new
            probs.append(jnp.exp2(scores[h] - m_new[0:1, :]).astype(BF16))
            alphas.append(jnp.exp2(m_old[0:1, :] - m_new[0:1, :]))
        for h in range(MLA_HEADS):
            v_aug = vt_ref[j, h * V_AUG:(h + 1) * V_AUG, :]
            acc_ref[h] = alphas[h] * acc_ref[h] + _dot(v_aug, probs[h])

    def full_tile(j, carry):
        tile(j, False)
        return carry

    lax.fori_loop(0, i, full_tile, 0)
    tile(i, True)

    outs = []
    for h in range(MLA_HEADS):
        acc = acc_ref[h]
        outs.append(acc[:MLA_V, :] / acc[MLA_V:MLA_V + 1, :])
    ssq = outs[0] * outs[0]
    for o in outs[1:]:
        ssq = ssq + o * o
    inv = lax.rsqrt(jnp.sum(ssq, axis=0, keepdims=True) / C_V + EPS)
    for h in range(MLA_HEADS):
        vs = slice(h * MLA_V, (h + 1) * MLA_V)
        ot_ref[vs, :] = outs[h] * inv * nw_ref[vs, :]


def _attention(qt, k, vt, norm_w_col, bsz, seq, tq):
    nq = seq // tq
    return pl.pallas_call(
        functools.partial(_attn_kernel, tq=tq),
        grid=(bsz, nq),
        in_specs=[pl.BlockSpec((C_HEADS, tq), lambda b, i: (0, b * nq + i)),
                  pl.BlockSpec((seq, C_HEADS), lambda b, i: (b, 0)),
                  pl.BlockSpec((nq, C_VAUG, tq), lambda b, i: (b, 0, 0)),
                  pl.BlockSpec((C_V, 1), lambda b, i: (0, 0))],
        out_specs=pl.BlockSpec((C_V, tq), lambda b, i: (0, b * nq + i)),
        out_shape=jax.ShapeDtypeStruct((C_V, bsz * seq), F32),
        scratch_shapes=[pltpu.VMEM((MLA_HEADS, 8, tq), F32),
                        pltpu.VMEM((MLA_HEADS, V_AUG, tq), F32)],
        compiler_params=pltpu.CompilerParams(dimension_semantics=("parallel", "parallel"),
                                             vmem_limit_bytes=VMEM_LIMIT),
        name="attention",
    )(qt, k, vt, norm_w_col)


def _router_kernel(x_ref, oa_ref, obt_ref, wo_ref, nw_ref, rw_ref, rb_ref,
                   x1_ref, h_ref, eidx_ref, gate_ref, rank_ref, cnt_ref, run_ref, *, tm):
    @pl.when(pl.program_id(0) == 0)
    def _():
        run_ref[...] = jnp.zeros_like(run_ref)

    half = oa_ref.shape[-1]
    y = (_dot(oa_ref[...].astype(BF16), wo_ref[0:half, :])
         + _dot_tn(obt_ref[...].astype(BF16), wo_ref[half:, :]))
    x1 = x_ref[...] + y
    x1_ref[...] = x1
    hn = x1 * lax.rsqrt(jnp.mean(x1 * x1, axis=-1, keepdims=True) + EPS) * nw_ref[...]
    d_half = hn.shape[-1] // 2
    _store_packed_rows(h_ref, _pack_bf16_pair(hn[:, :d_half], hn[:, d_half:]))
    lg = _dot_split_nt(_split_bf16(rw_ref[...]), _split_bf16(hn)) + rb_ref[...]
    expert = lax.broadcasted_iota(jnp.int32, lg.shape, 0)
    sels, tops, idxs = [], [], []
    for _ in range(TOP_K):
        m = jnp.max(lg, axis=0, keepdims=True)
        idx = jnp.min(jnp.where(lg == m, expert, N_EXPERTS), axis=0, keepdims=True)
        sel = expert == idx
        lg = jnp.where(sel, -jnp.inf, lg)
        sels.append(sel)
        tops.append(m)
        idxs.append(idx)
    exps = [jnp.exp(tv - tops[0]) for tv in tops]
    den = functools.reduce(lambda a, b: a + b, exps)
    multi = sels[0].astype(F32)
    for sel in sels[1:]:
        multi = multi + sel.astype(F32)
    ri = lax.broadcasted_iota(jnp.int32, (tm, tm), 0)
    ci = lax.broadcasted_iota(jnp.int32, (tm, tm), 1)
    before = _dot(multi.astype(BF16), (ri < ci).astype(BF16)) + run_ref[...]
    ranks = [jnp.sum(jnp.where(sel, before, 0.0), axis=0, keepdims=True).astype(jnp.int32) for sel in sels]
    run = run_ref[...] + jnp.sum(multi, axis=1, keepdims=True)
    run_ref[...] = run
    cnt_ref[...] = jnp.broadcast_to(run, cnt_ref.shape)
    pad_i = jnp.zeros((8 - TOP_K, tm), jnp.int32)
    eidx_ref[...] = jnp.concatenate(idxs + [pad_i], axis=0)
    rank_ref[...] = jnp.concatenate(ranks + [pad_i], axis=0)
    gates_t = jnp.concatenate([e / den for e in exps] + [jnp.zeros((LANES - TOP_K, tm), F32)], axis=0)
    gate_ref[...] = gates_t.T


def _router(x2, oa, obt, w_out, norm_w, rw, rb, tm):
    t, d = x2.shape
    half = oa.shape[-1]
    s_words = d // 2 // LANES
    row = lambda i: (i, 0)
    col = lambda i: (0, i)
    const = lambda i: (0, 0)
    return pl.pallas_call(
        functools.partial(_router_kernel, tm=tm),
        grid=(t // tm,),
        in_specs=[pl.BlockSpec((tm, d), row),
                  pl.BlockSpec((tm, half), row),
                  pl.BlockSpec((half, tm), lambda i: (0, i)),
                  pl.BlockSpec((2 * half, d), const),
                  pl.BlockSpec((1, d), const),
                  pl.BlockSpec((N_EXPERTS, d), const),
                  pl.BlockSpec((N_EXPERTS, 1), const)],
        out_specs=[pl.BlockSpec((tm, d), row), pl.BlockSpec((tm * s_words, LANES), row),
                   pl.BlockSpec((8, tm), col), pl.BlockSpec((tm, LANES), row),
                   pl.BlockSpec((8, tm), col), pl.BlockSpec((N_EXPERTS, LANES), const)],
        out_shape=[jax.ShapeDtypeStruct((t, d), F32), jax.ShapeDtypeStruct((t * s_words, LANES), jnp.int32),
                   jax.ShapeDtypeStruct((8, t), jnp.int32), jax.ShapeDtypeStruct((t, LANES), F32),
                   jax.ShapeDtypeStruct((8, t), jnp.int32), jax.ShapeDtypeStruct((N_EXPERTS, LANES), F32)],
        scratch_shapes=[pltpu.VMEM((N_EXPERTS, 1), F32)],
        compiler_params=pltpu.CompilerParams(dimension_semantics=("arbitrary",),
                                             vmem_limit_bytes=VMEM_LIMIT),
        name="router",
    )(x2, oa, obt, w_out, norm_w, rw, rb)


SC_ROWS = 64


def _sc_mesh():
    return plsc.VectorSubcoreMesh(core_axis_name="c", subcore_axis_name="s")


def _dispatch(dest_chunks, h3, n_pad):
    t, s, _ = h3.shape
    mesh = _sc_mesh()
    n_workers = mesh.num_cores * mesh.num_subcores
    per_worker = t // n_workers
    n_chunks = per_worker // SC_ROWS
    assert per_worker % SC_ROWS == 0 and n_chunks % 2 == 0

    @functools.partial(
        pl.kernel, mesh=mesh, out_type=jax.ShapeDtypeStruct((n_pad, s, LANES), h3.dtype),
        scratch_types=[pltpu.VMEM((2, TOP_K * SC_ROWS), jnp.int32), pltpu.VMEM((2, SC_ROWS, s, LANES), h3.dtype),
                       pltpu.SemaphoreType.DMA((2,)), pltpu.SemaphoreType.DMA((2,)), pltpu.SemaphoreType.DMA],
        name="sc_dispatch")
    def scatter(h_hbm, dest_hbm, out_hbm, idx_v, rows_v, rsem, isem, ssem):
        base = (lax.axis_index("s") * mesh.num_cores + lax.axis_index("c")) * per_worker

        def loads(chunk, slot):
            off = pl.multiple_of(base + chunk * SC_ROWS, SC_ROWS)
            return (pltpu.make_async_copy(h_hbm.at[pl.ds(off, SC_ROWS)], rows_v.at[slot], rsem.at[slot]),
                    pltpu.make_async_copy(dest_hbm.at[pl.ds(off * TOP_K, SC_ROWS * TOP_K)], idx_v.at[slot],
                                          isem.at[slot]))

        for cp in loads(0, 0):
            cp.start()

        @pl.loop(0, n_chunks // 2)
        def _(pair):
            for slot in (0, 1):
                chunk = 2 * pair + slot
                for cp in loads(chunk, slot):
                    cp.wait()

                @pl.when(chunk + 1 < n_chunks)
                def _():
                    for cp in loads(chunk + 1, 1 - slot):
                        cp.start()

                copies = [pltpu.async_copy(rows_v.at[slot],
                                           out_hbm.at[idx_v.at[slot, pl.ds(kk * SC_ROWS, SC_ROWS)]], ssem)
                          for kk in range(TOP_K)]
                for cp in copies:
                    cp.wait()

    return scatter(h3, dest_chunks)


def _expert_kernel(cnt_ref, blk0_ref, nblk_ref, total_ref, x_hbm, w1_ref, b1g_ref, b1l_ref, w2_ref, b2_ref,
                   y_hbm, w1s_ref, w2s_ref, xbuf, ybuf, xsem, ysem, *, bm):
    e = pl.program_id(0)
    n_blocks = nblk_ref[e]
    first = blk0_ref[e]
    total = total_ref[0]
    rows_per_block = xbuf.shape[1]
    s_words = rows_per_block // bm
    n_tiles = w1s_ref.shape[-1] // MXU_COLS
    half = MXU_COLS // 2

    def x_copy(g, slot):
        return pltpu.make_async_copy(x_hbm.at[pl.ds(pl.multiple_of(g * rows_per_block, rows_per_block),
                                                    rows_per_block)], xbuf.at[slot], xsem.at[slot])

    def y_copy(g, slot):
        return pltpu.make_async_copy(ybuf.at[slot], y_hbm.at[pl.ds(pl.multiple_of(g * rows_per_block,
                                                                                   rows_per_block),
                                                                   rows_per_block)], ysem.at[slot])

    @pl.when(e == 0)
    def _():
        x_copy(0, 0).start()

    @pl.when(n_blocks > 0)
    def _():
        r = lax.broadcasted_iota(jnp.int32, (MXU_COLS, MXU_COLS), 0)
        c = lax.broadcasted_iota(jnp.int32, (MXU_COLS, MXU_COLS), 1)
        perm = (r == jnp.where(c < half, 2 * c, 2 * (c - half) + 1)).astype(BF16)
        for tix in range(n_tiles):
            cols = slice(tix * MXU_COLS, (tix + 1) * MXU_COLS)
            w1s_ref[:, cols] = _dot(w1_ref[0, :, cols].astype(BF16), perm).astype(BF16)
        w2s_ref[...] = w2_ref[0].astype(BF16)

        def block(j, carry):
            g = first + j
            slot = lax.rem(g, 2)
            x_copy(g, slot).wait()

            @pl.when(g + 1 < total)
            def _():
                x_copy(g + 1, 1 - slot).start()

            @pl.when(g >= 2)
            def _():
                y_copy(g - 2, slot).wait()

            n_rows = cnt_ref[e] - j * bm

            def ffn(m):
                words = _load_packed_rows(xbuf, (slot,), m, s_words)
                row = lax.broadcasted_iota(jnp.int32, words.shape, 0)
                x_lo, x_hi = _unpack_bf16_pair(jnp.where(row < n_rows, words, 0))
                x = jnp.concatenate([x_lo.astype(BF16), x_hi.astype(BF16)], axis=1)
                acts = []
                for tix in range(n_tiles):
                    hp = _dot(x, w1s_ref[:, tix * MXU_COLS:(tix + 1) * MXU_COLS])
                    feat = slice(tix * half, (tix + 1) * half)
                    gl = jnp.minimum(hp[:, :half] + b1g_ref[0][:, feat], SWIGLU_LIMIT)
                    lin = jnp.clip(hp[:, half:] + b1l_ref[0][:, feat], -SWIGLU_LIMIT, SWIGLU_LIMIT)
                    acts.append((gl * _sigmoid(SWIGLU_ALPHA * gl) * (lin + 1.0)).astype(BF16))
                y = _dot(jnp.concatenate(acts, axis=1), w2s_ref[...]) + b2_ref[0]
                d_half = y.shape[-1] // 2
                _store_packed_rows(ybuf.at[slot], _pack_bf16_pair(y[:, :d_half], y[:, d_half:]))

            quarter = bm // 4
            for q in range(1, 5):
                in_range = n_rows > (q - 1) * quarter
                if q < 4:
                    in_range = jnp.logical_and(in_range, n_rows <= q * quarter)
                pl.when(in_range)(functools.partial(ffn, q * quarter))

            y_copy(g, slot).start()
            return carry

        lax.fori_loop(0, n_blocks, block, 0)

    @pl.when(e == pl.num_programs(0) - 1)
    def _():
        @pl.when(total >= 2)
        def _():
            y_copy(total - 2, lax.rem(total, 2)).wait()

        y_copy(total - 1, lax.rem(total - 1, 2)).wait()


def _experts(tables, xs, w1, b1g, b1l, w2, b2, bm, d):
    s_words = d // 2 // LANES
    dff2 = w1.shape[-1]
    dff = dff2 // 2
    emap = lambda e, *_: (e, 0, 0)
    grid_spec = pltpu.PrefetchScalarGridSpec(
        num_scalar_prefetch=len(tables),
        grid=(N_EXPERTS,),
        in_specs=[pl.BlockSpec(memory_space=pl.ANY),
                  pl.BlockSpec((1, d, dff2), emap),
                  pl.BlockSpec((1, 1, dff), emap),
                  pl.BlockSpec((1, 1, dff), emap),
                  pl.BlockSpec((1, dff, d), emap),
                  pl.BlockSpec((1, 1, d), emap)],
        out_specs=pl.BlockSpec(memory_space=pl.ANY),
        scratch_shapes=[pltpu.VMEM((d, dff2), BF16), pltpu.VMEM((dff, d), BF16),
                        pltpu.VMEM((2, bm * s_words, LANES), jnp.int32),
                        pltpu.VMEM((2, bm * s_words, LANES), jnp.int32),
                        pltpu.SemaphoreType.DMA((2,)), pltpu.SemaphoreType.DMA((2,))],
    )
    return pl.pallas_call(
        functools.partial(_expert_kernel, bm=bm),
        grid_spec=grid_spec,
        out_shape=jax.ShapeDtypeStruct(xs.shape, jnp.int32),
        compiler_params=pltpu.CompilerParams(dimension_semantics=("arbitrary",),
                                             vmem_limit_bytes=VMEM_LIMIT),
        name="experts",
    )(*tables, xs, w1, b1g, b1l, w2, b2)


def _gather_rows(dest_flat, y3):
    n_out = dest_flat.shape[0]
    _, s, _ = y3.shape
    mesh = _sc_mesh()
    n_workers = mesh.num_cores * mesh.num_subcores
    per_worker = n_out // n_workers
    n_chunks = per_worker // SC_ROWS
    assert per_worker % SC_ROWS == 0 and n_chunks % 2 == 0

    @functools.partial(
        pl.kernel, mesh=mesh, out_type=jax.ShapeDtypeStruct((n_out, s, LANES), y3.dtype),
        scratch_types=[pltpu.VMEM((2, SC_ROWS), jnp.int32), pltpu.VMEM((2, SC_ROWS, s, LANES), y3.dtype),
                       pltpu.SemaphoreType.DMA((2,)), pltpu.SemaphoreType.DMA((2,))],
        name="sc_gather")
    def gather(y_hbm, dest_hbm, out_hbm, idx_v, rows_v, gsem, wsem):
        base = (lax.axis_index("s") * mesh.num_cores + lax.axis_index("c")) * per_worker

        def rows_of(chunk):
            return pl.ds(pl.multiple_of(base + chunk * SC_ROWS, SC_ROWS), SC_ROWS)

        def gather_copy(slot):
            return pltpu.make_async_copy(y_hbm.at[idx_v.at[slot]], rows_v.at[slot], gsem.at[slot])

        def write_copy(chunk, slot):
            return pltpu.make_async_copy(rows_v.at[slot], out_hbm.at[rows_of(chunk)], wsem.at[slot])

        def start_gather(chunk, slot):
            pltpu.sync_copy(dest_hbm.at[rows_of(chunk)], idx_v.at[slot])
            gather_copy(slot).start()

        start_gather(0, 0)

        @pl.loop(0, n_chunks // 2)
        def _(pair):
            for slot in (0, 1):
                chunk = 2 * pair + slot
                gather_copy(slot).wait()
                write_copy(chunk, slot).start()

                @pl.when(chunk >= 1)
                def _():
                    write_copy(chunk - 1, 1 - slot).wait()

                @pl.when(chunk + 1 < n_chunks)
                def _():
                    start_gather(chunk + 1, 1 - slot)

        write_copy(n_chunks - 1, 1).wait()

    return gather(y3, dest_flat)


def _combine_kernel(x1_ref, gate_ref, y4_ref, nw_ref, o_ref, *, tm, normalize):
    x = x1_ref[...]
    gates = gate_ref[...]
    for kk in range(TOP_K):
        lo, hi = _unpack_bf16_pair(_load_packed_rows(y4_ref, (kk,), tm, y4_ref.shape[1] // tm))
        x = x + gates[:, kk:kk + 1] * jnp.concatenate([lo, hi], axis=1)
    if normalize:
        x = x * lax.rsqrt(jnp.mean(x * x, axis=-1, keepdims=True) + EPS) * nw_ref[...]
    o_ref[...] = x


def _combine(x1, gates, y4, norm_w, tm, normalize):
    t, d = x1.shape
    s_words = d // 2 // LANES
    row = lambda i: (i, 0)
    return pl.pallas_call(
        functools.partial(_combine_kernel, tm=tm, normalize=normalize),
        grid=(t // tm,),
        in_specs=[pl.BlockSpec((tm, d), row), pl.BlockSpec((tm, LANES), row),
                  pl.BlockSpec((TOP_K, tm * s_words, LANES), lambda i: (0, i, 0)),
                  pl.BlockSpec((1, d), lambda i: (0, 0))],
        out_specs=pl.BlockSpec((tm, d), row),
        out_shape=jax.ShapeDtypeStruct((t, d), F32),
        compiler_params=pltpu.CompilerParams(dimension_semantics=("parallel",)),
        name="combine",
    )(x1, gates, y4, norm_w)


def _tiles(t, seq):
    tm = min(2 * MXU_COLS, t)
    tr = min(2 * MXU_COLS, t)
    tq = min(2 * MXU_COLS, seq)
    gdn_rows = min(4 * CHUNK, seq)
    bm = 2 * MXU_COLS
    assert t % tm == 0 and t % tr == 0 and seq % tq == 0 and seq % gdn_rows == 0 and tq % CHUNK == 0
    return tm, tr, tq, gdn_rows, bm


def _rot_half(w):
    half = w.shape[-1] // 2
    return jnp.concatenate([-w[..., half:], w[..., :half]], axis=-1)


def _prep_mla_weights(w_uq, w_ukv):
    dq = MLA_NOPE + MLA_ROPE
    wq = w_uq.reshape(MLA_Q_LORA, MLA_HEADS, dq)
    wq_rot = jnp.concatenate([jnp.zeros_like(wq[..., :MLA_NOPE]), _rot_half(wq[..., MLA_NOPE:])], axis=-1)
    pad = lambda w: jnp.pad(w, ((0, 0), (0, 0), (0, LANES - w.shape[-1]))).reshape(w.shape[0], C_HEADS)
    wkv = w_ukv.reshape(MLA_KV_LORA, MLA_HEADS, MLA_NOPE + MLA_V)
    wv = wkv[..., MLA_NOPE:].reshape(MLA_KV_LORA, C_V)
    return (pad(wq).T.astype(BF16), pad(wq_rot).T.astype(BF16), pad(wkv[..., :MLA_NOPE]).astype(BF16),
            wv.T.astype(BF16))


def _rope_freq():
    half = MLA_ROPE // 2
    return (ROPE_THETA ** (-jnp.arange(half, dtype=F32) / half)).reshape(half, 1)


def _lane_row(v):
    return jnp.pad(v.astype(F32), (0, LANES - v.shape[0])).reshape(1, LANES)


def _routing_tables(cnt, eidx, rank, bm):
    e_ids = jnp.arange(N_EXPERTS, dtype=jnp.int32)
    counts = cnt[:, 0].astype(jnp.int32)
    padded = (counts + bm - 1) // bm * bm
    pend = jnp.sum(jnp.where(e_ids[None, :] <= e_ids[:, None], padded[None, :], 0), axis=1)
    pstart = pend - padded
    dest = jnp.sum(jnp.where(eidx[:TOP_K, :, None] == e_ids, pstart, 0), axis=-1) + rank[:TOP_K]
    tables = (counts, pstart // bm, padded // bm, pend[-1:] // bm)
    return dest.reshape(-1), tuple(t.astype(jnp.int32) for t in tables)


def kernel(x, positions, norm_mix_w, w_in, conv_w, a_log, dt_bias, gdn_norm_w, q_norm_w, w_uq, kv_norm_w,
           w_ukv, mla_out_norm_w, w_out, norm_ffn_w, router_w, router_b, w1, b1, w2, b2, norm_final_w):
    bsz, seq, d = x.shape
    t = bsz * seq
    depth = w_in.shape[0]
    tm, tr, tq, gdn_rows, bm = _tiles(t, seq)
    x2 = x.reshape(t, d)
    posf = positions.astype(F32)
    freq = _rope_freq()
    for l in range(depth):
        qkv, z, cq, ckv, kr2, ab = _inproj(x2, norm_mix_w[l].reshape(1, d), w_in[l], tm)
        conv_w8 = jnp.pad(conv_w[l], ((0, 8 - GDN_CONV), (0, 0)))
        o_a = _gdn(qkv.reshape(bsz, seq, C_QKV), z.reshape(bsz, seq, C_Z), ab.reshape(bsz, seq, LANES),
                   conv_w8, _lane_row(a_log[l]), _lane_row(dt_bias[l]), gdn_norm_w[l].reshape(1, GDN_D),
                   gdn_rows).reshape(t, C_Z)
        wqt, wqrt, wk, wvt = _prep_mla_weights(w_uq[l], w_ukv[l])
        qt, k, vt = _mla_proj(cq, ckv, kr2, posf.reshape(1, t),
                              q_norm_w[l].reshape(1, -1), kv_norm_w[l].reshape(1, -1),
                              wqt, wqrt, wk, wvt, freq, tq)
        o_bt = _attention(qt, k, vt, mla_out_norm_w[l].reshape(-1, 1), bsz, seq, tq)
        x1, hn, eidx, gates, rank, cnt = _router(x2, o_a, o_bt, w_out[l].astype(BF16),
                                                 norm_ffn_w[l].reshape(1, d), router_w[l].T,
                                                 router_b[l].reshape(N_EXPERTS, 1), tr)
        n_pad = t * TOP_K + N_EXPERTS * bm
        dest_flat, tables = _routing_tables(cnt, eidx, rank, bm)
        s_words = d // 2 // LANES
        dest_chunks = dest_flat.reshape(TOP_K, t // SC_ROWS, SC_ROWS).transpose(1, 0, 2).reshape(-1)
        buf = _dispatch(dest_chunks, hn.reshape(t, s_words, LANES), n_pad)
        yb = _experts(tables, buf.reshape(n_pad * s_words, LANES), w1[l],
                      b1[l][:, None, 0::2], b1[l][:, None, 1::2], w2[l], b2[l][:, None, :], bm, d)
        y4 = _gather_rows(dest_flat, yb.reshape(n_pad, s_words, LANES))
        x2 = _combine(x1, gates, y4.reshape(TOP_K, t * s_words, LANES), norm_final_w.reshape(1, d), tr,
                      normalize=(l == depth - 1))
    return x2.reshape(bsz, seq, d)
```

```python
import functools
import math

import jax
import jax.numpy as jnp
from jax import lax
from jax.experimental import pallas as pl
from jax.experimental.pallas import tpu as pltpu
from jax.experimental.pallas import tpu_sc as plsc

F32 = jnp.float32
BF16 = jnp.bfloat16
HIGHEST = lax.Precision.HIGHEST

LANES = 128
MXU_COLS = 256
EPS = 1e-6
CHUNK = 64

GDN_HEADS = 4
GDN_D = 128
GDN_CONV = 4
MLA_HEADS = 8
MLA_NOPE = 64
MLA_ROPE = 32
MLA_V = 64
MLA_Q_LORA = 384
MLA_KV_LORA = 256
ROPE_THETA = 10000.0
N_EXPERTS = 32
TOP_K = 4
SWIGLU_LIMIT = 7.0
SWIGLU_ALPHA = 1.702

NEG_BIG = -1e30
VMEM_LIMIT = 52 * 1024 * 1024


def _sigmoid(x):
    return 1.0 / (1.0 + jnp.exp(-x))


def _softplus(x):
    return jnp.maximum(x, 0.0) + jnp.log(1.0 + jnp.exp(-jnp.abs(x)))


def _dot(a, b, precision=None):
    return jnp.dot(a, b, preferred_element_type=F32, precision=precision)


def _dot_nt(a, b, precision=None):
    return lax.dot_general(a, b, (((1,), (1,)), ((), ())), preferred_element_type=F32,
                           precision=precision)


def _dot_tn(a, b):
    return lax.dot_general(a, b, (((0,), (0,)), ((), ())), preferred_element_type=F32)


def _split_bf16(a):
    hi = a.astype(BF16)
    return hi, (a - hi.astype(F32)).astype(BF16)


def _pack_bf16_pair(lo, hi):
    lo_bits = pltpu.bitcast(lo.astype(BF16).astype(F32), jnp.int32)
    hi_bits = pltpu.bitcast(hi.astype(BF16).astype(F32), jnp.int32)
    return jnp.bitwise_or(hi_bits, lax.shift_right_logical(lo_bits, 16))


def _unpack_bf16_pair(w):
    lo = pltpu.bitcast(lax.shift_left(w, 16), F32)
    hi = pltpu.bitcast(jnp.bitwise_and(w, -65536), F32)
    return lo, hi


def _load_packed_rows(ref, lead, m, s):
    return jnp.concatenate([ref[(*lead, pl.ds(j, m, stride=s), slice(None))] for j in range(s)], axis=1)


def _store_packed_rows(ref, words):
    m = words.shape[0]
    s = words.shape[1] // LANES
    for j in range(s):
        ref[pl.ds(j, m, stride=s), :] = words[:, j * LANES:(j + 1) * LANES]


def _dot_split_nt(a, b):
    return _dot_nt(a[0], b[0]) + _dot_nt(a[0], b[1]) + _dot_nt(a[1], b[0])


def _dot_split(a, b):
    return _dot(a[0], b[0]) + _dot(a[0], b[1]) + _dot(a[1], b[0])


C_QKV = 3 * GDN_HEADS * GDN_D
C_Z = GDN_HEADS * GDN_D
C_IN = C_QKV + C_Z + MLA_Q_LORA + MLA_KV_LORA + 3 * LANES


def _inproj_kernel(x_ref, nw_ref, w_ref, qkv_ref, z_ref, cq_ref, ckv_ref, kr_ref, ab_ref, wb_ref):
    @pl.when(pl.program_id(0) == 0)
    def _():
        d = w_ref.shape[0]
        hd4 = C_QKV + C_Z
        wb_ref[:, 0:hd4] = w_ref[:, 0:hd4].astype(BF16)
        tail = w_ref[:, hd4:]
        o_cq = 2 * GDN_HEADS
        o_ckv = o_cq + MLA_Q_LORA
        o_kr = o_ckv + MLA_KV_LORA
        kr = tail[:, o_kr:o_kr + MLA_ROPE]
        half = MLA_ROPE // 2
        zeros = lambda n: jnp.zeros((d, n), F32)
        pad = LANES - MLA_NOPE - MLA_ROPE
        groups = [tail[:, o_cq:o_ckv], tail[:, o_ckv:o_kr],
                  jnp.concatenate([zeros(MLA_NOPE), kr, zeros(pad)], axis=1),
                  jnp.concatenate([zeros(MLA_NOPE), -kr[:, half:], kr[:, :half], zeros(pad)], axis=1),
                  jnp.concatenate([tail[:, 0:o_cq], zeros(LANES - o_cq)], axis=1)]
        o = hd4
        for grp in groups:
            wb_ref[:, o:o + grp.shape[1]] = grp.astype(BF16)
            o += grp.shape[1]

    x = x_ref[...]
    var = jnp.mean(x * x, axis=-1, keepdims=True)
    h = (x * lax.rsqrt(var + EPS) * nw_ref[...]).astype(BF16)
    p = _dot(h, wb_ref[...])
    o = 0
    for ref in (qkv_ref, z_ref, cq_ref, ckv_ref, kr_ref, ab_ref):
        n = ref.shape[-1]
        ref[...] = p[:, o:o + n]
        o += n


def _inproj(x2, norm_w, w_in, tm):
    t, d = x2.shape
    widths = (C_QKV, C_Z, MLA_Q_LORA, MLA_KV_LORA, 2 * LANES, LANES)
    return pl.pallas_call(
        _inproj_kernel,
        grid=(t // tm,),
        in_specs=[pl.BlockSpec((tm, d), lambda i: (i, 0)),
                  pl.BlockSpec((1, d), lambda i: (0, 0)),
                  pl.BlockSpec(w_in.shape, lambda i: (0, 0), pipeline_mode=pl.Buffered(1))],
        out_specs=[pl.BlockSpec((tm, n), lambda i: (i, 0)) for n in widths],
        out_shape=[jax.ShapeDtypeStruct((t, n), F32) for n in widths],
        scratch_shapes=[pltpu.VMEM((d, C_IN), BF16)],
        compiler_params=pltpu.CompilerParams(dimension_semantics=("arbitrary",),
                                             vmem_limit_bytes=VMEM_LIMIT),
        name="inproj",
    )(x2, norm_w, w_in)


def _gdn_kernel(qkv_ref, z_ref, ab_ref, cw_ref, alog_ref, dtb_ref, nw_ref, o_ref, ext_ref, s_ref):
    c = CHUNK
    bsz, rows, _ = qkv_ref.shape
    chunks_per_batch = rows // c

    @pl.when(pl.program_id(0) == 0)
    def _():
        ext_ref[:, 0:8, :] = jnp.zeros((bsz, 8, C_QKV), F32)
        s_ref[...] = jnp.zeros_like(s_ref)

    cw = cw_ref[...]
    nw = nw_ref[...]
    rr = lax.broadcasted_iota(jnp.int32, (rows, rows), 0)
    rc = lax.broadcasted_iota(jnp.int32, (rows, rows), 1)
    in_chunk_prefix = jnp.where(rr // c == rc // c, jnp.where(rr >= rc, 1.0, 0.0), 0.0)
    e8 = (lax.broadcasted_iota(jnp.int32, (8, LANES), 0)
          == lax.broadcasted_iota(jnp.int32, (8, LANES), 1)).astype(F32)
    ri = lax.broadcasted_iota(jnp.int32, (c, c), 0)
    ci = lax.broadcasted_iota(jnp.int32, (c, c), 1)
    causal = ri >= ci
    strict = ri > ci
    eye = (ri == ci).astype(F32)
    chains = [(j, h) for j in range(chunks_per_batch) for h in range(GDN_HEADS)]

    def prep_pieces(b):
        g = {"xc": [None] * (C_QKV // LANES)}

        def conv(tiles):
            def run():
                cols = slice(tiles[0] * LANES, (tiles[-1] + 1) * LANES)
                x = qkv_ref[b, :, cols]
                ext_ref[b, 8:8 + rows, cols] = x
                ext = ext_ref[b, :, cols]
                xb = cw[3:4, cols] * x
                for j in range(1, GDN_CONV):
                    xb = xb + cw[3 - j:4 - j, cols] * pltpu.roll(ext, j, axis=0)[8:8 + rows, :]
                ext_ref[b, 0:8, cols] = x[rows - 8:rows, :]
                xb = xb * _sigmoid(xb)
                for i, tile in enumerate(tiles):
                    g["xc"][tile] = xb[:, i * LANES:(i + 1) * LANES]
            return run

        def gates():
            ab = ab_ref[b]
            g_all = -jnp.exp(alog_ref[...]) * _softplus(ab + dtb_ref[...])
            g["beta"] = _sigmoid(ab)
            g["g_cum"] = _dot(in_chunk_prefix, g_all, HIGHEST)
            g["g_row"] = _dot_nt(e8, g["g_cum"], HIGHEST)
            for name in ("q", "k", "v", "gc", "bt", "dec"):
                g[name] = []

        def chunk(j):
            def run():
                rs = slice(j * c, (j + 1) * c)
                xc = g["xc"]
                for h in range(GDN_HEADS):
                    q = xc[h][rs]
                    k = xc[GDN_HEADS + h][rs]
                    g["q"].append(q * lax.rsqrt(jnp.sum(q * q, axis=-1, keepdims=True) + EPS) * (GDN_D ** -0.5))
                    g["k"].append(k * lax.rsqrt(jnp.sum(k * k, axis=-1, keepdims=True) + EPS))
                    g["v"].append(xc[2 * GDN_HEADS + h][rs])
                    gc = g["g_cum"][rs, h:h + 1]
                    gr = g["g_row"][h:h + 1, rs]
                    g["gc"].append(gc)
                    g["bt"].append(g["beta"][rs, GDN_HEADS + h:GDN_HEADS + h + 1])
                    g["dec"].append(jnp.exp(jnp.where(causal, gc - gr, -jnp.inf)))
            return run

        def a_mats():
            g["kb"] = [k * bt for k, bt in zip(g["k"], g["bt"])]
            g["k16"] = [k.astype(BF16) for k in g["k"]]
            g["a"] = [jnp.where(strict, _dot_nt(kb.astype(BF16), k16) * dec, 0.0)
                      for kb, k16, dec in zip(g["kb"], g["k16"], g["dec"])]

        n_tiles = C_QKV // LANES
        conv_pieces = [conv(list(range(i, i + 2))) for i in range(0, n_tiles, 2)]
        return g, conv_pieces + [gates] + [chunk(j) for j in range(chunks_per_batch)] + [a_mats]

    def inverse_rounds(g):
        def start():
            g["t"] = [eye - a for a in g["a"]]
            g["pw"] = [_split_bf16(a) for a in g["a"]]

        def square():
            g["pw"] = [_split_bf16(_dot_split(p, p)) for p in g["pw"]]

        def update():
            g["t"] = [t + _dot_split(_split_bf16(t), p) for t, p in zip(g["t"], g["pw"])]

        return [start] + [square, update] * (CHUNK.bit_length() - 2)

    groups = []
    pending = []
    for b in range(bsz):
        g, pieces = prep_pieces(b)
        groups.append(g)
        n_a, n_b = len(pending), len(pieces)
        ia = ib = 0
        while ia < n_a or ib < n_b:
            if ib >= n_b or (ia < n_a and ia * n_b <= ib * n_a):
                pending[ia]()
                ia += 1
            else:
                pieces[ib]()
                ib += 1
        pending = inverse_rounds(g)
    for fn in pending:
        fn()

    uws, qk16s, qd16s, kd16s, g_tots = [], [], [], [], []
    for g in groups:
        e_gs = [jnp.exp(gc) for gc in g["gc"]]
        uws.append([_dot(t.astype(BF16), jnp.concatenate([v * bt, kb * eg], axis=1).astype(BF16))
                    for t, v, bt, kb, eg in zip(g["t"], g["v"], g["bt"], g["kb"], e_gs)])
        qk16s.append([(_dot_nt(q.astype(BF16), k16) * dec).astype(BF16)
                      for q, k16, dec in zip(g["q"], g["k16"], g["dec"])])
        qd16s.append([(q * eg).astype(BF16) for q, eg in zip(g["q"], e_gs)])
        kd16s.append([(k * jnp.exp(gc[c - 1:c, :] - gc)).astype(BF16) for k, gc in zip(g["k"], g["gc"])])
        g_tots.append([jnp.exp(gc[c - 1:c, :]) for gc in g["gc"]])

    pairs = [(b, h) for b in range(bsz) for h in range(GDN_HEADS)]
    states = [s_ref[b * GDN_HEADS + h] for b, h in pairs]
    for j in range(chunks_per_batch):
        rs = slice(j * c, (j + 1) * c)
        pick = lambda per_group: [per_group[b][chains.index((j, h))] for b, h in pairs]
        uw_j, qk_j, qd_j, kd_j, gt_j = (pick(x) for x in (uws, qk16s, qd16s, kd16s, g_tots))
        s16s = [s.astype(BF16) for s in states]
        v16s = [(uw[:, :GDN_D] - _dot(uw[:, GDN_D:].astype(BF16), s16)).astype(BF16)
                for uw, s16 in zip(uw_j, s16s)]
        states = [s * gt + _dot_tn(kd, v16) for s, gt, kd, v16 in zip(states, gt_j, kd_j, v16s)]
        os_ = [_dot(qd, s16) + _dot(qk, v16) for qd, qk, s16, v16 in zip(qd_j, qk_j, s16s, v16s)]
        for (b, h), o in zip(pairs, os_):
            on = o * lax.rsqrt(jnp.mean(o * o, axis=-1, keepdims=True) + EPS) * nw
            zh = z_ref[b, rs, h * GDN_D:(h + 1) * GDN_D]
            o_ref[b, rs, h * GDN_D:(h + 1) * GDN_D] = on * (zh * _sigmoid(zh))
    for p, s in enumerate(states):
        s_ref[p] = s


def _gdn(qkv, z, ab, conv_w8, alog_row, dtb_row, norm_w, rows):
    bsz, seq, _ = qkv.shape
    blk = lambda i: (0, i, 0)
    const = lambda i: (0, 0)
    return pl.pallas_call(
        _gdn_kernel,
        grid=(seq // rows,),
        in_specs=[pl.BlockSpec((bsz, rows, C_QKV), blk),
                  pl.BlockSpec((bsz, rows, C_Z), blk),
                  pl.BlockSpec((bsz, rows, LANES), blk),
                  pl.BlockSpec((8, C_QKV), const),
                  pl.BlockSpec((1, LANES), const),
                  pl.BlockSpec((1, LANES), const),
                  pl.BlockSpec((1, GDN_D), const)],
        out_specs=pl.BlockSpec((bsz, rows, C_Z), blk),
        out_shape=jax.ShapeDtypeStruct((bsz, seq, C_Z), F32),
        scratch_shapes=[pltpu.VMEM((bsz, 8 + rows, C_QKV), F32),
                        pltpu.VMEM((bsz * GDN_HEADS, GDN_D, GDN_D), F32)],
        compiler_params=pltpu.CompilerParams(dimension_semantics=("arbitrary",),
                                             vmem_limit_bytes=VMEM_LIMIT),
        name="gdn",
    )(qkv, z, ab, conv_w8, alog_row, dtb_row, norm_w)


D_QK_PAD = LANES
C_HEADS = MLA_HEADS * D_QK_PAD
C_V = MLA_HEADS * MLA_V
V_AUG = MLA_V + 16
C_VAUG = MLA_HEADS * V_AUG
Q_SCALE = (MLA_NOPE + MLA_ROPE) ** -0.5 * math.log2(math.e)


def _mla_proj_kernel(cq_ref, ckv_ref, kr_ref, posr_ref, qnw_ref, kvnw_ref, wqt_ref, wqrt_ref,
                     wk_ref, wvt_ref, freqc_ref, qt_ref, k_ref, vt_ref):
    cq = cq_ref[...]
    cqn = (cq * lax.rsqrt(jnp.mean(cq * cq, axis=-1, keepdims=True) + EPS) * qnw_ref[...]).astype(BF16)
    ckv = ckv_ref[...]
    ckvn = (ckv * lax.rsqrt(jnp.mean(ckv * ckv, axis=-1, keepdims=True) + EPS)
            * kvnw_ref[...]).astype(BF16)
    tm = posr_ref.shape[-1]
    ang = freqc_ref[...] * posr_ref[...]
    c16 = jnp.cos(ang)
    s16 = jnp.sin(ang)
    pad = LANES - MLA_NOPE - MLA_ROPE
    cs_t = jnp.concatenate([jnp.ones((MLA_NOPE, tm), F32), c16, c16, jnp.ones((pad, tm), F32)], axis=0)
    sn_t = jnp.concatenate([jnp.zeros((MLA_NOPE, tm), F32), s16, s16, jnp.zeros((pad, tm), F32)], axis=0)
    cs = cs_t.T
    sn = sn_t.T
    qa_t = _dot_nt(wqt_ref[...], cqn)
    qb_t = _dot_nt(wqrt_ref[...], cqn)
    kn = _dot(ckvn, wk_ref[...])
    kr = kr_ref[...]
    kpe = kr[:, :LANES] * cs + kr[:, LANES:] * sn
    for h in range(MLA_HEADS):
        sl = slice(h * LANES, (h + 1) * LANES)
        qt_ref[sl, :] = ((qa_t[sl, :] * cs_t + qb_t[sl, :] * sn_t) * Q_SCALE).astype(BF16)
        k_ref[:, sl] = (kn[:, sl] + kpe).astype(BF16)
    v_t = _dot_nt(wvt_ref[...], ckvn).astype(BF16)
    ones = jnp.ones((V_AUG - MLA_V, v_t.shape[1]), BF16)
    for h in range(MLA_HEADS):
        vt_ref[0, h * V_AUG:h * V_AUG + MLA_V, :] = v_t[h * MLA_V:(h + 1) * MLA_V, :]
        vt_ref[0, h * V_AUG + MLA_V:(h + 1) * V_AUG, :] = ones


def _mla_proj(cq, ckv, kr2, posr, qnw, kvnw, wqt, wqrt, wk, wvt, freqc, tm):
    t = cq.shape[0]
    row = lambda i: (i, 0)
    col = lambda i: (0, i)
    const = lambda i: (0, 0)
    return pl.pallas_call(
        _mla_proj_kernel,
        grid=(t // tm,),
        in_specs=[pl.BlockSpec((tm, MLA_Q_LORA), row),
                  pl.BlockSpec((tm, MLA_KV_LORA), row),
                  pl.BlockSpec((tm, 2 * LANES), row),
                  pl.BlockSpec((1, tm), col),
                  pl.BlockSpec((1, MLA_Q_LORA), const),
                  pl.BlockSpec((1, MLA_KV_LORA), const),
                  pl.BlockSpec((C_HEADS, MLA_Q_LORA), const),
                  pl.BlockSpec((C_HEADS, MLA_Q_LORA), const),
                  pl.BlockSpec((MLA_KV_LORA, C_HEADS), const),
                  pl.BlockSpec((C_V, MLA_KV_LORA), const),
                  pl.BlockSpec((MLA_ROPE // 2, 1), const)],
        out_specs=[pl.BlockSpec((C_HEADS, tm), col),
                   pl.BlockSpec((tm, C_HEADS), row),
                   pl.BlockSpec((1, C_VAUG, tm), lambda i: (i, 0, 0))],
        out_shape=[jax.ShapeDtypeStruct((C_HEADS, t), BF16),
                   jax.ShapeDtypeStruct((t, C_HEADS), BF16),
                   jax.ShapeDtypeStruct((t // tm, C_VAUG, tm), BF16)],
        compiler_params=pltpu.CompilerParams(dimension_semantics=("parallel",),
                                             vmem_limit_bytes=VMEM_LIMIT),
        name="mla_proj",
    )(cq, ckv, kr2, posr, qnw, kvnw, wqt, wqrt, wk, wvt, freqc)


def _attn_kernel(qt_ref, k_ref, vt_ref, nw_ref, ot_ref, m_ref, acc_ref, *, tq):
    i = pl.program_id(1)
    m_ref[...] = jnp.full(m_ref.shape, NEG_BIG, F32)
    acc_ref[...] = jnp.zeros(acc_ref.shape, F32)
    key_chunk = lax.broadcasted_iota(jnp.int32, (tq, tq), 0) // CHUNK
    qry_chunk = lax.broadcasted_iota(jnp.int32, (tq, tq), 1) // CHUNK
    diag_mask = key_chunk <= qry_chunk

    def tile(j, masked):
        rows = pl.ds(pl.multiple_of(j * tq, tq), tq)

        scores = []
        for h in range(MLA_HEADS):
            hs = slice(h * LANES, (h + 1) * LANES)
            s_t = _dot(k_ref[rows, hs], qt_ref[hs, :])
            scores.append(jnp.where(diag_mask, s_t, NEG_BIG) if masked else s_t)
        probs, alphas = [], []
        for h in range(MLA_HEADS):
            m_old = m_ref[h]
            m_new = jnp.maximum(m_old, jnp.max(scores[h], axis=0, keepdims=True))
            m_ref[h] = m_new
            probs.append(jnp.exp2(scores[h] - m_new[0:1, :]).astype(BF16))
            alphas.append(jnp.exp2(m_old[0:1, :] - m_new[0:1, :]))
        for h in range(MLA_HEADS):
            v_aug = vt_ref[j, h * V_AUG:(h + 1) * V_AUG, :]
            acc_ref[h] = alphas[h] * acc_ref[h] + _dot(v_aug, probs[h])

    def full_tile(j, carry):
        tile(j, False)
        return carry

    lax.fori_loop(0, i, full_tile, 0)
    tile(i, True)

    outs = []
    for h in range(MLA_HEADS):
        acc = acc_ref[h]
        outs.append(acc[:MLA_V, :] / acc[MLA_V:MLA_V + 1, :])
    ssq = outs[0] * outs[0]
    for o in outs[1:]:
        ssq = ssq + o * o
    inv = lax.rsqrt(jnp.sum(ssq, axis=0, keepdims=True) / C_V + EPS)
    for h in range(MLA_HEADS):
        vs = slice(h * MLA_V, (h + 1) * MLA_V)
        ot_ref[vs, :] = outs[h] * inv * nw_ref[vs, :]


def _attention(qt, k, vt, norm_w_col, bsz, seq, tq):
    nq = seq // tq
    return pl.pallas_call(
        functools.partial(_attn_kernel, tq=tq),
        grid=(bsz, nq),
        in_specs=[pl.BlockSpec((C_HEADS, tq), lambda b, i: (0, b * nq + i)),
                  pl.BlockSpec((seq, C_HEADS), lambda b, i: (b, 0)),
                  pl.BlockSpec((nq, C_VAUG, tq), lambda b, i: (b, 0, 0)),
                  pl.BlockSpec((C_V, 1), lambda b, i: (0, 0))],
        out_specs=pl.BlockSpec((C_V, tq), lambda b, i: (0, b * nq + i)),
        out_shape=jax.ShapeDtypeStruct((C_V, bsz * seq), F32),
        scratch_shapes=[pltpu.VMEM((MLA_HEADS, 8, tq), F32),
                        pltpu.VMEM((MLA_HEADS, V_AUG, tq), F32)],
        compiler_params=pltpu.CompilerParams(dimension_semantics=("parallel", "parallel"),
                                             vmem_limit_bytes=VMEM_LIMIT),
        name="attention",
    )(qt, k, vt, norm_w_col)


def _router_kernel(x_ref, oa_ref, obt_ref, wo_ref, nw_ref, rw_ref, rb_ref,
                   x1_ref, h_ref, eidx_ref, gate_ref, rank_ref, cnt_ref, run_ref, *, tm):
    @pl.when(pl.program_id(0) == 0)
    def _():
        run_ref[...] = jnp.zeros_like(run_ref)

    half = oa_ref.shape[-1]
    y = (_dot(oa_ref[...].astype(BF16), wo_ref[0:half, :])
         + _dot_tn(obt_ref[...].astype(BF16), wo_ref[half:, :]))
    x1 = x_ref[...] + y
    x1_ref[...] = x1
    hn = x1 * lax.rsqrt(jnp.mean(x1 * x1, axis=-1, keepdims=True) + EPS) * nw_ref[...]
    d_half = hn.shape[-1] // 2
    _store_packed_rows(h_ref, _pack_bf16_pair(hn[:, :d_half], hn[:, d_half:]))
    lg = _dot_split_nt(_split_bf16(rw_ref[...]), _split_bf16(hn)) + rb_ref[...]
    expert = lax.broadcasted_iota(jnp.int32, lg.shape, 0)
    sels, tops, idxs = [], [], []
    for _ in range(TOP_K):
        m = jnp.max(lg, axis=0, keepdims=True)
        idx = jnp.min(jnp.where(lg == m, expert, N_EXPERTS), axis=0, keepdims=True)
        sel = expert == idx
        lg = jnp.where(sel, -jnp.inf, lg)
        sels.append(sel)
        tops.append(m)
        idxs.append(idx)
    exps = [jnp.exp(tv - tops[0]) for tv in tops]
    den = functools.reduce(lambda a, b: a + b, exps)
    multi = sels[0].astype(F32)
    for sel in sels[1:]:
        multi = multi + sel.astype(F32)
    ri = lax.broadcasted_iota(jnp.int32, (tm, tm), 0)
    ci = lax.broadcasted_iota(jnp.int32, (tm, tm), 1)
    before = _dot(multi.astype(BF16), (ri < ci).astype(BF16)) + run_ref[...]
    ranks = [jnp.sum(jnp.where(sel, before, 0.0), axis=0, keepdims=True).astype(jnp.int32) for sel in sels]
    run = run_ref[...] + jnp.sum(multi, axis=1, keepdims=True)
    run_ref[...] = run
    cnt_ref[...] = jnp.broadcast_to(run, cnt_ref.shape)
    pad_i = jnp.zeros((8 - TOP_K, tm), jnp.int32)
    eidx_ref[...] = jnp.concatenate(idxs + [pad_i], axis=0)
    rank_ref[...] = jnp.concatenate(ranks + [pad_i], axis=0)
    gates_t = jnp.concatenate([e / den for e in exps] + [jnp.zeros((LANES - TOP_K, tm), F32)], axis=0)
    gate_ref[...] = gates_t.T


def _router(x2, oa, obt, w_out, norm_w, rw, rb, tm):
    t, d = x2.shape
    half = oa.shape[-1]
    s_words = d // 2 // LANES
    row = lambda i: (i, 0)
    col = lambda i: (0, i)
    const = lambda i: (0, 0)
    return pl.pallas_call(
        functools.partial(_router_kernel, tm=tm),
        grid=(t // tm,),
        in_specs=[pl.BlockSpec((tm, d), row),
                  pl.BlockSpec((tm, half), row),
                  pl.BlockSpec((half, tm), lambda i: (0, i)),
                  pl.BlockSpec((2 * half, d), const),
                  pl.BlockSpec((1, d), const),
                  pl.BlockSpec((N_EXPERTS, d), const),
                  pl.BlockSpec((N_EXPERTS, 1), const)],
        out_specs=[pl.BlockSpec((tm, d), row), pl.BlockSpec((tm * s_words, LANES), row),
                   pl.BlockSpec((8, tm), col), pl.BlockSpec((tm, LANES), row),
                   pl.BlockSpec((8, tm), col), pl.BlockSpec((N_EXPERTS, LANES), const)],
        out_shape=[jax.ShapeDtypeStruct((t, d), F32), jax.ShapeDtypeStruct((t * s_words, LANES), jnp.int32),
                   jax.ShapeDtypeStruct((8, t), jnp.int32), jax.ShapeDtypeStruct((t, LANES), F32),
                   jax.ShapeDtypeStruct((8, t), jnp.int32), jax.ShapeDtypeStruct((N_EXPERTS, LANES), F32)],
        scratch_shapes=[pltpu.VMEM((N_EXPERTS, 1), F32)],
        compiler_params=pltpu.CompilerParams(dimension_semantics=("arbitrary",),
                                             vmem_limit_bytes=VMEM_LIMIT),
        name="router",
    )(x2, oa, obt, w_out, norm_w, rw, rb)


SC_ROWS = 64


def _sc_mesh():
    return plsc.VectorSubcoreMesh(core_axis_name="c", subcore_axis_name="s")


def _dispatch(dest_chunks, h3, n_pad):
    t, s, _ = h3.shape
    mesh = _sc_mesh()
    n_workers = mesh.num_cores * mesh.num_subcores
    per_worker = t // n_workers
    n_chunks = per_worker // SC_ROWS
    assert per_worker % SC_ROWS == 0 and n_chunks % 2 == 0

    @functools.partial(
        pl.kernel, mesh=mesh, out_type=jax.ShapeDtypeStruct((n_pad, s, LANES), h3.dtype),
        scratch_types=[pltpu.VMEM((2, TOP_K * SC_ROWS), jnp.int32), pltpu.VMEM((2, SC_ROWS, s, LANES), h3.dtype),
                       pltpu.SemaphoreType.DMA((2,)), pltpu.SemaphoreType.DMA((2,)), pltpu.SemaphoreType.DMA],
        name="sc_dispatch")
    def scatter(h_hbm, dest_hbm, out_hbm, idx_v, rows_v, rsem, isem, ssem):
        base = (lax.axis_index("s") * mesh.num_cores + lax.axis_index("c")) * per_worker

        def loads(chunk, slot):
            off = pl.multiple_of(base + chunk * SC_ROWS, SC_ROWS)
            return (pltpu.make_async_copy(h_hbm.at[pl.ds(off, SC_ROWS)], rows_v.at[slot], rsem.at[slot]),
                    pltpu.make_async_copy(dest_hbm.at[pl.ds(off * TOP_K, SC_ROWS * TOP_K)], idx_v.at[slot],
                                          isem.at[slot]))

        for cp in loads(0, 0):
            cp.start()

        @pl.loop(0, n_chunks // 2)
        def _(pair):
            for slot in (0, 1):
                chunk = 2 * pair + slot
                for cp in loads(chunk, slot):
                    cp.wait()

                @pl.when(chunk + 1 < n_chunks)
                def _():
                    for cp in loads(chunk + 1, 1 - slot):
                        cp.start()

                copies = [pltpu.async_copy(rows_v.at[slot],
                                           out_hbm.at[idx_v.at[slot, pl.ds(kk * SC_ROWS, SC_ROWS)]], ssem)
                          for kk in range(TOP_K)]
                for cp in copies:
                    cp.wait()

    return scatter(h3, dest_chunks)


def _expert_kernel(cnt_ref, blk0_ref, nblk_ref, total_ref, x_hbm, w1_ref, b1g_ref, b1l_ref, w2_ref, b2_ref,
                   y_hbm, w1s_ref, w2s_ref, xbuf, ybuf, xsem, ysem, *, bm):
    e = pl.program_id(0)
    n_blocks = nblk_ref[e]
    first = blk0_ref[e]
    total = total_ref[0]
    rows_per_block = xbuf.shape[1]
    s_words = rows_per_block // bm
    n_tiles = w1s_ref.shape[-1] // MXU_COLS
    half = MXU_COLS // 2

    def x_copy(g, slot):
        return pltpu.make_async_copy(x_hbm.at[pl.ds(pl.multiple_of(g * rows_per_block, rows_per_block),
                                                    rows_per_block)], xbuf.at[slot], xsem.at[slot])

    def y_copy(g, slot):
        return pltpu.make_async_copy(ybuf.at[slot], y_hbm.at[pl.ds(pl.multiple_of(g * rows_per_block,
                                                                                   rows_per_block),
                                                                   rows_per_block)], ysem.at[slot])

    @pl.when(e == 0)
    def _():
        x_copy(0, 0).start()

    @pl.when(n_blocks > 0)
    def _():
        r = lax.broadcasted_iota(jnp.int32, (MXU_COLS, MXU_COLS), 0)
        c = lax.broadcasted_iota(jnp.int32, (MXU_COLS, MXU_COLS), 1)
        perm = (r == jnp.where(c < half, 2 * c, 2 * (c - half) + 1)).astype(BF16)
        for tix in range(n_tiles):
            cols = slice(tix * MXU_COLS, (tix + 1) * MXU_COLS)
            w1s_ref[:, cols] = _dot(w1_ref[0, :, cols].astype(BF16), perm).astype(BF16)
        w2s_ref[...] = w2_ref[0].astype(BF16)

        def block(j, carry):
            g = first + j
            slot = lax.rem(g, 2)
            x_copy(g, slot).wait()

            @pl.when(g + 1 < total)
            def _():
                x_copy(g + 1, 1 - slot).start()

            @pl.when(g >= 2)
            def _():
                y_copy(g - 2, slot).wait()

            n_rows = cnt_ref[e] - j * bm

            def ffn(m):
                words = _load_packed_rows(xbuf, (slot,), m, s_words)
                row = lax.broadcasted_iota(jnp.int32, words.shape, 0)
                x_lo, x_hi = _unpack_bf16_pair(jnp.where(row < n_rows, words, 0))
                x = jnp.concatenate([x_lo.astype(BF16), x_hi.astype(BF16)], axis=1)
                acts = []
                for tix in range(n_tiles):
                    hp = _dot(x, w1s_ref[:, tix * MXU_COLS:(tix + 1) * MXU_COLS])
                    feat = slice(tix * half, (tix + 1) * half)
                    gl = jnp.minimum(hp[:, :half] + b1g_ref[0][:, feat], SWIGLU_LIMIT)
                    lin = jnp.clip(hp[:, half:] + b1l_ref[0][:, feat], -SWIGLU_LIMIT, SWIGLU_LIMIT)
                    acts.append((gl * _sigmoid(SWIGLU_ALPHA * gl) * (lin + 1.0)).astype(BF16))
                y = _dot(jnp.concatenate(acts, axis=1), w2s_ref[...]) + b2_ref[0]
                d_half = y.shape[-1] // 2
                _store_packed_rows(ybuf.at[slot], _pack_bf16_pair(y[:, :d_half], y[:, d_half:]))

            @pl.when(n_rows > bm // 2)
            def _():
                ffn(bm)

            @pl.when(n_rows <= bm // 2)
            def _():
                ffn(bm // 2)

            y_copy(g, slot).start()
            return carry

        lax.fori_loop(0, n_blocks, block, 0)

    @pl.when(e == pl.num_programs(0) - 1)
    def _():
        @pl.when(total >= 2)
        def _():
            y_copy(total - 2, lax.rem(total, 2)).wait()

        y_copy(total - 1, lax.rem(total - 1, 2)).wait()


def _experts(tables, xs, w1, b1g, b1l, w2, b2, bm, d):
    s_words = d // 2 // LANES
    dff2 = w1.shape[-1]
    dff = dff2 // 2
    emap = lambda e, *_: (e, 0, 0)
    grid_spec = pltpu.PrefetchScalarGridSpec(
        num_scalar_prefetch=len(tables),
        grid=(N_EXPERTS,),
        in_specs=[pl.BlockSpec(memory_space=pl.ANY),
                  pl.BlockSpec((1, d, dff2), emap),
                  pl.BlockSpec((1, 1, dff), emap),
                  pl.BlockSpec((1, 1, dff), emap),
                  pl.BlockSpec((1, dff, d), emap),
                  pl.BlockSpec((1, 1, d), emap)],
        out_specs=pl.BlockSpec(memory_space=pl.ANY),
        scratch_shapes=[pltpu.VMEM((d, dff2), BF16), pltpu.VMEM((dff, d), BF16),
                        pltpu.VMEM((2, bm * s_words, LANES), jnp.int32),
                        pltpu.VMEM((2, bm * s_words, LANES), jnp.int32),
                        pltpu.SemaphoreType.DMA((2,)), pltpu.SemaphoreType.DMA((2,))],
    )
    return pl.pallas_call(
        functools.partial(_expert_kernel, bm=bm),
        grid_spec=grid_spec,
        out_shape=jax.ShapeDtypeStruct(xs.shape, jnp.int32),
        compiler_params=pltpu.CompilerParams(dimension_semantics=("arbitrary",),
                                             vmem_limit_bytes=VMEM_LIMIT),
        name="experts",
    )(*tables, xs, w1, b1g, b1l, w2, b2)


def _gather_rows(dest_flat, y3):
    n_out = dest_flat.shape[0]
    _, s, _ = y3.shape
    mesh = _sc_mesh()
    n_workers = mesh.num_cores * mesh.num_subcores
    per_worker = n_out // n_workers
    n_chunks = per_worker // SC_ROWS
    assert per_worker % SC_ROWS == 0 and n_chunks % 2 == 0

    @functools.partial(
        pl.kernel, mesh=mesh, out_type=jax.ShapeDtypeStruct((n_out, s, LANES), y3.dtype),
        scratch_types=[pltpu.VMEM((2, SC_ROWS), jnp.int32), pltpu.VMEM((2, SC_ROWS, s, LANES), y3.dtype),
                       pltpu.SemaphoreType.DMA((2,)), pltpu.SemaphoreType.DMA((2,))],
        name="sc_gather")
    def gather(y_hbm, dest_hbm, out_hbm, idx_v, rows_v, gsem, wsem):
        base = (lax.axis_index("s") * mesh.num_cores + lax.axis_index("c")) * per_worker

        def rows_of(chunk):
            return pl.ds(pl.multiple_of(base + chunk * SC_ROWS, SC_ROWS), SC_ROWS)

        def gather_copy(slot):
            return pltpu.make_async_copy(y_hbm.at[idx_v.at[slot]], rows_v.at[slot], gsem.at[slot])

        def write_copy(chunk, slot):
            return pltpu.make_async_copy(rows_v.at[slot], out_hbm.at[rows_of(chunk)], wsem.at[slot])

        def start_gather(chunk, slot):
            pltpu.sync_copy(dest_hbm.at[rows_of(chunk)], idx_v.at[slot])
            gather_copy(slot).start()

        start_gather(0, 0)

        @pl.loop(0, n_chunks // 2)
        def _(pair):
            for slot in (0, 1):
                chunk = 2 * pair + slot
                gather_copy(slot).wait()
                write_copy(chunk, slot).start()

                @pl.when(chunk >= 1)
                def _():
                    write_copy(chunk - 1, 1 - slot).wait()

                @pl.when(chunk + 1 < n_chunks)
                def _():
                    start_gather(chunk + 1, 1 - slot)

        write_copy(n_chunks - 1, 1).wait()

    return gather(y3, dest_flat)


def _combine_kernel(x1_ref, gate_ref, y4_ref, nw_ref, o_ref, *, tm, normalize):
    x = x1_ref[...]
    gates = gate_ref[...]
    for kk in range(TOP_K):
        lo, hi = _unpack_bf16_pair(_load_packed_rows(y4_ref, (kk,), tm, y4_ref.shape[1] // tm))
        x = x + gates[:, kk:kk + 1] * jnp.concatenate([lo, hi], axis=1)
    if normalize:
        x = x * lax.rsqrt(jnp.mean(x * x, axis=-1, keepdims=True) + EPS) * nw_ref[...]
    o_ref[...] = x


def _combine(x1, gates, y4, norm_w, tm, normalize):
    t, d = x1.shape
    s_words = d // 2 // LANES
    row = lambda i: (i, 0)
    return pl.pallas_call(
        functools.partial(_combine_kernel, tm=tm, normalize=normalize),
        grid=(t // tm,),
        in_specs=[pl.BlockSpec((tm, d), row), pl.BlockSpec((tm, LANES), row),
                  pl.BlockSpec((TOP_K, tm * s_words, LANES), lambda i: (0, i, 0)),
                  pl.BlockSpec((1, d), lambda i: (0, 0))],
        out_specs=pl.BlockSpec((tm, d), row),
        out_shape=jax.ShapeDtypeStruct((t, d), F32),
        compiler_params=pltpu.CompilerParams(dimension_semantics=("parallel",)),
        name="combine",
    )(x1, gates, y4, norm_w)


def _tiles(t, seq):
    tm = min(2 * MXU_COLS, t)
    tr = min(2 * MXU_COLS, t)
    tq = min(2 * MXU_COLS, seq)
    gdn_rows = min(4 * CHUNK, seq)
    bm = 2 * MXU_COLS
    assert t % tm == 0 and t % tr == 0 and seq % tq == 0 and seq % gdn_rows == 0 and tq % CHUNK == 0
    return tm, tr, tq, gdn_rows, bm


def _rot_half(w):
    half = w.shape[-1] // 2
    return jnp.concatenate([-w[..., half:], w[..., :half]], axis=-1)


def _prep_mla_weights(w_uq, w_ukv):
    dq = MLA_NOPE + MLA_ROPE
    wq = w_uq.reshape(MLA_Q_LORA, MLA_HEADS, dq)
    wq_rot = jnp.concatenate([jnp.zeros_like(wq[..., :MLA_NOPE]), _rot_half(wq[..., MLA_NOPE:])], axis=-1)
    pad = lambda w: jnp.pad(w, ((0, 0), (0, 0), (0, LANES - w.shape[-1]))).reshape(w.shape[0], C_HEADS)
    wkv = w_ukv.reshape(MLA_KV_LORA, MLA_HEADS, MLA_NOPE + MLA_V)
    wv = wkv[..., MLA_NOPE:].reshape(MLA_KV_LORA, C_V)
    return (pad(wq).T.astype(BF16), pad(wq_rot).T.astype(BF16), pad(wkv[..., :MLA_NOPE]).astype(BF16),
            wv.T.astype(BF16))


def _rope_freq():
    half = MLA_ROPE // 2
    return (ROPE_THETA ** (-jnp.arange(half, dtype=F32) / half)).reshape(half, 1)


def _lane_row(v):
    return jnp.pad(v.astype(F32), (0, LANES - v.shape[0])).reshape(1, LANES)


def _routing_tables(cnt, eidx, rank, bm):
    e_ids = jnp.arange(N_EXPERTS, dtype=jnp.int32)
    counts = cnt[:, 0].astype(jnp.int32)
    padded = (counts + bm - 1) // bm * bm
    pend = jnp.sum(jnp.where(e_ids[None, :] <= e_ids[:, None], padded[None, :], 0), axis=1)
    pstart = pend - padded
    dest = jnp.sum(jnp.where(eidx[:TOP_K, :, None] == e_ids, pstart, 0), axis=-1) + rank[:TOP_K]
    tables = (counts, pstart // bm, padded // bm, pend[-1:] // bm)
    return dest.reshape(-1), tuple(t.astype(jnp.int32) for t in tables)


def kernel(x, positions, norm_mix_w, w_in, conv_w, a_log, dt_bias, gdn_norm_w, q_norm_w, w_uq, kv_norm_w,
           w_ukv, mla_out_norm_w, w_out, norm_ffn_w, router_w, router_b, w1, b1, w2, b2, norm_final_w):
    bsz, seq, d = x.shape
    t = bsz * seq
    depth = w_in.shape[0]
    tm, tr, tq, gdn_rows, bm = _tiles(t, seq)
    x2 = x.reshape(t, d)
    posf = positions.astype(F32)
    freq = _rope_freq()
    for l in range(depth):
        qkv, z, cq, ckv, kr2, ab = _inproj(x2, norm_mix_w[l].reshape(1, d), w_in[l], tm)
        conv_w8 = jnp.pad(conv_w[l], ((0, 8 - GDN_CONV), (0, 0)))
        o_a = _gdn(qkv.reshape(bsz, seq, C_QKV), z.reshape(bsz, seq, C_Z), ab.reshape(bsz, seq, LANES),
                   conv_w8, _lane_row(a_log[l]), _lane_row(dt_bias[l]), gdn_norm_w[l].reshape(1, GDN_D),
                   gdn_rows).reshape(t, C_Z)
        wqt, wqrt, wk, wvt = _prep_mla_weights(w_uq[l], w_ukv[l])
        qt, k, vt = _mla_proj(cq, ckv, kr2, posf.reshape(1, t),
                              q_norm_w[l].reshape(1, -1), kv_norm_w[l].reshape(1, -1),
                              wqt, wqrt, wk, wvt, freq, tq)
        o_bt = _attention(qt, k, vt, mla_out_norm_w[l].reshape(-1, 1), bsz, seq, tq)
        x1, hn, eidx, gates, rank, cnt = _router(x2, o_a, o_bt, w_out[l].astype(BF16),
                                                 norm_ffn_w[l].reshape(1, d), router_w[l].T,
                                                 router_b[l].reshape(N_EXPERTS, 1), tr)
        n_pad = t * TOP_K + N_EXPERTS * bm
        dest_flat, tables = _routing_tables(cnt, eidx, rank, bm)
        s_words = d // 2 // LANES
        dest_chunks = dest_flat.reshape(TOP_K, t // SC_ROWS, SC_ROWS).transpose(1, 0, 2).reshape(-1)
        buf = _dispatch(dest_chunks, hn.reshape(t, s_words, LANES), n_pad)
        yb = _experts(tables, buf.reshape(n_pad * s_words, LANES), w1[l],
                      b1[l][:, None, 0::2], b1[l][:, None, 1::2], w2[l], b2[l][:, None, :], bm, d)
        y4 = _gather_rows(dest_flat, yb.reshape(n_pad, s_words, LANES))
        x2 = _combine(x1, gates, y4.reshape(TOP_K, t * s_words, LANES), norm_final_w.reshape(1, d), tr,
                      normalize=(l == depth - 1))
    return x2.reshape(bsz, seq, d)
```

```python
import functools
import math

import jax
import jax.numpy as jnp
from jax import lax
from jax.experimental import pallas as pl
from jax.experimental.pallas import tpu as pltpu
from jax.experimental.pallas import tpu_sc as plsc

F32 = jnp.float32
BF16 = jnp.bfloat16
HIGHEST = lax.Precision.HIGHEST

LANES = 128
MXU_COLS = 256
EPS = 1e-6
CHUNK = 64

GDN_HEADS = 4
GDN_D = 128
GDN_CONV = 4
MLA_HEADS = 8
MLA_NOPE = 64
MLA_ROPE = 32
MLA_V = 64
MLA_Q_LORA = 384
MLA_KV_LORA = 256
ROPE_THETA = 10000.0
N_EXPERTS = 32
TOP_K = 4
SWIGLU_LIMIT = 7.0
SWIGLU_ALPHA = 1.702

NEG_BIG = -1e30
VMEM_LIMIT = 52 * 1024 * 1024


def _sigmoid(x):
    return 1.0 / (1.0 + jnp.exp(-x))


def _softplus(x):
    return jnp.maximum(x, 0.0) + jnp.log(1.0 + jnp.exp(-jnp.abs(x)))


def _dot(a, b, precision=None):
    return jnp.dot(a, b, preferred_element_type=F32, precision=precision)


def _dot_nt(a, b, precision=None):
    return lax.dot_general(a, b, (((1,), (1,)), ((), ())), preferred_element_type=F32,
                           precision=precision)


def _dot_tn(a, b):
    return lax.dot_general(a, b, (((0,), (0,)), ((), ())), preferred_element_type=F32)


def _split_bf16(a):
    hi = a.astype(BF16)
    return hi, (a - hi.astype(F32)).astype(BF16)


def _pack_bf16_pair(lo, hi):
    lo_bits = pltpu.bitcast(lo.astype(BF16).astype(F32), jnp.int32)
    hi_bits = pltpu.bitcast(hi.astype(BF16).astype(F32), jnp.int32)
    return jnp.bitwise_or(hi_bits, lax.shift_right_logical(lo_bits, 16))


def _unpack_bf16_pair(w):
    lo = pltpu.bitcast(lax.shift_left(w, 16), F32)
    hi = pltpu.bitcast(jnp.bitwise_and(w, -65536), F32)
    return lo, hi


def _load_packed_rows(ref, lead, m, s):
    return jnp.concatenate([ref[(*lead, pl.ds(j, m, stride=s), slice(None))] for j in range(s)], axis=1)


def _store_packed_rows(ref, words):
    m = words.shape[0]
    s = words.shape[1] // LANES
    for j in range(s):
        ref[pl.ds(j, m, stride=s), :] = words[:, j * LANES:(j + 1) * LANES]


def _dot_split_nt(a, b):
    return _dot_nt(a[0], b[0]) + _dot_nt(a[0], b[1]) + _dot_nt(a[1], b[0])


def _dot_split(a, b):
    return _dot(a[0], b[0]) + _dot(a[0], b[1]) + _dot(a[1], b[0])


C_QKV = 3 * GDN_HEADS * GDN_D
C_Z = GDN_HEADS * GDN_D
C_IN = C_QKV + C_Z + MLA_Q_LORA + MLA_KV_LORA + 3 * LANES


def _inproj_kernel(x_ref, nw_ref, w_ref, qkv_ref, z_ref, cq_ref, ckv_ref, kr_ref, ab_ref, wb_ref):
    @pl.when(pl.program_id(0) == 0)
    def _():
        d = w_ref.shape[0]
        hd4 = C_QKV + C_Z
        wb_ref[:, 0:hd4] = w_ref[:, 0:hd4].astype(BF16)
        tail = w_ref[:, hd4:]
        o_cq = 2 * GDN_HEADS
        o_ckv = o_cq + MLA_Q_LORA
        o_kr = o_ckv + MLA_KV_LORA
        kr = tail[:, o_kr:o_kr + MLA_ROPE]
        half = MLA_ROPE // 2
        zeros = lambda n: jnp.zeros((d, n), F32)
        pad = LANES - MLA_NOPE - MLA_ROPE
        groups = [tail[:, o_cq:o_ckv], tail[:, o_ckv:o_kr],
                  jnp.concatenate([zeros(MLA_NOPE), kr, zeros(pad)], axis=1),
                  jnp.concatenate([zeros(MLA_NOPE), -kr[:, half:], kr[:, :half], zeros(pad)], axis=1),
                  jnp.concatenate([tail[:, 0:o_cq], zeros(LANES - o_cq)], axis=1)]
        o = hd4
        for grp in groups:
            wb_ref[:, o:o + grp.shape[1]] = grp.astype(BF16)
            o += grp.shape[1]

    x = x_ref[...]
    var = jnp.mean(x * x, axis=-1, keepdims=True)
    h = (x * lax.rsqrt(var + EPS) * nw_ref[...]).astype(BF16)
    p = _dot(h, wb_ref[...])
    o = 0
    for ref in (qkv_ref, z_ref, cq_ref, ckv_ref, kr_ref, ab_ref):
        n = ref.shape[-1]
        ref[...] = p[:, o:o + n]
        o += n


def _inproj(x2, norm_w, w_in, tm):
    t, d = x2.shape
    widths = (C_QKV, C_Z, MLA_Q_LORA, MLA_KV_LORA, 2 * LANES, LANES)
    return pl.pallas_call(
        _inproj_kernel,
        grid=(t // tm,),
        in_specs=[pl.BlockSpec((tm, d), lambda i: (i, 0)),
                  pl.BlockSpec((1, d), lambda i: (0, 0)),
                  pl.BlockSpec(w_in.shape, lambda i: (0, 0), pipeline_mode=pl.Buffered(1))],
        out_specs=[pl.BlockSpec((tm, n), lambda i: (i, 0)) for n in widths],
        out_shape=[jax.ShapeDtypeStruct((t, n), F32) for n in widths],
        scratch_shapes=[pltpu.VMEM((d, C_IN), BF16)],
        compiler_params=pltpu.CompilerParams(dimension_semantics=("arbitrary",),
                                             vmem_limit_bytes=VMEM_LIMIT),
        name="inproj",
    )(x2, norm_w, w_in)


def _gdn_kernel(qkv_ref, z_ref, ab_ref, cw_ref, alog_ref, dtb_ref, nw_ref, o_ref, ext_ref, s_ref):
    c = CHUNK
    bsz, rows, _ = qkv_ref.shape
    chunks_per_batch = rows // c

    @pl.when(pl.program_id(0) == 0)
    def _():
        ext_ref[:, 0:8, :] = jnp.zeros((bsz, 8, C_QKV), F32)
        s_ref[...] = jnp.zeros_like(s_ref)

    cw = cw_ref[...]
    nw = nw_ref[...]
    rr = lax.broadcasted_iota(jnp.int32, (rows, rows), 0)
    rc = lax.broadcasted_iota(jnp.int32, (rows, rows), 1)
    in_chunk_prefix = jnp.where(rr // c == rc // c, jnp.where(rr >= rc, 1.0, 0.0), 0.0)
    e8 = (lax.broadcasted_iota(jnp.int32, (8, LANES), 0)
          == lax.broadcasted_iota(jnp.int32, (8, LANES), 1)).astype(F32)
    ri = lax.broadcasted_iota(jnp.int32, (c, c), 0)
    ci = lax.broadcasted_iota(jnp.int32, (c, c), 1)
    causal = ri >= ci
    strict = ri > ci
    eye = (ri == ci).astype(F32)
    chains = [(j, h) for j in range(chunks_per_batch) for h in range(GDN_HEADS)]

    def prep_pieces(b):
        g = {"xc": [None] * (C_QKV // LANES)}

        def conv(tiles):
            def run():
                cols = slice(tiles[0] * LANES, (tiles[-1] + 1) * LANES)
                x = qkv_ref[b, :, cols]
                ext_ref[b, 8:8 + rows, cols] = x
                ext = ext_ref[b, :, cols]
                xb = cw[3:4, cols] * x
                for j in range(1, GDN_CONV):
                    xb = xb + cw[3 - j:4 - j, cols] * pltpu.roll(ext, j, axis=0)[8:8 + rows, :]
                ext_ref[b, 0:8, cols] = x[rows - 8:rows, :]
                xb = xb * _sigmoid(xb)
                for i, tile in enumerate(tiles):
                    g["xc"][tile] = xb[:, i * LANES:(i + 1) * LANES]
            return run

        def gates():
            ab = ab_ref[b]
            g_all = -jnp.exp(alog_ref[...]) * _softplus(ab + dtb_ref[...])
            g["beta"] = _sigmoid(ab)
            g["g_cum"] = _dot(in_chunk_prefix, g_all, HIGHEST)
            g["g_row"] = _dot_nt(e8, g["g_cum"], HIGHEST)
            for name in ("q", "k", "v", "gc", "bt", "dec"):
                g[name] = []

        def chunk(j):
            def run():
                rs = slice(j * c, (j + 1) * c)
                xc = g["xc"]
                for h in range(GDN_HEADS):
                    q = xc[h][rs]
                    k = xc[GDN_HEADS + h][rs]
                    g["q"].append(q * lax.rsqrt(jnp.sum(q * q, axis=-1, keepdims=True) + EPS) * (GDN_D ** -0.5))
                    g["k"].append(k * lax.rsqrt(jnp.sum(k * k, axis=-1, keepdims=True) + EPS))
                    g["v"].append(xc[2 * GDN_HEADS + h][rs])
                    gc = g["g_cum"][rs, h:h + 1]
                    gr = g["g_row"][h:h + 1, rs]
                    g["gc"].append(gc)
                    g["bt"].append(g["beta"][rs, GDN_HEADS + h:GDN_HEADS + h + 1])
                    g["dec"].append(jnp.exp(jnp.where(causal, gc - gr, -jnp.inf)))
            return run

        def a_mats():
            g["kb"] = [k * bt for k, bt in zip(g["k"], g["bt"])]
            g["k16"] = [k.astype(BF16) for k in g["k"]]
            g["a"] = [jnp.where(strict, _dot_nt(kb.astype(BF16), k16) * dec, 0.0)
                      for kb, k16, dec in zip(g["kb"], g["k16"], g["dec"])]

        n_tiles = C_QKV // LANES
        conv_pieces = [conv(list(range(i, i + 2))) for i in range(0, n_tiles, 2)]
        return g, conv_pieces + [gates] + [chunk(j) for j in range(chunks_per_batch)] + [a_mats]

    def inverse_rounds(g):
        def start():
            g["t"] = [eye - a for a in g["a"]]
            g["pw"] = [_split_bf16(a) for a in g["a"]]

        def square():
            g["pw"] = [_split_bf16(_dot_split(p, p)) for p in g["pw"]]

        def update():
            g["t"] = [t + _dot_split(_split_bf16(t), p) for t, p in zip(g["t"], g["pw"])]

        return [start] + [square, update] * (CHUNK.bit_length() - 2)

    groups = []
    pending = []
    for b in range(bsz):
        g, pieces = prep_pieces(b)
        groups.append(g)
        n_a, n_b = len(pending), len(pieces)
        ia = ib = 0
        while ia < n_a or ib < n_b:
            if ib >= n_b or (ia < n_a and ia * n_b <= ib * n_a):
                pending[ia]()
                ia += 1
            else:
                pieces[ib]()
                ib += 1
        pending = inverse_rounds(g)
    for fn in pending:
        fn()

    uws, qk16s, qd16s, kd16s, g_tots = [], [], [], [], []
    for g in groups:
        e_gs = [jnp.exp(gc) for gc in g["gc"]]
        uws.append([_dot(t.astype(BF16), jnp.concatenate([v * bt, kb * eg], axis=1).astype(BF16))
                    for t, v, bt, kb, eg in zip(g["t"], g["v"], g["bt"], g["kb"], e_gs)])
        qk16s.append([(_dot_nt(q.astype(BF16), k16) * dec).astype(BF16)
                      for q, k16, dec in zip(g["q"], g["k16"], g["dec"])])
        qd16s.append([(q * eg).astype(BF16) for q, eg in zip(g["q"], e_gs)])
        kd16s.append([(k * jnp.exp(gc[c - 1:c, :] - gc)).astype(BF16) for k, gc in zip(g["k"], g["gc"])])
        g_tots.append([jnp.exp(gc[c - 1:c, :]) for gc in g["gc"]])

    pairs = [(b, h) for b in range(bsz) for h in range(GDN_HEADS)]
    states = [s_ref[b * GDN_HEADS + h] for b, h in pairs]
    for j in range(chunks_per_batch):
        rs = slice(j * c, (j + 1) * c)
        pick = lambda per_group: [per_group[b][chains.index((j, h))] for b, h in pairs]
        uw_j, qk_j, qd_j, kd_j, gt_j = (pick(x) for x in (uws, qk16s, qd16s, kd16s, g_tots))
        s16s = [s.astype(BF16) for s in states]
        v16s = [(uw[:, :GDN_D] - _dot(uw[:, GDN_D:].astype(BF16), s16)).astype(BF16)
                for uw, s16 in zip(uw_j, s16s)]
        states = [s * gt + _dot_tn(kd, v16) for s, gt, kd, v16 in zip(states, gt_j, kd_j, v16s)]
        os_ = [_dot(qd, s16) + _dot(qk, v16) for qd, qk, s16, v16 in zip(qd_j, qk_j, s16s, v16s)]
        for (b, h), o in zip(pairs, os_):
            on = o * lax.rsqrt(jnp.mean(o * o, axis=-1, keepdims=True) + EPS) * nw
            zh = z_ref[b, rs, h * GDN_D:(h + 1) * GDN_D]
            o_ref[b, rs, h * GDN_D:(h + 1) * GDN_D] = on * (zh * _sigmoid(zh))
    for p, s in enumerate(states):
        s_ref[p] = s


def _gdn(qkv, z, ab, conv_w8, alog_row, dtb_row, norm_w, rows):
    bsz, seq, _ = qkv.shape
    blk = lambda i: (0, i, 0)
    const = lambda i: (0, 0)
    return pl.pallas_call(
        _gdn_kernel,
        grid=(seq // rows,),
        in_specs=[pl.BlockSpec((bsz, rows, C_QKV), blk),
                  pl.BlockSpec((bsz, rows, C_Z), blk),
                  pl.BlockSpec((bsz, rows, LANES), blk),
                  pl.BlockSpec((8, C_QKV), const),
                  pl.BlockSpec((1, LANES), const),
                  pl.BlockSpec((1, LANES), const),
                  pl.BlockSpec((1, GDN_D), const)],
        out_specs=pl.BlockSpec((bsz, rows, C_Z), blk),
        out_shape=jax.ShapeDtypeStruct((bsz, seq, C_Z), F32),
        scratch_shapes=[pltpu.VMEM((bsz, 8 + rows, C_QKV), F32),
                        pltpu.VMEM((bsz * GDN_HEADS, GDN_D, GDN_D), F32)],
        compiler_params=pltpu.CompilerParams(dimension_semantics=("arbitrary",),
                                             vmem_limit_bytes=VMEM_LIMIT),
        name="gdn",
    )(qkv, z, ab, conv_w8, alog_row, dtb_row, norm_w)


D_QK_PAD = LANES
C_HEADS = MLA_HEADS * D_QK_PAD
C_V = MLA_HEADS * MLA_V
V_AUG = MLA_V + 16
C_VAUG = MLA_HEADS * V_AUG
Q_SCALE = (MLA_NOPE + MLA_ROPE) ** -0.5 * math.log2(math.e)


def _mla_proj_kernel(cq_ref, ckv_ref, kr_ref, posr_ref, qnw_ref, kvnw_ref, wqt_ref, wqrt_ref,
                     wk_ref, wvt_ref, freqc_ref, qt_ref, k_ref, vt_ref):
    cq = cq_ref[...]
    cqn = (cq * lax.rsqrt(jnp.mean(cq * cq, axis=-1, keepdims=True) + EPS) * qnw_ref[...]).astype(BF16)
    ckv = ckv_ref[...]
    ckvn = (ckv * lax.rsqrt(jnp.mean(ckv * ckv, axis=-1, keepdims=True) + EPS)
            * kvnw_ref[...]).astype(BF16)
    tm = posr_ref.shape[-1]
    ang = freqc_ref[...] * posr_ref[...]
    c16 = jnp.cos(ang)
    s16 = jnp.sin(ang)
    pad = LANES - MLA_NOPE - MLA_ROPE
    cs_t = jnp.concatenate([jnp.ones((MLA_NOPE, tm), F32), c16, c16, jnp.ones((pad, tm), F32)], axis=0)
    sn_t = jnp.concatenate([jnp.zeros((MLA_NOPE, tm), F32), s16, s16, jnp.zeros((pad, tm), F32)], axis=0)
    cs = cs_t.T
    sn = sn_t.T
    qa_t = _dot_nt(wqt_ref[...], cqn)
    qb_t = _dot_nt(wqrt_ref[...], cqn)
    kn = _dot(ckvn, wk_ref[...])
    kr = kr_ref[...]
    kpe = kr[:, :LANES] * cs + kr[:, LANES:] * sn
    for h in range(MLA_HEADS):
        sl = slice(h * LANES, (h + 1) * LANES)
        qt_ref[sl, :] = ((qa_t[sl, :] * cs_t + qb_t[sl, :] * sn_t) * Q_SCALE).astype(BF16)
        k_ref[:, sl] = (kn[:, sl] + kpe).astype(BF16)
    v_t = _dot_nt(wvt_ref[...], ckvn).astype(BF16)
    ones = jnp.ones((V_AUG - MLA_V, v_t.shape[1]), BF16)
    for h in range(MLA_HEADS):
        vt_ref[0, h * V_AUG:h * V_AUG + MLA_V, :] = v_t[h * MLA_V:(h + 1) * MLA_V, :]
        vt_ref[0, h * V_AUG + MLA_V:(h + 1) * V_AUG, :] = ones


def _mla_proj(cq, ckv, kr2, posr, qnw, kvnw, wqt, wqrt, wk, wvt, freqc, tm):
    t = cq.shape[0]
    row = lambda i: (i, 0)
    col = lambda i: (0, i)
    const = lambda i: (0, 0)
    return pl.pallas_call(
        _mla_proj_kernel,
        grid=(t // tm,),
        in_specs=[pl.BlockSpec((tm, MLA_Q_LORA), row),
                  pl.BlockSpec((tm, MLA_KV_LORA), row),
                  pl.BlockSpec((tm, 2 * LANES), row),
                  pl.BlockSpec((1, tm), col),
                  pl.BlockSpec((1, MLA_Q_LORA), const),
                  pl.BlockSpec((1, MLA_KV_LORA), const),
                  pl.BlockSpec((C_HEADS, MLA_Q_LORA), const),
                  pl.BlockSpec((C_HEADS, MLA_Q_LORA), const),
                  pl.BlockSpec((MLA_KV_LORA, C_HEADS), const),
                  pl.BlockSpec((C_V, MLA_KV_LORA), const),
                  pl.BlockSpec((MLA_ROPE // 2, 1), const)],
        out_specs=[pl.BlockSpec((C_HEADS, tm), col),
                   pl.BlockSpec((tm, C_HEADS), row),
                   pl.BlockSpec((1, C_VAUG, tm), lambda i: (i, 0, 0))],
        out_shape=[jax.ShapeDtypeStruct((C_HEADS, t), BF16),
                   jax.ShapeDtypeStruct((t, C_HEADS), BF16),
                   jax.ShapeDtypeStruct((t // tm, C_VAUG, tm), BF16)],
        compiler_params=pltpu.CompilerParams(dimension_semantics=("parallel",),
                                             vmem_limit_bytes=VMEM_LIMIT),
        name="mla_proj",
    )(cq, ckv, kr2, posr, qnw, kvnw, wqt, wqrt, wk, wvt, freqc)


def _attn_kernel(qt_ref, k_ref, vt_ref, nw_ref, ot_ref, m_ref, acc_ref, *, tq):
    i = pl.program_id(1)
    m_ref[...] = jnp.full(m_ref.shape, NEG_BIG, F32)
    acc_ref[...] = jnp.zeros(acc_ref.shape, F32)
    def tile(j, k_lo, n_keys, q_lo, masked):
        rows = pl.ds(pl.multiple_of(j * tq + k_lo, n_keys), n_keys)
        qs = slice(q_lo, tq)
        if masked:
            key_chunk = (k_lo + lax.broadcasted_iota(jnp.int32, (n_keys, tq - q_lo), 0)) // CHUNK
            qry_chunk = (q_lo + lax.broadcasted_iota(jnp.int32, (n_keys, tq - q_lo), 1)) // CHUNK
            mask = key_chunk <= qry_chunk
        scores = []
        for h in range(MLA_HEADS):
            hs = slice(h * LANES, (h + 1) * LANES)
            s_t = _dot(k_ref[rows, hs], qt_ref[hs, qs])
            scores.append(jnp.where(mask, s_t, NEG_BIG) if masked else s_t)
        probs, alphas = [], []
        for h in range(MLA_HEADS):
            m_old = m_ref[h, :, qs]
            m_new = jnp.maximum(m_old, jnp.max(scores[h], axis=0, keepdims=True))
            m_ref[h, :, qs] = m_new
            probs.append(jnp.exp2(scores[h] - m_new[0:1, :]).astype(BF16))
            alphas.append(jnp.exp2(m_old[0:1, :] - m_new[0:1, :]))
        for h in range(MLA_HEADS):
            v_aug = vt_ref[j, h * V_AUG:(h + 1) * V_AUG, k_lo:k_lo + n_keys]
            acc_ref[h, :, qs] = alphas[h] * acc_ref[h, :, qs] + _dot(v_aug, probs[h])

    def full_tile(j, carry):
        tile(j, 0, tq, 0, False)
        return carry

    lax.fori_loop(0, i, full_tile, 0)
    half = tq // 2
    tile(i, 0, half, 0, True)
    tile(i, half, half, half, True)

    outs = []
    for h in range(MLA_HEADS):
        acc = acc_ref[h]
        outs.append(acc[:MLA_V, :] / acc[MLA_V:MLA_V + 1, :])
    ssq = outs[0] * outs[0]
    for o in outs[1:]:
        ssq = ssq + o * o
    inv = lax.rsqrt(jnp.sum(ssq, axis=0, keepdims=True) / C_V + EPS)
    for h in range(MLA_HEADS):
        vs = slice(h * MLA_V, (h + 1) * MLA_V)
        ot_ref[vs, :] = outs[h] * inv * nw_ref[vs, :]


def _attention(qt, k, vt, norm_w_col, bsz, seq, tq):
    nq = seq // tq
    return pl.pallas_call(
        functools.partial(_attn_kernel, tq=tq),
        grid=(bsz, nq),
        in_specs=[pl.BlockSpec((C_HEADS, tq), lambda b, i: (0, b * nq + i)),
                  pl.BlockSpec((seq, C_HEADS), lambda b, i: (b, 0)),
                  pl.BlockSpec((nq, C_VAUG, tq), lambda b, i: (b, 0, 0)),
                  pl.BlockSpec((C_V, 1), lambda b, i: (0, 0))],
        out_specs=pl.BlockSpec((C_V, tq), lambda b, i: (0, b * nq + i)),
        out_shape=jax.ShapeDtypeStruct((C_V, bsz * seq), F32),
        scratch_shapes=[pltpu.VMEM((MLA_HEADS, 8, tq), F32),
                        pltpu.VMEM((MLA_HEADS, V_AUG, tq), F32)],
        compiler_params=pltpu.CompilerParams(dimension_semantics=("parallel", "parallel"),
                                             vmem_limit_bytes=VMEM_LIMIT),
        name="attention",
    )(qt, k, vt, norm_w_col)


def _router_kernel(x_ref, oa_ref, obt_ref, wo_ref, nw_ref, rw_ref, rb_ref,
                   x1_ref, h_ref, eidx_ref, gate_ref, rank_ref, cnt_ref, run_ref, *, tm):
    @pl.when(pl.program_id(0) == 0)
    def _():
        run_ref[...] = jnp.zeros_like(run_ref)

    half = oa_ref.shape[-1]
    y = (_dot(oa_ref[...].astype(BF16), wo_ref[0:half, :])
         + _dot_tn(obt_ref[...].astype(BF16), wo_ref[half:, :]))
    x1 = x_ref[...] + y
    x1_ref[...] = x1
    hn = x1 * lax.rsqrt(jnp.mean(x1 * x1, axis=-1, keepdims=True) + EPS) * nw_ref[...]
    d_half = hn.shape[-1] // 2
    _store_packed_rows(h_ref, _pack_bf16_pair(hn[:, :d_half], hn[:, d_half:]))
    lg = _dot_split_nt(_split_bf16(rw_ref[...]), _split_bf16(hn)) + rb_ref[...]
    expert = lax.broadcasted_iota(jnp.int32, lg.shape, 0)
    sels, tops, idxs = [], [], []
    for _ in range(TOP_K):
        m = jnp.max(lg, axis=0, keepdims=True)
        idx = jnp.min(jnp.where(lg == m, expert, N_EXPERTS), axis=0, keepdims=True)
        sel = expert == idx
        lg = jnp.where(sel, -jnp.inf, lg)
        sels.append(sel)
        tops.append(m)
        idxs.append(idx)
    exps = [jnp.exp(tv - tops[0]) for tv in tops]
    den = functools.reduce(lambda a, b: a + b, exps)
    multi = sels[0].astype(F32)
    for sel in sels[1:]:
        multi = multi + sel.astype(F32)
    ri = lax.broadcasted_iota(jnp.int32, (tm, tm), 0)
    ci = lax.broadcasted_iota(jnp.int32, (tm, tm), 1)
    before = _dot(multi.astype(BF16), (ri < ci).astype(BF16)) + run_ref[...]
    ranks = [jnp.sum(jnp.where(sel, before, 0.0), axis=0, keepdims=True).astype(jnp.int32) for sel in sels]
    run = run_ref[...] + jnp.sum(multi, axis=1, keepdims=True)
    run_ref[...] = run
    cnt_ref[...] = jnp.broadcast_to(run, cnt_ref.shape)
    pad_i = jnp.zeros((8 - TOP_K, tm), jnp.int32)
    eidx_ref[...] = jnp.concatenate(idxs + [pad_i], axis=0)
    rank_ref[...] = jnp.concatenate(ranks + [pad_i], axis=0)
    gates_t = jnp.concatenate([e / den for e in exps] + [jnp.zeros((LANES - TOP_K, tm), F32)], axis=0)
    gate_ref[...] = gates_t.T


def _router(x2, oa, obt, w_out, norm_w, rw, rb, tm):
    t, d = x2.shape
    half = oa.shape[-1]
    s_words = d // 2 // LANES
    row = lambda i: (i, 0)
    col = lambda i: (0, i)
    const = lambda i: (0, 0)
    return pl.pallas_call(
        functools.partial(_router_kernel, tm=tm),
        grid=(t // tm,),
        in_specs=[pl.BlockSpec((tm, d), row),
                  pl.BlockSpec((tm, half), row),
                  pl.BlockSpec((half, tm), lambda i: (0, i)),
                  pl.BlockSpec((2 * half, d), const),
                  pl.BlockSpec((1, d), const),
                  pl.BlockSpec((N_EXPERTS, d), const),
                  pl.BlockSpec((N_EXPERTS, 1), const)],
        out_specs=[pl.BlockSpec((tm, d), row), pl.BlockSpec((tm * s_words, LANES), row),
                   pl.BlockSpec((8, tm), col), pl.BlockSpec((tm, LANES), row),
                   pl.BlockSpec((8, tm), col), pl.BlockSpec((N_EXPERTS, LANES), const)],
        out_shape=[jax.ShapeDtypeStruct((t, d), F32), jax.ShapeDtypeStruct((t * s_words, LANES), jnp.int32),
                   jax.ShapeDtypeStruct((8, t), jnp.int32), jax.ShapeDtypeStruct((t, LANES), F32),
                   jax.ShapeDtypeStruct((8, t), jnp.int32), jax.ShapeDtypeStruct((N_EXPERTS, LANES), F32)],
        scratch_shapes=[pltpu.VMEM((N_EXPERTS, 1), F32)],
        compiler_params=pltpu.CompilerParams(dimension_semantics=("arbitrary",),
                                             vmem_limit_bytes=VMEM_LIMIT),
        name="router",
    )(x2, oa, obt, w_out, norm_w, rw, rb)


SC_ROWS = 64


def _sc_mesh():
    return plsc.VectorSubcoreMesh(core_axis_name="c", subcore_axis_name="s")


def _dispatch(dest_chunks, h3, n_pad):
    t, s, _ = h3.shape
    mesh = _sc_mesh()
    n_workers = mesh.num_cores * mesh.num_subcores
    per_worker = t // n_workers
    n_chunks = per_worker // SC_ROWS
    assert per_worker % SC_ROWS == 0 and n_chunks % 2 == 0

    @functools.partial(
        pl.kernel, mesh=mesh, out_type=jax.ShapeDtypeStruct((n_pad, s, LANES), h3.dtype),
        scratch_types=[pltpu.VMEM((2, TOP_K * SC_ROWS), jnp.int32), pltpu.VMEM((2, SC_ROWS, s, LANES), h3.dtype),
                       pltpu.SemaphoreType.DMA((2,)), pltpu.SemaphoreType.DMA((2,)), pltpu.SemaphoreType.DMA],
        name="sc_dispatch")
    def scatter(h_hbm, dest_hbm, out_hbm, idx_v, rows_v, rsem, isem, ssem):
        base = (lax.axis_index("s") * mesh.num_cores + lax.axis_index("c")) * per_worker

        def loads(chunk, slot):
            off = pl.multiple_of(base + chunk * SC_ROWS, SC_ROWS)
            return (pltpu.make_async_copy(h_hbm.at[pl.ds(off, SC_ROWS)], rows_v.at[slot], rsem.at[slot]),
                    pltpu.make_async_copy(dest_hbm.at[pl.ds(off * TOP_K, SC_ROWS * TOP_K)], idx_v.at[slot],
                                          isem.at[slot]))

        for cp in loads(0, 0):
            cp.start()

        @pl.loop(0, n_chunks // 2)
        def _(pair):
            for slot in (0, 1):
                chunk = 2 * pair + slot
                for cp in loads(chunk, slot):
                    cp.wait()

                @pl.when(chunk + 1 < n_chunks)
                def _():
                    for cp in loads(chunk + 1, 1 - slot):
                        cp.start()

                copies = [pltpu.async_copy(rows_v.at[slot],
                                           out_hbm.at[idx_v.at[slot, pl.ds(kk * SC_ROWS, SC_ROWS)]], ssem)
                          for kk in range(TOP_K)]
                for cp in copies:
                    cp.wait()

    return scatter(h3, dest_chunks)


def _expert_kernel(cnt_ref, blk0_ref, nblk_ref, total_ref, x_hbm, w1_ref, b1g_ref, b1l_ref, w2_ref, b2_ref,
                   y_hbm, w1s_ref, w2s_ref, xbuf, ybuf, xsem, ysem, *, bm):
    e = pl.program_id(0)
    n_blocks = nblk_ref[e]
    first = blk0_ref[e]
    total = total_ref[0]
    rows_per_block = xbuf.shape[1]
    s_words = rows_per_block // bm
    n_tiles = w1s_ref.shape[-1] // MXU_COLS
    half = MXU_COLS // 2

    def x_copy(g, slot):
        return pltpu.make_async_copy(x_hbm.at[pl.ds(pl.multiple_of(g * rows_per_block, rows_per_block),
                                                    rows_per_block)], xbuf.at[slot], xsem.at[slot])

    def y_copy(g, slot):
        return pltpu.make_async_copy(ybuf.at[slot], y_hbm.at[pl.ds(pl.multiple_of(g * rows_per_block,
                                                                                   rows_per_block),
                                                                   rows_per_block)], ysem.at[slot])

    @pl.when(e == 0)
    def _():
        x_copy(0, 0).start()

    @pl.when(n_blocks > 0)
    def _():
        r = lax.broadcasted_iota(jnp.int32, (MXU_COLS, MXU_COLS), 0)
        c = lax.broadcasted_iota(jnp.int32, (MXU_COLS, MXU_COLS), 1)
        perm = (r == jnp.where(c < half, 2 * c, 2 * (c - half) + 1)).astype(BF16)
        for tix in range(n_tiles):
            cols = slice(tix * MXU_COLS, (tix + 1) * MXU_COLS)
            w1s_ref[:, cols] = _dot(w1_ref[0, :, cols].astype(BF16), perm).astype(BF16)
        w2s_ref[...] = w2_ref[0].astype(BF16)

        def block(j, carry):
            g = first + j
            slot = lax.rem(g, 2)
            x_copy(g, slot).wait()

            @pl.when(g + 1 < total)
            def _():
                x_copy(g + 1, 1 - slot).start()

            @pl.when(g >= 2)
            def _():
                y_copy(g - 2, slot).wait()

            n_rows = cnt_ref[e] - j * bm

            def ffn(m):
                words = _load_packed_rows(xbuf, (slot,), m, s_words)
                row = lax.broadcasted_iota(jnp.int32, words.shape, 0)
                x_lo, x_hi = _unpack_bf16_pair(jnp.where(row < n_rows, words, 0))
                x = jnp.concatenate([x_lo.astype(BF16), x_hi.astype(BF16)], axis=1)
                acts = []
                for tix in range(n_tiles):
                    hp = _dot(x, w1s_ref[:, tix * MXU_COLS:(tix + 1) * MXU_COLS])
                    feat = slice(tix * half, (tix + 1) * half)
                    gl = jnp.minimum(hp[:, :half] + b1g_ref[0][:, feat], SWIGLU_LIMIT)
                    lin = jnp.clip(hp[:, half:] + b1l_ref[0][:, feat], -SWIGLU_LIMIT, SWIGLU_LIMIT)
                    acts.append((gl * _sigmoid(SWIGLU_ALPHA * gl) * (lin + 1.0)).astype(BF16))
                y = _dot(jnp.concatenate(acts, axis=1), w2s_ref[...]) + b2_ref[0]
                d_half = y.shape[-1] // 2
                _store_packed_rows(ybuf.at[slot], _pack_bf16_pair(y[:, :d_half], y[:, d_half:]))

            @pl.when(n_rows > bm // 2)
            def _():
                ffn(bm)

            @pl.when(n_rows <= bm // 2)
            def _():
                ffn(bm // 2)

            y_copy(g, slot).start()
            return carry

        lax.fori_loop(0, n_blocks, block, 0)

    @pl.when(e == pl.num_programs(0) - 1)
    def _():
        @pl.when(total >= 2)
        def _():
            y_copy(total - 2, lax.rem(total, 2)).wait()

        y_copy(total - 1, lax.rem(total - 1, 2)).wait()


def _experts(tables, xs, w1, b1g, b1l, w2, b2, bm, d):
    s_words = d // 2 // LANES
    dff2 = w1.shape[-1]
    dff = dff2 // 2
    emap = lambda e, *_: (e, 0, 0)
    grid_spec = pltpu.PrefetchScalarGridSpec(
        num_scalar_prefetch=len(tables),
        grid=(N_EXPERTS,),
        in_specs=[pl.BlockSpec(memory_space=pl.ANY),
                  pl.BlockSpec((1, d, dff2), emap),
                  pl.BlockSpec((1, 1, dff), emap),
                  pl.BlockSpec((1, 1, dff), emap),
                  pl.BlockSpec((1, dff, d), emap),
                  pl.BlockSpec((1, 1, d), emap)],
        out_specs=pl.BlockSpec(memory_space=pl.ANY),
        scratch_shapes=[pltpu.VMEM((d, dff2), BF16), pltpu.VMEM((dff, d), BF16),
                        pltpu.VMEM((2, bm * s_words, LANES), jnp.int32),
                        pltpu.VMEM((2, bm * s_words, LANES), jnp.int32),
                        pltpu.SemaphoreType.DMA((2,)), pltpu.SemaphoreType.DMA((2,))],
    )
    return pl.pallas_call(
        functools.partial(_expert_kernel, bm=bm),
        grid_spec=grid_spec,
        out_shape=jax.ShapeDtypeStruct(xs.shape, jnp.int32),
        compiler_params=pltpu.CompilerParams(dimension_semantics=("arbitrary",),
                                             vmem_limit_bytes=VMEM_LIMIT),
        name="experts",
    )(*tables, xs, w1, b1g, b1l, w2, b2)


def _gather_rows(dest_flat, y3):
    n_out = dest_flat.shape[0]
    _, s, _ = y3.shape
    mesh = _sc_mesh()
    n_workers = mesh.num_cores * mesh.num_subcores
    per_worker = n_out // n_workers
    n_chunks = per_worker // SC_ROWS
    assert per_worker % SC_ROWS == 0 and n_chunks % 2 == 0

    @functools.partial(
        pl.kernel, mesh=mesh, out_type=jax.ShapeDtypeStruct((n_out, s, LANES), y3.dtype),
        scratch_types=[pltpu.VMEM((2, SC_ROWS), jnp.int32), pltpu.VMEM((2, SC_ROWS, s, LANES), y3.dtype),
                       pltpu.SemaphoreType.DMA((2,)), pltpu.SemaphoreType.DMA((2,))],
        name="sc_gather")
    def gather(y_hbm, dest_hbm, out_hbm, idx_v, rows_v, gsem, wsem):
        base = (lax.axis_index("s") * mesh.num_cores + lax.axis_index("c")) * per_worker

        def rows_of(chunk):
            return pl.ds(pl.multiple_of(base + chunk * SC_ROWS, SC_ROWS), SC_ROWS)

        def gather_copy(slot):
            return pltpu.make_async_copy(y_hbm.at[idx_v.at[slot]], rows_v.at[slot], gsem.at[slot])

        def write_copy(chunk, slot):
            return pltpu.make_async_copy(rows_v.at[slot], out_hbm.at[rows_of(chunk)], wsem.at[slot])

        def start_gather(chunk, slot):
            pltpu.sync_copy(dest_hbm.at[rows_of(chunk)], idx_v.at[slot])
            gather_copy(slot).start()

        start_gather(0, 0)

        @pl.loop(0, n_chunks // 2)
        def _(pair):
            for slot in (0, 1):
                chunk = 2 * pair + slot
                gather_copy(slot).wait()
                write_copy(chunk, slot).start()

                @pl.when(chunk >= 1)
                def _():
                    write_copy(chunk - 1, 1 - slot).wait()

                @pl.when(chunk + 1 < n_chunks)
                def _():
                    start_gather(chunk + 1, 1 - slot)

        write_copy(n_chunks - 1, 1).wait()

    return gather(y3, dest_flat)


def _combine_kernel(x1_ref, gate_ref, y4_ref, nw_ref, o_ref, *, tm, normalize):
    x = x1_ref[...]
    gates = gate_ref[...]
    for kk in range(TOP_K):
        lo, hi = _unpack_bf16_pair(_load_packed_rows(y4_ref, (kk,), tm, y4_ref.shape[1] // tm))
        x = x + gates[:, kk:kk + 1] * jnp.concatenate([lo, hi], axis=1)
    if normalize:
        x = x * lax.rsqrt(jnp.mean(x * x, axis=-1, keepdims=True) + EPS) * nw_ref[...]
    o_ref[...] = x


def _combine(x1, gates, y4, norm_w, tm, normalize):
    t, d = x1.shape
    s_words = d // 2 // LANES
    row = lambda i: (i, 0)
    return pl.pallas_call(
        functools.partial(_combine_kernel, tm=tm, normalize=normalize),
        grid=(t // tm,),
        in_specs=[pl.BlockSpec((tm, d), row), pl.BlockSpec((tm, LANES), row),
                  pl.BlockSpec((TOP_K, tm * s_words, LANES), lambda i: (0, i, 0)),
                  pl.BlockSpec((1, d), lambda i: (0, 0))],
        out_specs=pl.BlockSpec((tm, d), row),
        out_shape=jax.ShapeDtypeStruct((t, d), F32),
        compiler_params=pltpu.CompilerParams(dimension_semantics=("parallel",)),
        name="combine",
    )(x1, gates, y4, norm_w)


def _tiles(t, seq):
    tm = min(2 * MXU_COLS, t)
    tr = min(2 * MXU_COLS, t)
    tq = min(2 * MXU_COLS, seq)
    gdn_rows = min(4 * CHUNK, seq)
    bm = 2 * MXU_COLS
    assert t % tm == 0 and t % tr == 0 and seq % tq == 0 and seq % gdn_rows == 0 and tq % CHUNK == 0
    return tm, tr, tq, gdn_rows, bm


def _rot_half(w):
    half = w.shape[-1] // 2
    return jnp.concatenate([-w[..., half:], w[..., :half]], axis=-1)


def _prep_mla_weights(w_uq, w_ukv):
    dq = MLA_NOPE + MLA_ROPE
    wq = w_uq.reshape(MLA_Q_LORA, MLA_HEADS, dq)
    wq_rot = jnp.concatenate([jnp.zeros_like(wq[..., :MLA_NOPE]), _rot_half(wq[..., MLA_NOPE:])], axis=-1)
    pad = lambda w: jnp.pad(w, ((0, 0), (0, 0), (0, LANES - w.shape[-1]))).reshape(w.shape[0], C_HEADS)
    wkv = w_ukv.reshape(MLA_KV_LORA, MLA_HEADS, MLA_NOPE + MLA_V)
    wv = wkv[..., MLA_NOPE:].reshape(MLA_KV_LORA, C_V)
    return (pad(wq).T.astype(BF16), pad(wq_rot).T.astype(BF16), pad(wkv[..., :MLA_NOPE]).astype(BF16),
            wv.T.astype(BF16))


def _rope_freq():
    half = MLA_ROPE // 2
    return (ROPE_THETA ** (-jnp.arange(half, dtype=F32) / half)).reshape(half, 1)


def _lane_row(v):
    return jnp.pad(v.astype(F32), (0, LANES - v.shape[0])).reshape(1, LANES)


def _routing_tables(cnt, eidx, rank, bm):
    e_ids = jnp.arange(N_EXPERTS, dtype=jnp.int32)
    counts = cnt[:, 0].astype(jnp.int32)
    padded = (counts + bm - 1) // bm * bm
    pend = jnp.sum(jnp.where(e_ids[None, :] <= e_ids[:, None], padded[None, :], 0), axis=1)
    pstart = pend - padded
    dest = jnp.sum(jnp.where(eidx[:TOP_K, :, None] == e_ids, pstart, 0), axis=-1) + rank[:TOP_K]
    tables = (counts, pstart // bm, padded // bm, pend[-1:] // bm)
    return dest.reshape(-1), tuple(t.astype(jnp.int32) for t in tables)


def kernel(x, positions, norm_mix_w, w_in, conv_w, a_log, dt_bias, gdn_norm_w, q_norm_w, w_uq, kv_norm_w,
           w_ukv, mla_out_norm_w, w_out, norm_ffn_w, router_w, router_b, w1, b1, w2, b2, norm_final_w):
    bsz, seq, d = x.shape
    t = bsz * seq
    depth = w_in.shape[0]
    tm, tr, tq, gdn_rows, bm = _tiles(t, seq)
    x2 = x.reshape(t, d)
    posf = positions.astype(F32)
    freq = _rope_freq()
    for l in range(depth):
        qkv, z, cq, ckv, kr2, ab = _inproj(x2, norm_mix_w[l].reshape(1, d), w_in[l], tm)
        conv_w8 = jnp.pad(conv_w[l], ((0, 8 - GDN_CONV), (0, 0)))
        o_a = _gdn(qkv.reshape(bsz, seq, C_QKV), z.reshape(bsz, seq, C_Z), ab.reshape(bsz, seq, LANES),
                   conv_w8, _lane_row(a_log[l]), _lane_row(dt_bias[l]), gdn_norm_w[l].reshape(1, GDN_D),
                   gdn_rows).reshape(t, C_Z)
        wqt, wqrt, wk, wvt = _prep_mla_weights(w_uq[l], w_ukv[l])
        qt, k, vt = _mla_proj(cq, ckv, kr2, posf.reshape(1, t),
                              q_norm_w[l].reshape(1, -1), kv_norm_w[l].reshape(1, -1),
                              wqt, wqrt, wk, wvt, freq, tq)
        o_bt = _attention(qt, k, vt, mla_out_norm_w[l].reshape(-1, 1), bsz, seq, tq)
        x1, hn, eidx, gates, rank, cnt = _router(x2, o_a, o_bt, w_out[l].astype(BF16),
                                                 norm_ffn_w[l].reshape(1, d), router_w[l].T,
                                                 router_b[l].reshape(N_EXPERTS, 1), tr)
        n_pad = t * TOP_K + N_EXPERTS * bm
        dest_flat, tables = _routing_tables(cnt, eidx, rank, bm)
        s_words = d // 2 // LANES
        dest_chunks = dest_flat.reshape(TOP_K, t // SC_ROWS, SC_ROWS).transpose(1, 0, 2).reshape(-1)
        buf = _dispatch(dest_chunks, hn.reshape(t, s_words, LANES), n_pad)
        yb = _experts(tables, buf.reshape(n_pad * s_words, LANES), w1[l],
                      b1[l][:, None, 0::2], b1[l][:, None, 1::2], w2[l], b2[l][:, None, :], bm, d)
        y4 = _gather_rows(dest_flat, yb.reshape(n_pad, s_words, LANES))
        x2 = _combine(x1, gates, y4.reshape(TOP_K, t * s_words, LANES), norm_final_w.reshape(1, d), tr,
                      normalize=(l == depth - 1))
    return x2.reshape(bsz, seq, d)
```

```python
import functools
import math

import jax
import jax.numpy as jnp
from jax import lax
from jax.experimental import pallas as pl
from jax.experimental.pallas import tpu as pltpu
from jax.experimental.pallas import tpu_sc as plsc

F32 = jnp.float32
BF16 = jnp.bfloat16
HIGHEST = lax.Precision.HIGHEST

LANES = 128
MXU_COLS = 256
EPS = 1e-6
CHUNK = 64

GDN_HEADS = 4
GDN_D = 128
GDN_CONV = 4
MLA_HEADS = 8
MLA_NOPE = 64
MLA_ROPE = 32
MLA_V = 64
MLA_Q_LORA = 384
MLA_KV_LORA = 256
ROPE_THETA = 10000.0
N_EXPERTS = 32
TOP_K = 4
SWIGLU_LIMIT = 7.0
SWIGLU_ALPHA = 1.702

NEG_BIG = -1e30
VMEM_LIMIT = 52 * 1024 * 1024


def _sigmoid(x):
    return 1.0 / (1.0 + jnp.exp(-x))


def _softplus(x):
    return jnp.maximum(x, 0.0) + jnp.log(1.0 + jnp.exp(-jnp.abs(x)))


def _dot(a, b, precision=None):
    return jnp.dot(a, b, preferred_element_type=F32, precision=precision)


def _dot_nt(a, b, precision=None):
    return lax.dot_general(a, b, (((1,), (1,)), ((), ())), preferred_element_type=F32,
                           precision=precision)


def _dot_tn(a, b):
    return lax.dot_general(a, b, (((0,), (0,)), ((), ())), preferred_element_type=F32)


def _split_bf16(a):
    hi = a.astype(BF16)
    return hi, (a - hi.astype(F32)).astype(BF16)


def _pack_bf16_pair(lo, hi):
    lo_bits = pltpu.bitcast(lo.astype(BF16).astype(F32), jnp.int32)
    hi_bits = pltpu.bitcast(hi.astype(BF16).astype(F32), jnp.int32)
    return jnp.bitwise_or(hi_bits, lax.shift_right_logical(lo_bits, 16))


def _unpack_bf16_pair(w):
    lo = pltpu.bitcast(lax.shift_left(w, 16), F32)
    hi = pltpu.bitcast(jnp.bitwise_and(w, -65536), F32)
    return lo, hi


def _load_packed_rows(ref, lead, m, s):
    return jnp.concatenate([ref[(*lead, pl.ds(j, m, stride=s), slice(None))] for j in range(s)], axis=1)


def _store_packed_rows(ref, words):
    m = words.shape[0]
    s = words.shape[1] // LANES
    for j in range(s):
        ref[pl.ds(j, m, stride=s), :] = words[:, j * LANES:(j + 1) * LANES]


def _dot_split_nt(a, b):
    return _dot_nt(a[0], b[0]) + _dot_nt(a[0], b[1]) + _dot_nt(a[1], b[0])


def _dot_split(a, b):
    return _dot(a[0], b[0]) + _dot(a[0], b[1]) + _dot(a[1], b[0])


C_QKV = 3 * GDN_HEADS * GDN_D
C_Z = GDN_HEADS * GDN_D
C_IN = C_QKV + C_Z + MLA_Q_LORA + MLA_KV_LORA + 3 * LANES


def _inproj_kernel(x_ref, nw_ref, w_ref, qkv_ref, z_ref, cq_ref, ckv_ref, kr_ref, ab_ref, wb_ref):
    @pl.when(pl.program_id(0) == 0)
    def _():
        d = w_ref.shape[0]
        hd4 = C_QKV + C_Z
        wb_ref[:, 0:hd4] = w_ref[:, 0:hd4].astype(BF16)
        tail = w_ref[:, hd4:]
        o_cq = 2 * GDN_HEADS
        o_ckv = o_cq + MLA_Q_LORA
        o_kr = o_ckv + MLA_KV_LORA
        kr = tail[:, o_kr:o_kr + MLA_ROPE]
        half = MLA_ROPE // 2
        zeros = lambda n: jnp.zeros((d, n), F32)
        pad = LANES - MLA_NOPE - MLA_ROPE
        groups = [tail[:, o_cq:o_ckv], tail[:, o_ckv:o_kr],
                  jnp.concatenate([zeros(MLA_NOPE), kr, zeros(pad)], axis=1),
                  jnp.concatenate([zeros(MLA_NOPE), -kr[:, half:], kr[:, :half], zeros(pad)], axis=1),
                  jnp.concatenate([tail[:, 0:o_cq], zeros(LANES - o_cq)], axis=1)]
        o = hd4
        for grp in groups:
            wb_ref[:, o:o + grp.shape[1]] = grp.astype(BF16)
            o += grp.shape[1]

    x = x_ref[...]
    var = jnp.mean(x * x, axis=-1, keepdims=True)
    h = (x * lax.rsqrt(var + EPS) * nw_ref[...]).astype(BF16)
    p = _dot(h, wb_ref[...])
    o = 0
    for ref in (qkv_ref, z_ref, cq_ref, ckv_ref, kr_ref, ab_ref):
        n = ref.shape[-1]
        ref[...] = p[:, o:o + n]
        o += n


def _inproj(x2, norm_w, w_in, tm):
    t, d = x2.shape
    widths = (C_QKV, C_Z, MLA_Q_LORA, MLA_KV_LORA, 2 * LANES, LANES)
    return pl.pallas_call(
        _inproj_kernel,
        grid=(t // tm,),
        in_specs=[pl.BlockSpec((tm, d), lambda i: (i, 0)),
                  pl.BlockSpec((1, d), lambda i: (0, 0)),
                  pl.BlockSpec(w_in.shape, lambda i: (0, 0), pipeline_mode=pl.Buffered(1))],
        out_specs=[pl.BlockSpec((tm, n), lambda i: (i, 0)) for n in widths],
        out_shape=[jax.ShapeDtypeStruct((t, n), F32) for n in widths],
        scratch_shapes=[pltpu.VMEM((d, C_IN), BF16)],
        compiler_params=pltpu.CompilerParams(dimension_semantics=("arbitrary",),
                                             vmem_limit_bytes=VMEM_LIMIT),
        name="inproj",
    )(x2, norm_w, w_in)


def _gdn_kernel(qkv_ref, z_ref, ab_ref, cw_ref, alog_ref, dtb_ref, nw_ref, o_ref, ext_ref, s_ref):
    c = CHUNK
    bsz, rows, _ = qkv_ref.shape
    chunks_per_batch = rows // c

    @pl.when(pl.program_id(0) == 0)
    def _():
        ext_ref[:, 0:8, :] = jnp.zeros((bsz, 8, C_QKV), F32)
        s_ref[...] = jnp.zeros_like(s_ref)

    cw = cw_ref[...]
    nw = nw_ref[...]
    rr = lax.broadcasted_iota(jnp.int32, (rows, rows), 0)
    rc = lax.broadcasted_iota(jnp.int32, (rows, rows), 1)
    in_chunk_prefix = jnp.where(rr // c == rc // c, jnp.where(rr >= rc, 1.0, 0.0), 0.0)
    e8 = (lax.broadcasted_iota(jnp.int32, (8, LANES), 0)
          == lax.broadcasted_iota(jnp.int32, (8, LANES), 1)).astype(F32)
    ri = lax.broadcasted_iota(jnp.int32, (c, c), 0)
    ci = lax.broadcasted_iota(jnp.int32, (c, c), 1)
    causal = ri >= ci
    strict = ri > ci
    eye = (ri == ci).astype(F32)
    chains = [(j, h) for j in range(chunks_per_batch) for h in range(GDN_HEADS)]

    def prep_pieces(b):
        g = {"xc": [None] * (C_QKV // LANES)}

        def conv(tiles):
            def run():
                cols = slice(tiles[0] * LANES, (tiles[-1] + 1) * LANES)
                x = qkv_ref[b, :, cols]
                ext_ref[b, 8:8 + rows, cols] = x
                ext = ext_ref[b, :, cols]
                xb = cw[3:4, cols] * x
                for j in range(1, GDN_CONV):
                    xb = xb + cw[3 - j:4 - j, cols] * pltpu.roll(ext, j, axis=0)[8:8 + rows, :]
                ext_ref[b, 0:8, cols] = x[rows - 8:rows, :]
                xb = xb * _sigmoid(xb)
                for i, tile in enumerate(tiles):
                    g["xc"][tile] = xb[:, i * LANES:(i + 1) * LANES]
            return run

        def gates():
            ab = ab_ref[b]
            g_all = -jnp.exp(alog_ref[...]) * _softplus(ab + dtb_ref[...])
            g["beta"] = _sigmoid(ab)
            g["g_cum"] = _dot(in_chunk_prefix, g_all, HIGHEST)
            g["g_row"] = _dot_nt(e8, g["g_cum"], HIGHEST)
            for name in ("q", "k", "v", "gc", "bt", "dec"):
                g[name] = []

        def chunk(j):
            def run():
                rs = slice(j * c, (j + 1) * c)
                xc = g["xc"]
                for h in range(GDN_HEADS):
                    q = xc[h][rs]
                    k = xc[GDN_HEADS + h][rs]
                    g["q"].append(q * lax.rsqrt(jnp.sum(q * q, axis=-1, keepdims=True) + EPS) * (GDN_D ** -0.5))
                    g["k"].append(k * lax.rsqrt(jnp.sum(k * k, axis=-1, keepdims=True) + EPS))
                    g["v"].append(xc[2 * GDN_HEADS + h][rs])
                    gc = g["g_cum"][rs, h:h + 1]
                    gr = g["g_row"][h:h + 1, rs]
                    g["gc"].append(gc)
                    g["bt"].append(g["beta"][rs, GDN_HEADS + h:GDN_HEADS + h + 1])
                    g["dec"].append(jnp.exp(jnp.where(causal, gc - gr, -jnp.inf)))
            return run

        def a_mats():
            g["kb"] = [k * bt for k, bt in zip(g["k"], g["bt"])]
            g["k16"] = [k.astype(BF16) for k in g["k"]]
            g["a"] = [jnp.where(strict, _dot_nt(kb.astype(BF16), k16) * dec, 0.0)
                      for kb, k16, dec in zip(g["kb"], g["k16"], g["dec"])]

        n_tiles = C_QKV // LANES
        conv_pieces = [conv(list(range(i, i + 2))) for i in range(0, n_tiles, 2)]
        return g, conv_pieces + [gates] + [chunk(j) for j in range(chunks_per_batch)] + [a_mats]

    def inverse_rounds(g):
        def start():
            g["t"] = [eye - a for a in g["a"]]
            g["pw"] = [_split_bf16(a) for a in g["a"]]

        def square():
            g["pw"] = [_split_bf16(_dot_split(p, p)) for p in g["pw"]]

        def update():
            g["t"] = [t + _dot_split(_split_bf16(t), p) for t, p in zip(g["t"], g["pw"])]

        return [start] + [square, update] * (CHUNK.bit_length() - 2)

    groups = []
    pending = []
    for b in range(bsz):
        g, pieces = prep_pieces(b)
        groups.append(g)
        n_a, n_b = len(pending), len(pieces)
        ia = ib = 0
        while ia < n_a or ib < n_b:
            if ib >= n_b or (ia < n_a and ia * n_b <= ib * n_a):
                pending[ia]()
                ia += 1
            else:
                pieces[ib]()
                ib += 1
        pending = inverse_rounds(g)
    for fn in pending:
        fn()

    uws, qk16s, qd16s, kd16s, g_tots = [], [], [], [], []
    for g in groups:
        e_gs = [jnp.exp(gc) for gc in g["gc"]]
        uws.append([_dot(t.astype(BF16), jnp.concatenate([v * bt, kb * eg], axis=1).astype(BF16))
                    for t, v, bt, kb, eg in zip(g["t"], g["v"], g["bt"], g["kb"], e_gs)])
        qk16s.append([(_dot_nt(q.astype(BF16), k16) * dec).astype(BF16)
                      for q, k16, dec in zip(g["q"], g["k16"], g["dec"])])
        qd16s.append([(q * eg).astype(BF16) for q, eg in zip(g["q"], e_gs)])
        kd16s.append([(k * jnp.exp(gc[c - 1:c, :] - gc)).astype(BF16) for k, gc in zip(g["k"], g["gc"])])
        g_tots.append([jnp.exp(gc[c - 1:c, :]) for gc in g["gc"]])

    pairs = [(b, h) for b in range(bsz) for h in range(GDN_HEADS)]
    states = [s_ref[b * GDN_HEADS + h] for b, h in pairs]
    for j in range(chunks_per_batch):
        rs = slice(j * c, (j + 1) * c)
        pick = lambda per_group: [per_group[b][chains.index((j, h))] for b, h in pairs]
        uw_j, qk_j, qd_j, kd_j, gt_j = (pick(x) for x in (uws, qk16s, qd16s, kd16s, g_tots))
        s16s = [s.astype(BF16) for s in states]
        v16s = [(uw[:, :GDN_D] - _dot(uw[:, GDN_D:].astype(BF16), s16)).astype(BF16)
                for uw, s16 in zip(uw_j, s16s)]
        states = [s * gt + _dot_tn(kd, v16) for s, gt, kd, v16 in zip(states, gt_j, kd_j, v16s)]
        os_ = [_dot(qd, s16) + _dot(qk, v16) for qd, qk, s16, v16 in zip(qd_j, qk_j, s16s, v16s)]
        for (b, h), o in zip(pairs, os_):
            on = o * lax.rsqrt(jnp.mean(o * o, axis=-1, keepdims=True) + EPS) * nw
            zh = z_ref[b, rs, h * GDN_D:(h + 1) * GDN_D]
            o_ref[b, rs, h * GDN_D:(h + 1) * GDN_D] = on * (zh * _sigmoid(zh))
    for p, s in enumerate(states):
        s_ref[p] = s


def _gdn(qkv, z, ab, conv_w8, alog_row, dtb_row, norm_w, rows):
    bsz, seq, _ = qkv.shape
    blk = lambda i: (0, i, 0)
    const = lambda i: (0, 0)
    return pl.pallas_call(
        _gdn_kernel,
        grid=(seq // rows,),
        in_specs=[pl.BlockSpec((bsz, rows, C_QKV), blk),
                  pl.BlockSpec((bsz, rows, C_Z), blk),
                  pl.BlockSpec((bsz, rows, LANES), blk),
                  pl.BlockSpec((8, C_QKV), const),
                  pl.BlockSpec((1, LANES), const),
                  pl.BlockSpec((1, LANES), const),
                  pl.BlockSpec((1, GDN_D), const)],
        out_specs=pl.BlockSpec((bsz, rows, C_Z), blk),
        out_shape=jax.ShapeDtypeStruct((bsz, seq, C_Z), F32),
        scratch_shapes=[pltpu.VMEM((bsz, 8 + rows, C_QKV), F32),
                        pltpu.VMEM((bsz * GDN_HEADS, GDN_D, GDN_D), F32)],
        compiler_params=pltpu.CompilerParams(dimension_semantics=("arbitrary",),
                                             vmem_limit_bytes=VMEM_LIMIT),
        name="gdn",
    )(qkv, z, ab, conv_w8, alog_row, dtb_row, norm_w)


D_QK_PAD = LANES
C_HEADS = MLA_HEADS * D_QK_PAD
C_V = MLA_HEADS * MLA_V
V_AUG = MLA_V + 16
C_VAUG = MLA_HEADS * V_AUG
Q_SCALE = (MLA_NOPE + MLA_ROPE) ** -0.5 * math.log2(math.e)


def _mla_proj_kernel(cq_ref, ckv_ref, kr_ref, posr_ref, qnw_ref, kvnw_ref, wqt_ref,
                     wk_ref, wvt_ref, freqc_ref, qt_ref, k_ref, vt_ref):
    cq = cq_ref[...]
    cqn = (cq * lax.rsqrt(jnp.mean(cq * cq, axis=-1, keepdims=True) + EPS) * qnw_ref[...]).astype(BF16)
    ckv = ckv_ref[...]
    ckvn = (ckv * lax.rsqrt(jnp.mean(ckv * ckv, axis=-1, keepdims=True) + EPS)
            * kvnw_ref[...]).astype(BF16)
    tm = posr_ref.shape[-1]
    ang = freqc_ref[...] * posr_ref[...]
    c16 = jnp.cos(ang)
    s16 = jnp.sin(ang)
    pad = LANES - MLA_NOPE - MLA_ROPE
    cs_t = jnp.concatenate([jnp.ones((MLA_NOPE, tm), F32), c16, c16, jnp.ones((pad, tm), F32)], axis=0)
    sn_t = jnp.concatenate([jnp.zeros((MLA_NOPE, tm), F32), s16, s16, jnp.zeros((pad, tm), F32)], axis=0)
    cs = cs_t.T
    sn = sn_t.T
    q_t = _dot_nt(wqt_ref[...], cqn)
    kn = _dot(ckvn, wk_ref[...])
    kr = kr_ref[...]
    kpe = kr[:, :LANES] * cs + kr[:, LANES:] * sn
    half = MLA_ROPE // 2
    r0, r1, r2 = MLA_NOPE, MLA_NOPE + half, MLA_NOPE + MLA_ROPE
    for h in range(MLA_HEADS):
        sl = slice(h * LANES, (h + 1) * LANES)
        q = q_t[sl, :]
        rot = jnp.concatenate([jnp.zeros((r0, tm), F32), -q[r1:r2], q[r0:r1], jnp.zeros((pad, tm), F32)], axis=0)
        qt_ref[sl, :] = ((q * cs_t + rot * sn_t) * Q_SCALE).astype(BF16)
        k_ref[:, sl] = (kn[:, sl] + kpe).astype(BF16)
    v_t = _dot_nt(wvt_ref[...], ckvn).astype(BF16)
    ones = jnp.ones((V_AUG - MLA_V, v_t.shape[1]), BF16)
    for h in range(MLA_HEADS):
        vt_ref[0, h * V_AUG:h * V_AUG + MLA_V, :] = v_t[h * MLA_V:(h + 1) * MLA_V, :]
        vt_ref[0, h * V_AUG + MLA_V:(h + 1) * V_AUG, :] = ones


def _mla_proj(cq, ckv, kr2, posr, qnw, kvnw, wqt, wk, wvt, freqc, tm):
    t = cq.shape[0]
    row = lambda i: (i, 0)
    col = lambda i: (0, i)
    const = lambda i: (0, 0)
    return pl.pallas_call(
        _mla_proj_kernel,
        grid=(t // tm,),
        in_specs=[pl.BlockSpec((tm, MLA_Q_LORA), row),
                  pl.BlockSpec((tm, MLA_KV_LORA), row),
                  pl.BlockSpec((tm, 2 * LANES), row),
                  pl.BlockSpec((1, tm), col),
                  pl.BlockSpec((1, MLA_Q_LORA), const),
                  pl.BlockSpec((1, MLA_KV_LORA), const),
                  pl.BlockSpec((C_HEADS, MLA_Q_LORA), const),
                  pl.BlockSpec((MLA_KV_LORA, C_HEADS), const),
                  pl.BlockSpec((C_V, MLA_KV_LORA), const),
                  pl.BlockSpec((MLA_ROPE // 2, 1), const)],
        out_specs=[pl.BlockSpec((C_HEADS, tm), col),
                   pl.BlockSpec((tm, C_HEADS), row),
                   pl.BlockSpec((1, C_VAUG, tm), lambda i: (i, 0, 0))],
        out_shape=[jax.ShapeDtypeStruct((C_HEADS, t), BF16),
                   jax.ShapeDtypeStruct((t, C_HEADS), BF16),
                   jax.ShapeDtypeStruct((t // tm, C_VAUG, tm), BF16)],
        compiler_params=pltpu.CompilerParams(dimension_semantics=("parallel",),
                                             vmem_limit_bytes=VMEM_LIMIT),
        name="mla_proj",
    )(cq, ckv, kr2, posr, qnw, kvnw, wqt, wk, wvt, freqc)


def _attn_kernel(qt_ref, k_ref, vt_ref, nw_ref, ot_ref, m_ref, acc_ref, *, tq):
    i = pl.program_id(1)
    m_ref[...] = jnp.full(m_ref.shape, NEG_BIG, F32)
    acc_ref[...] = jnp.zeros(acc_ref.shape, F32)
    def tile(j, k_lo, n_keys, q_lo, masked):
        rows = pl.ds(pl.multiple_of(j * tq + k_lo, n_keys), n_keys)
        qs = slice(q_lo, tq)
        if masked:
            key_chunk = (k_lo + lax.broadcasted_iota(jnp.int32, (n_keys, tq - q_lo), 0)) // CHUNK
            qry_chunk = (q_lo + lax.broadcasted_iota(jnp.int32, (n_keys, tq - q_lo), 1)) // CHUNK
            mask = key_chunk <= qry_chunk
        scores = []
        for h in range(MLA_HEADS):
            hs = slice(h * LANES, (h + 1) * LANES)
            s_t = _dot(k_ref[rows, hs], qt_ref[hs, qs])
            scores.append(jnp.where(mask, s_t, NEG_BIG) if masked else s_t)
        probs, alphas = [], []
        for h in range(MLA_HEADS):
            m_old = m_ref[h, :, qs]
            m_new = jnp.maximum(m_old, jnp.max(scores[h], axis=0, keepdims=True))
            m_ref[h, :, qs] = m_new
            probs.append(jnp.exp2(scores[h] - m_new[0:1, :]).astype(BF16))
            alphas.append(jnp.exp2(m_old[0:1, :] - m_new[0:1, :]))
        for h in range(MLA_HEADS):
            v_aug = vt_ref[j, h * V_AUG:(h + 1) * V_AUG, k_lo:k_lo + n_keys]
            acc_ref[h, :, qs] = alphas[h] * acc_ref[h, :, qs] + _dot(v_aug, probs[h])

    def full_tile(j, carry):
        tile(j, 0, tq, 0, False)
        return carry

    lax.fori_loop(0, i, full_tile, 0)
    half = tq // 2
    tile(i, 0, half, 0, True)
    tile(i, half, half, half, True)

    outs = []
    for h in range(MLA_HEADS):
        acc = acc_ref[h]
        outs.append(acc[:MLA_V, :] / acc[MLA_V:MLA_V + 1, :])
    ssq = outs[0] * outs[0]
    for o in outs[1:]:
        ssq = ssq + o * o
    inv = lax.rsqrt(jnp.sum(ssq, axis=0, keepdims=True) / C_V + EPS)
    for h in range(MLA_HEADS):
        vs = slice(h * MLA_V, (h + 1) * MLA_V)
        ot_ref[vs, :] = outs[h] * inv * nw_ref[vs, :]


def _attention(qt, k, vt, norm_w_col, bsz, seq, tq):
    nq = seq // tq
    return pl.pallas_call(
        functools.partial(_attn_kernel, tq=tq),
        grid=(bsz, nq),
        in_specs=[pl.BlockSpec((C_HEADS, tq), lambda b, i: (0, b * nq + i)),
                  pl.BlockSpec((seq, C_HEADS), lambda b, i: (b, 0)),
                  pl.BlockSpec((nq, C_VAUG, tq), lambda b, i: (b, 0, 0)),
                  pl.BlockSpec((C_V, 1), lambda b, i: (0, 0))],
        out_specs=pl.BlockSpec((C_V, tq), lambda b, i: (0, b * nq + i)),
        out_shape=jax.ShapeDtypeStruct((C_V, bsz * seq), F32),
        scratch_shapes=[pltpu.VMEM((MLA_HEADS, 8, tq), F32),
                        pltpu.VMEM((MLA_HEADS, V_AUG, tq), F32)],
        compiler_params=pltpu.CompilerParams(dimension_semantics=("parallel", "parallel"),
                                             vmem_limit_bytes=VMEM_LIMIT),
        name="attention",
    )(qt, k, vt, norm_w_col)


def _router_kernel(x_ref, oa_ref, obt_ref, wo_ref, nw_ref, rw_ref, rb_ref,
                   x1_ref, h_ref, eidx_ref, gate_ref, rank_ref, cnt_ref, run_ref, *, tm):
    @pl.when(pl.program_id(0) == 0)
    def _():
        run_ref[...] = jnp.zeros_like(run_ref)

    half = oa_ref.shape[-1]
    y = (_dot(oa_ref[...].astype(BF16), wo_ref[0:half, :])
         + _dot_tn(obt_ref[...].astype(BF16), wo_ref[half:, :]))
    x1 = x_ref[...] + y
    x1_ref[...] = x1
    hn = x1 * lax.rsqrt(jnp.mean(x1 * x1, axis=-1, keepdims=True) + EPS) * nw_ref[...]
    d_half = hn.shape[-1] // 2
    _store_packed_rows(h_ref, _pack_bf16_pair(hn[:, :d_half], hn[:, d_half:]))
    lg = _dot_split_nt(_split_bf16(rw_ref[...]), _split_bf16(hn)) + rb_ref[...]
    expert = lax.broadcasted_iota(jnp.int32, lg.shape, 0)
    sels, tops, idxs = [], [], []
    for _ in range(TOP_K):
        m = jnp.max(lg, axis=0, keepdims=True)
        idx = jnp.min(jnp.where(lg == m, expert, N_EXPERTS), axis=0, keepdims=True)
        sel = expert == idx
        lg = jnp.where(sel, -jnp.inf, lg)
        sels.append(sel)
        tops.append(m)
        idxs.append(idx)
    exps = [jnp.exp(tv - tops[0]) for tv in tops]
    den = functools.reduce(lambda a, b: a + b, exps)
    multi = sels[0].astype(F32)
    for sel in sels[1:]:
        multi = multi + sel.astype(F32)
    ri = lax.broadcasted_iota(jnp.int32, (tm, tm), 0)
    ci = lax.broadcasted_iota(jnp.int32, (tm, tm), 1)
    before = _dot(multi.astype(BF16), (ri < ci).astype(BF16)) + run_ref[...]
    ranks = [jnp.sum(jnp.where(sel, before, 0.0), axis=0, keepdims=True).astype(jnp.int32) for sel in sels]
    run = run_ref[...] + jnp.sum(multi, axis=1, keepdims=True)
    run_ref[...] = run
    cnt_ref[...] = jnp.broadcast_to(run, cnt_ref.shape)
    pad_i = jnp.zeros((8 - TOP_K, tm), jnp.int32)
    eidx_ref[...] = jnp.concatenate(idxs + [pad_i], axis=0)
    rank_ref[...] = jnp.concatenate(ranks + [pad_i], axis=0)
    gates_t = jnp.concatenate([e / den for e in exps] + [jnp.zeros((LANES - TOP_K, tm), F32)], axis=0)
    gate_ref[...] = gates_t.T


def _router(x2, oa, obt, w_out, norm_w, rw, rb, tm):
    t, d = x2.shape
    half = oa.shape[-1]
    s_words = d // 2 // LANES
    row = lambda i: (i, 0)
    col = lambda i: (0, i)
    const = lambda i: (0, 0)
    return pl.pallas_call(
        functools.partial(_router_kernel, tm=tm),
        grid=(t // tm,),
        in_specs=[pl.BlockSpec((tm, d), row),
                  pl.BlockSpec((tm, half), row),
                  pl.BlockSpec((half, tm), lambda i: (0, i)),
                  pl.BlockSpec((2 * half, d), const),
                  pl.BlockSpec((1, d), const),
                  pl.BlockSpec((N_EXPERTS, d), const),
                  pl.BlockSpec((N_EXPERTS, 1), const)],
        out_specs=[pl.BlockSpec((tm, d), row), pl.BlockSpec((tm * s_words, LANES), row),
                   pl.BlockSpec((8, tm), col), pl.BlockSpec((tm, LANES), row),
                   pl.BlockSpec((8, tm), col), pl.BlockSpec((N_EXPERTS, LANES), const)],
        out_shape=[jax.ShapeDtypeStruct((t, d), F32), jax.ShapeDtypeStruct((t * s_words, LANES), jnp.int32),
                   jax.ShapeDtypeStruct((8, t), jnp.int32), jax.ShapeDtypeStruct((t, LANES), F32),
                   jax.ShapeDtypeStruct((8, t), jnp.int32), jax.ShapeDtypeStruct((N_EXPERTS, LANES), F32)],
        scratch_shapes=[pltpu.VMEM((N_EXPERTS, 1), F32)],
        compiler_params=pltpu.CompilerParams(dimension_semantics=("arbitrary",),
                                             vmem_limit_bytes=VMEM_LIMIT),
        name="router",
    )(x2, oa, obt, w_out, norm_w, rw, rb)


SC_ROWS = 64


def _sc_mesh():
    return plsc.VectorSubcoreMesh(core_axis_name="c", subcore_axis_name="s")


def _dispatch(dest_chunks, h3, n_pad):
    t, s, _ = h3.shape
    mesh = _sc_mesh()
    n_workers = mesh.num_cores * mesh.num_subcores
    per_worker = t // n_workers
    n_chunks = per_worker // SC_ROWS
    assert per_worker % SC_ROWS == 0 and n_chunks % 2 == 0

    @functools.partial(
        pl.kernel, mesh=mesh, out_type=jax.ShapeDtypeStruct((n_pad, s, LANES), h3.dtype),
        scratch_types=[pltpu.VMEM((2, TOP_K * SC_ROWS), jnp.int32), pltpu.VMEM((2, SC_ROWS, s, LANES), h3.dtype),
                       pltpu.SemaphoreType.DMA((2,)), pltpu.SemaphoreType.DMA((2,)), pltpu.SemaphoreType.DMA],
        name="sc_dispatch")
    def scatter(h_hbm, dest_hbm, out_hbm, idx_v, rows_v, rsem, isem, ssem):
        base = (lax.axis_index("s") * mesh.num_cores + lax.axis_index("c")) * per_worker

        def loads(chunk, slot):
            off = pl.multiple_of(base + chunk * SC_ROWS, SC_ROWS)
            return (pltpu.make_async_copy(h_hbm.at[pl.ds(off, SC_ROWS)], rows_v.at[slot], rsem.at[slot]),
                    pltpu.make_async_copy(dest_hbm.at[pl.ds(off * TOP_K, SC_ROWS * TOP_K)], idx_v.at[slot],
                                          isem.at[slot]))

        for cp in loads(0, 0):
            cp.start()

        @pl.loop(0, n_chunks // 2)
        def _(pair):
            for slot in (0, 1):
                chunk = 2 * pair + slot
                for cp in loads(chunk, slot):
                    cp.wait()

                @pl.when(chunk + 1 < n_chunks)
                def _():
                    for cp in loads(chunk + 1, 1 - slot):
                        cp.start()

                copies = [pltpu.async_copy(rows_v.at[slot],
                                           out_hbm.at[idx_v.at[slot, pl.ds(kk * SC_ROWS, SC_ROWS)]], ssem)
                          for kk in range(TOP_K)]
                for cp in copies:
                    cp.wait()

    return scatter(h3, dest_chunks)


def _expert_kernel(cnt_ref, blk0_ref, nblk_ref, total_ref, x_hbm, w1_ref, b1g_ref, b1l_ref, w2_ref, b2_ref,
                   y_hbm, w1s_ref, w2s_ref, xbuf, ybuf, xsem, ysem, *, bm):
    e = pl.program_id(0)
    n_blocks = nblk_ref[e]
    first = blk0_ref[e]
    total = total_ref[0]
    rows_per_block = xbuf.shape[1]
    s_words = rows_per_block // bm
    n_tiles = w1s_ref.shape[-1] // MXU_COLS
    half = MXU_COLS // 2

    def x_copy(g, slot):
        return pltpu.make_async_copy(x_hbm.at[pl.ds(pl.multiple_of(g * rows_per_block, rows_per_block),
                                                    rows_per_block)], xbuf.at[slot], xsem.at[slot])

    def y_copy(g, slot):
        return pltpu.make_async_copy(ybuf.at[slot], y_hbm.at[pl.ds(pl.multiple_of(g * rows_per_block,
                                                                                   rows_per_block),
                                                                   rows_per_block)], ysem.at[slot])

    @pl.when(e == 0)
    def _():
        x_copy(0, 0).start()

    @pl.when(n_blocks > 0)
    def _():
        r = lax.broadcasted_iota(jnp.int32, (MXU_COLS, MXU_COLS), 0)
        c = lax.broadcasted_iota(jnp.int32, (MXU_COLS, MXU_COLS), 1)
        perm = (r == jnp.where(c < half, 2 * c, 2 * (c - half) + 1)).astype(BF16)
        for tix in range(n_tiles):
            cols = slice(tix * MXU_COLS, (tix + 1) * MXU_COLS)
            w1s_ref[:, cols] = _dot(w1_ref[0, :, cols].astype(BF16), perm).astype(BF16)
        w2s_ref[...] = w2_ref[0].astype(BF16)

        def block(j, carry):
            g = first + j
            slot = lax.rem(g, 2)
            x_copy(g, slot).wait()

            @pl.when(g + 1 < total)
            def _():
                x_copy(g + 1, 1 - slot).start()

            @pl.when(g >= 2)
            def _():
                y_copy(g - 2, slot).wait()

            n_rows = cnt_ref[e] - j * bm

            def ffn(m):
                words = _load_packed_rows(xbuf, (slot,), m, s_words)
                row = lax.broadcasted_iota(jnp.int32, words.shape, 0)
                x_lo, x_hi = _unpack_bf16_pair(jnp.where(row < n_rows, words, 0))
                x = jnp.concatenate([x_lo.astype(BF16), x_hi.astype(BF16)], axis=1)
                acts = []
                for tix in range(n_tiles):
                    hp = _dot(x, w1s_ref[:, tix * MXU_COLS:(tix + 1) * MXU_COLS])
                    feat = slice(tix * half, (tix + 1) * half)
                    gl = jnp.minimum(hp[:, :half] + b1g_ref[0][:, feat], SWIGLU_LIMIT)
                    lin = jnp.clip(hp[:, half:] + b1l_ref[0][:, feat], -SWIGLU_LIMIT, SWIGLU_LIMIT)
                    acts.append((gl * _sigmoid(SWIGLU_ALPHA * gl) * (lin + 1.0)).astype(BF16))
                y = _dot(jnp.concatenate(acts, axis=1), w2s_ref[...]) + b2_ref[0]
                d_half = y.shape[-1] // 2
                _store_packed_rows(ybuf.at[slot], _pack_bf16_pair(y[:, :d_half], y[:, d_half:]))

            @pl.when(n_rows > bm // 2)
            def _():
                ffn(bm)

            @pl.when(n_rows <= bm // 2)
            def _():
                ffn(bm // 2)

            y_copy(g, slot).start()
            return carry

        lax.fori_loop(0, n_blocks, block, 0)

    @pl.when(e == pl.num_programs(0) - 1)
    def _():
        @pl.when(total >= 2)
        def _():
            y_copy(total - 2, lax.rem(total, 2)).wait()

        y_copy(total - 1, lax.rem(total - 1, 2)).wait()


def _experts(tables, xs, w1, b1g, b1l, w2, b2, bm, d):
    s_words = d // 2 // LANES
    dff2 = w1.shape[-1]
    dff = dff2 // 2
    emap = lambda e, *_: (e, 0, 0)
    grid_spec = pltpu.PrefetchScalarGridSpec(
        num_scalar_prefetch=len(tables),
        grid=(N_EXPERTS,),
        in_specs=[pl.BlockSpec(memory_space=pl.ANY),
                  pl.BlockSpec((1, d, dff2), emap),
                  pl.BlockSpec((1, 1, dff), emap),
                  pl.BlockSpec((1, 1, dff), emap),
                  pl.BlockSpec((1, dff, d), emap),
                  pl.BlockSpec((1, 1, d), emap)],
        out_specs=pl.BlockSpec(memory_space=pl.ANY),
        scratch_shapes=[pltpu.VMEM((d, dff2), BF16), pltpu.VMEM((dff, d), BF16),
                        pltpu.VMEM((2, bm * s_words, LANES), jnp.int32),
                        pltpu.VMEM((2, bm * s_words, LANES), jnp.int32),
                        pltpu.SemaphoreType.DMA((2,)), pltpu.SemaphoreType.DMA((2,))],
    )
    return pl.pallas_call(
        functools.partial(_expert_kernel, bm=bm),
        grid_spec=grid_spec,
        out_shape=jax.ShapeDtypeStruct(xs.shape, jnp.int32),
        compiler_params=pltpu.CompilerParams(dimension_semantics=("arbitrary",),
                                             vmem_limit_bytes=VMEM_LIMIT),
        name="experts",
    )(*tables, xs, w1, b1g, b1l, w2, b2)


def _gather_rows(dest_flat, y3):
    n_out = dest_flat.shape[0]
    _, s, _ = y3.shape
    mesh = _sc_mesh()
    n_workers = mesh.num_cores * mesh.num_subcores
    per_worker = n_out // n_workers
    n_chunks = per_worker // SC_ROWS
    assert per_worker % SC_ROWS == 0 and n_chunks % 2 == 0

    @functools.partial(
        pl.kernel, mesh=mesh, out_type=jax.ShapeDtypeStruct((n_out, s, LANES), y3.dtype),
        scratch_types=[pltpu.VMEM((2, SC_ROWS), jnp.int32), pltpu.VMEM((2, SC_ROWS, s, LANES), y3.dtype),
                       pltpu.SemaphoreType.DMA((2,)), pltpu.SemaphoreType.DMA((2,))],
        name="sc_gather")
    def gather(y_hbm, dest_hbm, out_hbm, idx_v, rows_v, gsem, wsem):
        base = (lax.axis_index("s") * mesh.num_cores + lax.axis_index("c")) * per_worker

        def rows_of(chunk):
            return pl.ds(pl.multiple_of(base + chunk * SC_ROWS, SC_ROWS), SC_ROWS)

        def gather_copy(slot):
            return pltpu.make_async_copy(y_hbm.at[idx_v.at[slot]], rows_v.at[slot], gsem.at[slot])

        def write_copy(chunk, slot):
            return pltpu.make_async_copy(rows_v.at[slot], out_hbm.at[rows_of(chunk)], wsem.at[slot])

        def start_gather(chunk, slot):
            pltpu.sync_copy(dest_hbm.at[rows_of(chunk)], idx_v.at[slot])
            gather_copy(slot).start()

        start_gather(0, 0)

        @pl.loop(0, n_chunks // 2)
        def _(pair):
            for slot in (0, 1):
                chunk = 2 * pair + slot
                gather_copy(slot).wait()
                write_copy(chunk, slot).start()

                @pl.when(chunk >= 1)
                def _():
                    write_copy(chunk - 1, 1 - slot).wait()

                @pl.when(chunk + 1 < n_chunks)
                def _():
                    start_gather(chunk + 1, 1 - slot)

        write_copy(n_chunks - 1, 1).wait()

    return gather(y3, dest_flat)


def _combine_kernel(x1_ref, gate_ref, y4_ref, nw_ref, o_ref, *, tm, normalize):
    x = x1_ref[...]
    gates = gate_ref[...]
    for kk in range(TOP_K):
        lo, hi = _unpack_bf16_pair(_load_packed_rows(y4_ref, (kk,), tm, y4_ref.shape[1] // tm))
        x = x + gates[:, kk:kk + 1] * jnp.concatenate([lo, hi], axis=1)
    if normalize:
        x = x * lax.rsqrt(jnp.mean(x * x, axis=-1, keepdims=True) + EPS) * nw_ref[...]
    o_ref[...] = x


def _combine(x1, gates, y4, norm_w, tm, normalize):
    t, d = x1.shape
    s_words = d // 2 // LANES
    row = lambda i: (i, 0)
    return pl.pallas_call(
        functools.partial(_combine_kernel, tm=tm, normalize=normalize),
        grid=(t // tm,),
        in_specs=[pl.BlockSpec((tm, d), row), pl.BlockSpec((tm, LANES), row),
                  pl.BlockSpec((TOP_K, tm * s_words, LANES), lambda i: (0, i, 0)),
                  pl.BlockSpec((1, d), lambda i: (0, 0))],
        out_specs=pl.BlockSpec((tm, d), row),
        out_shape=jax.ShapeDtypeStruct((t, d), F32),
        compiler_params=pltpu.CompilerParams(dimension_semantics=("parallel",)),
        name="combine",
    )(x1, gates, y4, norm_w)


def _tiles(t, seq):
    tm = min(2 * MXU_COLS, t)
    tr = min(2 * MXU_COLS, t)
    tq = min(2 * MXU_COLS, seq)
    gdn_rows = min(4 * CHUNK, seq)
    bm = 2 * MXU_COLS
    assert t % tm == 0 and t % tr == 0 and seq % tq == 0 and seq % gdn_rows == 0 and tq % CHUNK == 0
    return tm, tr, tq, gdn_rows, bm


def _prep_mla_weights(w_uq, w_ukv):
    dq = MLA_NOPE + MLA_ROPE
    wq = w_uq.reshape(MLA_Q_LORA, MLA_HEADS, dq)
    pad = lambda w: jnp.pad(w, ((0, 0), (0, 0), (0, LANES - w.shape[-1]))).reshape(w.shape[0], C_HEADS)
    wkv = w_ukv.reshape(MLA_KV_LORA, MLA_HEADS, MLA_NOPE + MLA_V)
    wv = wkv[..., MLA_NOPE:].reshape(MLA_KV_LORA, C_V)
    return pad(wq).T.astype(BF16), pad(wkv[..., :MLA_NOPE]).astype(BF16), wv.T.astype(BF16)


def _rope_freq():
    half = MLA_ROPE // 2
    return (ROPE_THETA ** (-jnp.arange(half, dtype=F32) / half)).reshape(half, 1)


def _lane_row(v):
    return jnp.pad(v.astype(F32), (0, LANES - v.shape[0])).reshape(1, LANES)


def _routing_tables(cnt, eidx, rank, bm):
    e_ids = jnp.arange(N_EXPERTS, dtype=jnp.int32)
    counts = cnt[:, 0].astype(jnp.int32)
    padded = (counts + bm - 1) // bm * bm
    pend = jnp.sum(jnp.where(e_ids[None, :] <= e_ids[:, None], padded[None, :], 0), axis=1)
    pstart = pend - padded
    dest = jnp.sum(jnp.where(eidx[:TOP_K, :, None] == e_ids, pstart, 0), axis=-1) + rank[:TOP_K]
    tables = (counts, pstart // bm, padded // bm, pend[-1:] // bm)
    return dest.reshape(-1), tuple(t.astype(jnp.int32) for t in tables)


def kernel(x, positions, norm_mix_w, w_in, conv_w, a_log, dt_bias, gdn_norm_w, q_norm_w, w_uq, kv_norm_w,
           w_ukv, mla_out_norm_w, w_out, norm_ffn_w, router_w, router_b, w1, b1, w2, b2, norm_final_w):
    bsz, seq, d = x.shape
    t = bsz * seq
    depth = w_in.shape[0]
    tm, tr, tq, gdn_rows, bm = _tiles(t, seq)
    x2 = x.reshape(t, d)
    posf = positions.astype(F32)
    freq = _rope_freq()
    for l in range(depth):
        qkv, z, cq, ckv, kr2, ab = _inproj(x2, norm_mix_w[l].reshape(1, d), w_in[l], tm)
        conv_w8 = jnp.pad(conv_w[l], ((0, 8 - GDN_CONV), (0, 0)))
        o_a = _gdn(qkv.reshape(bsz, seq, C_QKV), z.reshape(bsz, seq, C_Z), ab.reshape(bsz, seq, LANES),
                   conv_w8, _lane_row(a_log[l]), _lane_row(dt_bias[l]), gdn_norm_w[l].reshape(1, GDN_D),
                   gdn_rows).reshape(t, C_Z)
        wqt, wk, wvt = _prep_mla_weights(w_uq[l], w_ukv[l])
        qt, k, vt = _mla_proj(cq, ckv, kr2, posf.reshape(1, t),
                              q_norm_w[l].reshape(1, -1), kv_norm_w[l].reshape(1, -1),
                              wqt, wk, wvt, freq, tq)
        o_bt = _attention(qt, k, vt, mla_out_norm_w[l].reshape(-1, 1), bsz, seq, tq)
        x1, hn, eidx, gates, rank, cnt = _router(x2, o_a, o_bt, w_out[l].astype(BF16),
                                                 norm_ffn_w[l].reshape(1, d), router_w[l].T,
                                                 router_b[l].reshape(N_EXPERTS, 1), tr)
        n_pad = t * TOP_K + N_EXPERTS * bm
        dest_flat, tables = _routing_tables(cnt, eidx, rank, bm)
        s_words = d // 2 // LANES
        dest_chunks = dest_flat.reshape(TOP_K, t // SC_ROWS, SC_ROWS).transpose(1, 0, 2).reshape(-1)
        buf = _dispatch(dest_chunks, hn.reshape(t, s_words, LANES), n_pad)
        yb = _experts(tables, buf.reshape(n_pad * s_words, LANES), w1[l],
                      b1[l][:, None, 0::2], b1[l][:, None, 1::2], w2[l], b2[l][:, None, :], bm, d)
        y4 = _gather_rows(dest_flat, yb.reshape(n_pad, s_words, LANES))
        x2 = _combine(x1, gates, y4.reshape(TOP_K, t * s_words, LANES), norm_final_w.reshape(1, d), tr,
                      normalize=(l == depth - 1))
    return x2.reshape(bsz, seq, d)
```

```python
import functools
import math

import jax
import jax.numpy as jnp
from jax import lax
from jax.experimental import pallas as pl
from jax.experimental.pallas import tpu as pltpu
from jax.experimental.pallas import tpu_sc as plsc

F32 = jnp.float32
BF16 = jnp.bfloat16
HIGHEST = lax.Precision.HIGHEST

LANES = 128
MXU_COLS = 256
EPS = 1e-6
CHUNK = 64

GDN_HEADS = 4
GDN_D = 128
GDN_CONV = 4
MLA_HEADS = 8
MLA_NOPE = 64
MLA_ROPE = 32
MLA_V = 64
MLA_Q_LORA = 384
MLA_KV_LORA = 256
ROPE_THETA = 10000.0
N_EXPERTS = 32
TOP_K = 4
SWIGLU_LIMIT = 7.0
SWIGLU_ALPHA = 1.702

NEG_BIG = -1e30
VMEM_LIMIT = 52 * 1024 * 1024


def _sigmoid(x):
    return 1.0 / (1.0 + jnp.exp(-x))


def _softplus(x):
    return jnp.maximum(x, 0.0) + jnp.log(1.0 + jnp.exp(-jnp.abs(x)))


def _dot(a, b, precision=None):
    return jnp.dot(a, b, preferred_element_type=F32, precision=precision)


def _dot_nt(a, b, precision=None):
    return lax.dot_general(a, b, (((1,), (1,)), ((), ())), preferred_element_type=F32,
                           precision=precision)


def _dot_tn(a, b):
    return lax.dot_general(a, b, (((0,), (0,)), ((), ())), preferred_element_type=F32)


def _split_bf16(a):
    hi = a.astype(BF16)
    return hi, (a - hi.astype(F32)).astype(BF16)


def _pack_bf16_pair(lo, hi):
    lo_bits = pltpu.bitcast(lo.astype(BF16).astype(F32), jnp.int32)
    hi_bits = pltpu.bitcast(hi.astype(BF16).astype(F32), jnp.int32)
    return jnp.bitwise_or(hi_bits, lax.shift_right_logical(lo_bits, 16))


def _unpack_bf16_pair(w):
    lo = pltpu.bitcast(lax.shift_left(w, 16), F32)
    hi = pltpu.bitcast(jnp.bitwise_and(w, -65536), F32)
    return lo, hi


def _load_packed_rows(ref, lead, m, s):
    return jnp.concatenate([ref[(*lead, pl.ds(j, m, stride=s), slice(None))] for j in range(s)], axis=1)


def _store_packed_rows(ref, words):
    m = words.shape[0]
    s = words.shape[1] // LANES
    for j in range(s):
        ref[pl.ds(j, m, stride=s), :] = words[:, j * LANES:(j + 1) * LANES]


def _dot_split_nt(a, b):
    return _dot_nt(a[0], b[0]) + _dot_nt(a[0], b[1]) + _dot_nt(a[1], b[0])


def _dot_split(a, b):
    return _dot(a[0], b[0]) + _dot(a[0], b[1]) + _dot(a[1], b[0])


C_QKV = 3 * GDN_HEADS * GDN_D
C_Z = GDN_HEADS * GDN_D
C_IN = C_QKV + C_Z + MLA_Q_LORA + MLA_KV_LORA + LANES


def _inproj_kernel(x_ref, nw_ref, w_ref, qkv_ref, z_ref, cq_ref, ckv_ref, abk_ref, wb_ref):
    @pl.when(pl.program_id(0) == 0)
    def _():
        d = w_ref.shape[0]
        hd4 = C_QKV + C_Z
        wb_ref[:, 0:hd4] = w_ref[:, 0:hd4].astype(BF16)
        tail = w_ref[:, hd4:]
        o_cq = 2 * GDN_HEADS
        o_ckv = o_cq + MLA_Q_LORA
        o_kr = o_ckv + MLA_KV_LORA
        zeros = lambda n: jnp.zeros((d, n), F32)
        groups = [tail[:, o_cq:o_ckv], tail[:, o_ckv:o_kr],
                  jnp.concatenate([tail[:, 0:o_cq], zeros(MLA_NOPE - o_cq), tail[:, o_kr:o_kr + MLA_ROPE],
                                   zeros(LANES - MLA_NOPE - MLA_ROPE)], axis=1)]
        o = hd4
        for grp in groups:
            wb_ref[:, o:o + grp.shape[1]] = grp.astype(BF16)
            o += grp.shape[1]

    x = x_ref[...]
    var = jnp.mean(x * x, axis=-1, keepdims=True)
    h = (x * lax.rsqrt(var + EPS) * nw_ref[...]).astype(BF16)
    p = _dot(h, wb_ref[...])
    o = 0
    for ref in (qkv_ref, z_ref, cq_ref, ckv_ref, abk_ref):
        n = ref.shape[-1]
        ref[...] = p[:, o:o + n]
        o += n


def _inproj(x2, norm_w, w_in, tm):
    t, d = x2.shape
    widths = (C_QKV, C_Z, MLA_Q_LORA, MLA_KV_LORA, LANES)
    return pl.pallas_call(
        _inproj_kernel,
        grid=(t // tm,),
        in_specs=[pl.BlockSpec((tm, d), lambda i: (i, 0)),
                  pl.BlockSpec((1, d), lambda i: (0, 0)),
                  pl.BlockSpec(w_in.shape, lambda i: (0, 0), pipeline_mode=pl.Buffered(1))],
        out_specs=[pl.BlockSpec((tm, n), lambda i: (i, 0)) for n in widths],
        out_shape=[jax.ShapeDtypeStruct((t, n), F32) for n in widths],
        scratch_shapes=[pltpu.VMEM((d, C_IN), BF16)],
        compiler_params=pltpu.CompilerParams(dimension_semantics=("arbitrary",),
                                             vmem_limit_bytes=VMEM_LIMIT),
        name="inproj",
    )(x2, norm_w, w_in)


def _gdn_kernel(qkv_ref, z_ref, ab_ref, cw_ref, alog_ref, dtb_ref, nw_ref, o_ref, ext_ref, s_ref):
    c = CHUNK
    bsz, rows, _ = qkv_ref.shape
    chunks_per_batch = rows // c

    @pl.when(pl.program_id(0) == 0)
    def _():
        ext_ref[:, 0:8, :] = jnp.zeros((bsz, 8, C_QKV), F32)
        s_ref[...] = jnp.zeros_like(s_ref)

    cw = cw_ref[...]
    nw = nw_ref[...]
    rr = lax.broadcasted_iota(jnp.int32, (rows, rows), 0)
    rc = lax.broadcasted_iota(jnp.int32, (rows, rows), 1)
    in_chunk_prefix = jnp.where(rr // c == rc // c, jnp.where(rr >= rc, 1.0, 0.0), 0.0)
    e8 = (lax.broadcasted_iota(jnp.int32, (8, LANES), 0)
          == lax.broadcasted_iota(jnp.int32, (8, LANES), 1)).astype(F32)
    ri = lax.broadcasted_iota(jnp.int32, (c, c), 0)
    ci = lax.broadcasted_iota(jnp.int32, (c, c), 1)
    causal = ri >= ci
    strict = ri > ci
    eye = (ri == ci).astype(F32)
    chains = [(j, h) for j in range(chunks_per_batch) for h in range(GDN_HEADS)]

    def prep_pieces(b):
        g = {"xc": [None] * (C_QKV // LANES)}

        def conv(tiles):
            def run():
                cols = slice(tiles[0] * LANES, (tiles[-1] + 1) * LANES)
                x = qkv_ref[b, :, cols]
                ext_ref[b, 8:8 + rows, cols] = x
                ext = ext_ref[b, :, cols]
                xb = cw[3:4, cols] * x
                for j in range(1, GDN_CONV):
                    xb = xb + cw[3 - j:4 - j, cols] * pltpu.roll(ext, j, axis=0)[8:8 + rows, :]
                ext_ref[b, 0:8, cols] = x[rows - 8:rows, :]
                xb = xb * _sigmoid(xb)
                for i, tile in enumerate(tiles):
                    g["xc"][tile] = xb[:, i * LANES:(i + 1) * LANES]
            return run

        def gates():
            ab = ab_ref[b]
            g_all = -jnp.exp(alog_ref[...]) * _softplus(ab + dtb_ref[...])
            g["beta"] = _sigmoid(ab)
            g["g_cum"] = _dot(in_chunk_prefix, g_all, HIGHEST)
            g["g_row"] = _dot_nt(e8, g["g_cum"], HIGHEST)
            for name in ("q", "k", "v", "gc", "bt", "dec"):
                g[name] = []

        def chunk(j):
            def run():
                rs = slice(j * c, (j + 1) * c)
                xc = g["xc"]
                for h in range(GDN_HEADS):
                    q = xc[h][rs]
                    k = xc[GDN_HEADS + h][rs]
                    g["q"].append(q * lax.rsqrt(jnp.sum(q * q, axis=-1, keepdims=True) + EPS) * (GDN_D ** -0.5))
                    g["k"].append(k * lax.rsqrt(jnp.sum(k * k, axis=-1, keepdims=True) + EPS))
                    g["v"].append(xc[2 * GDN_HEADS + h][rs])
                    gc = g["g_cum"][rs, h:h + 1]
                    gr = g["g_row"][h:h + 1, rs]
                    g["gc"].append(gc)
                    g["bt"].append(g["beta"][rs, GDN_HEADS + h:GDN_HEADS + h + 1])
                    g["dec"].append(jnp.exp(jnp.where(causal, gc - gr, -jnp.inf)))
            return run

        def a_mats():
            g["kb"] = [k * bt for k, bt in zip(g["k"], g["bt"])]
            g["k16"] = [k.astype(BF16) for k in g["k"]]
            g["a"] = [jnp.where(strict, _dot_nt(kb.astype(BF16), k16) * dec, 0.0)
                      for kb, k16, dec in zip(g["kb"], g["k16"], g["dec"])]

        n_tiles = C_QKV // LANES
        conv_pieces = [conv(list(range(i, i + 2))) for i in range(0, n_tiles, 2)]
        return g, conv_pieces + [gates] + [chunk(j) for j in range(chunks_per_batch)] + [a_mats]

    def inverse_rounds(g):
        def start():
            g["t"] = [eye - a for a in g["a"]]
            g["pw"] = [_split_bf16(a) for a in g["a"]]

        def square():
            g["pw"] = [_split_bf16(_dot_split(p, p)) for p in g["pw"]]

        def update():
            g["t"] = [t + _dot_split(_split_bf16(t), p) for t, p in zip(g["t"], g["pw"])]

        return [start] + [square, update] * (CHUNK.bit_length() - 2)

    groups = []
    pending = []
    for b in range(bsz):
        g, pieces = prep_pieces(b)
        groups.append(g)
        n_a, n_b = len(pending), len(pieces)
        ia = ib = 0
        while ia < n_a or ib < n_b:
            if ib >= n_b or (ia < n_a and ia * n_b <= ib * n_a):
                pending[ia]()
                ia += 1
            else:
                pieces[ib]()
                ib += 1
        pending = inverse_rounds(g)
    for fn in pending:
        fn()

    uws, qk16s, qd16s, kd16s, g_tots = [], [], [], [], []
    for g in groups:
        e_gs = [jnp.exp(gc) for gc in g["gc"]]
        uws.append([_dot(t.astype(BF16), jnp.concatenate([v * bt, kb * eg], axis=1).astype(BF16))
                    for t, v, bt, kb, eg in zip(g["t"], g["v"], g["bt"], g["kb"], e_gs)])
        qk16s.append([(_dot_nt(q.astype(BF16), k16) * dec).astype(BF16)
                      for q, k16, dec in zip(g["q"], g["k16"], g["dec"])])
        qd16s.append([(q * eg).astype(BF16) for q, eg in zip(g["q"], e_gs)])
        kd16s.append([(k * jnp.exp(gc[c - 1:c, :] - gc)).astype(BF16) for k, gc in zip(g["k"], g["gc"])])
        g_tots.append([jnp.exp(gc[c - 1:c, :]) for gc in g["gc"]])

    pairs = [(b, h) for b in range(bsz) for h in range(GDN_HEADS)]
    states = [s_ref[b * GDN_HEADS + h] for b, h in pairs]
    for j in range(chunks_per_batch):
        rs = slice(j * c, (j + 1) * c)
        pick = lambda per_group: [per_group[b][chains.index((j, h))] for b, h in pairs]
        uw_j, qk_j, qd_j, kd_j, gt_j = (pick(x) for x in (uws, qk16s, qd16s, kd16s, g_tots))
        s16s = [s.astype(BF16) for s in states]
        v16s = [(uw[:, :GDN_D] - _dot(uw[:, GDN_D:].astype(BF16), s16)).astype(BF16)
                for uw, s16 in zip(uw_j, s16s)]
        states = [s * gt + _dot_tn(kd, v16) for s, gt, kd, v16 in zip(states, gt_j, kd_j, v16s)]
        os_ = [_dot(qd, s16) + _dot(qk, v16) for qd, qk, s16, v16 in zip(qd_j, qk_j, s16s, v16s)]
        for (b, h), o in zip(pairs, os_):
            on = o * lax.rsqrt(jnp.mean(o * o, axis=-1, keepdims=True) + EPS) * nw
            zh = z_ref[b, rs, h * GDN_D:(h + 1) * GDN_D]
            o_ref[b, rs, h * GDN_D:(h + 1) * GDN_D] = on * (zh * _sigmoid(zh))
    for p, s in enumerate(states):
        s_ref[p] = s


def _gdn(qkv, z, ab, conv_w8, alog_row, dtb_row, norm_w, rows):
    bsz, seq, _ = qkv.shape
    blk = lambda i: (0, i, 0)
    const = lambda i: (0, 0)
    return pl.pallas_call(
        _gdn_kernel,
        grid=(seq // rows,),
        in_specs=[pl.BlockSpec((bsz, rows, C_QKV), blk),
                  pl.BlockSpec((bsz, rows, C_Z), blk),
                  pl.BlockSpec((bsz, rows, LANES), blk),
                  pl.BlockSpec((8, C_QKV), const),
                  pl.BlockSpec((1, LANES), const),
                  pl.BlockSpec((1, LANES), const),
                  pl.BlockSpec((1, GDN_D), const)],
        out_specs=pl.BlockSpec((bsz, rows, C_Z), blk),
        out_shape=jax.ShapeDtypeStruct((bsz, seq, C_Z), F32),
        scratch_shapes=[pltpu.VMEM((bsz, 8 + rows, C_QKV), F32),
                        pltpu.VMEM((bsz * GDN_HEADS, GDN_D, GDN_D), F32)],
        compiler_params=pltpu.CompilerParams(dimension_semantics=("arbitrary",),
                                             vmem_limit_bytes=VMEM_LIMIT),
        name="gdn",
    )(qkv, z, ab, conv_w8, alog_row, dtb_row, norm_w)


D_QK_PAD = LANES
C_HEADS = MLA_HEADS * D_QK_PAD
C_V = MLA_HEADS * MLA_V
V_AUG = MLA_V + 16
C_VAUG = MLA_HEADS * V_AUG
Q_SCALE = (MLA_NOPE + MLA_ROPE) ** -0.5 * math.log2(math.e)


def _mla_proj_kernel(cq_ref, ckv_ref, kr_ref, posr_ref, qnw_ref, kvnw_ref, wqt_ref,
                     wk_ref, wvt_ref, freqc_ref, qt_ref, k_ref, vt_ref):
    cq = cq_ref[...]
    cqn = (cq * lax.rsqrt(jnp.mean(cq * cq, axis=-1, keepdims=True) + EPS) * qnw_ref[...]).astype(BF16)
    ckv = ckv_ref[...]
    ckvn = (ckv * lax.rsqrt(jnp.mean(ckv * ckv, axis=-1, keepdims=True) + EPS)
            * kvnw_ref[...]).astype(BF16)
    tm = posr_ref.shape[-1]
    ang = freqc_ref[...] * posr_ref[...]
    c16 = jnp.cos(ang)
    s16 = jnp.sin(ang)
    pad = LANES - MLA_NOPE - MLA_ROPE
    cs_t = jnp.concatenate([jnp.ones((MLA_NOPE, tm), F32), c16, c16, jnp.ones((pad, tm), F32)], axis=0)
    sn_t = jnp.concatenate([jnp.zeros((MLA_NOPE, tm), F32), s16, s16, jnp.zeros((pad, tm), F32)], axis=0)
    cs = cs_t.T
    sn = sn_t.T
    q_t = _dot_nt(wqt_ref[...], cqn)
    kn = _dot(ckvn, wk_ref[...])
    half = MLA_ROPE // 2
    r0, r1, r2 = MLA_NOPE, MLA_NOPE + half, MLA_NOPE + MLA_ROPE
    lane = lax.broadcasted_iota(jnp.int32, (tm, LANES), 1)
    kr = jnp.where(lane >= r0, jnp.where(lane < r2, kr_ref[...], 0.0), 0.0)
    kr_rot = jnp.where(lane < r1, -pltpu.roll(kr, LANES - half, axis=1), pltpu.roll(kr, half, axis=1))
    kpe = kr * cs + kr_rot * sn
    for h in range(MLA_HEADS):
        sl = slice(h * LANES, (h + 1) * LANES)
        q = q_t[sl, :]
        rot = jnp.concatenate([jnp.zeros((r0, tm), F32), -q[r1:r2], q[r0:r1], jnp.zeros((pad, tm), F32)], axis=0)
        qt_ref[sl, :] = ((q * cs_t + rot * sn_t) * Q_SCALE).astype(BF16)
        k_ref[:, sl] = (kn[:, sl] + kpe).astype(BF16)
    v_t = _dot_nt(wvt_ref[...], ckvn).astype(BF16)
    ones = jnp.ones((V_AUG - MLA_V, v_t.shape[1]), BF16)
    for h in range(MLA_HEADS):
        vt_ref[0, h * V_AUG:h * V_AUG + MLA_V, :] = v_t[h * MLA_V:(h + 1) * MLA_V, :]
        vt_ref[0, h * V_AUG + MLA_V:(h + 1) * V_AUG, :] = ones


def _mla_proj(cq, ckv, abk, posr, qnw, kvnw, wqt, wk, wvt, freqc, tm):
    t = cq.shape[0]
    row = lambda i: (i, 0)
    col = lambda i: (0, i)
    const = lambda i: (0, 0)
    return pl.pallas_call(
        _mla_proj_kernel,
        grid=(t // tm,),
        in_specs=[pl.BlockSpec((tm, MLA_Q_LORA), row),
                  pl.BlockSpec((tm, MLA_KV_LORA), row),
                  pl.BlockSpec((tm, LANES), row),
                  pl.BlockSpec((1, tm), col),
                  pl.BlockSpec((1, MLA_Q_LORA), const),
                  pl.BlockSpec((1, MLA_KV_LORA), const),
                  pl.BlockSpec((C_HEADS, MLA_Q_LORA), const),
                  pl.BlockSpec((MLA_KV_LORA, C_HEADS), const),
                  pl.BlockSpec((C_V, MLA_KV_LORA), const),
                  pl.BlockSpec((MLA_ROPE // 2, 1), const)],
        out_specs=[pl.BlockSpec((C_HEADS, tm), col),
                   pl.BlockSpec((tm, C_HEADS), row),
                   pl.BlockSpec((1, C_VAUG, tm), lambda i: (i, 0, 0))],
        out_shape=[jax.ShapeDtypeStruct((C_HEADS, t), BF16),
                   jax.ShapeDtypeStruct((t, C_HEADS), BF16),
                   jax.ShapeDtypeStruct((t // tm, C_VAUG, tm), BF16)],
        compiler_params=pltpu.CompilerParams(dimension_semantics=("parallel",),
                                             vmem_limit_bytes=VMEM_LIMIT),
        name="mla_proj",
    )(cq, ckv, abk, posr, qnw, kvnw, wqt, wk, wvt, freqc)


def _attn_kernel(qt_ref, k_ref, vt_ref, nw_ref, ot_ref, m_ref, acc_ref, *, tq):
    i = pl.program_id(1)
    m_ref[...] = jnp.full(m_ref.shape, NEG_BIG, F32)
    acc_ref[...] = jnp.zeros(acc_ref.shape, F32)
    def tile(j, k_lo, n_keys, q_lo, masked):
        rows = pl.ds(pl.multiple_of(j * tq + k_lo, n_keys), n_keys)
        qs = slice(q_lo, tq)
        if masked:
            key_chunk = (k_lo + lax.broadcasted_iota(jnp.int32, (n_keys, tq - q_lo), 0)) // CHUNK
            qry_chunk = (q_lo + lax.broadcasted_iota(jnp.int32, (n_keys, tq - q_lo), 1)) // CHUNK
            mask = key_chunk <= qry_chunk
        scores = []
        for h in range(MLA_HEADS):
            hs = slice(h * LANES, (h + 1) * LANES)
            s_t = _dot(k_ref[rows, hs], qt_ref[hs, qs])
            scores.append(jnp.where(mask, s_t, NEG_BIG) if masked else s_t)
        probs, alphas = [], []
        for h in range(MLA_HEADS):
            m_old = m_ref[h, :, qs]
            m_new = jnp.maximum(m_old, jnp.max(scores[h], axis=0, keepdims=True))
            m_ref[h, :, qs] = m_new
            probs.append(jnp.exp2(scores[h] - m_new[0:1, :]).astype(BF16))
            alphas.append(jnp.exp2(m_old[0:1, :] - m_new[0:1, :]))
        for h in range(MLA_HEADS):
            v_aug = vt_ref[j, h * V_AUG:(h + 1) * V_AUG, k_lo:k_lo + n_keys]
            acc_ref[h, :, qs] = alphas[h] * acc_ref[h, :, qs] + _dot(v_aug, probs[h])

    def full_tile(j, carry):
        tile(j, 0, tq, 0, False)
        return carry

    lax.fori_loop(0, i, full_tile, 0)
    half = tq // 2
    tile(i, 0, half, 0, True)
    tile(i, half, half, half, True)

    outs = []
    for h in range(MLA_HEADS):
        acc = acc_ref[h]
        outs.append(acc[:MLA_V, :] / acc[MLA_V:MLA_V + 1, :])
    ssq = outs[0] * outs[0]
    for o in outs[1:]:
        ssq = ssq + o * o
    inv = lax.rsqrt(jnp.sum(ssq, axis=0, keepdims=True) / C_V + EPS)
    for h in range(MLA_HEADS):
        vs = slice(h * MLA_V, (h + 1) * MLA_V)
        ot_ref[vs, :] = outs[h] * inv * nw_ref[vs, :]


def _attention(qt, k, vt, norm_w_col, bsz, seq, tq):
    nq = seq // tq
    return pl.pallas_call(
        functools.partial(_attn_kernel, tq=tq),
        grid=(bsz, nq),
        in_specs=[pl.BlockSpec((C_HEADS, tq), lambda b, i: (0, b * nq + i)),
                  pl.BlockSpec((seq, C_HEADS), lambda b, i: (b, 0)),
                  pl.BlockSpec((nq, C_VAUG, tq), lambda b, i: (b, 0, 0)),
                  pl.BlockSpec((C_V, 1), lambda b, i: (0, 0))],
        out_specs=pl.BlockSpec((C_V, tq), lambda b, i: (0, b * nq + i)),
        out_shape=jax.ShapeDtypeStruct((C_V, bsz * seq), F32),
        scratch_shapes=[pltpu.VMEM((MLA_HEADS, 8, tq), F32),
                        pltpu.VMEM((MLA_HEADS, V_AUG, tq), F32)],
        compiler_params=pltpu.CompilerParams(dimension_semantics=("parallel", "parallel"),
                                             vmem_limit_bytes=VMEM_LIMIT),
        name="attention",
    )(qt, k, vt, norm_w_col)


def _router_kernel(x_ref, oa_ref, obt_ref, wo_ref, nw_ref, rw_ref, rb_ref,
                   x1_ref, h_ref, eidx_ref, gate_ref, rank_ref, cnt_ref, run_ref, *, tm):
    @pl.when(pl.program_id(0) == 0)
    def _():
        run_ref[...] = jnp.zeros_like(run_ref)

    half = oa_ref.shape[-1]
    y = (_dot(oa_ref[...].astype(BF16), wo_ref[0:half, :])
         + _dot_tn(obt_ref[...].astype(BF16), wo_ref[half:, :]))
    x1 = x_ref[...] + y
    x1_ref[...] = x1
    hn = x1 * lax.rsqrt(jnp.mean(x1 * x1, axis=-1, keepdims=True) + EPS) * nw_ref[...]
    d_half = hn.shape[-1] // 2
    _store_packed_rows(h_ref, _pack_bf16_pair(hn[:, :d_half], hn[:, d_half:]))
    lg = _dot_split_nt(_split_bf16(rw_ref[...]), _split_bf16(hn)) + rb_ref[...]
    expert = lax.broadcasted_iota(jnp.int32, lg.shape, 0)
    sels, tops, idxs = [], [], []
    for _ in range(TOP_K):
        m = jnp.max(lg, axis=0, keepdims=True)
        idx = jnp.min(jnp.where(lg == m, expert, N_EXPERTS), axis=0, keepdims=True)
        sel = expert == idx
        lg = jnp.where(sel, -jnp.inf, lg)
        sels.append(sel)
        tops.append(m)
        idxs.append(idx)
    exps = [jnp.exp(tv - tops[0]) for tv in tops]
    den = functools.reduce(lambda a, b: a + b, exps)
    multi = sels[0].astype(F32)
    for sel in sels[1:]:
        multi = multi + sel.astype(F32)
    ri = lax.broadcasted_iota(jnp.int32, (tm, tm), 0)
    ci = lax.broadcasted_iota(jnp.int32, (tm, tm), 1)
    before = _dot(multi.astype(BF16), (ri < ci).astype(BF16)) + run_ref[...]
    ranks = [jnp.sum(jnp.where(sel, before, 0.0), axis=0, keepdims=True).astype(jnp.int32) for sel in sels]
    run = run_ref[...] + jnp.sum(multi, axis=1, keepdims=True)
    run_ref[...] = run
    cnt_ref[...] = jnp.broadcast_to(run, cnt_ref.shape)
    pad_i = jnp.zeros((8 - TOP_K, tm), jnp.int32)
    eidx_ref[...] = jnp.concatenate(idxs + [pad_i], axis=0)
    rank_ref[...] = jnp.concatenate(ranks + [pad_i], axis=0)
    gates_t = jnp.concatenate([e / den for e in exps] + [jnp.zeros((LANES - TOP_K, tm), F32)], axis=0)
    gate_ref[...] = gates_t.T


def _router(x2, oa, obt, w_out, norm_w, rw, rb, tm):
    t, d = x2.shape
    half = oa.shape[-1]
    s_words = d // 2 // LANES
    row = lambda i: (i, 0)
    col = lambda i: (0, i)
    const = lambda i: (0, 0)
    return pl.pallas_call(
        functools.partial(_router_kernel, tm=tm),
        grid=(t // tm,),
        in_specs=[pl.BlockSpec((tm, d), row),
                  pl.BlockSpec((tm, half), row),
                  pl.BlockSpec((half, tm), lambda i: (0, i)),
                  pl.BlockSpec((2 * half, d), const),
                  pl.BlockSpec((1, d), const),
                  pl.BlockSpec((N_EXPERTS, d), const),
                  pl.BlockSpec((N_EXPERTS, 1), const)],
        out_specs=[pl.BlockSpec((tm, d), row), pl.BlockSpec((tm * s_words, LANES), row),
                   pl.BlockSpec((8, tm), col), pl.BlockSpec((tm, LANES), row),
                   pl.BlockSpec((8, tm), col), pl.BlockSpec((N_EXPERTS, LANES), const)],
        out_shape=[jax.ShapeDtypeStruct((t, d), F32), jax.ShapeDtypeStruct((t * s_words, LANES), jnp.int32),
                   jax.ShapeDtypeStruct((8, t), jnp.int32), jax.ShapeDtypeStruct((t, LANES), F32),
                   jax.ShapeDtypeStruct((8, t), jnp.int32), jax.ShapeDtypeStruct((N_EXPERTS, LANES), F32)],
        scratch_shapes=[pltpu.VMEM((N_EXPERTS, 1), F32)],
        compiler_params=pltpu.CompilerParams(dimension_semantics=("arbitrary",),
                                             vmem_limit_bytes=VMEM_LIMIT),
        name="router",
    )(x2, oa, obt, w_out, norm_w, rw, rb)


SC_ROWS = 64


def _sc_mesh():
    return plsc.VectorSubcoreMesh(core_axis_name="c", subcore_axis_name="s")


def _dispatch(dest_chunks, h3, n_pad):
    t, s, _ = h3.shape
    mesh = _sc_mesh()
    n_workers = mesh.num_cores * mesh.num_subcores
    per_worker = t // n_workers
    n_chunks = per_worker // SC_ROWS
    assert per_worker % SC_ROWS == 0 and n_chunks % 2 == 0

    @functools.partial(
        pl.kernel, mesh=mesh, out_type=jax.ShapeDtypeStruct((n_pad, s, LANES), h3.dtype),
        scratch_types=[pltpu.VMEM((2, TOP_K * SC_ROWS), jnp.int32), pltpu.VMEM((2, SC_ROWS, s, LANES), h3.dtype),
                       pltpu.SemaphoreType.DMA((2,)), pltpu.SemaphoreType.DMA((2,)), pltpu.SemaphoreType.DMA],
        name="sc_dispatch")
    def scatter(h_hbm, dest_hbm, out_hbm, idx_v, rows_v, rsem, isem, ssem):
        base = (lax.axis_index("s") * mesh.num_cores + lax.axis_index("c")) * per_worker

        def loads(chunk, slot):
            off = pl.multiple_of(base + chunk * SC_ROWS, SC_ROWS)
            return (pltpu.make_async_copy(h_hbm.at[pl.ds(off, SC_ROWS)], rows_v.at[slot], rsem.at[slot]),
                    pltpu.make_async_copy(dest_hbm.at[pl.ds(off * TOP_K, SC_ROWS * TOP_K)], idx_v.at[slot],
                                          isem.at[slot]))

        for cp in loads(0, 0):
            cp.start()

        @pl.loop(0, n_chunks // 2)
        def _(pair):
            for slot in (0, 1):
                chunk = 2 * pair + slot
                for cp in loads(chunk, slot):
                    cp.wait()

                @pl.when(chunk + 1 < n_chunks)
                def _():
                    for cp in loads(chunk + 1, 1 - slot):
                        cp.start()

                copies = [pltpu.async_copy(rows_v.at[slot],
                                           out_hbm.at[idx_v.at[slot, pl.ds(kk * SC_ROWS, SC_ROWS)]], ssem)
                          for kk in range(TOP_K)]
                for cp in copies:
                    cp.wait()

    return scatter(h3, dest_chunks)


def _expert_kernel(cnt_ref, blk0_ref, nblk_ref, total_ref, x_hbm, w1_ref, b1g_ref, b1l_ref, w2_ref, b2_ref,
                   y_hbm, w1s_ref, w2s_ref, xbuf, ybuf, xsem, ysem, *, bm):
    e = pl.program_id(0)
    n_blocks = nblk_ref[e]
    first = blk0_ref[e]
    total = total_ref[0]
    rows_per_block = xbuf.shape[1]
    s_words = rows_per_block // bm
    n_tiles = w1s_ref.shape[-1] // MXU_COLS
    half = MXU_COLS // 2

    def x_copy(g, slot):
        return pltpu.make_async_copy(x_hbm.at[pl.ds(pl.multiple_of(g * rows_per_block, rows_per_block),
                                                    rows_per_block)], xbuf.at[slot], xsem.at[slot])

    def y_copy(g, slot):
        return pltpu.make_async_copy(ybuf.at[slot], y_hbm.at[pl.ds(pl.multiple_of(g * rows_per_block,
                                                                                   rows_per_block),
                                                                   rows_per_block)], ysem.at[slot])

    @pl.when(e == 0)
    def _():
        x_copy(0, 0).start()

    @pl.when(n_blocks > 0)
    def _():
        r = lax.broadcasted_iota(jnp.int32, (MXU_COLS, MXU_COLS), 0)
        c = lax.broadcasted_iota(jnp.int32, (MXU_COLS, MXU_COLS), 1)
        perm = (r == jnp.where(c < half, 2 * c, 2 * (c - half) + 1)).astype(BF16)
        for tix in range(n_tiles):
            cols = slice(tix * MXU_COLS, (tix + 1) * MXU_COLS)
            w1s_ref[:, cols] = _dot(w1_ref[0, :, cols].astype(BF16), perm).astype(BF16)
        w2s_ref[...] = w2_ref[0].astype(BF16)

        def block(j, carry):
            g = first + j
            slot = lax.rem(g, 2)
            x_copy(g, slot).wait()

            @pl.when(g + 1 < total)
            def _():
                x_copy(g + 1, 1 - slot).start()

            @pl.when(g >= 2)
            def _():
                y_copy(g - 2, slot).wait()

            n_rows = cnt_ref[e] - j * bm

            def ffn(m):
                words = _load_packed_rows(xbuf, (slot,), m, s_words)
                row = lax.broadcasted_iota(jnp.int32, words.shape, 0)
                x_lo, x_hi = _unpack_bf16_pair(jnp.where(row < n_rows, words, 0))
                x = jnp.concatenate([x_lo.astype(BF16), x_hi.astype(BF16)], axis=1)
                acts = []
                for tix in range(n_tiles):
                    hp = _dot(x, w1s_ref[:, tix * MXU_COLS:(tix + 1) * MXU_COLS])
                    feat = slice(tix * half, (tix + 1) * half)
                    gl = jnp.minimum(hp[:, :half] + b1g_ref[0][:, feat], SWIGLU_LIMIT)
                    lin = jnp.clip(hp[:, half:] + b1l_ref[0][:, feat], -SWIGLU_LIMIT, SWIGLU_LIMIT)
                    acts.append((gl * _sigmoid(SWIGLU_ALPHA * gl) * (lin + 1.0)).astype(BF16))
                y = _dot(jnp.concatenate(acts, axis=1), w2s_ref[...]) + b2_ref[0]
                d_half = y.shape[-1] // 2
                _store_packed_rows(ybuf.at[slot], _pack_bf16_pair(y[:, :d_half], y[:, d_half:]))

            @pl.when(n_rows > bm // 2)
            def _():
                ffn(bm)

            @pl.when(n_rows <= bm // 2)
            def _():
                ffn(bm // 2)

            y_copy(g, slot).start()
            return carry

        lax.fori_loop(0, n_blocks, block, 0)

    @pl.when(e == pl.num_programs(0) - 1)
    def _():
        @pl.when(total >= 2)
        def _():
            y_copy(total - 2, lax.rem(total, 2)).wait()

        y_copy(total - 1, lax.rem(total - 1, 2)).wait()


def _experts(tables, xs, w1, b1g, b1l, w2, b2, bm, d):
    s_words = d // 2 // LANES
    dff2 = w1.shape[-1]
    dff = dff2 // 2
    emap = lambda e, *_: (e, 0, 0)
    grid_spec = pltpu.PrefetchScalarGridSpec(
        num_scalar_prefetch=len(tables),
        grid=(N_EXPERTS,),
        in_specs=[pl.BlockSpec(memory_space=pl.ANY),
                  pl.BlockSpec((1, d, dff2), emap),
                  pl.BlockSpec((1, 1, dff), emap),
                  pl.BlockSpec((1, 1, dff), emap),
                  pl.BlockSpec((1, dff, d), emap),
                  pl.BlockSpec((1, 1, d), emap)],
        out_specs=pl.BlockSpec(memory_space=pl.ANY),
        scratch_shapes=[pltpu.VMEM((d, dff2), BF16), pltpu.VMEM((dff, d), BF16),
                        pltpu.VMEM((2, bm * s_words, LANES), jnp.int32),
                        pltpu.VMEM((2, bm * s_words, LANES), jnp.int32),
                        pltpu.SemaphoreType.DMA((2,)), pltpu.SemaphoreType.DMA((2,))],
    )
    return pl.pallas_call(
        functools.partial(_expert_kernel, bm=bm),
        grid_spec=grid_spec,
        out_shape=jax.ShapeDtypeStruct(xs.shape, jnp.int32),
        compiler_params=pltpu.CompilerParams(dimension_semantics=("arbitrary",),
                                             vmem_limit_bytes=VMEM_LIMIT),
        name="experts",
    )(*tables, xs, w1, b1g, b1l, w2, b2)


def _gather_rows(dest_flat, y3):
    n_out = dest_flat.shape[0]
    _, s, _ = y3.shape
    mesh = _sc_mesh()
    n_workers = mesh.num_cores * mesh.num_subcores
    per_worker = n_out // n_workers
    n_chunks = per_worker // SC_ROWS
    assert per_worker % SC_ROWS == 0 and n_chunks % 2 == 0

    @functools.partial(
        pl.kernel, mesh=mesh, out_type=jax.ShapeDtypeStruct((n_out, s, LANES), y3.dtype),
        scratch_types=[pltpu.VMEM((2, SC_ROWS), jnp.int32), pltpu.VMEM((2, SC_ROWS, s, LANES), y3.dtype),
                       pltpu.SemaphoreType.DMA((2,)), pltpu.SemaphoreType.DMA((2,))],
        name="sc_gather")
    def gather(y_hbm, dest_hbm, out_hbm, idx_v, rows_v, gsem, wsem):
        base = (lax.axis_index("s") * mesh.num_cores + lax.axis_index("c")) * per_worker

        def rows_of(chunk):
            return pl.ds(pl.multiple_of(base + chunk * SC_ROWS, SC_ROWS), SC_ROWS)

        def gather_copy(slot):
            return pltpu.make_async_copy(y_hbm.at[idx_v.at[slot]], rows_v.at[slot], gsem.at[slot])

        def write_copy(chunk, slot):
            return pltpu.make_async_copy(rows_v.at[slot], out_hbm.at[rows_of(chunk)], wsem.at[slot])

        def start_gather(chunk, slot):
            pltpu.sync_copy(dest_hbm.at[rows_of(chunk)], idx_v.at[slot])
            gather_copy(slot).start()

        start_gather(0, 0)

        @pl.loop(0, n_chunks // 2)
        def _(pair):
            for slot in (0, 1):
                chunk = 2 * pair + slot
                gather_copy(slot).wait()
                write_copy(chunk, slot).start()

                @pl.when(chunk >= 1)
                def _():
                    write_copy(chunk - 1, 1 - slot).wait()

                @pl.when(chunk + 1 < n_chunks)
                def _():
                    start_gather(chunk + 1, 1 - slot)

        write_copy(n_chunks - 1, 1).wait()

    return gather(y3, dest_flat)


def _combine_kernel(x1_ref, gate_ref, y4_ref, nw_ref, o_ref, *, tm, normalize):
    x = x1_ref[...]
    gates = gate_ref[...]
    for kk in range(TOP_K):
        lo, hi = _unpack_bf16_pair(_load_packed_rows(y4_ref, (kk,), tm, y4_ref.shape[1] // tm))
        x = x + gates[:, kk:kk + 1] * jnp.concatenate([lo, hi], axis=1)
    if normalize:
        x = x * lax.rsqrt(jnp.mean(x * x, axis=-1, keepdims=True) + EPS) * nw_ref[...]
    o_ref[...] = x


def _combine(x1, gates, y4, norm_w, tm, normalize):
    t, d = x1.shape
    s_words = d // 2 // LANES
    row = lambda i: (i, 0)
    return pl.pallas_call(
        functools.partial(_combine_kernel, tm=tm, normalize=normalize),
        grid=(t // tm,),
        in_specs=[pl.BlockSpec((tm, d), row), pl.BlockSpec((tm, LANES), row),
                  pl.BlockSpec((TOP_K, tm * s_words, LANES), lambda i: (0, i, 0)),
                  pl.BlockSpec((1, d), lambda i: (0, 0))],
        out_specs=pl.BlockSpec((tm, d), row),
        out_shape=jax.ShapeDtypeStruct((t, d), F32),
        compiler_params=pltpu.CompilerParams(dimension_semantics=("parallel",)),
        name="combine",
    )(x1, gates, y4, norm_w)


def _tiles(t, seq):
    tm = min(2 * MXU_COLS, t)
    tr = min(2 * MXU_COLS, t)
    tq = min(2 * MXU_COLS, seq)
    gdn_rows = min(4 * CHUNK, seq)
    bm = 2 * MXU_COLS
    assert t % tm == 0 and t % tr == 0 and seq % tq == 0 and seq % gdn_rows == 0 and tq % CHUNK == 0
    return tm, tr, tq, gdn_rows, bm


def _prep_mla_weights(w_uq, w_ukv):
    dq = MLA_NOPE + MLA_ROPE
    wq = w_uq.reshape(MLA_Q_LORA, MLA_HEADS, dq)
    pad = lambda w: jnp.pad(w, ((0, 0), (0, 0), (0, LANES - w.shape[-1]))).reshape(w.shape[0], C_HEADS)
    wkv = w_ukv.reshape(MLA_KV_LORA, MLA_HEADS, MLA_NOPE + MLA_V)
    wv = wkv[..., MLA_NOPE:].reshape(MLA_KV_LORA, C_V)
    return pad(wq).T.astype(BF16), pad(wkv[..., :MLA_NOPE]).astype(BF16), wv.T.astype(BF16)


def _rope_freq():
    half = MLA_ROPE // 2
    return (ROPE_THETA ** (-jnp.arange(half, dtype=F32) / half)).reshape(half, 1)


def _lane_row(v):
    return jnp.pad(v.astype(F32), (0, LANES - v.shape[0])).reshape(1, LANES)


def _routing_tables(cnt, eidx, rank, bm):
    e_ids = jnp.arange(N_EXPERTS, dtype=jnp.int32)
    counts = cnt[:, 0].astype(jnp.int32)
    padded = (counts + bm - 1) // bm * bm
    pend = jnp.sum(jnp.where(e_ids[None, :] <= e_ids[:, None], padded[None, :], 0), axis=1)
    pstart = pend - padded
    dest = jnp.sum(jnp.where(eidx[:TOP_K, :, None] == e_ids, pstart, 0), axis=-1) + rank[:TOP_K]
    tables = (counts, pstart // bm, padded // bm, pend[-1:] // bm)
    return dest.reshape(-1), tuple(t.astype(jnp.int32) for t in tables)


def kernel(x, positions, norm_mix_w, w_in, conv_w, a_log, dt_bias, gdn_norm_w, q_norm_w, w_uq, kv_norm_w,
           w_ukv, mla_out_norm_w, w_out, norm_ffn_w, router_w, router_b, w1, b1, w2, b2, norm_final_w):
    bsz, seq, d = x.shape
    t = bsz * seq
    depth = w_in.shape[0]
    tm, tr, tq, gdn_rows, bm = _tiles(t, seq)
    x2 = x.reshape(t, d)
    posf = positions.astype(F32)
    freq = _rope_freq()
    for l in range(depth):
        qkv, z, cq, ckv, abk = _inproj(x2, norm_mix_w[l].reshape(1, d), w_in[l], tm)
        conv_w8 = jnp.pad(conv_w[l], ((0, 8 - GDN_CONV), (0, 0)))
        o_a = _gdn(qkv.reshape(bsz, seq, C_QKV), z.reshape(bsz, seq, C_Z), abk.reshape(bsz, seq, LANES),
                   conv_w8, _lane_row(a_log[l]), _lane_row(dt_bias[l]), gdn_norm_w[l].reshape(1, GDN_D),
                   gdn_rows).reshape(t, C_Z)
        wqt, wk, wvt = _prep_mla_weights(w_uq[l], w_ukv[l])
        qt, k, vt = _mla_proj(cq, ckv, abk, posf.reshape(1, t),
                              q_norm_w[l].reshape(1, -1), kv_norm_w[l].reshape(1, -1),
                              wqt, wk, wvt, freq, tq)
        o_bt = _attention(qt, k, vt, mla_out_norm_w[l].reshape(-1, 1), bsz, seq, tq)
        x1, hn, eidx, gates, rank, cnt = _router(x2, o_a, o_bt, w_out[l].astype(BF16),
                                                 norm_ffn_w[l].reshape(1, d), router_w[l].T,
                                                 router_b[l].reshape(N_EXPERTS, 1), tr)
        n_pad = t * TOP_K + N_EXPERTS * bm
        dest_flat, tables = _routing_tables(cnt, eidx, rank, bm)
        s_words = d // 2 // LANES
        dest_chunks = dest_flat.reshape(TOP_K, t // SC_ROWS, SC_ROWS).transpose(1, 0, 2).reshape(-1)
        buf = _dispatch(dest_chunks, hn.reshape(t, s_words, LANES), n_pad)
        yb = _experts(tables, buf.reshape(n_pad * s_words, LANES), w1[l],
                      b1[l][:, None, 0::2], b1[l][:, None, 1::2], w2[l], b2[l][:, None, :], bm, d)
        y4 = _gather_rows(dest_flat, yb.reshape(n_pad, s_words, LANES))
        x2 = _combine(x1, gates, y4.reshape(TOP_K, t * s_words, LANES), norm_final_w.reshape(1, d), tr,
                      normalize=(l == depth - 1))
    return x2.reshape(bsz, seq, d)
```

```python
import functools
import math

import jax
import jax.numpy as jnp
from jax import lax
from jax.experimental import pallas as pl
from jax.experimental.pallas import tpu as pltpu
from jax.experimental.pallas import tpu_sc as plsc

F32 = jnp.float32
BF16 = jnp.bfloat16
HIGHEST = lax.Precision.HIGHEST

LANES = 128
MXU_COLS = 256
EPS = 1e-6
CHUNK = 64

GDN_HEADS = 4
GDN_D = 128
GDN_CONV = 4
MLA_HEADS = 8
MLA_NOPE = 64
MLA_ROPE = 32
MLA_V = 64
MLA_Q_LORA = 384
MLA_KV_LORA = 256
ROPE_THETA = 10000.0
N_EXPERTS = 32
TOP_K = 4
SWIGLU_LIMIT = 7.0
SWIGLU_ALPHA = 1.702

NEG_BIG = -1e30
VMEM_LIMIT = 52 * 1024 * 1024


def _sigmoid(x):
    return 1.0 / (1.0 + jnp.exp(-x))


def _softplus(x):
    return jnp.maximum(x, 0.0) + jnp.log(1.0 + jnp.exp(-jnp.abs(x)))


def _dot(a, b, precision=None):
    return jnp.dot(a, b, preferred_element_type=F32, precision=precision)


def _dot_nt(a, b, precision=None):
    return lax.dot_general(a, b, (((1,), (1,)), ((), ())), preferred_element_type=F32,
                           precision=precision)


def _dot_tn(a, b):
    return lax.dot_general(a, b, (((0,), (0,)), ((), ())), preferred_element_type=F32)


def _split_bf16(a):
    hi = a.astype(BF16)
    return hi, (a - hi.astype(F32)).astype(BF16)


def _pack_bf16_pair(lo, hi):
    lo_bits = pltpu.bitcast(lo.astype(BF16).astype(F32), jnp.int32)
    hi_bits = pltpu.bitcast(hi.astype(BF16).astype(F32), jnp.int32)
    return jnp.bitwise_or(hi_bits, lax.shift_right_logical(lo_bits, 16))


def _unpack_bf16_pair(w):
    lo = pltpu.bitcast(lax.shift_left(w, 16), F32)
    hi = pltpu.bitcast(jnp.bitwise_and(w, -65536), F32)
    return lo, hi


def _load_packed_rows(ref, lead, m, s):
    return jnp.concatenate([ref[(*lead, pl.ds(j, m, stride=s), slice(None))] for j in range(s)], axis=1)


def _store_packed_rows(ref, words):
    m = words.shape[0]
    s = words.shape[1] // LANES
    for j in range(s):
        ref[pl.ds(j, m, stride=s), :] = words[:, j * LANES:(j + 1) * LANES]


def _dot_split_nt(a, b):
    return _dot_nt(a[0], b[0]) + _dot_nt(a[0], b[1]) + _dot_nt(a[1], b[0])


def _dot_split(a, b):
    return _dot(a[0], b[0]) + _dot(a[0], b[1]) + _dot(a[1], b[0])


C_QKV = 3 * GDN_HEADS * GDN_D
C_Z = GDN_HEADS * GDN_D
C_IN = C_QKV + C_Z + MLA_Q_LORA + MLA_KV_LORA + LANES


def _inproj_kernel(x_ref, nw_ref, w_ref, qkv_ref, z_ref, cq_ref, ckv_ref, abk_ref, wb_ref):
    @pl.when(pl.program_id(0) == 0)
    def _():
        d = w_ref.shape[0]
        hd4 = C_QKV + C_Z
        wb_ref[:, 0:hd4] = w_ref[:, 0:hd4].astype(BF16)
        tail = w_ref[:, hd4:]
        o_cq = 2 * GDN_HEADS
        o_ckv = o_cq + MLA_Q_LORA
        o_kr = o_ckv + MLA_KV_LORA
        zeros = lambda n: jnp.zeros((d, n), F32)
        groups = [tail[:, o_cq:o_ckv], tail[:, o_ckv:o_kr],
                  jnp.concatenate([tail[:, 0:o_cq], zeros(MLA_NOPE - o_cq), tail[:, o_kr:o_kr + MLA_ROPE],
                                   zeros(LANES - MLA_NOPE - MLA_ROPE)], axis=1)]
        o = hd4
        for grp in groups:
            wb_ref[:, o:o + grp.shape[1]] = grp.astype(BF16)
            o += grp.shape[1]

    x = x_ref[...]
    var = jnp.mean(x * x, axis=-1, keepdims=True)
    h = (x * lax.rsqrt(var + EPS) * nw_ref[...]).astype(BF16)
    p = _dot(h, wb_ref[...])
    o = 0
    for ref in (qkv_ref, z_ref, cq_ref, ckv_ref, abk_ref):
        n = ref.shape[-1]
        ref[...] = p[:, o:o + n]
        o += n


def _inproj(x2, norm_w, w_in, tm):
    t, d = x2.shape
    widths = (C_QKV, C_Z, MLA_Q_LORA, MLA_KV_LORA, LANES)
    return pl.pallas_call(
        _inproj_kernel,
        grid=(t // tm,),
        in_specs=[pl.BlockSpec((tm, d), lambda i: (i, 0)),
                  pl.BlockSpec((1, d), lambda i: (0, 0)),
                  pl.BlockSpec(w_in.shape, lambda i: (0, 0), pipeline_mode=pl.Buffered(1))],
        out_specs=[pl.BlockSpec((tm, n), lambda i: (i, 0)) for n in widths],
        out_shape=[jax.ShapeDtypeStruct((t, n), F32) for n in widths],
        scratch_shapes=[pltpu.VMEM((d, C_IN), BF16)],
        compiler_params=pltpu.CompilerParams(dimension_semantics=("arbitrary",),
                                             vmem_limit_bytes=VMEM_LIMIT),
        name="inproj",
    )(x2, norm_w, w_in)


def _gdn_kernel(qkv_ref, z_ref, ab_ref, cw_ref, alog_ref, dtb_ref, nw_ref, o_ref, ext_ref, s_ref):
    c = CHUNK
    bsz, rows, _ = qkv_ref.shape
    chunks_per_batch = rows // c

    @pl.when(pl.program_id(0) == 0)
    def _():
        ext_ref[:, 0:8, :] = jnp.zeros((bsz, 8, C_QKV), F32)
        s_ref[...] = jnp.zeros_like(s_ref)

    cw = cw_ref[...]
    nw = nw_ref[...]
    rr = lax.broadcasted_iota(jnp.int32, (rows, rows), 0)
    rc = lax.broadcasted_iota(jnp.int32, (rows, rows), 1)
    in_chunk_prefix = jnp.where(rr // c == rc // c, jnp.where(rr >= rc, 1.0, 0.0), 0.0)
    e8 = (lax.broadcasted_iota(jnp.int32, (8, LANES), 0)
          == lax.broadcasted_iota(jnp.int32, (8, LANES), 1)).astype(F32)
    ri = lax.broadcasted_iota(jnp.int32, (c, c), 0)
    ci = lax.broadcasted_iota(jnp.int32, (c, c), 1)
    causal = ri >= ci
    strict = ri > ci
    eye = (ri == ci).astype(F32)
    chains = [(j, h) for j in range(chunks_per_batch) for h in range(GDN_HEADS)]

    def prep_pieces(b):
        g = {"xc": [None] * (C_QKV // LANES)}

        def conv(tiles):
            def run():
                cols = slice(tiles[0] * LANES, (tiles[-1] + 1) * LANES)
                x = qkv_ref[b, :, cols]
                ext_ref[b, 8:8 + rows, cols] = x
                ext = ext_ref[b, :, cols]
                xb = cw[3:4, cols] * x
                for j in range(1, GDN_CONV):
                    xb = xb + cw[3 - j:4 - j, cols] * pltpu.roll(ext, j, axis=0)[8:8 + rows, :]
                ext_ref[b, 0:8, cols] = x[rows - 8:rows, :]
                xb = xb * _sigmoid(xb)
                for i, tile in enumerate(tiles):
                    g["xc"][tile] = xb[:, i * LANES:(i + 1) * LANES]
            return run

        def gates():
            ab = ab_ref[b]
            g_all = -jnp.exp(alog_ref[...]) * _softplus(ab + dtb_ref[...])
            g["beta"] = _sigmoid(ab)
            g["g_cum"] = _dot(in_chunk_prefix, g_all, HIGHEST)
            g["g_row"] = _dot_nt(e8, g["g_cum"], HIGHEST)
            for name in ("q", "k", "v", "gc", "bt", "dec"):
                g[name] = []

        def chunk(j):
            def run():
                rs = slice(j * c, (j + 1) * c)
                xc = g["xc"]
                for h in range(GDN_HEADS):
                    q = xc[h][rs]
                    k = xc[GDN_HEADS + h][rs]
                    g["q"].append(q * lax.rsqrt(jnp.sum(q * q, axis=-1, keepdims=True) + EPS) * (GDN_D ** -0.5))
                    g["k"].append(k * lax.rsqrt(jnp.sum(k * k, axis=-1, keepdims=True) + EPS))
                    g["v"].append(xc[2 * GDN_HEADS + h][rs])
                    gc = g["g_cum"][rs, h:h + 1]
                    gr = g["g_row"][h:h + 1, rs]
                    g["gc"].append(gc)
                    g["bt"].append(g["beta"][rs, GDN_HEADS + h:GDN_HEADS + h + 1])
                    g["dec"].append(jnp.exp(jnp.where(causal, gc - gr, -jnp.inf)))
            return run

        def a_mats():
            g["kb"] = [k * bt for k, bt in zip(g["k"], g["bt"])]
            g["k16"] = [k.astype(BF16) for k in g["k"]]
            g["a"] = [jnp.where(strict, _dot_nt(kb.astype(BF16), k16) * dec, 0.0)
                      for kb, k16, dec in zip(g["kb"], g["k16"], g["dec"])]

        n_tiles = C_QKV // LANES
        conv_pieces = [conv(list(range(i, i + 2))) for i in range(0, n_tiles, 2)]
        return g, conv_pieces + [gates] + [chunk(j) for j in range(chunks_per_batch)] + [a_mats]

    def inverse_rounds(g):
        def start():
            g["t"] = [eye - a for a in g["a"]]
            g["pw"] = [_split_bf16(a) for a in g["a"]]

        def square():
            g["pw"] = [_split_bf16(_dot_split(p, p)) for p in g["pw"]]

        def update():
            g["t"] = [t + _dot_split(_split_bf16(t), p) for t, p in zip(g["t"], g["pw"])]

        return [start] + [square, update] * (CHUNK.bit_length() - 2)

    groups = []
    pending = []
    for b in range(bsz):
        g, pieces = prep_pieces(b)
        groups.append(g)
        n_a, n_b = len(pending), len(pieces)
        ia = ib = 0
        while ia < n_a or ib < n_b:
            if ib >= n_b or (ia < n_a and ia * n_b <= ib * n_a):
                pending[ia]()
                ia += 1
            else:
                pieces[ib]()
                ib += 1
        pending = inverse_rounds(g)
    for fn in pending:
        fn()

    uws, qk16s, qd16s, kd16s, g_tots = [], [], [], [], []
    for g in groups:
        e_gs = [jnp.exp(gc) for gc in g["gc"]]
        uws.append([_dot(t.astype(BF16), jnp.concatenate([v * bt, kb * eg], axis=1).astype(BF16))
                    for t, v, bt, kb, eg in zip(g["t"], g["v"], g["bt"], g["kb"], e_gs)])
        qk16s.append([(_dot_nt(q.astype(BF16), k16) * dec).astype(BF16)
                      for q, k16, dec in zip(g["q"], g["k16"], g["dec"])])
        qd16s.append([(q * eg).astype(BF16) for q, eg in zip(g["q"], e_gs)])
        kd16s.append([(k * jnp.exp(gc[c - 1:c, :] - gc)).astype(BF16) for k, gc in zip(g["k"], g["gc"])])
        g_tots.append([jnp.exp(gc[c - 1:c, :]) for gc in g["gc"]])

    pairs = [(b, h) for b in range(bsz) for h in range(GDN_HEADS)]
    states = [s_ref[b * GDN_HEADS + h] for b, h in pairs]
    for j in range(chunks_per_batch):
        rs = slice(j * c, (j + 1) * c)
        pick = lambda per_group: [per_group[b][chains.index((j, h))] for b, h in pairs]
        uw_j, qk_j, qd_j, kd_j, gt_j = (pick(x) for x in (uws, qk16s, qd16s, kd16s, g_tots))
        s16s = [s.astype(BF16) for s in states]
        v16s = [(uw[:, :GDN_D] - _dot(uw[:, GDN_D:].astype(BF16), s16)).astype(BF16)
                for uw, s16 in zip(uw_j, s16s)]
        states = [s * gt + _dot_tn(kd, v16) for s, gt, kd, v16 in zip(states, gt_j, kd_j, v16s)]
        os_ = [_dot(qd, s16) + _dot(qk, v16) for qd, qk, s16, v16 in zip(qd_j, qk_j, s16s, v16s)]
        for (b, h), o in zip(pairs, os_):
            on = o * lax.rsqrt(jnp.mean(o * o, axis=-1, keepdims=True) + EPS) * nw
            zh = z_ref[b, rs, h * GDN_D:(h + 1) * GDN_D]
            o_ref[b, rs, h * GDN_D:(h + 1) * GDN_D] = on * (zh * _sigmoid(zh))
    for p, s in enumerate(states):
        s_ref[p] = s


def _gdn(qkv, z, ab, conv_w8, alog_row, dtb_row, norm_w, rows):
    bsz, seq, _ = qkv.shape
    blk = lambda i: (0, i, 0)
    const = lambda i: (0, 0)
    return pl.pallas_call(
        _gdn_kernel,
        grid=(seq // rows,),
        in_specs=[pl.BlockSpec((bsz, rows, C_QKV), blk),
                  pl.BlockSpec((bsz, rows, C_Z), blk),
                  pl.BlockSpec((bsz, rows, LANES), blk),
                  pl.BlockSpec((8, C_QKV), const),
                  pl.BlockSpec((1, LANES), const),
                  pl.BlockSpec((1, LANES), const),
                  pl.BlockSpec((1, GDN_D), const)],
        out_specs=pl.BlockSpec((bsz, rows, C_Z), blk),
        out_shape=jax.ShapeDtypeStruct((bsz, seq, C_Z), F32),
        scratch_shapes=[pltpu.VMEM((bsz, 8 + rows, C_QKV), F32),
                        pltpu.VMEM((bsz * GDN_HEADS, GDN_D, GDN_D), F32)],
        compiler_params=pltpu.CompilerParams(dimension_semantics=("arbitrary",),
                                             vmem_limit_bytes=VMEM_LIMIT),
        name="gdn",
    )(qkv, z, ab, conv_w8, alog_row, dtb_row, norm_w)


D_QK_PAD = LANES
C_HEADS = MLA_HEADS * D_QK_PAD
C_V = MLA_HEADS * MLA_V
V_AUG = MLA_V + 16
C_VAUG = MLA_HEADS * V_AUG
Q_SCALE = (MLA_NOPE + MLA_ROPE) ** -0.5 * math.log2(math.e)


def _mla_proj_kernel(cq_ref, ckv_ref, kr_ref, posr_ref, qnw_ref, kvnw_ref, wqt_ref,
                     wk_ref, wvt_ref, freqc_ref, qt_ref, k_ref, vt_ref):
    cq = cq_ref[...]
    cqn = (cq * lax.rsqrt(jnp.mean(cq * cq, axis=-1, keepdims=True) + EPS) * qnw_ref[...]).astype(BF16)
    ckv = ckv_ref[...]
    ckvn = (ckv * lax.rsqrt(jnp.mean(ckv * ckv, axis=-1, keepdims=True) + EPS)
            * kvnw_ref[...]).astype(BF16)
    tm = posr_ref.shape[-1]
    ang = freqc_ref[...] * posr_ref[...]
    c16 = jnp.cos(ang)
    s16 = jnp.sin(ang)
    pad = LANES - MLA_NOPE - MLA_ROPE
    cs_t = jnp.concatenate([jnp.ones((MLA_NOPE, tm), F32), c16, c16, jnp.ones((pad, tm), F32)], axis=0)
    sn_t = jnp.concatenate([jnp.zeros((MLA_NOPE, tm), F32), s16, s16, jnp.zeros((pad, tm), F32)], axis=0)
    cs = cs_t.T
    sn = sn_t.T
    q_t = _dot_nt(wqt_ref[...], cqn)
    kn = _dot(ckvn, wk_ref[...])
    half = MLA_ROPE // 2
    r0, r1, r2 = MLA_NOPE, MLA_NOPE + half, MLA_NOPE + MLA_ROPE
    lane = lax.broadcasted_iota(jnp.int32, (tm, LANES), 1)
    kr = jnp.where(lane >= r0, jnp.where(lane < r2, kr_ref[...], 0.0), 0.0)
    kr_rot = jnp.where(lane < r1, -pltpu.roll(kr, LANES - half, axis=1), pltpu.roll(kr, half, axis=1))
    kpe = kr * cs + kr_rot * sn
    for h in range(MLA_HEADS):
        sl = slice(h * LANES, (h + 1) * LANES)
        q = q_t[sl, :]
        rot = jnp.concatenate([jnp.zeros((r0, tm), F32), -q[r1:r2], q[r0:r1], jnp.zeros((pad, tm), F32)], axis=0)
        qt_ref[sl, :] = ((q * cs_t + rot * sn_t) * Q_SCALE).astype(BF16)
        k_ref[:, sl] = (kn[:, sl] + kpe).astype(BF16)
    v_t = _dot_nt(wvt_ref[...], ckvn).astype(BF16)
    ones = jnp.ones((V_AUG - MLA_V, v_t.shape[1]), BF16)
    for h in range(MLA_HEADS):
        vt_ref[0, h * V_AUG:h * V_AUG + MLA_V, :] = v_t[h * MLA_V:(h + 1) * MLA_V, :]
        vt_ref[0, h * V_AUG + MLA_V:(h + 1) * V_AUG, :] = ones


def _mla_proj(cq, ckv, abk, posr, qnw, kvnw, wqt, wk, wvt, freqc, tm):
    t = cq.shape[0]
    row = lambda i: (i, 0)
    col = lambda i: (0, i)
    const = lambda i: (0, 0)
    return pl.pallas_call(
        _mla_proj_kernel,
        grid=(t // tm,),
        in_specs=[pl.BlockSpec((tm, MLA_Q_LORA), row),
                  pl.BlockSpec((tm, MLA_KV_LORA), row),
                  pl.BlockSpec((tm, LANES), row),
                  pl.BlockSpec((1, tm), col),
                  pl.BlockSpec((1, MLA_Q_LORA), const),
                  pl.BlockSpec((1, MLA_KV_LORA), const),
                  pl.BlockSpec((C_HEADS, MLA_Q_LORA), const),
                  pl.BlockSpec((MLA_KV_LORA, C_HEADS), const),
                  pl.BlockSpec((C_V, MLA_KV_LORA), const),
                  pl.BlockSpec((MLA_ROPE // 2, 1), const)],
        out_specs=[pl.BlockSpec((C_HEADS, tm), col),
                   pl.BlockSpec((tm, C_HEADS), row),
                   pl.BlockSpec((1, C_VAUG, tm), lambda i: (i, 0, 0))],
        out_shape=[jax.ShapeDtypeStruct((C_HEADS, t), BF16),
                   jax.ShapeDtypeStruct((t, C_HEADS), BF16),
                   jax.ShapeDtypeStruct((t // tm, C_VAUG, tm), BF16)],
        compiler_params=pltpu.CompilerParams(dimension_semantics=("parallel",),
                                             vmem_limit_bytes=VMEM_LIMIT),
        name="mla_proj",
    )(cq, ckv, abk, posr, qnw, kvnw, wqt, wk, wvt, freqc)


def _attn_kernel(qt_ref, k_ref, vt_ref, nw_ref, ot_ref, m_ref, acc_ref, *, tq):
    i = pl.program_id(1)
    m_ref[...] = jnp.full(m_ref.shape, NEG_BIG, F32)
    acc_ref[...] = jnp.zeros(acc_ref.shape, F32)
    def tile(j, k_lo, n_keys, q_lo, masked):
        rows = pl.ds(pl.multiple_of(j * tq + k_lo, n_keys), n_keys)
        qs = slice(q_lo, tq)
        if masked:
            key_chunk = (k_lo + lax.broadcasted_iota(jnp.int32, (n_keys, tq - q_lo), 0)) // CHUNK
            qry_chunk = (q_lo + lax.broadcasted_iota(jnp.int32, (n_keys, tq - q_lo), 1)) // CHUNK
            mask = key_chunk <= qry_chunk
        scores = []
        for h in range(MLA_HEADS):
            hs = slice(h * LANES, (h + 1) * LANES)
            s_t = _dot(k_ref[rows, hs], qt_ref[hs, qs])
            scores.append(jnp.where(mask, s_t, NEG_BIG) if masked else s_t)
        probs, alphas = [], []
        for h in range(MLA_HEADS):
            m_old = m_ref[h, :, qs]
            m_new = jnp.maximum(m_old, jnp.max(scores[h], axis=0, keepdims=True))
            m_ref[h, :, qs] = m_new
            probs.append(jnp.exp2(scores[h] - m_new[0:1, :]).astype(BF16))
            alphas.append(jnp.exp2(m_old[0:1, :] - m_new[0:1, :]))
        for h in range(MLA_HEADS):
            v_aug = vt_ref[j, h * V_AUG:(h + 1) * V_AUG, k_lo:k_lo + n_keys]
            acc_ref[h, :, qs] = alphas[h] * acc_ref[h, :, qs] + _dot(v_aug, probs[h])

    def full_tile(j, carry):
        tile(j, 0, tq, 0, False)
        return carry

    lax.fori_loop(0, i, full_tile, 0)
    half = tq // 2
    tile(i, 0, half, 0, True)
    tile(i, half, half, half, True)

    outs = []
    for h in range(MLA_HEADS):
        acc = acc_ref[h]
        outs.append(acc[:MLA_V, :] / acc[MLA_V:MLA_V + 1, :])
    ssq = outs[0] * outs[0]
    for o in outs[1:]:
        ssq = ssq + o * o
    inv = lax.rsqrt(jnp.sum(ssq, axis=0, keepdims=True) / C_V + EPS)
    for h in range(MLA_HEADS):
        vs = slice(h * MLA_V, (h + 1) * MLA_V)
        ot_ref[vs, :] = outs[h] * inv * nw_ref[vs, :]


def _attention(qt, k, vt, norm_w_col, bsz, seq, tq):
    nq = seq // tq
    return pl.pallas_call(
        functools.partial(_attn_kernel, tq=tq),
        grid=(bsz, nq),
        in_specs=[pl.BlockSpec((C_HEADS, tq), lambda b, i: (0, b * nq + i)),
                  pl.BlockSpec((seq, C_HEADS), lambda b, i: (b, 0)),
                  pl.BlockSpec((nq, C_VAUG, tq), lambda b, i: (b, 0, 0)),
                  pl.BlockSpec((C_V, 1), lambda b, i: (0, 0))],
        out_specs=pl.BlockSpec((C_V, tq), lambda b, i: (0, b * nq + i)),
        out_shape=jax.ShapeDtypeStruct((C_V, bsz * seq), F32),
        scratch_shapes=[pltpu.VMEM((MLA_HEADS, 8, tq), F32),
                        pltpu.VMEM((MLA_HEADS, V_AUG, tq), F32)],
        compiler_params=pltpu.CompilerParams(dimension_semantics=("parallel", "parallel"),
                                             vmem_limit_bytes=VMEM_LIMIT),
        name="attention",
    )(qt, k, vt, norm_w_col)


def _router_kernel(x_ref, oa_ref, obt_ref, wo_ref, nw_ref, rw_ref, rb_ref,
                   x1_ref, h_ref, eidx_ref, gate_ref, rank_ref, cnt_ref, run_ref, *, tm):
    @pl.when(pl.program_id(0) == 0)
    def _():
        run_ref[...] = jnp.zeros_like(run_ref)

    half = oa_ref.shape[-1]
    y = (_dot(oa_ref[...].astype(BF16), wo_ref[0:half, :])
         + _dot_tn(obt_ref[...].astype(BF16), wo_ref[half:, :]))
    x1 = x_ref[...] + y
    x1_ref[...] = x1
    hn = x1 * lax.rsqrt(jnp.mean(x1 * x1, axis=-1, keepdims=True) + EPS) * nw_ref[...]
    d_half = hn.shape[-1] // 2
    _store_packed_rows(h_ref, _pack_bf16_pair(hn[:, :d_half], hn[:, d_half:]))
    lg = _dot_split_nt(_split_bf16(rw_ref[...]), _split_bf16(hn)) + rb_ref[...]
    expert = lax.broadcasted_iota(jnp.int32, lg.shape, 0)
    sels, tops, idxs = [], [], []
    for _ in range(TOP_K):
        m = jnp.max(lg, axis=0, keepdims=True)
        idx = jnp.min(jnp.where(lg == m, expert, N_EXPERTS), axis=0, keepdims=True)
        sel = expert == idx
        lg = jnp.where(sel, -jnp.inf, lg)
        sels.append(sel)
        tops.append(m)
        idxs.append(idx)
    exps = [jnp.exp(tv - tops[0]) for tv in tops]
    den = functools.reduce(lambda a, b: a + b, exps)
    multi = sels[0].astype(F32)
    for sel in sels[1:]:
        multi = multi + sel.astype(F32)
    ri = lax.broadcasted_iota(jnp.int32, (tm, tm), 0)
    ci = lax.broadcasted_iota(jnp.int32, (tm, tm), 1)
    before = _dot(multi.astype(BF16), (ri < ci).astype(BF16)) + run_ref[...]
    ranks = [jnp.sum(jnp.where(sel, before, 0.0), axis=0, keepdims=True).astype(jnp.int32) for sel in sels]
    run = run_ref[...] + jnp.sum(multi, axis=1, keepdims=True)
    run_ref[...] = run
    cnt_ref[...] = jnp.broadcast_to(run, cnt_ref.shape)
    pad_i = jnp.zeros((8 - TOP_K, tm), jnp.int32)
    eidx_ref[...] = jnp.concatenate(idxs + [pad_i], axis=0)
    rank_ref[...] = jnp.concatenate(ranks + [pad_i], axis=0)
    gates_t = jnp.concatenate([e / den for e in exps] + [jnp.zeros((LANES - TOP_K, tm), F32)], axis=0)
    gate_ref[...] = gates_t.T


def _router(x2, oa, obt, w_out, norm_w, rw, rb, tm):
    t, d = x2.shape
    half = oa.shape[-1]
    s_words = d // 2 // LANES
    row = lambda i: (i, 0)
    col = lambda i: (0, i)
    const = lambda i: (0, 0)
    return pl.pallas_call(
        functools.partial(_router_kernel, tm=tm),
        grid=(t // tm,),
        in_specs=[pl.BlockSpec((tm, d), row),
                  pl.BlockSpec((tm, half), row),
                  pl.BlockSpec((half, tm), lambda i: (0, i)),
                  pl.BlockSpec((2 * half, d), const),
                  pl.BlockSpec((1, d), const),
                  pl.BlockSpec((N_EXPERTS, d), const),
                  pl.BlockSpec((N_EXPERTS, 1), const)],
        out_specs=[pl.BlockSpec((tm, d), row), pl.BlockSpec((tm * s_words, LANES), row),
                   pl.BlockSpec((8, tm), col), pl.BlockSpec((tm, LANES), row),
                   pl.BlockSpec((8, tm), col), pl.BlockSpec((N_EXPERTS, LANES), const)],
        out_shape=[jax.ShapeDtypeStruct((t, d), F32), jax.ShapeDtypeStruct((t * s_words, LANES), jnp.int32),
                   jax.ShapeDtypeStruct((8, t), jnp.int32), jax.ShapeDtypeStruct((t, LANES), F32),
                   jax.ShapeDtypeStruct((8, t), jnp.int32), jax.ShapeDtypeStruct((N_EXPERTS, LANES), F32)],
        scratch_shapes=[pltpu.VMEM((N_EXPERTS, 1), F32)],
        compiler_params=pltpu.CompilerParams(dimension_semantics=("arbitrary",),
                                             vmem_limit_bytes=VMEM_LIMIT),
        name="router",
    )(x2, oa, obt, w_out, norm_w, rw, rb)


SC_ROWS = 64


def _sc_mesh():
    return plsc.VectorSubcoreMesh(core_axis_name="c", subcore_axis_name="s")


def _dispatch(dest_chunks, h3, n_pad):
    t, s, _ = h3.shape
    mesh = _sc_mesh()
    n_workers = mesh.num_cores * mesh.num_subcores
    per_worker = t // n_workers
    n_chunks = per_worker // SC_ROWS
    assert per_worker % SC_ROWS == 0 and n_chunks % 2 == 0

    @functools.partial(
        pl.kernel, mesh=mesh, out_type=jax.ShapeDtypeStruct((n_pad, s, LANES), h3.dtype),
        scratch_types=[pltpu.VMEM((2, TOP_K * SC_ROWS), jnp.int32), pltpu.VMEM((2, SC_ROWS, s, LANES), h3.dtype),
                       pltpu.SemaphoreType.DMA((2,)), pltpu.SemaphoreType.DMA((2,)), pltpu.SemaphoreType.DMA],
        name="sc_dispatch")
    def scatter(h_hbm, dest_hbm, out_hbm, idx_v, rows_v, rsem, isem, ssem):
        base = (lax.axis_index("s") * mesh.num_cores + lax.axis_index("c")) * per_worker

        def loads(chunk, slot):
            off = pl.multiple_of(base + chunk * SC_ROWS, SC_ROWS)
            return (pltpu.make_async_copy(h_hbm.at[pl.ds(off, SC_ROWS)], rows_v.at[slot], rsem.at[slot]),
                    pltpu.make_async_copy(dest_hbm.at[pl.ds(off * TOP_K, SC_ROWS * TOP_K)], idx_v.at[slot],
                                          isem.at[slot]))

        for cp in loads(0, 0):
            cp.start()

        @pl.loop(0, n_chunks // 2)
        def _(pair):
            for slot in (0, 1):
                chunk = 2 * pair + slot
                for cp in loads(chunk, slot):
                    cp.wait()

                @pl.when(chunk + 1 < n_chunks)
                def _():
                    for cp in loads(chunk + 1, 1 - slot):
                        cp.start()

                copies = [pltpu.async_copy(rows_v.at[slot],
                                           out_hbm.at[idx_v.at[slot, pl.ds(kk * SC_ROWS, SC_ROWS)]], ssem)
                          for kk in range(TOP_K)]
                for cp in copies:
                    cp.wait()

    return scatter(h3, dest_chunks)


W_SLOTS = 3

def _expert_kernel(cnt_ref, blk0_ref, nblk_ref, total_ref, x_hbm, w1_hbm, b1g_ref, b1l_ref, w2_hbm, b2_ref,
                   y_hbm, w1f_ref, w2f_ref, w1s_ref, w2s_ref, xbuf, ybuf, xsem, ysem, wsem, *, bm):
    e = pl.program_id(0)
    n_experts = pl.num_programs(0)

    def w_copies(expert, slot):
        return (pltpu.make_async_copy(w1_hbm.at[expert], w1f_ref.at[slot], wsem.at[0, slot]),
                pltpu.make_async_copy(w2_hbm.at[expert], w2f_ref.at[slot], wsem.at[1, slot]))

    @pl.when(e == 0)
    def _():
        for ahead in range(W_SLOTS - 1):
            for cp in w_copies(ahead, ahead):
                cp.start()

    @pl.when(e + W_SLOTS - 1 < n_experts)
    def _():
        for cp in w_copies(e + W_SLOTS - 1, lax.rem(e + W_SLOTS - 1, W_SLOTS)):
            cp.start()

    w_slot = lax.rem(e, W_SLOTS)
    for cp in w_copies(e, w_slot):
        cp.wait()
    n_blocks = nblk_ref[e]
    first = blk0_ref[e]
    total = total_ref[0]
    rows_per_block = xbuf.shape[1]
    s_words = rows_per_block // bm
    n_tiles = w1s_ref.shape[-1] // MXU_COLS
    half = MXU_COLS // 2

    def x_copy(g, slot):
        return pltpu.make_async_copy(x_hbm.at[pl.ds(pl.multiple_of(g * rows_per_block, rows_per_block),
                                                    rows_per_block)], xbuf.at[slot], xsem.at[slot])

    def y_copy(g, slot):
        return pltpu.make_async_copy(ybuf.at[slot], y_hbm.at[pl.ds(pl.multiple_of(g * rows_per_block,
                                                                                   rows_per_block),
                                                                   rows_per_block)], ysem.at[slot])

    @pl.when(e == 0)
    def _():
        x_copy(0, 0).start()

    @pl.when(n_blocks > 0)
    def _():
        r = lax.broadcasted_iota(jnp.int32, (MXU_COLS, MXU_COLS), 0)
        c = lax.broadcasted_iota(jnp.int32, (MXU_COLS, MXU_COLS), 1)
        perm = (r == jnp.where(c < half, 2 * c, 2 * (c - half) + 1)).astype(BF16)
        for tix in range(n_tiles):
            cols = slice(tix * MXU_COLS, (tix + 1) * MXU_COLS)
            w1s_ref[:, cols] = _dot(w1f_ref[w_slot, :, cols].astype(BF16), perm).astype(BF16)
        w2s_ref[...] = w2f_ref[w_slot].astype(BF16)

        def block(j, carry):
            g = first + j
            slot = lax.rem(g, 2)
            x_copy(g, slot).wait()

            @pl.when(g + 1 < total)
            def _():
                x_copy(g + 1, 1 - slot).start()

            @pl.when(g >= 2)
            def _():
                y_copy(g - 2, slot).wait()

            n_rows = cnt_ref[e] - j * bm

            def ffn(m):
                words = _load_packed_rows(xbuf, (slot,), m, s_words)
                row = lax.broadcasted_iota(jnp.int32, words.shape, 0)
                x_lo, x_hi = _unpack_bf16_pair(jnp.where(row < n_rows, words, 0))
                x = jnp.concatenate([x_lo.astype(BF16), x_hi.astype(BF16)], axis=1)
                acts = []
                for tix in range(n_tiles):
                    hp = _dot(x, w1s_ref[:, tix * MXU_COLS:(tix + 1) * MXU_COLS])
                    feat = slice(tix * half, (tix + 1) * half)
                    gl = jnp.minimum(hp[:, :half] + b1g_ref[0][:, feat], SWIGLU_LIMIT)
                    lin = jnp.clip(hp[:, half:] + b1l_ref[0][:, feat], -SWIGLU_LIMIT, SWIGLU_LIMIT)
                    acts.append((gl * _sigmoid(SWIGLU_ALPHA * gl) * (lin + 1.0)).astype(BF16))
                y = _dot(jnp.concatenate(acts, axis=1), w2s_ref[...]) + b2_ref[0]
                d_half = y.shape[-1] // 2
                _store_packed_rows(ybuf.at[slot], _pack_bf16_pair(y[:, :d_half], y[:, d_half:]))

            @pl.when(n_rows > bm // 2)
            def _():
                ffn(bm)

            @pl.when(n_rows <= bm // 2)
            def _():
                ffn(bm // 2)

            y_copy(g, slot).start()
            return carry

        lax.fori_loop(0, n_blocks, block, 0)

    @pl.when(e == pl.num_programs(0) - 1)
    def _():
        @pl.when(total >= 2)
        def _():
            y_copy(total - 2, lax.rem(total, 2)).wait()

        y_copy(total - 1, lax.rem(total - 1, 2)).wait()


def _experts(tables, xs, w1, b1g, b1l, w2, b2, bm, d):
    s_words = d // 2 // LANES
    dff2 = w1.shape[-1]
    dff = dff2 // 2
    emap = lambda e, *_: (e, 0, 0)
    grid_spec = pltpu.PrefetchScalarGridSpec(
        num_scalar_prefetch=len(tables),
        grid=(N_EXPERTS,),
        in_specs=[pl.BlockSpec(memory_space=pl.ANY),
                  pl.BlockSpec(memory_space=pl.ANY),
                  pl.BlockSpec((1, 1, dff), emap),
                  pl.BlockSpec((1, 1, dff), emap),
                  pl.BlockSpec(memory_space=pl.ANY),
                  pl.BlockSpec((1, 1, d), emap)],
        out_specs=pl.BlockSpec(memory_space=pl.ANY),
        scratch_shapes=[pltpu.VMEM((W_SLOTS, d, dff2), F32), pltpu.VMEM((W_SLOTS, dff, d), F32),
                        pltpu.VMEM((d, dff2), BF16), pltpu.VMEM((dff, d), BF16),
                        pltpu.VMEM((2, bm * s_words, LANES), jnp.int32),
                        pltpu.VMEM((2, bm * s_words, LANES), jnp.int32),
                        pltpu.SemaphoreType.DMA((2,)), pltpu.SemaphoreType.DMA((2,)),
                        pltpu.SemaphoreType.DMA((2, W_SLOTS))],
    )
    return pl.pallas_call(
        functools.partial(_expert_kernel, bm=bm),
        grid_spec=grid_spec,
        out_shape=jax.ShapeDtypeStruct(xs.shape, jnp.int32),
        compiler_params=pltpu.CompilerParams(dimension_semantics=("arbitrary",),
                                             vmem_limit_bytes=VMEM_LIMIT),
        name="experts",
    )(*tables, xs, w1, b1g, b1l, w2, b2)


def _gather_rows(dest_flat, y3):
    n_out = dest_flat.shape[0]
    _, s, _ = y3.shape
    mesh = _sc_mesh()
    n_workers = mesh.num_cores * mesh.num_subcores
    per_worker = n_out // n_workers
    n_chunks = per_worker // SC_ROWS
    assert per_worker % SC_ROWS == 0 and n_chunks % 2 == 0

    @functools.partial(
        pl.kernel, mesh=mesh, out_type=jax.ShapeDtypeStruct((n_out, s, LANES), y3.dtype),
        scratch_types=[pltpu.VMEM((2, SC_ROWS), jnp.int32), pltpu.VMEM((2, SC_ROWS, s, LANES), y3.dtype),
                       pltpu.SemaphoreType.DMA((2,)), pltpu.SemaphoreType.DMA((2,))],
        name="sc_gather")
    def gather(y_hbm, dest_hbm, out_hbm, idx_v, rows_v, gsem, wsem):
        base = (lax.axis_index("s") * mesh.num_cores + lax.axis_index("c")) * per_worker

        def rows_of(chunk):
            return pl.ds(pl.multiple_of(base + chunk * SC_ROWS, SC_ROWS), SC_ROWS)

        def gather_copy(slot):
            return pltpu.make_async_copy(y_hbm.at[idx_v.at[slot]], rows_v.at[slot], gsem.at[slot])

        def write_copy(chunk, slot):
            return pltpu.make_async_copy(rows_v.at[slot], out_hbm.at[rows_of(chunk)], wsem.at[slot])

        def start_gather(chunk, slot):
            pltpu.sync_copy(dest_hbm.at[rows_of(chunk)], idx_v.at[slot])
            gather_copy(slot).start()

        start_gather(0, 0)

        @pl.loop(0, n_chunks // 2)
        def _(pair):
            for slot in (0, 1):
                chunk = 2 * pair + slot
                gather_copy(slot).wait()
                write_copy(chunk, slot).start()

                @pl.when(chunk >= 1)
                def _():
                    write_copy(chunk - 1, 1 - slot).wait()

                @pl.when(chunk + 1 < n_chunks)
                def _():
                    start_gather(chunk + 1, 1 - slot)

        write_copy(n_chunks - 1, 1).wait()

    return gather(y3, dest_flat)


def _combine_kernel(x1_ref, gate_ref, y4_ref, nw_ref, o_ref, *, tm, normalize):
    x = x1_ref[...]
    gates = gate_ref[...]
    for kk in range(TOP_K):
        lo, hi = _unpack_bf16_pair(_load_packed_rows(y4_ref, (kk,), tm, y4_ref.shape[1] // tm))
        x = x + gates[:, kk:kk + 1] * jnp.concatenate([lo, hi], axis=1)
    if normalize:
        x = x * lax.rsqrt(jnp.mean(x * x, axis=-1, keepdims=True) + EPS) * nw_ref[...]
    o_ref[...] = x


def _combine(x1, gates, y4, norm_w, tm, normalize):
    t, d = x1.shape
    s_words = d // 2 // LANES
    row = lambda i: (i, 0)
    return pl.pallas_call(
        functools.partial(_combine_kernel, tm=tm, normalize=normalize),
        grid=(t // tm,),
        in_specs=[pl.BlockSpec((tm, d), row), pl.BlockSpec((tm, LANES), row),
                  pl.BlockSpec((TOP_K, tm * s_words, LANES), lambda i: (0, i, 0)),
                  pl.BlockSpec((1, d), lambda i: (0, 0))],
        out_specs=pl.BlockSpec((tm, d), row),
        out_shape=jax.ShapeDtypeStruct((t, d), F32),
        compiler_params=pltpu.CompilerParams(dimension_semantics=("parallel",)),
        name="combine",
    )(x1, gates, y4, norm_w)


def _tiles(t, seq):
    tm = min(2 * MXU_COLS, t)
    tr = min(2 * MXU_COLS, t)
    tq = min(2 * MXU_COLS, seq)
    gdn_rows = min(4 * CHUNK, seq)
    bm = 2 * MXU_COLS
    assert t % tm == 0 and t % tr == 0 and seq % tq == 0 and seq % gdn_rows == 0 and tq % CHUNK == 0
    return tm, tr, tq, gdn_rows, bm


def _prep_mla_weights(w_uq, w_ukv):
    dq = MLA_NOPE + MLA_ROPE
    wq = w_uq.reshape(MLA_Q_LORA, MLA_HEADS, dq)
    pad = lambda w: jnp.pad(w, ((0, 0), (0, 0), (0, LANES - w.shape[-1]))).reshape(w.shape[0], C_HEADS)
    wkv = w_ukv.reshape(MLA_KV_LORA, MLA_HEADS, MLA_NOPE + MLA_V)
    wv = wkv[..., MLA_NOPE:].reshape(MLA_KV_LORA, C_V)
    return pad(wq).T.astype(BF16), pad(wkv[..., :MLA_NOPE]).astype(BF16), wv.T.astype(BF16)


def _rope_freq():
    half = MLA_ROPE // 2
    return (ROPE_THETA ** (-jnp.arange(half, dtype=F32) / half)).reshape(half, 1)


def _lane_row(v):
    return jnp.pad(v.astype(F32), (0, LANES - v.shape[0])).reshape(1, LANES)


def _routing_tables(cnt, eidx, rank, bm):
    e_ids = jnp.arange(N_EXPERTS, dtype=jnp.int32)
    counts = cnt[:, 0].astype(jnp.int32)
    padded = (counts + bm - 1) // bm * bm
    pend = jnp.sum(jnp.where(e_ids[None, :] <= e_ids[:, None], padded[None, :], 0), axis=1)
    pstart = pend - padded
    dest = jnp.sum(jnp.where(eidx[:TOP_K, :, None] == e_ids, pstart, 0), axis=-1) + rank[:TOP_K]
    tables = (counts, pstart // bm, padded // bm, pend[-1:] // bm)
    return dest.reshape(-1), tuple(t.astype(jnp.int32) for t in tables)


def kernel(x, positions, norm_mix_w, w_in, conv_w, a_log, dt_bias, gdn_norm_w, q_norm_w, w_uq, kv_norm_w,
           w_ukv, mla_out_norm_w, w_out, norm_ffn_w, router_w, router_b, w1, b1, w2, b2, norm_final_w):
    bsz, seq, d = x.shape
    t = bsz * seq
    depth = w_in.shape[0]
    tm, tr, tq, gdn_rows, bm = _tiles(t, seq)
    x2 = x.reshape(t, d)
    posf = positions.astype(F32)
    freq = _rope_freq()
    for l in range(depth):
        qkv, z, cq, ckv, abk = _inproj(x2, norm_mix_w[l].reshape(1, d), w_in[l], tm)
        conv_w8 = jnp.pad(conv_w[l], ((0, 8 - GDN_CONV), (0, 0)))
        o_a = _gdn(qkv.reshape(bsz, seq, C_QKV), z.reshape(bsz, seq, C_Z), abk.reshape(bsz, seq, LANES),
                   conv_w8, _lane_row(a_log[l]), _lane_row(dt_bias[l]), gdn_norm_w[l].reshape(1, GDN_D),
                   gdn_rows).reshape(t, C_Z)
        wqt, wk, wvt = _prep_mla_weights(w_uq[l], w_ukv[l])
        qt, k, vt = _mla_proj(cq, ckv, abk, posf.reshape(1, t),
                              q_norm_w[l].reshape(1, -1), kv_norm_w[l].reshape(1, -1),
                              wqt, wk, wvt, freq, tq)
        o_bt = _attention(qt, k, vt, mla_out_norm_w[l].reshape(-1, 1), bsz, seq, tq)
        x1, hn, eidx, gates, rank, cnt = _router(x2, o_a, o_bt, w_out[l].astype(BF16),
                                                 norm_ffn_w[l].reshape(1, d), router_w[l].T,
                                                 router_b[l].reshape(N_EXPERTS, 1), tr)
        n_pad = t * TOP_K + N_EXPERTS * bm
        dest_flat, tables = _routing_tables(cnt, eidx, rank, bm)
        s_words = d // 2 // LANES
        dest_chunks = dest_flat.reshape(TOP_K, t // SC_ROWS, SC_ROWS).transpose(1, 0, 2).reshape(-1)
        buf = _dispatch(dest_chunks, hn.reshape(t, s_words, LANES), n_pad)
        yb = _experts(tables, buf.reshape(n_pad * s_words, LANES), w1[l],
                      b1[l][:, None, 0::2], b1[l][:, None, 1::2], w2[l], b2[l][:, None, :], bm, d)
        y4 = _gather_rows(dest_flat, yb.reshape(n_pad, s_words, LANES))
        x2 = _combine(x1, gates, y4.reshape(TOP_K, t * s_words, LANES), norm_final_w.reshape(1, d), tr,
                      normalize=(l == depth - 1))
    return x2.reshape(bsz, seq, d)
```

```python
import functools
import math

import jax
import jax.numpy as jnp
from jax import lax
from jax.experimental import pallas as pl
from jax.experimental.pallas import tpu as pltpu
from jax.experimental.pallas import tpu_sc as plsc

F32 = jnp.float32
BF16 = jnp.bfloat16
HIGHEST = lax.Precision.HIGHEST

LANES = 128
MXU_COLS = 256
EPS = 1e-6
CHUNK = 64

GDN_HEADS = 4
GDN_D = 128
GDN_CONV = 4
MLA_HEADS = 8
MLA_NOPE = 64
MLA_ROPE = 32
MLA_V = 64
MLA_Q_LORA = 384
MLA_KV_LORA = 256
ROPE_THETA = 10000.0
N_EXPERTS = 32
TOP_K = 4
SWIGLU_LIMIT = 7.0
SWIGLU_ALPHA = 1.702

NEG_BIG = -1e30
VMEM_LIMIT = 52 * 1024 * 1024


def _sigmoid(x):
    return 1.0 / (1.0 + jnp.exp(-x))


def _softplus(x):
    return jnp.maximum(x, 0.0) + jnp.log(1.0 + jnp.exp(-jnp.abs(x)))


def _dot(a, b, precision=None):
    return jnp.dot(a, b, preferred_element_type=F32, precision=precision)


def _dot_nt(a, b, precision=None):
    return lax.dot_general(a, b, (((1,), (1,)), ((), ())), preferred_element_type=F32,
                           precision=precision)


def _dot_tn(a, b):
    return lax.dot_general(a, b, (((0,), (0,)), ((), ())), preferred_element_type=F32)


def _split_bf16(a):
    hi = a.astype(BF16)
    return hi, (a - hi.astype(F32)).astype(BF16)


def _pack_bf16_pair(lo, hi):
    lo_bits = pltpu.bitcast(lo.astype(BF16).astype(F32), jnp.int32)
    hi_bits = pltpu.bitcast(hi.astype(BF16).astype(F32), jnp.int32)
    return jnp.bitwise_or(hi_bits, lax.shift_right_logical(lo_bits, 16))


def _unpack_bf16_pair(w):
    lo = pltpu.bitcast(lax.shift_left(w, 16), F32)
    hi = pltpu.bitcast(jnp.bitwise_and(w, -65536), F32)
    return lo, hi


def _load_packed_rows(ref, lead, m, s):
    return jnp.concatenate([ref[(*lead, pl.ds(j, m, stride=s), slice(None))] for j in range(s)], axis=1)


def _store_packed_rows(ref, words):
    m = words.shape[0]
    s = words.shape[1] // LANES
    for j in range(s):
        ref[pl.ds(j, m, stride=s), :] = words[:, j * LANES:(j + 1) * LANES]


def _dot_split_nt(a, b):
    return _dot_nt(a[0], b[0]) + _dot_nt(a[0], b[1]) + _dot_nt(a[1], b[0])


def _dot_split(a, b):
    return _dot(a[0], b[0]) + _dot(a[0], b[1]) + _dot(a[1], b[0])


C_QKV = 3 * GDN_HEADS * GDN_D
C_Z = GDN_HEADS * GDN_D
C_IN = C_QKV + C_Z + MLA_Q_LORA + MLA_KV_LORA + LANES


def _inproj_kernel(x_ref, nw_ref, w_ref, qkv_ref, z_ref, cq_ref, ckv_ref, abk_ref, wb_ref):
    @pl.when(pl.program_id(0) == 0)
    def _():
        d = w_ref.shape[0]
        hd4 = C_QKV + C_Z
        wb_ref[:, 0:hd4] = w_ref[:, 0:hd4].astype(BF16)
        tail = w_ref[:, hd4:]
        o_cq = 2 * GDN_HEADS
        o_ckv = o_cq + MLA_Q_LORA
        o_kr = o_ckv + MLA_KV_LORA
        zeros = lambda n: jnp.zeros((d, n), F32)
        groups = [tail[:, o_cq:o_ckv], tail[:, o_ckv:o_kr],
                  jnp.concatenate([tail[:, 0:o_cq], zeros(MLA_NOPE - o_cq), tail[:, o_kr:o_kr + MLA_ROPE],
                                   zeros(LANES - MLA_NOPE - MLA_ROPE)], axis=1)]
        o = hd4
        for grp in groups:
            wb_ref[:, o:o + grp.shape[1]] = grp.astype(BF16)
            o += grp.shape[1]

    x = x_ref[...]
    var = jnp.mean(x * x, axis=-1, keepdims=True)
    h = (x * lax.rsqrt(var + EPS) * nw_ref[...]).astype(BF16)
    p = _dot(h, wb_ref[...])
    o = 0
    for ref in (qkv_ref, z_ref, cq_ref, ckv_ref, abk_ref):
        n = ref.shape[-1]
        ref[...] = p[:, o:o + n]
        o += n


def _inproj(x2, norm_w, w_in, tm):
    t, d = x2.shape
    widths = (C_QKV, C_Z, MLA_Q_LORA, MLA_KV_LORA, LANES)
    return pl.pallas_call(
        _inproj_kernel,
        grid=(t // tm,),
        in_specs=[pl.BlockSpec((tm, d), lambda i: (i, 0)),
                  pl.BlockSpec((1, d), lambda i: (0, 0)),
                  pl.BlockSpec(w_in.shape, lambda i: (0, 0), pipeline_mode=pl.Buffered(1))],
        out_specs=[pl.BlockSpec((tm, n), lambda i: (i, 0)) for n in widths],
        out_shape=[jax.ShapeDtypeStruct((t, n), F32) for n in widths],
        scratch_shapes=[pltpu.VMEM((d, C_IN), BF16)],
        compiler_params=pltpu.CompilerParams(dimension_semantics=("arbitrary",),
                                             vmem_limit_bytes=VMEM_LIMIT),
        name="inproj",
    )(x2, norm_w, w_in)


def _gdn_kernel(qkv_ref, z_ref, ab_ref, cw_ref, alog_ref, dtb_ref, nw_ref, o_ref, ext_ref, s_ref):
    c = CHUNK
    bsz, rows, _ = qkv_ref.shape
    chunks_per_batch = rows // c

    @pl.when(pl.program_id(0) == 0)
    def _():
        ext_ref[:, 0:8, :] = jnp.zeros((bsz, 8, C_QKV), F32)
        s_ref[...] = jnp.zeros_like(s_ref)

    cw = cw_ref[...]
    nw = nw_ref[...]
    rr = lax.broadcasted_iota(jnp.int32, (rows, rows), 0)
    rc = lax.broadcasted_iota(jnp.int32, (rows, rows), 1)
    in_chunk_prefix = jnp.where(rr // c == rc // c, jnp.where(rr >= rc, 1.0, 0.0), 0.0)
    e8 = (lax.broadcasted_iota(jnp.int32, (8, LANES), 0)
          == lax.broadcasted_iota(jnp.int32, (8, LANES), 1)).astype(F32)
    ri = lax.broadcasted_iota(jnp.int32, (c, c), 0)
    ci = lax.broadcasted_iota(jnp.int32, (c, c), 1)
    causal = ri >= ci
    strict = ri > ci
    eye = (ri == ci).astype(F32)
    chains = [(j, h) for j in range(chunks_per_batch) for h in range(GDN_HEADS)]

    def prep_pieces(b):
        g = {"xc": [None] * (C_QKV // LANES)}

        def conv(tiles):
            def run():
                cols = slice(tiles[0] * LANES, (tiles[-1] + 1) * LANES)
                x = qkv_ref[b, :, cols]
                ext_ref[b, 8:8 + rows, cols] = x
                ext = ext_ref[b, :, cols]
                xb = cw[3:4, cols] * x
                for j in range(1, GDN_CONV):
                    xb = xb + cw[3 - j:4 - j, cols] * pltpu.roll(ext, j, axis=0)[8:8 + rows, :]
                ext_ref[b, 0:8, cols] = x[rows - 8:rows, :]
                xb = xb * _sigmoid(xb)
                for i, tile in enumerate(tiles):
                    g["xc"][tile] = xb[:, i * LANES:(i + 1) * LANES]
            return run

        def gates():
            ab = ab_ref[b]
            g_all = -jnp.exp(alog_ref[...]) * _softplus(ab + dtb_ref[...])
            g["beta"] = _sigmoid(ab)
            g["g_cum"] = _dot(in_chunk_prefix, g_all, HIGHEST)
            g["g_row"] = _dot_nt(e8, g["g_cum"], HIGHEST)
            for name in ("q", "k", "v", "gc", "bt", "dec"):
                g[name] = []

        def chunk(j):
            def run():
                rs = slice(j * c, (j + 1) * c)
                xc = g["xc"]
                for h in range(GDN_HEADS):
                    q = xc[h][rs]
                    k = xc[GDN_HEADS + h][rs]
                    g["q"].append(q * lax.rsqrt(jnp.sum(q * q, axis=-1, keepdims=True) + EPS) * (GDN_D ** -0.5))
                    g["k"].append(k * lax.rsqrt(jnp.sum(k * k, axis=-1, keepdims=True) + EPS))
                    g["v"].append(xc[2 * GDN_HEADS + h][rs])
                    gc = g["g_cum"][rs, h:h + 1]
                    gr = g["g_row"][h:h + 1, rs]
                    g["gc"].append(gc)
                    g["bt"].append(g["beta"][rs, GDN_HEADS + h:GDN_HEADS + h + 1])
                    g["dec"].append(jnp.exp(jnp.where(causal, gc - gr, -jnp.inf)))
            return run

        def a_mats():
            g["kb"] = [k * bt for k, bt in zip(g["k"], g["bt"])]
            g["k16"] = [k.astype(BF16) for k in g["k"]]
            g["a"] = [jnp.where(strict, _dot_nt(kb.astype(BF16), k16) * dec, 0.0)
                      for kb, k16, dec in zip(g["kb"], g["k16"], g["dec"])]

        n_tiles = C_QKV // LANES
        conv_pieces = [conv(list(range(i, i + 2))) for i in range(0, n_tiles, 2)]
        return g, conv_pieces + [gates] + [chunk(j) for j in range(chunks_per_batch)] + [a_mats]

    def inverse_rounds(g):
        def start():
            g["t"] = [eye - a for a in g["a"]]
            g["pw"] = [_split_bf16(a) for a in g["a"]]

        def square():
            g["pw"] = [_split_bf16(_dot_split(p, p)) for p in g["pw"]]

        def update():
            g["t"] = [t + _dot_split(_split_bf16(t), p) for t, p in zip(g["t"], g["pw"])]

        return [start] + [square, update] * (CHUNK.bit_length() - 2)

    groups = []
    pending = []
    for b in range(bsz):
        g, pieces = prep_pieces(b)
        groups.append(g)
        n_a, n_b = len(pending), len(pieces)
        ia = ib = 0
        while ia < n_a or ib < n_b:
            if ib >= n_b or (ia < n_a and ia * n_b <= ib * n_a):
                pending[ia]()
                ia += 1
            else:
                pieces[ib]()
                ib += 1
        pending = inverse_rounds(g)
    for fn in pending:
        fn()

    uws, qk16s, qd16s, kd16s, g_tots = [], [], [], [], []
    for g in groups:
        e_gs = [jnp.exp(gc) for gc in g["gc"]]
        uws.append([_dot(t.astype(BF16), jnp.concatenate([v * bt, kb * eg], axis=1).astype(BF16))
                    for t, v, bt, kb, eg in zip(g["t"], g["v"], g["bt"], g["kb"], e_gs)])
        qk16s.append([(_dot_nt(q.astype(BF16), k16) * dec).astype(BF16)
                      for q, k16, dec in zip(g["q"], g["k16"], g["dec"])])
        qd16s.append([(q * eg).astype(BF16) for q, eg in zip(g["q"], e_gs)])
        kd16s.append([(k * jnp.exp(gc[c - 1:c, :] - gc)).astype(BF16) for k, gc in zip(g["k"], g["gc"])])
        g_tots.append([jnp.exp(gc[c - 1:c, :]) for gc in g["gc"]])

    pairs = [(b, h) for b in range(bsz) for h in range(GDN_HEADS)]
    states = [s_ref[b * GDN_HEADS + h] for b, h in pairs]
    for j in range(chunks_per_batch):
        rs = slice(j * c, (j + 1) * c)
        pick = lambda per_group: [per_group[b][chains.index((j, h))] for b, h in pairs]
        uw_j, qk_j, qd_j, kd_j, gt_j = (pick(x) for x in (uws, qk16s, qd16s, kd16s, g_tots))
        s16s = [s.astype(BF16) for s in states]
        v16s = [(uw[:, :GDN_D] - _dot(uw[:, GDN_D:].astype(BF16), s16)).astype(BF16)
                for uw, s16 in zip(uw_j, s16s)]
        states = [s * gt + _dot_tn(kd, v16) for s, gt, kd, v16 in zip(states, gt_j, kd_j, v16s)]
        os_ = [_dot(qd, s16) + _dot(qk, v16) for qd, qk, s16, v16 in zip(qd_j, qk_j, s16s, v16s)]
        for (b, h), o in zip(pairs, os_):
            on = o * lax.rsqrt(jnp.mean(o * o, axis=-1, keepdims=True) + EPS) * nw
            zh = z_ref[b, rs, h * GDN_D:(h + 1) * GDN_D]
            o_ref[b, rs, h * GDN_D:(h + 1) * GDN_D] = (on * (zh * _sigmoid(zh))).astype(o_ref.dtype)
    for p, s in enumerate(states):
        s_ref[p] = s


def _gdn(qkv, z, ab, conv_w8, alog_row, dtb_row, norm_w, rows):
    bsz, seq, _ = qkv.shape
    blk = lambda i: (0, i, 0)
    const = lambda i: (0, 0)
    return pl.pallas_call(
        _gdn_kernel,
        grid=(seq // rows,),
        in_specs=[pl.BlockSpec((bsz, rows, C_QKV), blk),
                  pl.BlockSpec((bsz, rows, C_Z), blk),
                  pl.BlockSpec((bsz, rows, LANES), blk),
                  pl.BlockSpec((8, C_QKV), const),
                  pl.BlockSpec((1, LANES), const),
                  pl.BlockSpec((1, LANES), const),
                  pl.BlockSpec((1, GDN_D), const)],
        out_specs=pl.BlockSpec((bsz, rows, C_Z), blk),
        out_shape=jax.ShapeDtypeStruct((bsz, seq, C_Z), BF16),
        scratch_shapes=[pltpu.VMEM((bsz, 8 + rows, C_QKV), F32),
                        pltpu.VMEM((bsz * GDN_HEADS, GDN_D, GDN_D), F32)],
        compiler_params=pltpu.CompilerParams(dimension_semantics=("arbitrary",),
                                             vmem_limit_bytes=VMEM_LIMIT),
        name="gdn",
    )(qkv, z, ab, conv_w8, alog_row, dtb_row, norm_w)


D_QK_PAD = LANES
C_HEADS = MLA_HEADS * D_QK_PAD
C_V = MLA_HEADS * MLA_V
V_AUG = MLA_V + 16
C_VAUG = MLA_HEADS * V_AUG
Q_SCALE = (MLA_NOPE + MLA_ROPE) ** -0.5 * math.log2(math.e)


def _mla_proj_kernel(cq_ref, ckv_ref, kr_ref, posr_ref, qnw_ref, kvnw_ref, wqt_ref,
                     wk_ref, wvt_ref, freqc_ref, qt_ref, k_ref, vt_ref):
    cq = cq_ref[...]
    cqn = (cq * lax.rsqrt(jnp.mean(cq * cq, axis=-1, keepdims=True) + EPS) * qnw_ref[...]).astype(BF16)
    ckv = ckv_ref[...]
    ckvn = (ckv * lax.rsqrt(jnp.mean(ckv * ckv, axis=-1, keepdims=True) + EPS)
            * kvnw_ref[...]).astype(BF16)
    tm = posr_ref.shape[-1]
    ang = freqc_ref[...] * posr_ref[...]
    c16 = jnp.cos(ang)
    s16 = jnp.sin(ang)
    pad = LANES - MLA_NOPE - MLA_ROPE
    cs_t = jnp.concatenate([jnp.ones((MLA_NOPE, tm), F32), c16, c16, jnp.ones((pad, tm), F32)], axis=0)
    sn_t = jnp.concatenate([jnp.zeros((MLA_NOPE, tm), F32), s16, s16, jnp.zeros((pad, tm), F32)], axis=0)
    cs = cs_t.T
    sn = sn_t.T
    q_t = _dot_nt(wqt_ref[...], cqn)
    kn = _dot(ckvn, wk_ref[...])
    half = MLA_ROPE // 2
    r0, r1, r2 = MLA_NOPE, MLA_NOPE + half, MLA_NOPE + MLA_ROPE
    lane = lax.broadcasted_iota(jnp.int32, (tm, LANES), 1)
    kr = jnp.where(lane >= r0, jnp.where(lane < r2, kr_ref[...], 0.0), 0.0)
    kr_rot = jnp.where(lane < r1, -pltpu.roll(kr, LANES - half, axis=1), pltpu.roll(kr, half, axis=1))
    kpe = kr * cs + kr_rot * sn
    for h in range(MLA_HEADS):
        sl = slice(h * LANES, (h + 1) * LANES)
        q = q_t[sl, :]
        rot = jnp.concatenate([jnp.zeros((r0, tm), F32), -q[r1:r2], q[r0:r1], jnp.zeros((pad, tm), F32)], axis=0)
        qt_ref[sl, :] = ((q * cs_t + rot * sn_t) * Q_SCALE).astype(BF16)
        k_ref[:, sl] = (kn[:, sl] + kpe).astype(BF16)
    v_t = _dot_nt(wvt_ref[...], ckvn).astype(BF16)
    ones = jnp.ones((V_AUG - MLA_V, v_t.shape[1]), BF16)
    for h in range(MLA_HEADS):
        vt_ref[0, h * V_AUG:h * V_AUG + MLA_V, :] = v_t[h * MLA_V:(h + 1) * MLA_V, :]
        vt_ref[0, h * V_AUG + MLA_V:(h + 1) * V_AUG, :] = ones


def _mla_proj(cq, ckv, abk, posr, qnw, kvnw, wqt, wk, wvt, freqc, tm):
    t = cq.shape[0]
    row = lambda i: (i, 0)
    col = lambda i: (0, i)
    const = lambda i: (0, 0)
    return pl.pallas_call(
        _mla_proj_kernel,
        grid=(t // tm,),
        in_specs=[pl.BlockSpec((tm, MLA_Q_LORA), row),
                  pl.BlockSpec((tm, MLA_KV_LORA), row),
                  pl.BlockSpec((tm, LANES), row),
                  pl.BlockSpec((1, tm), col),
                  pl.BlockSpec((1, MLA_Q_LORA), const),
                  pl.BlockSpec((1, MLA_KV_LORA), const),
                  pl.BlockSpec((C_HEADS, MLA_Q_LORA), const),
                  pl.BlockSpec((MLA_KV_LORA, C_HEADS), const),
                  pl.BlockSpec((C_V, MLA_KV_LORA), const),
                  pl.BlockSpec((MLA_ROPE // 2, 1), const)],
        out_specs=[pl.BlockSpec((C_HEADS, tm), col),
                   pl.BlockSpec((tm, C_HEADS), row),
                   pl.BlockSpec((1, C_VAUG, tm), lambda i: (i, 0, 0))],
        out_shape=[jax.ShapeDtypeStruct((C_HEADS, t), BF16),
                   jax.ShapeDtypeStruct((t, C_HEADS), BF16),
                   jax.ShapeDtypeStruct((t // tm, C_VAUG, tm), BF16)],
        compiler_params=pltpu.CompilerParams(dimension_semantics=("parallel",),
                                             vmem_limit_bytes=VMEM_LIMIT),
        name="mla_proj",
    )(cq, ckv, abk, posr, qnw, kvnw, wqt, wk, wvt, freqc)


def _attn_kernel(qt_ref, k_ref, vt_ref, nw_ref, ot_ref, m_ref, acc_ref, *, tq):
    i = pl.program_id(1)
    m_ref[...] = jnp.full(m_ref.shape, NEG_BIG, F32)
    acc_ref[...] = jnp.zeros(acc_ref.shape, F32)
    def tile(j, k_lo, n_keys, q_lo, masked):
        rows = pl.ds(pl.multiple_of(j * tq + k_lo, n_keys), n_keys)
        qs = slice(q_lo, tq)
        if masked:
            key_chunk = (k_lo + lax.broadcasted_iota(jnp.int32, (n_keys, tq - q_lo), 0)) // CHUNK
            qry_chunk = (q_lo + lax.broadcasted_iota(jnp.int32, (n_keys, tq - q_lo), 1)) // CHUNK
            mask = key_chunk <= qry_chunk
        scores = []
        for h in range(MLA_HEADS):
            hs = slice(h * LANES, (h + 1) * LANES)
            s_t = _dot(k_ref[rows, hs], qt_ref[hs, qs])
            scores.append(jnp.where(mask, s_t, NEG_BIG) if masked else s_t)
        probs, alphas = [], []
        for h in range(MLA_HEADS):
            m_old = m_ref[h, :, qs]
            m_new = jnp.maximum(m_old, jnp.max(scores[h], axis=0, keepdims=True))
            m_ref[h, :, qs] = m_new
            probs.append(jnp.exp2(scores[h] - m_new[0:1, :]).astype(BF16))
            alphas.append(jnp.exp2(m_old[0:1, :] - m_new[0:1, :]))
        for h in range(MLA_HEADS):
            v_aug = vt_ref[j, h * V_AUG:(h + 1) * V_AUG, k_lo:k_lo + n_keys]
            acc_ref[h, :, qs] = alphas[h] * acc_ref[h, :, qs] + _dot(v_aug, probs[h])

    def full_tile(j, carry):
        tile(j, 0, tq, 0, False)
        return carry

    lax.fori_loop(0, i, full_tile, 0)
    half = tq // 2
    tile(i, 0, half, 0, True)
    tile(i, half, half, half, True)

    outs = []
    for h in range(MLA_HEADS):
        acc = acc_ref[h]
        outs.append(acc[:MLA_V, :] / acc[MLA_V:MLA_V + 1, :])
    ssq = outs[0] * outs[0]
    for o in outs[1:]:
        ssq = ssq + o * o
    inv = lax.rsqrt(jnp.sum(ssq, axis=0, keepdims=True) / C_V + EPS)
    for h in range(MLA_HEADS):
        vs = slice(h * MLA_V, (h + 1) * MLA_V)
        ot_ref[vs, :] = (outs[h] * inv * nw_ref[vs, :]).astype(ot_ref.dtype)


def _attention(qt, k, vt, norm_w_col, bsz, seq, tq):
    nq = seq // tq
    return pl.pallas_call(
        functools.partial(_attn_kernel, tq=tq),
        grid=(bsz, nq),
        in_specs=[pl.BlockSpec((C_HEADS, tq), lambda b, i: (0, b * nq + i)),
                  pl.BlockSpec((seq, C_HEADS), lambda b, i: (b, 0)),
                  pl.BlockSpec((nq, C_VAUG, tq), lambda b, i: (b, 0, 0)),
                  pl.BlockSpec((C_V, 1), lambda b, i: (0, 0))],
        out_specs=pl.BlockSpec((C_V, tq), lambda b, i: (0, b * nq + i)),
        out_shape=jax.ShapeDtypeStruct((C_V, bsz * seq), BF16),
        scratch_shapes=[pltpu.VMEM((MLA_HEADS, 8, tq), F32),
                        pltpu.VMEM((MLA_HEADS, V_AUG, tq), F32)],
        compiler_params=pltpu.CompilerParams(dimension_semantics=("parallel", "parallel"),
                                             vmem_limit_bytes=VMEM_LIMIT),
        name="attention",
    )(qt, k, vt, norm_w_col)


def _router_kernel(x_ref, oa_ref, obt_ref, wo_ref, nw_ref, rw_ref, rb_ref,
                   x1_ref, h_ref, eidx_ref, gate_ref, rank_ref, cnt_ref, run_ref, *, tm):
    @pl.when(pl.program_id(0) == 0)
    def _():
        run_ref[...] = jnp.zeros_like(run_ref)

    half = oa_ref.shape[-1]
    y = (_dot(oa_ref[...].astype(BF16), wo_ref[0:half, :])
         + _dot_tn(obt_ref[...].astype(BF16), wo_ref[half:, :]))
    x1 = x_ref[...] + y
    x1_ref[...] = x1
    hn = x1 * lax.rsqrt(jnp.mean(x1 * x1, axis=-1, keepdims=True) + EPS) * nw_ref[...]
    d_half = hn.shape[-1] // 2
    _store_packed_rows(h_ref, _pack_bf16_pair(hn[:, :d_half], hn[:, d_half:]))
    lg = _dot_split_nt(_split_bf16(rw_ref[...]), _split_bf16(hn)) + rb_ref[...]
    expert = lax.broadcasted_iota(jnp.int32, lg.shape, 0)
    sels, tops, idxs = [], [], []
    for _ in range(TOP_K):
        m = jnp.max(lg, axis=0, keepdims=True)
        idx = jnp.min(jnp.where(lg == m, expert, N_EXPERTS), axis=0, keepdims=True)
        sel = expert == idx
        lg = jnp.where(sel, -jnp.inf, lg)
        sels.append(sel)
        tops.append(m)
        idxs.append(idx)
    exps = [jnp.exp(tv - tops[0]) for tv in tops]
    den = functools.reduce(lambda a, b: a + b, exps)
    multi = sels[0].astype(F32)
    for sel in sels[1:]:
        multi = multi + sel.astype(F32)
    ri = lax.broadcasted_iota(jnp.int32, (tm, tm), 0)
    ci = lax.broadcasted_iota(jnp.int32, (tm, tm), 1)
    before = _dot(multi.astype(BF16), (ri < ci).astype(BF16)) + run_ref[...]
    ranks = [jnp.sum(jnp.where(sel, before, 0.0), axis=0, keepdims=True).astype(jnp.int32) for sel in sels]
    run = run_ref[...] + jnp.sum(multi, axis=1, keepdims=True)
    run_ref[...] = run
    cnt_ref[...] = jnp.broadcast_to(run, cnt_ref.shape)
    pad_i = jnp.zeros((8 - TOP_K, tm), jnp.int32)
    eidx_ref[...] = jnp.concatenate(idxs + [pad_i], axis=0)
    rank_ref[...] = jnp.concatenate(ranks + [pad_i], axis=0)
    gates_t = jnp.concatenate([e / den for e in exps] + [jnp.zeros((LANES - TOP_K, tm), F32)], axis=0)
    gate_ref[...] = gates_t.T


def _router(x2, oa, obt, w_out, norm_w, rw, rb, tm):
    t, d = x2.shape
    half = oa.shape[-1]
    s_words = d // 2 // LANES
    row = lambda i: (i, 0)
    col = lambda i: (0, i)
    const = lambda i: (0, 0)
    return pl.pallas_call(
        functools.partial(_router_kernel, tm=tm),
        grid=(t // tm,),
        in_specs=[pl.BlockSpec((tm, d), row),
                  pl.BlockSpec((tm, half), row),
                  pl.BlockSpec((half, tm), lambda i: (0, i)),
                  pl.BlockSpec((2 * half, d), const),
                  pl.BlockSpec((1, d), const),
                  pl.BlockSpec((N_EXPERTS, d), const),
                  pl.BlockSpec((N_EXPERTS, 1), const)],
        out_specs=[pl.BlockSpec((tm, d), row), pl.BlockSpec((tm * s_words, LANES), row),
                   pl.BlockSpec((8, tm), col), pl.BlockSpec((tm, LANES), row),
                   pl.BlockSpec((8, tm), col), pl.BlockSpec((N_EXPERTS, LANES), const)],
        out_shape=[jax.ShapeDtypeStruct((t, d), F32), jax.ShapeDtypeStruct((t * s_words, LANES), jnp.int32),
                   jax.ShapeDtypeStruct((8, t), jnp.int32), jax.ShapeDtypeStruct((t, LANES), F32),
                   jax.ShapeDtypeStruct((8, t), jnp.int32), jax.ShapeDtypeStruct((N_EXPERTS, LANES), F32)],
        scratch_shapes=[pltpu.VMEM((N_EXPERTS, 1), F32)],
        compiler_params=pltpu.CompilerParams(dimension_semantics=("arbitrary",),
                                             vmem_limit_bytes=VMEM_LIMIT),
        name="router",
    )(x2, oa, obt, w_out, norm_w, rw, rb)


SC_ROWS = 64


def _sc_mesh():
    return plsc.VectorSubcoreMesh(core_axis_name="c", subcore_axis_name="s")


def _dispatch(dest_chunks, h3, n_pad):
    t, s, _ = h3.shape
    mesh = _sc_mesh()
    n_workers = mesh.num_cores * mesh.num_subcores
    per_worker = t // n_workers
    n_chunks = per_worker // SC_ROWS
    assert per_worker % SC_ROWS == 0 and n_chunks % 2 == 0

    @functools.partial(
        pl.kernel, mesh=mesh, out_type=jax.ShapeDtypeStruct((n_pad, s, LANES), h3.dtype),
        scratch_types=[pltpu.VMEM((2, TOP_K * SC_ROWS), jnp.int32), pltpu.VMEM((2, SC_ROWS, s, LANES), h3.dtype),
                       pltpu.SemaphoreType.DMA((2,)), pltpu.SemaphoreType.DMA((2,)), pltpu.SemaphoreType.DMA],
        name="sc_dispatch")
    def scatter(h_hbm, dest_hbm, out_hbm, idx_v, rows_v, rsem, isem, ssem):
        base = (lax.axis_index("s") * mesh.num_cores + lax.axis_index("c")) * per_worker

        def loads(chunk, slot):
            off = pl.multiple_of(base + chunk * SC_ROWS, SC_ROWS)
            return (pltpu.make_async_copy(h_hbm.at[pl.ds(off, SC_ROWS)], rows_v.at[slot], rsem.at[slot]),
                    pltpu.make_async_copy(dest_hbm.at[pl.ds(off * TOP_K, SC_ROWS * TOP_K)], idx_v.at[slot],
                                          isem.at[slot]))

        for cp in loads(0, 0):
            cp.start()

        @pl.loop(0, n_chunks // 2)
        def _(pair):
            for slot in (0, 1):
                chunk = 2 * pair + slot
                for cp in loads(chunk, slot):
                    cp.wait()

                @pl.when(chunk + 1 < n_chunks)
                def _():
                    for cp in loads(chunk + 1, 1 - slot):
                        cp.start()

                copies = [pltpu.async_copy(rows_v.at[slot],
                                           out_hbm.at[idx_v.at[slot, pl.ds(kk * SC_ROWS, SC_ROWS)]], ssem)
                          for kk in range(TOP_K)]
                for cp in copies:
                    cp.wait()

    return scatter(h3, dest_chunks)


def _expert_kernel(cnt_ref, blk0_ref, nblk_ref, total_ref, x_hbm, w1_ref, b1g_ref, b1l_ref, w2_ref, b2_ref,
                   y_hbm, w1s_ref, w2s_ref, xbuf, ybuf, xsem, ysem, *, bm):
    e = pl.program_id(0)
    n_blocks = nblk_ref[e]
    first = blk0_ref[e]
    total = total_ref[0]
    rows_per_block = xbuf.shape[1]
    s_words = rows_per_block // bm
    n_tiles = w1s_ref.shape[-1] // MXU_COLS
    half = MXU_COLS // 2

    def x_copy(g, slot):
        return pltpu.make_async_copy(x_hbm.at[pl.ds(pl.multiple_of(g * rows_per_block, rows_per_block),
                                                    rows_per_block)], xbuf.at[slot], xsem.at[slot])

    def y_copy(g, slot):
        return pltpu.make_async_copy(ybuf.at[slot], y_hbm.at[pl.ds(pl.multiple_of(g * rows_per_block,
                                                                                   rows_per_block),
                                                                   rows_per_block)], ysem.at[slot])

    @pl.when(e == 0)
    def _():
        x_copy(0, 0).start()

    @pl.when(n_blocks > 0)
    def _():
        r = lax.broadcasted_iota(jnp.int32, (MXU_COLS, MXU_COLS), 0)
        c = lax.broadcasted_iota(jnp.int32, (MXU_COLS, MXU_COLS), 1)
        perm = (r == jnp.where(c < half, 2 * c, 2 * (c - half) + 1)).astype(BF16)
        for tix in range(n_tiles):
            cols = slice(tix * MXU_COLS, (tix + 1) * MXU_COLS)
            w1s_ref[:, cols] = _dot(w1_ref[0, :, cols].astype(BF16), perm).astype(BF16)
        w2s_ref[...] = w2_ref[0].astype(BF16)

        def block(j, carry):
            g = first + j
            slot = lax.rem(g, 2)
            x_copy(g, slot).wait()

            @pl.when(g + 1 < total)
            def _():
                x_copy(g + 1, 1 - slot).start()

            @pl.when(g >= 2)
            def _():
                y_copy(g - 2, slot).wait()

            n_rows = cnt_ref[e] - j * bm

            def ffn(m):
                words = _load_packed_rows(xbuf, (slot,), m, s_words)
                row = lax.broadcasted_iota(jnp.int32, words.shape, 0)
                x_lo, x_hi = _unpack_bf16_pair(jnp.where(row < n_rows, words, 0))
                x = jnp.concatenate([x_lo.astype(BF16), x_hi.astype(BF16)], axis=1)
                acts = []
                for tix in range(n_tiles):
                    hp = _dot(x, w1s_ref[:, tix * MXU_COLS:(tix + 1) * MXU_COLS])
                    feat = slice(tix * half, (tix + 1) * half)
                    gl = jnp.minimum(hp[:, :half] + b1g_ref[0][:, feat], SWIGLU_LIMIT)
                    lin = jnp.clip(hp[:, half:] + b1l_ref[0][:, feat], -SWIGLU_LIMIT, SWIGLU_LIMIT)
                    acts.append((gl * _sigmoid(SWIGLU_ALPHA * gl) * (lin + 1.0)).astype(BF16))
                y = _dot(jnp.concatenate(acts, axis=1), w2s_ref[...]) + b2_ref[0]
                d_half = y.shape[-1] // 2
                _store_packed_rows(ybuf.at[slot], _pack_bf16_pair(y[:, :d_half], y[:, d_half:]))

            @pl.when(n_rows > bm // 2)
            def _():
                ffn(bm)

            @pl.when(n_rows <= bm // 2)
            def _():
                ffn(bm // 2)

            y_copy(g, slot).start()
            return carry

        lax.fori_loop(0, n_blocks, block, 0)

    @pl.when(e == pl.num_programs(0) - 1)
    def _():
        @pl.when(total >= 2)
        def _():
            y_copy(total - 2, lax.rem(total, 2)).wait()

        y_copy(total - 1, lax.rem(total - 1, 2)).wait()


def _experts(tables, xs, w1, b1g, b1l, w2, b2, bm, d):
    s_words = d // 2 // LANES
    dff2 = w1.shape[-1]
    dff = dff2 // 2
    emap = lambda e, *_: (e, 0, 0)
    grid_spec = pltpu.PrefetchScalarGridSpec(
        num_scalar_prefetch=len(tables),
        grid=(N_EXPERTS,),
        in_specs=[pl.BlockSpec(memory_space=pl.ANY),
                  pl.BlockSpec((1, d, dff2), emap),
                  pl.BlockSpec((1, 1, dff), emap),
                  pl.BlockSpec((1, 1, dff), emap),
                  pl.BlockSpec((1, dff, d), emap),
                  pl.BlockSpec((1, 1, d), emap)],
        out_specs=pl.BlockSpec(memory_space=pl.ANY),
        scratch_shapes=[pltpu.VMEM((d, dff2), BF16), pltpu.VMEM((dff, d), BF16),
                        pltpu.VMEM((2, bm * s_words, LANES), jnp.int32),
                        pltpu.VMEM((2, bm * s_words, LANES), jnp.int32),
                        pltpu.SemaphoreType.DMA((2,)), pltpu.SemaphoreType.DMA((2,))],
    )
    return pl.pallas_call(
        functools.partial(_expert_kernel, bm=bm),
        grid_spec=grid_spec,
        out_shape=jax.ShapeDtypeStruct(xs.shape, jnp.int32),
        compiler_params=pltpu.CompilerParams(dimension_semantics=("arbitrary",),
                                             vmem_limit_bytes=VMEM_LIMIT),
        name="experts",
    )(*tables, xs, w1, b1g, b1l, w2, b2)


def _gather_rows(dest_flat, y3):
    n_out = dest_flat.shape[0]
    _, s, _ = y3.shape
    mesh = _sc_mesh()
    n_workers = mesh.num_cores * mesh.num_subcores
    per_worker = n_out // n_workers
    n_chunks = per_worker // SC_ROWS
    assert per_worker % SC_ROWS == 0 and n_chunks % 2 == 0

    @functools.partial(
        pl.kernel, mesh=mesh, out_type=jax.ShapeDtypeStruct((n_out, s, LANES), y3.dtype),
        scratch_types=[pltpu.VMEM((2, SC_ROWS), jnp.int32), pltpu.VMEM((2, SC_ROWS, s, LANES), y3.dtype),
                       pltpu.SemaphoreType.DMA((2,)), pltpu.SemaphoreType.DMA((2,))],
        name="sc_gather")
    def gather(y_hbm, dest_hbm, out_hbm, idx_v, rows_v, gsem, wsem):
        base = (lax.axis_index("s") * mesh.num_cores + lax.axis_index("c")) * per_worker

        def rows_of(chunk):
            return pl.ds(pl.multiple_of(base + chunk * SC_ROWS, SC_ROWS), SC_ROWS)

        def gather_copy(slot):
            return pltpu.make_async_copy(y_hbm.at[idx_v.at[slot]], rows_v.at[slot], gsem.at[slot])

        def write_copy(chunk, slot):
            return pltpu.make_async_copy(rows_v.at[slot], out_hbm.at[rows_of(chunk)], wsem.at[slot])

        def start_gather(chunk, slot):
            pltpu.sync_copy(dest_hbm.at[rows_of(chunk)], idx_v.at[slot])
            gather_copy(slot).start()

        start_gather(0, 0)

        @pl.loop(0, n_chunks // 2)
        def _(pair):
            for slot in (0, 1):
                chunk = 2 * pair + slot
                gather_copy(slot).wait()
                write_copy(chunk, slot).start()

                @pl.when(chunk >= 1)
                def _():
                    write_copy(chunk - 1, 1 - slot).wait()

                @pl.when(chunk + 1 < n_chunks)
                def _():
                    start_gather(chunk + 1, 1 - slot)

        write_copy(n_chunks - 1, 1).wait()

    return gather(y3, dest_flat)


def _combine_kernel(x1_ref, gate_ref, y4_ref, nw_ref, o_ref, *, tm, normalize):
    x = x1_ref[...]
    gates = gate_ref[...]
    for kk in range(TOP_K):
        lo, hi = _unpack_bf16_pair(_load_packed_rows(y4_ref, (kk,), tm, y4_ref.shape[1] // tm))
        x = x + gates[:, kk:kk + 1] * jnp.concatenate([lo, hi], axis=1)
    if normalize:
        x = x * lax.rsqrt(jnp.mean(x * x, axis=-1, keepdims=True) + EPS) * nw_ref[...]
    o_ref[...] = x


def _combine(x1, gates, y4, norm_w, tm, normalize):
    t, d = x1.shape
    s_words = d // 2 // LANES
    row = lambda i: (i, 0)
    return pl.pallas_call(
        functools.partial(_combine_kernel, tm=tm, normalize=normalize),
        grid=(t // tm,),
        in_specs=[pl.BlockSpec((tm, d), row), pl.BlockSpec((tm, LANES), row),
                  pl.BlockSpec((TOP_K, tm * s_words, LANES), lambda i: (0, i, 0)),
                  pl.BlockSpec((1, d), lambda i: (0, 0))],
        out_specs=pl.BlockSpec((tm, d), row),
        out_shape=jax.ShapeDtypeStruct((t, d), F32),
        compiler_params=pltpu.CompilerParams(dimension_semantics=("parallel",)),
        name="combine",
    )(x1, gates, y4, norm_w)


def _tiles(t, seq):
    tm = min(2 * MXU_COLS, t)
    tr = min(2 * MXU_COLS, t)
    tq = min(2 * MXU_COLS, seq)
    gdn_rows = min(4 * CHUNK, seq)
    bm = 2 * MXU_COLS
    assert t % tm == 0 and t % tr == 0 and seq % tq == 0 and seq % gdn_rows == 0 and tq % CHUNK == 0
    return tm, tr, tq, gdn_rows, bm


def _prep_mla_weights(w_uq, w_ukv):
    dq = MLA_NOPE + MLA_ROPE
    wq = w_uq.reshape(MLA_Q_LORA, MLA_HEADS, dq)
    pad = lambda w: jnp.pad(w, ((0, 0), (0, 0), (0, LANES - w.shape[-1]))).reshape(w.shape[0], C_HEADS)
    wkv = w_ukv.reshape(MLA_KV_LORA, MLA_HEADS, MLA_NOPE + MLA_V)
    wv = wkv[..., MLA_NOPE:].reshape(MLA_KV_LORA, C_V)
    return pad(wq).T.astype(BF16), pad(wkv[..., :MLA_NOPE]).astype(BF16), wv.T.astype(BF16)


def _rope_freq():
    half = MLA_ROPE // 2
    return (ROPE_THETA ** (-jnp.arange(half, dtype=F32) / half)).reshape(half, 1)


def _lane_row(v):
    return jnp.pad(v.astype(F32), (0, LANES - v.shape[0])).reshape(1, LANES)


def _routing_tables(cnt, eidx, rank, bm):
    e_ids = jnp.arange(N_EXPERTS, dtype=jnp.int32)
    counts = cnt[:, 0].astype(jnp.int32)
    padded = (counts + bm - 1) // bm * bm
    pend = jnp.sum(jnp.where(e_ids[None, :] <= e_ids[:, None], padded[None, :], 0), axis=1)
    pstart = pend - padded
    dest = jnp.sum(jnp.where(eidx[:TOP_K, :, None] == e_ids, pstart, 0), axis=-1) + rank[:TOP_K]
    tables = (counts, pstart // bm, padded // bm, pend[-1:] // bm)
    return dest.reshape(-1), tuple(t.astype(jnp.int32) for t in tables)


def kernel(x, positions, norm_mix_w, w_in, conv_w, a_log, dt_bias, gdn_norm_w, q_norm_w, w_uq, kv_norm_w,
           w_ukv, mla_out_norm_w, w_out, norm_ffn_w, router_w, router_b, w1, b1, w2, b2, norm_final_w):
    bsz, seq, d = x.shape
    t = bsz * seq
    depth = w_in.shape[0]
    tm, tr, tq, gdn_rows, bm = _tiles(t, seq)
    x2 = x.reshape(t, d)
    posf = positions.astype(F32)
    freq = _rope_freq()
    for l in range(depth):
        qkv, z, cq, ckv, abk = _inproj(x2, norm_mix_w[l].reshape(1, d), w_in[l], tm)
        conv_w8 = jnp.pad(conv_w[l], ((0, 8 - GDN_CONV), (0, 0)))
        o_a = _gdn(qkv.reshape(bsz, seq, C_QKV), z.reshape(bsz, seq, C_Z), abk.reshape(bsz, seq, LANES),
                   conv_w8, _lane_row(a_log[l]), _lane_row(dt_bias[l]), gdn_norm_w[l].reshape(1, GDN_D),
                   gdn_rows).reshape(t, C_Z)
        wqt, wk, wvt = _prep_mla_weights(w_uq[l], w_ukv[l])
        qt, k, vt = _mla_proj(cq, ckv, abk, posf.reshape(1, t),
                              q_norm_w[l].reshape(1, -1), kv_norm_w[l].reshape(1, -1),
                              wqt, wk, wvt, freq, tq)
        o_bt = _attention(qt, k, vt, mla_out_norm_w[l].reshape(-1, 1), bsz, seq, tq)
        x1, hn, eidx, gates, rank, cnt = _router(x2, o_a, o_bt, w_out[l].astype(BF16),
                                                 norm_ffn_w[l].reshape(1, d), router_w[l].T,
                                                 router_b[l].reshape(N_EXPERTS, 1), tr)
        n_pad = t * TOP_K + N_EXPERTS * bm
        dest_flat, tables = _routing_tables(cnt, eidx, rank, bm)
        s_words = d // 2 // LANES
        dest_chunks = dest_flat.reshape(TOP_K, t // SC_ROWS, SC_ROWS).transpose(1, 0, 2).reshape(-1)
        buf = _dispatch(dest_chunks, hn.reshape(t, s_words, LANES), n_pad)
        yb = _experts(tables, buf.reshape(n_pad * s_words, LANES), w1[l],
                      b1[l][:, None, 0::2], b1[l][:, None, 1::2], w2[l], b2[l][:, None, :], bm, d)
        y4 = _gather_rows(dest_flat, yb.reshape(n_pad, s_words, LANES))
        x2 = _combine(x1, gates, y4.reshape(TOP_K, t * s_words, LANES), norm_final_w.reshape(1, d), tr,
                      normalize=(l == depth - 1))
    return x2.reshape(bsz, seq, d)
```
